```python
import math
import jax, jax.numpy as jnp
from jax import lax
import numpy as np

D_MODEL = 1024
BATCH = 8
SEQ = 8192
DEPTH = 4

D_MIX = D_MODEL
DN_HEADS = 4
DN_HEAD_DIM = 128
DN_WIDTH = DN_HEADS * DN_HEAD_DIM
DN_CONV = 4
CHUNK = 64
SC_WIDTH = D_MIX - DN_WIDTH
SC_GROUPS = 4
SC_GROUP_DIM = SC_WIDTH // SC_GROUPS
SC_CONV = 3
D_FF = ((8 * D_MODEL + 3 * 256 - 1) // (3 * 256)) * 256
W_IN_COLS = 4 * DN_WIDTH + 2 * DN_HEADS + 3 * SC_WIDTH
EPS = 1e-6

kernel_name = 'hybrid_gdn_shortconv_swiglu'


def rms_norm(x, gain):
    xf = x.astype(jnp.float32)
    y = xf * lax.rsqrt(jnp.mean(xf * xf, axis=-1, keepdims=True) + EPS)
    return (y * gain.astype(jnp.float32)).astype(x.dtype)


def l2_normalize(x):
    return x * lax.rsqrt(jnp.sum(x * x, axis=-1, keepdims=True) + EPS)


def causal_depthwise_conv(x, w):
    K = w.shape[0]
    T = x.shape[1]
    xp = jnp.pad(x, ((0, 0), (K - 1, 0), (0, 0)))
    y = xp[:, 0:T, :] * w[0]
    for j in range(1, K):
        y = y + xp[:, j:j + T, :] * w[j]
    return y


def chunk_gated_delta_rule(q, k, v, g, beta):
    Bsz, T, H, DK = q.shape
    DV = v.shape[-1]
    C = CHUNK
    N = T // C
    q = q * (DK ** -0.5)

    def to_chunks(t):
        return t.reshape(Bsz, N, C, H, t.shape[-1]).transpose(0, 3, 1, 2, 4)

    q, k, v = to_chunks(q), to_chunks(k), to_chunks(v)
    g = jnp.cumsum(g.reshape(Bsz, N, C, H).transpose(0, 3, 1, 2), axis=-1)
    beta = beta.reshape(Bsz, N, C, H).transpose(0, 3, 1, 2)

    causal = jnp.tril(jnp.ones((C, C), dtype=bool))
    strict = jnp.tril(jnp.ones((C, C), dtype=bool), -1)
    decay = jnp.exp(jnp.where(causal, g[..., :, None] - g[..., None, :], -jnp.inf))

    k_beta = k * beta[..., None]
    v_beta = v * beta[..., None]
    lower = jnp.where(strict, jnp.einsum('bhncd,bhnmd->bhncm', k_beta, k) * decay, 0.0)
    a_mat = lower + jnp.eye(C, dtype=jnp.float32)
    rhs = jnp.concatenate([v_beta, k_beta * jnp.exp(g)[..., None]], axis=-1)
    sol = lax.linalg.triangular_solve(a_mat, rhs, left_side=True, lower=True, unit_diagonal=True)
    u, w = sol[..., :DV], sol[..., DV:]

    qk = jnp.einsum('bhncd,bhnmd->bhncm', q, k) * decay
    q_dec = q * jnp.exp(g)[..., None]
    k_dec = k * jnp.exp(g[..., -1:] - g)[..., None]
    g_last = jnp.exp(g[..., -1])

    def step(S, xs):
        qk_i, q_dec_i, k_dec_i, u_i, w_i, gl_i = xs
        v_new = u_i - jnp.einsum('bhck,bhkv->bhcv', w_i, S)
        o_i = jnp.einsum('bhck,bhkv->bhcv', q_dec_i, S) + jnp.einsum('bhcm,bhmv->bhcv', qk_i, v_new)
        S = S * gl_i[..., None, None] + jnp.einsum('bhck,bhcv->bhkv', k_dec_i, v_new)
        return S, o_i

    xs = (jnp.moveaxis(qk, 2, 0), jnp.moveaxis(q_dec, 2, 0), jnp.moveaxis(k_dec, 2, 0),
          jnp.moveaxis(u, 2, 0), jnp.moveaxis(w, 2, 0), jnp.moveaxis(g_last, 2, 0))
    S0 = jnp.zeros((Bsz, H, DK, DV), dtype=jnp.float32)
    _, o = lax.scan(step, S0, xs)
    return o.transpose(1, 0, 3, 2, 4).reshape(Bsz, T, H, DV)


def hybrid_layer(x, norm1_g, w_in, dn_conv_w, dn_a_log, dn_dt_bias, dn_norm_g,
                 sc_conv_w, sc_norm_g, w_out, norm2_g, ffn_w_gate, ffn_w_up, ffn_w_down):
    Bsz, T, _ = x.shape
    h = rms_norm(x, norm1_g)
    proj = h @ w_in
    o1 = 3 * DN_WIDTH
    o2 = o1 + DN_WIDTH
    o3 = o2 + DN_HEADS
    o4 = o3 + DN_HEADS
    qkv, z, b_in, a_in, sc_in = proj[..., :o1], proj[..., o1:o2], proj[..., o2:o3], proj[..., o3:o4], proj[..., o4:]

    qkv = jax.nn.silu(causal_depthwise_conv(qkv, dn_conv_w)).astype(jnp.float32)
    q = l2_normalize(qkv[..., :DN_WIDTH].reshape(Bsz, T, DN_HEADS, DN_HEAD_DIM))
    k = l2_normalize(qkv[..., DN_WIDTH:2 * DN_WIDTH].reshape(Bsz, T, DN_HEADS, DN_HEAD_DIM))
    v = qkv[..., 2 * DN_WIDTH:].reshape(Bsz, T, DN_HEADS, DN_HEAD_DIM)
    beta = jax.nn.sigmoid(b_in.astype(jnp.float32))
    g = -jnp.exp(dn_a_log.astype(jnp.float32)) * jax.nn.softplus(
        a_in.astype(jnp.float32) + dn_dt_bias.astype(jnp.float32))
    o_dn = chunk_gated_delta_rule(q, k, v, g, beta)
    zf = z.astype(jnp.float32).reshape(Bsz, T, DN_HEADS, DN_HEAD_DIM)
    o_dn = (o_dn * lax.rsqrt(jnp.mean(o_dn * o_dn, axis=-1, keepdims=True) + EPS)
            * dn_norm_g.astype(jnp.float32) * jax.nn.silu(zf))
    o_dn = o_dn.reshape(Bsz, T, DN_WIDTH).astype(x.dtype)

    gate_b, gate_c, hv = sc_in[..., :SC_WIDTH], sc_in[..., SC_WIDTH:2 * SC_WIDTH], sc_in[..., 2 * SC_WIDTH:]
    y = gate_b * causal_depthwise_conv(gate_c * hv, sc_conv_w)
    yf = y.astype(jnp.float32).reshape(Bsz, T, SC_GROUPS, SC_GROUP_DIM)
    yf = yf * lax.rsqrt(jnp.mean(yf * yf, axis=-1, keepdims=True) + EPS)
    o_sc = (yf * sc_norm_g.astype(jnp.float32).reshape(SC_GROUPS, SC_GROUP_DIM)).reshape(Bsz, T, SC_WIDTH).astype(x.dtype)

    x = x + jnp.concatenate([o_dn, o_sc], axis=-1) @ w_out

    h2 = rms_norm(x, norm2_g)
    x = x + (jax.nn.silu(h2 @ ffn_w_gate) * (h2 @ ffn_w_up)) @ ffn_w_down
    return x


def _fwd_setup_inputs(seed: int = 0) -> dict:
    key = jax.random.key(seed)
    ks = jax.random.split(key, 16)
    f32 = jnp.float32

    def nrm(k, shape, scale):
        return jax.random.normal(k, shape, f32) * scale

    def gain(k, shape):
        return 1.0 + 0.02 * jax.random.normal(k, shape, f32)

    x = nrm(ks[0], (BATCH, SEQ, D_MODEL), 1.0)
    norm1_g = gain(ks[1], (DEPTH, D_MODEL))
    w_in = nrm(ks[2], (DEPTH, D_MODEL, W_IN_COLS), D_MODEL ** -0.5)
    dn_conv_w = nrm(ks[3], (DEPTH, DN_CONV, 3 * DN_WIDTH), DN_CONV ** -0.5)
    dn_a_log = jnp.log(jax.random.uniform(ks[4], (DEPTH, DN_HEADS), f32, 1.0, 16.0))
    dt = jnp.exp(jax.random.uniform(ks[5], (DEPTH, DN_HEADS), f32, math.log(1e-3), math.log(1e-1)))
    dn_dt_bias = dt + jnp.log(-jnp.expm1(-dt))
    dn_norm_g = gain(ks[6], (DEPTH, DN_HEAD_DIM))
    sc_conv_w = nrm(ks[7], (DEPTH, SC_CONV, SC_WIDTH), SC_CONV ** -0.5)
    sc_norm_g = gain(ks[8], (DEPTH, SC_WIDTH))
    w_out = nrm(ks[9], (DEPTH, D_MIX, D_MODEL), D_MIX ** -0.5)
    norm2_g = gain(ks[10], (DEPTH, D_MODEL))
    ffn_w_gate = nrm(ks[11], (DEPTH, D_MODEL, D_FF), D_MODEL ** -0.5)
    ffn_w_up = nrm(ks[12], (DEPTH, D_MODEL, D_FF), D_MODEL ** -0.5)
    ffn_w_down = nrm(ks[13], (DEPTH, D_FF, D_MODEL), D_FF ** -0.5)
    final_norm_g = gain(ks[14], (D_MODEL,))
    return {'x': x, 'norm1_g': norm1_g, 'w_in': w_in, 'dn_conv_w': dn_conv_w,
            'dn_a_log': dn_a_log, 'dn_dt_bias': dn_dt_bias, 'dn_norm_g': dn_norm_g,
            'sc_conv_w': sc_conv_w, 'sc_norm_g': sc_norm_g, 'w_out': w_out,
            'norm2_g': norm2_g, 'ffn_w_gate': ffn_w_gate, 'ffn_w_up': ffn_w_up,
            'ffn_w_down': ffn_w_down, 'final_norm_g': final_norm_g}


def _fwd_reference(x, norm1_g, w_in, dn_conv_w, dn_a_log, dn_dt_bias, dn_norm_g, sc_conv_w,
              sc_norm_g, w_out, norm2_g, ffn_w_gate, ffn_w_up, ffn_w_down, final_norm_g):
    for l in range(DEPTH):
        x = hybrid_layer(x, norm1_g[l], w_in[l], dn_conv_w[l], dn_a_log[l], dn_dt_bias[l],
                         dn_norm_g[l], sc_conv_w[l], sc_norm_g[l], w_out[l], norm2_g[l],
                         ffn_w_gate[l], ffn_w_up[l], ffn_w_down[l])
    return rms_norm(x, final_norm_g)


import jax as _jax
import jax.numpy as _jnp

TWIN_FORMAT = 'train_step'
FWD_PARAMS = ['x', 'norm1_g', 'w_in', 'dn_conv_w', 'dn_a_log', 'dn_dt_bias', 'dn_norm_g', 'sc_conv_w', 'sc_norm_g', 'w_out', 'norm2_g', 'ffn_w_gate', 'ffn_w_up', 'ffn_w_down', 'final_norm_g']
TWIN_WEIGHTS = ['norm1_g', 'w_in', 'dn_conv_w', 'dn_a_log', 'dn_dt_bias', 'dn_norm_g', 'sc_conv_w', 'sc_norm_g', 'w_out', 'norm2_g', 'ffn_w_gate', 'ffn_w_up', 'ffn_w_down', 'final_norm_g']
TWIN_DIFF_INPUT = 'x'
TWIN_INPUTS = ['x', 'norm1_g', 'w_in', 'dn_conv_w', 'dn_a_log', 'dn_dt_bias', 'dn_norm_g', 'sc_conv_w', 'sc_norm_g', 'w_out', 'norm2_g', 'ffn_w_gate', 'ffn_w_up', 'ffn_w_down', 'final_norm_g', 'loss_target', 'm_norm1_g', 'm_w_in', 'm_dn_conv_w', 'm_dn_a_log', 'm_dn_dt_bias', 'm_dn_norm_g', 'm_sc_conv_w', 'm_sc_norm_g', 'm_w_out', 'm_norm2_g', 'm_ffn_w_gate', 'm_ffn_w_up', 'm_ffn_w_down', 'm_final_norm_g', 'v_norm1_g', 'v_w_in', 'v_dn_conv_w', 'v_dn_a_log', 'v_dn_dt_bias', 'v_dn_norm_g', 'v_sc_conv_w', 'v_sc_norm_g', 'v_w_out', 'v_norm2_g', 'v_ffn_w_gate', 'v_ffn_w_up', 'v_ffn_w_down', 'v_final_norm_g']
TWIN_OUTPUTS = ['loss', 'grad_x', 'grad_norm1_g', 'grad_w_in', 'grad_dn_conv_w', 'grad_dn_a_log', 'grad_dn_dt_bias', 'grad_dn_norm_g', 'grad_sc_conv_w', 'grad_sc_norm_g', 'grad_w_out', 'grad_norm2_g', 'grad_ffn_w_gate', 'grad_ffn_w_up', 'grad_ffn_w_down', 'grad_final_norm_g', 'delta_norm1_g', 'delta_w_in', 'delta_dn_conv_w', 'delta_dn_a_log', 'delta_dn_dt_bias', 'delta_dn_norm_g', 'delta_sc_conv_w', 'delta_sc_norm_g', 'delta_w_out', 'delta_norm2_g', 'delta_ffn_w_gate', 'delta_ffn_w_up', 'delta_ffn_w_down', 'delta_final_norm_g', 'new_m_norm1_g', 'new_m_w_in', 'new_m_dn_conv_w', 'new_m_dn_a_log', 'new_m_dn_dt_bias', 'new_m_dn_norm_g', 'new_m_sc_conv_w', 'new_m_sc_norm_g', 'new_m_w_out', 'new_m_norm2_g', 'new_m_ffn_w_gate', 'new_m_ffn_w_up', 'new_m_ffn_w_down', 'new_m_final_norm_g', 'new_v_norm1_g', 'new_v_w_in', 'new_v_dn_conv_w', 'new_v_dn_a_log', 'new_v_dn_dt_bias', 'new_v_dn_norm_g', 'new_v_sc_conv_w', 'new_v_sc_norm_g', 'new_v_w_out', 'new_v_norm2_g', 'new_v_ffn_w_gate', 'new_v_ffn_w_up', 'new_v_ffn_w_down', 'new_v_final_norm_g']
TWIN_LEAF_KINDS = {'loss': 'loss', 'grad_x': 'grad_x', 'grad_norm1_g': 'grad_w', 'grad_w_in': 'grad_w', 'grad_dn_conv_w': 'grad_w', 'grad_dn_a_log': 'grad_w', 'grad_dn_dt_bias': 'grad_w', 'grad_dn_norm_g': 'grad_w', 'grad_sc_conv_w': 'grad_w', 'grad_sc_norm_g': 'grad_w', 'grad_w_out': 'grad_w', 'grad_norm2_g': 'grad_w', 'grad_ffn_w_gate': 'grad_w', 'grad_ffn_w_up': 'grad_w', 'grad_ffn_w_down': 'grad_w', 'grad_final_norm_g': 'grad_w', 'delta_norm1_g': 'delta_w', 'delta_w_in': 'delta_w', 'delta_dn_conv_w': 'delta_w', 'delta_dn_a_log': 'delta_w', 'delta_dn_dt_bias': 'delta_w', 'delta_dn_norm_g': 'delta_w', 'delta_sc_conv_w': 'delta_w', 'delta_sc_norm_g': 'delta_w', 'delta_w_out': 'delta_w', 'delta_norm2_g': 'delta_w', 'delta_ffn_w_gate': 'delta_w', 'delta_ffn_w_up': 'delta_w', 'delta_ffn_w_down': 'delta_w', 'delta_final_norm_g': 'delta_w', 'new_m_norm1_g': 'new_m', 'new_m_w_in': 'new_m', 'new_m_dn_conv_w': 'new_m', 'new_m_dn_a_log': 'new_m', 'new_m_dn_dt_bias': 'new_m', 'new_m_dn_norm_g': 'new_m', 'new_m_sc_conv_w': 'new_m', 'new_m_sc_norm_g': 'new_m', 'new_m_w_out': 'new_m', 'new_m_norm2_g': 'new_m', 'new_m_ffn_w_gate': 'new_m', 'new_m_ffn_w_up': 'new_m', 'new_m_ffn_w_down': 'new_m', 'new_m_final_norm_g': 'new_m', 'new_v_norm1_g': 'new_v', 'new_v_w_in': 'new_v', 'new_v_dn_conv_w': 'new_v', 'new_v_dn_a_log': 'new_v', 'new_v_dn_dt_bias': 'new_v', 'new_v_dn_norm_g': 'new_v', 'new_v_sc_conv_w': 'new_v', 'new_v_sc_norm_g': 'new_v', 'new_v_w_out': 'new_v', 'new_v_norm2_g': 'new_v', 'new_v_ffn_w_gate': 'new_v', 'new_v_ffn_w_up': 'new_v', 'new_v_ffn_w_down': 'new_v', 'new_v_final_norm_g': 'new_v'}


def _forward(args):
    return _fwd_reference(*[args[k] for k in FWD_PARAMS])


def _output_shape():
    out = _jax.eval_shape(lambda: _forward(_fwd_setup_inputs(0)))
    return out.shape, out.dtype

N_MICROBATCH = 1
ADAM_LR = 0.001
ADAM_B1 = 0.9
ADAM_B2 = 0.999
ADAM_EPS = 1e-08
ADAM_WD = 0.01
ADAM_STEP = 10
PER_EXAMPLE_BATCH_AXIS = {'x': 0, 'loss_target': 0}
SHARED_INPUTS = []
_WEIGHT_DTYPES = {'norm1_g': _jnp.float32, 'w_in': _jnp.float32, 'dn_conv_w': _jnp.float32, 'dn_a_log': _jnp.float32, 'dn_dt_bias': _jnp.float32, 'dn_norm_g': _jnp.float32, 'sc_conv_w': _jnp.float32, 'sc_norm_g': _jnp.float32, 'w_out': _jnp.float32, 'norm2_g': _jnp.float32, 'ffn_w_gate': _jnp.float32, 'ffn_w_up': _jnp.float32, 'ffn_w_down': _jnp.float32, 'final_norm_g': _jnp.float32}
MOMENT_SCALE = {'norm1_g': 3.646719e-01, 'w_in': 1.959251e-01, 'dn_conv_w': 1.113151e-01, 'dn_a_log': 7.628188e-01, 'dn_dt_bias': 7.326219e-01, 'dn_norm_g': 2.850396e-01, 'sc_conv_w': 2.655728e-01, 'sc_norm_g': 2.616827e-01, 'w_out': 2.078938e-01, 'norm2_g': 1.860829e-01, 'ffn_w_gate': 8.121041e-02, 'ffn_w_up': 7.866183e-02, 'ffn_w_down': 1.302540e-01, 'final_norm_g': 6.401418e+01}


def _to_microbatches(a, axis):
    t = _jnp.moveaxis(a, axis, 0)
    t = t.reshape((N_MICROBATCH, t.shape[0] // N_MICROBATCH) + t.shape[1:])
    return _jnp.moveaxis(t, 1, axis + 1)


def setup_inputs(seed: int = 0) -> dict:
    inp = _fwd_setup_inputs(seed)
    key = _jax.random.fold_in(_jax.random.key(seed), 7919)
    shape, _ = _output_shape()
    out = dict(inp)
    out["loss_target"] = _jax.random.normal(_jax.random.fold_in(key, 0), shape, _jnp.float32)
    for i, name in enumerate(TWIN_WEIGHTS):
        w = inp[name].astype(_jnp.float32)
        if MOMENT_SCALE is None:
            s = _jnp.sqrt(_jnp.mean(_jnp.square(w)) + 1e-30)
        else:
            s = MOMENT_SCALE[name]
        km, kv = _jax.random.split(_jax.random.fold_in(key, i + 1))
        out[name] = w
        out["m_" + name] = s * _jax.random.normal(km, w.shape, _jnp.float32)
        out["v_" + name] = (s * s) * _jax.random.uniform(kv, w.shape, _jnp.float32, 0.5, 1.5)
    if N_MICROBATCH > 1:
        for name, axis in PER_EXAMPLE_BATCH_AXIS.items():
            out[name] = _to_microbatches(out[name], axis)
    return {'x': out['x'], 'norm1_g': out['norm1_g'], 'w_in': out['w_in'], 'dn_conv_w': out['dn_conv_w'], 'dn_a_log': out['dn_a_log'], 'dn_dt_bias': out['dn_dt_bias'], 'dn_norm_g': out['dn_norm_g'], 'sc_conv_w': out['sc_conv_w'], 'sc_norm_g': out['sc_norm_g'], 'w_out': out['w_out'], 'norm2_g': out['norm2_g'], 'ffn_w_gate': out['ffn_w_gate'], 'ffn_w_up': out['ffn_w_up'], 'ffn_w_down': out['ffn_w_down'], 'final_norm_g': out['final_norm_g'], 'loss_target': out['loss_target'], 'm_norm1_g': out['m_norm1_g'], 'm_w_in': out['m_w_in'], 'm_dn_conv_w': out['m_dn_conv_w'], 'm_dn_a_log': out['m_dn_a_log'], 'm_dn_dt_bias': out['m_dn_dt_bias'], 'm_dn_norm_g': out['m_dn_norm_g'], 'm_sc_conv_w': out['m_sc_conv_w'], 'm_sc_norm_g': out['m_sc_norm_g'], 'm_w_out': out['m_w_out'], 'm_norm2_g': out['m_norm2_g'], 'm_ffn_w_gate': out['m_ffn_w_gate'], 'm_ffn_w_up': out['m_ffn_w_up'], 'm_ffn_w_down': out['m_ffn_w_down'], 'm_final_norm_g': out['m_final_norm_g'], 'v_norm1_g': out['v_norm1_g'], 'v_w_in': out['v_w_in'], 'v_dn_conv_w': out['v_dn_conv_w'], 'v_dn_a_log': out['v_dn_a_log'], 'v_dn_dt_bias': out['v_dn_dt_bias'], 'v_dn_norm_g': out['v_dn_norm_g'], 'v_sc_conv_w': out['v_sc_conv_w'], 'v_sc_norm_g': out['v_sc_norm_g'], 'v_w_out': out['v_w_out'], 'v_norm2_g': out['v_norm2_g'], 'v_ffn_w_gate': out['v_ffn_w_gate'], 'v_ffn_w_up': out['v_ffn_w_up'], 'v_ffn_w_down': out['v_ffn_w_down'], 'v_final_norm_g': out['v_final_norm_g']}


def _loss(weights, diff, rest, loss_target):
    with _jax.named_scope("forward"):
        args = {**rest, TWIN_DIFF_INPUT: diff, **{k: w.astype(_WEIGHT_DTYPES[k]) for k, w in weights.items()}}
        y = _forward(args)
    with _jax.named_scope("loss_head"):
        err = _jnp.square(y.astype(_jnp.float32) - loss_target)
        return 0.5 * _jnp.sum(_jnp.mean(err, axis=-1)) if err.ndim else 0.5 * err


def _adamw(w, g, m, v):
    m = ADAM_B1 * m + (1.0 - ADAM_B1) * g
    v = ADAM_B2 * v + (1.0 - ADAM_B2) * _jnp.square(g)
    m_hat = m / (1.0 - ADAM_B1 ** ADAM_STEP)
    v_hat = v / (1.0 - ADAM_B2 ** ADAM_STEP)
    delta = -ADAM_LR * (m_hat / (_jnp.sqrt(v_hat) + ADAM_EPS) + ADAM_WD * w)
    return delta, m, v


def reference(x, norm1_g, w_in, dn_conv_w, dn_a_log, dn_dt_bias, dn_norm_g, sc_conv_w, sc_norm_g, w_out, norm2_g, ffn_w_gate, ffn_w_up, ffn_w_down, final_norm_g, loss_target, m_norm1_g, m_w_in, m_dn_conv_w, m_dn_a_log, m_dn_dt_bias, m_dn_norm_g, m_sc_conv_w, m_sc_norm_g, m_w_out, m_norm2_g, m_ffn_w_gate, m_ffn_w_up, m_ffn_w_down, m_final_norm_g, v_norm1_g, v_w_in, v_dn_conv_w, v_dn_a_log, v_dn_dt_bias, v_dn_norm_g, v_sc_conv_w, v_sc_norm_g, v_w_out, v_norm2_g, v_ffn_w_gate, v_ffn_w_up, v_ffn_w_down, v_final_norm_g):
    given = dict(x=x, norm1_g=norm1_g, w_in=w_in, dn_conv_w=dn_conv_w, dn_a_log=dn_a_log, dn_dt_bias=dn_dt_bias, dn_norm_g=dn_norm_g, sc_conv_w=sc_conv_w, sc_norm_g=sc_norm_g, w_out=w_out, norm2_g=norm2_g, ffn_w_gate=ffn_w_gate, ffn_w_up=ffn_w_up, ffn_w_down=ffn_w_down, final_norm_g=final_norm_g, loss_target=loss_target, m_norm1_g=m_norm1_g, m_w_in=m_w_in, m_dn_conv_w=m_dn_conv_w, m_dn_a_log=m_dn_a_log, m_dn_dt_bias=m_dn_dt_bias, m_dn_norm_g=m_dn_norm_g, m_sc_conv_w=m_sc_conv_w, m_sc_norm_g=m_sc_norm_g, m_w_out=m_w_out, m_norm2_g=m_norm2_g, m_ffn_w_gate=m_ffn_w_gate, m_ffn_w_up=m_ffn_w_up, m_ffn_w_down=m_ffn_w_down, m_final_norm_g=m_final_norm_g, v_norm1_g=v_norm1_g, v_w_in=v_w_in, v_dn_conv_w=v_dn_conv_w, v_dn_a_log=v_dn_a_log, v_dn_dt_bias=v_dn_dt_bias, v_dn_norm_g=v_dn_norm_g, v_sc_conv_w=v_sc_conv_w, v_sc_norm_g=v_sc_norm_g, v_w_out=v_w_out, v_norm2_g=v_norm2_g, v_ffn_w_gate=v_ffn_w_gate, v_ffn_w_up=v_ffn_w_up, v_ffn_w_down=v_ffn_w_down, v_final_norm_g=v_final_norm_g)
    weights = {n: given[n] for n in TWIN_WEIGHTS}
    shared = {n: given[n] for n in SHARED_INPUTS}
    per_example = {n: given[n] for n in ['x']}
    grad_fn = _jax.value_and_grad(_loss, argnums=(0, 1))

    def one_microbatch(ex, loss_target):
        ex = dict(ex)
        diff = ex.pop(TWIN_DIFF_INPUT)
        return grad_fn(weights, diff, {**shared, **ex}, loss_target)

    if N_MICROBATCH == 1:
        loss, (grad_w, grad_x) = one_microbatch(per_example, given["loss_target"])
    else:
        def body(carry, xs):
            loss_sum, grad_sum = carry
            l_k, (gw_k, gx_k) = one_microbatch(xs[0], xs[1])
            with _jax.named_scope("update"):
                return (loss_sum + l_k, _jax.tree.map(_jnp.add, grad_sum, gw_k)), gx_k

        init = (_jnp.zeros((), _jnp.float32), _jax.tree.map(_jnp.zeros_like, weights))
        (loss, grad_w), grad_x = _jax.lax.scan(body, init, (per_example, given["loss_target"]))
    with _jax.named_scope("update"):
        delta_w, new_m, new_v = {}, {}, {}
        for n in TWIN_WEIGHTS:
            delta_w[n], new_m[n], new_v[n] = _adamw(weights[n], grad_w[n], given["m_" + n], given["v_" + n])
    return (loss, grad_x, *[grad_w[n] for n in TWIN_WEIGHTS], *[delta_w[n] for n in TWIN_WEIGHTS],
            *[new_m[n] for n in TWIN_WEIGHTS], *[new_v[n] for n in TWIN_WEIGHTS])
```

```python
import functools

import jax
import jax.numpy as jnp
from jax import lax
from jax.experimental import pallas as pl
from jax.experimental.pallas import tpu as pltpu

F32 = jnp.float32
MXU_DTYPE = jnp.bfloat16
HIGHEST = lax.Precision.HIGHEST
MESH = pl.DeviceIdType.MESH

N_DEV = 8
EPS = 1e-6
DN_HEADS = 4
HEAD_DIM = 128
DN_WIDTH = DN_HEADS * HEAD_DIM
SC_WIDTH = 512
SC_GROUPS = 4
DN_CONV = 4
SC_CONV = 3
CHUNK = 64
HALO = 8
LANES = 128

QKV_W = 3 * DN_WIDTH
Z_OFF = QKV_W
SC_OFF = Z_OFF + DN_WIDTH
BA_OFF = SC_OFF + 3 * SC_WIDTH
BA_W = 256
PROJ_W = BA_OFF + BA_W

ADAM_LR = 0.001
ADAM_B1 = 0.9
ADAM_B2 = 0.999
ADAM_EPS = 1e-08
ADAM_WD = 0.01
ADAM_STEP = 10


def _pick(n, cands):
    for c in cands:
        if n % c == 0:
            return c
    return n


def _params(*sem):
    return pltpu.CompilerParams(dimension_semantics=sem)


def _rms_norm(x, g):
    return x * lax.rsqrt(jnp.mean(x * x, axis=-1, keepdims=True) + EPS) * g


def _dot(a, b, dims=(((1,), (0,)), ((), ()))):
    return lax.dot_general(a.astype(MXU_DTYPE), b.astype(MXU_DTYPE), dims, preferred_element_type=F32)


def _bmm(a, b, spec, exact=False):
    if exact:
        return jnp.einsum(spec, a, b, preferred_element_type=F32, precision=HIGHEST)
    return jnp.einsum(spec, a.astype(MXU_DTYPE), b.astype(MXU_DTYPE), preferred_element_type=F32)


def _causal_conv(cur, halo, w, k):
    tt = cur.shape[0]
    xp = jnp.concatenate([halo, cur], axis=0)
    y = None
    for j in range(k):
        start = HALO - (k - 1) + j
        term = xp[start:start + tt] * w[j:j + 1]
        y = term if y is None else y + term
    return y


def _matmul(a, b, mode, out_dtype, name, residual=None):
    if mode == "nn":
        (m, k), (k2, n) = a.shape, b.shape
    elif mode == "nt":
        (m, k), (n, k2) = a.shape, b.shape
    else:
        (k, m), (k2, n) = a.shape, b.shape
    assert k == k2
    tm = _pick(m, (512, 1408, 256, 128))
    tn = _pick(n, (1280, 1408, 1024, 512, 256, 128))
    tk = k if k <= 2816 else _pick(k, (1024, 768, 512))
    nk = k // tk
    dims = {"nn": (((1,), (0,)), ((), ())), "nt": (((1,), (1,)), ((), ())), "tn": (((0,), (0,)), ((), ()))}[mode]
    a_spec = {"nn": pl.BlockSpec((tm, tk), lambda i, j, q: (i, q)),
              "nt": pl.BlockSpec((tm, tk), lambda i, j, q: (i, q)),
              "tn": pl.BlockSpec((tk, tm), lambda i, j, q: (q, i))}[mode]
    b_spec = {"nn": pl.BlockSpec((tk, tn), lambda i, j, q: (q, j)),
              "nt": pl.BlockSpec((tn, tk), lambda i, j, q: (j, q)),
              "tn": pl.BlockSpec((tk, tn), lambda i, j, q: (q, j))}[mode]
    o_spec = pl.BlockSpec((tm, tn), lambda i, j, q: (i, j))
    has_res = residual is not None

    def body(*refs):
        if has_res:
            a_ref, b_ref, r_ref, o_ref, acc = refs
        else:
            a_ref, b_ref, o_ref, acc = refs
        q = pl.program_id(2)

        @pl.when(q == 0)
        def _():
            acc[...] = jnp.zeros_like(acc)

        acc[...] += _dot(a_ref[...], b_ref[...], dims)

        @pl.when(q == nk - 1)
        def _():
            r = acc[...]
            if has_res:
                r = r + r_ref[...]
            o_ref[...] = r.astype(o_ref.dtype)

    in_specs = [a_spec, b_spec] + ([o_spec] if has_res else [])
    args = (a, b) + ((residual,) if has_res else ())
    return pl.pallas_call(
        body, name=name, grid=(m // tm, n // tn, nk), in_specs=in_specs, out_specs=o_spec,
        out_shape=jax.ShapeDtypeStruct((m, n), out_dtype),
        scratch_shapes=[pltpu.VMEM((tm, tn), F32)],
        compiler_params=_params("parallel", "parallel", "arbitrary"),
    )(*args)


def _norm_matmul(x, g, w, name):
    t, d = x.shape
    n = w.shape[1]
    tm = _pick(t, (512, 256, 128))
    tn = _pick(n, (1280, 1408, 1024, 512, 256, 128))

    def body(x_ref, g_ref, w_ref, h_ref, y_ref, h_scr):
        @pl.when(pl.program_id(1) == 0)
        def _():
            h = _rms_norm(x_ref[...], g_ref[...]).astype(MXU_DTYPE)
            h_scr[...] = h
            h_ref[...] = h

        y_ref[...] = _dot(h_scr[...], w_ref[...])

    return pl.pallas_call(
        body, name=name, grid=(t // tm, n // tn),
        in_specs=[pl.BlockSpec((tm, d), lambda i, j: (i, 0)), pl.BlockSpec((1, d), lambda i, j: (0, 0)),
                  pl.BlockSpec((d, tn), lambda i, j: (0, j))],
        out_specs=[pl.BlockSpec((tm, d), lambda i, j: (i, 0)), pl.BlockSpec((tm, tn), lambda i, j: (i, j))],
        out_shape=[jax.ShapeDtypeStruct((t, d), MXU_DTYPE), jax.ShapeDtypeStruct((t, n), F32)],
        scratch_shapes=[pltpu.VMEM((tm, d), MXU_DTYPE)],
        compiler_params=_params("parallel", "arbitrary"),
    )(x, g, w)


def _tok_call(name, fn, t, tt, tok_in, const_in, tok_out, acc_out):
    nblk = t // tt
    hb = tt // HALO
    in_specs, args = [], []
    for arr, w, cb, mode in tok_in:
        if mode == "cur":
            spec = pl.BlockSpec((tt, w), lambda i, cb=cb: (i, cb))
        elif mode == "prev":
            spec = pl.BlockSpec((HALO, w), lambda i, cb=cb: (jnp.maximum(i * hb - 1, 0), cb))
        else:
            spec = pl.BlockSpec((HALO, w), lambda i, cb=cb: (jnp.minimum(i + 1, nblk - 1), cb))
        in_specs.append(spec)
        args.append(arr)
    for arr in const_in:
        in_specs.append(pl.BlockSpec(arr.shape, lambda i: (0, 0)))
        args.append(arr)
    out_specs, out_shape = [], []
    for rows, w, dt in tok_out:
        out_specs.append(pl.BlockSpec((rows, w), lambda i: (i, 0)))
        out_shape.append(jax.ShapeDtypeStruct((nblk * rows, w), dt))
    for shp, dt in acc_out:
        out_specs.append(pl.BlockSpec(shp, lambda i: (0, 0)))
        out_shape.append(jax.ShapeDtypeStruct(shp, dt))
    n_tok, n_const, n_out = len(tok_in), len(const_in), len(tok_out)

    def body(*refs):
        i = pl.program_id(0)
        tok_vals = []
        for (_, _, _, mode), r in zip(tok_in, refs[:n_tok]):
            v = r[...]
            if mode == "prev":
                v = jnp.where(i > 0, v, jnp.zeros_like(v))
            elif mode == "next8":
                v = jnp.where(i < nblk - 1, v, jnp.zeros_like(v))
            tok_vals.append(v)
        const_vals = [r[...] for r in refs[n_tok:n_tok + n_const]]
        outs, accs = fn(tok_vals, const_vals)
        o_refs = refs[n_tok + n_const:n_tok + n_const + n_out]
        a_refs = refs[n_tok + n_const + n_out:]
        for r, v in zip(o_refs, outs):
            r[...] = v.astype(r.dtype)
        if a_refs:
            @pl.when(i == 0)
            def _():
                for r in a_refs:
                    r[...] = jnp.zeros_like(r)

            for r, v in zip(a_refs, accs):
                r[...] += v.astype(r.dtype)

    res = pl.pallas_call(
        body, name=name, grid=(nblk,), in_specs=in_specs, out_specs=out_specs, out_shape=out_shape,
        compiler_params=_params("arbitrary" if acc_out else "parallel"),
    )(*args)
    return res


def _dn_pre_math(cur, halo, ba, cw, alog, dtb):
    tt = cur.shape[0]
    a = jax.nn.silu(_causal_conv(cur, halo, cw, DN_CONV))
    pieces = []
    for p in range(2 * DN_HEADS):
        xh = a[:, p * HEAD_DIM:(p + 1) * HEAD_DIM]
        xh = xh * lax.rsqrt(jnp.sum(xh * xh, axis=-1, keepdims=True) + EPS)
        if p < DN_HEADS:
            xh = xh * (HEAD_DIM ** -0.5)
        pieces.append(xh)
    pieces.append(a[:, 2 * DN_WIDTH:])
    qkvn = jnp.concatenate(pieces, axis=1)
    lane = lax.broadcasted_iota(jnp.int32, ba.shape, 1)
    raw = jnp.where(lane < DN_HEADS, jax.nn.sigmoid(ba), -jnp.exp(alog) * jax.nn.softplus(ba + dtb))
    r = lax.broadcasted_iota(jnp.int32, (tt, tt), 0)
    c = lax.broadcasted_iota(jnp.int32, (tt, tt), 1)
    tri = jnp.where((r // CHUNK == c // CHUNK) & (c <= r), 1.0, 0.0).astype(F32)
    cums = jnp.dot(tri, raw, preferred_element_type=F32, precision=HIGHEST)
    bg = jnp.where(lane < DN_HEADS, raw, cums)
    return qkvn, bg


def _mix_math(o, z, gb, gc, gc_halo, hv, hv_halo, dng, scg, scw):
    outs = []
    for h in range(DN_HEADS):
        sl = slice(h * HEAD_DIM, (h + 1) * HEAD_DIM)
        oh = o[:, sl]
        outs.append(oh * lax.rsqrt(jnp.mean(oh * oh, axis=-1, keepdims=True) + EPS) * dng * jax.nn.silu(z[:, sl]))
    y = gb * _causal_conv(gc * hv, gc_halo * hv_halo, scw, SC_CONV)
    gw = SC_WIDTH // SC_GROUPS
    for g in range(SC_GROUPS):
        sl = slice(g * gw, (g + 1) * gw)
        yg = y[:, sl]
        outs.append(yg * lax.rsqrt(jnp.mean(yg * yg, axis=-1, keepdims=True) + EPS) * scg[:, sl])
    return jnp.concatenate(outs, axis=1)


def _swiglu_math(g, u):
    return jax.nn.silu(g) * u


def _tri_inverse(a):
    c = a.shape[-1]
    r = lax.broadcasted_iota(jnp.int32, (c, c), 0)
    q = lax.broadcasted_iota(jnp.int32, (c, c), 1)
    eye = jnp.where(r == q, 1.0, 0.0).astype(F32)[None]
    blk = (r // 16 == q // 16)[None]
    d = jnp.where(blk, a, 0.0)
    o = a - d
    mm = functools.partial(_bmm, spec="bij,bjk->bik", exact=True)
    n = -d
    p = eye + n
    for _ in range(3):
        n = mm(n, n)
        p = p + mm(p, n)
    e = mm(p, o)
    e2 = mm(e, e)
    left = eye - e + e2 - mm(e, e2)
    return mm(left, p)


def _delta_intra_math(q, k, v, bg, head):
    n = q.shape[0]
    nb = n // CHUNK
    lane = lax.broadcasted_iota(jnp.int32, bg.shape, 1)
    beta = jnp.sum(jnp.where(lane == head, bg, 0.0), axis=1, keepdims=True).reshape(nb, CHUNK, 1)
    gc = jnp.sum(jnp.where(lane == head + DN_HEADS, bg, 0.0), axis=1, keepdims=True).reshape(nb, CHUNK, 1)
    q3, k3, v3 = (a.reshape(nb, CHUNK, HEAD_DIM) for a in (q, k, v))
    r = lax.broadcasted_iota(jnp.int32, (CHUNK, CHUNK), 0)
    c = lax.broadcasted_iota(jnp.int32, (CHUNK, CHUNK), 1)
    eye = jnp.where(r == c, 1.0, 0.0).astype(F32)[None]
    gcr = _bmm(jnp.ones((nb, CHUNK, CHUNK), F32), gc * eye, "bik,bkj->bij", exact=True)
    decay = jnp.exp(jnp.where((r >= c)[None], gc - gcr, -1e30))
    kb = k3 * beta
    vb = v3 * beta
    egc = jnp.exp(gc)
    a = jnp.where((r > c)[None], _bmm(kb, k3, "bcd,bmd->bcm") * decay, 0.0)
    tinv = _tri_inverse(a)
    u = _bmm(tinv, vb, "bcm,bmd->bcd", exact=True)
    w = _bmm(tinv, kb * egc, "bcm,bmd->bcd", exact=True)
    qk = _bmm(q3, k3, "bcd,bmd->bcm") * decay
    row = lax.broadcasted_iota(jnp.int32, (nb, CHUNK, 1), 1)
    glast = jnp.sum(jnp.where(row == CHUNK - 1, gc, 0.0), axis=1, keepdims=True)
    qd = q3 * egc
    kd = k3 * jnp.exp(glast - gc)
    glb = jnp.broadcast_to(jnp.exp(glast), (nb, HALO, LANES))
    flat = lambda x: x.reshape(n, HEAD_DIM)
    return flat(u), flat(w), flat(qd), flat(kd), qk, glb


def _delta_step_math(u, w, qd, kd, qk, gl, s):
    vnew = u - _dot(w, s)
    o = _dot(qd, s) + _dot(qk, vnew)
    s2 = s * gl + _dot(kd, vnew, (((0,), (0,)), ((), ())))
    return o, s2


def _delta_intra(qkvn, bg, nb):
    t = qkvn.shape[0]
    n = t // CHUNK
    rows = nb * CHUNK

    def body(q_ref, k_ref, v_ref, bg_ref, u_ref, w_ref, qd_ref, kd_ref, qk_ref, gl_ref):
        outs = _delta_intra_math(q_ref[...], k_ref[...], v_ref[...], bg_ref[...], pl.program_id(1))
        for r, v in zip((u_ref, w_ref, qd_ref, kd_ref, qk_ref, gl_ref), outs):
            r[...] = v

    col = lambda off: pl.BlockSpec((rows, HEAD_DIM), lambda b, h, off=off: (b, off + h))
    tok = jax.ShapeDtypeStruct((t, DN_WIDTH), F32)
    return pl.pallas_call(
        body, name="delta_intra", grid=(n // nb, DN_HEADS),
        in_specs=[col(0), col(DN_HEADS), col(2 * DN_HEADS), pl.BlockSpec((rows, LANES), lambda b, h: (b, 0))],
        out_specs=[col(0)] * 4 + [pl.BlockSpec((nb, None, CHUNK, CHUNK), lambda b, h: (b, h, 0, 0)),
                                  pl.BlockSpec((nb, None, HALO, LANES), lambda b, h: (b, h, 0, 0))],
        out_shape=[tok] * 4 + [jax.ShapeDtypeStruct((n, DN_HEADS, CHUNK, CHUNK), F32),
                               jax.ShapeDtypeStruct((n, DN_HEADS, HALO, LANES), F32)],
        compiler_params=_params("parallel", "arbitrary"),
    )(qkvn, qkvn, qkvn, bg)


def _delta_intra_bwd(qkvn, bg, cts, nb):
    t = qkvn.shape[0]
    n = t // CHUNK
    rows = nb * CHUNK

    def body(q_ref, k_ref, v_ref, bg_ref, du, dw, dqd, dkd, dqk, dgl, dq_ref, dk_ref, dv_ref, dbg_ref):
        h = pl.program_id(1)
        _, vjp = jax.vjp(lambda q, k, v, b: _delta_intra_math(q, k, v, b, h), q_ref[...], k_ref[...], v_ref[...], bg_ref[...])
        dq, dk, dv, dbg = vjp((du[...], dw[...], dqd[...], dkd[...], dqk[...], dgl[...]))
        dq_ref[...] = dq
        dk_ref[...] = dk
        dv_ref[...] = dv

        @pl.when(h == 0)
        def _():
            dbg_ref[...] = jnp.zeros_like(dbg_ref)

        dbg_ref[...] += dbg

    col = lambda off: pl.BlockSpec((rows, HEAD_DIM), lambda b, h, off=off: (b, off + h))
    bgs = pl.BlockSpec((rows, LANES), lambda b, h: (b, 0))
    qks = pl.BlockSpec((nb, None, CHUNK, CHUNK), lambda b, h: (b, h, 0, 0))
    gls = pl.BlockSpec((nb, None, HALO, LANES), lambda b, h: (b, h, 0, 0))
    tok = jax.ShapeDtypeStruct((t, DN_WIDTH), F32)
    return pl.pallas_call(
        body, name="delta_intra_bwd", grid=(n // nb, DN_HEADS),
        in_specs=[col(0), col(DN_HEADS), col(2 * DN_HEADS), bgs, col(0), col(0), col(0), col(0), qks, gls],
        out_specs=[col(0), col(0), col(0), bgs],
        out_shape=[tok, tok, tok, jax.ShapeDtypeStruct((t, LANES), F32)],
        compiler_params=_params("parallel", "arbitrary"),
    )(qkvn, qkvn, qkvn, bg, *cts)


def _delta_scan(u, w, qd, kd, qk, glb, cb):
    t = u.shape[0]
    n = t // CHUNK
    rows = cb * CHUNK

    def body(u_ref, w_ref, qd_ref, kd_ref, qk_ref, gl_ref, o_ref, s_ref, s_scr):
        @pl.when(pl.program_id(0) == 0)
        def _():
            s_scr[...] = jnp.zeros_like(s_scr)

        def chunk(c, carry):
            r0 = pl.multiple_of(c * CHUNK, CHUNK)
            for h in range(DN_HEADS):
                sl = (pl.ds(r0, CHUNK), slice(h * HEAD_DIM, (h + 1) * HEAD_DIM))
                s = s_scr[h]
                s_ref[c, h] = s
                o, s2 = _delta_step_math(u_ref[sl], w_ref[sl], qd_ref[sl], kd_ref[sl], qk_ref[c, h],
                                         gl_ref[c, h][0:1, :], s)
                o_ref[sl] = o
                s_scr[h] = s2
            return carry

        lax.fori_loop(0, cb, chunk, 0)

    tok = pl.BlockSpec((rows, DN_WIDTH), lambda i: (i, 0))
    return pl.pallas_call(
        body, name="delta_scan", grid=(n // cb,),
        in_specs=[tok] * 4 + [pl.BlockSpec((cb, DN_HEADS, CHUNK, CHUNK), lambda i: (i, 0, 0, 0)),
                              pl.BlockSpec((cb, DN_HEADS, HALO, LANES), lambda i: (i, 0, 0, 0))],
        out_specs=[tok, pl.BlockSpec((cb, DN_HEADS, HEAD_DIM, HEAD_DIM), lambda i: (i, 0, 0, 0))],
        out_shape=[jax.ShapeDtypeStruct((t, DN_WIDTH), F32),
                   jax.ShapeDtypeStruct((n, DN_HEADS, HEAD_DIM, HEAD_DIM), F32)],
        scratch_shapes=[pltpu.VMEM((DN_HEADS, HEAD_DIM, HEAD_DIM), F32)],
        compiler_params=_params("arbitrary"),
    )(u, w, qd, kd, qk, glb)


def _delta_scan_bwd(u, w, qd, kd, qk, glb, s_all, do, cb):
    t = u.shape[0]
    n = t // CHUNK
    nblk = n // cb
    rows = cb * CHUNK

    def body(u_ref, w_ref, qd_ref, kd_ref, qk_ref, gl_ref, s_ref, do_ref,
             du_ref, dw_ref, dqd_ref, dkd_ref, dqk_ref, dgl_ref, ds_scr):
        @pl.when(pl.program_id(0) == 0)
        def _():
            ds_scr[...] = jnp.zeros_like(ds_scr)

        def chunk(step, carry):
            c = cb - 1 - step
            r0 = pl.multiple_of(c * CHUNK, CHUNK)
            for h in range(DN_HEADS):
                sl = (pl.ds(r0, CHUNK), slice(h * HEAD_DIM, (h + 1) * HEAD_DIM))
                gl_tile = gl_ref[c, h]
                prim = (u_ref[sl], w_ref[sl], qd_ref[sl], kd_ref[sl], qk_ref[c, h], gl_tile, s_ref[c, h])
                _, vjp = jax.vjp(lambda a, b, cc, d, e, g, s: _delta_step_math(a, b, cc, d, e, g[0:1, :], s), *prim)
                du, dw, dqd, dkd, dqk, dgl, ds = vjp((do_ref[sl], ds_scr[h]))
                du_ref[sl] = du
                dw_ref[sl] = dw
                dqd_ref[sl] = dqd
                dkd_ref[sl] = dkd
                dqk_ref[c, h] = dqk
                dgl_ref[c, h] = dgl
                ds_scr[h] = ds
            return carry

        lax.fori_loop(0, cb, chunk, 0)

    rev = lambda i: nblk - 1 - i
    tok = pl.BlockSpec((rows, DN_WIDTH), lambda i: (rev(i), 0))
    qks = pl.BlockSpec((cb, DN_HEADS, CHUNK, CHUNK), lambda i: (rev(i), 0, 0, 0))
    gls = pl.BlockSpec((cb, DN_HEADS, HALO, LANES), lambda i: (rev(i), 0, 0, 0))
    ss = pl.BlockSpec((cb, DN_HEADS, HEAD_DIM, HEAD_DIM), lambda i: (rev(i), 0, 0, 0))
    tshape = jax.ShapeDtypeStruct((t, DN_WIDTH), F32)
    return pl.pallas_call(
        body, name="delta_scan_bwd", grid=(nblk,),
        in_specs=[tok] * 4 + [qks, gls, ss, tok],
        out_specs=[tok] * 4 + [qks, gls],
        out_shape=[tshape] * 4 + [jax.ShapeDtypeStruct(qk.shape, F32), jax.ShapeDtypeStruct(glb.shape, F32)],
        scratch_shapes=[pltpu.VMEM((DN_HEADS, HEAD_DIM, HEAD_DIM), F32)],
        compiler_params=_params("arbitrary"),
    )(u, w, qd, kd, qk, glb, s_all, do)


def _peer(mask):
    x, y, c = lax.axis_index("x"), lax.axis_index("y"), lax.axis_index("c")
    return (x ^ ((mask >> 2) & 1), y ^ ((mask >> 1) & 1), c ^ (mask & 1))


def _my_index():
    return 4 * lax.axis_index("x") + 2 * lax.axis_index("y") + lax.axis_index("c")


def _all_gather(shards, name):
    na = len(shards)

    def body(*refs):
        ins, outs = refs[:na], refs[na:2 * na]
        send_sems, recv_sems, local_sems = refs[2 * na:]
        me = _my_index()
        copies = []
        for a in range(na):
            loc = pltpu.make_async_copy(ins[a], outs[a].at[me], local_sems.at[a])
            loc.start()
            copies.append(loc)

        def rc(a, k, block, to, src=None):
            dst = outs[a].at[block]
            return pltpu.make_async_remote_copy(src_ref=dst if src is None else src, dst_ref=dst,
                                                send_sem=send_sems.at[a, k], recv_sem=recv_sems.at[a, k],
                                                device_id=to, device_id_type=MESH)

        sib = _peer(1)
        chip_masks = (4, 2, 6)
        first = []
        for a in range(na):
            first.append(rc(a, 0, me, sib, src=ins[a]))
            for j, m in enumerate(chip_masks):
                first.append(rc(a, 1 + j, me, _peer(m), src=ins[a]))
        for cp in first:
            cp.start()
        passed = []
        for j, m in enumerate(chip_masks):
            for a in range(na):
                rc(a, 1 + j, me ^ m, _peer(0)).wait_recv()
                p = rc(a, 4 + j, me ^ m, sib)
                p.start()
                passed.append(p)
        for a in range(na):
            rc(a, 0, me ^ 1, _peer(0)).wait_recv()
            for j, m in enumerate(chip_masks):
                rc(a, 4 + j, me ^ m ^ 1, _peer(0)).wait_recv()
        for cp in first + passed:
            cp.wait_send()
        for cp in copies:
            cp.wait()

    hbm = pl.BlockSpec(memory_space=pltpu.HBM)
    return pl.pallas_call(
        body, name=name, in_specs=[hbm] * na, out_specs=[hbm] * na,
        out_shape=[jax.ShapeDtypeStruct((N_DEV,) + s.shape, s.dtype) for s in shards],
        scratch_shapes=[pltpu.SemaphoreType.DMA((na, 7)), pltpu.SemaphoreType.DMA((na, 7)),
                        pltpu.SemaphoreType.DMA((na,))],
        compiler_params=pltpu.CompilerParams(has_side_effects=True),
    )(*shards)


def _all_to_all(parts, name):
    na = len(parts)

    def body(*refs):
        ins, outs = refs[:na], refs[na:2 * na]
        send_sems, recv_sems, local_sems = refs[2 * na:]
        me = _my_index()
        started = []
        for a in range(na):
            loc = pltpu.make_async_copy(ins[a].at[me], outs[a].at[me], local_sems.at[a])
            loc.start()
            started.append(loc)
        remote = []
        for m in range(1, N_DEV):
            for a in range(na):
                cp = pltpu.make_async_remote_copy(src_ref=ins[a].at[me ^ m], dst_ref=outs[a].at[me],
                                                  send_sem=send_sems.at[a, m - 1], recv_sem=recv_sems.at[a, m - 1],
                                                  device_id=_peer(m), device_id_type=MESH)
                cp.start()
                remote.append(cp)
        for cp in remote:
            cp.wait()
        for cp in started:
            cp.wait()

    hbm = pl.BlockSpec(memory_space=pltpu.HBM)
    return pl.pallas_call(
        body, name=name, in_specs=[hbm] * na, out_specs=[hbm] * na,
        out_shape=[jax.ShapeDtypeStruct(p.shape, p.dtype) for p in parts],
        scratch_shapes=[pltpu.SemaphoreType.DMA((na, 7)), pltpu.SemaphoreType.DMA((na, 7)),
                        pltpu.SemaphoreType.DMA((na,))],
        compiler_params=pltpu.CompilerParams(has_side_effects=True),
    )(*parts)


def _adamw_math(w, g, m, v):
    m2 = ADAM_B1 * m + (1.0 - ADAM_B1) * g
    v2 = ADAM_B2 * v + (1.0 - ADAM_B2) * jnp.square(g)
    m_hat = m2 / (1.0 - ADAM_B1 ** ADAM_STEP)
    v_hat = v2 / (1.0 - ADAM_B2 ** ADAM_STEP)
    delta = -ADAM_LR * (m_hat / (jnp.sqrt(v_hat) + ADAM_EPS) + ADAM_WD * w)
    return delta, m2, v2


def _sum_adamw(parts, w, m, v, name):
    r, c = w.shape
    tr = _pick(r, (512, 256, 352, 128, 64, 32, 16, 8))
    np_ = parts.shape[0]

    def body(p_ref, w_ref, m_ref, v_ref, g_ref, d_ref, m2_ref, v2_ref):
        g = p_ref[0]
        for d in range(1, np_):
            g = g + p_ref[d]
        delta, m2, v2 = _adamw_math(w_ref[...], g, m_ref[...], v_ref[...])
        g_ref[...] = g
        d_ref[...] = delta
        m2_ref[...] = m2
        v2_ref[...] = v2

    blk = pl.BlockSpec((tr, c), lambda i: (i, 0))
    shp = jax.ShapeDtypeStruct((r, c), F32)
    return pl.pallas_call(
        body, name=name, grid=(r // tr,),
        in_specs=[pl.BlockSpec((np_, tr, c), lambda i: (0, i, 0)), blk, blk, blk],
        out_specs=[blk] * 4, out_shape=[shp] * 4, compiler_params=_params("parallel"),
    )(parts, w, m, v)


def _sum_rows(parts, name):
    np_, r, c = parts.shape

    def body(p_ref, o_ref):
        g = p_ref[0]
        for d in range(1, np_):
            g = g + p_ref[d]
        o_ref[...] = g

    return pl.pallas_call(body, name=name, out_shape=jax.ShapeDtypeStruct((r, c), F32))(parts)


def _pad_w_in(w):
    d = w.shape[0]
    n_ba = 2 * DN_HEADS
    a = w[:, :SC_OFF]
    ba = w[:, SC_OFF:SC_OFF + n_ba]
    sc = w[:, SC_OFF + n_ba:]
    return jnp.concatenate([a, sc, ba, jnp.zeros((d, BA_W - n_ba), w.dtype)], axis=1)


def _unpad_w_in(wp):
    n_ba = 2 * DN_HEADS
    return jnp.concatenate([wp[:, :SC_OFF], wp[:, BA_OFF:BA_OFF + n_ba], wp[:, SC_OFF:BA_OFF]], axis=1)


def _lane_row(v, off):
    return jnp.zeros((1, LANES), F32).at[0, off:off + v.shape[0]].set(v)


TT = 256
NB_INTRA = 8
CB_SCAN = 8


def _layer_fwd(x, p):
    t, d = x.shape
    h, proj = _norm_matmul(x, p["norm1_g"], p["w_in"], "proj_fwd")
    qkvn, bg = _tok_call(
        "dn_pre", lambda tv, cv: (_dn_pre_math(*tv, *cv), ()), t, TT,
        [(proj, QKV_W, 0, "cur"), (proj, QKV_W, 0, "prev"), (proj, LANES, BA_OFF // LANES, "cur")],
        [p["dn_conv_w"], p["alog_row"], p["dt_row"]],
        [(TT, QKV_W, F32), (TT, LANES, F32)], [])
    u, w, qd, kd, qk, glb = _delta_intra(qkvn, bg, NB_INTRA)
    o, s_all = _delta_scan(u, w, qd, kd, qk, glb, CB_SCAN)
    cb0 = SC_OFF // SC_WIDTH
    mix_in = [(o, DN_WIDTH, 0, "cur"), (proj, DN_WIDTH, Z_OFF // DN_WIDTH, "cur"),
              (proj, SC_WIDTH, cb0, "cur"), (proj, SC_WIDTH, cb0 + 1, "cur"), (proj, SC_WIDTH, cb0 + 1, "prev"),
              (proj, SC_WIDTH, cb0 + 2, "cur"), (proj, SC_WIDTH, cb0 + 2, "prev")]
    mix_const = [p["dn_norm_g"], p["sc_norm_g"], p["sc_conv_w"]]
    (cat,) = _tok_call("mix_post", lambda tv, cv: ((_mix_math(*tv, *cv),), ()), t, TT,
                       mix_in, mix_const, [(TT, 2 * DN_WIDTH, MXU_DTYPE)], [])
    x_mid = _matmul(cat, p["w_out"], "nn", F32, "out_proj", residual=x)
    h2, gu = _norm_matmul(x_mid, p["norm2_g"], p["w_gu"], "ffn_up")
    dff = gu.shape[1] // 2
    (act,) = _tok_call("swiglu", lambda tv, cv: ((_swiglu_math(*tv),), ()), t, TT,
                       [(gu, dff, 0, "cur"), (gu, dff, 1, "cur")], [], [(TT, dff, MXU_DTYPE)], [])
    x_out = _matmul(act, p["w_down"], "nn", F32, "ffn_down", residual=x_mid)
    saved = dict(x=x, h=h, proj=proj, qkvn=qkvn, bg=bg, u=u, w=w, qd=qd, kd=kd, qk=qk, glb=glb, s_all=s_all, o=o,
                 cat=cat, x_mid=x_mid, h2=h2, gu=gu, act=act, mix_in=mix_in, mix_const=mix_const)
    return x_out, saved


def _norm_bwd(x, g, dh, dres, name):
    t, d = x.shape

    def fn(tv, cv):
        xv, dhv, drv = tv
        _, vjp = jax.vjp(_rms_norm, xv, cv[0])
        dx, dg = vjp(dhv)
        return (drv + dx,), (dg,)

    return _tok_call(name, fn, t, TT, [(x, d, 0, "cur"), (dh, d, 0, "cur"), (dres, d, 0, "cur")], [g],
                     [(TT, d, F32)], [((1, d), F32)])


def _layer_bwd(dx_out, p, s):
    t, d = dx_out.shape
    nblk = t // TT
    dact = _matmul(dx_out, p["w_down"], "nt", F32, "ffn_down_dx")
    d_w_down = _matmul(s["act"], dx_out, "tn", F32, "ffn_down_dw")
    gu = s["gu"]
    dff = gu.shape[1] // 2

    def swiglu_bwd(tv, cv):
        g, u, da = tv
        _, vjp = jax.vjp(_swiglu_math, g, u)
        dg, du = vjp(da)
        return (jnp.concatenate([dg, du], axis=1),), ()

    (dgu,) = _tok_call("swiglu_bwd", swiglu_bwd, t, TT,
                       [(gu, dff, 0, "cur"), (gu, dff, 1, "cur"), (dact, dff, 0, "cur")], [],
                       [(TT, 2 * dff, MXU_DTYPE)], [])
    dh2 = _matmul(dgu, p["w_gu"], "nt", F32, "ffn_up_dx")
    d_w_gu = _matmul(s["h2"], dgu, "tn", F32, "ffn_up_dw")
    dx_mid, d_norm2 = _norm_bwd(s["x_mid"], p["norm2_g"], dh2, dx_out, "norm2_bwd")
    dcat = _matmul(dx_mid, p["w_out"], "nt", F32, "out_proj_dx")
    d_w_out = _matmul(s["cat"], dx_mid, "tn", F32, "out_proj_dw")

    def mix_bwd(tv, cv):
        prim = tuple(tv[:7]) + tuple(cv)
        _, vjp = jax.vjp(_mix_math, *prim)
        do, dz, dgb, dgc, dgch, dhv, dhvh, ddng, dscg, dscw = vjp(tv[7])
        return (do, dz, dgb, dgc, dgch, dhv, dhvh), (ddng, dscg, dscw)

    wide = (TT, DN_WIDTH, F32)
    halo = (HALO, SC_WIDTH, F32)
    do, dz, dgb, dgc, dgc_h, dhv, dhv_h, d_dn_norm, d_sc_norm, d_sc_conv = _tok_call(
        "mix_post_bwd", mix_bwd, t, TT, s["mix_in"] + [(dcat, 2 * DN_WIDTH, 0, "cur")], s["mix_const"],
        [wide, wide, wide, wide, halo, wide, halo],
        [((1, HEAD_DIM), F32), ((1, SC_WIDTH), F32), ((SC_CONV, SC_WIDTH), F32)])
    cts = _delta_scan_bwd(s["u"], s["w"], s["qd"], s["kd"], s["qk"], s["glb"], s["s_all"], do, CB_SCAN)
    dq, dk, dv, dbg = _delta_intra_bwd(s["qkvn"], s["bg"], cts, NB_INTRA)
    proj = s["proj"]

    def dn_pre_bwd(tv, cv):
        cur, hal, ba, dq_, dk_, dv_, dbg_ = tv
        _, vjp = jax.vjp(_dn_pre_math, cur, hal, ba, *cv)
        dcur, dhal, dba, dcw, dal, ddt = vjp((jnp.concatenate([dq_, dk_, dv_], axis=1), dbg_))
        return (dcur, dhal, dba), (dcw, dal, ddt)

    dqkv, dqkv_h, dba, d_dn_conv, d_alog, d_dt = _tok_call(
        "dn_pre_bwd", dn_pre_bwd, t, TT,
        [(proj, QKV_W, 0, "cur"), (proj, QKV_W, 0, "prev"), (proj, LANES, BA_OFF // LANES, "cur"),
         (dq, DN_WIDTH, 0, "cur"), (dk, DN_WIDTH, 0, "cur"), (dv, DN_WIDTH, 0, "cur"), (dbg, LANES, 0, "cur")],
        [p["dn_conv_w"], p["alog_row"], p["dt_row"]],
        [(TT, QKV_W, F32), (HALO, QKV_W, F32), (TT, LANES, F32)],
        [((DN_CONV, QKV_W), F32), ((1, LANES), F32), ((1, LANES), F32)])

    def assemble(tv, cv):
        dqkv_, dqkv_n, dz_, dgb_, dgc_, dgc_n, dhv_, dhv_n, dba_ = tv

        def with_halo(cur, nxt):
            return cur + jnp.concatenate([jnp.zeros((TT - HALO, cur.shape[1]), F32), nxt], axis=0)

        out = jnp.concatenate([with_halo(dqkv_, dqkv_n), dz_, dgb_, with_halo(dgc_, dgc_n), with_halo(dhv_, dhv_n),
                               dba_, jnp.zeros((TT, BA_W - LANES), F32)], axis=1)
        return (out,), ()

    (dproj,) = _tok_call(
        "dproj_assemble", assemble, t, TT,
        [(dqkv, QKV_W, 0, "cur"), (dqkv_h, QKV_W, 0, "next8"), (dz, DN_WIDTH, 0, "cur"), (dgb, SC_WIDTH, 0, "cur"),
         (dgc, SC_WIDTH, 0, "cur"), (dgc_h, SC_WIDTH, 0, "next8"), (dhv, SC_WIDTH, 0, "cur"),
         (dhv_h, SC_WIDTH, 0, "next8"), (dba, LANES, 0, "cur")], [],
        [(TT, PROJ_W, MXU_DTYPE)], [])
    dh = _matmul(dproj, p["w_in"], "nt", F32, "proj_dx")
    d_w_in = _matmul(s["h"], dproj, "tn", F32, "proj_dw")
    dx_in, d_norm1 = _norm_bwd(s["x"], p["norm1_g"], dh, dx_mid, "norm1_bwd")
    grads = dict(w_in=d_w_in, w_out=d_w_out, w_gu=d_w_gu, w_down=d_w_down, norm1_g=d_norm1, norm2_g=d_norm2,
                 dn_norm_g=d_dn_norm, sc_norm_g=d_sc_norm, sc_conv_w=d_sc_conv, dn_conv_w=d_dn_conv,
                 alog=d_alog, dt=d_dt)
    return dx_in, grads


def _final_loss(x, g, target):
    t, d = x.shape

    def fn(tv, cv):
        xv, tg = tv

        def loss_fn(xx, gg):
            err = jnp.square(_rms_norm(xx, gg) - tg)
            return 0.5 * jnp.sum(jnp.mean(err, axis=-1))

        loss, vjp = jax.vjp(loss_fn, xv, cv[0])
        dx, dg = vjp(jnp.ones((), F32))
        return (dx,), (jnp.full((1, LANES), loss, F32), dg)

    return _tok_call("final_loss", fn, t, TT, [(x, d, 0, "cur"), (target, d, 0, "cur")], [g],
                     [(TT, d, F32)], [((1, LANES), F32), ((1, d), F32)])


def _pack_rows(arrs):
    rows, offs, r0 = [], [], 0
    for a in arrs:
        n = a.size
        nr = -(-n // LANES)
        flat = jnp.pad(a.reshape(-1).astype(F32), (0, nr * LANES - n))
        rows.append(flat.reshape(nr, LANES))
        offs.append((r0, nr, a.shape))
        r0 += nr
    pad = (-r0) % 8
    if pad:
        rows.append(jnp.zeros((pad, LANES), F32))
    return jnp.concatenate(rows, axis=0), offs


def _unpack_rows(packed, offs):
    out = []
    for r0, nr, shp in offs:
        n = 1
        for s_ in shp:
            n *= s_
        out.append(packed[r0:r0 + nr].reshape(-1)[:n].reshape(shp))
    return out


def kernel(x, norm1_g, w_in, dn_conv_w, dn_a_log, dn_dt_bias, dn_norm_g, sc_conv_w, sc_norm_g, w_out, norm2_g, ffn_w_gate, ffn_w_up, ffn_w_down, final_norm_g, loss_target, m_norm1_g, m_w_in, m_dn_conv_w, m_dn_a_log, m_dn_dt_bias, m_dn_norm_g, m_sc_conv_w, m_sc_norm_g, m_w_out, m_norm2_g, m_ffn_w_gate, m_ffn_w_up, m_ffn_w_down, m_final_norm_g, v_norm1_g, v_w_in, v_dn_conv_w, v_dn_a_log, v_dn_dt_bias, v_dn_norm_g, v_sc_conv_w, v_sc_norm_g, v_w_out, v_norm2_g, v_ffn_w_gate, v_ffn_w_up, v_ffn_w_down, v_final_norm_g):
    depth, d, cin = w_in.shape
    t = x.shape[1]
    dff_s = ffn_w_gate.shape[2]
    me = _my_index()
    x2 = x.reshape(t, d)
    tgt = loss_target.reshape(t, d)

    conv_pack, conv_offs = _pack_rows([dn_conv_w, sc_conv_w])
    (conv_all,) = _all_gather([conv_pack], "gather_conv")
    dn_conv_full, sc_conv_full = [], []
    dn_parts, sc_parts = zip(*[_unpack_rows(conv_all[j], conv_offs) for j in range(N_DEV)])
    dn_conv_full = jnp.concatenate(dn_parts, axis=2)
    sc_conv_full = jnp.concatenate(sc_parts, axis=2)

    layer_params = []
    for l in range(depth):
        shards = [a[l].astype(MXU_DTYPE) for a in (w_in, w_out, ffn_w_gate, ffn_w_up, ffn_w_down)]
        g_in, g_out, g_gate, g_up, g_down = _all_gather(shards, "gather_weights")
        full_in = g_in.transpose(1, 0, 2).reshape(d, N_DEV * cin)
        full_gate = g_gate.transpose(1, 0, 2).reshape(d, N_DEV * dff_s)
        full_up = g_up.transpose(1, 0, 2).reshape(d, N_DEV * dff_s)
        layer_params.append(dict(
            w_in=_pad_w_in(full_in), w_out=g_out.reshape(d, d),
            w_gu=jnp.concatenate([full_gate, full_up], axis=1), w_down=g_down.reshape(N_DEV * dff_s, d),
            norm1_g=norm1_g[l][None], norm2_g=norm2_g[l][None], dn_norm_g=dn_norm_g[l][None],
            sc_norm_g=sc_norm_g[l][None], dn_conv_w=dn_conv_full[l], sc_conv_w=sc_conv_full[l],
            alog_row=_lane_row(dn_a_log[l], DN_HEADS), dt_row=_lane_row(dn_dt_bias[l], DN_HEADS)))

    saved = []
    xc = x2
    for l in range(depth):
        xc, s = _layer_fwd(xc, layer_params[l])
        saved.append(s)
    dx, loss_part, d_final = _final_loss(xc, final_norm_g[None], tgt)

    grads = [None] * depth
    for l in reversed(range(depth)):
        dx, grads[l] = _layer_bwd(dx, layer_params[l], saved[l])
    grad_x = dx.reshape(x.shape)

    big = {k: [] for k in ("g", "d", "m", "v")}
    names = ("w_in", "w_out", "ffn_w_gate", "ffn_w_up", "ffn_w_down")
    big_out = {n: {k: [] for k in ("g", "d", "m", "v")} for n in names}
    w_loc = dict(w_in=w_in, w_out=w_out, ffn_w_gate=ffn_w_gate, ffn_w_up=ffn_w_up, ffn_w_down=ffn_w_down)
    m_loc = dict(w_in=m_w_in, w_out=m_w_out, ffn_w_gate=m_ffn_w_gate, ffn_w_up=m_ffn_w_up, ffn_w_down=m_ffn_w_down)
    v_loc = dict(w_in=v_w_in, w_out=v_w_out, ffn_w_gate=v_ffn_w_gate, ffn_w_up=v_ffn_w_up, ffn_w_down=v_ffn_w_down)
    for l in range(depth):
        g = grads[l]
        cols = lambda a, c: a.reshape(a.shape[0], N_DEV, c).transpose(1, 0, 2)
        parts = [cols(_unpad_w_in(g["w_in"]), cin), g["w_out"].reshape(N_DEV, d // N_DEV, d),
                 cols(g["w_gu"][:, :N_DEV * dff_s], dff_s), cols(g["w_gu"][:, N_DEV * dff_s:], dff_s),
                 g["w_down"].reshape(N_DEV, dff_s, d)]
        recv = _all_to_all(parts, "scatter_grads")
        for n, r in zip(names, recv):
            res = _sum_adamw(r, w_loc[n][l], m_loc[n][l], v_loc[n][l], "adamw_" + n)
            for k, a in zip(("g", "d", "m", "v"), res):
                big_out[n][k].append(a)
    big_out = {n: {k: jnp.stack(v_) for k, v_ in o.items()} for n, o in big_out.items()}

    stack = lambda key: jnp.stack([grads[l][key] for l in range(depth)])
    small_parts = [stack("norm1_g").reshape(depth, d), stack("norm2_g").reshape(depth, d), d_final.reshape(d),
                   stack("dn_norm_g").reshape(depth, HEAD_DIM), stack("sc_norm_g").reshape(depth, SC_WIDTH),
                   stack("alog").reshape(depth, LANES), stack("dt").reshape(depth, LANES),
                   stack("dn_conv_w"), stack("sc_conv_w"), loss_part]
    small_pack, small_offs = _pack_rows(small_parts)
    (small_all,) = _all_gather([small_pack], "gather_small")
    total = _sum_rows(small_all, "sum_small")
    (g_n1, g_n2, g_fin, g_dnn, g_scn, g_alog, g_dt, g_dnc, g_scc, loss_row) = _unpack_rows(total, small_offs)
    loss = loss_row[0, 0]
    g_alog = g_alog[:, DN_HEADS:2 * DN_HEADS]
    g_dt = g_dt[:, DN_HEADS:2 * DN_HEADS]
    dnc_w = dn_conv_w.shape[2]
    scc_w = sc_conv_w.shape[2]
    g_dnc = lax.dynamic_slice_in_dim(g_dnc, me * dnc_w, dnc_w, axis=2)
    g_scc = lax.dynamic_slice_in_dim(g_scc, me * scc_w, scc_w, axis=2)
    sm_g = [g_n1, g_dnc, g_alog, g_dt, g_dnn, g_scc, g_scn, g_n2, g_fin]
    sm_w = [norm1_g, dn_conv_w, dn_a_log, dn_dt_bias, dn_norm_g, sc_conv_w, sc_norm_g, norm2_g, final_norm_g]
    sm_m = [m_norm1_g, m_dn_conv_w, m_dn_a_log, m_dn_dt_bias, m_dn_norm_g, m_sc_conv_w, m_sc_norm_g, m_norm2_g, m_final_norm_g]
    sm_v = [v_norm1_g, v_dn_conv_w, v_dn_a_log, v_dn_dt_bias, v_dn_norm_g, v_sc_conv_w, v_sc_norm_g, v_norm2_g, v_final_norm_g]
    pg, offs = _pack_rows(sm_g)
    pw, _ = _pack_rows(sm_w)
    pm, _ = _pack_rows(sm_m)
    pv, _ = _pack_rows(sm_v)
    sg, sd, sm_, sv = _sum_adamw(pg[None], pw, pm, pv, "adamw_small")
    small_out = {k: _unpack_rows(a, offs) for k, a in zip(("g", "d", "m", "v"), (sg, sd, sm_, sv))}

    def outputs(k):
        s_ = small_out[k]
        b = big_out
        return [s_[0], b["w_in"][k], s_[1], s_[2], s_[3], s_[4], s_[5], s_[6], b["w_out"][k], s_[7],
                b["ffn_w_gate"][k], b["ffn_w_up"][k], b["ffn_w_down"][k], s_[8]]

    return (loss, grad_x, *outputs("g"), *outputs("d"), *outputs("m"), *outputs("v"))
```

```python
import functools

import jax
import jax.numpy as jnp
from jax import lax
from jax.experimental import pallas as pl
from jax.experimental.pallas import tpu as pltpu

F32 = jnp.float32
MXU_DTYPE = jnp.bfloat16
HIGHEST = lax.Precision.HIGHEST
MESH = pl.DeviceIdType.MESH

N_DEV = 8
EPS = 1e-6
DN_HEADS = 4
HEAD_DIM = 128
DN_WIDTH = DN_HEADS * HEAD_DIM
SC_WIDTH = 512
SC_GROUPS = 4
DN_CONV = 4
SC_CONV = 3
CHUNK = 64
HALO = 8
LANES = 128

QKV_W = 3 * DN_WIDTH
Z_OFF = QKV_W
SC_OFF = Z_OFF + DN_WIDTH
BA_OFF = SC_OFF + 3 * SC_WIDTH
BA_W = 256
PROJ_W = BA_OFF + BA_W

ADAM_LR = 0.001
ADAM_B1 = 0.9
ADAM_B2 = 0.999
ADAM_EPS = 1e-08
ADAM_WD = 0.01
ADAM_STEP = 10


def _pick(n, cands):
    for c in cands:
        if n % c == 0:
            return c
    return n


def _params(*sem):
    return pltpu.CompilerParams(dimension_semantics=sem)


def _rms_norm(x, g):
    return x * lax.rsqrt(jnp.mean(x * x, axis=-1, keepdims=True) + EPS) * g


def _dot(a, b, dims=(((1,), (0,)), ((), ()))):
    return lax.dot_general(a.astype(MXU_DTYPE), b.astype(MXU_DTYPE), dims, preferred_element_type=F32)


def _bmm(a, b, spec, exact=False):
    if exact:
        return jnp.einsum(spec, a, b, preferred_element_type=F32, precision=HIGHEST)
    return jnp.einsum(spec, a.astype(MXU_DTYPE), b.astype(MXU_DTYPE), preferred_element_type=F32)


def _causal_conv(cur, halo, w, k):
    tt = cur.shape[0]
    xp = jnp.concatenate([halo, cur], axis=0)
    y = None
    for j in range(k):
        start = HALO - (k - 1) + j
        term = xp[start:start + tt] * w[j:j + 1]
        y = term if y is None else y + term
    return y


def _matmul(a, b, mode, out_dtype, name, residual=None):
    if mode == "nn":
        (m, k), (k2, n) = a.shape, b.shape
    elif mode == "nt":
        (m, k), (n, k2) = a.shape, b.shape
    else:
        (k, m), (k2, n) = a.shape, b.shape
    assert k == k2
    tm = _pick(m, (1024, 1408, 512, 256, 128))
    tn = _pick(n, (1280, 1408, 1024, 512, 256, 128))
    tk = k if k <= 2816 else _pick(k, (1024, 768, 512))
    nk = k // tk
    dims = {"nn": (((1,), (0,)), ((), ())), "nt": (((1,), (1,)), ((), ())), "tn": (((0,), (0,)), ((), ()))}[mode]
    a_spec = {"nn": pl.BlockSpec((tm, tk), lambda i, j, q: (i, q)),
              "nt": pl.BlockSpec((tm, tk), lambda i, j, q: (i, q)),
              "tn": pl.BlockSpec((tk, tm), lambda i, j, q: (q, i))}[mode]
    b_spec = {"nn": pl.BlockSpec((tk, tn), lambda i, j, q: (q, j)),
              "nt": pl.BlockSpec((tn, tk), lambda i, j, q: (j, q)),
              "tn": pl.BlockSpec((tk, tn), lambda i, j, q: (q, j))}[mode]
    o_spec = pl.BlockSpec((tm, tn), lambda i, j, q: (i, j))
    has_res = residual is not None

    def body(*refs):
        if has_res:
            a_ref, b_ref, r_ref, o_ref, acc = refs
        else:
            a_ref, b_ref, o_ref, acc = refs
        q = pl.program_id(2)

        @pl.when(q == 0)
        def _():
            acc[...] = jnp.zeros_like(acc)

        acc[...] += _dot(a_ref[...], b_ref[...], dims)

        @pl.when(q == nk - 1)
        def _():
            r = acc[...]
            if has_res:
                r = r + r_ref[...]
            o_ref[...] = r.astype(o_ref.dtype)

    in_specs = [a_spec, b_spec] + ([o_spec] if has_res else [])
    args = (a, b) + ((residual,) if has_res else ())
    return pl.pallas_call(
        body, name=name, grid=(m // tm, n // tn, nk), in_specs=in_specs, out_specs=o_spec,
        out_shape=jax.ShapeDtypeStruct((m, n), out_dtype),
        scratch_shapes=[pltpu.VMEM((tm, tn), F32)],
        compiler_params=_params("parallel", "parallel", "arbitrary"),
    )(*args)


def _norm_matmul(x, g, w, name):
    t, d = x.shape
    n = w.shape[1]
    tm = _pick(t, (1024, 512, 256, 128))
    tn = _pick(n, (1280, 1408, 1024, 512, 256, 128))

    def body(x_ref, g_ref, w_ref, h_ref, y_ref, h_scr):
        @pl.when(pl.program_id(1) == 0)
        def _():
            h = _rms_norm(x_ref[...], g_ref[...]).astype(MXU_DTYPE)
            h_scr[...] = h
            h_ref[...] = h

        y_ref[...] = _dot(h_scr[...], w_ref[...])

    return pl.pallas_call(
        body, name=name, grid=(t // tm, n // tn),
        in_specs=[pl.BlockSpec((tm, d), lambda i, j: (i, 0)), pl.BlockSpec((1, d), lambda i, j: (0, 0)),
                  pl.BlockSpec((d, tn), lambda i, j: (0, j))],
        out_specs=[pl.BlockSpec((tm, d), lambda i, j: (i, 0)), pl.BlockSpec((tm, tn), lambda i, j: (i, j))],
        out_shape=[jax.ShapeDtypeStruct((t, d), MXU_DTYPE), jax.ShapeDtypeStruct((t, n), F32)],
        scratch_shapes=[pltpu.VMEM((tm, d), MXU_DTYPE)],
        compiler_params=_params("parallel", "arbitrary"),
    )(x, g, w)


def _tok_call(name, fn, t, tt, tok_in, const_in, tok_out, acc_out):
    nblk = t // tt
    hb = tt // HALO
    in_specs, args = [], []
    for arr, w, cb, mode in tok_in:
        if mode == "cur":
            spec = pl.BlockSpec((tt, w), lambda i, cb=cb: (i, cb))
        elif mode == "prev":
            spec = pl.BlockSpec((HALO, w), lambda i, cb=cb: (jnp.maximum(i * hb - 1, 0), cb))
        else:
            spec = pl.BlockSpec((HALO, w), lambda i, cb=cb: (jnp.minimum(i + 1, nblk - 1), cb))
        in_specs.append(spec)
        args.append(arr)
    for arr in const_in:
        in_specs.append(pl.BlockSpec(arr.shape, lambda i: (0, 0)))
        args.append(arr)
    out_specs, out_shape = [], []
    for rows, w, dt in tok_out:
        out_specs.append(pl.BlockSpec((rows, w), lambda i: (i, 0)))
        out_shape.append(jax.ShapeDtypeStruct((nblk * rows, w), dt))
    for shp, dt in acc_out:
        out_specs.append(pl.BlockSpec(shp, lambda i: (0, 0)))
        out_shape.append(jax.ShapeDtypeStruct(shp, dt))
    n_tok, n_const, n_out = len(tok_in), len(const_in), len(tok_out)

    def body(*refs):
        i = pl.program_id(0)
        tok_vals = []
        for (_, _, _, mode), r in zip(tok_in, refs[:n_tok]):
            v = r[...]
            if mode == "prev":
                v = jnp.where(i > 0, v, jnp.zeros_like(v))
            elif mode == "next8":
                v = jnp.where(i < nblk - 1, v, jnp.zeros_like(v))
            tok_vals.append(v)
        const_vals = [r[...] for r in refs[n_tok:n_tok + n_const]]
        outs, accs = fn(tok_vals, const_vals)
        o_refs = refs[n_tok + n_const:n_tok + n_const + n_out]
        a_refs = refs[n_tok + n_const + n_out:]
        for r, v in zip(o_refs, outs):
            r[...] = v.astype(r.dtype)
        if a_refs:
            @pl.when(i == 0)
            def _():
                for r in a_refs:
                    r[...] = jnp.zeros_like(r)

            for r, v in zip(a_refs, accs):
                r[...] += v.astype(r.dtype)

    res = pl.pallas_call(
        body, name=name, grid=(nblk,), in_specs=in_specs, out_specs=out_specs, out_shape=out_shape,
        compiler_params=_params("arbitrary" if acc_out else "parallel"),
    )(*args)
    return res


def _dn_pre_math(cur, halo, ba, cw, alog, dtb):
    tt = cur.shape[0]
    a = jax.nn.silu(_causal_conv(cur, halo, cw, DN_CONV))
    pieces = []
    for p in range(2 * DN_HEADS):
        xh = a[:, p * HEAD_DIM:(p + 1) * HEAD_DIM]
        xh = xh * lax.rsqrt(jnp.sum(xh * xh, axis=-1, keepdims=True) + EPS)
        if p < DN_HEADS:
            xh = xh * (HEAD_DIM ** -0.5)
        pieces.append(xh)
    pieces.append(a[:, 2 * DN_WIDTH:])
    qkvn = jnp.concatenate(pieces, axis=1)
    lane = lax.broadcasted_iota(jnp.int32, ba.shape, 1)
    raw = jnp.where(lane < DN_HEADS, jax.nn.sigmoid(ba), -jnp.exp(alog) * jax.nn.softplus(ba + dtb))
    r = lax.broadcasted_iota(jnp.int32, (tt, tt), 0)
    c = lax.broadcasted_iota(jnp.int32, (tt, tt), 1)
    tri = jnp.where((r // CHUNK == c // CHUNK) & (c <= r), 1.0, 0.0).astype(F32)
    cums = jnp.dot(tri, raw, preferred_element_type=F32, precision=HIGHEST)
    bg = jnp.where(lane < DN_HEADS, raw, cums)
    return qkvn, bg


def _mix_math(o, z, gb, gc, gc_halo, hv, hv_halo, dng, scg, scw):
    outs = []
    for h in range(DN_HEADS):
        sl = slice(h * HEAD_DIM, (h + 1) * HEAD_DIM)
        oh = o[:, sl]
        outs.append(oh * lax.rsqrt(jnp.mean(oh * oh, axis=-1, keepdims=True) + EPS) * dng * jax.nn.silu(z[:, sl]))
    y = gb * _causal_conv(gc * hv, gc_halo * hv_halo, scw, SC_CONV)
    gw = SC_WIDTH // SC_GROUPS
    for g in range(SC_GROUPS):
        sl = slice(g * gw, (g + 1) * gw)
        yg = y[:, sl]
        outs.append(yg * lax.rsqrt(jnp.mean(yg * yg, axis=-1, keepdims=True) + EPS) * scg[:, sl])
    return jnp.concatenate(outs, axis=1)


def _swiglu_math(g, u):
    return jax.nn.silu(g) * u


def _tri_inverse(a):
    c = a.shape[-1]
    r = lax.broadcasted_iota(jnp.int32, (c, c), 0)
    q = lax.broadcasted_iota(jnp.int32, (c, c), 1)
    eye = jnp.where(r == q, 1.0, 0.0).astype(F32)[None]
    blk = (r // 16 == q // 16)[None]
    d = jnp.where(blk, a, 0.0)
    o = a - d
    mm = functools.partial(_bmm, spec="bij,bjk->bik", exact=True)
    n = -d
    p = eye + n
    for _ in range(3):
        n = mm(n, n)
        p = p + mm(p, n)
    e = mm(p, o)
    e2 = mm(e, e)
    left = eye - e + e2 - mm(e, e2)
    return mm(left, p)


@jax.custom_vjp
def _inverse_known(a, tinv):
    return tinv


def _inverse_known_fwd(a, tinv):
    return tinv, tinv


def _inverse_known_bwd(tinv, ct):
    left = _bmm(tinv, ct, "bji,bjk->bik", exact=True)
    return -_bmm(left, tinv, "bik,bjk->bij", exact=True), jnp.zeros_like(tinv)


_inverse_known.defvjp(_inverse_known_fwd, _inverse_known_bwd)


def _delta_intra_math(q, k, v, bg, head, tinv_known=None):
    n = q.shape[0]
    nb = n // CHUNK
    lane = lax.broadcasted_iota(jnp.int32, bg.shape, 1)
    beta = jnp.sum(jnp.where(lane == head, bg, 0.0), axis=1, keepdims=True).reshape(nb, CHUNK, 1)
    gc = jnp.sum(jnp.where(lane == head + DN_HEADS, bg, 0.0), axis=1, keepdims=True).reshape(nb, CHUNK, 1)
    q3, k3, v3 = (a.reshape(nb, CHUNK, HEAD_DIM) for a in (q, k, v))
    r = lax.broadcasted_iota(jnp.int32, (CHUNK, CHUNK), 0)
    c = lax.broadcasted_iota(jnp.int32, (CHUNK, CHUNK), 1)
    eye = jnp.where(r == c, 1.0, 0.0).astype(F32)[None]
    gcr = _bmm(jnp.ones((nb, CHUNK, CHUNK), F32), gc * eye, "bik,bkj->bij", exact=True)
    decay = jnp.exp(jnp.where((r >= c)[None], gc - gcr, -1e30))
    kb = k3 * beta
    vb = v3 * beta
    egc = jnp.exp(gc)
    a = jnp.where((r > c)[None], _bmm(kb, k3, "bcd,bmd->bcm") * decay, 0.0)
    tinv = _tri_inverse(a) if tinv_known is None else _inverse_known(a, tinv_known)
    u = _bmm(tinv, vb, "bcm,bmd->bcd", exact=True)
    w = _bmm(tinv, kb * egc, "bcm,bmd->bcd", exact=True)
    qk = _bmm(q3, k3, "bcd,bmd->bcm") * decay
    row = lax.broadcasted_iota(jnp.int32, (nb, CHUNK, 1), 1)
    glast = jnp.sum(jnp.where(row == CHUNK - 1, gc, 0.0), axis=1, keepdims=True)
    qd = q3 * egc
    kd = k3 * jnp.exp(glast - gc)
    glb = jnp.broadcast_to(jnp.exp(glast), (nb, HALO, LANES))
    flat = lambda x: x.reshape(n, HEAD_DIM)
    return flat(u), flat(w), flat(qd), flat(kd), qk, glb, tinv


def _delta_step_math(u, w, qd, kd, qk, gl, s):
    vnew = u - _dot(w, s)
    o = _dot(qd, s) + _dot(qk, vnew)
    s2 = s * gl + _dot(kd, vnew, (((0,), (0,)), ((), ())))
    return o, s2


def _delta_intra(qkvn, bg, nb):
    t = qkvn.shape[0]
    n = t // CHUNK
    rows = nb * CHUNK

    def body(q_ref, k_ref, v_ref, bg_ref, u_ref, w_ref, qd_ref, kd_ref, qk_ref, gl_ref, ti_ref):
        outs = _delta_intra_math(q_ref[...], k_ref[...], v_ref[...], bg_ref[...], pl.program_id(1))
        for r, v in zip((u_ref, w_ref, qd_ref, kd_ref, qk_ref, gl_ref, ti_ref), outs):
            r[...] = v

    col = lambda off: pl.BlockSpec((rows, HEAD_DIM), lambda b, h, off=off: (b, off + h))
    tok = jax.ShapeDtypeStruct((t, DN_WIDTH), F32)
    return pl.pallas_call(
        body, name="delta_intra", grid=(n // nb, DN_HEADS),
        in_specs=[col(0), col(DN_HEADS), col(2 * DN_HEADS), pl.BlockSpec((rows, LANES), lambda b, h: (b, 0))],
        out_specs=[col(0)] * 4 + [pl.BlockSpec((nb, None, CHUNK, CHUNK), lambda b, h: (b, h, 0, 0)),
                                  pl.BlockSpec((nb, None, HALO, LANES), lambda b, h: (b, h, 0, 0)),
                                  pl.BlockSpec((nb, None, CHUNK, CHUNK), lambda b, h: (b, h, 0, 0))],
        out_shape=[tok] * 4 + [jax.ShapeDtypeStruct((n, DN_HEADS, CHUNK, CHUNK), F32),
                               jax.ShapeDtypeStruct((n, DN_HEADS, HALO, LANES), F32),
                               jax.ShapeDtypeStruct((n, DN_HEADS, CHUNK, CHUNK), F32)],
        compiler_params=_params("parallel", "arbitrary"),
    )(qkvn, qkvn, qkvn, bg)


def _delta_intra_bwd(qkvn, bg, tinv, cts, nb):
    t = qkvn.shape[0]
    n = t // CHUNK
    rows = nb * CHUNK

    def body(q_ref, k_ref, v_ref, bg_ref, ti_ref, du, dw, dqd, dkd, dqk, dgl, dq_ref, dk_ref, dv_ref, dbg_ref):
        h = pl.program_id(1)
        ti = ti_ref[...]
        _, vjp = jax.vjp(lambda q, k, v, b: _delta_intra_math(q, k, v, b, h, ti)[:6],
                         q_ref[...], k_ref[...], v_ref[...], bg_ref[...])
        dq, dk, dv, dbg = vjp((du[...], dw[...], dqd[...], dkd[...], dqk[...], dgl[...]))
        dq_ref[...] = dq
        dk_ref[...] = dk
        dv_ref[...] = dv

        @pl.when(h == 0)
        def _():
            dbg_ref[...] = jnp.zeros_like(dbg_ref)

        dbg_ref[...] += dbg

    col = lambda off: pl.BlockSpec((rows, HEAD_DIM), lambda b, h, off=off: (b, off + h))
    bgs = pl.BlockSpec((rows, LANES), lambda b, h: (b, 0))
    qks = pl.BlockSpec((nb, None, CHUNK, CHUNK), lambda b, h: (b, h, 0, 0))
    gls = pl.BlockSpec((nb, None, HALO, LANES), lambda b, h: (b, h, 0, 0))
    tok = jax.ShapeDtypeStruct((t, DN_WIDTH), F32)
    return pl.pallas_call(
        body, name="delta_intra_bwd", grid=(n // nb, DN_HEADS),
        in_specs=[col(0), col(DN_HEADS), col(2 * DN_HEADS), bgs, qks, col(0), col(0), col(0), col(0), qks, gls],
        out_specs=[col(0), col(0), col(0), bgs],
        out_shape=[tok, tok, tok, jax.ShapeDtypeStruct((t, LANES), F32)],
        compiler_params=_params("parallel", "arbitrary"),
    )(qkvn, qkvn, qkvn, bg, tinv, *cts)


def _delta_scan(u, w, qd, kd, qk, glb, cb):
    t = u.shape[0]
    n = t // CHUNK
    rows = cb * CHUNK

    def body(u_ref, w_ref, qd_ref, kd_ref, qk_ref, gl_ref, o_ref, s_ref, s_scr):
        @pl.when(pl.program_id(0) == 0)
        def _():
            s_scr[...] = jnp.zeros_like(s_scr)

        def chunk(c, carry):
            r0 = pl.multiple_of(c * CHUNK, CHUNK)
            for h in range(DN_HEADS):
                sl = (pl.ds(r0, CHUNK), slice(h * HEAD_DIM, (h + 1) * HEAD_DIM))
                s = s_scr[h]
                s_ref[c, h] = s
                o, s2 = _delta_step_math(u_ref[sl], w_ref[sl], qd_ref[sl], kd_ref[sl], qk_ref[c, h],
                                         gl_ref[c, h][0:1, :], s)
                o_ref[sl] = o
                s_scr[h] = s2
            return carry

        lax.fori_loop(0, cb, chunk, 0)

    tok = pl.BlockSpec((rows, DN_WIDTH), lambda i: (i, 0))
    return pl.pallas_call(
        body, name="delta_scan", grid=(n // cb,),
        in_specs=[tok] * 4 + [pl.BlockSpec((cb, DN_HEADS, CHUNK, CHUNK), lambda i: (i, 0, 0, 0)),
                              pl.BlockSpec((cb, DN_HEADS, HALO, LANES), lambda i: (i, 0, 0, 0))],
        out_specs=[tok, pl.BlockSpec((cb, DN_HEADS, HEAD_DIM, HEAD_DIM), lambda i: (i, 0, 0, 0))],
        out_shape=[jax.ShapeDtypeStruct((t, DN_WIDTH), F32),
                   jax.ShapeDtypeStruct((n, DN_HEADS, HEAD_DIM, HEAD_DIM), F32)],
        scratch_shapes=[pltpu.VMEM((DN_HEADS, HEAD_DIM, HEAD_DIM), F32)],
        compiler_params=_params("arbitrary"),
    )(u, w, qd, kd, qk, glb)


def _delta_scan_bwd(u, w, qd, kd, qk, glb, s_all, do, cb):
    t = u.shape[0]
    n = t // CHUNK
    nblk = n // cb
    rows = cb * CHUNK

    def body(u_ref, w_ref, qd_ref, kd_ref, qk_ref, gl_ref, s_ref, do_ref,
             du_ref, dw_ref, dqd_ref, dkd_ref, dqk_ref, dgl_ref, ds_scr):
        @pl.when(pl.program_id(0) == 0)
        def _():
            ds_scr[...] = jnp.zeros_like(ds_scr)

        def chunk(step, carry):
            c = cb - 1 - step
            r0 = pl.multiple_of(c * CHUNK, CHUNK)
            for h in range(DN_HEADS):
                sl = (pl.ds(r0, CHUNK), slice(h * HEAD_DIM, (h + 1) * HEAD_DIM))
                gl_tile = gl_ref[c, h]
                prim = (u_ref[sl], w_ref[sl], qd_ref[sl], kd_ref[sl], qk_ref[c, h], gl_tile, s_ref[c, h])
                _, vjp = jax.vjp(lambda a, b, cc, d, e, g, s: _delta_step_math(a, b, cc, d, e, g[0:1, :], s), *prim)
                du, dw, dqd, dkd, dqk, dgl, ds = vjp((do_ref[sl], ds_scr[h]))
                du_ref[sl] = du
                dw_ref[sl] = dw
                dqd_ref[sl] = dqd
                dkd_ref[sl] = dkd
                dqk_ref[c, h] = dqk
                dgl_ref[c, h] = dgl
                ds_scr[h] = ds
            return carry

        lax.fori_loop(0, cb, chunk, 0)

    rev = lambda i: nblk - 1 - i
    tok = pl.BlockSpec((rows, DN_WIDTH), lambda i: (rev(i), 0))
    qks = pl.BlockSpec((cb, DN_HEADS, CHUNK, CHUNK), lambda i: (rev(i), 0, 0, 0))
    gls = pl.BlockSpec((cb, DN_HEADS, HALO, LANES), lambda i: (rev(i), 0, 0, 0))
    ss = pl.BlockSpec((cb, DN_HEADS, HEAD_DIM, HEAD_DIM), lambda i: (rev(i), 0, 0, 0))
    tshape = jax.ShapeDtypeStruct((t, DN_WIDTH), F32)
    return pl.pallas_call(
        body, name="delta_scan_bwd", grid=(nblk,),
        in_specs=[tok] * 4 + [qks, gls, ss, tok],
        out_specs=[tok] * 4 + [qks, gls],
        out_shape=[tshape] * 4 + [jax.ShapeDtypeStruct(qk.shape, F32), jax.ShapeDtypeStruct(glb.shape, F32)],
        scratch_shapes=[pltpu.VMEM((DN_HEADS, HEAD_DIM, HEAD_DIM), F32)],
        compiler_params=_params("arbitrary"),
    )(u, w, qd, kd, qk, glb, s_all, do)


def _peer(mask):
    x, y, c = lax.axis_index("x"), lax.axis_index("y"), lax.axis_index("c")
    return (x ^ ((mask >> 2) & 1), y ^ ((mask >> 1) & 1), c ^ (mask & 1))


def _my_index():
    return 4 * lax.axis_index("x") + 2 * lax.axis_index("y") + lax.axis_index("c")


def _all_gather(shards, name):
    na = len(shards)

    def body(*refs):
        ins, outs = refs[:na], refs[na:2 * na]
        send_sems, recv_sems, local_sems = refs[2 * na:]
        me = _my_index()
        copies = []
        for a in range(na):
            loc = pltpu.make_async_copy(ins[a], outs[a].at[me], local_sems.at[a])
            loc.start()
            copies.append(loc)

        def rc(a, k, block, to, src=None):
            dst = outs[a].at[block]
            return pltpu.make_async_remote_copy(src_ref=dst if src is None else src, dst_ref=dst,
                                                send_sem=send_sems.at[a, k], recv_sem=recv_sems.at[a, k],
                                                device_id=to, device_id_type=MESH)

        sib = _peer(1)
        chip_masks = (4, 2, 6)
        first = []
        for a in range(na):
            first.append(rc(a, 0, me, sib, src=ins[a]))
            for j, m in enumerate(chip_masks):
                first.append(rc(a, 1 + j, me, _peer(m), src=ins[a]))
        for cp in first:
            cp.start()
        passed = []
        for j, m in enumerate(chip_masks):
            for a in range(na):
                rc(a, 1 + j, me ^ m, _peer(0)).wait_recv()
                p = rc(a, 4 + j, me ^ m, sib)
                p.start()
                passed.append(p)
        for a in range(na):
            rc(a, 0, me ^ 1, _peer(0)).wait_recv()
            for j, m in enumerate(chip_masks):
                rc(a, 4 + j, me ^ m ^ 1, _peer(0)).wait_recv()
        for cp in first + passed:
            cp.wait_send()
        for cp in copies:
            cp.wait()

    hbm = pl.BlockSpec(memory_space=pltpu.HBM)
    return pl.pallas_call(
        body, name=name, in_specs=[hbm] * na, out_specs=[hbm] * na,
        out_shape=[jax.ShapeDtypeStruct((N_DEV,) + s.shape, s.dtype) for s in shards],
        scratch_shapes=[pltpu.SemaphoreType.DMA((na, 7)), pltpu.SemaphoreType.DMA((na, 7)),
                        pltpu.SemaphoreType.DMA((na,))],
        compiler_params=pltpu.CompilerParams(has_side_effects=True),
    )(*shards)


def _all_to_all(parts, name):
    na = len(parts)

    def body(*refs):
        ins, outs = refs[:na], refs[na:2 * na]
        send_sems, recv_sems, local_sems = refs[2 * na:]
        me = _my_index()
        started = []
        for a in range(na):
            loc = pltpu.make_async_copy(ins[a].at[me], outs[a].at[me], local_sems.at[a])
            loc.start()
            started.append(loc)
        remote = []
        for m in range(1, N_DEV):
            for a in range(na):
                cp = pltpu.make_async_remote_copy(src_ref=ins[a].at[me ^ m], dst_ref=outs[a].at[me],
                                                  send_sem=send_sems.at[a, m - 1], recv_sem=recv_sems.at[a, m - 1],
                                                  device_id=_peer(m), device_id_type=MESH)
                cp.start()
                remote.append(cp)
        for cp in remote:
            cp.wait()
        for cp in started:
            cp.wait()

    hbm = pl.BlockSpec(memory_space=pltpu.HBM)
    return pl.pallas_call(
        body, name=name, in_specs=[hbm] * na, out_specs=[hbm] * na,
        out_shape=[jax.ShapeDtypeStruct(p.shape, p.dtype) for p in parts],
        scratch_shapes=[pltpu.SemaphoreType.DMA((na, 7)), pltpu.SemaphoreType.DMA((na, 7)),
                        pltpu.SemaphoreType.DMA((na,))],
        compiler_params=pltpu.CompilerParams(has_side_effects=True),
    )(*parts)


def _adamw_math(w, g, m, v):
    m2 = ADAM_B1 * m + (1.0 - ADAM_B1) * g
    v2 = ADAM_B2 * v + (1.0 - ADAM_B2) * jnp.square(g)
    m_hat = m2 / (1.0 - ADAM_B1 ** ADAM_STEP)
    v_hat = v2 / (1.0 - ADAM_B2 ** ADAM_STEP)
    delta = -ADAM_LR * (m_hat / (jnp.sqrt(v_hat) + ADAM_EPS) + ADAM_WD * w)
    return delta, m2, v2


def _sum_adamw(parts, w, m, v, name):
    r, c = w.shape
    tr = _pick(r, (512, 256, 352, 128, 64, 32, 16, 8))
    np_ = parts.shape[0]

    def body(p_ref, w_ref, m_ref, v_ref, g_ref, d_ref, m2_ref, v2_ref):
        g = p_ref[0].astype(F32)
        for d in range(1, np_):
            g = g + p_ref[d].astype(F32)
        delta, m2, v2 = _adamw_math(w_ref[...], g, m_ref[...], v_ref[...])
        g_ref[...] = g
        d_ref[...] = delta
        m2_ref[...] = m2
        v2_ref[...] = v2

    blk = pl.BlockSpec((tr, c), lambda i: (i, 0))
    shp = jax.ShapeDtypeStruct((r, c), F32)
    return pl.pallas_call(
        body, name=name, grid=(r // tr,),
        in_specs=[pl.BlockSpec((np_, tr, c), lambda i: (0, i, 0)), blk, blk, blk],
        out_specs=[blk] * 4, out_shape=[shp] * 4, compiler_params=_params("parallel"),
    )(parts, w, m, v)


def _sum_rows(parts, name):
    np_, r, c = parts.shape

    def body(p_ref, o_ref):
        g = p_ref[0]
        for d in range(1, np_):
            g = g + p_ref[d]
        o_ref[...] = g

    return pl.pallas_call(body, name=name, out_shape=jax.ShapeDtypeStruct((r, c), F32))(parts)


def _pad_w_in(w):
    d = w.shape[0]
    n_ba = 2 * DN_HEADS
    a = w[:, :SC_OFF]
    ba = w[:, SC_OFF:SC_OFF + n_ba]
    sc = w[:, SC_OFF + n_ba:]
    return jnp.concatenate([a, sc, ba, jnp.zeros((d, BA_W - n_ba), w.dtype)], axis=1)


def _unpad_w_in(wp):
    n_ba = 2 * DN_HEADS
    return jnp.concatenate([wp[:, :SC_OFF], wp[:, BA_OFF:BA_OFF + n_ba], wp[:, SC_OFF:BA_OFF]], axis=1)


def _lane_row(v, off):
    return jnp.pad(v.astype(F32), (off, LANES - off - v.shape[0]))[None]


TT = 256
NB_INTRA = 8
CB_SCAN = 8


def _layer_fwd(x, p):
    t, d = x.shape
    h, proj = _norm_matmul(x, p["norm1_g"], p["w_in"], "proj_fwd")
    qkvn, bg = _tok_call(
        "dn_pre", lambda tv, cv: (_dn_pre_math(*tv, *cv), ()), t, TT,
        [(proj, QKV_W, 0, "cur"), (proj, QKV_W, 0, "prev"), (proj, LANES, BA_OFF // LANES, "cur")],
        [p["dn_conv_w"], p["alog_row"], p["dt_row"]],
        [(TT, QKV_W, F32), (TT, LANES, F32)], [])
    u, w, qd, kd, qk, glb, tinv = _delta_intra(qkvn, bg, NB_INTRA)
    o, s_all = _delta_scan(u, w, qd, kd, qk, glb, CB_SCAN)
    cb0 = SC_OFF // SC_WIDTH
    mix_in = [(o, DN_WIDTH, 0, "cur"), (proj, DN_WIDTH, Z_OFF // DN_WIDTH, "cur"),
              (proj, SC_WIDTH, cb0, "cur"), (proj, SC_WIDTH, cb0 + 1, "cur"), (proj, SC_WIDTH, cb0 + 1, "prev"),
              (proj, SC_WIDTH, cb0 + 2, "cur"), (proj, SC_WIDTH, cb0 + 2, "prev")]
    mix_const = [p["dn_norm_g"], p["sc_norm_g"], p["sc_conv_w"]]
    (cat,) = _tok_call("mix_post", lambda tv, cv: ((_mix_math(*tv, *cv),), ()), t, TT,
                       mix_in, mix_const, [(TT, 2 * DN_WIDTH, MXU_DTYPE)], [])
    x_mid = _matmul(cat, p["w_out"], "nn", F32, "out_proj", residual=x)
    h2, gu = _norm_matmul(x_mid, p["norm2_g"], p["w_gu"], "ffn_up")
    dff = gu.shape[1] // 2
    (act,) = _tok_call("swiglu", lambda tv, cv: ((_swiglu_math(*tv),), ()), t, TT,
                       [(gu, dff, 0, "cur"), (gu, dff, 1, "cur")], [], [(TT, dff, MXU_DTYPE)], [])
    x_out = _matmul(act, p["w_down"], "nn", F32, "ffn_down", residual=x_mid)
    saved = dict(x=x, h=h, proj=proj, qkvn=qkvn, bg=bg, u=u, w=w, qd=qd, kd=kd, qk=qk, glb=glb, tinv=tinv, s_all=s_all, o=o,
                 cat=cat, x_mid=x_mid, h2=h2, gu=gu, act=act, mix_in=mix_in, mix_const=mix_const)
    return x_out, saved


def _norm_bwd(x, g, dh, dres, name):
    t, d = x.shape

    def fn(tv, cv):
        xv, dhv, drv = tv
        _, vjp = jax.vjp(_rms_norm, xv, cv[0])
        dx, dg = vjp(dhv)
        return (drv + dx,), (dg,)

    return _tok_call(name, fn, t, TT, [(x, d, 0, "cur"), (dh, d, 0, "cur"), (dres, d, 0, "cur")], [g],
                     [(TT, d, F32)], [((1, d), F32)])


def _layer_bwd(dx_out, p, s):
    t, d = dx_out.shape
    nblk = t // TT
    dact = _matmul(dx_out, p["w_down"], "nt", F32, "ffn_down_dx")
    d_w_down = _matmul(s["act"], dx_out, "tn", F32, "ffn_down_dw")
    gu = s["gu"]
    dff = gu.shape[1] // 2

    def swiglu_bwd(tv, cv):
        g, u, da = tv
        _, vjp = jax.vjp(_swiglu_math, g, u)
        dg, du = vjp(da)
        return (jnp.concatenate([dg, du], axis=1),), ()

    (dgu,) = _tok_call("swiglu_bwd", swiglu_bwd, t, TT,
                       [(gu, dff, 0, "cur"), (gu, dff, 1, "cur"), (dact, dff, 0, "cur")], [],
                       [(TT, 2 * dff, MXU_DTYPE)], [])
    dh2 = _matmul(dgu, p["w_gu"], "nt", F32, "ffn_up_dx")
    d_w_gu = _matmul(s["h2"], dgu, "tn", F32, "ffn_up_dw")
    dx_mid, d_norm2 = _norm_bwd(s["x_mid"], p["norm2_g"], dh2, dx_out, "norm2_bwd")
    dcat = _matmul(dx_mid, p["w_out"], "nt", F32, "out_proj_dx")
    d_w_out = _matmul(s["cat"], dx_mid, "tn", F32, "out_proj_dw")

    def mix_bwd(tv, cv):
        prim = tuple(tv[:7]) + tuple(cv)
        _, vjp = jax.vjp(_mix_math, *prim)
        do, dz, dgb, dgc, dgch, dhv, dhvh, ddng, dscg, dscw = vjp(tv[7])
        return (do, dz, dgb, dgc, dgch, dhv, dhvh), (ddng, dscg, dscw)

    wide = (TT, DN_WIDTH, F32)
    halo = (HALO, SC_WIDTH, F32)
    do, dz, dgb, dgc, dgc_h, dhv, dhv_h, d_dn_norm, d_sc_norm, d_sc_conv = _tok_call(
        "mix_post_bwd", mix_bwd, t, TT, s["mix_in"] + [(dcat, 2 * DN_WIDTH, 0, "cur")], s["mix_const"],
        [wide, wide, wide, wide, halo, wide, halo],
        [((1, HEAD_DIM), F32), ((1, SC_WIDTH), F32), ((SC_CONV, SC_WIDTH), F32)])
    cts = _delta_scan_bwd(s["u"], s["w"], s["qd"], s["kd"], s["qk"], s["glb"], s["s_all"], do, CB_SCAN)
    dq, dk, dv, dbg = _delta_intra_bwd(s["qkvn"], s["bg"], s["tinv"], cts, NB_INTRA)
    proj = s["proj"]

    def dn_pre_bwd(tv, cv):
        cur, hal, ba, dq_, dk_, dv_, dbg_ = tv
        _, vjp = jax.vjp(_dn_pre_math, cur, hal, ba, *cv)
        dcur, dhal, dba, dcw, dal, ddt = vjp((jnp.concatenate([dq_, dk_, dv_], axis=1), dbg_))
        return (dcur, dhal, dba), (dcw, dal, ddt)

    dqkv, dqkv_h, dba, d_dn_conv, d_alog, d_dt = _tok_call(
        "dn_pre_bwd", dn_pre_bwd, t, TT,
        [(proj, QKV_W, 0, "cur"), (proj, QKV_W, 0, "prev"), (proj, LANES, BA_OFF // LANES, "cur"),
         (dq, DN_WIDTH, 0, "cur"), (dk, DN_WIDTH, 0, "cur"), (dv, DN_WIDTH, 0, "cur"), (dbg, LANES, 0, "cur")],
        [p["dn_conv_w"], p["alog_row"], p["dt_row"]],
        [(TT, QKV_W, F32), (HALO, QKV_W, F32), (TT, LANES, F32)],
        [((DN_CONV, QKV_W), F32), ((1, LANES), F32), ((1, LANES), F32)])

    def assemble(tv, cv):
        dqkv_, dqkv_n, dz_, dgb_, dgc_, dgc_n, dhv_, dhv_n, dba_ = tv

        def with_halo(cur, nxt):
            return cur + jnp.concatenate([jnp.zeros((TT - HALO, cur.shape[1]), F32), nxt], axis=0)

        out = jnp.concatenate([with_halo(dqkv_, dqkv_n), dz_, dgb_, with_halo(dgc_, dgc_n), with_halo(dhv_, dhv_n),
                               dba_, jnp.zeros((TT, BA_W - LANES), F32)], axis=1)
        return (out,), ()

    (dproj,) = _tok_call(
        "dproj_assemble", assemble, t, TT,
        [(dqkv, QKV_W, 0, "cur"), (dqkv_h, QKV_W, 0, "next8"), (dz, DN_WIDTH, 0, "cur"), (dgb, SC_WIDTH, 0, "cur"),
         (dgc, SC_WIDTH, 0, "cur"), (dgc_h, SC_WIDTH, 0, "next8"), (dhv, SC_WIDTH, 0, "cur"),
         (dhv_h, SC_WIDTH, 0, "next8"), (dba, LANES, 0, "cur")], [],
        [(TT, PROJ_W, MXU_DTYPE)], [])
    dh = _matmul(dproj, p["w_in"], "nt", F32, "proj_dx")
    d_w_in = _matmul(s["h"], dproj, "tn", F32, "proj_dw")
    dx_in, d_norm1 = _norm_bwd(s["x"], p["norm1_g"], dh, dx_mid, "norm1_bwd")
    grads = dict(w_in=d_w_in, w_out=d_w_out, w_gu=d_w_gu, w_down=d_w_down, norm1_g=d_norm1, norm2_g=d_norm2,
                 dn_norm_g=d_dn_norm, sc_norm_g=d_sc_norm, sc_conv_w=d_sc_conv, dn_conv_w=d_dn_conv,
                 alog=d_alog, dt=d_dt)
    return dx_in, grads


def _final_loss(x, g, target):
    t, d = x.shape

    def fn(tv, cv):
        xv, tg = tv

        def loss_fn(xx, gg):
            err = jnp.square(_rms_norm(xx, gg) - tg)
            return 0.5 * jnp.sum(jnp.mean(err, axis=-1))

        loss, vjp = jax.vjp(loss_fn, xv, cv[0])
        dx, dg = vjp(jnp.ones((), F32))
        return (dx,), (jnp.full((1, LANES), loss, F32), dg)

    return _tok_call("final_loss", fn, t, TT, [(x, d, 0, "cur"), (target, d, 0, "cur")], [g],
                     [(TT, d, F32)], [((1, LANES), F32), ((1, d), F32)])


def _pack_rows(arrs):
    rows, offs, r0 = [], [], 0
    for a in arrs:
        n = a.size
        nr = -(-n // LANES)
        flat = jnp.pad(a.reshape(-1).astype(F32), (0, nr * LANES - n))
        rows.append(flat.reshape(nr, LANES))
        offs.append((r0, nr, a.shape))
        r0 += nr
    pad = (-r0) % 8
    if pad:
        rows.append(jnp.zeros((pad, LANES), F32))
    return jnp.concatenate(rows, axis=0), offs


def _unpack_rows(packed, offs):
    out = []
    for r0, nr, shp in offs:
        n = 1
        for s_ in shp:
            n *= s_
        out.append(packed[r0:r0 + nr].reshape(-1)[:n].reshape(shp))
    return out


def kernel(x, norm1_g, w_in, dn_conv_w, dn_a_log, dn_dt_bias, dn_norm_g, sc_conv_w, sc_norm_g, w_out, norm2_g, ffn_w_gate, ffn_w_up, ffn_w_down, final_norm_g, loss_target, m_norm1_g, m_w_in, m_dn_conv_w, m_dn_a_log, m_dn_dt_bias, m_dn_norm_g, m_sc_conv_w, m_sc_norm_g, m_w_out, m_norm2_g, m_ffn_w_gate, m_ffn_w_up, m_ffn_w_down, m_final_norm_g, v_norm1_g, v_w_in, v_dn_conv_w, v_dn_a_log, v_dn_dt_bias, v_dn_norm_g, v_sc_conv_w, v_sc_norm_g, v_w_out, v_norm2_g, v_ffn_w_gate, v_ffn_w_up, v_ffn_w_down, v_final_norm_g):
    depth, d, cin = w_in.shape
    t = x.shape[1]
    dff_s = ffn_w_gate.shape[2]
    me = _my_index()
    x2 = x.reshape(t, d)
    tgt = loss_target.reshape(t, d)

    conv_pack, conv_offs = _pack_rows([dn_conv_w, sc_conv_w])
    (conv_all,) = _all_gather([conv_pack], "gather_conv")
    dn_conv_full, sc_conv_full = [], []
    dn_parts, sc_parts = zip(*[_unpack_rows(conv_all[j], conv_offs) for j in range(N_DEV)])
    dn_conv_full = jnp.concatenate(dn_parts, axis=2)
    sc_conv_full = jnp.concatenate(sc_parts, axis=2)

    layer_params = []
    for l in range(depth):
        shards = [a[l].astype(MXU_DTYPE) for a in (w_in, w_out, ffn_w_gate, ffn_w_up, ffn_w_down)]
        g_in, g_out, g_gate, g_up, g_down = _all_gather(shards, "gather_weights")
        full_in = g_in.transpose(1, 0, 2).reshape(d, N_DEV * cin)
        full_gate = g_gate.transpose(1, 0, 2).reshape(d, N_DEV * dff_s)
        full_up = g_up.transpose(1, 0, 2).reshape(d, N_DEV * dff_s)
        layer_params.append(dict(
            w_in=_pad_w_in(full_in), w_out=g_out.reshape(d, d),
            w_gu=jnp.concatenate([full_gate, full_up], axis=1), w_down=g_down.reshape(N_DEV * dff_s, d),
            norm1_g=norm1_g[l][None], norm2_g=norm2_g[l][None], dn_norm_g=dn_norm_g[l][None],
            sc_norm_g=sc_norm_g[l][None], dn_conv_w=dn_conv_full[l], sc_conv_w=sc_conv_full[l],
            alog_row=_lane_row(dn_a_log[l], DN_HEADS), dt_row=_lane_row(dn_dt_bias[l], DN_HEADS)))

    saved = []
    xc = x2
    for l in range(depth):
        xc, s = _layer_fwd(xc, layer_params[l])
        saved.append(s)
    dx, loss_part, d_final = _final_loss(xc, final_norm_g[None], tgt)

    grads = [None] * depth
    for l in reversed(range(depth)):
        dx, grads[l] = _layer_bwd(dx, layer_params[l], saved[l])
    grad_x = dx.reshape(x.shape)

    names = ("w_in", "w_out", "ffn_w_gate", "ffn_w_up", "ffn_w_down")
    big_out = {n: {k: [] for k in ("g", "d", "m", "v")} for n in names}
    w_loc = dict(w_in=w_in, w_out=w_out, ffn_w_gate=ffn_w_gate, ffn_w_up=ffn_w_up, ffn_w_down=ffn_w_down)
    m_loc = dict(w_in=m_w_in, w_out=m_w_out, ffn_w_gate=m_ffn_w_gate, ffn_w_up=m_ffn_w_up, ffn_w_down=m_ffn_w_down)
    v_loc = dict(w_in=v_w_in, w_out=v_w_out, ffn_w_gate=v_ffn_w_gate, ffn_w_up=v_ffn_w_up, ffn_w_down=v_ffn_w_down)
    for l in range(depth):
        g = grads[l]
        cols = lambda a, c: a.reshape(a.shape[0], N_DEV, c).transpose(1, 0, 2)
        parts = [cols(_unpad_w_in(g["w_in"]), cin), g["w_out"].reshape(N_DEV, d // N_DEV, d),
                 cols(g["w_gu"][:, :N_DEV * dff_s], dff_s), cols(g["w_gu"][:, N_DEV * dff_s:], dff_s),
                 g["w_down"].reshape(N_DEV, dff_s, d)]
        recv = _all_to_all([a.astype(MXU_DTYPE) for a in parts], "scatter_grads")
        for n, r in zip(names, recv):
            res = _sum_adamw(r, w_loc[n][l], m_loc[n][l], v_loc[n][l], "adamw_" + n)
            for k, a in zip(("g", "d", "m", "v"), res):
                big_out[n][k].append(a)
    big_out = {n: {k: jnp.stack(v_) for k, v_ in o.items()} for n, o in big_out.items()}

    stack = lambda key: jnp.stack([grads[l][key] for l in range(depth)])
    small_parts = [stack("norm1_g").reshape(depth, d), stack("norm2_g").reshape(depth, d), d_final.reshape(d),
                   stack("dn_norm_g").reshape(depth, HEAD_DIM), stack("sc_norm_g").reshape(depth, SC_WIDTH),
                   stack("alog").reshape(depth, LANES), stack("dt").reshape(depth, LANES),
                   stack("dn_conv_w"), stack("sc_conv_w"), loss_part]
    small_pack, small_offs = _pack_rows(small_parts)
    (small_all,) = _all_gather([small_pack], "gather_small")
    total = _sum_rows(small_all, "sum_small")
    (g_n1, g_n2, g_fin, g_dnn, g_scn, g_alog, g_dt, g_dnc, g_scc, loss_row) = _unpack_rows(total, small_offs)
    loss = loss_row[0, 0]
    g_alog = g_alog[:, DN_HEADS:2 * DN_HEADS]
    g_dt = g_dt[:, DN_HEADS:2 * DN_HEADS]
    dnc_w = dn_conv_w.shape[2]
    scc_w = sc_conv_w.shape[2]
    g_dnc = lax.dynamic_slice_in_dim(g_dnc, me * dnc_w, dnc_w, axis=2)
    g_scc = lax.dynamic_slice_in_dim(g_scc, me * scc_w, scc_w, axis=2)
    sm_g = [g_n1, g_dnc, g_alog, g_dt, g_dnn, g_scc, g_scn, g_n2, g_fin]
    sm_w = [norm1_g, dn_conv_w, dn_a_log, dn_dt_bias, dn_norm_g, sc_conv_w, sc_norm_g, norm2_g, final_norm_g]
    sm_m = [m_norm1_g, m_dn_conv_w, m_dn_a_log, m_dn_dt_bias, m_dn_norm_g, m_sc_conv_w, m_sc_norm_g, m_norm2_g, m_final_norm_g]
    sm_v = [v_norm1_g, v_dn_conv_w, v_dn_a_log, v_dn_dt_bias, v_dn_norm_g, v_sc_conv_w, v_sc_norm_g, v_norm2_g, v_final_norm_g]
    pg, offs = _pack_rows(sm_g)
    pw, _ = _pack_rows(sm_w)
    pm, _ = _pack_rows(sm_m)
    pv, _ = _pack_rows(sm_v)
    sg, sd, sm_, sv = _sum_adamw(pg[None], pw, pm, pv, "adamw_small")
    small_out = {k: _unpack_rows(a, offs) for k, a in zip(("g", "d", "m", "v"), (sg, sd, sm_, sv))}

    def outputs(k):
        s_ = small_out[k]
        b = big_out
        return [s_[0], b["w_in"][k], s_[1], s_[2], s_[3], s_[4], s_[5], s_[6], b["w_out"][k], s_[7],
                b["ffn_w_gate"][k], b["ffn_w_up"][k], b["ffn_w_down"][k], s_[8]]

    return (loss, grad_x, *outputs("g"), *outputs("d"), *outputs("m"), *outputs("v"))
```

```python
import functools

import jax
import jax.numpy as jnp
from jax import lax
from jax.experimental import pallas as pl
from jax.experimental.pallas import tpu as pltpu

F32 = jnp.float32
MXU_DTYPE = jnp.bfloat16
MESH = pl.DeviceIdType.MESH

N_DEV = 8
EPS = 1e-6
DN_HEADS = 4
HEAD_DIM = 128
DN_WIDTH = DN_HEADS * HEAD_DIM
SC_WIDTH = 512
SC_GROUPS = 4
DN_CONV = 4
SC_CONV = 3
CHUNK = 64
HALO = 8
LANES = 128

QKV_W = 3 * DN_WIDTH
Z_OFF = QKV_W
SC_OFF = Z_OFF + DN_WIDTH
BA_OFF = SC_OFF + 3 * SC_WIDTH
BA_W = 256
PROJ_W = BA_OFF + BA_W

ADAM_LR = 0.001
ADAM_B1 = 0.9
ADAM_B2 = 0.999
ADAM_EPS = 1e-08
ADAM_WD = 0.01
ADAM_STEP = 10


def _pick(n, cands):
    for c in cands:
        if n % c == 0:
            return c
    return n


def _params(*sem):
    return pltpu.CompilerParams(dimension_semantics=sem)


def _rms_norm(x, g):
    return x * lax.rsqrt(jnp.mean(x * x, axis=-1, keepdims=True) + EPS) * g


def _dot(a, b, dims=(((1,), (0,)), ((), ()))):
    return lax.dot_general(a.astype(MXU_DTYPE), b.astype(MXU_DTYPE), dims, preferred_element_type=F32)


def _split_terms(x, terms):
    out = []
    for _ in range(terms):
        hi = x.astype(MXU_DTYPE)
        out.append(hi)
        x = x - hi.astype(F32)
    return out


def _ein_impl(spec, terms, a, b):
    ta, tb = terms
    if ta == 1 and tb == 1:
        return jnp.einsum(spec, a.astype(MXU_DTYPE), b.astype(MXU_DTYPE), preferred_element_type=F32)
    pa, pb = _split_terms(a, ta), _split_terms(b, tb)
    order = max(ta, tb) - 1
    acc = None
    for deg in range(order, -1, -1):
        for i in range(ta):
            j = deg - i
            if 0 <= j < tb:
                t = jnp.einsum(spec, pa[i], pb[j], preferred_element_type=F32)
                acc = t if acc is None else acc + t
    return acc


@functools.partial(jax.custom_vjp, nondiff_argnums=(0, 1))
def _ein(spec, terms, a, b):
    return _ein_impl(spec, terms, a, b)


def _ein_fwd(spec, terms, a, b):
    return _ein_impl(spec, terms, a, b), (a, b)


def _ein_bwd(spec, terms, res, ct):
    a, b = res
    xy, z = spec.split("->")
    x, y = xy.split(",")
    tc = min(max(terms), 2)
    da = _ein_impl(f"{z},{y}->{x}", (tc, terms[1]), ct, b)
    db = _ein_impl(f"{x},{z}->{y}", (terms[0], tc), a, ct)
    return da, db


_ein.defvjp(_ein_fwd, _ein_bwd)

FAST = (1, 1)
PRECISE = (2, 2)
LHS_EXACT = (1, 3)


def _causal_conv(cur, halo, w, k):
    tt = cur.shape[0]
    xp = jnp.concatenate([halo, cur], axis=0)
    y = None
    for j in range(k):
        start = HALO - (k - 1) + j
        term = xp[start:start + tt] * w[j:j + 1]
        y = term if y is None else y + term
    return y


def _matmul(a, b, mode, out_dtype, name, residual=None, ride=None):
    if mode == "nn":
        (m, k), (k2, n) = a.shape, b.shape
    elif mode == "nt":
        (m, k), (n, k2) = a.shape, b.shape
    else:
        (k, m), (k2, n) = a.shape, b.shape
    assert k == k2
    tm = _pick(m, (1024, 1408, 512, 256, 128))
    tn = _pick(n, (1280, 1408, 1024, 512, 256, 128))
    tk = k if k <= 2816 else _pick(k, (1024, 768, 512))
    gi, gj, nk = m // tm, n // tn, k // tk
    dims = {"nn": (((1,), (0,)), ((), ())), "nt": (((1,), (1,)), ((), ())), "tn": (((0,), (0,)), ((), ()))}[mode]
    a_spec = {"nn": pl.BlockSpec((tm, tk), lambda i, j, q: (i, q)),
              "nt": pl.BlockSpec((tm, tk), lambda i, j, q: (i, q)),
              "tn": pl.BlockSpec((tk, tm), lambda i, j, q: (q, i))}[mode]
    b_spec = {"nn": pl.BlockSpec((tk, tn), lambda i, j, q: (q, j)),
              "nt": pl.BlockSpec((tn, tk), lambda i, j, q: (j, q)),
              "tn": pl.BlockSpec((tk, tn), lambda i, j, q: (q, j))}[mode]
    o_spec = pl.BlockSpec((tm, tn), lambda i, j, q: (i, j))
    has_res = residual is not None
    n_in = 3 if has_res else 2
    nr = ride.na if ride else 0

    def body(*refs):
        a_ref, b_ref = refs[0], refs[1]
        r_ref = refs[2] if has_res else None
        ride_in = refs[n_in:n_in + nr]
        o_ref = refs[n_in + nr]
        ride_out = refs[n_in + nr + 1:n_in + 2 * nr + 1]
        acc = refs[n_in + 2 * nr + 1]
        ride_sems = refs[n_in + 2 * nr + 2:]
        i, j, q = pl.program_id(0), pl.program_id(1), pl.program_id(2)
        if ride:
            @pl.when((i == 0) & (j == 0) & (q == 0))
            def _():
                ride.start(ride_in, ride_out, ride_sems)

        @pl.when(q == 0)
        def _():
            acc[...] = jnp.zeros_like(acc)

        acc[...] += _dot(a_ref[...], b_ref[...], dims)

        @pl.when(q == nk - 1)
        def _():
            r = acc[...]
            if has_res:
                r = r + r_ref[...]
            o_ref[...] = r.astype(o_ref.dtype)

        if ride:
            @pl.when((i == gi - 1) & (j == gj - 1) & (q == nk - 1))
            def _():
                ride.finish(ride_in, ride_out, ride_sems)

    hbm = pl.BlockSpec(memory_space=pltpu.HBM)
    in_specs = [a_spec, b_spec] + ([o_spec] if has_res else []) + [hbm] * nr
    args = (a, b) + ((residual,) if has_res else ()) + (tuple(ride.arrays) if ride else ())
    res = pl.pallas_call(
        body, name=name, grid=(gi, gj, nk), in_specs=in_specs, out_specs=[o_spec] + [hbm] * nr,
        out_shape=[jax.ShapeDtypeStruct((m, n), out_dtype)] + (ride.out_shape if ride else []),
        scratch_shapes=[pltpu.VMEM((tm, tn), F32)] + (ride.scratch if ride else []),
        compiler_params=pltpu.CompilerParams(
            dimension_semantics=("arbitrary",) * 3 if ride else ("parallel", "parallel", "arbitrary"),
            has_side_effects=bool(ride)),
    )(*args)
    return (res[0], res[1:]) if ride else res[0]


def _norm_matmul(x, g, w, name, ride=None):
    t, d = x.shape
    n = w.shape[1]
    tm = _pick(t, (1024, 512, 256, 128))
    tn = _pick(n, (1280, 1408, 1024, 512, 256, 128))
    gi, gj = t // tm, n // tn
    nr = ride.na if ride else 0

    def body(*refs):
        x_ref, g_ref, w_ref = refs[:3]
        ride_in = refs[3:3 + nr]
        h_ref, y_ref = refs[3 + nr], refs[4 + nr]
        ride_out = refs[5 + nr:5 + 2 * nr]
        h_scr = refs[5 + 2 * nr]
        ride_sems = refs[6 + 2 * nr:]
        i, j = pl.program_id(0), pl.program_id(1)
        if ride:
            @pl.when((i == 0) & (j == 0))
            def _():
                ride.start(ride_in, ride_out, ride_sems)

        @pl.when(j == 0)
        def _():
            h = _rms_norm(x_ref[...], g_ref[...]).astype(MXU_DTYPE)
            h_scr[...] = h
            h_ref[...] = h

        y_ref[...] = _dot(h_scr[...], w_ref[...])

        if ride:
            @pl.when((i == gi - 1) & (j == gj - 1))
            def _():
                ride.finish(ride_in, ride_out, ride_sems)

    hbm = pl.BlockSpec(memory_space=pltpu.HBM)
    res = pl.pallas_call(
        body, name=name, grid=(gi, gj),
        in_specs=[pl.BlockSpec((tm, d), lambda i, j: (i, 0)), pl.BlockSpec((1, d), lambda i, j: (0, 0)),
                  pl.BlockSpec((d, tn), lambda i, j: (0, j))] + [hbm] * nr,
        out_specs=[pl.BlockSpec((tm, d), lambda i, j: (i, 0)), pl.BlockSpec((tm, tn), lambda i, j: (i, j))] + [hbm] * nr,
        out_shape=[jax.ShapeDtypeStruct((t, d), MXU_DTYPE), jax.ShapeDtypeStruct((t, n), F32)]
        + (ride.out_shape if ride else []),
        scratch_shapes=[pltpu.VMEM((tm, d), MXU_DTYPE)] + (ride.scratch if ride else []),
        compiler_params=pltpu.CompilerParams(
            dimension_semantics=("arbitrary",) * 2 if ride else ("parallel", "arbitrary"),
            has_side_effects=bool(ride)),
    )(x, g, w, *(ride.arrays if ride else ()))
    return (res[0], res[1], res[2:]) if ride else (res[0], res[1])


def _tok_call(name, fn, t, tt, tok_in, const_in, tok_out, acc_out):
    nblk = t // tt
    hb = tt // HALO
    in_specs, args = [], []
    for arr, w, cb, mode in tok_in:
        if mode == "cur":
            spec = pl.BlockSpec((tt, w), lambda i, cb=cb: (i, cb))
        elif mode == "prev":
            spec = pl.BlockSpec((HALO, w), lambda i, cb=cb: (jnp.maximum(i * hb - 1, 0), cb))
        else:
            spec = pl.BlockSpec((HALO, w), lambda i, cb=cb: (jnp.minimum(i + 1, nblk - 1), cb))
        in_specs.append(spec)
        args.append(arr)
    for arr in const_in:
        in_specs.append(pl.BlockSpec(arr.shape, lambda i: (0, 0)))
        args.append(arr)
    out_specs, out_shape = [], []
    for rows, w, dt in tok_out:
        out_specs.append(pl.BlockSpec((rows, w), lambda i: (i, 0)))
        out_shape.append(jax.ShapeDtypeStruct((nblk * rows, w), dt))
    for shp, dt in acc_out:
        out_specs.append(pl.BlockSpec(shp, lambda i: (0, 0)))
        out_shape.append(jax.ShapeDtypeStruct(shp, dt))
    n_tok, n_const, n_out = len(tok_in), len(const_in), len(tok_out)

    def body(*refs):
        i = pl.program_id(0)
        tok_vals = []
        for (_, _, _, mode), r in zip(tok_in, refs[:n_tok]):
            v = r[...]
            if mode == "prev":
                v = jnp.where(i > 0, v, jnp.zeros_like(v))
            elif mode == "next8":
                v = jnp.where(i < nblk - 1, v, jnp.zeros_like(v))
            tok_vals.append(v)
        const_vals = [r[...] for r in refs[n_tok:n_tok + n_const]]
        outs, accs = fn(tok_vals, const_vals)
        o_refs = refs[n_tok + n_const:n_tok + n_const + n_out]
        a_refs = refs[n_tok + n_const + n_out:]
        for r, v in zip(o_refs, outs):
            r[...] = v.astype(r.dtype)
        if a_refs:
            @pl.when(i == 0)
            def _():
                for r in a_refs:
                    r[...] = jnp.zeros_like(r)

            for r, v in zip(a_refs, accs):
                r[...] += v.astype(r.dtype)

    res = pl.pallas_call(
        body, name=name, grid=(nblk,), in_specs=in_specs, out_specs=out_specs, out_shape=out_shape,
        compiler_params=_params("arbitrary" if acc_out else "parallel"),
    )(*args)
    return res


def _dn_pre_math(cur, halo, ba, cw, alog, dtb):
    tt = cur.shape[0]
    a = jax.nn.silu(_causal_conv(cur, halo, cw, DN_CONV))
    pieces = []
    for p in range(2 * DN_HEADS):
        xh = a[:, p * HEAD_DIM:(p + 1) * HEAD_DIM]
        xh = xh * lax.rsqrt(jnp.sum(xh * xh, axis=-1, keepdims=True) + EPS)
        if p < DN_HEADS:
            xh = xh * (HEAD_DIM ** -0.5)
        pieces.append(xh)
    pieces.append(a[:, 2 * DN_WIDTH:])
    qkvn = jnp.concatenate(pieces, axis=1)
    lane = lax.broadcasted_iota(jnp.int32, ba.shape, 1)
    raw = jnp.where(lane < DN_HEADS, jax.nn.sigmoid(ba), -jnp.exp(alog) * jax.nn.softplus(ba + dtb))
    r = lax.broadcasted_iota(jnp.int32, (tt, tt), 0)
    c = lax.broadcasted_iota(jnp.int32, (tt, tt), 1)
    tri = jnp.where((r // CHUNK == c // CHUNK) & (c <= r), 1.0, 0.0).astype(F32)
    cums = _ein("ij,jk->ik", LHS_EXACT, tri, raw)
    bg = jnp.where(lane < DN_HEADS, raw, cums)
    return qkvn, bg


def _mix_math(o, z, gb, gc, gc_halo, hv, hv_halo, dng, scg, scw):
    outs = []
    for h in range(DN_HEADS):
        sl = slice(h * HEAD_DIM, (h + 1) * HEAD_DIM)
        oh = o[:, sl]
        outs.append(oh * lax.rsqrt(jnp.mean(oh * oh, axis=-1, keepdims=True) + EPS) * dng * jax.nn.silu(z[:, sl]))
    y = gb * _causal_conv(gc * hv, gc_halo * hv_halo, scw, SC_CONV)
    gw = SC_WIDTH // SC_GROUPS
    for g in range(SC_GROUPS):
        sl = slice(g * gw, (g + 1) * gw)
        yg = y[:, sl]
        outs.append(yg * lax.rsqrt(jnp.mean(yg * yg, axis=-1, keepdims=True) + EPS) * scg[:, sl])
    return jnp.concatenate(outs, axis=1)


def _swiglu_math(g, u):
    return jax.nn.silu(g) * u


def _tri_inverse(a):
    c = a.shape[-1]
    r = lax.broadcasted_iota(jnp.int32, (c, c), 0)
    q = lax.broadcasted_iota(jnp.int32, (c, c), 1)
    eye = jnp.where(r == q, 1.0, 0.0).astype(F32)[None]
    blk = (r // 16 == q // 16)[None]
    d = jnp.where(blk, a, 0.0)
    o = a - d
    mm = functools.partial(_ein, "bij,bjk->bik", PRECISE)
    p = eye - d
    n = mm(d, d)
    for _ in range(2):
        both = mm(jnp.concatenate([n, p], axis=1), n)
        n = both[:, :c]
        p = p + both[:, c:]
    p = p + mm(p, n)
    e = mm(p, o)
    e2 = mm(e, e)
    left = eye - e + e2 - mm(e, e2)
    return mm(left, p)


@jax.custom_vjp
def _inverse_known(a, tinv):
    return tinv


def _inverse_known_fwd(a, tinv):
    return tinv, tinv


def _inverse_known_bwd(tinv, ct):
    left = _ein("bji,bjk->bik", PRECISE, tinv, ct)
    return -_ein("bik,bjk->bij", PRECISE, left, tinv), jnp.zeros_like(tinv)


_inverse_known.defvjp(_inverse_known_fwd, _inverse_known_bwd)


def _delta_intra_math(q, k, v, bg, head, tinv_known=None):
    n = q.shape[0]
    nb = n // CHUNK
    lane = lax.broadcasted_iota(jnp.int32, bg.shape, 1)
    beta = jnp.sum(jnp.where(lane == head, bg, 0.0), axis=1, keepdims=True).reshape(nb, CHUNK, 1)
    gc = jnp.sum(jnp.where(lane == head + DN_HEADS, bg, 0.0), axis=1, keepdims=True).reshape(nb, CHUNK, 1)
    q3, k3, v3 = (a.reshape(nb, CHUNK, HEAD_DIM) for a in (q, k, v))
    r = lax.broadcasted_iota(jnp.int32, (CHUNK, CHUNK), 0)
    c = lax.broadcasted_iota(jnp.int32, (CHUNK, CHUNK), 1)
    eye = jnp.where(r == c, 1.0, 0.0).astype(F32)[None]
    gcr = _ein("bik,bkj->bij", LHS_EXACT, jnp.ones((nb, CHUNK, CHUNK), F32), gc * eye)
    decay = jnp.exp(jnp.where((r >= c)[None], gc - gcr, -1e30))
    kb = k3 * beta
    vb = v3 * beta
    egc = jnp.exp(gc)
    a = jnp.where((r > c)[None], _ein("bcd,bmd->bcm", FAST, kb, k3) * decay, 0.0)
    tinv = _tri_inverse(a) if tinv_known is None else _inverse_known(a, tinv_known)
    uw = _ein("bcm,bmd->bcd", PRECISE, tinv, jnp.concatenate([vb, kb * egc], axis=2))
    u, w = uw[:, :, :HEAD_DIM], uw[:, :, HEAD_DIM:]
    qk = _ein("bcd,bmd->bcm", FAST, q3, k3) * decay
    row = lax.broadcasted_iota(jnp.int32, (nb, CHUNK, 1), 1)
    glast = jnp.sum(jnp.where(row == CHUNK - 1, gc, 0.0), axis=1, keepdims=True)
    qd = q3 * egc
    kd = k3 * jnp.exp(glast - gc)
    glb = jnp.broadcast_to(jnp.exp(glast), (nb, HALO, LANES))
    flat = lambda x: x.reshape(n, HEAD_DIM)
    return flat(u), flat(w), flat(qd), flat(kd), qk, glb, tinv


def _delta_step_math(u, w, qd, kd, qk, gl, s):
    vnew = u - _ein("ck,kv->cv", FAST, w, s)
    o = _ein("ck,kv->cv", FAST, qd, s) + _ein("cm,mv->cv", FAST, qk, vnew)
    s2 = s * gl + _ein("ck,cv->kv", FAST, kd, vnew)
    return o, s2


def _delta_intra(qkvn, bg, nb):
    t = qkvn.shape[0]
    n = t // CHUNK
    rows = nb * CHUNK

    def body(q_ref, k_ref, v_ref, bg_ref, u_ref, w_ref, qd_ref, kd_ref, qk_ref, gl_ref, ti_ref):
        outs = _delta_intra_math(q_ref[...], k_ref[...], v_ref[...], bg_ref[...], pl.program_id(1))
        for r, v in zip((u_ref, w_ref, qd_ref, kd_ref, qk_ref, gl_ref, ti_ref), outs):
            r[...] = v

    col = lambda off: pl.BlockSpec((rows, HEAD_DIM), lambda b, h, off=off: (b, off + h))
    tok = jax.ShapeDtypeStruct((t, DN_WIDTH), F32)
    return pl.pallas_call(
        body, name="delta_intra", grid=(n // nb, DN_HEADS),
        in_specs=[col(0), col(DN_HEADS), col(2 * DN_HEADS), pl.BlockSpec((rows, LANES), lambda b, h: (b, 0))],
        out_specs=[col(0)] * 4 + [pl.BlockSpec((nb, None, CHUNK, CHUNK), lambda b, h: (b, h, 0, 0)),
                                  pl.BlockSpec((nb, None, HALO, LANES), lambda b, h: (b, h, 0, 0)),
                                  pl.BlockSpec((nb, None, CHUNK, CHUNK), lambda b, h: (b, h, 0, 0))],
        out_shape=[tok] * 4 + [jax.ShapeDtypeStruct((n, DN_HEADS, CHUNK, CHUNK), F32),
                               jax.ShapeDtypeStruct((n, DN_HEADS, HALO, LANES), F32),
                               jax.ShapeDtypeStruct((n, DN_HEADS, CHUNK, CHUNK), F32)],
        compiler_params=_params("parallel", "arbitrary"),
    )(qkvn, qkvn, qkvn, bg)


def _delta_intra_bwd(qkvn, bg, tinv, cts, nb):
    t = qkvn.shape[0]
    n = t // CHUNK
    rows = nb * CHUNK

    def body(q_ref, k_ref, v_ref, bg_ref, ti_ref, du, dw, dqd, dkd, dqk, dgl, dq_ref, dk_ref, dv_ref, dbg_ref):
        h = pl.program_id(1)
        ti = ti_ref[...]
        _, vjp = jax.vjp(lambda q, k, v, b: _delta_intra_math(q, k, v, b, h, ti)[:6],
                         q_ref[...], k_ref[...], v_ref[...], bg_ref[...])
        dq, dk, dv, dbg = vjp((du[...], dw[...], dqd[...], dkd[...], dqk[...], dgl[...]))
        dq_ref[...] = dq
        dk_ref[...] = dk
        dv_ref[...] = dv

        @pl.when(h == 0)
        def _():
            dbg_ref[...] = jnp.zeros_like(dbg_ref)

        dbg_ref[...] += dbg

    col = lambda off: pl.BlockSpec((rows, HEAD_DIM), lambda b, h, off=off: (b, off + h))
    bgs = pl.BlockSpec((rows, LANES), lambda b, h: (b, 0))
    qks = pl.BlockSpec((nb, None, CHUNK, CHUNK), lambda b, h: (b, h, 0, 0))
    gls = pl.BlockSpec((nb, None, HALO, LANES), lambda b, h: (b, h, 0, 0))
    tok = jax.ShapeDtypeStruct((t, DN_WIDTH), F32)
    return pl.pallas_call(
        body, name="delta_intra_bwd", grid=(n // nb, DN_HEADS),
        in_specs=[col(0), col(DN_HEADS), col(2 * DN_HEADS), bgs, qks, col(0), col(0), col(0), col(0), qks, gls],
        out_specs=[col(0), col(0), col(0), bgs],
        out_shape=[tok, tok, tok, jax.ShapeDtypeStruct((t, LANES), F32)],
        compiler_params=_params("parallel", "arbitrary"),
    )(qkvn, qkvn, qkvn, bg, tinv, *cts)


def _delta_scan(u, w, qd, kd, qk, glb, cb):
    t = u.shape[0]
    n = t // CHUNK
    rows = cb * CHUNK

    def body(u_ref, w_ref, qd_ref, kd_ref, qk_ref, gl_ref, o_ref, s_ref, s_scr):
        @pl.when(pl.program_id(0) == 0)
        def _():
            s_scr[...] = jnp.zeros_like(s_scr)

        def chunk(c, carry):
            r0 = pl.multiple_of(c * CHUNK, CHUNK)
            for h in range(DN_HEADS):
                sl = (pl.ds(r0, CHUNK), slice(h * HEAD_DIM, (h + 1) * HEAD_DIM))
                s = s_scr[h]
                s_ref[c, h] = s
                o, s2 = _delta_step_math(u_ref[sl], w_ref[sl], qd_ref[sl], kd_ref[sl], qk_ref[c, h],
                                         gl_ref[c, h][0:1, :], s)
                o_ref[sl] = o
                s_scr[h] = s2
            return carry

        lax.fori_loop(0, cb, chunk, 0)

    tok = pl.BlockSpec((rows, DN_WIDTH), lambda i: (i, 0))
    return pl.pallas_call(
        body, name="delta_scan", grid=(n // cb,),
        in_specs=[tok] * 4 + [pl.BlockSpec((cb, DN_HEADS, CHUNK, CHUNK), lambda i: (i, 0, 0, 0)),
                              pl.BlockSpec((cb, DN_HEADS, HALO, LANES), lambda i: (i, 0, 0, 0))],
        out_specs=[tok, pl.BlockSpec((cb, DN_HEADS, HEAD_DIM, HEAD_DIM), lambda i: (i, 0, 0, 0))],
        out_shape=[jax.ShapeDtypeStruct((t, DN_WIDTH), F32),
                   jax.ShapeDtypeStruct((n, DN_HEADS, HEAD_DIM, HEAD_DIM), F32)],
        scratch_shapes=[pltpu.VMEM((DN_HEADS, HEAD_DIM, HEAD_DIM), F32)],
        compiler_params=_params("arbitrary"),
    )(u, w, qd, kd, qk, glb)


def _delta_scan_bwd(u, w, qd, kd, qk, glb, s_all, do, cb):
    t = u.shape[0]
    n = t // CHUNK
    nblk = n // cb
    rows = cb * CHUNK

    def body(u_ref, w_ref, qd_ref, kd_ref, qk_ref, gl_ref, s_ref, do_ref,
             du_ref, dw_ref, dqd_ref, dkd_ref, dqk_ref, dgl_ref, ds_scr):
        @pl.when(pl.program_id(0) == 0)
        def _():
            ds_scr[...] = jnp.zeros_like(ds_scr)

        def chunk(step, carry):
            c = cb - 1 - step
            r0 = pl.multiple_of(c * CHUNK, CHUNK)
            for h in range(DN_HEADS):
                sl = (pl.ds(r0, CHUNK), slice(h * HEAD_DIM, (h + 1) * HEAD_DIM))
                gl_tile = gl_ref[c, h]
                prim = (u_ref[sl], w_ref[sl], qd_ref[sl], kd_ref[sl], qk_ref[c, h], gl_tile, s_ref[c, h])
                _, vjp = jax.vjp(lambda a, b, cc, d, e, g, s: _delta_step_math(a, b, cc, d, e, g[0:1, :], s), *prim)
                du, dw, dqd, dkd, dqk, dgl, ds = vjp((do_ref[sl], ds_scr[h]))
                du_ref[sl] = du
                dw_ref[sl] = dw
                dqd_ref[sl] = dqd
                dkd_ref[sl] = dkd
                dqk_ref[c, h] = dqk
                dgl_ref[c, h] = dgl
                ds_scr[h] = ds
            return carry

        lax.fori_loop(0, cb, chunk, 0)

    rev = lambda i: nblk - 1 - i
    tok = pl.BlockSpec((rows, DN_WIDTH), lambda i: (rev(i), 0))
    qks = pl.BlockSpec((cb, DN_HEADS, CHUNK, CHUNK), lambda i: (rev(i), 0, 0, 0))
    gls = pl.BlockSpec((cb, DN_HEADS, HALO, LANES), lambda i: (rev(i), 0, 0, 0))
    ss = pl.BlockSpec((cb, DN_HEADS, HEAD_DIM, HEAD_DIM), lambda i: (rev(i), 0, 0, 0))
    tshape = jax.ShapeDtypeStruct((t, DN_WIDTH), F32)
    return pl.pallas_call(
        body, name="delta_scan_bwd", grid=(nblk,),
        in_specs=[tok] * 4 + [qks, gls, ss, tok],
        out_specs=[tok] * 4 + [qks, gls],
        out_shape=[tshape] * 4 + [jax.ShapeDtypeStruct(qk.shape, F32), jax.ShapeDtypeStruct(glb.shape, F32)],
        scratch_shapes=[pltpu.VMEM((DN_HEADS, HEAD_DIM, HEAD_DIM), F32)],
        compiler_params=_params("arbitrary"),
    )(u, w, qd, kd, qk, glb, s_all, do)


def _peer(mask):
    x, y, c = lax.axis_index("x"), lax.axis_index("y"), lax.axis_index("c")
    return (x ^ ((mask >> 2) & 1), y ^ ((mask >> 1) & 1), c ^ (mask & 1))


def _my_index():
    return 4 * lax.axis_index("x") + 2 * lax.axis_index("y") + lax.axis_index("c")


class _Exchange:
    CHIP_MASKS = (4, 2, 6)

    def __init__(self, kind, arrays):
        self.kind = kind
        self.arrays = list(arrays)
        self.na = na = len(self.arrays)
        if kind == "gather":
            self.out_shape = [jax.ShapeDtypeStruct((N_DEV,) + a.shape, a.dtype) for a in self.arrays]
        else:
            self.out_shape = [jax.ShapeDtypeStruct(a.shape, a.dtype) for a in self.arrays]
        self.scratch = [pltpu.SemaphoreType.DMA((na, 7)), pltpu.SemaphoreType.DMA((na, 7)),
                        pltpu.SemaphoreType.DMA((na,))]

    def _copies(self, ins, outs, sems):
        send_sems, recv_sems, local_sems = sems
        me = _my_index()
        local, first, passed, arrivals = [], [], [], []
        if self.kind == "gather":
            def rc(a, k, block, to, own=False):
                def make():
                    dst = outs[a].at[block]
                    return pltpu.make_async_remote_copy(src_ref=ins[a] if own else dst, dst_ref=dst,
                                                        send_sem=send_sems.at[a, k], recv_sem=recv_sems.at[a, k],
                                                        device_id=to, device_id_type=MESH)
                return make

            sib = _peer(1)
            for a in range(self.na):
                local.append(lambda a=a: pltpu.make_async_copy(ins[a], outs[a].at[me], local_sems.at[a]))
                first.append(rc(a, 0, me, sib, own=True))
                arrivals.append(rc(a, 0, me ^ 1, _peer(0)))
                for j, m in enumerate(self.CHIP_MASKS):
                    first.append(rc(a, 1 + j, me, _peer(m), own=True))
                    passed.append((rc(a, 1 + j, me ^ m, _peer(0)), rc(a, 4 + j, me ^ m, sib)))
                    arrivals.append(rc(a, 4 + j, me ^ m ^ 1, _peer(0)))
        else:
            for a in range(self.na):
                local.append(lambda a=a: pltpu.make_async_copy(ins[a].at[me], outs[a].at[me], local_sems.at[a]))
                for m in range(1, N_DEV):
                    def make(a=a, m=m):
                        return pltpu.make_async_remote_copy(
                            src_ref=ins[a].at[me ^ m], dst_ref=outs[a].at[me], send_sem=send_sems.at[a, m - 1],
                            recv_sem=recv_sems.at[a, m - 1], device_id=_peer(m), device_id_type=MESH)
                    first.append(make)
                    arrivals.append(make)
        return local, first, passed, arrivals

    def start(self, ins, outs, sems):
        local, first, _, _ = self._copies(ins, outs, sems)
        for make in local + first:
            make().start()

    def finish(self, ins, outs, sems):
        local, first, passed, arrivals = self._copies(ins, outs, sems)
        for landed, onward in passed:
            landed().wait_recv()
            onward().start()
        for make in arrivals:
            make().wait_recv()
        for make in first + [p for _, p in passed]:
            make().wait_send()
        for make in local:
            make().wait()

    def run(self, name):
        na = self.na

        def body(*refs):
            ins, outs, sems = refs[:na], refs[na:2 * na], refs[2 * na:]
            self.start(ins, outs, sems)
            self.finish(ins, outs, sems)

        hbm = pl.BlockSpec(memory_space=pltpu.HBM)
        return pl.pallas_call(
            body, name=name, in_specs=[hbm] * na, out_specs=[hbm] * na, out_shape=self.out_shape,
            scratch_shapes=self.scratch, compiler_params=pltpu.CompilerParams(has_side_effects=True),
        )(*self.arrays)


def _all_gather(shards, name):
    return _Exchange("gather", shards).run(name)


def _adamw_math(w, g, m, v):
    m2 = ADAM_B1 * m + (1.0 - ADAM_B1) * g
    v2 = ADAM_B2 * v + (1.0 - ADAM_B2) * jnp.square(g)
    m_hat = m2 / (1.0 - ADAM_B1 ** ADAM_STEP)
    v_hat = v2 / (1.0 - ADAM_B2 ** ADAM_STEP)
    delta = -ADAM_LR * (m_hat / (jnp.sqrt(v_hat) + ADAM_EPS) + ADAM_WD * w)
    return delta, m2, v2


def _sum_adamw(parts, w, m, v, name):
    r, c = w.shape
    tr = _pick(r, (512, 256, 352, 128, 64, 32, 16, 8))
    np_ = parts.shape[0]

    def body(p_ref, w_ref, m_ref, v_ref, g_ref, d_ref, m2_ref, v2_ref):
        g = p_ref[0].astype(F32)
        for d in range(1, np_):
            g = g + p_ref[d].astype(F32)
        delta, m2, v2 = _adamw_math(w_ref[...], g, m_ref[...], v_ref[...])
        g_ref[...] = g
        d_ref[...] = delta
        m2_ref[...] = m2
        v2_ref[...] = v2

    blk = pl.BlockSpec((tr, c), lambda i: (i, 0))
    shp = jax.ShapeDtypeStruct((r, c), F32)
    return pl.pallas_call(
        body, name=name, grid=(r // tr,),
        in_specs=[pl.BlockSpec((np_, tr, c), lambda i: (0, i, 0)), blk, blk, blk],
        out_specs=[blk] * 4, out_shape=[shp] * 4, compiler_params=_params("parallel"),
    )(parts, w, m, v)


def _sum_rows(parts, name):
    np_, r, c = parts.shape

    def body(p_ref, o_ref):
        g = p_ref[0]
        for d in range(1, np_):
            g = g + p_ref[d]
        o_ref[...] = g

    return pl.pallas_call(body, name=name, out_shape=jax.ShapeDtypeStruct((r, c), F32))(parts)


def _pad_w_in(w):
    d = w.shape[0]
    n_ba = 2 * DN_HEADS
    a = w[:, :SC_OFF]
    ba = w[:, SC_OFF:SC_OFF + n_ba]
    sc = w[:, SC_OFF + n_ba:]
    return jnp.concatenate([a, sc, ba, jnp.zeros((d, BA_W - n_ba), w.dtype)], axis=1)


def _unpad_w_in(wp):
    n_ba = 2 * DN_HEADS
    return jnp.concatenate([wp[:, :SC_OFF], wp[:, BA_OFF:BA_OFF + n_ba], wp[:, SC_OFF:BA_OFF]], axis=1)


def _lane_row(v, off):
    return jnp.pad(v.astype(F32), (off, LANES - off - v.shape[0]))[None]


TT = 256
NB_INTRA = 8
CB_SCAN = 8


def _layer_fwd(x, p, rides):
    t, d = x.shape
    got = {}
    if rides:
        h, proj, got["proj"] = _norm_matmul(x, p["norm1_g"], p["w_in"], "proj_fwd_gather", ride=rides["proj"])
    else:
        h, proj = _norm_matmul(x, p["norm1_g"], p["w_in"], "proj_fwd")
    qkvn, bg = _tok_call(
        "dn_pre", lambda tv, cv: (_dn_pre_math(*tv, *cv), ()), t, TT,
        [(proj, QKV_W, 0, "cur"), (proj, QKV_W, 0, "prev"), (proj, LANES, BA_OFF // LANES, "cur")],
        [p["dn_conv_w"], p["alog_row"], p["dt_row"]],
        [(TT, QKV_W, F32), (TT, LANES, F32)], [])
    u, w, qd, kd, qk, glb, tinv = _delta_intra(qkvn, bg, NB_INTRA)
    o, s_all = _delta_scan(u, w, qd, kd, qk, glb, CB_SCAN)
    cb0 = SC_OFF // SC_WIDTH
    mix_in = [(o, DN_WIDTH, 0, "cur"), (proj, DN_WIDTH, Z_OFF // DN_WIDTH, "cur"),
              (proj, SC_WIDTH, cb0, "cur"), (proj, SC_WIDTH, cb0 + 1, "cur"), (proj, SC_WIDTH, cb0 + 1, "prev"),
              (proj, SC_WIDTH, cb0 + 2, "cur"), (proj, SC_WIDTH, cb0 + 2, "prev")]
    mix_const = [p["dn_norm_g"], p["sc_norm_g"], p["sc_conv_w"]]
    (cat,) = _tok_call("mix_post", lambda tv, cv: ((_mix_math(*tv, *cv),), ()), t, TT,
                       mix_in, mix_const, [(TT, 2 * DN_WIDTH, MXU_DTYPE)], [])
    x_mid = _matmul(cat, p["w_out"], "nn", F32, "out_proj", residual=x)
    if rides:
        h2, gu, got["ffn"] = _norm_matmul(x_mid, p["norm2_g"], p["w_gu"], "ffn_up_gather", ride=rides["ffn"])
    else:
        h2, gu = _norm_matmul(x_mid, p["norm2_g"], p["w_gu"], "ffn_up")
    dff = gu.shape[1] // 2
    (act,) = _tok_call("swiglu", lambda tv, cv: ((_swiglu_math(*tv),), ()), t, TT,
                       [(gu, dff, 0, "cur"), (gu, dff, 1, "cur")], [], [(TT, dff, MXU_DTYPE)], [])
    x_out = _matmul(act, p["w_down"], "nn", F32, "ffn_down", residual=x_mid)
    saved = dict(x=x, h=h, proj=proj, qkvn=qkvn, bg=bg, u=u, w=w, qd=qd, kd=kd, qk=qk, glb=glb, tinv=tinv, s_all=s_all, o=o,
                 cat=cat, x_mid=x_mid, h2=h2, gu=gu, act=act, mix_in=mix_in, mix_const=mix_const)
    return x_out, saved, got


def _norm_bwd(x, g, dh, dres, name):
    t, d = x.shape

    def fn(tv, cv):
        xv, dhv, drv = tv
        _, vjp = jax.vjp(_rms_norm, xv, cv[0])
        dx, dg = vjp(dhv)
        return (drv + dx,), (dg,)

    return _tok_call(name, fn, t, TT, [(x, d, 0, "cur"), (dh, d, 0, "cur"), (dres, d, 0, "cur")], [g],
                     [(TT, d, F32)], [((1, d), F32)])


def _layer_bwd(dx_out, p, s, rides):
    t, d = dx_out.shape
    got = {}
    nblk = t // TT
    dact = _matmul(dx_out, p["w_down"], "nt", F32, "ffn_down_dx")
    d_w_down = _matmul(s["act"], dx_out, "tn", F32, "ffn_down_dw")
    gu = s["gu"]
    dff = gu.shape[1] // 2

    def swiglu_bwd(tv, cv):
        g, u, da = tv
        _, vjp = jax.vjp(_swiglu_math, g, u)
        dg, du = vjp(da)
        return (jnp.concatenate([dg, du], axis=1),), ()

    (dgu,) = _tok_call("swiglu_bwd", swiglu_bwd, t, TT,
                       [(gu, dff, 0, "cur"), (gu, dff, 1, "cur"), (dact, dff, 0, "cur")], [],
                       [(TT, 2 * dff, MXU_DTYPE)], [])
    if rides:
        dh2, got["gu"] = _matmul(dgu, p["w_gu"], "nt", F32, "ffn_up_dx_scatter", ride=rides["gu"])
        d_w_gu, got["win"] = _matmul(s["h2"], dgu, "tn", F32, "ffn_up_dw_scatter", ride=rides["win"])
    else:
        dh2 = _matmul(dgu, p["w_gu"], "nt", F32, "ffn_up_dx")
        d_w_gu = _matmul(s["h2"], dgu, "tn", F32, "ffn_up_dw")
    dx_mid, d_norm2 = _norm_bwd(s["x_mid"], p["norm2_g"], dh2, dx_out, "norm2_bwd")
    dcat = _matmul(dx_mid, p["w_out"], "nt", F32, "out_proj_dx")
    d_w_out = _matmul(s["cat"], dx_mid, "tn", F32, "out_proj_dw")

    def mix_bwd(tv, cv):
        prim = tuple(tv[:7]) + tuple(cv)
        _, vjp = jax.vjp(_mix_math, *prim)
        do, dz, dgb, dgc, dgch, dhv, dhvh, ddng, dscg, dscw = vjp(tv[7])
        return (do, dz, dgb, dgc, dgch, dhv, dhvh), (ddng, dscg, dscw)

    wide = (TT, DN_WIDTH, F32)
    halo = (HALO, SC_WIDTH, F32)
    do, dz, dgb, dgc, dgc_h, dhv, dhv_h, d_dn_norm, d_sc_norm, d_sc_conv = _tok_call(
        "mix_post_bwd", mix_bwd, t, TT, s["mix_in"] + [(dcat, 2 * DN_WIDTH, 0, "cur")], s["mix_const"],
        [wide, wide, wide, wide, halo, wide, halo],
        [((1, HEAD_DIM), F32), ((1, SC_WIDTH), F32), ((SC_CONV, SC_WIDTH), F32)])
    cts = _delta_scan_bwd(s["u"], s["w"], s["qd"], s["kd"], s["qk"], s["glb"], s["s_all"], do, CB_SCAN)
    dq, dk, dv, dbg = _delta_intra_bwd(s["qkvn"], s["bg"], s["tinv"], cts, NB_INTRA)
    proj = s["proj"]

    def dn_pre_bwd(tv, cv):
        cur, hal, ba, dq_, dk_, dv_, dbg_ = tv
        _, vjp = jax.vjp(_dn_pre_math, cur, hal, ba, *cv)
        dcur, dhal, dba, dcw, dal, ddt = vjp((jnp.concatenate([dq_, dk_, dv_], axis=1), dbg_))
        return (dcur, dhal, dba), (dcw, dal, ddt)

    dqkv, dqkv_h, dba, d_dn_conv, d_alog, d_dt = _tok_call(
        "dn_pre_bwd", dn_pre_bwd, t, TT,
        [(proj, QKV_W, 0, "cur"), (proj, QKV_W, 0, "prev"), (proj, LANES, BA_OFF // LANES, "cur"),
         (dq, DN_WIDTH, 0, "cur"), (dk, DN_WIDTH, 0, "cur"), (dv, DN_WIDTH, 0, "cur"), (dbg, LANES, 0, "cur")],
        [p["dn_conv_w"], p["alog_row"], p["dt_row"]],
        [(TT, QKV_W, F32), (HALO, QKV_W, F32), (TT, LANES, F32)],
        [((DN_CONV, QKV_W), F32), ((1, LANES), F32), ((1, LANES), F32)])

    def assemble(tv, cv):
        dqkv_, dqkv_n, dz_, dgb_, dgc_, dgc_n, dhv_, dhv_n, dba_ = tv

        def with_halo(cur, nxt):
            return cur + jnp.concatenate([jnp.zeros((TT - HALO, cur.shape[1]), F32), nxt], axis=0)

        out = jnp.concatenate([with_halo(dqkv_, dqkv_n), dz_, dgb_, with_halo(dgc_, dgc_n), with_halo(dhv_, dhv_n),
                               dba_, jnp.zeros((TT, BA_W - LANES), F32)], axis=1)
        return (out,), ()

    (dproj,) = _tok_call(
        "dproj_assemble", assemble, t, TT,
        [(dqkv, QKV_W, 0, "cur"), (dqkv_h, QKV_W, 0, "next8"), (dz, DN_WIDTH, 0, "cur"), (dgb, SC_WIDTH, 0, "cur"),
         (dgc, SC_WIDTH, 0, "cur"), (dgc_h, SC_WIDTH, 0, "next8"), (dhv, SC_WIDTH, 0, "cur"),
         (dhv_h, SC_WIDTH, 0, "next8"), (dba, LANES, 0, "cur")], [],
        [(TT, PROJ_W, MXU_DTYPE)], [])
    if rides:
        dh, got["rest"] = _matmul(dproj, p["w_in"], "nt", F32, "proj_dx_scatter", ride=rides["rest"])
    else:
        dh = _matmul(dproj, p["w_in"], "nt", F32, "proj_dx")
    d_w_in = _matmul(s["h"], dproj, "tn", F32, "proj_dw")
    dx_in, d_norm1 = _norm_bwd(s["x"], p["norm1_g"], dh, dx_mid, "norm1_bwd")
    grads = dict(w_in=d_w_in, w_out=d_w_out, w_gu=d_w_gu, w_down=d_w_down, norm1_g=d_norm1, norm2_g=d_norm2,
                 dn_norm_g=d_dn_norm, sc_norm_g=d_sc_norm, sc_conv_w=d_sc_conv, dn_conv_w=d_dn_conv,
                 alog=d_alog, dt=d_dt)
    return dx_in, grads, got


def _final_loss(x, g, target):
    t, d = x.shape

    def fn(tv, cv):
        xv, tg = tv

        def loss_fn(xx, gg):
            err = jnp.square(_rms_norm(xx, gg) - tg)
            return 0.5 * jnp.sum(jnp.mean(err, axis=-1))

        loss, vjp = jax.vjp(loss_fn, xv, cv[0])
        dx, dg = vjp(jnp.ones((), F32))
        return (dx,), (jnp.full((1, LANES), loss, F32), dg)

    return _tok_call("final_loss", fn, t, TT, [(x, d, 0, "cur"), (target, d, 0, "cur")], [g],
                     [(TT, d, F32)], [((1, LANES), F32), ((1, d), F32)])


def _pack_rows(arrs):
    rows, offs, r0 = [], [], 0
    for a in arrs:
        n = a.size
        nr = -(-n // LANES)
        flat = jnp.pad(a.reshape(-1).astype(F32), (0, nr * LANES - n))
        rows.append(flat.reshape(nr, LANES))
        offs.append((r0, nr, a.shape))
        r0 += nr
    pad = (-r0) % 8
    if pad:
        rows.append(jnp.zeros((pad, LANES), F32))
    return jnp.concatenate(rows, axis=0), offs


def _unpack_rows(packed, offs):
    out = []
    for r0, nr, shp in offs:
        n = 1
        for s_ in shp:
            n *= s_
        out.append(packed[r0:r0 + nr].reshape(-1)[:n].reshape(shp))
    return out


def kernel(x, norm1_g, w_in, dn_conv_w, dn_a_log, dn_dt_bias, dn_norm_g, sc_conv_w, sc_norm_g, w_out, norm2_g, ffn_w_gate, ffn_w_up, ffn_w_down, final_norm_g, loss_target, m_norm1_g, m_w_in, m_dn_conv_w, m_dn_a_log, m_dn_dt_bias, m_dn_norm_g, m_sc_conv_w, m_sc_norm_g, m_w_out, m_norm2_g, m_ffn_w_gate, m_ffn_w_up, m_ffn_w_down, m_final_norm_g, v_norm1_g, v_w_in, v_dn_conv_w, v_dn_a_log, v_dn_dt_bias, v_dn_norm_g, v_sc_conv_w, v_sc_norm_g, v_w_out, v_norm2_g, v_ffn_w_gate, v_ffn_w_up, v_ffn_w_down, v_final_norm_g):
    depth, d, cin = w_in.shape
    t = x.shape[1]
    dff_s = ffn_w_gate.shape[2]
    me = _my_index()
    x2 = x.reshape(t, d)
    tgt = loss_target.reshape(t, d)

    conv_pack, conv_offs = _pack_rows([dn_conv_w, sc_conv_w])
    (conv_all,) = _all_gather([conv_pack], "gather_conv")
    dn_parts, sc_parts = zip(*[_unpack_rows(conv_all[j], conv_offs) for j in range(N_DEV)])
    dn_conv_full = jnp.concatenate(dn_parts, axis=2)
    sc_conv_full = jnp.concatenate(sc_parts, axis=2)

    def shards(l):
        return [a[l].astype(MXU_DTYPE) for a in (w_in, w_out, ffn_w_gate, ffn_w_up, ffn_w_down)]

    def layer_params(l, g_in, g_out, g_gate, g_up, g_down):
        full_in = g_in.transpose(1, 0, 2).reshape(d, N_DEV * cin)
        full_gate = g_gate.transpose(1, 0, 2).reshape(d, N_DEV * dff_s)
        full_up = g_up.transpose(1, 0, 2).reshape(d, N_DEV * dff_s)
        return dict(
            w_in=_pad_w_in(full_in), w_out=g_out.reshape(d, d),
            w_gu=jnp.concatenate([full_gate, full_up], axis=1), w_down=g_down.reshape(N_DEV * dff_s, d),
            norm1_g=norm1_g[l][None], norm2_g=norm2_g[l][None], dn_norm_g=dn_norm_g[l][None],
            sc_norm_g=sc_norm_g[l][None], dn_conv_w=dn_conv_full[l], sc_conv_w=sc_conv_full[l],
            alog_row=_lane_row(dn_a_log[l], DN_HEADS), dt_row=_lane_row(dn_dt_bias[l], DN_HEADS))

    params = [layer_params(0, *_all_gather(shards(0), "gather_weights"))]
    saved = []
    xc = x2
    for l in range(depth):
        rides = None
        if l + 1 < depth:
            nxt = shards(l + 1)
            rides = dict(proj=_Exchange("gather", nxt[:2]), ffn=_Exchange("gather", nxt[2:]))
        xc, s, got = _layer_fwd(xc, params[l], rides)
        saved.append(s)
        if rides:
            params.append(layer_params(l + 1, *got["proj"], *got["ffn"]))
    dx, loss_part, d_final = _final_loss(xc, final_norm_g[None], tgt)

    names = ("w_in", "w_out", "ffn_w_gate", "ffn_w_up", "ffn_w_down")
    big_out = {n: {k: [None] * depth for k in ("g", "d", "m", "v")} for n in names}
    w_loc = dict(w_in=w_in, w_out=w_out, ffn_w_gate=ffn_w_gate, ffn_w_up=ffn_w_up, ffn_w_down=ffn_w_down)
    m_loc = dict(w_in=m_w_in, w_out=m_w_out, ffn_w_gate=m_ffn_w_gate, ffn_w_up=m_ffn_w_up, ffn_w_down=m_ffn_w_down)
    v_loc = dict(w_in=v_w_in, w_out=v_w_out, ffn_w_gate=v_ffn_w_gate, ffn_w_up=v_ffn_w_up, ffn_w_down=v_ffn_w_down)

    def parts_of(g):
        cols = lambda a, c: a.reshape(a.shape[0], N_DEV, c).transpose(1, 0, 2)
        parts = [cols(_unpad_w_in(g["w_in"]), cin), g["w_out"].reshape(N_DEV, d // N_DEV, d),
                 cols(g["w_gu"][:, :N_DEV * dff_s], dff_s), cols(g["w_gu"][:, N_DEV * dff_s:], dff_s),
                 g["w_down"].reshape(N_DEV, dff_s, d)]
        return [a.astype(MXU_DTYPE) for a in parts]

    def apply(l, recv):
        for n, r in zip(names, recv):
            res = _sum_adamw(r, w_loc[n][l], m_loc[n][l], v_loc[n][l], "adamw_" + n)
            for k, a in zip(("g", "d", "m", "v"), res):
                big_out[n][k][l] = a

    grads = [None] * depth
    pending = None
    for l in reversed(range(depth)):
        rides = None
        if pending is not None:
            rides = dict(gu=_Exchange("scatter", pending[2:4]), win=_Exchange("scatter", pending[0:1]),
                         rest=_Exchange("scatter", [pending[4], pending[1]]))
        dx, grads[l], got = _layer_bwd(dx, params[l], saved[l], rides)
        if rides:
            apply(l + 1, [got["win"][0], got["rest"][1], got["gu"][0], got["gu"][1], got["rest"][0]])
        pending = parts_of(grads[l])
    apply(0, _Exchange("scatter", pending).run("scatter_grads"))
    grad_x = dx.reshape(x.shape)
    big_out = {n: {k: jnp.stack(v_) for k, v_ in o.items()} for n, o in big_out.items()}

    stack = lambda key: jnp.stack([grads[l][key] for l in range(depth)])
    small_parts = [stack("norm1_g").reshape(depth, d), stack("norm2_g").reshape(depth, d), d_final.reshape(d),
                   stack("dn_norm_g").reshape(depth, HEAD_DIM), stack("sc_norm_g").reshape(depth, SC_WIDTH),
                   stack("alog").reshape(depth, LANES), stack("dt").reshape(depth, LANES),
                   stack("dn_conv_w"), stack("sc_conv_w"), loss_part]
    small_pack, small_offs = _pack_rows(small_parts)
    (small_all,) = _all_gather([small_pack], "gather_small")
    total = _sum_rows(small_all, "sum_small")
    (g_n1, g_n2, g_fin, g_dnn, g_scn, g_alog, g_dt, g_dnc, g_scc, loss_row) = _unpack_rows(total, small_offs)
    loss = loss_row[0, 0]
    g_alog = g_alog[:, DN_HEADS:2 * DN_HEADS]
    g_dt = g_dt[:, DN_HEADS:2 * DN_HEADS]
    dnc_w = dn_conv_w.shape[2]
    scc_w = sc_conv_w.shape[2]
    g_dnc = lax.dynamic_slice_in_dim(g_dnc, me * dnc_w, dnc_w, axis=2)
    g_scc = lax.dynamic_slice_in_dim(g_scc, me * scc_w, scc_w, axis=2)
    sm_g = [g_n1, g_dnc, g_alog, g_dt, g_dnn, g_scc, g_scn, g_n2, g_fin]
    sm_w = [norm1_g, dn_conv_w, dn_a_log, dn_dt_bias, dn_norm_g, sc_conv_w, sc_norm_g, norm2_g, final_norm_g]
    sm_m = [m_norm1_g, m_dn_conv_w, m_dn_a_log, m_dn_dt_bias, m_dn_norm_g, m_sc_conv_w, m_sc_norm_g, m_norm2_g, m_final_norm_g]
    sm_v = [v_norm1_g, v_dn_conv_w, v_dn_a_log, v_dn_dt_bias, v_dn_norm_g, v_sc_conv_w, v_sc_norm_g, v_norm2_g, v_final_norm_g]
    pg, offs = _pack_rows(sm_g)
    pw, _ = _pack_rows(sm_w)
    pm, _ = _pack_rows(sm_m)
    pv, _ = _pack_rows(sm_v)
    sg, sd, sm_, sv = _sum_adamw(pg[None], pw, pm, pv, "adamw_small")
    small_out = {k: _unpack_rows(a, offs) for k, a in zip(("g", "d", "m", "v"), (sg, sd, sm_, sv))}

    def outputs(k):
        s_ = small_out[k]
        b = big_out
        return [s_[0], b["w_in"][k], s_[1], s_[2], s_[3], s_[4], s_[5], s_[6], b["w_out"][k], s_[7],
                b["ffn_w_gate"][k], b["ffn_w_up"][k], b["ffn_w_down"][k], s_[8]]

    return (loss, grad_x, *outputs("g"), *outputs("d"), *outputs("m"), *outputs("v"))
```

```python
import functools

import jax
import jax.numpy as jnp
from jax import lax
from jax.experimental import pallas as pl
from jax.experimental.pallas import tpu as pltpu

F32 = jnp.float32
MXU_DTYPE = jnp.bfloat16
MESH = pl.DeviceIdType.MESH

N_DEV = 8
EPS = 1e-6
DN_HEADS = 4
HEAD_DIM = 128
DN_WIDTH = DN_HEADS * HEAD_DIM
SC_WIDTH = 512
SC_GROUPS = 4
DN_CONV = 4
SC_CONV = 3
CHUNK = 64
HALO = 8
LANES = 128

QKV_W = 3 * DN_WIDTH
Z_OFF = QKV_W
SC_OFF = Z_OFF + DN_WIDTH
BA_OFF = SC_OFF + 3 * SC_WIDTH
BA_W = 256
PROJ_W = BA_OFF + BA_W

ADAM_LR = 0.001
ADAM_B1 = 0.9
ADAM_B2 = 0.999
ADAM_EPS = 1e-08
ADAM_WD = 0.01
ADAM_STEP = 10


def _pick(n, cands):
    for c in cands:
        if n % c == 0:
            return c
    return n


def _params(*sem):
    return pltpu.CompilerParams(dimension_semantics=sem)


def _rms_norm(x, g):
    return x * lax.rsqrt(jnp.mean(x * x, axis=-1, keepdims=True) + EPS) * g


def _dot(a, b, dims=(((1,), (0,)), ((), ()))):
    return lax.dot_general(a.astype(MXU_DTYPE), b.astype(MXU_DTYPE), dims, preferred_element_type=F32)


def _split_terms(x, terms):
    out = []
    for _ in range(terms):
        hi = x.astype(MXU_DTYPE)
        out.append(hi)
        x = x - hi.astype(F32)
    return out


def _ein_impl(spec, terms, a, b):
    ta, tb = terms
    if ta == 1 and tb == 1:
        return jnp.einsum(spec, a.astype(MXU_DTYPE), b.astype(MXU_DTYPE), preferred_element_type=F32)
    pa, pb = _split_terms(a, ta), _split_terms(b, tb)
    order = max(ta, tb) - 1
    acc = None
    for deg in range(order, -1, -1):
        for i in range(ta):
            j = deg - i
            if 0 <= j < tb:
                t = jnp.einsum(spec, pa[i], pb[j], preferred_element_type=F32)
                acc = t if acc is None else acc + t
    return acc


@functools.partial(jax.custom_vjp, nondiff_argnums=(0, 1))
def _ein(spec, terms, a, b):
    return _ein_impl(spec, terms, a, b)


def _ein_fwd(spec, terms, a, b):
    return _ein_impl(spec, terms, a, b), (a, b)


def _ein_bwd(spec, terms, res, ct):
    a, b = res
    xy, z = spec.split("->")
    x, y = xy.split(",")
    tc = min(max(terms), 2)
    da = _ein_impl(f"{z},{y}->{x}", (tc, terms[1]), ct, b)
    db = _ein_impl(f"{x},{z}->{y}", (terms[0], tc), a, ct)
    return da, db


_ein.defvjp(_ein_fwd, _ein_bwd)

FAST = (1, 1)
PRECISE = (2, 2)
LHS_EXACT = (1, 3)


def _causal_conv(cur, halo, w, k):
    tt = cur.shape[0]
    xp = jnp.concatenate([halo, cur], axis=0)
    y = None
    for j in range(k):
        start = HALO - (k - 1) + j
        term = xp[start:start + tt] * w[j:j + 1]
        y = term if y is None else y + term
    return y


def _matmul(a, b, mode, out_dtype, name, residual=None, ride=None):
    if mode == "nn":
        (m, k), (k2, n) = a.shape, b.shape
    elif mode == "nt":
        (m, k), (n, k2) = a.shape, b.shape
    else:
        (k, m), (k2, n) = a.shape, b.shape
    assert k == k2
    tm = _pick(m, (1024, 1408, 512, 256, 128))
    tn = _pick(n, (1280, 1408, 1024, 512, 256, 128))
    tk = k if k <= 2816 else _pick(k, (1024, 768, 512))
    gi, gj, nk = m // tm, n // tn, k // tk
    dims = {"nn": (((1,), (0,)), ((), ())), "nt": (((1,), (1,)), ((), ())), "tn": (((0,), (0,)), ((), ()))}[mode]
    a_spec = {"nn": pl.BlockSpec((tm, tk), lambda i, j, q: (i, q)),
              "nt": pl.BlockSpec((tm, tk), lambda i, j, q: (i, q)),
              "tn": pl.BlockSpec((tk, tm), lambda i, j, q: (q, i))}[mode]
    b_spec = {"nn": pl.BlockSpec((tk, tn), lambda i, j, q: (q, j)),
              "nt": pl.BlockSpec((tn, tk), lambda i, j, q: (j, q)),
              "tn": pl.BlockSpec((tk, tn), lambda i, j, q: (q, j))}[mode]
    o_spec = pl.BlockSpec((tm, tn), lambda i, j, q: (i, j))
    has_res = residual is not None
    n_in = 3 if has_res else 2
    nr = ride.na if ride else 0

    def body(*refs):
        a_ref, b_ref = refs[0], refs[1]
        r_ref = refs[2] if has_res else None
        ride_in = refs[n_in:n_in + nr]
        o_ref = refs[n_in + nr]
        ride_out = refs[n_in + nr + 1:n_in + 2 * nr + 1]
        acc = refs[n_in + 2 * nr + 1]
        ride_sems = refs[n_in + 2 * nr + 2:]
        i, j, q = pl.program_id(0), pl.program_id(1), pl.program_id(2)
        if ride:
            @pl.when((i == 0) & (j == 0) & (q == 0))
            def _():
                ride.start(ride_in, ride_out, ride_sems)

        @pl.when(q == 0)
        def _():
            acc[...] = jnp.zeros_like(acc)

        acc[...] += _dot(a_ref[...], b_ref[...], dims)

        @pl.when(q == nk - 1)
        def _():
            r = acc[...]
            if has_res:
                r = r + r_ref[...]
            o_ref[...] = r.astype(o_ref.dtype)

        if ride:
            @pl.when((i == gi - 1) & (j == gj - 1) & (q == nk - 1))
            def _():
                ride.finish(ride_in, ride_out, ride_sems)

    hbm = pl.BlockSpec(memory_space=pltpu.HBM)
    in_specs = [a_spec, b_spec] + ([o_spec] if has_res else []) + [hbm] * nr
    args = (a, b) + ((residual,) if has_res else ()) + (tuple(ride.arrays) if ride else ())
    res = pl.pallas_call(
        body, name=name, grid=(gi, gj, nk), in_specs=in_specs, out_specs=[o_spec] + [hbm] * nr,
        out_shape=[jax.ShapeDtypeStruct((m, n), out_dtype)] + (ride.out_shape if ride else []),
        scratch_shapes=[pltpu.VMEM((tm, tn), F32)] + (ride.scratch if ride else []),
        compiler_params=pltpu.CompilerParams(
            dimension_semantics=("arbitrary",) * 3 if ride else ("parallel", "parallel", "arbitrary"),
            has_side_effects=bool(ride)),
    )(*args)
    return (res[0], res[1:]) if ride else res[0]


def _norm_matmul(x, g, w, name, ride=None):
    t, d = x.shape
    n = w.shape[1]
    tm = _pick(t, (1024, 512, 256, 128))
    tn = _pick(n, (1280, 1408, 1024, 512, 256, 128))
    gi, gj = t // tm, n // tn
    nr = ride.na if ride else 0

    def body(*refs):
        x_ref, g_ref, w_ref = refs[:3]
        ride_in = refs[3:3 + nr]
        h_ref, y_ref = refs[3 + nr], refs[4 + nr]
        ride_out = refs[5 + nr:5 + 2 * nr]
        h_scr = refs[5 + 2 * nr]
        ride_sems = refs[6 + 2 * nr:]
        i, j = pl.program_id(0), pl.program_id(1)
        if ride:
            @pl.when((i == 0) & (j == 0))
            def _():
                ride.start(ride_in, ride_out, ride_sems)

        @pl.when(j == 0)
        def _():
            h = _rms_norm(x_ref[...], g_ref[...]).astype(MXU_DTYPE)
            h_scr[...] = h
            h_ref[...] = h

        y_ref[...] = _dot(h_scr[...], w_ref[...])

        if ride:
            @pl.when((i == gi - 1) & (j == gj - 1))
            def _():
                ride.finish(ride_in, ride_out, ride_sems)

    hbm = pl.BlockSpec(memory_space=pltpu.HBM)
    res = pl.pallas_call(
        body, name=name, grid=(gi, gj),
        in_specs=[pl.BlockSpec((tm, d), lambda i, j: (i, 0)), pl.BlockSpec((1, d), lambda i, j: (0, 0)),
                  pl.BlockSpec((d, tn), lambda i, j: (0, j))] + [hbm] * nr,
        out_specs=[pl.BlockSpec((tm, d), lambda i, j: (i, 0)), pl.BlockSpec((tm, tn), lambda i, j: (i, j))] + [hbm] * nr,
        out_shape=[jax.ShapeDtypeStruct((t, d), MXU_DTYPE), jax.ShapeDtypeStruct((t, n), F32)]
        + (ride.out_shape if ride else []),
        scratch_shapes=[pltpu.VMEM((tm, d), MXU_DTYPE)] + (ride.scratch if ride else []),
        compiler_params=pltpu.CompilerParams(
            dimension_semantics=("arbitrary",) * 2 if ride else ("parallel", "arbitrary"),
            has_side_effects=bool(ride)),
    )(x, g, w, *(ride.arrays if ride else ()))
    return (res[0], res[1], res[2:]) if ride else (res[0], res[1])


def _swiglu_math(g, u):
    return jax.nn.silu(g) * u


def _gu_tile(dff):
    return _pick(dff, (1408, 1024, 512, 256, 128))


def _interleave_gu(gate, up):
    tn = _gu_tile(gate.shape[1])
    pieces = []
    for j in range(gate.shape[1] // tn):
        pieces += [gate[:, j * tn:(j + 1) * tn], up[:, j * tn:(j + 1) * tn]]
    return jnp.concatenate(pieces, axis=1)


def _split_gu(gu):
    dff = gu.shape[1] // 2
    tn = _gu_tile(dff)
    tiles = [gu[:, j * tn:(j + 1) * tn] for j in range(2 * dff // tn)]
    return jnp.concatenate(tiles[0::2], axis=1), jnp.concatenate(tiles[1::2], axis=1)


def _ffn_up_swiglu(x, g, w_gu, name, ride=None):
    t, d = x.shape
    dff = w_gu.shape[1] // 2
    tn = _gu_tile(dff)
    tm = _pick(t, (512, 256, 128))
    gj, gi = dff // tn, t // tm
    nr = ride.na if ride else 0

    def body(*refs):
        x_ref, g_ref, w_ref = refs[:3]
        ride_in = refs[3:3 + nr]
        h_ref, gu_ref, act_ref = refs[3 + nr:6 + nr]
        ride_out = refs[6 + nr:6 + 2 * nr]
        ride_sems = refs[6 + 2 * nr:]
        j, i = pl.program_id(0), pl.program_id(1)
        if ride:
            @pl.when((i == 0) & (j == 0))
            def _():
                ride.start(ride_in, ride_out, ride_sems)

        h = _rms_norm(x_ref[...], g_ref[...]).astype(MXU_DTYPE)

        @pl.when(j == 0)
        def _():
            h_ref[...] = h

        y = _dot(h, w_ref[...])
        gu_ref[...] = y.astype(gu_ref.dtype)
        act_ref[...] = _swiglu_math(y[:, :tn], y[:, tn:]).astype(act_ref.dtype)

        if ride:
            @pl.when((i == gi - 1) & (j == gj - 1))
            def _():
                ride.finish(ride_in, ride_out, ride_sems)

    hbm = pl.BlockSpec(memory_space=pltpu.HBM)
    res = pl.pallas_call(
        body, name=name, grid=(gj, gi),
        in_specs=[pl.BlockSpec((tm, d), lambda j, i: (i, 0)), pl.BlockSpec((1, d), lambda j, i: (0, 0)),
                  pl.BlockSpec((d, 2 * tn), lambda j, i: (0, j))] + [hbm] * nr,
        out_specs=[pl.BlockSpec((tm, d), lambda j, i: (jnp.where(j == 0, i, gi - 1), 0)),
                   pl.BlockSpec((tm, 2 * tn), lambda j, i: (i, j)),
                   pl.BlockSpec((tm, tn), lambda j, i: (i, j))] + [hbm] * nr,
        out_shape=[jax.ShapeDtypeStruct((t, d), MXU_DTYPE), jax.ShapeDtypeStruct((t, 2 * dff), MXU_DTYPE),
                   jax.ShapeDtypeStruct((t, dff), MXU_DTYPE)] + (ride.out_shape if ride else []),
        scratch_shapes=(ride.scratch if ride else []),
        compiler_params=pltpu.CompilerParams(dimension_semantics=("arbitrary", "arbitrary"),
                                             has_side_effects=bool(ride)),
    )(x, g, w_gu, *(ride.arrays if ride else ()))
    return (res[0], res[1], res[2], res[3:]) if ride else tuple(res)


def _ffn_down_dx_swiglu(dx_out, w_down, gu, name):
    t, d = dx_out.shape
    dff = w_down.shape[0]
    tn = _gu_tile(dff)
    tm = _pick(t, (512, 256, 128))

    def body(dx_ref, w_ref, gu_ref, o_ref):
        dact = _dot(dx_ref[...], w_ref[...], (((1,), (1,)), ((), ())))
        gu_v = gu_ref[...].astype(F32)
        _, vjp = jax.vjp(_swiglu_math, gu_v[:, :tn], gu_v[:, tn:])
        dg, du = vjp(dact)
        o_ref[...] = jnp.concatenate([dg, du], axis=1).astype(o_ref.dtype)

    return pl.pallas_call(
        body, name=name, grid=(dff // tn, t // tm),
        in_specs=[pl.BlockSpec((tm, d), lambda j, i: (i, 0)), pl.BlockSpec((tn, d), lambda j, i: (j, 0)),
                  pl.BlockSpec((tm, 2 * tn), lambda j, i: (i, j))],
        out_specs=pl.BlockSpec((tm, 2 * tn), lambda j, i: (i, j)),
        out_shape=jax.ShapeDtypeStruct((t, 2 * dff), MXU_DTYPE),
        compiler_params=_params("parallel", "parallel"),
    )(dx_out, w_down, gu)


def _matmul_norm_bwd(dy, w, x, g, dres, name, ride=None):
    t, k = dy.shape
    d = w.shape[0]
    tm = _pick(t, (512, 256, 128))
    tk = k if k <= 2816 else _pick(k, (1024, 768, 512))
    gi, nk = t // tm, k // tk
    nr = ride.na if ride else 0

    def body(*refs):
        dy_ref, w_ref, x_ref, g_ref, dres_ref = refs[:5]
        ride_in = refs[5:5 + nr]
        dx_ref, dg_ref = refs[5 + nr], refs[6 + nr]
        ride_out = refs[7 + nr:7 + 2 * nr]
        acc = refs[7 + 2 * nr]
        ride_sems = refs[8 + 2 * nr:]
        i, q = pl.program_id(0), pl.program_id(1)

        @pl.when((i == 0) & (q == 0))
        def _():
            dg_ref[...] = jnp.zeros_like(dg_ref)
            if ride:
                ride.start(ride_in, ride_out, ride_sems)

        @pl.when(q == 0)
        def _():
            acc[...] = jnp.zeros_like(acc)

        acc[...] += _dot(dy_ref[...], w_ref[...], (((1,), (1,)), ((), ())))

        @pl.when(q == nk - 1)
        def _():
            _, vjp = jax.vjp(_rms_norm, x_ref[...], g_ref[...])
            dxn, dg = vjp(acc[...])
            dx_ref[...] = dres_ref[...] + dxn
            dg_ref[...] += dg

        if ride:
            @pl.when((i == gi - 1) & (q == nk - 1))
            def _():
                ride.finish(ride_in, ride_out, ride_sems)

    hbm = pl.BlockSpec(memory_space=pltpu.HBM)
    row = pl.BlockSpec((tm, d), lambda i, q: (i, 0))
    res = pl.pallas_call(
        body, name=name, grid=(gi, nk),
        in_specs=[pl.BlockSpec((tm, tk), lambda i, q: (i, q)), pl.BlockSpec((d, tk), lambda i, q: (0, q)), row,
                  pl.BlockSpec((1, d), lambda i, q: (0, 0)), row] + [hbm] * nr,
        out_specs=[row, pl.BlockSpec((1, d), lambda i, q: (0, 0))] + [hbm] * nr,
        out_shape=[jax.ShapeDtypeStruct((t, d), F32), jax.ShapeDtypeStruct((1, d), F32)]
        + (ride.out_shape if ride else []),
        scratch_shapes=[pltpu.VMEM((tm, d), F32)] + (ride.scratch if ride else []),
        compiler_params=pltpu.CompilerParams(dimension_semantics=("arbitrary", "arbitrary"),
                                             has_side_effects=bool(ride)),
    )(dy, w, x, g, dres, *(ride.arrays if ride else ()))
    return (res[0], res[1], res[2:]) if ride else (res[0], res[1])


def _tok_call(name, fn, t, tt, tok_in, const_in, tok_out, acc_out):
    nblk = t // tt
    hb = tt // HALO
    in_specs, args = [], []
    for arr, w, cb, mode in tok_in:
        if mode == "cur":
            spec = pl.BlockSpec((tt, w), lambda i, cb=cb: (i, cb))
        elif mode == "prev":
            spec = pl.BlockSpec((HALO, w), lambda i, cb=cb: (jnp.maximum(i * hb - 1, 0), cb))
        else:
            spec = pl.BlockSpec((HALO, w), lambda i, cb=cb: (jnp.minimum(i + 1, nblk - 1), cb))
        in_specs.append(spec)
        args.append(arr)
    for arr in const_in:
        in_specs.append(pl.BlockSpec(arr.shape, lambda i: (0, 0)))
        args.append(arr)
    out_specs, out_shape = [], []
    for rows, w, dt in tok_out:
        out_specs.append(pl.BlockSpec((rows, w), lambda i: (i, 0)))
        out_shape.append(jax.ShapeDtypeStruct((nblk * rows, w), dt))
    for shp, dt in acc_out:
        out_specs.append(pl.BlockSpec(shp, lambda i: (0, 0)))
        out_shape.append(jax.ShapeDtypeStruct(shp, dt))
    n_tok, n_const, n_out = len(tok_in), len(const_in), len(tok_out)

    def body(*refs):
        i = pl.program_id(0)
        tok_vals = []
        for (_, _, _, mode), r in zip(tok_in, refs[:n_tok]):
            v = r[...]
            if mode == "prev":
                v = jnp.where(i > 0, v, jnp.zeros_like(v))
            elif mode == "next8":
                v = jnp.where(i < nblk - 1, v, jnp.zeros_like(v))
            tok_vals.append(v)
        const_vals = [r[...] for r in refs[n_tok:n_tok + n_const]]
        outs, accs = fn(tok_vals, const_vals)
        o_refs = refs[n_tok + n_const:n_tok + n_const + n_out]
        a_refs = refs[n_tok + n_const + n_out:]
        for r, v in zip(o_refs, outs):
            r[...] = v.astype(r.dtype)
        if a_refs:
            @pl.when(i == 0)
            def _():
                for r in a_refs:
                    r[...] = jnp.zeros_like(r)

            for r, v in zip(a_refs, accs):
                r[...] += v.astype(r.dtype)

    res = pl.pallas_call(
        body, name=name, grid=(nblk,), in_specs=in_specs, out_specs=out_specs, out_shape=out_shape,
        compiler_params=_params("arbitrary" if acc_out else "parallel"),
    )(*args)
    return res


def _dn_pre_math(cur, halo, ba, cw, alog, dtb):
    tt = cur.shape[0]
    a = jax.nn.silu(_causal_conv(cur, halo, cw, DN_CONV))
    pieces = []
    for p in range(2 * DN_HEADS):
        xh = a[:, p * HEAD_DIM:(p + 1) * HEAD_DIM]
        xh = xh * lax.rsqrt(jnp.sum(xh * xh, axis=-1, keepdims=True) + EPS)
        if p < DN_HEADS:
            xh = xh * (HEAD_DIM ** -0.5)
        pieces.append(xh)
    pieces.append(a[:, 2 * DN_WIDTH:])
    qkvn = jnp.concatenate(pieces, axis=1)
    lane = lax.broadcasted_iota(jnp.int32, ba.shape, 1)
    raw = jnp.where(lane < DN_HEADS, jax.nn.sigmoid(ba), -jnp.exp(alog) * jax.nn.softplus(ba + dtb))
    r = lax.broadcasted_iota(jnp.int32, (tt, tt), 0)
    c = lax.broadcasted_iota(jnp.int32, (tt, tt), 1)
    tri = jnp.where((r // CHUNK == c // CHUNK) & (c <= r), 1.0, 0.0).astype(F32)
    cums = _ein("ij,jk->ik", LHS_EXACT, tri, raw)
    bg = jnp.where(lane < DN_HEADS, raw, cums)
    return qkvn, bg


def _mix_math(o, z, gb, gc, gc_halo, hv, hv_halo, dng, scg, scw):
    outs = []
    for h in range(DN_HEADS):
        sl = slice(h * HEAD_DIM, (h + 1) * HEAD_DIM)
        oh = o[:, sl]
        outs.append(oh * lax.rsqrt(jnp.mean(oh * oh, axis=-1, keepdims=True) + EPS) * dng * jax.nn.silu(z[:, sl]))
    y = gb * _causal_conv(gc * hv, gc_halo * hv_halo, scw, SC_CONV)
    gw = SC_WIDTH // SC_GROUPS
    for g in range(SC_GROUPS):
        sl = slice(g * gw, (g + 1) * gw)
        yg = y[:, sl]
        outs.append(yg * lax.rsqrt(jnp.mean(yg * yg, axis=-1, keepdims=True) + EPS) * scg[:, sl])
    return jnp.concatenate(outs, axis=1)


def _tri_inverse(a):
    c = a.shape[-1]
    r = lax.broadcasted_iota(jnp.int32, (c, c), 0)
    q = lax.broadcasted_iota(jnp.int32, (c, c), 1)
    eye = jnp.where(r == q, 1.0, 0.0).astype(F32)[None]
    blk = (r // 16 == q // 16)[None]
    d = jnp.where(blk, a, 0.0)
    o = a - d
    mm = functools.partial(_ein, "bij,bjk->bik", PRECISE)
    p = eye - d
    n = mm(d, d)
    for _ in range(2):
        both = mm(jnp.concatenate([n, p], axis=1), n)
        n = both[:, :c]
        p = p + both[:, c:]
    p = p + mm(p, n)
    e = mm(p, o)
    e2 = mm(e, e)
    left = eye - e + e2 - mm(e, e2)
    return mm(left, p)


@jax.custom_vjp
def _inverse_known(a, tinv):
    return tinv


def _inverse_known_fwd(a, tinv):
    return tinv, tinv


def _inverse_known_bwd(tinv, ct):
    left = _ein("bji,bjk->bik", PRECISE, tinv, ct)
    return -_ein("bik,bjk->bij", PRECISE, left, tinv), jnp.zeros_like(tinv)


_inverse_known.defvjp(_inverse_known_fwd, _inverse_known_bwd)


def _delta_intra_math(q, k, v, bg, head, tinv_known=None):
    n = q.shape[0]
    nb = n // CHUNK
    lane = lax.broadcasted_iota(jnp.int32, bg.shape, 1)
    beta = jnp.sum(jnp.where(lane == head, bg, 0.0), axis=1, keepdims=True).reshape(nb, CHUNK, 1)
    gc = jnp.sum(jnp.where(lane == head + DN_HEADS, bg, 0.0), axis=1, keepdims=True).reshape(nb, CHUNK, 1)
    q3, k3, v3 = (a.reshape(nb, CHUNK, HEAD_DIM) for a in (q, k, v))
    r = lax.broadcasted_iota(jnp.int32, (CHUNK, CHUNK), 0)
    c = lax.broadcasted_iota(jnp.int32, (CHUNK, CHUNK), 1)
    eye = jnp.where(r == c, 1.0, 0.0).astype(F32)[None]
    gcr = _ein("bik,bkj->bij", LHS_EXACT, jnp.ones((nb, CHUNK, CHUNK), F32), gc * eye)
    decay = jnp.exp(jnp.where((r >= c)[None], gc - gcr, -1e30))
    kb = k3 * beta
    vb = v3 * beta
    egc = jnp.exp(gc)
    a = jnp.where((r > c)[None], _ein("bcd,bmd->bcm", FAST, kb, k3) * decay, 0.0)
    tinv = _tri_inverse(a) if tinv_known is None else _inverse_known(a, tinv_known)
    uw = _ein("bcm,bmd->bcd", PRECISE, tinv, jnp.concatenate([vb, kb * egc], axis=2))
    u, w = uw[:, :, :HEAD_DIM], uw[:, :, HEAD_DIM:]
    qk = _ein("bcd,bmd->bcm", FAST, q3, k3) * decay
    row = lax.broadcasted_iota(jnp.int32, (nb, CHUNK, 1), 1)
    glast = jnp.sum(jnp.where(row == CHUNK - 1, gc, 0.0), axis=1, keepdims=True)
    qd = q3 * egc
    kd = k3 * jnp.exp(glast - gc)
    glb = jnp.broadcast_to(jnp.exp(glast), (nb, HALO, LANES))
    flat = lambda x: x.reshape(n, HEAD_DIM)
    return flat(u), flat(w), flat(qd), flat(kd), qk, glb, tinv


def _delta_step_math(u, w, qd, kd, qk, gl, s):
    vnew = u - _ein("ck,kv->cv", FAST, w, s)
    o = _ein("ck,kv->cv", FAST, qd, s) + _ein("cm,mv->cv", FAST, qk, vnew)
    s2 = s * gl + _ein("ck,cv->kv", FAST, kd, vnew)
    return o, s2


def _delta_intra(qkvn, bg, nb):
    t = qkvn.shape[0]
    n = t // CHUNK
    rows = nb * CHUNK

    def body(q_ref, k_ref, v_ref, bg_ref, u_ref, w_ref, qd_ref, kd_ref, qk_ref, gl_ref, ti_ref):
        outs = _delta_intra_math(q_ref[...], k_ref[...], v_ref[...], bg_ref[...], pl.program_id(1))
        for r, v in zip((u_ref, w_ref, qd_ref, kd_ref, qk_ref, gl_ref, ti_ref), outs):
            r[...] = v

    col = lambda off: pl.BlockSpec((rows, HEAD_DIM), lambda b, h, off=off: (b, off + h))
    tok = jax.ShapeDtypeStruct((t, DN_WIDTH), F32)
    return pl.pallas_call(
        body, name="delta_intra", grid=(n // nb, DN_HEADS),
        in_specs=[col(0), col(DN_HEADS), col(2 * DN_HEADS), pl.BlockSpec((rows, LANES), lambda b, h: (b, 0))],
        out_specs=[col(0)] * 4 + [pl.BlockSpec((nb, None, CHUNK, CHUNK), lambda b, h: (b, h, 0, 0)),
                                  pl.BlockSpec((nb, None, HALO, LANES), lambda b, h: (b, h, 0, 0)),
                                  pl.BlockSpec((nb, None, CHUNK, CHUNK), lambda b, h: (b, h, 0, 0))],
        out_shape=[tok] * 4 + [jax.ShapeDtypeStruct((n, DN_HEADS, CHUNK, CHUNK), F32),
                               jax.ShapeDtypeStruct((n, DN_HEADS, HALO, LANES), F32),
                               jax.ShapeDtypeStruct((n, DN_HEADS, CHUNK, CHUNK), F32)],
        compiler_params=_params("parallel", "arbitrary"),
    )(qkvn, qkvn, qkvn, bg)


def _delta_intra_bwd(qkvn, bg, tinv, cts, nb):
    t = qkvn.shape[0]
    n = t // CHUNK
    rows = nb * CHUNK

    def body(q_ref, k_ref, v_ref, bg_ref, ti_ref, du, dw, dqd, dkd, dqk, dgl, dq_ref, dk_ref, dv_ref, dbg_ref):
        h = pl.program_id(1)
        ti = ti_ref[...]
        _, vjp = jax.vjp(lambda q, k, v, b: _delta_intra_math(q, k, v, b, h, ti)[:6],
                         q_ref[...], k_ref[...], v_ref[...], bg_ref[...])
        dq, dk, dv, dbg = vjp((du[...], dw[...], dqd[...], dkd[...], dqk[...], dgl[...]))
        dq_ref[...] = dq
        dk_ref[...] = dk
        dv_ref[...] = dv

        @pl.when(h == 0)
        def _():
            dbg_ref[...] = jnp.zeros_like(dbg_ref)

        dbg_ref[...] += dbg

    col = lambda off: pl.BlockSpec((rows, HEAD_DIM), lambda b, h, off=off: (b, off + h))
    bgs = pl.BlockSpec((rows, LANES), lambda b, h: (b, 0))
    qks = pl.BlockSpec((nb, None, CHUNK, CHUNK), lambda b, h: (b, h, 0, 0))
    gls = pl.BlockSpec((nb, None, HALO, LANES), lambda b, h: (b, h, 0, 0))
    tok = jax.ShapeDtypeStruct((t, DN_WIDTH), F32)
    return pl.pallas_call(
        body, name="delta_intra_bwd", grid=(n // nb, DN_HEADS),
        in_specs=[col(0), col(DN_HEADS), col(2 * DN_HEADS), bgs, qks, col(0), col(0), col(0), col(0), qks, gls],
        out_specs=[col(0), col(0), col(0), bgs],
        out_shape=[tok, tok, tok, jax.ShapeDtypeStruct((t, LANES), F32)],
        compiler_params=_params("parallel", "arbitrary"),
    )(qkvn, qkvn, qkvn, bg, tinv, *cts)


def _delta_scan(u, w, qd, kd, qk, glb, cb):
    t = u.shape[0]
    n = t // CHUNK
    rows = cb * CHUNK

    def body(u_ref, w_ref, qd_ref, kd_ref, qk_ref, gl_ref, o_ref, s_ref, s_scr):
        @pl.when(pl.program_id(0) == 0)
        def _():
            s_scr[...] = jnp.zeros_like(s_scr)

        def chunk(c, carry):
            r0 = pl.multiple_of(c * CHUNK, CHUNK)
            for h in range(DN_HEADS):
                sl = (pl.ds(r0, CHUNK), slice(h * HEAD_DIM, (h + 1) * HEAD_DIM))
                s = s_scr[h]
                s_ref[c, h] = s
                o, s2 = _delta_step_math(u_ref[sl], w_ref[sl], qd_ref[sl], kd_ref[sl], qk_ref[c, h],
                                         gl_ref[c, h][0:1, :], s)
                o_ref[sl] = o
                s_scr[h] = s2
            return carry

        lax.fori_loop(0, cb, chunk, 0)

    tok = pl.BlockSpec((rows, DN_WIDTH), lambda i: (i, 0))
    return pl.pallas_call(
        body, name="delta_scan", grid=(n // cb,),
        in_specs=[tok] * 4 + [pl.BlockSpec((cb, DN_HEADS, CHUNK, CHUNK), lambda i: (i, 0, 0, 0)),
                              pl.BlockSpec((cb, DN_HEADS, HALO, LANES), lambda i: (i, 0, 0, 0))],
        out_specs=[tok, pl.BlockSpec((cb, DN_HEADS, HEAD_DIM, HEAD_DIM), lambda i: (i, 0, 0, 0))],
        out_shape=[jax.ShapeDtypeStruct((t, DN_WIDTH), F32),
                   jax.ShapeDtypeStruct((n, DN_HEADS, HEAD_DIM, HEAD_DIM), F32)],
        scratch_shapes=[pltpu.VMEM((DN_HEADS, HEAD_DIM, HEAD_DIM), F32)],
        compiler_params=_params("arbitrary"),
    )(u, w, qd, kd, qk, glb)


def _delta_scan_bwd(u, w, qd, kd, qk, glb, s_all, do, cb):
    t = u.shape[0]
    n = t // CHUNK
    nblk = n // cb
    rows = cb * CHUNK

    def body(u_ref, w_ref, qd_ref, kd_ref, qk_ref, gl_ref, s_ref, do_ref,
             du_ref, dw_ref, dqd_ref, dkd_ref, dqk_ref, dgl_ref, ds_scr):
        @pl.when(pl.program_id(0) == 0)
        def _():
            ds_scr[...] = jnp.zeros_like(ds_scr)

        def chunk(step, carry):
            c = cb - 1 - step
            r0 = pl.multiple_of(c * CHUNK, CHUNK)
            for h in range(DN_HEADS):
                sl = (pl.ds(r0, CHUNK), slice(h * HEAD_DIM, (h + 1) * HEAD_DIM))
                gl_tile = gl_ref[c, h]
                prim = (u_ref[sl], w_ref[sl], qd_ref[sl], kd_ref[sl], qk_ref[c, h], gl_tile, s_ref[c, h])
                _, vjp = jax.vjp(lambda a, b, cc, d, e, g, s: _delta_step_math(a, b, cc, d, e, g[0:1, :], s), *prim)
                du, dw, dqd, dkd, dqk, dgl, ds = vjp((do_ref[sl], ds_scr[h]))
                du_ref[sl] = du
                dw_ref[sl] = dw
                dqd_ref[sl] = dqd
                dkd_ref[sl] = dkd
                dqk_ref[c, h] = dqk
                dgl_ref[c, h] = dgl
                ds_scr[h] = ds
            return carry

        lax.fori_loop(0, cb, chunk, 0)

    rev = lambda i: nblk - 1 - i
    tok = pl.BlockSpec((rows, DN_WIDTH), lambda i: (rev(i), 0))
    qks = pl.BlockSpec((cb, DN_HEADS, CHUNK, CHUNK), lambda i: (rev(i), 0, 0, 0))
    gls = pl.BlockSpec((cb, DN_HEADS, HALO, LANES), lambda i: (rev(i), 0, 0, 0))
    ss = pl.BlockSpec((cb, DN_HEADS, HEAD_DIM, HEAD_DIM), lambda i: (rev(i), 0, 0, 0))
    tshape = jax.ShapeDtypeStruct((t, DN_WIDTH), F32)
    return pl.pallas_call(
        body, name="delta_scan_bwd", grid=(nblk,),
        in_specs=[tok] * 4 + [qks, gls, ss, tok],
        out_specs=[tok] * 4 + [qks, gls],
        out_shape=[tshape] * 4 + [jax.ShapeDtypeStruct(qk.shape, F32), jax.ShapeDtypeStruct(glb.shape, F32)],
        scratch_shapes=[pltpu.VMEM((DN_HEADS, HEAD_DIM, HEAD_DIM), F32)],
        compiler_params=_params("arbitrary"),
    )(u, w, qd, kd, qk, glb, s_all, do)


def _peer(mask):
    x, y, c = lax.axis_index("x"), lax.axis_index("y"), lax.axis_index("c")
    return (x ^ ((mask >> 2) & 1), y ^ ((mask >> 1) & 1), c ^ (mask & 1))


def _my_index():
    return 4 * lax.axis_index("x") + 2 * lax.axis_index("y") + lax.axis_index("c")


class _Exchange:
    CHIP_MASKS = (4, 2, 6)

    def __init__(self, kind, arrays):
        self.kind = kind
        self.arrays = list(arrays)
        self.na = na = len(self.arrays)
        if kind == "gather":
            self.out_shape = [jax.ShapeDtypeStruct((N_DEV,) + a.shape, a.dtype) for a in self.arrays]
        else:
            self.out_shape = [jax.ShapeDtypeStruct(a.shape, a.dtype) for a in self.arrays]
        self.scratch = [pltpu.SemaphoreType.DMA((na, 7)), pltpu.SemaphoreType.DMA((na, 7)),
                        pltpu.SemaphoreType.DMA((na,))]

    def _copies(self, ins, outs, sems):
        send_sems, recv_sems, local_sems = sems
        me = _my_index()
        local, first, passed, arrivals = [], [], [], []
        if self.kind == "gather":
            def rc(a, k, block, to, own=False):
                def make():
                    dst = outs[a].at[block]
                    return pltpu.make_async_remote_copy(src_ref=ins[a] if own else dst, dst_ref=dst,
                                                        send_sem=send_sems.at[a, k], recv_sem=recv_sems.at[a, k],
                                                        device_id=to, device_id_type=MESH)
                return make

            sib = _peer(1)
            for a in range(self.na):
                local.append(lambda a=a: pltpu.make_async_copy(ins[a], outs[a].at[me], local_sems.at[a]))
                first.append(rc(a, 0, me, sib, own=True))
                arrivals.append(rc(a, 0, me ^ 1, _peer(0)))
                for j, m in enumerate(self.CHIP_MASKS):
                    first.append(rc(a, 1 + j, me, _peer(m), own=True))
                    passed.append((rc(a, 1 + j, me ^ m, _peer(0)), rc(a, 4 + j, me ^ m, sib)))
                    arrivals.append(rc(a, 4 + j, me ^ m ^ 1, _peer(0)))
        else:
            for a in range(self.na):
                local.append(lambda a=a: pltpu.make_async_copy(ins[a].at[me], outs[a].at[me], local_sems.at[a]))
                for m in range(1, N_DEV):
                    def make(a=a, m=m):
                        return pltpu.make_async_remote_copy(
                            src_ref=ins[a].at[me ^ m], dst_ref=outs[a].at[me], send_sem=send_sems.at[a, m - 1],
                            recv_sem=recv_sems.at[a, m - 1], device_id=_peer(m), device_id_type=MESH)
                    first.append(make)
                    arrivals.append(make)
        return local, first, passed, arrivals

    def start(self, ins, outs, sems):
        local, first, _, _ = self._copies(ins, outs, sems)
        for make in local + first:
            make().start()

    def finish(self, ins, outs, sems):
        local, first, passed, arrivals = self._copies(ins, outs, sems)
        for landed, onward in passed:
            landed().wait_recv()
            onward().start()
        for make in arrivals:
            make().wait_recv()
        for make in first + [p for _, p in passed]:
            make().wait_send()
        for make in local:
            make().wait()

    def run(self, name):
        na = self.na

        def body(*refs):
            ins, outs, sems = refs[:na], refs[na:2 * na], refs[2 * na:]
            self.start(ins, outs, sems)
            self.finish(ins, outs, sems)

        hbm = pl.BlockSpec(memory_space=pltpu.HBM)
        return pl.pallas_call(
            body, name=name, in_specs=[hbm] * na, out_specs=[hbm] * na, out_shape=self.out_shape,
            scratch_shapes=self.scratch, compiler_params=pltpu.CompilerParams(has_side_effects=True),
        )(*self.arrays)


def _all_gather(shards, name):
    return _Exchange("gather", shards).run(name)


def _adamw_math(w, g, m, v):
    m2 = ADAM_B1 * m + (1.0 - ADAM_B1) * g
    v2 = ADAM_B2 * v + (1.0 - ADAM_B2) * jnp.square(g)
    m_hat = m2 / (1.0 - ADAM_B1 ** ADAM_STEP)
    v_hat = v2 / (1.0 - ADAM_B2 ** ADAM_STEP)
    delta = -ADAM_LR * (m_hat / (jnp.sqrt(v_hat) + ADAM_EPS) + ADAM_WD * w)
    return delta, m2, v2


def _sum_adamw(parts, w, m, v, name):
    r, c = w.shape
    tr = _pick(r, (512, 256, 352, 128, 64, 32, 16, 8))
    np_ = parts.shape[0]

    def body(p_ref, w_ref, m_ref, v_ref, g_ref, d_ref, m2_ref, v2_ref):
        g = p_ref[0].astype(F32)
        for d in range(1, np_):
            g = g + p_ref[d].astype(F32)
        delta, m2, v2 = _adamw_math(w_ref[...], g, m_ref[...], v_ref[...])
        g_ref[...] = g
        d_ref[...] = delta
        m2_ref[...] = m2
        v2_ref[...] = v2

    blk = pl.BlockSpec((tr, c), lambda i: (i, 0))
    shp = jax.ShapeDtypeStruct((r, c), F32)
    return pl.pallas_call(
        body, name=name, grid=(r // tr,),
        in_specs=[pl.BlockSpec((np_, tr, c), lambda i: (0, i, 0)), blk, blk, blk],
        out_specs=[blk] * 4, out_shape=[shp] * 4, compiler_params=_params("parallel"),
    )(parts, w, m, v)


def _sum_rows(parts, name):
    np_, r, c = parts.shape

    def body(p_ref, o_ref):
        g = p_ref[0]
        for d in range(1, np_):
            g = g + p_ref[d]
        o_ref[...] = g

    return pl.pallas_call(body, name=name, out_shape=jax.ShapeDtypeStruct((r, c), F32))(parts)


def _pad_w_in(w):
    d = w.shape[0]
    n_ba = 2 * DN_HEADS
    a = w[:, :SC_OFF]
    ba = w[:, SC_OFF:SC_OFF + n_ba]
    sc = w[:, SC_OFF + n_ba:]
    return jnp.concatenate([a, sc, ba, jnp.zeros((d, BA_W - n_ba), w.dtype)], axis=1)


def _unpad_w_in(wp):
    n_ba = 2 * DN_HEADS
    return jnp.concatenate([wp[:, :SC_OFF], wp[:, BA_OFF:BA_OFF + n_ba], wp[:, SC_OFF:BA_OFF]], axis=1)


def _lane_row(v, off):
    return jnp.pad(v.astype(F32), (off, LANES - off - v.shape[0]))[None]


TT = 256
NB_INTRA = 8
CB_SCAN = 8


def _layer_fwd(x, p, rides):
    t, d = x.shape
    got = {}
    if rides:
        h, proj, got["proj"] = _norm_matmul(x, p["norm1_g"], p["w_in"], "proj_fwd_gather", ride=rides["proj"])
    else:
        h, proj = _norm_matmul(x, p["norm1_g"], p["w_in"], "proj_fwd")
    qkvn, bg = _tok_call(
        "dn_pre", lambda tv, cv: (_dn_pre_math(*tv, *cv), ()), t, TT,
        [(proj, QKV_W, 0, "cur"), (proj, QKV_W, 0, "prev"), (proj, LANES, BA_OFF // LANES, "cur")],
        [p["dn_conv_w"], p["alog_row"], p["dt_row"]],
        [(TT, QKV_W, F32), (TT, LANES, F32)], [])
    u, w, qd, kd, qk, glb, tinv = _delta_intra(qkvn, bg, NB_INTRA)
    o, s_all = _delta_scan(u, w, qd, kd, qk, glb, CB_SCAN)
    cb0 = SC_OFF // SC_WIDTH
    mix_in = [(o, DN_WIDTH, 0, "cur"), (proj, DN_WIDTH, Z_OFF // DN_WIDTH, "cur"),
              (proj, SC_WIDTH, cb0, "cur"), (proj, SC_WIDTH, cb0 + 1, "cur"), (proj, SC_WIDTH, cb0 + 1, "prev"),
              (proj, SC_WIDTH, cb0 + 2, "cur"), (proj, SC_WIDTH, cb0 + 2, "prev")]
    mix_const = [p["dn_norm_g"], p["sc_norm_g"], p["sc_conv_w"]]
    (cat,) = _tok_call("mix_post", lambda tv, cv: ((_mix_math(*tv, *cv),), ()), t, TT,
                       mix_in, mix_const, [(TT, 2 * DN_WIDTH, MXU_DTYPE)], [])
    x_mid = _matmul(cat, p["w_out"], "nn", F32, "out_proj", residual=x)
    if rides:
        h2, gu, act, got["ffn"] = _ffn_up_swiglu(x_mid, p["norm2_g"], p["w_gu"], "ffn_up_gather", ride=rides["ffn"])
    else:
        h2, gu, act = _ffn_up_swiglu(x_mid, p["norm2_g"], p["w_gu"], "ffn_up")
    x_out = _matmul(act, p["w_down"], "nn", F32, "ffn_down", residual=x_mid)
    saved = dict(x=x, h=h, proj=proj, qkvn=qkvn, bg=bg, u=u, w=w, qd=qd, kd=kd, qk=qk, glb=glb, tinv=tinv, s_all=s_all, o=o,
                 cat=cat, x_mid=x_mid, h2=h2, gu=gu, act=act, mix_in=mix_in, mix_const=mix_const)
    return x_out, saved, got


def _layer_bwd(dx_out, p, s, rides):
    t, d = dx_out.shape
    got = {}
    dgu = _ffn_down_dx_swiglu(dx_out, p["w_down"], s["gu"], "ffn_down_dx")
    d_w_down = _matmul(s["act"], dx_out, "tn", MXU_DTYPE, "ffn_down_dw")
    if rides:
        dx_mid, d_norm2, got["gu"] = _matmul_norm_bwd(dgu, p["w_gu"], s["x_mid"], p["norm2_g"], dx_out,
                                                      "ffn_up_dx_scatter", ride=rides["gu"])
        d_w_gu, got["win"] = _matmul(s["h2"], dgu, "tn", MXU_DTYPE, "ffn_up_dw_scatter", ride=rides["win"])
    else:
        dx_mid, d_norm2 = _matmul_norm_bwd(dgu, p["w_gu"], s["x_mid"], p["norm2_g"], dx_out, "ffn_up_dx")
        d_w_gu = _matmul(s["h2"], dgu, "tn", MXU_DTYPE, "ffn_up_dw")
    dcat = _matmul(dx_mid, p["w_out"], "nt", F32, "out_proj_dx")
    d_w_out = _matmul(s["cat"], dx_mid, "tn", MXU_DTYPE, "out_proj_dw")

    def mix_bwd(tv, cv):
        prim = tuple(tv[:7]) + tuple(cv)
        _, vjp = jax.vjp(_mix_math, *prim)
        do, dz, dgb, dgc, dgch, dhv, dhvh, ddng, dscg, dscw = vjp(tv[7])
        return (do, dz, dgb, dgc, dgch, dhv, dhvh), (ddng, dscg, dscw)

    wide = (TT, DN_WIDTH, F32)
    halo = (HALO, SC_WIDTH, F32)
    do, dz, dgb, dgc, dgc_h, dhv, dhv_h, d_dn_norm, d_sc_norm, d_sc_conv = _tok_call(
        "mix_post_bwd", mix_bwd, t, TT, s["mix_in"] + [(dcat, 2 * DN_WIDTH, 0, "cur")], s["mix_const"],
        [wide, wide, wide, wide, halo, wide, halo],
        [((1, HEAD_DIM), F32), ((1, SC_WIDTH), F32), ((SC_CONV, SC_WIDTH), F32)])
    cts = _delta_scan_bwd(s["u"], s["w"], s["qd"], s["kd"], s["qk"], s["glb"], s["s_all"], do, CB_SCAN)
    dq, dk, dv, dbg = _delta_intra_bwd(s["qkvn"], s["bg"], s["tinv"], cts, NB_INTRA)
    proj = s["proj"]

    def dn_pre_bwd(tv, cv):
        cur, hal, ba, dq_, dk_, dv_, dbg_ = tv
        _, vjp = jax.vjp(_dn_pre_math, cur, hal, ba, *cv)
        dcur, dhal, dba, dcw, dal, ddt = vjp((jnp.concatenate([dq_, dk_, dv_], axis=1), dbg_))
        return (dcur, dhal, dba), (dcw, dal, ddt)

    dqkv, dqkv_h, dba, d_dn_conv, d_alog, d_dt = _tok_call(
        "dn_pre_bwd", dn_pre_bwd, t, TT,
        [(proj, QKV_W, 0, "cur"), (proj, QKV_W, 0, "prev"), (proj, LANES, BA_OFF // LANES, "cur"),
         (dq, DN_WIDTH, 0, "cur"), (dk, DN_WIDTH, 0, "cur"), (dv, DN_WIDTH, 0, "cur"), (dbg, LANES, 0, "cur")],
        [p["dn_conv_w"], p["alog_row"], p["dt_row"]],
        [(TT, QKV_W, F32), (HALO, QKV_W, F32), (TT, LANES, F32)],
        [((DN_CONV, QKV_W), F32), ((1, LANES), F32), ((1, LANES), F32)])

    def assemble(tv, cv):
        dqkv_, dqkv_n, dz_, dgb_, dgc_, dgc_n, dhv_, dhv_n, dba_ = tv

        def with_halo(cur, nxt):
            return cur + jnp.concatenate([jnp.zeros((TT - HALO, cur.shape[1]), F32), nxt], axis=0)

        out = jnp.concatenate([with_halo(dqkv_, dqkv_n), dz_, dgb_, with_halo(dgc_, dgc_n), with_halo(dhv_, dhv_n),
                               dba_, jnp.zeros((TT, BA_W - LANES), F32)], axis=1)
        return (out,), ()

    (dproj,) = _tok_call(
        "dproj_assemble", assemble, t, TT,
        [(dqkv, QKV_W, 0, "cur"), (dqkv_h, QKV_W, 0, "next8"), (dz, DN_WIDTH, 0, "cur"), (dgb, SC_WIDTH, 0, "cur"),
         (dgc, SC_WIDTH, 0, "cur"), (dgc_h, SC_WIDTH, 0, "next8"), (dhv, SC_WIDTH, 0, "cur"),
         (dhv_h, SC_WIDTH, 0, "next8"), (dba, LANES, 0, "cur")], [],
        [(TT, PROJ_W, MXU_DTYPE)], [])
    if rides:
        dx_in, d_norm1, got["rest"] = _matmul_norm_bwd(dproj, p["w_in"], s["x"], p["norm1_g"], dx_mid,
                                                       "proj_dx_scatter", ride=rides["rest"])
    else:
        dx_in, d_norm1 = _matmul_norm_bwd(dproj, p["w_in"], s["x"], p["norm1_g"], dx_mid, "proj_dx")
    d_w_in = _matmul(s["h"], dproj, "tn", MXU_DTYPE, "proj_dw")
    grads = dict(w_in=d_w_in, w_out=d_w_out, w_gu=d_w_gu, w_down=d_w_down, norm1_g=d_norm1, norm2_g=d_norm2,
                 dn_norm_g=d_dn_norm, sc_norm_g=d_sc_norm, sc_conv_w=d_sc_conv, dn_conv_w=d_dn_conv,
                 alog=d_alog, dt=d_dt)
    return dx_in, grads, got


def _final_loss(x, g, target):
    t, d = x.shape

    def fn(tv, cv):
        xv, tg = tv

        def loss_fn(xx, gg):
            err = jnp.square(_rms_norm(xx, gg) - tg)
            return 0.5 * jnp.sum(jnp.mean(err, axis=-1))

        loss, vjp = jax.vjp(loss_fn, xv, cv[0])
        dx, dg = vjp(jnp.ones((), F32))
        return (dx,), (jnp.full((1, LANES), loss, F32), dg)

    return _tok_call("final_loss", fn, t, TT, [(x, d, 0, "cur"), (target, d, 0, "cur")], [g],
                     [(TT, d, F32)], [((1, LANES), F32), ((1, d), F32)])


def _pack_rows(arrs):
    rows, offs, r0 = [], [], 0
    for a in arrs:
        n = a.size
        nr = -(-n // LANES)
        flat = jnp.pad(a.reshape(-1).astype(F32), (0, nr * LANES - n))
        rows.append(flat.reshape(nr, LANES))
        offs.append((r0, nr, a.shape))
        r0 += nr
    pad = (-r0) % 8
    if pad:
        rows.append(jnp.zeros((pad, LANES), F32))
    return jnp.concatenate(rows, axis=0), offs


def _unpack_rows(packed, offs):
    out = []
    for r0, nr, shp in offs:
        n = 1
        for s_ in shp:
            n *= s_
        out.append(packed[r0:r0 + nr].reshape(-1)[:n].reshape(shp))
    return out


def kernel(x, norm1_g, w_in, dn_conv_w, dn_a_log, dn_dt_bias, dn_norm_g, sc_conv_w, sc_norm_g, w_out, norm2_g, ffn_w_gate, ffn_w_up, ffn_w_down, final_norm_g, loss_target, m_norm1_g, m_w_in, m_dn_conv_w, m_dn_a_log, m_dn_dt_bias, m_dn_norm_g, m_sc_conv_w, m_sc_norm_g, m_w_out, m_norm2_g, m_ffn_w_gate, m_ffn_w_up, m_ffn_w_down, m_final_norm_g, v_norm1_g, v_w_in, v_dn_conv_w, v_dn_a_log, v_dn_dt_bias, v_dn_norm_g, v_sc_conv_w, v_sc_norm_g, v_w_out, v_norm2_g, v_ffn_w_gate, v_ffn_w_up, v_ffn_w_down, v_final_norm_g):
    depth, d, cin = w_in.shape
    t = x.shape[1]
    dff_s = ffn_w_gate.shape[2]
    me = _my_index()
    x2 = x.reshape(t, d)
    tgt = loss_target.reshape(t, d)

    conv_pack, conv_offs = _pack_rows([dn_conv_w, sc_conv_w])
    (conv_all,) = _all_gather([conv_pack], "gather_conv")
    dn_parts, sc_parts = zip(*[_unpack_rows(conv_all[j], conv_offs) for j in range(N_DEV)])
    dn_conv_full = jnp.concatenate(dn_parts, axis=2)
    sc_conv_full = jnp.concatenate(sc_parts, axis=2)

    def shards(l):
        return [a[l].astype(MXU_DTYPE) for a in (w_in, w_out, ffn_w_gate, ffn_w_up, ffn_w_down)]

    def layer_params(l, g_in, g_out, g_gate, g_up, g_down):
        full_in = g_in.transpose(1, 0, 2).reshape(d, N_DEV * cin)
        full_gate = g_gate.transpose(1, 0, 2).reshape(d, N_DEV * dff_s)
        full_up = g_up.transpose(1, 0, 2).reshape(d, N_DEV * dff_s)
        return dict(
            w_in=_pad_w_in(full_in), w_out=g_out.reshape(d, d),
            w_gu=_interleave_gu(full_gate, full_up), w_down=g_down.reshape(N_DEV * dff_s, d),
            norm1_g=norm1_g[l][None], norm2_g=norm2_g[l][None], dn_norm_g=dn_norm_g[l][None],
            sc_norm_g=sc_norm_g[l][None], dn_conv_w=dn_conv_full[l], sc_conv_w=sc_conv_full[l],
            alog_row=_lane_row(dn_a_log[l], DN_HEADS), dt_row=_lane_row(dn_dt_bias[l], DN_HEADS))

    params = [layer_params(0, *_all_gather(shards(0), "gather_weights"))]
    saved = []
    xc = x2
    for l in range(depth):
        rides = None
        if l + 1 < depth:
            nxt = shards(l + 1)
            rides = dict(proj=_Exchange("gather", nxt[:2]), ffn=_Exchange("gather", nxt[2:]))
        xc, s, got = _layer_fwd(xc, params[l], rides)
        saved.append(s)
        if rides:
            params.append(layer_params(l + 1, *got["proj"], *got["ffn"]))
    dx, loss_part, d_final = _final_loss(xc, final_norm_g[None], tgt)

    names = ("w_in", "w_out", "ffn_w_gate", "ffn_w_up", "ffn_w_down")
    big_out = {n: {k: [None] * depth for k in ("g", "d", "m", "v")} for n in names}
    w_loc = dict(w_in=w_in, w_out=w_out, ffn_w_gate=ffn_w_gate, ffn_w_up=ffn_w_up, ffn_w_down=ffn_w_down)
    m_loc = dict(w_in=m_w_in, w_out=m_w_out, ffn_w_gate=m_ffn_w_gate, ffn_w_up=m_ffn_w_up, ffn_w_down=m_ffn_w_down)
    v_loc = dict(w_in=v_w_in, w_out=v_w_out, ffn_w_gate=v_ffn_w_gate, ffn_w_up=v_ffn_w_up, ffn_w_down=v_ffn_w_down)

    def parts_of(g):
        cols = lambda a, c: a.reshape(a.shape[0], N_DEV, c).transpose(1, 0, 2)
        g_gate, g_up = _split_gu(g["w_gu"])
        parts = [cols(_unpad_w_in(g["w_in"]), cin), g["w_out"].reshape(N_DEV, d // N_DEV, d),
                 cols(g_gate, dff_s), cols(g_up, dff_s), g["w_down"].reshape(N_DEV, dff_s, d)]
        return [a.astype(MXU_DTYPE) for a in parts]

    def apply(l, recv):
        for n, r in zip(names, recv):
            res = _sum_adamw(r, w_loc[n][l], m_loc[n][l], v_loc[n][l], "adamw_" + n)
            for k, a in zip(("g", "d", "m", "v"), res):
                big_out[n][k][l] = a

    grads = [None] * depth
    pending = None
    for l in reversed(range(depth)):
        rides = None
        if pending is not None:
            rides = dict(gu=_Exchange("scatter", pending[2:4]), win=_Exchange("scatter", pending[0:1]),
                         rest=_Exchange("scatter", [pending[4], pending[1]]))
        dx, grads[l], got = _layer_bwd(dx, params[l], saved[l], rides)
        if rides:
            apply(l + 1, [got["win"][0], got["rest"][1], got["gu"][0], got["gu"][1], got["rest"][0]])
        pending = parts_of(grads[l])
    apply(0, _Exchange("scatter", pending).run("scatter_grads"))
    grad_x = dx.reshape(x.shape)
    big_out = {n: {k: jnp.stack(v_) for k, v_ in o.items()} for n, o in big_out.items()}

    stack = lambda key: jnp.stack([grads[l][key] for l in range(depth)])
    small_parts = [stack("norm1_g").reshape(depth, d), stack("norm2_g").reshape(depth, d), d_final.reshape(d),
                   stack("dn_norm_g").reshape(depth, HEAD_DIM), stack("sc_norm_g").reshape(depth, SC_WIDTH),
                   stack("alog").reshape(depth, LANES), stack("dt").reshape(depth, LANES),
                   stack("dn_conv_w"), stack("sc_conv_w"), loss_part]
    small_pack, small_offs = _pack_rows(small_parts)
    (small_all,) = _all_gather([small_pack], "gather_small")
    total = _sum_rows(small_all, "sum_small")
    (g_n1, g_n2, g_fin, g_dnn, g_scn, g_alog, g_dt, g_dnc, g_scc, loss_row) = _unpack_rows(total, small_offs)
    loss = loss_row[0, 0]
    g_alog = g_alog[:, DN_HEADS:2 * DN_HEADS]
    g_dt = g_dt[:, DN_HEADS:2 * DN_HEADS]
    dnc_w = dn_conv_w.shape[2]
    scc_w = sc_conv_w.shape[2]
    g_dnc = lax.dynamic_slice_in_dim(g_dnc, me * dnc_w, dnc_w, axis=2)
    g_scc = lax.dynamic_slice_in_dim(g_scc, me * scc_w, scc_w, axis=2)
    sm_g = [g_n1, g_dnc, g_alog, g_dt, g_dnn, g_scc, g_scn, g_n2, g_fin]
    sm_w = [norm1_g, dn_conv_w, dn_a_log, dn_dt_bias, dn_norm_g, sc_conv_w, sc_norm_g, norm2_g, final_norm_g]
    sm_m = [m_norm1_g, m_dn_conv_w, m_dn_a_log, m_dn_dt_bias, m_dn_norm_g, m_sc_conv_w, m_sc_norm_g, m_norm2_g, m_final_norm_g]
    sm_v = [v_norm1_g, v_dn_conv_w, v_dn_a_log, v_dn_dt_bias, v_dn_norm_g, v_sc_conv_w, v_sc_norm_g, v_norm2_g, v_final_norm_g]
    pg, offs = _pack_rows(sm_g)
    pw, _ = _pack_rows(sm_w)
    pm, _ = _pack_rows(sm_m)
    pv, _ = _pack_rows(sm_v)
    sg, sd, sm_, sv = _sum_adamw(pg[None], pw, pm, pv, "adamw_small")
    small_out = {k: _unpack_rows(a, offs) for k, a in zip(("g", "d", "m", "v"), (sg, sd, sm_, sv))}

    def outputs(k):
        s_ = small_out[k]
        b = big_out
        return [s_[0], b["w_in"][k], s_[1], s_[2], s_[3], s_[4], s_[5], s_[6], b["w_out"][k], s_[7],
                b["ffn_w_gate"][k], b["ffn_w_up"][k], b["ffn_w_down"][k], s_[8]]

    return (loss, grad_x, *outputs("g"), *outputs("d"), *outputs("m"), *outputs("v"))
```

```python
import functools

import jax
import jax.numpy as jnp
from jax import lax
from jax.experimental import pallas as pl
from jax.experimental.pallas import tpu as pltpu

F32 = jnp.float32
MXU_DTYPE = jnp.bfloat16
MESH = pl.DeviceIdType.MESH

N_DEV = 8
EPS = 1e-6
DN_HEADS = 4
HEAD_DIM = 128
DN_WIDTH = DN_HEADS * HEAD_DIM
SC_WIDTH = 512
SC_GROUPS = 4
DN_CONV = 4
SC_CONV = 3
CHUNK = 64
HALO = 8
LANES = 128

QKV_W = 3 * DN_WIDTH
Z_OFF = QKV_W
SC_OFF = Z_OFF + DN_WIDTH
BA_OFF = SC_OFF + 3 * SC_WIDTH
BA_W = 256
PROJ_W = BA_OFF + BA_W

ADAM_LR = 0.001
ADAM_B1 = 0.9
ADAM_B2 = 0.999
ADAM_EPS = 1e-08
ADAM_WD = 0.01
ADAM_STEP = 10


def _pick(n, cands):
    for c in cands:
        if n % c == 0:
            return c
    return n


def _params(*sem):
    return pltpu.CompilerParams(dimension_semantics=sem)


def _rms_norm(x, g):
    return x * lax.rsqrt(jnp.mean(x * x, axis=-1, keepdims=True) + EPS) * g


def _dot(a, b, dims=(((1,), (0,)), ((), ()))):
    return lax.dot_general(a.astype(MXU_DTYPE), b.astype(MXU_DTYPE), dims, preferred_element_type=F32)


def _split_terms(x, terms):
    out = []
    for _ in range(terms):
        hi = x.astype(MXU_DTYPE)
        out.append(hi)
        x = x - hi.astype(F32)
    return out


def _ein_impl(spec, terms, a, b):
    ta, tb = terms
    if ta == 1 and tb == 1:
        return jnp.einsum(spec, a.astype(MXU_DTYPE), b.astype(MXU_DTYPE), preferred_element_type=F32)
    pa, pb = _split_terms(a, ta), _split_terms(b, tb)
    order = max(ta, tb) - 1
    acc = None
    for deg in range(order, -1, -1):
        for i in range(ta):
            j = deg - i
            if 0 <= j < tb:
                t = jnp.einsum(spec, pa[i], pb[j], preferred_element_type=F32)
                acc = t if acc is None else acc + t
    return acc


@functools.partial(jax.custom_vjp, nondiff_argnums=(0, 1))
def _ein(spec, terms, a, b):
    return _ein_impl(spec, terms, a, b)


def _ein_fwd(spec, terms, a, b):
    return _ein_impl(spec, terms, a, b), (a, b)


def _ein_bwd(spec, terms, res, ct):
    a, b = res
    xy, z = spec.split("->")
    x, y = xy.split(",")
    tc = min(max(terms), 2)
    da = _ein_impl(f"{z},{y}->{x}", (tc, terms[1]), ct, b)
    db = _ein_impl(f"{x},{z}->{y}", (terms[0], tc), a, ct)
    return da, db


_ein.defvjp(_ein_fwd, _ein_bwd)

FAST = (1, 1)
PRECISE = (2, 2)
LHS_EXACT = (1, 3)


def _causal_conv(cur, halo, w, k):
    tt = cur.shape[0]
    xp = jnp.concatenate([halo, cur], axis=0)
    y = None
    for j in range(k):
        start = HALO - (k - 1) + j
        term = xp[start:start + tt] * w[j:j + 1]
        y = term if y is None else y + term
    return y


def _matmul(a, b, mode, out_dtype, name, residual=None, ride=None):
    if mode == "nn":
        (m, k), (k2, n) = a.shape, b.shape
    elif mode == "nt":
        (m, k), (n, k2) = a.shape, b.shape
    else:
        (k, m), (k2, n) = a.shape, b.shape
    assert k == k2
    tm = _pick(m, (1024, 1408, 512, 256, 128))
    tn = _pick(n, (1280, 1408, 1024, 512, 256, 128))
    tk = k if k <= 2816 else _pick(k, (1024, 768, 512))
    gi, gj, nk = m // tm, n // tn, k // tk
    dims = {"nn": (((1,), (0,)), ((), ())), "nt": (((1,), (1,)), ((), ())), "tn": (((0,), (0,)), ((), ()))}[mode]
    a_spec = {"nn": pl.BlockSpec((tm, tk), lambda i, j, q: (i, q)),
              "nt": pl.BlockSpec((tm, tk), lambda i, j, q: (i, q)),
              "tn": pl.BlockSpec((tk, tm), lambda i, j, q: (q, i))}[mode]
    b_spec = {"nn": pl.BlockSpec((tk, tn), lambda i, j, q: (q, j)),
              "nt": pl.BlockSpec((tn, tk), lambda i, j, q: (j, q)),
              "tn": pl.BlockSpec((tk, tn), lambda i, j, q: (q, j))}[mode]
    o_spec = pl.BlockSpec((tm, tn), lambda i, j, q: (i, j))
    has_res = residual is not None
    n_in = 3 if has_res else 2
    nr = ride.na if ride else 0

    def body(*refs):
        a_ref, b_ref = refs[0], refs[1]
        r_ref = refs[2] if has_res else None
        ride_in = refs[n_in:n_in + nr]
        o_ref = refs[n_in + nr]
        ride_out = refs[n_in + nr + 1:n_in + 2 * nr + 1]
        acc = refs[n_in + 2 * nr + 1]
        ride_sems = refs[n_in + 2 * nr + 2:]
        i, j, q = pl.program_id(0), pl.program_id(1), pl.program_id(2)
        if ride:
            @pl.when((i == 0) & (j == 0) & (q == 0))
            def _():
                ride.start(ride_in, ride_out, ride_sems)

        @pl.when(q == 0)
        def _():
            acc[...] = jnp.zeros_like(acc)

        acc[...] += _dot(a_ref[...], b_ref[...], dims)

        @pl.when(q == nk - 1)
        def _():
            r = acc[...]
            if has_res:
                r = r + r_ref[...]
            o_ref[...] = r.astype(o_ref.dtype)

        if ride:
            @pl.when((i == gi - 1) & (j == gj - 1) & (q == nk - 1))
            def _():
                ride.finish(ride_in, ride_out, ride_sems)

    hbm = pl.BlockSpec(memory_space=pltpu.HBM)
    in_specs = [a_spec, b_spec] + ([o_spec] if has_res else []) + [hbm] * nr
    args = (a, b) + ((residual,) if has_res else ()) + (tuple(ride.arrays) if ride else ())
    res = pl.pallas_call(
        body, name=name, grid=(gi, gj, nk), in_specs=in_specs, out_specs=[o_spec] + [hbm] * nr,
        out_shape=[jax.ShapeDtypeStruct((m, n), out_dtype)] + (ride.out_shape if ride else []),
        scratch_shapes=[pltpu.VMEM((tm, tn), F32)] + (ride.scratch if ride else []),
        compiler_params=pltpu.CompilerParams(
            dimension_semantics=("arbitrary",) * 3 if ride else ("parallel", "parallel", "arbitrary"),
            has_side_effects=bool(ride)),
    )(*args)
    return (res[0], res[1:]) if ride else res[0]


def _norm_matmul(x, g, w, name, ride=None):
    t, d = x.shape
    n = w.shape[1]
    tm = _pick(t, (1024, 512, 256, 128))
    tn = _pick(n, (1280, 1408, 1024, 512, 256, 128))
    gi, gj = t // tm, n // tn
    nr = ride.na if ride else 0

    def body(*refs):
        x_ref, g_ref, w_ref = refs[:3]
        ride_in = refs[3:3 + nr]
        h_ref, y_ref = refs[3 + nr], refs[4 + nr]
        ride_out = refs[5 + nr:5 + 2 * nr]
        h_scr = refs[5 + 2 * nr]
        ride_sems = refs[6 + 2 * nr:]
        i, j = pl.program_id(0), pl.program_id(1)
        if ride:
            @pl.when((i == 0) & (j == 0))
            def _():
                ride.start(ride_in, ride_out, ride_sems)

        @pl.when(j == 0)
        def _():
            h = _rms_norm(x_ref[...], g_ref[...]).astype(MXU_DTYPE)
            h_scr[...] = h
            h_ref[...] = h

        y_ref[...] = _dot(h_scr[...], w_ref[...])

        if ride:
            @pl.when((i == gi - 1) & (j == gj - 1))
            def _():
                ride.finish(ride_in, ride_out, ride_sems)

    hbm = pl.BlockSpec(memory_space=pltpu.HBM)
    res = pl.pallas_call(
        body, name=name, grid=(gi, gj),
        in_specs=[pl.BlockSpec((tm, d), lambda i, j: (i, 0)), pl.BlockSpec((1, d), lambda i, j: (0, 0)),
                  pl.BlockSpec((d, tn), lambda i, j: (0, j))] + [hbm] * nr,
        out_specs=[pl.BlockSpec((tm, d), lambda i, j: (i, 0)), pl.BlockSpec((tm, tn), lambda i, j: (i, j))] + [hbm] * nr,
        out_shape=[jax.ShapeDtypeStruct((t, d), MXU_DTYPE), jax.ShapeDtypeStruct((t, n), F32)]
        + (ride.out_shape if ride else []),
        scratch_shapes=[pltpu.VMEM((tm, d), MXU_DTYPE)] + (ride.scratch if ride else []),
        compiler_params=pltpu.CompilerParams(
            dimension_semantics=("arbitrary",) * 2 if ride else ("parallel", "arbitrary"),
            has_side_effects=bool(ride)),
    )(x, g, w, *(ride.arrays if ride else ()))
    return (res[0], res[1], res[2:]) if ride else (res[0], res[1])


def _swiglu_math(g, u):
    return jax.nn.silu(g) * u


def _gu_tile(dff):
    return _pick(dff, (1408, 1024, 512, 256, 128))


def _interleave_gu(gate, up):
    tn = _gu_tile(gate.shape[1])
    pieces = []
    for j in range(gate.shape[1] // tn):
        pieces += [gate[:, j * tn:(j + 1) * tn], up[:, j * tn:(j + 1) * tn]]
    return jnp.concatenate(pieces, axis=1)


def _split_gu(gu):
    dff = gu.shape[1] // 2
    tn = _gu_tile(dff)
    tiles = [gu[:, j * tn:(j + 1) * tn] for j in range(2 * dff // tn)]
    return jnp.concatenate(tiles[0::2], axis=1), jnp.concatenate(tiles[1::2], axis=1)


def _ffn_up_swiglu(x, g, w_gu, name, ride=None):
    t, d = x.shape
    dff = w_gu.shape[1] // 2
    tn = _gu_tile(dff)
    tm = _pick(t, (512, 256, 128))
    gj, gi = dff // tn, t // tm
    nr = ride.na if ride else 0

    def body(*refs):
        x_ref, g_ref, w_ref = refs[:3]
        ride_in = refs[3:3 + nr]
        h_ref, gu_ref, act_ref = refs[3 + nr:6 + nr]
        ride_out = refs[6 + nr:6 + 2 * nr]
        ride_sems = refs[6 + 2 * nr:]
        j, i = pl.program_id(0), pl.program_id(1)
        if ride:
            @pl.when((i == 0) & (j == 0))
            def _():
                ride.start(ride_in, ride_out, ride_sems)

        h = _rms_norm(x_ref[...], g_ref[...]).astype(MXU_DTYPE)

        @pl.when(j == 0)
        def _():
            h_ref[...] = h

        y = _dot(h, w_ref[...])
        gu_ref[...] = y.astype(gu_ref.dtype)
        act_ref[...] = _swiglu_math(y[:, :tn], y[:, tn:]).astype(act_ref.dtype)

        if ride:
            @pl.when((i == gi - 1) & (j == gj - 1))
            def _():
                ride.finish(ride_in, ride_out, ride_sems)

    hbm = pl.BlockSpec(memory_space=pltpu.HBM)
    res = pl.pallas_call(
        body, name=name, grid=(gj, gi),
        in_specs=[pl.BlockSpec((tm, d), lambda j, i: (i, 0)), pl.BlockSpec((1, d), lambda j, i: (0, 0)),
                  pl.BlockSpec((d, 2 * tn), lambda j, i: (0, j))] + [hbm] * nr,
        out_specs=[pl.BlockSpec((tm, d), lambda j, i: (jnp.where(j == 0, i, gi - 1), 0)),
                   pl.BlockSpec((tm, 2 * tn), lambda j, i: (i, j)),
                   pl.BlockSpec((tm, tn), lambda j, i: (i, j))] + [hbm] * nr,
        out_shape=[jax.ShapeDtypeStruct((t, d), MXU_DTYPE), jax.ShapeDtypeStruct((t, 2 * dff), MXU_DTYPE),
                   jax.ShapeDtypeStruct((t, dff), MXU_DTYPE)] + (ride.out_shape if ride else []),
        scratch_shapes=(ride.scratch if ride else []),
        compiler_params=pltpu.CompilerParams(dimension_semantics=("arbitrary", "arbitrary"),
                                             has_side_effects=bool(ride)),
    )(x, g, w_gu, *(ride.arrays if ride else ()))
    return (res[0], res[1], res[2], res[3:]) if ride else tuple(res)


def _ffn_down_dx_swiglu(dx_out, w_down, gu, name):
    t, d = dx_out.shape
    dff = w_down.shape[0]
    tn = _gu_tile(dff)
    tm = _pick(t, (512, 256, 128))

    def body(dx_ref, w_ref, gu_ref, o_ref):
        dact = _dot(dx_ref[...], w_ref[...], (((1,), (1,)), ((), ())))
        gu_v = gu_ref[...].astype(F32)
        _, vjp = jax.vjp(_swiglu_math, gu_v[:, :tn], gu_v[:, tn:])
        dg, du = vjp(dact)
        o_ref[...] = jnp.concatenate([dg, du], axis=1).astype(o_ref.dtype)

    return pl.pallas_call(
        body, name=name, grid=(dff // tn, t // tm),
        in_specs=[pl.BlockSpec((tm, d), lambda j, i: (i, 0)), pl.BlockSpec((tn, d), lambda j, i: (j, 0)),
                  pl.BlockSpec((tm, 2 * tn), lambda j, i: (i, j))],
        out_specs=pl.BlockSpec((tm, 2 * tn), lambda j, i: (i, j)),
        out_shape=jax.ShapeDtypeStruct((t, 2 * dff), MXU_DTYPE),
        compiler_params=_params("parallel", "parallel"),
    )(dx_out, w_down, gu)


def _matmul_norm_bwd(dy, w, x, g, dres, name, ride=None):
    t, k = dy.shape
    d = w.shape[0]
    tm = _pick(t, (1024, 512, 256, 128))
    tk = k if k <= 2816 else _pick(k, (1408, 1280, 1024, 768, 512))
    tr = _pick(tm, (256, 128))
    gi, nk = t // tm, k // tk
    nr = ride.na if ride else 0

    def body(*refs):
        dy_ref, w_ref, x_ref, g_ref, dres_ref = refs[:5]
        ride_in = refs[5:5 + nr]
        dx_ref, dg_ref = refs[5 + nr], refs[6 + nr]
        ride_out = refs[7 + nr:7 + 2 * nr]
        acc = refs[7 + 2 * nr]
        ride_sems = refs[8 + 2 * nr:]
        i, q = pl.program_id(0), pl.program_id(1)

        @pl.when((i == 0) & (q == 0))
        def _():
            dg_ref[...] = jnp.zeros_like(dg_ref)
            if ride:
                ride.start(ride_in, ride_out, ride_sems)

        @pl.when(q == 0)
        def _():
            acc[...] = jnp.zeros_like(acc)

        acc[...] += _dot(dy_ref[...], w_ref[...], (((1,), (1,)), ((), ())))

        @pl.when(q == nk - 1)
        def _():
            for r in range(tm // tr):
                rows = slice(r * tr, (r + 1) * tr)
                _, vjp = jax.vjp(_rms_norm, x_ref[rows], g_ref[...])
                dxn, dg = vjp(acc[rows])
                dx_ref[rows] = dres_ref[rows] + dxn
                dg_ref[...] += dg

        if ride:
            @pl.when((i == gi - 1) & (q == nk - 1))
            def _():
                ride.finish(ride_in, ride_out, ride_sems)

    hbm = pl.BlockSpec(memory_space=pltpu.HBM)
    row = pl.BlockSpec((tm, d), lambda i, q: (i, 0))
    res = pl.pallas_call(
        body, name=name, grid=(gi, nk),
        in_specs=[pl.BlockSpec((tm, tk), lambda i, q: (i, q)), pl.BlockSpec((d, tk), lambda i, q: (0, q)), row,
                  pl.BlockSpec((1, d), lambda i, q: (0, 0)), row] + [hbm] * nr,
        out_specs=[row, pl.BlockSpec((1, d), lambda i, q: (0, 0))] + [hbm] * nr,
        out_shape=[jax.ShapeDtypeStruct((t, d), F32), jax.ShapeDtypeStruct((1, d), F32)]
        + (ride.out_shape if ride else []),
        scratch_shapes=[pltpu.VMEM((tm, d), F32)] + (ride.scratch if ride else []),
        compiler_params=pltpu.CompilerParams(dimension_semantics=("arbitrary", "arbitrary"),
                                             has_side_effects=bool(ride)),
    )(dy, w, x, g, dres, *(ride.arrays if ride else ()))
    return (res[0], res[1], res[2:]) if ride else (res[0], res[1])


def _tok_call(name, fn, t, tt, tok_in, const_in, tok_out, acc_out):
    nblk = t // tt
    hb = tt // HALO
    in_specs, args = [], []
    for arr, w, cb, mode in tok_in:
        if mode == "cur":
            spec = pl.BlockSpec((tt, w), lambda i, cb=cb: (i, cb))
        elif mode == "prev":
            spec = pl.BlockSpec((HALO, w), lambda i, cb=cb: (jnp.maximum(i * hb - 1, 0), cb))
        else:
            spec = pl.BlockSpec((HALO, w), lambda i, cb=cb: (jnp.minimum(i + 1, nblk - 1), cb))
        in_specs.append(spec)
        args.append(arr)
    for arr in const_in:
        in_specs.append(pl.BlockSpec(arr.shape, lambda i: (0, 0)))
        args.append(arr)
    out_specs, out_shape = [], []
    for rows, w, dt in tok_out:
        out_specs.append(pl.BlockSpec((rows, w), lambda i: (i, 0)))
        out_shape.append(jax.ShapeDtypeStruct((nblk * rows, w), dt))
    for shp, dt in acc_out:
        out_specs.append(pl.BlockSpec(shp, lambda i: (0, 0)))
        out_shape.append(jax.ShapeDtypeStruct(shp, dt))
    n_tok, n_const, n_out = len(tok_in), len(const_in), len(tok_out)

    def body(*refs):
        i = pl.program_id(0)
        tok_vals = []
        for (_, _, _, mode), r in zip(tok_in, refs[:n_tok]):
            v = r[...]
            if mode == "prev":
                v = jnp.where(i > 0, v, jnp.zeros_like(v))
            elif mode == "next8":
                v = jnp.where(i < nblk - 1, v, jnp.zeros_like(v))
            tok_vals.append(v)
        const_vals = [r[...] for r in refs[n_tok:n_tok + n_const]]
        outs, accs = fn(tok_vals, const_vals)
        o_refs = refs[n_tok + n_const:n_tok + n_const + n_out]
        a_refs = refs[n_tok + n_const + n_out:]
        for r, v in zip(o_refs, outs):
            r[...] = v.astype(r.dtype)
        if a_refs:
            @pl.when(i == 0)
            def _():
                for r in a_refs:
                    r[...] = jnp.zeros_like(r)

            for r, v in zip(a_refs, accs):
                r[...] += v.astype(r.dtype)

    res = pl.pallas_call(
        body, name=name, grid=(nblk,), in_specs=in_specs, out_specs=out_specs, out_shape=out_shape,
        compiler_params=_params("arbitrary" if acc_out else "parallel"),
    )(*args)
    return res


def _dn_pre_math(cur, halo, ba, cw, alog, dtb):
    tt = cur.shape[0]
    a = jax.nn.silu(_causal_conv(cur, halo, cw, DN_CONV))
    pieces = []
    for p in range(2 * DN_HEADS):
        xh = a[:, p * HEAD_DIM:(p + 1) * HEAD_DIM]
        xh = xh * lax.rsqrt(jnp.sum(xh * xh, axis=-1, keepdims=True) + EPS)
        if p < DN_HEADS:
            xh = xh * (HEAD_DIM ** -0.5)
        pieces.append(xh)
    pieces.append(a[:, 2 * DN_WIDTH:])
    qkvn = jnp.concatenate(pieces, axis=1)
    lane = lax.broadcasted_iota(jnp.int32, ba.shape, 1)
    raw = jnp.where(lane < DN_HEADS, jax.nn.sigmoid(ba), -jnp.exp(alog) * jax.nn.softplus(ba + dtb))
    r = lax.broadcasted_iota(jnp.int32, (tt, tt), 0)
    c = lax.broadcasted_iota(jnp.int32, (tt, tt), 1)
    tri = jnp.where((r // CHUNK == c // CHUNK) & (c <= r), 1.0, 0.0).astype(F32)
    cums = _ein("ij,jk->ik", LHS_EXACT, tri, raw)
    bg = jnp.where(lane < DN_HEADS, raw, cums)
    return qkvn, bg


def _mix_math(o, z, gb, gc, gc_halo, hv, hv_halo, dng, scg, scw):
    outs = []
    for h in range(DN_HEADS):
        sl = slice(h * HEAD_DIM, (h + 1) * HEAD_DIM)
        oh = o[:, sl]
        outs.append(oh * lax.rsqrt(jnp.mean(oh * oh, axis=-1, keepdims=True) + EPS) * dng * jax.nn.silu(z[:, sl]))
    y = gb * _causal_conv(gc * hv, gc_halo * hv_halo, scw, SC_CONV)
    gw = SC_WIDTH // SC_GROUPS
    for g in range(SC_GROUPS):
        sl = slice(g * gw, (g + 1) * gw)
        yg = y[:, sl]
        outs.append(yg * lax.rsqrt(jnp.mean(yg * yg, axis=-1, keepdims=True) + EPS) * scg[:, sl])
    return jnp.concatenate(outs, axis=1)


def _tri_inverse(a):
    c = a.shape[-1]
    r = lax.broadcasted_iota(jnp.int32, (c, c), 0)
    q = lax.broadcasted_iota(jnp.int32, (c, c), 1)
    eye = jnp.where(r == q, 1.0, 0.0).astype(F32)[None]
    blk = (r // 16 == q // 16)[None]
    d = jnp.where(blk, a, 0.0)
    o = a - d
    mm = functools.partial(_ein, "bij,bjk->bik", PRECISE)
    p = eye - d
    n = mm(d, d)
    for _ in range(2):
        both = mm(jnp.concatenate([n, p], axis=1), n)
        n = both[:, :c]
        p = p + both[:, c:]
    p = p + mm(p, n)
    e = mm(p, o)
    e2 = mm(e, e)
    left = eye - e + e2 - mm(e, e2)
    return mm(left, p)


@jax.custom_vjp
def _inverse_known(a, tinv):
    return tinv


def _inverse_known_fwd(a, tinv):
    return tinv, tinv


def _inverse_known_bwd(tinv, ct):
    left = _ein("bji,bjk->bik", PRECISE, tinv, ct)
    return -_ein("bik,bjk->bij", PRECISE, left, tinv), jnp.zeros_like(tinv)


_inverse_known.defvjp(_inverse_known_fwd, _inverse_known_bwd)


def _delta_intra_math(q, k, v, bg, head, tinv_known=None):
    n = q.shape[0]
    nb = n // CHUNK
    lane = lax.broadcasted_iota(jnp.int32, bg.shape, 1)
    beta = jnp.sum(jnp.where(lane == head, bg, 0.0), axis=1, keepdims=True).reshape(nb, CHUNK, 1)
    gc = jnp.sum(jnp.where(lane == head + DN_HEADS, bg, 0.0), axis=1, keepdims=True).reshape(nb, CHUNK, 1)
    q3, k3, v3 = (a.reshape(nb, CHUNK, HEAD_DIM) for a in (q, k, v))
    r = lax.broadcasted_iota(jnp.int32, (CHUNK, CHUNK), 0)
    c = lax.broadcasted_iota(jnp.int32, (CHUNK, CHUNK), 1)
    eye = jnp.where(r == c, 1.0, 0.0).astype(F32)[None]
    gcr = _ein("bik,bkj->bij", LHS_EXACT, jnp.ones((nb, CHUNK, CHUNK), F32), gc * eye)
    decay = jnp.exp(jnp.where((r >= c)[None], gc - gcr, -1e30))
    kb = k3 * beta
    vb = v3 * beta
    egc = jnp.exp(gc)
    a = jnp.where((r > c)[None], _ein("bcd,bmd->bcm", FAST, kb, k3) * decay, 0.0)
    tinv = _tri_inverse(a) if tinv_known is None else _inverse_known(a, tinv_known)
    uw = _ein("bcm,bmd->bcd", PRECISE, tinv, jnp.concatenate([vb, kb * egc], axis=2))
    u, w = uw[:, :, :HEAD_DIM], uw[:, :, HEAD_DIM:]
    qk = _ein("bcd,bmd->bcm", FAST, q3, k3) * decay
    row = lax.broadcasted_iota(jnp.int32, (nb, CHUNK, 1), 1)
    glast = jnp.sum(jnp.where(row == CHUNK - 1, gc, 0.0), axis=1, keepdims=True)
    qd = q3 * egc
    kd = k3 * jnp.exp(glast - gc)
    glb = jnp.broadcast_to(jnp.exp(glast), (nb, HALO, LANES))
    flat = lambda x: x.reshape(n, HEAD_DIM)
    return flat(u), flat(w), flat(qd), flat(kd), qk, glb, tinv


def _delta_step_math(u, w, qd, kd, qk, gl, s):
    vnew = u - _ein("ck,kv->cv", FAST, w, s)
    o = _ein("ck,kv->cv", FAST, qd, s) + _ein("cm,mv->cv", FAST, qk, vnew)
    s2 = s * gl + _ein("ck,cv->kv", FAST, kd, vnew)
    return o, s2


def _delta_intra(qkvn, bg, nb):
    t = qkvn.shape[0]
    n = t // CHUNK
    rows = nb * CHUNK

    def body(q_ref, k_ref, v_ref, bg_ref, u_ref, w_ref, qd_ref, kd_ref, qk_ref, gl_ref, ti_ref):
        outs = _delta_intra_math(q_ref[...], k_ref[...], v_ref[...], bg_ref[...], pl.program_id(1))
        for r, v in zip((u_ref, w_ref, qd_ref, kd_ref, qk_ref, gl_ref, ti_ref), outs):
            r[...] = v

    col = lambda off: pl.BlockSpec((rows, HEAD_DIM), lambda b, h, off=off: (b, off + h))
    tok = jax.ShapeDtypeStruct((t, DN_WIDTH), F32)
    return pl.pallas_call(
        body, name="delta_intra", grid=(n // nb, DN_HEADS),
        in_specs=[col(0), col(DN_HEADS), col(2 * DN_HEADS), pl.BlockSpec((rows, LANES), lambda b, h: (b, 0))],
        out_specs=[col(0)] * 4 + [pl.BlockSpec((nb, None, CHUNK, CHUNK), lambda b, h: (b, h, 0, 0)),
                                  pl.BlockSpec((nb, None, HALO, LANES), lambda b, h: (b, h, 0, 0)),
                                  pl.BlockSpec((nb, None, CHUNK, CHUNK), lambda b, h: (b, h, 0, 0))],
        out_shape=[tok] * 4 + [jax.ShapeDtypeStruct((n, DN_HEADS, CHUNK, CHUNK), F32),
                               jax.ShapeDtypeStruct((n, DN_HEADS, HALO, LANES), F32),
                               jax.ShapeDtypeStruct((n, DN_HEADS, CHUNK, CHUNK), F32)],
        compiler_params=_params("parallel", "arbitrary"),
    )(qkvn, qkvn, qkvn, bg)


def _delta_intra_bwd(qkvn, bg, tinv, cts, nb):
    t = qkvn.shape[0]
    n = t // CHUNK
    rows = nb * CHUNK

    def body(q_ref, k_ref, v_ref, bg_ref, ti_ref, du, dw, dqd, dkd, dqk, dgl, dq_ref, dk_ref, dv_ref, dbg_ref):
        h = pl.program_id(1)
        ti = ti_ref[...]
        _, vjp = jax.vjp(lambda q, k, v, b: _delta_intra_math(q, k, v, b, h, ti)[:6],
                         q_ref[...], k_ref[...], v_ref[...], bg_ref[...])
        dq, dk, dv, dbg = vjp((du[...], dw[...], dqd[...], dkd[...], dqk[...], dgl[...]))
        dq_ref[...] = dq
        dk_ref[...] = dk
        dv_ref[...] = dv

        @pl.when(h == 0)
        def _():
            dbg_ref[...] = jnp.zeros_like(dbg_ref)

        dbg_ref[...] += dbg

    col = lambda off: pl.BlockSpec((rows, HEAD_DIM), lambda b, h, off=off: (b, off + h))
    bgs = pl.BlockSpec((rows, LANES), lambda b, h: (b, 0))
    qks = pl.BlockSpec((nb, None, CHUNK, CHUNK), lambda b, h: (b, h, 0, 0))
    gls = pl.BlockSpec((nb, None, HALO, LANES), lambda b, h: (b, h, 0, 0))
    tok = jax.ShapeDtypeStruct((t, DN_WIDTH), F32)
    return pl.pallas_call(
        body, name="delta_intra_bwd", grid=(n // nb, DN_HEADS),
        in_specs=[col(0), col(DN_HEADS), col(2 * DN_HEADS), bgs, qks, col(0), col(0), col(0), col(0), qks, gls],
        out_specs=[col(0), col(0), col(0), bgs],
        out_shape=[tok, tok, tok, jax.ShapeDtypeStruct((t, LANES), F32)],
        compiler_params=_params("parallel", "arbitrary"),
    )(qkvn, qkvn, qkvn, bg, tinv, *cts)


def _delta_scan(u, w, qd, kd, qk, glb, cb):
    t = u.shape[0]
    n = t // CHUNK
    rows = cb * CHUNK

    def body(u_ref, w_ref, qd_ref, kd_ref, qk_ref, gl_ref, o_ref, s_ref, s_scr):
        @pl.when(pl.program_id(0) == 0)
        def _():
            s_scr[...] = jnp.zeros_like(s_scr)

        def chunk(c, carry):
            r0 = pl.multiple_of(c * CHUNK, CHUNK)
            for h in range(DN_HEADS):
                sl = (pl.ds(r0, CHUNK), slice(h * HEAD_DIM, (h + 1) * HEAD_DIM))
                s = s_scr[h]
                s_ref[c, h] = s
                o, s2 = _delta_step_math(u_ref[sl], w_ref[sl], qd_ref[sl], kd_ref[sl], qk_ref[c, h],
                                         gl_ref[c, h][0:1, :], s)
                o_ref[sl] = o
                s_scr[h] = s2
            return carry

        lax.fori_loop(0, cb, chunk, 0)

    tok = pl.BlockSpec((rows, DN_WIDTH), lambda i: (i, 0))
    return pl.pallas_call(
        body, name="delta_scan", grid=(n // cb,),
        in_specs=[tok] * 4 + [pl.BlockSpec((cb, DN_HEADS, CHUNK, CHUNK), lambda i: (i, 0, 0, 0)),
                              pl.BlockSpec((cb, DN_HEADS, HALO, LANES), lambda i: (i, 0, 0, 0))],
        out_specs=[tok, pl.BlockSpec((cb, DN_HEADS, HEAD_DIM, HEAD_DIM), lambda i: (i, 0, 0, 0))],
        out_shape=[jax.ShapeDtypeStruct((t, DN_WIDTH), F32),
                   jax.ShapeDtypeStruct((n, DN_HEADS, HEAD_DIM, HEAD_DIM), F32)],
        scratch_shapes=[pltpu.VMEM((DN_HEADS, HEAD_DIM, HEAD_DIM), F32)],
        compiler_params=_params("arbitrary"),
    )(u, w, qd, kd, qk, glb)


def _delta_scan_bwd(u, w, qd, kd, qk, glb, s_all, do, cb):
    t = u.shape[0]
    n = t // CHUNK
    nblk = n // cb
    rows = cb * CHUNK

    def body(u_ref, w_ref, qd_ref, kd_ref, qk_ref, gl_ref, s_ref, do_ref,
             du_ref, dw_ref, dqd_ref, dkd_ref, dqk_ref, dgl_ref, ds_scr):
        @pl.when(pl.program_id(0) == 0)
        def _():
            ds_scr[...] = jnp.zeros_like(ds_scr)

        def chunk(step, carry):
            c = cb - 1 - step
            r0 = pl.multiple_of(c * CHUNK, CHUNK)
            for h in range(DN_HEADS):
                sl = (pl.ds(r0, CHUNK), slice(h * HEAD_DIM, (h + 1) * HEAD_DIM))
                gl_tile = gl_ref[c, h]
                prim = (u_ref[sl], w_ref[sl], qd_ref[sl], kd_ref[sl], qk_ref[c, h], gl_tile, s_ref[c, h])
                _, vjp = jax.vjp(lambda a, b, cc, d, e, g, s: _delta_step_math(a, b, cc, d, e, g[0:1, :], s), *prim)
                du, dw, dqd, dkd, dqk, dgl, ds = vjp((do_ref[sl], ds_scr[h]))
                du_ref[sl] = du
                dw_ref[sl] = dw
                dqd_ref[sl] = dqd
                dkd_ref[sl] = dkd
                dqk_ref[c, h] = dqk
                dgl_ref[c, h] = dgl
                ds_scr[h] = ds
            return carry

        lax.fori_loop(0, cb, chunk, 0)

    rev = lambda i: nblk - 1 - i
    tok = pl.BlockSpec((rows, DN_WIDTH), lambda i: (rev(i), 0))
    qks = pl.BlockSpec((cb, DN_HEADS, CHUNK, CHUNK), lambda i: (rev(i), 0, 0, 0))
    gls = pl.BlockSpec((cb, DN_HEADS, HALO, LANES), lambda i: (rev(i), 0, 0, 0))
    ss = pl.BlockSpec((cb, DN_HEADS, HEAD_DIM, HEAD_DIM), lambda i: (rev(i), 0, 0, 0))
    tshape = jax.ShapeDtypeStruct((t, DN_WIDTH), F32)
    return pl.pallas_call(
        body, name="delta_scan_bwd", grid=(nblk,),
        in_specs=[tok] * 4 + [qks, gls, ss, tok],
        out_specs=[tok] * 4 + [qks, gls],
        out_shape=[tshape] * 4 + [jax.ShapeDtypeStruct(qk.shape, F32), jax.ShapeDtypeStruct(glb.shape, F32)],
        scratch_shapes=[pltpu.VMEM((DN_HEADS, HEAD_DIM, HEAD_DIM), F32)],
        compiler_params=_params("arbitrary"),
    )(u, w, qd, kd, qk, glb, s_all, do)


def _peer(mask):
    x, y, c = lax.axis_index("x"), lax.axis_index("y"), lax.axis_index("c")
    return (x ^ ((mask >> 2) & 1), y ^ ((mask >> 1) & 1), c ^ (mask & 1))


def _my_index():
    return 4 * lax.axis_index("x") + 2 * lax.axis_index("y") + lax.axis_index("c")


class _Exchange:
    CHIP_MASKS = (4, 2, 6)

    def __init__(self, kind, arrays):
        self.kind = kind
        self.arrays = list(arrays)
        self.na = na = len(self.arrays)
        if kind == "gather":
            self.out_shape = [jax.ShapeDtypeStruct((N_DEV,) + a.shape, a.dtype) for a in self.arrays]
        else:
            self.out_shape = [jax.ShapeDtypeStruct(a.shape, a.dtype) for a in self.arrays]
        self.scratch = [pltpu.SemaphoreType.DMA((na, 7)), pltpu.SemaphoreType.DMA((na, 7)),
                        pltpu.SemaphoreType.DMA((na,))]

    def _copies(self, ins, outs, sems):
        send_sems, recv_sems, local_sems = sems
        me = _my_index()
        local, first, passed, arrivals = [], [], [], []
        if self.kind == "gather":
            def rc(a, k, block, to, own=False):
                def make():
                    dst = outs[a].at[block]
                    return pltpu.make_async_remote_copy(src_ref=ins[a] if own else dst, dst_ref=dst,
                                                        send_sem=send_sems.at[a, k], recv_sem=recv_sems.at[a, k],
                                                        device_id=to, device_id_type=MESH)
                return make

            sib = _peer(1)
            for a in range(self.na):
                local.append(lambda a=a: pltpu.make_async_copy(ins[a], outs[a].at[me], local_sems.at[a]))
                first.append(rc(a, 0, me, sib, own=True))
                arrivals.append(rc(a, 0, me ^ 1, _peer(0)))
                for j, m in enumerate(self.CHIP_MASKS):
                    first.append(rc(a, 1 + j, me, _peer(m), own=True))
                    passed.append((rc(a, 1 + j, me ^ m, _peer(0)), rc(a, 4 + j, me ^ m, sib)))
                    arrivals.append(rc(a, 4 + j, me ^ m ^ 1, _peer(0)))
        else:
            for a in range(self.na):
                local.append(lambda a=a: pltpu.make_async_copy(ins[a].at[me], outs[a].at[me], local_sems.at[a]))
                for m in range(1, N_DEV):
                    def make(a=a, m=m):
                        return pltpu.make_async_remote_copy(
                            src_ref=ins[a].at[me ^ m], dst_ref=outs[a].at[me], send_sem=send_sems.at[a, m - 1],
                            recv_sem=recv_sems.at[a, m - 1], device_id=_peer(m), device_id_type=MESH)
                    first.append(make)
                    arrivals.append(make)
        return local, first, passed, arrivals

    def start(self, ins, outs, sems):
        local, first, _, _ = self._copies(ins, outs, sems)
        for make in local + first:
            make().start()

    def finish(self, ins, outs, sems):
        local, first, passed, arrivals = self._copies(ins, outs, sems)
        for landed, onward in passed:
            landed().wait_recv()
            onward().start()
        for make in arrivals:
            make().wait_recv()
        for make in first + [p for _, p in passed]:
            make().wait_send()
        for make in local:
            make().wait()

    def run(self, name):
        na = self.na

        def body(*refs):
            ins, outs, sems = refs[:na], refs[na:2 * na], refs[2 * na:]
            self.start(ins, outs, sems)
            self.finish(ins, outs, sems)

        hbm = pl.BlockSpec(memory_space=pltpu.HBM)
        return pl.pallas_call(
            body, name=name, in_specs=[hbm] * na, out_specs=[hbm] * na, out_shape=self.out_shape,
            scratch_shapes=self.scratch, compiler_params=pltpu.CompilerParams(has_side_effects=True),
        )(*self.arrays)


def _all_gather(shards, name):
    return _Exchange("gather", shards).run(name)


def _adamw_math(w, g, m, v):
    m2 = ADAM_B1 * m + (1.0 - ADAM_B1) * g
    v2 = ADAM_B2 * v + (1.0 - ADAM_B2) * jnp.square(g)
    m_hat = m2 / (1.0 - ADAM_B1 ** ADAM_STEP)
    v_hat = v2 / (1.0 - ADAM_B2 ** ADAM_STEP)
    delta = -ADAM_LR * (m_hat / (jnp.sqrt(v_hat) + ADAM_EPS) + ADAM_WD * w)
    return delta, m2, v2


def _sum_adamw(parts, w, m, v, name):
    r, c = w.shape
    tr = _pick(r, (512, 256, 352, 128, 64, 32, 16, 8))
    np_ = parts.shape[0]

    def body(p_ref, w_ref, m_ref, v_ref, g_ref, d_ref, m2_ref, v2_ref):
        g = p_ref[0].astype(F32)
        for d in range(1, np_):
            g = g + p_ref[d].astype(F32)
        delta, m2, v2 = _adamw_math(w_ref[...], g, m_ref[...], v_ref[...])
        g_ref[...] = g
        d_ref[...] = delta
        m2_ref[...] = m2
        v2_ref[...] = v2

    blk = pl.BlockSpec((tr, c), lambda i: (i, 0))
    shp = jax.ShapeDtypeStruct((r, c), F32)
    return pl.pallas_call(
        body, name=name, grid=(r // tr,),
        in_specs=[pl.BlockSpec((np_, tr, c), lambda i: (0, i, 0)), blk, blk, blk],
        out_specs=[blk] * 4, out_shape=[shp] * 4, compiler_params=_params("parallel"),
    )(parts, w, m, v)


def _sum_rows(parts, name):
    np_, r, c = parts.shape

    def body(p_ref, o_ref):
        g = p_ref[0]
        for d in range(1, np_):
            g = g + p_ref[d]
        o_ref[...] = g

    return pl.pallas_call(body, name=name, out_shape=jax.ShapeDtypeStruct((r, c), F32))(parts)


def _pad_w_in(w):
    d = w.shape[0]
    n_ba = 2 * DN_HEADS
    a = w[:, :SC_OFF]
    ba = w[:, SC_OFF:SC_OFF + n_ba]
    sc = w[:, SC_OFF + n_ba:]
    return jnp.concatenate([a, sc, ba, jnp.zeros((d, BA_W - n_ba), w.dtype)], axis=1)


def _unpad_w_in(wp):
    n_ba = 2 * DN_HEADS
    return jnp.concatenate([wp[:, :SC_OFF], wp[:, BA_OFF:BA_OFF + n_ba], wp[:, SC_OFF:BA_OFF]], axis=1)


def _lane_row(v, off):
    return jnp.pad(v.astype(F32), (off, LANES - off - v.shape[0]))[None]


TT = 256
NB_INTRA = 8
CB_SCAN = 8


def _layer_fwd(x, p, ride_proj, ride_ffn, late):
    t, d = x.shape
    got_proj = got_ffn = None
    if ride_proj:
        h, proj, got_proj = _norm_matmul(x, p["norm1_g"], p["w_in"], "proj_fwd_gather", ride=ride_proj)
    else:
        h, proj = _norm_matmul(x, p["norm1_g"], p["w_in"], "proj_fwd")
    p = {**p, **late(got_proj)}
    qkvn, bg = _tok_call(
        "dn_pre", lambda tv, cv: (_dn_pre_math(*tv, *cv), ()), t, TT,
        [(proj, QKV_W, 0, "cur"), (proj, QKV_W, 0, "prev"), (proj, LANES, BA_OFF // LANES, "cur")],
        [p["dn_conv_w"], p["alog_row"], p["dt_row"]],
        [(TT, QKV_W, F32), (TT, LANES, F32)], [])
    u, w, qd, kd, qk, glb, tinv = _delta_intra(qkvn, bg, NB_INTRA)
    o, s_all = _delta_scan(u, w, qd, kd, qk, glb, CB_SCAN)
    cb0 = SC_OFF // SC_WIDTH
    mix_in = [(o, DN_WIDTH, 0, "cur"), (proj, DN_WIDTH, Z_OFF // DN_WIDTH, "cur"),
              (proj, SC_WIDTH, cb0, "cur"), (proj, SC_WIDTH, cb0 + 1, "cur"), (proj, SC_WIDTH, cb0 + 1, "prev"),
              (proj, SC_WIDTH, cb0 + 2, "cur"), (proj, SC_WIDTH, cb0 + 2, "prev")]
    mix_const = [p["dn_norm_g"], p["sc_norm_g"], p["sc_conv_w"]]
    (cat,) = _tok_call("mix_post", lambda tv, cv: ((_mix_math(*tv, *cv),), ()), t, TT,
                       mix_in, mix_const, [(TT, 2 * DN_WIDTH, MXU_DTYPE)], [])
    x_mid = _matmul(cat, p["w_out"], "nn", F32, "out_proj", residual=x)
    if ride_ffn:
        h2, gu, act, got_ffn = _ffn_up_swiglu(x_mid, p["norm2_g"], p["w_gu"], "ffn_up_gather", ride=ride_ffn)
    else:
        h2, gu, act = _ffn_up_swiglu(x_mid, p["norm2_g"], p["w_gu"], "ffn_up")
    x_out = _matmul(act, p["w_down"], "nn", F32, "ffn_down", residual=x_mid)
    saved = dict(x=x, h=h, proj=proj, qkvn=qkvn, bg=bg, u=u, w=w, qd=qd, kd=kd, qk=qk, glb=glb, tinv=tinv, s_all=s_all, o=o,
                 cat=cat, x_mid=x_mid, h2=h2, gu=gu, act=act, mix_in=mix_in, mix_const=mix_const)
    return x_out, saved, p, got_proj, got_ffn


def _layer_bwd(dx_out, p, s, ride_prev, ride_gu, ride_down):
    t, d = dx_out.shape
    got = {}
    dgu = _ffn_down_dx_swiglu(dx_out, p["w_down"], s["gu"], "ffn_down_dx")
    d_w_down = _matmul(s["act"], dx_out, "tn", MXU_DTYPE, "ffn_down_dw")
    if ride_prev:
        dx_mid, d_norm2, got["prev"] = _matmul_norm_bwd(dgu, p["w_gu"], s["x_mid"], p["norm2_g"], dx_out,
                                                        "ffn_up_dx_scatter", ride=ride_prev)
    else:
        dx_mid, d_norm2 = _matmul_norm_bwd(dgu, p["w_gu"], s["x_mid"], p["norm2_g"], dx_out, "ffn_up_dx")
    d_w_gu = _matmul(s["h2"], dgu, "tn", MXU_DTYPE, "ffn_up_dw")
    dcat = _matmul(dx_mid, p["w_out"], "nt", F32, "out_proj_dx")
    d_w_out = _matmul(s["cat"], dx_mid, "tn", MXU_DTYPE, "out_proj_dw")

    def mix_bwd(tv, cv):
        prim = tuple(tv[:7]) + tuple(cv)
        _, vjp = jax.vjp(_mix_math, *prim)
        do, dz, dgb, dgc, dgch, dhv, dhvh, ddng, dscg, dscw = vjp(tv[7])
        return (do, dz, dgb, dgc, dgch, dhv, dhvh), (ddng, dscg, dscw)

    wide = (TT, DN_WIDTH, F32)
    halo = (HALO, SC_WIDTH, F32)
    do, dz, dgb, dgc, dgc_h, dhv, dhv_h, d_dn_norm, d_sc_norm, d_sc_conv = _tok_call(
        "mix_post_bwd", mix_bwd, t, TT, s["mix_in"] + [(dcat, 2 * DN_WIDTH, 0, "cur")], s["mix_const"],
        [wide, wide, wide, wide, halo, wide, halo],
        [((1, HEAD_DIM), F32), ((1, SC_WIDTH), F32), ((SC_CONV, SC_WIDTH), F32)])
    cts = _delta_scan_bwd(s["u"], s["w"], s["qd"], s["kd"], s["qk"], s["glb"], s["s_all"], do, CB_SCAN)
    dq, dk, dv, dbg = _delta_intra_bwd(s["qkvn"], s["bg"], s["tinv"], cts, NB_INTRA)
    proj = s["proj"]

    def dn_pre_bwd(tv, cv):
        cur, hal, ba, dq_, dk_, dv_, dbg_ = tv
        _, vjp = jax.vjp(_dn_pre_math, cur, hal, ba, *cv)
        dcur, dhal, dba, dcw, dal, ddt = vjp((jnp.concatenate([dq_, dk_, dv_], axis=1), dbg_))
        return (dcur, dhal, dba), (dcw, dal, ddt)

    dqkv, dqkv_h, dba, d_dn_conv, d_alog, d_dt = _tok_call(
        "dn_pre_bwd", dn_pre_bwd, t, TT,
        [(proj, QKV_W, 0, "cur"), (proj, QKV_W, 0, "prev"), (proj, LANES, BA_OFF // LANES, "cur"),
         (dq, DN_WIDTH, 0, "cur"), (dk, DN_WIDTH, 0, "cur"), (dv, DN_WIDTH, 0, "cur"), (dbg, LANES, 0, "cur")],
        [p["dn_conv_w"], p["alog_row"], p["dt_row"]],
        [(TT, QKV_W, F32), (HALO, QKV_W, F32), (TT, LANES, F32)],
        [((DN_CONV, QKV_W), F32), ((1, LANES), F32), ((1, LANES), F32)])

    def assemble(tv, cv):
        dqkv_, dqkv_n, dz_, dgb_, dgc_, dgc_n, dhv_, dhv_n, dba_ = tv

        def with_halo(cur, nxt):
            return cur + jnp.concatenate([jnp.zeros((TT - HALO, cur.shape[1]), F32), nxt], axis=0)

        out = jnp.concatenate([with_halo(dqkv_, dqkv_n), dz_, dgb_, with_halo(dgc_, dgc_n), with_halo(dhv_, dhv_n),
                               dba_, jnp.zeros((TT, BA_W - LANES), F32)], axis=1)
        return (out,), ()

    (dproj,) = _tok_call(
        "dproj_assemble", assemble, t, TT,
        [(dqkv, QKV_W, 0, "cur"), (dqkv_h, QKV_W, 0, "next8"), (dz, DN_WIDTH, 0, "cur"), (dgb, SC_WIDTH, 0, "cur"),
         (dgc, SC_WIDTH, 0, "cur"), (dgc_h, SC_WIDTH, 0, "next8"), (dhv, SC_WIDTH, 0, "cur"),
         (dhv_h, SC_WIDTH, 0, "next8"), (dba, LANES, 0, "cur")], [],
        [(TT, PROJ_W, MXU_DTYPE)], [])
    dx_in, d_norm1, got["gu"] = _matmul_norm_bwd(dproj, p["w_in"], s["x"], p["norm1_g"], dx_mid,
                                                 "proj_dx_scatter", ride=ride_gu(d_w_gu))
    d_w_in, got["down"] = _matmul(s["h"], dproj, "tn", MXU_DTYPE, "proj_dw_scatter", ride=ride_down(d_w_down))
    grads = dict(w_in=d_w_in, w_out=d_w_out, w_gu=d_w_gu, w_down=d_w_down, norm1_g=d_norm1, norm2_g=d_norm2,
                 dn_norm_g=d_dn_norm, sc_norm_g=d_sc_norm, sc_conv_w=d_sc_conv, dn_conv_w=d_dn_conv,
                 alog=d_alog, dt=d_dt)
    return dx_in, grads, got


def _final_loss(x, g, target):
    t, d = x.shape

    def fn(tv, cv):
        xv, tg = tv

        def loss_fn(xx, gg):
            err = jnp.square(_rms_norm(xx, gg) - tg)
            return 0.5 * jnp.sum(jnp.mean(err, axis=-1))

        loss, vjp = jax.vjp(loss_fn, xv, cv[0])
        dx, dg = vjp(jnp.ones((), F32))
        return (dx,), (jnp.full((1, LANES), loss, F32), dg)

    return _tok_call("final_loss", fn, t, TT, [(x, d, 0, "cur"), (target, d, 0, "cur")], [g],
                     [(TT, d, F32)], [((1, LANES), F32), ((1, d), F32)])


def _pack_rows(arrs):
    rows, offs, r0 = [], [], 0
    for a in arrs:
        n = a.size
        nr = -(-n // LANES)
        flat = jnp.pad(a.reshape(-1).astype(F32), (0, nr * LANES - n))
        rows.append(flat.reshape(nr, LANES))
        offs.append((r0, nr, a.shape))
        r0 += nr
    pad = (-r0) % 8
    if pad:
        rows.append(jnp.zeros((pad, LANES), F32))
    return jnp.concatenate(rows, axis=0), offs


def _unpack_rows(packed, offs):
    out = []
    for r0, nr, shp in offs:
        n = 1
        for s_ in shp:
            n *= s_
        out.append(packed[r0:r0 + nr].reshape(-1)[:n].reshape(shp))
    return out


def kernel(x, norm1_g, w_in, dn_conv_w, dn_a_log, dn_dt_bias, dn_norm_g, sc_conv_w, sc_norm_g, w_out, norm2_g, ffn_w_gate, ffn_w_up, ffn_w_down, final_norm_g, loss_target, m_norm1_g, m_w_in, m_dn_conv_w, m_dn_a_log, m_dn_dt_bias, m_dn_norm_g, m_sc_conv_w, m_sc_norm_g, m_w_out, m_norm2_g, m_ffn_w_gate, m_ffn_w_up, m_ffn_w_down, m_final_norm_g, v_norm1_g, v_w_in, v_dn_conv_w, v_dn_a_log, v_dn_dt_bias, v_dn_norm_g, v_sc_conv_w, v_sc_norm_g, v_w_out, v_norm2_g, v_ffn_w_gate, v_ffn_w_up, v_ffn_w_down, v_final_norm_g):
    depth, d, cin = w_in.shape
    t = x.shape[1]
    dff_s = ffn_w_gate.shape[2]
    me = _my_index()
    x2 = x.reshape(t, d)
    tgt = loss_target.reshape(t, d)

    conv_pack, conv_offs = _pack_rows([dn_conv_w, sc_conv_w])
    (conv_all,) = _all_gather([conv_pack], "gather_conv")
    dn_parts, sc_parts = zip(*[_unpack_rows(conv_all[j], conv_offs) for j in range(N_DEV)])
    dn_conv_full = jnp.concatenate(dn_parts, axis=2)
    sc_conv_full = jnp.concatenate(sc_parts, axis=2)

    def shards(l):
        return [a[l].astype(MXU_DTYPE) for a in (w_in, w_out, ffn_w_gate, ffn_w_up, ffn_w_down)]

    def mixer_params(l, g_in, g_out):
        full_in = g_in.transpose(1, 0, 2).reshape(d, N_DEV * cin)
        return dict(
            w_in=_pad_w_in(full_in), w_out=g_out.reshape(d, d),
            norm1_g=norm1_g[l][None], norm2_g=norm2_g[l][None], dn_norm_g=dn_norm_g[l][None],
            sc_norm_g=sc_norm_g[l][None], dn_conv_w=dn_conv_full[l], sc_conv_w=sc_conv_full[l],
            alog_row=_lane_row(dn_a_log[l], DN_HEADS), dt_row=_lane_row(dn_dt_bias[l], DN_HEADS))

    def ffn_params(g_gate, g_up, g_down):
        full_gate = g_gate.transpose(1, 0, 2).reshape(d, N_DEV * dff_s)
        full_up = g_up.transpose(1, 0, 2).reshape(d, N_DEV * dff_s)
        return dict(w_gu=_interleave_gu(full_gate, full_up), w_down=g_down.reshape(N_DEV * dff_s, d))

    nxt = mixer_params(0, *_all_gather(shards(0)[:2], "gather_first"))
    nxt_ffn = None
    params, saved = [], []
    xc = x2
    for l in range(depth):
        last = l + 1 == depth
        own_ffn = shards(l)[2:] if nxt_ffn is None else []
        ahead = [] if last else shards(l + 1)
        ride_proj = _Exchange("gather", own_ffn + ahead[:2]) if own_ffn or ahead else None
        ride_ffn = None if last else _Exchange("gather", ahead[2:])

        def late(got, own_ffn=own_ffn, nxt_ffn=nxt_ffn):
            return ffn_params(*got[:3]) if own_ffn else nxt_ffn

        xc, s, p_l, got_proj, got_ffn = _layer_fwd(xc, nxt, ride_proj, ride_ffn, late)
        params.append(p_l)
        saved.append(s)
        if not last:
            nxt = mixer_params(l + 1, *got_proj[len(own_ffn):])
            nxt_ffn = ffn_params(*got_ffn)
    dx, loss_part, d_final = _final_loss(xc, final_norm_g[None], tgt)

    names = ("w_in", "w_out", "ffn_w_gate", "ffn_w_up", "ffn_w_down")
    big_out = {n: {k: [None] * depth for k in ("g", "d", "m", "v")} for n in names}
    w_loc = dict(w_in=w_in, w_out=w_out, ffn_w_gate=ffn_w_gate, ffn_w_up=ffn_w_up, ffn_w_down=ffn_w_down)
    m_loc = dict(w_in=m_w_in, w_out=m_w_out, ffn_w_gate=m_ffn_w_gate, ffn_w_up=m_ffn_w_up, ffn_w_down=m_ffn_w_down)
    v_loc = dict(w_in=v_w_in, w_out=v_w_out, ffn_w_gate=v_ffn_w_gate, ffn_w_up=v_ffn_w_up, ffn_w_down=v_ffn_w_down)

    cols = lambda a, c: a.reshape(a.shape[0], N_DEV, c).transpose(1, 0, 2).astype(MXU_DTYPE)

    def ride_gu(d_w_gu):
        g_gate, g_up = _split_gu(d_w_gu)
        return _Exchange("scatter", [cols(g_gate, dff_s), cols(g_up, dff_s)])

    def ride_down(d_w_down):
        return _Exchange("scatter", [d_w_down.reshape(N_DEV, dff_s, d).astype(MXU_DTYPE)])

    def apply(l, which, recv):
        for n, r in zip(which, recv):
            res = _sum_adamw(r, w_loc[n][l], m_loc[n][l], v_loc[n][l], "adamw_" + n)
            for k, a in zip(("g", "d", "m", "v"), res):
                big_out[n][k][l] = a

    grads = [None] * depth
    pending = None
    for l in reversed(range(depth)):
        ride_prev = _Exchange("scatter", pending) if pending else None
        dx, grads[l], got = _layer_bwd(dx, params[l], saved[l], ride_prev, ride_gu, ride_down)
        if ride_prev:
            apply(l + 1, names[:2], got["prev"])
        apply(l, names[2:], [*got["gu"], *got["down"]])
        pending = [cols(_unpad_w_in(grads[l]["w_in"]), cin),
                   grads[l]["w_out"].reshape(N_DEV, d // N_DEV, d).astype(MXU_DTYPE)]
    apply(0, names[:2], _Exchange("scatter", pending).run("scatter_last"))
    grad_x = dx.reshape(x.shape)
    big_out = {n: {k: jnp.stack(v_) for k, v_ in o.items()} for n, o in big_out.items()}

    stack = lambda key: jnp.stack([grads[l][key] for l in range(depth)])
    small_parts = [stack("norm1_g").reshape(depth, d), stack("norm2_g").reshape(depth, d), d_final.reshape(d),
                   stack("dn_norm_g").reshape(depth, HEAD_DIM), stack("sc_norm_g").reshape(depth, SC_WIDTH),
                   stack("alog").reshape(depth, LANES), stack("dt").reshape(depth, LANES),
                   stack("dn_conv_w"), stack("sc_conv_w"), loss_part]
    small_pack, small_offs = _pack_rows(small_parts)
    (small_all,) = _all_gather([small_pack], "gather_small")
    total = _sum_rows(small_all, "sum_small")
    (g_n1, g_n2, g_fin, g_dnn, g_scn, g_alog, g_dt, g_dnc, g_scc, loss_row) = _unpack_rows(total, small_offs)
    loss = loss_row[0, 0]
    g_alog = g_alog[:, DN_HEADS:2 * DN_HEADS]
    g_dt = g_dt[:, DN_HEADS:2 * DN_HEADS]
    dnc_w = dn_conv_w.shape[2]
    scc_w = sc_conv_w.shape[2]
    g_dnc = lax.dynamic_slice_in_dim(g_dnc, me * dnc_w, dnc_w, axis=2)
    g_scc = lax.dynamic_slice_in_dim(g_scc, me * scc_w, scc_w, axis=2)
    sm_g = [g_n1, g_dnc, g_alog, g_dt, g_dnn, g_scc, g_scn, g_n2, g_fin]
    sm_w = [norm1_g, dn_conv_w, dn_a_log, dn_dt_bias, dn_norm_g, sc_conv_w, sc_norm_g, norm2_g, final_norm_g]
    sm_m = [m_norm1_g, m_dn_conv_w, m_dn_a_log, m_dn_dt_bias, m_dn_norm_g, m_sc_conv_w, m_sc_norm_g, m_norm2_g, m_final_norm_g]
    sm_v = [v_norm1_g, v_dn_conv_w, v_dn_a_log, v_dn_dt_bias, v_dn_norm_g, v_sc_conv_w, v_sc_norm_g, v_norm2_g, v_final_norm_g]
    pg, offs = _pack_rows(sm_g)
    pw, _ = _pack_rows(sm_w)
    pm, _ = _pack_rows(sm_m)
    pv, _ = _pack_rows(sm_v)
    sg, sd, sm_, sv = _sum_adamw(pg[None], pw, pm, pv, "adamw_small")
    small_out = {k: _unpack_rows(a, offs) for k, a in zip(("g", "d", "m", "v"), (sg, sd, sm_, sv))}

    def outputs(k):
        s_ = small_out[k]
        b = big_out
        return [s_[0], b["w_in"][k], s_[1], s_[2], s_[3], s_[4], s_[5], s_[6], b["w_out"][k], s_[7],
                b["ffn_w_gate"][k], b["ffn_w_up"][k], b["ffn_w_down"][k], s_[8]]

    return (loss, grad_x, *outputs("g"), *outputs("d"), *outputs("m"), *outputs("v"))
```

```python
import functools

import jax
import jax.numpy as jnp
from jax import lax
from jax.experimental import pallas as pl
from jax.experimental.pallas import tpu as pltpu

F32 = jnp.float32
MXU_DTYPE = jnp.bfloat16
MESH = pl.DeviceIdType.MESH

N_DEV = 8
EPS = 1e-6
DN_HEADS = 4
HEAD_DIM = 128
DN_WIDTH = DN_HEADS * HEAD_DIM
SC_WIDTH = 512
SC_GROUPS = 4
DN_CONV = 4
SC_CONV = 3
CHUNK = 64
HALO = 8
LANES = 128

QKV_W = 3 * DN_WIDTH
Z_OFF = QKV_W
SC_OFF = Z_OFF + DN_WIDTH
BA_OFF = SC_OFF + 3 * SC_WIDTH
BA_W = 256
PROJ_W = BA_OFF + BA_W

ADAM_LR = 0.001
ADAM_B1 = 0.9
ADAM_B2 = 0.999
ADAM_EPS = 1e-08
ADAM_WD = 0.01
ADAM_STEP = 10


def _pick(n, cands):
    for c in cands:
        if n % c == 0:
            return c
    return n


def _params(*sem):
    return pltpu.CompilerParams(dimension_semantics=sem)


def _rms_norm(x, g):
    return x * lax.rsqrt(jnp.mean(x * x, axis=-1, keepdims=True) + EPS) * g


def _dot(a, b, dims=(((1,), (0,)), ((), ()))):
    return lax.dot_general(a.astype(MXU_DTYPE), b.astype(MXU_DTYPE), dims, preferred_element_type=F32)


def _split_terms(x, terms):
    out = []
    for _ in range(terms):
        hi = x.astype(MXU_DTYPE)
        out.append(hi)
        x = x - hi.astype(F32)
    return out


def _ein_impl(spec, terms, a, b):
    ta, tb = terms
    if ta == 1 and tb == 1:
        return jnp.einsum(spec, a.astype(MXU_DTYPE), b.astype(MXU_DTYPE), preferred_element_type=F32)
    pa, pb = _split_terms(a, ta), _split_terms(b, tb)
    order = max(ta, tb) - 1
    acc = None
    for deg in range(order, -1, -1):
        for i in range(ta):
            j = deg - i
            if 0 <= j < tb:
                t = jnp.einsum(spec, pa[i], pb[j], preferred_element_type=F32)
                acc = t if acc is None else acc + t
    return acc


@functools.partial(jax.custom_vjp, nondiff_argnums=(0, 1))
def _ein(spec, terms, a, b):
    return _ein_impl(spec, terms, a, b)


def _ein_fwd(spec, terms, a, b):
    return _ein_impl(spec, terms, a, b), (a, b)


def _ein_bwd(spec, terms, res, ct):
    a, b = res
    xy, z = spec.split("->")
    x, y = xy.split(",")
    tc = min(max(terms), 2)
    da = _ein_impl(f"{z},{y}->{x}", (tc, terms[1]), ct, b)
    db = _ein_impl(f"{x},{z}->{y}", (terms[0], tc), a, ct)
    return da, db


_ein.defvjp(_ein_fwd, _ein_bwd)

FAST = (1, 1)
PRECISE = (2, 2)
LHS_EXACT = (1, 3)


def _causal_conv(cur, halo, w, k):
    tt = cur.shape[0]
    xp = jnp.concatenate([halo, cur], axis=0)
    y = None
    for j in range(k):
        start = HALO - (k - 1) + j
        term = xp[start:start + tt] * w[j:j + 1]
        y = term if y is None else y + term
    return y


def _matmul(a, b, mode, out_dtype, name, residual=None, ride=None):
    if mode == "nn":
        (m, k), (k2, n) = a.shape, b.shape
    elif mode == "nt":
        (m, k), (n, k2) = a.shape, b.shape
    else:
        (k, m), (k2, n) = a.shape, b.shape
    assert k == k2
    tm = _pick(m, (1024, 1408, 512, 256, 128))
    tn = _pick(n, (1280, 1408, 1024, 512, 256, 128))
    tk = k if k <= 2816 else _pick(k, (1024, 768, 512))
    gi, gj, nk = m // tm, n // tn, k // tk
    dims = {"nn": (((1,), (0,)), ((), ())), "nt": (((1,), (1,)), ((), ())), "tn": (((0,), (0,)), ((), ()))}[mode]
    a_spec = {"nn": pl.BlockSpec((tm, tk), lambda i, j, q: (i, q)),
              "nt": pl.BlockSpec((tm, tk), lambda i, j, q: (i, q)),
              "tn": pl.BlockSpec((tk, tm), lambda i, j, q: (q, i))}[mode]
    b_spec = {"nn": pl.BlockSpec((tk, tn), lambda i, j, q: (q, j)),
              "nt": pl.BlockSpec((tn, tk), lambda i, j, q: (j, q)),
              "tn": pl.BlockSpec((tk, tn), lambda i, j, q: (q, j))}[mode]
    o_spec = pl.BlockSpec((tm, tn), lambda i, j, q: (i, j))
    has_res = residual is not None
    n_in = 3 if has_res else 2
    nr = ride.na if ride else 0

    def body(*refs):
        a_ref, b_ref = refs[0], refs[1]
        r_ref = refs[2] if has_res else None
        ride_in = refs[n_in:n_in + nr]
        o_ref = refs[n_in + nr]
        ride_out = refs[n_in + nr + 1:n_in + 2 * nr + 1]
        acc = refs[n_in + 2 * nr + 1]
        ride_sems = refs[n_in + 2 * nr + 2:]
        i, j, q = pl.program_id(0), pl.program_id(1), pl.program_id(2)
        if ride:
            @pl.when((i == 0) & (j == 0) & (q == 0))
            def _():
                ride.start(ride_in, ride_out, ride_sems)

        @pl.when(q == 0)
        def _():
            acc[...] = jnp.zeros_like(acc)

        acc[...] += _dot(a_ref[...], b_ref[...], dims)

        @pl.when(q == nk - 1)
        def _():
            r = acc[...]
            if has_res:
                r = r + r_ref[...]
            o_ref[...] = r.astype(o_ref.dtype)

        if ride:
            @pl.when((i == gi - 1) & (j == gj - 1) & (q == nk - 1))
            def _():
                ride.finish(ride_in, ride_out, ride_sems)

    hbm = pl.BlockSpec(memory_space=pltpu.HBM)
    in_specs = [a_spec, b_spec] + ([o_spec] if has_res else []) + [hbm] * nr
    args = (a, b) + ((residual,) if has_res else ()) + (tuple(ride.arrays) if ride else ())
    res = pl.pallas_call(
        body, name=name, grid=(gi, gj, nk), in_specs=in_specs, out_specs=[o_spec] + [hbm] * nr,
        out_shape=[jax.ShapeDtypeStruct((m, n), out_dtype)] + (ride.out_shape if ride else []),
        scratch_shapes=[pltpu.VMEM((tm, tn), F32)] + (ride.scratch if ride else []),
        compiler_params=pltpu.CompilerParams(
            dimension_semantics=("arbitrary",) * 3 if ride else ("parallel", "parallel", "arbitrary"),
            has_side_effects=bool(ride)),
    )(*args)
    return (res[0], res[1:]) if ride else res[0]


def _norm_matmul(x, g, w, name, ride=None):
    t, d = x.shape
    n = w.shape[1]
    tm = _pick(t, (1024, 512, 256, 128))
    tn = _pick(n, (1280, 1408, 1024, 512, 256, 128))
    gi, gj = t // tm, n // tn
    nr = ride.na if ride else 0

    def body(*refs):
        x_ref, g_ref, w_ref = refs[:3]
        ride_in = refs[3:3 + nr]
        h_ref, y_ref = refs[3 + nr], refs[4 + nr]
        ride_out = refs[5 + nr:5 + 2 * nr]
        h_scr = refs[5 + 2 * nr]
        ride_sems = refs[6 + 2 * nr:]
        i, j = pl.program_id(0), pl.program_id(1)
        if ride:
            @pl.when((i == 0) & (j == 0))
            def _():
                ride.start(ride_in, ride_out, ride_sems)

        @pl.when(j == 0)
        def _():
            h = _rms_norm(x_ref[...], g_ref[...]).astype(MXU_DTYPE)
            h_scr[...] = h
            h_ref[...] = h

        y_ref[...] = _dot(h_scr[...], w_ref[...])

        if ride:
            @pl.when((i == gi - 1) & (j == gj - 1))
            def _():
                ride.finish(ride_in, ride_out, ride_sems)

    hbm = pl.BlockSpec(memory_space=pltpu.HBM)
    res = pl.pallas_call(
        body, name=name, grid=(gi, gj),
        in_specs=[pl.BlockSpec((tm, d), lambda i, j: (i, 0)), pl.BlockSpec((1, d), lambda i, j: (0, 0)),
                  pl.BlockSpec((d, tn), lambda i, j: (0, j))] + [hbm] * nr,
        out_specs=[pl.BlockSpec((tm, d), lambda i, j: (i, 0)), pl.BlockSpec((tm, tn), lambda i, j: (i, j))] + [hbm] * nr,
        out_shape=[jax.ShapeDtypeStruct((t, d), MXU_DTYPE), jax.ShapeDtypeStruct((t, n), F32)]
        + (ride.out_shape if ride else []),
        scratch_shapes=[pltpu.VMEM((tm, d), MXU_DTYPE)] + (ride.scratch if ride else []),
        compiler_params=pltpu.CompilerParams(
            dimension_semantics=("arbitrary",) * 2 if ride else ("parallel", "arbitrary"),
            has_side_effects=bool(ride)),
    )(x, g, w, *(ride.arrays if ride else ()))
    return (res[0], res[1], res[2:]) if ride else (res[0], res[1])


def _swiglu_math(g, u):
    return jax.nn.silu(g) * u


def _gu_tile(dff):
    return _pick(dff, (1408, 1024, 512, 256, 128))


def _interleave_gu(gate, up):
    tn = _gu_tile(gate.shape[1])
    pieces = []
    for j in range(gate.shape[1] // tn):
        pieces += [gate[:, j * tn:(j + 1) * tn], up[:, j * tn:(j + 1) * tn]]
    return jnp.concatenate(pieces, axis=1)


def _split_gu(gu):
    dff = gu.shape[1] // 2
    tn = _gu_tile(dff)
    tiles = [gu[:, j * tn:(j + 1) * tn] for j in range(2 * dff // tn)]
    return jnp.concatenate(tiles[0::2], axis=1), jnp.concatenate(tiles[1::2], axis=1)


def _ffn_up_swiglu(x, g, w_gu, name, ride=None):
    t, d = x.shape
    dff = w_gu.shape[1] // 2
    tn = _gu_tile(dff)
    tm = _pick(t, (512, 256, 128))
    gj, gi = dff // tn, t // tm
    nr = ride.na if ride else 0

    def body(*refs):
        x_ref, g_ref, w_ref = refs[:3]
        ride_in = refs[3:3 + nr]
        h_ref, gu_ref, act_ref = refs[3 + nr:6 + nr]
        ride_out = refs[6 + nr:6 + 2 * nr]
        ride_sems = refs[6 + 2 * nr:]
        j, i = pl.program_id(0), pl.program_id(1)
        if ride:
            @pl.when((i == 0) & (j == 0))
            def _():
                ride.start(ride_in, ride_out, ride_sems)

        h = _rms_norm(x_ref[...], g_ref[...]).astype(MXU_DTYPE)

        @pl.when(j == 0)
        def _():
            h_ref[...] = h

        y = _dot(h, w_ref[...])
        gu_ref[...] = y.astype(gu_ref.dtype)
        act_ref[...] = _swiglu_math(y[:, :tn], y[:, tn:]).astype(act_ref.dtype)

        if ride:
            @pl.when((i == gi - 1) & (j == gj - 1))
            def _():
                ride.finish(ride_in, ride_out, ride_sems)

    hbm = pl.BlockSpec(memory_space=pltpu.HBM)
    res = pl.pallas_call(
        body, name=name, grid=(gj, gi),
        in_specs=[pl.BlockSpec((tm, d), lambda j, i: (i, 0)), pl.BlockSpec((1, d), lambda j, i: (0, 0)),
                  pl.BlockSpec((d, 2 * tn), lambda j, i: (0, j))] + [hbm] * nr,
        out_specs=[pl.BlockSpec((tm, d), lambda j, i: (jnp.where(j == 0, i, gi - 1), 0)),
                   pl.BlockSpec((tm, 2 * tn), lambda j, i: (i, j)),
                   pl.BlockSpec((tm, tn), lambda j, i: (i, j))] + [hbm] * nr,
        out_shape=[jax.ShapeDtypeStruct((t, d), MXU_DTYPE), jax.ShapeDtypeStruct((t, 2 * dff), MXU_DTYPE),
                   jax.ShapeDtypeStruct((t, dff), MXU_DTYPE)] + (ride.out_shape if ride else []),
        scratch_shapes=(ride.scratch if ride else []),
        compiler_params=pltpu.CompilerParams(dimension_semantics=("arbitrary", "arbitrary"),
                                             has_side_effects=bool(ride)),
    )(x, g, w_gu, *(ride.arrays if ride else ()))
    return (res[0], res[1], res[2], res[3:]) if ride else tuple(res)


def _ffn_down_dx_swiglu(dx_out, w_down, gu, name):
    t, d = dx_out.shape
    dff = w_down.shape[0]
    tn = _gu_tile(dff)
    tm = _pick(t, (512, 256, 128))

    def body(dx_ref, w_ref, gu_ref, o_ref):
        dact = _dot(dx_ref[...], w_ref[...], (((1,), (1,)), ((), ())))
        gu_v = gu_ref[...].astype(F32)
        _, vjp = jax.vjp(_swiglu_math, gu_v[:, :tn], gu_v[:, tn:])
        dg, du = vjp(dact)
        o_ref[...] = jnp.concatenate([dg, du], axis=1).astype(o_ref.dtype)

    return pl.pallas_call(
        body, name=name, grid=(dff // tn, t // tm),
        in_specs=[pl.BlockSpec((tm, d), lambda j, i: (i, 0)), pl.BlockSpec((tn, d), lambda j, i: (j, 0)),
                  pl.BlockSpec((tm, 2 * tn), lambda j, i: (i, j))],
        out_specs=pl.BlockSpec((tm, 2 * tn), lambda j, i: (i, j)),
        out_shape=jax.ShapeDtypeStruct((t, 2 * dff), MXU_DTYPE),
        compiler_params=_params("parallel", "parallel"),
    )(dx_out, w_down, gu)


def _matmul_norm_bwd(dy, w, x, g, dres, name, ride=None):
    t, k = dy.shape
    d = w.shape[0]
    tm = _pick(t, (1024, 512, 256, 128))
    tk = k if k <= 2816 else _pick(k, (1408, 1280, 1024, 768, 512))
    tr = _pick(tm, (256, 128))
    gi, nk = t // tm, k // tk
    nr = ride.na if ride else 0

    def body(*refs):
        dy_ref, w_ref, x_ref, g_ref, dres_ref = refs[:5]
        ride_in = refs[5:5 + nr]
        dx_ref, dg_ref = refs[5 + nr], refs[6 + nr]
        ride_out = refs[7 + nr:7 + 2 * nr]
        acc = refs[7 + 2 * nr]
        ride_sems = refs[8 + 2 * nr:]
        i, q = pl.program_id(0), pl.program_id(1)

        @pl.when((i == 0) & (q == 0))
        def _():
            dg_ref[...] = jnp.zeros_like(dg_ref)
            if ride:
                ride.start(ride_in, ride_out, ride_sems)

        @pl.when(q == 0)
        def _():
            acc[...] = jnp.zeros_like(acc)

        acc[...] += _dot(dy_ref[...], w_ref[...], (((1,), (1,)), ((), ())))

        @pl.when(q == nk - 1)
        def _():
            for r in range(tm // tr):
                rows = slice(r * tr, (r + 1) * tr)
                _, vjp = jax.vjp(_rms_norm, x_ref[rows], g_ref[...])
                dxn, dg = vjp(acc[rows])
                dx_ref[rows] = dres_ref[rows] + dxn
                dg_ref[...] += dg

        if ride:
            @pl.when((i == gi - 1) & (q == nk - 1))
            def _():
                ride.finish(ride_in, ride_out, ride_sems)

    hbm = pl.BlockSpec(memory_space=pltpu.HBM)
    row = pl.BlockSpec((tm, d), lambda i, q: (i, 0))
    res = pl.pallas_call(
        body, name=name, grid=(gi, nk),
        in_specs=[pl.BlockSpec((tm, tk), lambda i, q: (i, q)), pl.BlockSpec((d, tk), lambda i, q: (0, q)), row,
                  pl.BlockSpec((1, d), lambda i, q: (0, 0)), row] + [hbm] * nr,
        out_specs=[row, pl.BlockSpec((1, d), lambda i, q: (0, 0))] + [hbm] * nr,
        out_shape=[jax.ShapeDtypeStruct((t, d), F32), jax.ShapeDtypeStruct((1, d), F32)]
        + (ride.out_shape if ride else []),
        scratch_shapes=[pltpu.VMEM((tm, d), F32)] + (ride.scratch if ride else []),
        compiler_params=pltpu.CompilerParams(dimension_semantics=("arbitrary", "arbitrary"),
                                             has_side_effects=bool(ride)),
    )(dy, w, x, g, dres, *(ride.arrays if ride else ()))
    return (res[0], res[1], res[2:]) if ride else (res[0], res[1])


def _tok_call(name, fn, t, tt, tok_in, const_in, tok_out, acc_out):
    nblk = t // tt
    hb = tt // HALO
    in_specs, args = [], []
    for arr, w, cb, mode in tok_in:
        if mode == "cur":
            spec = pl.BlockSpec((tt, w), lambda i, cb=cb: (i, cb))
        elif mode == "prev":
            spec = pl.BlockSpec((HALO, w), lambda i, cb=cb: (jnp.maximum(i * hb - 1, 0), cb))
        else:
            spec = pl.BlockSpec((HALO, w), lambda i, cb=cb: (jnp.minimum(i + 1, nblk - 1), cb))
        in_specs.append(spec)
        args.append(arr)
    for arr in const_in:
        in_specs.append(pl.BlockSpec(arr.shape, lambda i: (0, 0)))
        args.append(arr)
    out_specs, out_shape = [], []
    for rows, w, dt in tok_out:
        out_specs.append(pl.BlockSpec((rows, w), lambda i: (i, 0)))
        out_shape.append(jax.ShapeDtypeStruct((nblk * rows, w), dt))
    for shp, dt in acc_out:
        out_specs.append(pl.BlockSpec(shp, lambda i: (0, 0)))
        out_shape.append(jax.ShapeDtypeStruct(shp, dt))
    n_tok, n_const, n_out = len(tok_in), len(const_in), len(tok_out)

    def body(*refs):
        i = pl.program_id(0)
        tok_vals = []
        for (_, _, _, mode), r in zip(tok_in, refs[:n_tok]):
            v = r[...]
            if mode == "prev":
                v = jnp.where(i > 0, v, jnp.zeros_like(v))
            elif mode == "next8":
                v = jnp.where(i < nblk - 1, v, jnp.zeros_like(v))
            tok_vals.append(v)
        const_vals = [r[...] for r in refs[n_tok:n_tok + n_const]]
        outs, accs = fn(tok_vals, const_vals)
        o_refs = refs[n_tok + n_const:n_tok + n_const + n_out]
        a_refs = refs[n_tok + n_const + n_out:]
        for r, v in zip(o_refs, outs):
            r[...] = v.astype(r.dtype)
        if a_refs:
            @pl.when(i == 0)
            def _():
                for r in a_refs:
                    r[...] = jnp.zeros_like(r)

            for r, v in zip(a_refs, accs):
                r[...] += v.astype(r.dtype)

    res = pl.pallas_call(
        body, name=name, grid=(nblk,), in_specs=in_specs, out_specs=out_specs, out_shape=out_shape,
        compiler_params=_params("arbitrary" if acc_out else "parallel"),
    )(*args)
    return res


def _dn_pre_math(cur, halo, ba, cw, alog, dtb):
    tt = cur.shape[0]
    a = jax.nn.silu(_causal_conv(cur, halo, cw, DN_CONV))
    pieces = []
    for p in range(2 * DN_HEADS):
        xh = a[:, p * HEAD_DIM:(p + 1) * HEAD_DIM]
        xh = xh * lax.rsqrt(jnp.sum(xh * xh, axis=-1, keepdims=True) + EPS)
        if p < DN_HEADS:
            xh = xh * (HEAD_DIM ** -0.5)
        pieces.append(xh)
    pieces.append(a[:, 2 * DN_WIDTH:])
    qkvn = jnp.concatenate(pieces, axis=1)
    lane = lax.broadcasted_iota(jnp.int32, ba.shape, 1)
    raw = jnp.where(lane < DN_HEADS, jax.nn.sigmoid(ba), -jnp.exp(alog) * jax.nn.softplus(ba + dtb))
    r = lax.broadcasted_iota(jnp.int32, (tt, tt), 0)
    c = lax.broadcasted_iota(jnp.int32, (tt, tt), 1)
    tri = jnp.where((r // CHUNK == c // CHUNK) & (c <= r), 1.0, 0.0).astype(F32)
    cums = _ein("ij,jk->ik", LHS_EXACT, tri, raw)
    bg = jnp.where(lane < DN_HEADS, raw, cums)
    return qkvn, bg


def _mix_math(o, z, gb, gc, gc_halo, hv, hv_halo, dng, scg, scw):
    outs = []
    for h in range(DN_HEADS):
        sl = slice(h * HEAD_DIM, (h + 1) * HEAD_DIM)
        oh = o[:, sl]
        outs.append(oh * lax.rsqrt(jnp.mean(oh * oh, axis=-1, keepdims=True) + EPS) * dng * jax.nn.silu(z[:, sl]))
    y = gb * _causal_conv(gc * hv, gc_halo * hv_halo, scw, SC_CONV)
    gw = SC_WIDTH // SC_GROUPS
    for g in range(SC_GROUPS):
        sl = slice(g * gw, (g + 1) * gw)
        yg = y[:, sl]
        outs.append(yg * lax.rsqrt(jnp.mean(yg * yg, axis=-1, keepdims=True) + EPS) * scg[:, sl])
    return jnp.concatenate(outs, axis=1)


def _tri_inverse(a):
    c = a.shape[-1]
    r = lax.broadcasted_iota(jnp.int32, (c, c), 0)
    q = lax.broadcasted_iota(jnp.int32, (c, c), 1)
    eye = jnp.where(r == q, 1.0, 0.0).astype(F32)[None]
    blk = (r // 16 == q // 16)[None]
    d = jnp.where(blk, a, 0.0)
    o = a - d
    mm = functools.partial(_ein, "bij,bjk->bik", PRECISE)
    p = eye - d
    n = mm(d, d)
    for _ in range(2):
        both = mm(jnp.concatenate([n, p], axis=1), n)
        n = both[:, :c]
        p = p + both[:, c:]
    p = p + mm(p, n)
    e = mm(p, o)
    e2 = mm(e, e)
    left = eye - e + e2 - mm(e, e2)
    return mm(left, p)


@jax.custom_vjp
def _inverse_known(a, tinv):
    return tinv


def _inverse_known_fwd(a, tinv):
    return tinv, tinv


def _inverse_known_bwd(tinv, ct):
    left = _ein("bji,bjk->bik", PRECISE, tinv, ct)
    return -_ein("bik,bjk->bij", PRECISE, left, tinv), jnp.zeros_like(tinv)


_inverse_known.defvjp(_inverse_known_fwd, _inverse_known_bwd)


def _delta_intra_math(q, k, v, bg, head, tinv_known=None):
    n = q.shape[0]
    nb = n // CHUNK
    lane = lax.broadcasted_iota(jnp.int32, bg.shape, 1)
    beta = jnp.sum(jnp.where(lane == head, bg, 0.0), axis=1, keepdims=True).reshape(nb, CHUNK, 1)
    gc = jnp.sum(jnp.where(lane == head + DN_HEADS, bg, 0.0), axis=1, keepdims=True).reshape(nb, CHUNK, 1)
    q3, k3, v3 = (a.reshape(nb, CHUNK, HEAD_DIM) for a in (q, k, v))
    r = lax.broadcasted_iota(jnp.int32, (CHUNK, CHUNK), 0)
    c = lax.broadcasted_iota(jnp.int32, (CHUNK, CHUNK), 1)
    eye = jnp.where(r == c, 1.0, 0.0).astype(F32)[None]
    gcr = _ein("bik,bkj->bij", LHS_EXACT, jnp.ones((nb, CHUNK, CHUNK), F32), gc * eye)
    decay = jnp.exp(jnp.where((r >= c)[None], gc - gcr, -1e30))
    kb = k3 * beta
    vb = v3 * beta
    egc = jnp.exp(gc)
    on_k = _ein("bcd,bmd->bcm", FAST, jnp.concatenate([kb, q3], axis=1), k3)
    a = jnp.where((r > c)[None], on_k[:, :CHUNK] * decay, 0.0)
    tinv = _tri_inverse(a) if tinv_known is None else _inverse_known(a, tinv_known)
    uw = _ein("bcm,bmd->bcd", PRECISE, tinv, jnp.concatenate([vb, kb * egc], axis=2))
    u, w = uw[:, :, :HEAD_DIM], uw[:, :, HEAD_DIM:]
    qk = on_k[:, CHUNK:] * decay
    row = lax.broadcasted_iota(jnp.int32, (nb, CHUNK, 1), 1)
    glast = jnp.sum(jnp.where(row == CHUNK - 1, gc, 0.0), axis=1, keepdims=True)
    qd = q3 * egc
    kd = k3 * jnp.exp(glast - gc)
    glb = jnp.broadcast_to(jnp.exp(glast), (nb, HALO, LANES))
    flat = lambda x: x.reshape(n, HEAD_DIM)
    return flat(u), flat(w), flat(qd), flat(kd), qk, glb, tinv


def _delta_step_math(u, w, qd, kd, qk, gl, s):
    c = u.shape[0]
    on_s = _ein("ck,kv->cv", FAST, jnp.concatenate([w, qd], axis=0), s)
    vnew = u - on_s[:c]
    on_v = _ein("cm,mv->cv", FAST, jnp.concatenate([qk, kd.T], axis=0), vnew)
    o = on_s[c:] + on_v[:c]
    s2 = s * gl + on_v[c:]
    return o, s2


def _delta_intra(qkvn, bg, nb):
    t = qkvn.shape[0]
    n = t // CHUNK
    rows = nb * CHUNK

    def body(q_ref, k_ref, v_ref, bg_ref, u_ref, w_ref, qd_ref, kd_ref, qk_ref, gl_ref, ti_ref):
        outs = _delta_intra_math(q_ref[...], k_ref[...], v_ref[...], bg_ref[...], pl.program_id(1))
        for r, v in zip((u_ref, w_ref, qd_ref, kd_ref, qk_ref, gl_ref, ti_ref), outs):
            r[...] = v

    col = lambda off: pl.BlockSpec((rows, HEAD_DIM), lambda b, h, off=off: (b, off + h))
    tok = jax.ShapeDtypeStruct((t, DN_WIDTH), F32)
    return pl.pallas_call(
        body, name="delta_intra", grid=(n // nb, DN_HEADS),
        in_specs=[col(0), col(DN_HEADS), col(2 * DN_HEADS), pl.BlockSpec((rows, LANES), lambda b, h: (b, 0))],
        out_specs=[col(0)] * 4 + [pl.BlockSpec((nb, None, CHUNK, CHUNK), lambda b, h: (b, h, 0, 0)),
                                  pl.BlockSpec((nb, None, HALO, LANES), lambda b, h: (b, h, 0, 0)),
                                  pl.BlockSpec((nb, None, CHUNK, CHUNK), lambda b, h: (b, h, 0, 0))],
        out_shape=[tok] * 4 + [jax.ShapeDtypeStruct((n, DN_HEADS, CHUNK, CHUNK), F32),
                               jax.ShapeDtypeStruct((n, DN_HEADS, HALO, LANES), F32),
                               jax.ShapeDtypeStruct((n, DN_HEADS, CHUNK, CHUNK), F32)],
        compiler_params=_params("parallel", "arbitrary"),
    )(qkvn, qkvn, qkvn, bg)


def _delta_intra_bwd(qkvn, bg, tinv, cts, nb):
    t = qkvn.shape[0]
    n = t // CHUNK
    rows = nb * CHUNK

    def body(q_ref, k_ref, v_ref, bg_ref, ti_ref, du, dw, dqd, dkd, dqk, dgl, dq_ref, dk_ref, dv_ref, dbg_ref):
        h = pl.program_id(1)
        ti = ti_ref[...]
        _, vjp = jax.vjp(lambda q, k, v, b: _delta_intra_math(q, k, v, b, h, ti)[:6],
                         q_ref[...], k_ref[...], v_ref[...], bg_ref[...])
        dq, dk, dv, dbg = vjp((du[...], dw[...], dqd[...], dkd[...], dqk[...], dgl[...]))
        dq_ref[...] = dq
        dk_ref[...] = dk
        dv_ref[...] = dv

        @pl.when(h == 0)
        def _():
            dbg_ref[...] = jnp.zeros_like(dbg_ref)

        dbg_ref[...] += dbg

    col = lambda off: pl.BlockSpec((rows, HEAD_DIM), lambda b, h, off=off: (b, off + h))
    bgs = pl.BlockSpec((rows, LANES), lambda b, h: (b, 0))
    qks = pl.BlockSpec((nb, None, CHUNK, CHUNK), lambda b, h: (b, h, 0, 0))
    gls = pl.BlockSpec((nb, None, HALO, LANES), lambda b, h: (b, h, 0, 0))
    tok = jax.ShapeDtypeStruct((t, DN_WIDTH), F32)
    return pl.pallas_call(
        body, name="delta_intra_bwd", grid=(n // nb, DN_HEADS),
        in_specs=[col(0), col(DN_HEADS), col(2 * DN_HEADS), bgs, qks, col(0), col(0), col(0), col(0), qks, gls],
        out_specs=[col(0), col(0), col(0), bgs],
        out_shape=[tok, tok, tok, jax.ShapeDtypeStruct((t, LANES), F32)],
        compiler_params=_params("parallel", "arbitrary"),
    )(qkvn, qkvn, qkvn, bg, tinv, *cts)


def _delta_scan(u, w, qd, kd, qk, glb, cb):
    t = u.shape[0]
    n = t // CHUNK
    rows = cb * CHUNK

    def body(u_ref, w_ref, qd_ref, kd_ref, qk_ref, gl_ref, o_ref, s_ref, s_scr):
        @pl.when(pl.program_id(0) == 0)
        def _():
            s_scr[...] = jnp.zeros_like(s_scr)

        def chunk(c, carry):
            r0 = pl.multiple_of(c * CHUNK, CHUNK)
            for h in range(DN_HEADS):
                sl = (pl.ds(r0, CHUNK), slice(h * HEAD_DIM, (h + 1) * HEAD_DIM))
                s = s_scr[h]
                s_ref[c, h] = s
                o, s2 = _delta_step_math(u_ref[sl], w_ref[sl], qd_ref[sl], kd_ref[sl], qk_ref[c, h],
                                         gl_ref[c, h][0:1, :], s)
                o_ref[sl] = o
                s_scr[h] = s2
            return carry

        lax.fori_loop(0, cb, chunk, 0)

    tok = pl.BlockSpec((rows, DN_WIDTH), lambda i: (i, 0))
    return pl.pallas_call(
        body, name="delta_scan", grid=(n // cb,),
        in_specs=[tok] * 4 + [pl.BlockSpec((cb, DN_HEADS, CHUNK, CHUNK), lambda i: (i, 0, 0, 0)),
                              pl.BlockSpec((cb, DN_HEADS, HALO, LANES), lambda i: (i, 0, 0, 0))],
        out_specs=[tok, pl.BlockSpec((cb, DN_HEADS, HEAD_DIM, HEAD_DIM), lambda i: (i, 0, 0, 0))],
        out_shape=[jax.ShapeDtypeStruct((t, DN_WIDTH), F32),
                   jax.ShapeDtypeStruct((n, DN_HEADS, HEAD_DIM, HEAD_DIM), F32)],
        scratch_shapes=[pltpu.VMEM((DN_HEADS, HEAD_DIM, HEAD_DIM), F32)],
        compiler_params=_params("arbitrary"),
    )(u, w, qd, kd, qk, glb)


def _delta_scan_bwd(u, w, qd, kd, qk, glb, s_all, do, cb):
    t = u.shape[0]
    n = t // CHUNK
    nblk = n // cb
    rows = cb * CHUNK

    def body(u_ref, w_ref, qd_ref, kd_ref, qk_ref, gl_ref, s_ref, do_ref,
             du_ref, dw_ref, dqd_ref, dkd_ref, dqk_ref, dgl_ref, ds_scr):
        @pl.when(pl.program_id(0) == 0)
        def _():
            ds_scr[...] = jnp.zeros_like(ds_scr)

        def chunk(step, carry):
            c = cb - 1 - step
            r0 = pl.multiple_of(c * CHUNK, CHUNK)
            for h in range(DN_HEADS):
                sl = (pl.ds(r0, CHUNK), slice(h * HEAD_DIM, (h + 1) * HEAD_DIM))
                gl_tile = gl_ref[c, h]
                prim = (u_ref[sl], w_ref[sl], qd_ref[sl], kd_ref[sl], qk_ref[c, h], gl_tile, s_ref[c, h])
                _, vjp = jax.vjp(lambda a, b, cc, d, e, g, s: _delta_step_math(a, b, cc, d, e, g[0:1, :], s), *prim)
                du, dw, dqd, dkd, dqk, dgl, ds = vjp((do_ref[sl], ds_scr[h]))
                du_ref[sl] = du
                dw_ref[sl] = dw
                dqd_ref[sl] = dqd
                dkd_ref[sl] = dkd
                dqk_ref[c, h] = dqk
                dgl_ref[c, h] = dgl
                ds_scr[h] = ds
            return carry

        lax.fori_loop(0, cb, chunk, 0)

    rev = lambda i: nblk - 1 - i
    tok = pl.BlockSpec((rows, DN_WIDTH), lambda i: (rev(i), 0))
    qks = pl.BlockSpec((cb, DN_HEADS, CHUNK, CHUNK), lambda i: (rev(i), 0, 0, 0))
    gls = pl.BlockSpec((cb, DN_HEADS, HALO, LANES), lambda i: (rev(i), 0, 0, 0))
    ss = pl.BlockSpec((cb, DN_HEADS, HEAD_DIM, HEAD_DIM), lambda i: (rev(i), 0, 0, 0))
    tshape = jax.ShapeDtypeStruct((t, DN_WIDTH), F32)
    return pl.pallas_call(
        body, name="delta_scan_bwd", grid=(nblk,),
        in_specs=[tok] * 4 + [qks, gls, ss, tok],
        out_specs=[tok] * 4 + [qks, gls],
        out_shape=[tshape] * 4 + [jax.ShapeDtypeStruct(qk.shape, F32), jax.ShapeDtypeStruct(glb.shape, F32)],
        scratch_shapes=[pltpu.VMEM((DN_HEADS, HEAD_DIM, HEAD_DIM), F32)],
        compiler_params=_params("arbitrary"),
    )(u, w, qd, kd, qk, glb, s_all, do)


def _peer(mask):
    x, y, c = lax.axis_index("x"), lax.axis_index("y"), lax.axis_index("c")
    return (x ^ ((mask >> 2) & 1), y ^ ((mask >> 1) & 1), c ^ (mask & 1))


def _my_index():
    return 4 * lax.axis_index("x") + 2 * lax.axis_index("y") + lax.axis_index("c")


class _Exchange:
    CHIP_MASKS = (4, 2, 6)

    def __init__(self, kind, arrays):
        self.kind = kind
        self.arrays = list(arrays)
        self.na = na = len(self.arrays)
        if kind == "gather":
            self.out_shape = [jax.ShapeDtypeStruct((N_DEV,) + a.shape, a.dtype) for a in self.arrays]
        else:
            self.out_shape = [jax.ShapeDtypeStruct(a.shape, a.dtype) for a in self.arrays]
        self.scratch = [pltpu.SemaphoreType.DMA((na, 7)), pltpu.SemaphoreType.DMA((na, 7)),
                        pltpu.SemaphoreType.DMA((na,))]

    def _copies(self, ins, outs, sems):
        send_sems, recv_sems, local_sems = sems
        me = _my_index()
        local, first, passed, arrivals = [], [], [], []
        if self.kind == "gather":
            def rc(a, k, block, to, own=False):
                def make():
                    dst = outs[a].at[block]
                    return pltpu.make_async_remote_copy(src_ref=ins[a] if own else dst, dst_ref=dst,
                                                        send_sem=send_sems.at[a, k], recv_sem=recv_sems.at[a, k],
                                                        device_id=to, device_id_type=MESH)
                return make

            sib = _peer(1)
            for a in range(self.na):
                local.append(lambda a=a: pltpu.make_async_copy(ins[a], outs[a].at[me], local_sems.at[a]))
                first.append(rc(a, 0, me, sib, own=True))
                arrivals.append(rc(a, 0, me ^ 1, _peer(0)))
                for j, m in enumerate(self.CHIP_MASKS):
                    first.append(rc(a, 1 + j, me, _peer(m), own=True))
                    passed.append((rc(a, 1 + j, me ^ m, _peer(0)), rc(a, 4 + j, me ^ m, sib)))
                    arrivals.append(rc(a, 4 + j, me ^ m ^ 1, _peer(0)))
        else:
            for a in range(self.na):
                local.append(lambda a=a: pltpu.make_async_copy(ins[a].at[me], outs[a].at[me], local_sems.at[a]))
                for m in range(1, N_DEV):
                    def make(a=a, m=m):
                        return pltpu.make_async_remote_copy(
                            src_ref=ins[a].at[me ^ m], dst_ref=outs[a].at[me], send_sem=send_sems.at[a, m - 1],
                            recv_sem=recv_sems.at[a, m - 1], device_id=_peer(m), device_id_type=MESH)
                    first.append(make)
                    arrivals.append(make)
        return local, first, passed, arrivals

    def start(self, ins, outs, sems):
        local, first, _, _ = self._copies(ins, outs, sems)
        for make in local + first:
            make().start()

    def finish(self, ins, outs, sems):
        local, first, passed, arrivals = self._copies(ins, outs, sems)
        for landed, onward in passed:
            landed().wait_recv()
            onward().start()
        for make in arrivals:
            make().wait_recv()
        for make in first + [p for _, p in passed]:
            make().wait_send()
        for make in local:
            make().wait()

    def run(self, name):
        na = self.na

        def body(*refs):
            ins, outs, sems = refs[:na], refs[na:2 * na], refs[2 * na:]
            self.start(ins, outs, sems)
            self.finish(ins, outs, sems)

        hbm = pl.BlockSpec(memory_space=pltpu.HBM)
        return pl.pallas_call(
            body, name=name, in_specs=[hbm] * na, out_specs=[hbm] * na, out_shape=self.out_shape,
            scratch_shapes=self.scratch, compiler_params=pltpu.CompilerParams(has_side_effects=True),
        )(*self.arrays)


def _all_gather(shards, name):
    return _Exchange("gather", shards).run(name)


def _adamw_math(w, g, m, v):
    m2 = ADAM_B1 * m + (1.0 - ADAM_B1) * g
    v2 = ADAM_B2 * v + (1.0 - ADAM_B2) * jnp.square(g)
    m_hat = m2 / (1.0 - ADAM_B1 ** ADAM_STEP)
    v_hat = v2 / (1.0 - ADAM_B2 ** ADAM_STEP)
    delta = -ADAM_LR * (m_hat / (jnp.sqrt(v_hat) + ADAM_EPS) + ADAM_WD * w)
    return delta, m2, v2


def _sum_adamw(parts, w, m, v, name):
    r, c = w.shape
    tr = _pick(r, (512, 256, 352, 128, 64, 32, 16, 8))
    np_ = parts.shape[0]

    def body(p_ref, w_ref, m_ref, v_ref, g_ref, d_ref, m2_ref, v2_ref):
        g = p_ref[0].astype(F32)
        for d in range(1, np_):
            g = g + p_ref[d].astype(F32)
        delta, m2, v2 = _adamw_math(w_ref[...], g, m_ref[...], v_ref[...])
        g_ref[...] = g
        d_ref[...] = delta
        m2_ref[...] = m2
        v2_ref[...] = v2

    blk = pl.BlockSpec((tr, c), lambda i: (i, 0))
    shp = jax.ShapeDtypeStruct((r, c), F32)
    return pl.pallas_call(
        body, name=name, grid=(r // tr,),
        in_specs=[pl.BlockSpec((np_, tr, c), lambda i: (0, i, 0)), blk, blk, blk],
        out_specs=[blk] * 4, out_shape=[shp] * 4, compiler_params=_params("parallel"),
    )(parts, w, m, v)


def _sum_rows(parts, name):
    np_, r, c = parts.shape

    def body(p_ref, o_ref):
        g = p_ref[0]
        for d in range(1, np_):
            g = g + p_ref[d]
        o_ref[...] = g

    return pl.pallas_call(body, name=name, out_shape=jax.ShapeDtypeStruct((r, c), F32))(parts)


def _pad_w_in(w):
    d = w.shape[0]
    n_ba = 2 * DN_HEADS
    a = w[:, :SC_OFF]
    ba = w[:, SC_OFF:SC_OFF + n_ba]
    sc = w[:, SC_OFF + n_ba:]
    return jnp.concatenate([a, sc, ba, jnp.zeros((d, BA_W - n_ba), w.dtype)], axis=1)


def _unpad_w_in(wp):
    n_ba = 2 * DN_HEADS
    return jnp.concatenate([wp[:, :SC_OFF], wp[:, BA_OFF:BA_OFF + n_ba], wp[:, SC_OFF:BA_OFF]], axis=1)


def _lane_row(v, off):
    return jnp.pad(v.astype(F32), (off, LANES - off - v.shape[0]))[None]


TT = 256
NB_INTRA = 8
CB_SCAN = 8


def _layer_fwd(x, p, ride_proj, ride_ffn, late):
    t, d = x.shape
    got_proj = got_ffn = None
    if ride_proj:
        h, proj, got_proj = _norm_matmul(x, p["norm1_g"], p["w_in"], "proj_fwd_gather", ride=ride_proj)
    else:
        h, proj = _norm_matmul(x, p["norm1_g"], p["w_in"], "proj_fwd")
    p = {**p, **late(got_proj)}
    qkvn, bg = _tok_call(
        "dn_pre", lambda tv, cv: (_dn_pre_math(*tv, *cv), ()), t, TT,
        [(proj, QKV_W, 0, "cur"), (proj, QKV_W, 0, "prev"), (proj, LANES, BA_OFF // LANES, "cur")],
        [p["dn_conv_w"], p["alog_row"], p["dt_row"]],
        [(TT, QKV_W, F32), (TT, LANES, F32)], [])
    u, w, qd, kd, qk, glb, tinv = _delta_intra(qkvn, bg, NB_INTRA)
    o, s_all = _delta_scan(u, w, qd, kd, qk, glb, CB_SCAN)
    cb0 = SC_OFF // SC_WIDTH
    mix_in = [(o, DN_WIDTH, 0, "cur"), (proj, DN_WIDTH, Z_OFF // DN_WIDTH, "cur"),
              (proj, SC_WIDTH, cb0, "cur"), (proj, SC_WIDTH, cb0 + 1, "cur"), (proj, SC_WIDTH, cb0 + 1, "prev"),
              (proj, SC_WIDTH, cb0 + 2, "cur"), (proj, SC_WIDTH, cb0 + 2, "prev")]
    mix_const = [p["dn_norm_g"], p["sc_norm_g"], p["sc_conv_w"]]
    (cat,) = _tok_call("mix_post", lambda tv, cv: ((_mix_math(*tv, *cv),), ()), t, TT,
                       mix_in, mix_const, [(TT, 2 * DN_WIDTH, MXU_DTYPE)], [])
    x_mid = _matmul(cat, p["w_out"], "nn", F32, "out_proj", residual=x)
    if ride_ffn:
        h2, gu, act, got_ffn = _ffn_up_swiglu(x_mid, p["norm2_g"], p["w_gu"], "ffn_up_gather", ride=ride_ffn)
    else:
        h2, gu, act = _ffn_up_swiglu(x_mid, p["norm2_g"], p["w_gu"], "ffn_up")
    x_out = _matmul(act, p["w_down"], "nn", F32, "ffn_down", residual=x_mid)
    saved = dict(x=x, h=h, proj=proj, qkvn=qkvn, bg=bg, u=u, w=w, qd=qd, kd=kd, qk=qk, glb=glb, tinv=tinv, s_all=s_all, o=o,
                 cat=cat, x_mid=x_mid, h2=h2, gu=gu, act=act, mix_in=mix_in, mix_const=mix_const)
    return x_out, saved, p, got_proj, got_ffn


def _layer_bwd(dx_out, p, s, ride_prev, ride_gu, ride_down):
    t, d = dx_out.shape
    got = {}
    dgu = _ffn_down_dx_swiglu(dx_out, p["w_down"], s["gu"], "ffn_down_dx")
    d_w_down = _matmul(s["act"], dx_out, "tn", MXU_DTYPE, "ffn_down_dw")
    if ride_prev:
        dx_mid, d_norm2, got["prev"] = _matmul_norm_bwd(dgu, p["w_gu"], s["x_mid"], p["norm2_g"], dx_out,
                                                        "ffn_up_dx_scatter", ride=ride_prev)
    else:
        dx_mid, d_norm2 = _matmul_norm_bwd(dgu, p["w_gu"], s["x_mid"], p["norm2_g"], dx_out, "ffn_up_dx")
    d_w_gu = _matmul(s["h2"], dgu, "tn", MXU_DTYPE, "ffn_up_dw")
    dcat = _matmul(dx_mid, p["w_out"], "nt", F32, "out_proj_dx")
    d_w_out = _matmul(s["cat"], dx_mid, "tn", MXU_DTYPE, "out_proj_dw")

    def mix_bwd(tv, cv):
        prim = tuple(tv[:7]) + tuple(cv)
        _, vjp = jax.vjp(_mix_math, *prim)
        do, dz, dgb, dgc, dgch, dhv, dhvh, ddng, dscg, dscw = vjp(tv[7])
        return (do, dz, dgb, dgc, dgch, dhv, dhvh), (ddng, dscg, dscw)

    wide = (TT, DN_WIDTH, F32)
    halo = (HALO, SC_WIDTH, F32)
    do, dz, dgb, dgc, dgc_h, dhv, dhv_h, d_dn_norm, d_sc_norm, d_sc_conv = _tok_call(
        "mix_post_bwd", mix_bwd, t, TT, s["mix_in"] + [(dcat, 2 * DN_WIDTH, 0, "cur")], s["mix_const"],
        [wide, wide, wide, wide, halo, wide, halo],
        [((1, HEAD_DIM), F32), ((1, SC_WIDTH), F32), ((SC_CONV, SC_WIDTH), F32)])
    cts = _delta_scan_bwd(s["u"], s["w"], s["qd"], s["kd"], s["qk"], s["glb"], s["s_all"], do, CB_SCAN)
    dq, dk, dv, dbg = _delta_intra_bwd(s["qkvn"], s["bg"], s["tinv"], cts, NB_INTRA)
    proj = s["proj"]

    def dn_pre_bwd(tv, cv):
        cur, hal, ba, dq_, dk_, dv_, dbg_ = tv
        _, vjp = jax.vjp(_dn_pre_math, cur, hal, ba, *cv)
        dcur, dhal, dba, dcw, dal, ddt = vjp((jnp.concatenate([dq_, dk_, dv_], axis=1), dbg_))
        return (dcur, dhal, dba), (dcw, dal, ddt)

    dqkv, dqkv_h, dba, d_dn_conv, d_alog, d_dt = _tok_call(
        "dn_pre_bwd", dn_pre_bwd, t, TT,
        [(proj, QKV_W, 0, "cur"), (proj, QKV_W, 0, "prev"), (proj, LANES, BA_OFF // LANES, "cur"),
         (dq, DN_WIDTH, 0, "cur"), (dk, DN_WIDTH, 0, "cur"), (dv, DN_WIDTH, 0, "cur"), (dbg, LANES, 0, "cur")],
        [p["dn_conv_w"], p["alog_row"], p["dt_row"]],
        [(TT, QKV_W, F32), (HALO, QKV_W, F32), (TT, LANES, F32)],
        [((DN_CONV, QKV_W), F32), ((1, LANES), F32), ((1, LANES), F32)])

    def assemble(tv, cv):
        dqkv_, dqkv_n, dz_, dgb_, dgc_, dgc_n, dhv_, dhv_n, dba_ = tv

        def with_halo(cur, nxt):
            return cur + jnp.concatenate([jnp.zeros((TT - HALO, cur.shape[1]), F32), nxt], axis=0)

        out = jnp.concatenate([with_halo(dqkv_, dqkv_n), dz_, dgb_, with_halo(dgc_, dgc_n), with_halo(dhv_, dhv_n),
                               dba_, jnp.zeros((TT, BA_W - LANES), F32)], axis=1)
        return (out,), ()

    (dproj,) = _tok_call(
        "dproj_assemble", assemble, t, TT,
        [(dqkv, QKV_W, 0, "cur"), (dqkv_h, QKV_W, 0, "next8"), (dz, DN_WIDTH, 0, "cur"), (dgb, SC_WIDTH, 0, "cur"),
         (dgc, SC_WIDTH, 0, "cur"), (dgc_h, SC_WIDTH, 0, "next8"), (dhv, SC_WIDTH, 0, "cur"),
         (dhv_h, SC_WIDTH, 0, "next8"), (dba, LANES, 0, "cur")], [],
        [(TT, PROJ_W, MXU_DTYPE)], [])
    dx_in, d_norm1, got["gu"] = _matmul_norm_bwd(dproj, p["w_in"], s["x"], p["norm1_g"], dx_mid,
                                                 "proj_dx_scatter", ride=ride_gu(d_w_gu))
    d_w_in, got["down"] = _matmul(s["h"], dproj, "tn", MXU_DTYPE, "proj_dw_scatter", ride=ride_down(d_w_down))
    grads = dict(w_in=d_w_in, w_out=d_w_out, w_gu=d_w_gu, w_down=d_w_down, norm1_g=d_norm1, norm2_g=d_norm2,
                 dn_norm_g=d_dn_norm, sc_norm_g=d_sc_norm, sc_conv_w=d_sc_conv, dn_conv_w=d_dn_conv,
                 alog=d_alog, dt=d_dt)
    return dx_in, grads, got


def _final_loss(x, g, target):
    t, d = x.shape

    def fn(tv, cv):
        xv, tg = tv

        def loss_fn(xx, gg):
            err = jnp.square(_rms_norm(xx, gg) - tg)
            return 0.5 * jnp.sum(jnp.mean(err, axis=-1))

        loss, vjp = jax.vjp(loss_fn, xv, cv[0])
        dx, dg = vjp(jnp.ones((), F32))
        return (dx,), (jnp.full((1, LANES), loss, F32), dg)

    return _tok_call("final_loss", fn, t, TT, [(x, d, 0, "cur"), (target, d, 0, "cur")], [g],
                     [(TT, d, F32)], [((1, LANES), F32), ((1, d), F32)])


def _pack_rows(arrs):
    rows, offs, r0 = [], [], 0
    for a in arrs:
        n = a.size
        nr = -(-n // LANES)
        flat = jnp.pad(a.reshape(-1).astype(F32), (0, nr * LANES - n))
        rows.append(flat.reshape(nr, LANES))
        offs.append((r0, nr, a.shape))
        r0 += nr
    pad = (-r0) % 8
    if pad:
        rows.append(jnp.zeros((pad, LANES), F32))
    return jnp.concatenate(rows, axis=0), offs


def _unpack_rows(packed, offs):
    out = []
    for r0, nr, shp in offs:
        n = 1
        for s_ in shp:
            n *= s_
        out.append(packed[r0:r0 + nr].reshape(-1)[:n].reshape(shp))
    return out


def kernel(x, norm1_g, w_in, dn_conv_w, dn_a_log, dn_dt_bias, dn_norm_g, sc_conv_w, sc_norm_g, w_out, norm2_g, ffn_w_gate, ffn_w_up, ffn_w_down, final_norm_g, loss_target, m_norm1_g, m_w_in, m_dn_conv_w, m_dn_a_log, m_dn_dt_bias, m_dn_norm_g, m_sc_conv_w, m_sc_norm_g, m_w_out, m_norm2_g, m_ffn_w_gate, m_ffn_w_up, m_ffn_w_down, m_final_norm_g, v_norm1_g, v_w_in, v_dn_conv_w, v_dn_a_log, v_dn_dt_bias, v_dn_norm_g, v_sc_conv_w, v_sc_norm_g, v_w_out, v_norm2_g, v_ffn_w_gate, v_ffn_w_up, v_ffn_w_down, v_final_norm_g):
    depth, d, cin = w_in.shape
    t = x.shape[1]
    dff_s = ffn_w_gate.shape[2]
    me = _my_index()
    x2 = x.reshape(t, d)
    tgt = loss_target.reshape(t, d)

    conv_pack, conv_offs = _pack_rows([dn_conv_w, sc_conv_w])
    (conv_all,) = _all_gather([conv_pack], "gather_conv")
    dn_parts, sc_parts = zip(*[_unpack_rows(conv_all[j], conv_offs) for j in range(N_DEV)])
    dn_conv_full = jnp.concatenate(dn_parts, axis=2)
    sc_conv_full = jnp.concatenate(sc_parts, axis=2)

    def shards(l):
        return [a[l].astype(MXU_DTYPE) for a in (w_in, w_out, ffn_w_gate, ffn_w_up, ffn_w_down)]

    def mixer_params(l, g_in, g_out):
        full_in = g_in.transpose(1, 0, 2).reshape(d, N_DEV * cin)
        return dict(
            w_in=_pad_w_in(full_in), w_out=g_out.reshape(d, d),
            norm1_g=norm1_g[l][None], norm2_g=norm2_g[l][None], dn_norm_g=dn_norm_g[l][None],
            sc_norm_g=sc_norm_g[l][None], dn_conv_w=dn_conv_full[l], sc_conv_w=sc_conv_full[l],
            alog_row=_lane_row(dn_a_log[l], DN_HEADS), dt_row=_lane_row(dn_dt_bias[l], DN_HEADS))

    def ffn_params(g_gate, g_up, g_down):
        full_gate = g_gate.transpose(1, 0, 2).reshape(d, N_DEV * dff_s)
        full_up = g_up.transpose(1, 0, 2).reshape(d, N_DEV * dff_s)
        return dict(w_gu=_interleave_gu(full_gate, full_up), w_down=g_down.reshape(N_DEV * dff_s, d))

    nxt = mixer_params(0, *_all_gather(shards(0)[:2], "gather_first"))
    nxt_ffn = None
    params, saved = [], []
    xc = x2
    for l in range(depth):
        last = l + 1 == depth
        own_ffn = shards(l)[2:] if nxt_ffn is None else []
        ahead = [] if last else shards(l + 1)
        ride_proj = _Exchange("gather", own_ffn + ahead[:2]) if own_ffn or ahead else None
        ride_ffn = None if last else _Exchange("gather", ahead[2:])

        def late(got, own_ffn=own_ffn, nxt_ffn=nxt_ffn):
            return ffn_params(*got[:3]) if own_ffn else nxt_ffn

        xc, s, p_l, got_proj, got_ffn = _layer_fwd(xc, nxt, ride_proj, ride_ffn, late)
        params.append(p_l)
        saved.append(s)
        if not last:
            nxt = mixer_params(l + 1, *got_proj[len(own_ffn):])
            nxt_ffn = ffn_params(*got_ffn)
    dx, loss_part, d_final = _final_loss(xc, final_norm_g[None], tgt)

    names = ("w_in", "w_out", "ffn_w_gate", "ffn_w_up", "ffn_w_down")
    big_out = {n: {k: [None] * depth for k in ("g", "d", "m", "v")} for n in names}
    w_loc = dict(w_in=w_in, w_out=w_out, ffn_w_gate=ffn_w_gate, ffn_w_up=ffn_w_up, ffn_w_down=ffn_w_down)
    m_loc = dict(w_in=m_w_in, w_out=m_w_out, ffn_w_gate=m_ffn_w_gate, ffn_w_up=m_ffn_w_up, ffn_w_down=m_ffn_w_down)
    v_loc = dict(w_in=v_w_in, w_out=v_w_out, ffn_w_gate=v_ffn_w_gate, ffn_w_up=v_ffn_w_up, ffn_w_down=v_ffn_w_down)

    cols = lambda a, c: a.reshape(a.shape[0], N_DEV, c).transpose(1, 0, 2).astype(MXU_DTYPE)

    def ride_gu(d_w_gu):
        g_gate, g_up = _split_gu(d_w_gu)
        return _Exchange("scatter", [cols(g_gate, dff_s), cols(g_up, dff_s)])

    def ride_down(d_w_down):
        return _Exchange("scatter", [d_w_down.reshape(N_DEV, dff_s, d).astype(MXU_DTYPE)])

    def apply(l, which, recv):
        for n, r in zip(which, recv):
            res = _sum_adamw(r, w_loc[n][l], m_loc[n][l], v_loc[n][l], "adamw_" + n)
            for k, a in zip(("g", "d", "m", "v"), res):
                big_out[n][k][l] = a

    grads = [None] * depth
    pending = None
    for l in reversed(range(depth)):
        ride_prev = _Exchange("scatter", pending) if pending else None
        dx, grads[l], got = _layer_bwd(dx, params[l], saved[l], ride_prev, ride_gu, ride_down)
        if ride_prev:
            apply(l + 1, names[:2], got["prev"])
        apply(l, names[2:], [*got["gu"], *got["down"]])
        pending = [cols(_unpad_w_in(grads[l]["w_in"]), cin),
                   grads[l]["w_out"].reshape(N_DEV, d // N_DEV, d).astype(MXU_DTYPE)]
    apply(0, names[:2], _Exchange("scatter", pending).run("scatter_last"))
    grad_x = dx.reshape(x.shape)
    big_out = {n: {k: jnp.stack(v_) for k, v_ in o.items()} for n, o in big_out.items()}

    stack = lambda key: jnp.stack([grads[l][key] for l in range(depth)])
    small_parts = [stack("norm1_g").reshape(depth, d), stack("norm2_g").reshape(depth, d), d_final.reshape(d),
                   stack("dn_norm_g").reshape(depth, HEAD_DIM), stack("sc_norm_g").reshape(depth, SC_WIDTH),
                   stack("alog").reshape(depth, LANES), stack("dt").reshape(depth, LANES),
                   stack("dn_conv_w"), stack("sc_conv_w"), loss_part]
    small_pack, small_offs = _pack_rows(small_parts)
    (small_all,) = _all_gather([small_pack], "gather_small")
    total = _sum_rows(small_all, "sum_small")
    (g_n1, g_n2, g_fin, g_dnn, g_scn, g_alog, g_dt, g_dnc, g_scc, loss_row) = _unpack_rows(total, small_offs)
    loss = loss_row[0, 0]
    g_alog = g_alog[:, DN_HEADS:2 * DN_HEADS]
    g_dt = g_dt[:, DN_HEADS:2 * DN_HEADS]
    dnc_w = dn_conv_w.shape[2]
    scc_w = sc_conv_w.shape[2]
    g_dnc = lax.dynamic_slice_in_dim(g_dnc, me * dnc_w, dnc_w, axis=2)
    g_scc = lax.dynamic_slice_in_dim(g_scc, me * scc_w, scc_w, axis=2)
    sm_g = [g_n1, g_dnc, g_alog, g_dt, g_dnn, g_scc, g_scn, g_n2, g_fin]
    sm_w = [norm1_g, dn_conv_w, dn_a_log, dn_dt_bias, dn_norm_g, sc_conv_w, sc_norm_g, norm2_g, final_norm_g]
    sm_m = [m_norm1_g, m_dn_conv_w, m_dn_a_log, m_dn_dt_bias, m_dn_norm_g, m_sc_conv_w, m_sc_norm_g, m_norm2_g, m_final_norm_g]
    sm_v = [v_norm1_g, v_dn_conv_w, v_dn_a_log, v_dn_dt_bias, v_dn_norm_g, v_sc_conv_w, v_sc_norm_g, v_norm2_g, v_final_norm_g]
    pg, offs = _pack_rows(sm_g)
    pw, _ = _pack_rows(sm_w)
    pm, _ = _pack_rows(sm_m)
    pv, _ = _pack_rows(sm_v)
    sg, sd, sm_, sv = _sum_adamw(pg[None], pw, pm, pv, "adamw_small")
    small_out = {k: _unpack_rows(a, offs) for k, a in zip(("g", "d", "m", "v"), (sg, sd, sm_, sv))}

    def outputs(k):
        s_ = small_out[k]
        b = big_out
        return [s_[0], b["w_in"][k], s_[1], s_[2], s_[3], s_[4], s_[5], s_[6], b["w_out"][k], s_[7],
                b["ffn_w_gate"][k], b["ffn_w_up"][k], b["ffn_w_down"][k], s_[8]]

    return (loss, grad_x, *outputs("g"), *outputs("d"), *outputs("m"), *outputs("v"))
```

```python
import functools

import jax
import jax.numpy as jnp
from jax import lax
from jax.experimental import pallas as pl
from jax.experimental.pallas import tpu as pltpu

F32 = jnp.float32
MXU_DTYPE = jnp.bfloat16
MESH = pl.DeviceIdType.MESH

N_DEV = 8
EPS = 1e-6
DN_HEADS = 4
HEAD_DIM = 128
DN_WIDTH = DN_HEADS * HEAD_DIM
SC_WIDTH = 512
SC_GROUPS = 4
DN_CONV = 4
SC_CONV = 3
CHUNK = 64
HALO = 8
LANES = 128

QKV_W = 3 * DN_WIDTH
Z_OFF = QKV_W
SC_OFF = Z_OFF + DN_WIDTH
BA_OFF = SC_OFF + 3 * SC_WIDTH
BA_W = 256
PROJ_W = BA_OFF + BA_W

ADAM_LR = 0.001
ADAM_B1 = 0.9
ADAM_B2 = 0.999
ADAM_EPS = 1e-08
ADAM_WD = 0.01
ADAM_STEP = 10


def _pick(n, cands):
    for c in cands:
        if n % c == 0:
            return c
    return n


def _params(*sem):
    return pltpu.CompilerParams(dimension_semantics=sem)


def _rms_norm(x, g):
    return x * lax.rsqrt(jnp.mean(x * x, axis=-1, keepdims=True) + EPS) * g


def _dot(a, b, dims=(((1,), (0,)), ((), ()))):
    return lax.dot_general(a.astype(MXU_DTYPE), b.astype(MXU_DTYPE), dims, preferred_element_type=F32)


def _split_terms(x, terms):
    out = []
    for _ in range(terms):
        hi = x.astype(MXU_DTYPE)
        out.append(hi)
        x = x - hi.astype(F32)
    return out


def _ein_impl(spec, terms, a, b):
    ta, tb = terms
    if ta == 1 and tb == 1:
        return jnp.einsum(spec, a.astype(MXU_DTYPE), b.astype(MXU_DTYPE), preferred_element_type=F32)
    pa, pb = _split_terms(a, ta), _split_terms(b, tb)
    order = max(ta, tb) - 1
    acc = None
    for deg in range(order, -1, -1):
        for i in range(ta):
            j = deg - i
            if 0 <= j < tb:
                t = jnp.einsum(spec, pa[i], pb[j], preferred_element_type=F32)
                acc = t if acc is None else acc + t
    return acc


@functools.partial(jax.custom_vjp, nondiff_argnums=(0, 1))
def _ein(spec, terms, a, b):
    return _ein_impl(spec, terms, a, b)


def _ein_fwd(spec, terms, a, b):
    return _ein_impl(spec, terms, a, b), (a, b)


def _ein_bwd(spec, terms, res, ct):
    a, b = res
    xy, z = spec.split("->")
    x, y = xy.split(",")
    tc = min(max(terms), 2)
    da = _ein_impl(f"{z},{y}->{x}", (tc, terms[1]), ct, b)
    db = _ein_impl(f"{x},{z}->{y}", (terms[0], tc), a, ct)
    return da, db


_ein.defvjp(_ein_fwd, _ein_bwd)

FAST = (1, 1)
PRECISE = (2, 2)
LHS_EXACT = (1, 3)


def _causal_conv(cur, halo, w, k):
    tt = cur.shape[0]
    xp = jnp.concatenate([halo, cur], axis=0)
    y = None
    for j in range(k):
        start = HALO - (k - 1) + j
        term = xp[start:start + tt] * w[j:j + 1]
        y = term if y is None else y + term
    return y


def _matmul(a, b, mode, out_dtype, name, residual=None, ride=None):
    if mode == "nn":
        (m, k), (k2, n) = a.shape, b.shape
    elif mode == "nt":
        (m, k), (n, k2) = a.shape, b.shape
    else:
        (k, m), (k2, n) = a.shape, b.shape
    assert k == k2
    tm = _pick(m, (1024, 1408, 512, 256, 128))
    tn = _pick(n, (1280, 1408, 1024, 512, 256, 128))
    tk = k if k <= 2816 else _pick(k, (1024, 768, 512))
    gi, gj, nk = m // tm, n // tn, k // tk
    dims = {"nn": (((1,), (0,)), ((), ())), "nt": (((1,), (1,)), ((), ())), "tn": (((0,), (0,)), ((), ()))}[mode]
    a_spec = {"nn": pl.BlockSpec((tm, tk), lambda i, j, q: (i, q)),
              "nt": pl.BlockSpec((tm, tk), lambda i, j, q: (i, q)),
              "tn": pl.BlockSpec((tk, tm), lambda i, j, q: (q, i))}[mode]
    b_spec = {"nn": pl.BlockSpec((tk, tn), lambda i, j, q: (q, j)),
              "nt": pl.BlockSpec((tn, tk), lambda i, j, q: (j, q)),
              "tn": pl.BlockSpec((tk, tn), lambda i, j, q: (q, j))}[mode]
    o_spec = pl.BlockSpec((tm, tn), lambda i, j, q: (i, j))
    has_res = residual is not None
    n_in = 3 if has_res else 2
    nr = ride.na if ride else 0

    def body(*refs):
        a_ref, b_ref = refs[0], refs[1]
        r_ref = refs[2] if has_res else None
        ride_in = refs[n_in:n_in + nr]
        o_ref = refs[n_in + nr]
        ride_out = refs[n_in + nr + 1:n_in + 2 * nr + 1]
        acc = refs[n_in + 2 * nr + 1]
        ride_sems = refs[n_in + 2 * nr + 2:]
        i, j, q = pl.program_id(0), pl.program_id(1), pl.program_id(2)
        if ride:
            @pl.when((i == 0) & (j == 0) & (q == 0))
            def _():
                ride.start(ride_in, ride_out, ride_sems)

        @pl.when(q == 0)
        def _():
            acc[...] = jnp.zeros_like(acc)

        acc[...] += _dot(a_ref[...], b_ref[...], dims)

        @pl.when(q == nk - 1)
        def _():
            r = acc[...]
            if has_res:
                r = r + r_ref[...]
            o_ref[...] = r.astype(o_ref.dtype)

        if ride:
            @pl.when((i == gi - 1) & (j == gj - 1) & (q == nk - 1))
            def _():
                ride.finish(ride_in, ride_out, ride_sems)

    hbm = pl.BlockSpec(memory_space=pltpu.HBM)
    in_specs = [a_spec, b_spec] + ([o_spec] if has_res else []) + [hbm] * nr
    args = (a, b) + ((residual,) if has_res else ()) + (tuple(ride.arrays) if ride else ())
    res = pl.pallas_call(
        body, name=name, grid=(gi, gj, nk), in_specs=in_specs, out_specs=[o_spec] + [hbm] * nr,
        out_shape=[jax.ShapeDtypeStruct((m, n), out_dtype)] + (ride.out_shape if ride else []),
        scratch_shapes=[pltpu.VMEM((tm, tn), F32)] + (ride.scratch if ride else []),
        compiler_params=pltpu.CompilerParams(
            dimension_semantics=("arbitrary",) * 3 if ride else ("parallel", "parallel", "arbitrary"),
            has_side_effects=bool(ride)),
    )(*args)
    return (res[0], res[1:]) if ride else res[0]


def _norm_matmul(x, g, w, name, ride=None):
    t, d = x.shape
    n = w.shape[1]
    tm = _pick(t, (1024, 512, 256, 128))
    tn = _pick(n, (1280, 1408, 1024, 512, 256, 128))
    gi, gj = t // tm, n // tn
    nr = ride.na if ride else 0

    def body(*refs):
        x_ref, g_ref, w_ref = refs[:3]
        ride_in = refs[3:3 + nr]
        h_ref, y_ref = refs[3 + nr], refs[4 + nr]
        ride_out = refs[5 + nr:5 + 2 * nr]
        h_scr = refs[5 + 2 * nr]
        ride_sems = refs[6 + 2 * nr:]
        i, j = pl.program_id(0), pl.program_id(1)
        if ride:
            @pl.when((i == 0) & (j == 0))
            def _():
                ride.start(ride_in, ride_out, ride_sems)

        @pl.when(j == 0)
        def _():
            h = _rms_norm(x_ref[...], g_ref[...]).astype(MXU_DTYPE)
            h_scr[...] = h
            h_ref[...] = h

        y_ref[...] = _dot(h_scr[...], w_ref[...])

        if ride:
            @pl.when((i == gi - 1) & (j == gj - 1))
            def _():
                ride.finish(ride_in, ride_out, ride_sems)

    hbm = pl.BlockSpec(memory_space=pltpu.HBM)
    res = pl.pallas_call(
        body, name=name, grid=(gi, gj),
        in_specs=[pl.BlockSpec((tm, d), lambda i, j: (i, 0)), pl.BlockSpec((1, d), lambda i, j: (0, 0)),
                  pl.BlockSpec((d, tn), lambda i, j: (0, j))] + [hbm] * nr,
        out_specs=[pl.BlockSpec((tm, d), lambda i, j: (i, 0)), pl.BlockSpec((tm, tn), lambda i, j: (i, j))] + [hbm] * nr,
        out_shape=[jax.ShapeDtypeStruct((t, d), MXU_DTYPE), jax.ShapeDtypeStruct((t, n), F32)]
        + (ride.out_shape if ride else []),
        scratch_shapes=[pltpu.VMEM((tm, d), MXU_DTYPE)] + (ride.scratch if ride else []),
        compiler_params=pltpu.CompilerParams(
            dimension_semantics=("arbitrary",) * 2 if ride else ("parallel", "arbitrary"),
            has_side_effects=bool(ride)),
    )(x, g, w, *(ride.arrays if ride else ()))
    return (res[0], res[1], res[2:]) if ride else (res[0], res[1])


def _swiglu_math(g, u):
    return jax.nn.silu(g) * u


def _gu_tile(dff):
    return _pick(dff, (1408, 1024, 512, 256, 128))


def _interleave_gu(gate, up):
    tn = _gu_tile(gate.shape[1])
    pieces = []
    for j in range(gate.shape[1] // tn):
        pieces += [gate[:, j * tn:(j + 1) * tn], up[:, j * tn:(j + 1) * tn]]
    return jnp.concatenate(pieces, axis=1)


def _split_gu(gu):
    dff = gu.shape[1] // 2
    tn = _gu_tile(dff)
    tiles = [gu[:, j * tn:(j + 1) * tn] for j in range(2 * dff // tn)]
    return jnp.concatenate(tiles[0::2], axis=1), jnp.concatenate(tiles[1::2], axis=1)


def _ffn_up_swiglu(x, g, w_gu, name, ride=None):
    t, d = x.shape
    dff = w_gu.shape[1] // 2
    tn = _gu_tile(dff)
    tm = _pick(t, (512, 256, 128))
    gj, gi = dff // tn, t // tm
    nr = ride.na if ride else 0

    def body(*refs):
        x_ref, g_ref, w_ref = refs[:3]
        ride_in = refs[3:3 + nr]
        h_ref, gu_ref, act_ref = refs[3 + nr:6 + nr]
        ride_out = refs[6 + nr:6 + 2 * nr]
        ride_sems = refs[6 + 2 * nr:]
        j, i = pl.program_id(0), pl.program_id(1)
        if ride:
            @pl.when((i == 0) & (j == 0))
            def _():
                ride.start(ride_in, ride_out, ride_sems)

        h = _rms_norm(x_ref[...], g_ref[...]).astype(MXU_DTYPE)

        @pl.when(j == 0)
        def _():
            h_ref[...] = h

        y = _dot(h, w_ref[...])
        gu_ref[...] = y.astype(gu_ref.dtype)
        act_ref[...] = _swiglu_math(y[:, :tn], y[:, tn:]).astype(act_ref.dtype)

        if ride:
            @pl.when((i == gi - 1) & (j == gj - 1))
            def _():
                ride.finish(ride_in, ride_out, ride_sems)

    hbm = pl.BlockSpec(memory_space=pltpu.HBM)
    res = pl.pallas_call(
        body, name=name, grid=(gj, gi),
        in_specs=[pl.BlockSpec((tm, d), lambda j, i: (i, 0)), pl.BlockSpec((1, d), lambda j, i: (0, 0)),
                  pl.BlockSpec((d, 2 * tn), lambda j, i: (0, j))] + [hbm] * nr,
        out_specs=[pl.BlockSpec((tm, d), lambda j, i: (jnp.where(j == 0, i, gi - 1), 0)),
                   pl.BlockSpec((tm, 2 * tn), lambda j, i: (i, j)),
                   pl.BlockSpec((tm, tn), lambda j, i: (i, j))] + [hbm] * nr,
        out_shape=[jax.ShapeDtypeStruct((t, d), MXU_DTYPE), jax.ShapeDtypeStruct((t, 2 * dff), MXU_DTYPE),
                   jax.ShapeDtypeStruct((t, dff), MXU_DTYPE)] + (ride.out_shape if ride else []),
        scratch_shapes=(ride.scratch if ride else []),
        compiler_params=pltpu.CompilerParams(dimension_semantics=("arbitrary", "arbitrary"),
                                             has_side_effects=bool(ride)),
    )(x, g, w_gu, *(ride.arrays if ride else ()))
    return (res[0], res[1], res[2], res[3:]) if ride else tuple(res)


def _ffn_down_dx_swiglu(dx_out, w_down, gu, name):
    t, d = dx_out.shape
    dff = w_down.shape[0]
    tn = _gu_tile(dff)
    tm = _pick(t, (512, 256, 128))

    def body(dx_ref, w_ref, gu_ref, o_ref):
        dact = _dot(dx_ref[...], w_ref[...], (((1,), (1,)), ((), ())))
        gu_v = gu_ref[...].astype(F32)
        _, vjp = jax.vjp(_swiglu_math, gu_v[:, :tn], gu_v[:, tn:])
        dg, du = vjp(dact)
        o_ref[...] = jnp.concatenate([dg, du], axis=1).astype(o_ref.dtype)

    return pl.pallas_call(
        body, name=name, grid=(dff // tn, t // tm),
        in_specs=[pl.BlockSpec((tm, d), lambda j, i: (i, 0)), pl.BlockSpec((tn, d), lambda j, i: (j, 0)),
                  pl.BlockSpec((tm, 2 * tn), lambda j, i: (i, j))],
        out_specs=pl.BlockSpec((tm, 2 * tn), lambda j, i: (i, j)),
        out_shape=jax.ShapeDtypeStruct((t, 2 * dff), MXU_DTYPE),
        compiler_params=_params("parallel", "parallel"),
    )(dx_out, w_down, gu)


def _matmul_norm_bwd(dy, w, x, g, dres, name, ride=None):
    t, k = dy.shape
    d = w.shape[0]
    tm = _pick(t, (1024, 512, 256, 128))
    tk = k if k <= 2816 else _pick(k, (1408, 1280, 1024, 768, 512))
    tr = _pick(tm, (256, 128))
    gi, nk = t // tm, k // tk
    nr = ride.na if ride else 0

    def body(*refs):
        dy_ref, w_ref, x_ref, g_ref, dres_ref = refs[:5]
        ride_in = refs[5:5 + nr]
        dx_ref, dg_ref = refs[5 + nr], refs[6 + nr]
        ride_out = refs[7 + nr:7 + 2 * nr]
        acc = refs[7 + 2 * nr]
        ride_sems = refs[8 + 2 * nr:]
        i, q = pl.program_id(0), pl.program_id(1)

        @pl.when((i == 0) & (q == 0))
        def _():
            dg_ref[...] = jnp.zeros_like(dg_ref)
            if ride:
                ride.start(ride_in, ride_out, ride_sems)

        @pl.when(q == 0)
        def _():
            acc[...] = jnp.zeros_like(acc)

        acc[...] += _dot(dy_ref[...], w_ref[...], (((1,), (1,)), ((), ())))

        @pl.when(q == nk - 1)
        def _():
            for r in range(tm // tr):
                rows = slice(r * tr, (r + 1) * tr)
                _, vjp = jax.vjp(_rms_norm, x_ref[rows], g_ref[...])
                dxn, dg = vjp(acc[rows])
                dx_ref[rows] = dres_ref[rows] + dxn
                dg_ref[...] += dg

        if ride:
            @pl.when((i == gi - 1) & (q == nk - 1))
            def _():
                ride.finish(ride_in, ride_out, ride_sems)

    hbm = pl.BlockSpec(memory_space=pltpu.HBM)
    row = pl.BlockSpec((tm, d), lambda i, q: (i, 0))
    res = pl.pallas_call(
        body, name=name, grid=(gi, nk),
        in_specs=[pl.BlockSpec((tm, tk), lambda i, q: (i, q)), pl.BlockSpec((d, tk), lambda i, q: (0, q)), row,
                  pl.BlockSpec((1, d), lambda i, q: (0, 0)), row] + [hbm] * nr,
        out_specs=[row, pl.BlockSpec((1, d), lambda i, q: (0, 0))] + [hbm] * nr,
        out_shape=[jax.ShapeDtypeStruct((t, d), F32), jax.ShapeDtypeStruct((1, d), F32)]
        + (ride.out_shape if ride else []),
        scratch_shapes=[pltpu.VMEM((tm, d), F32)] + (ride.scratch if ride else []),
        compiler_params=pltpu.CompilerParams(dimension_semantics=("arbitrary", "arbitrary"),
                                             has_side_effects=bool(ride)),
    )(dy, w, x, g, dres, *(ride.arrays if ride else ()))
    return (res[0], res[1], res[2:]) if ride else (res[0], res[1])


def _tok_call(name, fn, t, tt, tok_in, const_in, tok_out, acc_out):
    nblk = t // tt
    hb = tt // HALO
    in_specs, args = [], []
    for arr, w, cb, mode in tok_in:
        if mode == "cur":
            spec = pl.BlockSpec((tt, w), lambda i, cb=cb: (i, cb))
        elif mode == "prev":
            spec = pl.BlockSpec((HALO, w), lambda i, cb=cb: (jnp.maximum(i * hb - 1, 0), cb))
        else:
            spec = pl.BlockSpec((HALO, w), lambda i, cb=cb: (jnp.minimum(i + 1, nblk - 1), cb))
        in_specs.append(spec)
        args.append(arr)
    for arr in const_in:
        in_specs.append(pl.BlockSpec(arr.shape, lambda i: (0, 0)))
        args.append(arr)
    out_specs, out_shape = [], []
    for rows, w, dt in tok_out:
        out_specs.append(pl.BlockSpec((rows, w), lambda i: (i, 0)))
        out_shape.append(jax.ShapeDtypeStruct((nblk * rows, w), dt))
    for shp, dt in acc_out:
        out_specs.append(pl.BlockSpec(shp, lambda i: (0, 0)))
        out_shape.append(jax.ShapeDtypeStruct(shp, dt))
    n_tok, n_const, n_out = len(tok_in), len(const_in), len(tok_out)

    def body(*refs):
        i = pl.program_id(0)
        tok_vals = []
        for (_, _, _, mode), r in zip(tok_in, refs[:n_tok]):
            v = r[...]
            if mode == "prev":
                v = jnp.where(i > 0, v, jnp.zeros_like(v))
            elif mode == "next8":
                v = jnp.where(i < nblk - 1, v, jnp.zeros_like(v))
            tok_vals.append(v)
        const_vals = [r[...] for r in refs[n_tok:n_tok + n_const]]
        outs, accs = fn(tok_vals, const_vals)
        o_refs = refs[n_tok + n_const:n_tok + n_const + n_out]
        a_refs = refs[n_tok + n_const + n_out:]
        for r, v in zip(o_refs, outs):
            r[...] = v.astype(r.dtype)
        if a_refs:
            @pl.when(i == 0)
            def _():
                for r in a_refs:
                    r[...] = jnp.zeros_like(r)

            for r, v in zip(a_refs, accs):
                r[...] += v.astype(r.dtype)

    res = pl.pallas_call(
        body, name=name, grid=(nblk,), in_specs=in_specs, out_specs=out_specs, out_shape=out_shape,
        compiler_params=_params("arbitrary" if acc_out else "parallel"),
    )(*args)
    return res


def _dn_pre_math(cur, halo, ba, cw, alog, dtb):
    tt = cur.shape[0]
    a = jax.nn.silu(_causal_conv(cur, halo, cw, DN_CONV))
    pieces = []
    for p in range(2 * DN_HEADS):
        xh = a[:, p * HEAD_DIM:(p + 1) * HEAD_DIM]
        xh = xh * lax.rsqrt(jnp.sum(xh * xh, axis=-1, keepdims=True) + EPS)
        if p < DN_HEADS:
            xh = xh * (HEAD_DIM ** -0.5)
        pieces.append(xh)
    pieces.append(a[:, 2 * DN_WIDTH:])
    qkvn = jnp.concatenate(pieces, axis=1)
    lane = lax.broadcasted_iota(jnp.int32, ba.shape, 1)
    raw = jnp.where(lane < DN_HEADS, jax.nn.sigmoid(ba), -jnp.exp(alog) * jax.nn.softplus(ba + dtb))
    r = lax.broadcasted_iota(jnp.int32, (tt, tt), 0)
    c = lax.broadcasted_iota(jnp.int32, (tt, tt), 1)
    tri = jnp.where((r // CHUNK == c // CHUNK) & (c <= r), 1.0, 0.0).astype(F32)
    cums = _ein("ij,jk->ik", LHS_EXACT, tri, raw)
    bg = jnp.where(lane < DN_HEADS, raw, cums)
    return qkvn, bg


def _mix_math(o, z, gb, gc, gc_halo, hv, hv_halo, dng, scg, scw):
    outs = []
    for h in range(DN_HEADS):
        sl = slice(h * HEAD_DIM, (h + 1) * HEAD_DIM)
        oh = o[:, sl]
        outs.append(oh * lax.rsqrt(jnp.mean(oh * oh, axis=-1, keepdims=True) + EPS) * dng * jax.nn.silu(z[:, sl]))
    y = gb * _causal_conv(gc * hv, gc_halo * hv_halo, scw, SC_CONV)
    gw = SC_WIDTH // SC_GROUPS
    for g in range(SC_GROUPS):
        sl = slice(g * gw, (g + 1) * gw)
        yg = y[:, sl]
        outs.append(yg * lax.rsqrt(jnp.mean(yg * yg, axis=-1, keepdims=True) + EPS) * scg[:, sl])
    return jnp.concatenate(outs, axis=1)


def _tri_inverse(a):
    c = a.shape[-1]
    r = lax.broadcasted_iota(jnp.int32, (c, c), 0)
    q = lax.broadcasted_iota(jnp.int32, (c, c), 1)
    eye = jnp.where(r == q, 1.0, 0.0).astype(F32)[None]
    blk = (r // 16 == q // 16)[None]
    d = jnp.where(blk, a, 0.0)
    o = a - d
    mm = functools.partial(_ein, "bij,bjk->bik", PRECISE)
    p = eye - d
    n = mm(d, d)
    for _ in range(2):
        both = mm(jnp.concatenate([n, p], axis=1), n)
        n = both[:, :c]
        p = p + both[:, c:]
    p = p + mm(p, n)
    e = mm(p, o)
    e2 = mm(e, e)
    left = eye - e + e2 - mm(e, e2)
    return mm(left, p)


@jax.custom_vjp
def _inverse_known(a, tinv):
    return tinv


def _inverse_known_fwd(a, tinv):
    return tinv, tinv


def _inverse_known_bwd(tinv, ct):
    left = _ein("bji,bjk->bik", PRECISE, tinv, ct)
    return -_ein("bik,bjk->bij", PRECISE, left, tinv), jnp.zeros_like(tinv)


_inverse_known.defvjp(_inverse_known_fwd, _inverse_known_bwd)


def _delta_intra_math(q, k, v, bg, head, tinv_known=None):
    n = q.shape[0]
    nb = n // CHUNK
    lane = lax.broadcasted_iota(jnp.int32, bg.shape, 1)
    beta = jnp.sum(jnp.where(lane == head, bg, 0.0), axis=1, keepdims=True).reshape(nb, CHUNK, 1)
    gc = jnp.sum(jnp.where(lane == head + DN_HEADS, bg, 0.0), axis=1, keepdims=True).reshape(nb, CHUNK, 1)
    q3, k3, v3 = (a.reshape(nb, CHUNK, HEAD_DIM) for a in (q, k, v))
    r = lax.broadcasted_iota(jnp.int32, (CHUNK, CHUNK), 0)
    c = lax.broadcasted_iota(jnp.int32, (CHUNK, CHUNK), 1)
    eye = jnp.where(r == c, 1.0, 0.0).astype(F32)[None]
    gcr = _ein("bik,bkj->bij", LHS_EXACT, jnp.ones((nb, CHUNK, CHUNK), F32), gc * eye)
    decay = jnp.exp(jnp.where((r >= c)[None], gc - gcr, -1e30))
    kb = k3 * beta
    vb = v3 * beta
    egc = jnp.exp(gc)
    on_k = _ein("bcd,bmd->bcm", FAST, jnp.concatenate([kb, q3], axis=1), k3)
    a = jnp.where((r > c)[None], on_k[:, :CHUNK] * decay, 0.0)
    tinv = _tri_inverse(a) if tinv_known is None else _inverse_known(a, tinv_known)
    uw = _ein("bcm,bmd->bcd", PRECISE, tinv, jnp.concatenate([vb, kb * egc], axis=2))
    u, w = uw[:, :, :HEAD_DIM], uw[:, :, HEAD_DIM:]
    qk = on_k[:, CHUNK:] * decay
    row = lax.broadcasted_iota(jnp.int32, (nb, CHUNK, 1), 1)
    glast = jnp.sum(jnp.where(row == CHUNK - 1, gc, 0.0), axis=1, keepdims=True)
    qd = q3 * egc
    kd = k3 * jnp.exp(glast - gc)
    glb = jnp.broadcast_to(jnp.exp(glast), (nb, HALO, LANES))
    flat = lambda x: x.reshape(n, HEAD_DIM)
    return flat(u), flat(w), flat(qd), flat(kd), qk, glb, tinv


def _delta_step_math(u, w, qd, kd, qk, gl, s):
    c = u.shape[0]
    on_s = _ein("ck,kv->cv", FAST, jnp.concatenate([w, qd], axis=0), s)
    vnew = u - on_s[:c]
    on_v = _ein("cm,mv->cv", FAST, jnp.concatenate([qk, kd.T], axis=0), vnew)
    o = on_s[c:] + on_v[:c]
    s2 = s * gl + on_v[c:]
    return o, s2


def _delta_intra(qkvn, bg, nb):
    t = qkvn.shape[0]
    n = t // CHUNK
    rows = nb * CHUNK

    def body(q_ref, k_ref, v_ref, bg_ref, u_ref, w_ref, qd_ref, kd_ref, qk_ref, gl_ref, ti_ref):
        outs = _delta_intra_math(q_ref[...], k_ref[...], v_ref[...], bg_ref[...], pl.program_id(1))
        for r, v in zip((u_ref, w_ref, qd_ref, kd_ref, qk_ref, gl_ref, ti_ref), outs):
            r[...] = v

    col = lambda off: pl.BlockSpec((rows, HEAD_DIM), lambda b, h, off=off: (b, off + h))
    tok = jax.ShapeDtypeStruct((t, DN_WIDTH), F32)
    return pl.pallas_call(
        body, name="delta_intra", grid=(n // nb, DN_HEADS),
        in_specs=[col(0), col(DN_HEADS), col(2 * DN_HEADS), pl.BlockSpec((rows, LANES), lambda b, h: (b, 0))],
        out_specs=[col(0)] * 4 + [pl.BlockSpec((nb, None, CHUNK, CHUNK), lambda b, h: (b, h, 0, 0)),
                                  pl.BlockSpec((nb, None, HALO, LANES), lambda b, h: (b, h, 0, 0)),
                                  pl.BlockSpec((nb, None, CHUNK, CHUNK), lambda b, h: (b, h, 0, 0))],
        out_shape=[tok] * 4 + [jax.ShapeDtypeStruct((n, DN_HEADS, CHUNK, CHUNK), F32),
                               jax.ShapeDtypeStruct((n, DN_HEADS, HALO, LANES), F32),
                               jax.ShapeDtypeStruct((n, DN_HEADS, CHUNK, CHUNK), F32)],
        compiler_params=_params("parallel", "arbitrary"),
    )(qkvn, qkvn, qkvn, bg)


def _delta_intra_bwd(qkvn, bg, tinv, cts, nb):
    t = qkvn.shape[0]
    n = t // CHUNK
    rows = nb * CHUNK

    def body(q_ref, k_ref, v_ref, bg_ref, ti_ref, du, dw, dqd, dkd, dqk, dgl, dq_ref, dk_ref, dv_ref, dbg_ref):
        h = pl.program_id(1)
        ti = ti_ref[...]
        _, vjp = jax.vjp(lambda q, k, v, b: _delta_intra_math(q, k, v, b, h, ti)[:6],
                         q_ref[...], k_ref[...], v_ref[...], bg_ref[...])
        dq, dk, dv, dbg = vjp((du[...], dw[...], dqd[...], dkd[...], dqk[...], dgl[...]))
        dq_ref[...] = dq
        dk_ref[...] = dk
        dv_ref[...] = dv

        @pl.when(h == 0)
        def _():
            dbg_ref[...] = jnp.zeros_like(dbg_ref)

        dbg_ref[...] += dbg

    col = lambda off: pl.BlockSpec((rows, HEAD_DIM), lambda b, h, off=off: (b, off + h))
    bgs = pl.BlockSpec((rows, LANES), lambda b, h: (b, 0))
    qks = pl.BlockSpec((nb, None, CHUNK, CHUNK), lambda b, h: (b, h, 0, 0))
    gls = pl.BlockSpec((nb, None, HALO, LANES), lambda b, h: (b, h, 0, 0))
    tok = jax.ShapeDtypeStruct((t, DN_WIDTH), F32)
    return pl.pallas_call(
        body, name="delta_intra_bwd", grid=(n // nb, DN_HEADS),
        in_specs=[col(0), col(DN_HEADS), col(2 * DN_HEADS), bgs, qks, col(0), col(0), col(0), col(0), qks, gls],
        out_specs=[col(0), col(0), col(0), bgs],
        out_shape=[tok, tok, tok, jax.ShapeDtypeStruct((t, LANES), F32)],
        compiler_params=_params("parallel", "arbitrary"),
    )(qkvn, qkvn, qkvn, bg, tinv, *cts)


def _delta_scan(u, w, qd, kd, qk, glb, cb):
    t = u.shape[0]
    n = t // CHUNK
    rows = cb * CHUNK

    def body(u_ref, w_ref, qd_ref, kd_ref, qk_ref, gl_ref, o_ref, s_ref, s_scr):
        @pl.when(pl.program_id(0) == 0)
        def _():
            s_scr[...] = jnp.zeros_like(s_scr)

        def chunk(c, carry):
            r0 = pl.multiple_of(c * CHUNK, CHUNK)
            for h in range(DN_HEADS):
                sl = (pl.ds(r0, CHUNK), slice(h * HEAD_DIM, (h + 1) * HEAD_DIM))
                s = s_scr[h]
                s_ref[c, h] = s
                o, s2 = _delta_step_math(u_ref[sl], w_ref[sl], qd_ref[sl], kd_ref[sl], qk_ref[c, h],
                                         gl_ref[c, h][0:1, :], s)
                o_ref[sl] = o
                s_scr[h] = s2
            return carry

        lax.fori_loop(0, cb, chunk, 0)

    tok = pl.BlockSpec((rows, DN_WIDTH), lambda i: (i, 0))
    return pl.pallas_call(
        body, name="delta_scan", grid=(n // cb,),
        in_specs=[tok] * 4 + [pl.BlockSpec((cb, DN_HEADS, CHUNK, CHUNK), lambda i: (i, 0, 0, 0)),
                              pl.BlockSpec((cb, DN_HEADS, HALO, LANES), lambda i: (i, 0, 0, 0))],
        out_specs=[tok, pl.BlockSpec((cb, DN_HEADS, HEAD_DIM, HEAD_DIM), lambda i: (i, 0, 0, 0))],
        out_shape=[jax.ShapeDtypeStruct((t, DN_WIDTH), F32),
                   jax.ShapeDtypeStruct((n, DN_HEADS, HEAD_DIM, HEAD_DIM), F32)],
        scratch_shapes=[pltpu.VMEM((DN_HEADS, HEAD_DIM, HEAD_DIM), F32)],
        compiler_params=_params("arbitrary"),
    )(u, w, qd, kd, qk, glb)


def _delta_step_batched(u, w, qd, kd, qk, gl, s):
    vnew = u - _ein("bck,bkv->bcv", FAST, w, s)
    o = _ein("bck,bkv->bcv", FAST, qd, s) + _ein("bcm,bmv->bcv", FAST, qk, vnew)
    s2 = s * gl + _ein("bck,bcv->bkv", FAST, kd, vnew)
    return o, s2


def _delta_scan_bwd_state(w, qd, kd, qk, glb, do, cb):
    t = w.shape[0]
    n = t // CHUNK
    nblk = n // cb
    rows = cb * CHUNK

    def body(w_ref, qd_ref, kd_ref, qk_ref, gl_ref, do_ref, gn_ref, ds_scr):
        @pl.when(pl.program_id(0) == 0)
        def _():
            ds_scr[...] = jnp.zeros_like(ds_scr)

        def chunk(step, carry):
            c = cb - 1 - step
            r0 = pl.multiple_of(c * CHUNK, CHUNK)
            for h in range(DN_HEADS):
                sl = (pl.ds(r0, CHUNK), slice(h * HEAD_DIM, (h + 1) * HEAD_DIM))
                ds_out = ds_scr[h]
                gn_ref[c, h] = ds_out
                args = (w_ref[sl], qd_ref[sl], kd_ref[sl], qk_ref[c, h], gl_ref[c, h][0:1, :])
                zero = jnp.zeros((CHUNK, HEAD_DIM), F32)
                _, vjp = jax.vjp(lambda s_: _delta_step_math(zero, *args, s_), jnp.zeros((HEAD_DIM, HEAD_DIM), F32))
                (ds_in,) = vjp((do_ref[sl], ds_out))
                ds_scr[h] = ds_in
            return carry

        lax.fori_loop(0, cb, chunk, 0)

    rev = lambda i: nblk - 1 - i
    tok = pl.BlockSpec((rows, DN_WIDTH), lambda i: (rev(i), 0))
    state = pl.BlockSpec((cb, DN_HEADS, HEAD_DIM, HEAD_DIM), lambda i: (rev(i), 0, 0, 0))
    return pl.pallas_call(
        body, name="delta_scan_bwd_state", grid=(nblk,),
        in_specs=[tok] * 3 + [pl.BlockSpec((cb, DN_HEADS, CHUNK, CHUNK), lambda i: (rev(i), 0, 0, 0)),
                              pl.BlockSpec((cb, DN_HEADS, HALO, LANES), lambda i: (rev(i), 0, 0, 0)), tok],
        out_specs=state,
        out_shape=jax.ShapeDtypeStruct((n, DN_HEADS, HEAD_DIM, HEAD_DIM), F32),
        scratch_shapes=[pltpu.VMEM((DN_HEADS, HEAD_DIM, HEAD_DIM), F32)],
        compiler_params=_params("arbitrary"),
    )(w, qd, kd, qk, glb, do)


def _delta_scan_bwd_inputs(u, w, qd, kd, qk, glb, s_all, do, gn, nb):
    t = u.shape[0]
    n = t // CHUNK
    rows = nb * CHUNK

    def body(u_ref, w_ref, qd_ref, kd_ref, qk_ref, gl_ref, s_ref, do_ref, gn_ref,
             du_ref, dw_ref, dqd_ref, dkd_ref, dqk_ref, dgl_ref):
        r3 = lambda ref: ref[...].reshape(nb, CHUNK, HEAD_DIM)
        s_in = s_ref[...]
        _, vjp = jax.vjp(lambda a, b, c, d, e, g: _delta_step_batched(a, b, c, d, e, g[:, 0:1, :], s_in),
                         r3(u_ref), r3(w_ref), r3(qd_ref), r3(kd_ref), qk_ref[...], gl_ref[...])
        du, dw, dqd, dkd, dqk, dgl = vjp((r3(do_ref), gn_ref[...]))
        for ref, v in zip((du_ref, dw_ref, dqd_ref, dkd_ref), (du, dw, dqd, dkd)):
            ref[...] = v.reshape(rows, HEAD_DIM)
        dqk_ref[...] = dqk
        dgl_ref[...] = dgl

    col = pl.BlockSpec((rows, HEAD_DIM), lambda b, h: (b, h))
    qks = pl.BlockSpec((nb, None, CHUNK, CHUNK), lambda b, h: (b, h, 0, 0))
    gls = pl.BlockSpec((nb, None, HALO, LANES), lambda b, h: (b, h, 0, 0))
    state = pl.BlockSpec((nb, None, HEAD_DIM, HEAD_DIM), lambda b, h: (b, h, 0, 0))
    tok = jax.ShapeDtypeStruct((t, DN_WIDTH), F32)
    return pl.pallas_call(
        body, name="delta_scan_bwd_inputs", grid=(n // nb, DN_HEADS),
        in_specs=[col] * 4 + [qks, gls, state, col, state],
        out_specs=[col] * 4 + [qks, gls],
        out_shape=[tok] * 4 + [jax.ShapeDtypeStruct(qk.shape, F32), jax.ShapeDtypeStruct(glb.shape, F32)],
        compiler_params=_params("parallel", "parallel"),
    )(u, w, qd, kd, qk, glb, s_all, do, gn)


def _peer(mask):
    x, y, c = lax.axis_index("x"), lax.axis_index("y"), lax.axis_index("c")
    return (x ^ ((mask >> 2) & 1), y ^ ((mask >> 1) & 1), c ^ (mask & 1))


def _my_index():
    return 4 * lax.axis_index("x") + 2 * lax.axis_index("y") + lax.axis_index("c")


class _Exchange:
    CHIP_MASKS = (4, 2, 6)

    def __init__(self, kind, arrays):
        self.kind = kind
        self.arrays = list(arrays)
        self.na = na = len(self.arrays)
        if kind == "gather":
            self.out_shape = [jax.ShapeDtypeStruct((N_DEV,) + a.shape, a.dtype) for a in self.arrays]
        else:
            self.out_shape = [jax.ShapeDtypeStruct(a.shape, a.dtype) for a in self.arrays]
        self.scratch = [pltpu.SemaphoreType.DMA((na, 7)), pltpu.SemaphoreType.DMA((na, 7)),
                        pltpu.SemaphoreType.DMA((na,))]

    def _copies(self, ins, outs, sems):
        send_sems, recv_sems, local_sems = sems
        me = _my_index()
        local, first, passed, arrivals = [], [], [], []
        if self.kind == "gather":
            def rc(a, k, block, to, own=False):
                def make():
                    dst = outs[a].at[block]
                    return pltpu.make_async_remote_copy(src_ref=ins[a] if own else dst, dst_ref=dst,
                                                        send_sem=send_sems.at[a, k], recv_sem=recv_sems.at[a, k],
                                                        device_id=to, device_id_type=MESH)
                return make

            sib = _peer(1)
            for a in range(self.na):
                local.append(lambda a=a: pltpu.make_async_copy(ins[a], outs[a].at[me], local_sems.at[a]))
                first.append(rc(a, 0, me, sib, own=True))
                arrivals.append(rc(a, 0, me ^ 1, _peer(0)))
                for j, m in enumerate(self.CHIP_MASKS):
                    first.append(rc(a, 1 + j, me, _peer(m), own=True))
                    passed.append((rc(a, 1 + j, me ^ m, _peer(0)), rc(a, 4 + j, me ^ m, sib)))
                    arrivals.append(rc(a, 4 + j, me ^ m ^ 1, _peer(0)))
        else:
            for a in range(self.na):
                local.append(lambda a=a: pltpu.make_async_copy(ins[a].at[me], outs[a].at[me], local_sems.at[a]))
                for m in range(1, N_DEV):
                    def make(a=a, m=m):
                        return pltpu.make_async_remote_copy(
                            src_ref=ins[a].at[me ^ m], dst_ref=outs[a].at[me], send_sem=send_sems.at[a, m - 1],
                            recv_sem=recv_sems.at[a, m - 1], device_id=_peer(m), device_id_type=MESH)
                    first.append(make)
                    arrivals.append(make)
        return local, first, passed, arrivals

    def start(self, ins, outs, sems):
        local, first, _, _ = self._copies(ins, outs, sems)
        for make in local + first:
            make().start()

    def finish(self, ins, outs, sems):
        local, first, passed, arrivals = self._copies(ins, outs, sems)
        for landed, onward in passed:
            landed().wait_recv()
            onward().start()
        for make in arrivals:
            make().wait_recv()
        for make in first + [p for _, p in passed]:
            make().wait_send()
        for make in local:
            make().wait()

    def run(self, name):
        na = self.na

        def body(*refs):
            ins, outs, sems = refs[:na], refs[na:2 * na], refs[2 * na:]
            self.start(ins, outs, sems)
            self.finish(ins, outs, sems)

        hbm = pl.BlockSpec(memory_space=pltpu.HBM)
        return pl.pallas_call(
            body, name=name, in_specs=[hbm] * na, out_specs=[hbm] * na, out_shape=self.out_shape,
            scratch_shapes=self.scratch, compiler_params=pltpu.CompilerParams(has_side_effects=True),
        )(*self.arrays)


def _all_gather(shards, name):
    return _Exchange("gather", shards).run(name)


def _adamw_math(w, g, m, v):
    m2 = ADAM_B1 * m + (1.0 - ADAM_B1) * g
    v2 = ADAM_B2 * v + (1.0 - ADAM_B2) * jnp.square(g)
    m_hat = m2 / (1.0 - ADAM_B1 ** ADAM_STEP)
    v_hat = v2 / (1.0 - ADAM_B2 ** ADAM_STEP)
    delta = -ADAM_LR * (m_hat / (jnp.sqrt(v_hat) + ADAM_EPS) + ADAM_WD * w)
    return delta, m2, v2


def _sum_adamw(parts, w, m, v, name):
    r, c = w.shape
    tr = _pick(r, (512, 256, 352, 128, 64, 32, 16, 8))
    np_ = parts.shape[0]

    def body(p_ref, w_ref, m_ref, v_ref, g_ref, d_ref, m2_ref, v2_ref):
        g = p_ref[0].astype(F32)
        for d in range(1, np_):
            g = g + p_ref[d].astype(F32)
        delta, m2, v2 = _adamw_math(w_ref[...], g, m_ref[...], v_ref[...])
        g_ref[...] = g
        d_ref[...] = delta
        m2_ref[...] = m2
        v2_ref[...] = v2

    blk = pl.BlockSpec((tr, c), lambda i: (i, 0))
    shp = jax.ShapeDtypeStruct((r, c), F32)
    return pl.pallas_call(
        body, name=name, grid=(r // tr,),
        in_specs=[pl.BlockSpec((np_, tr, c), lambda i: (0, i, 0)), blk, blk, blk],
        out_specs=[blk] * 4, out_shape=[shp] * 4, compiler_params=_params("parallel"),
    )(parts, w, m, v)


def _sum_rows(parts, name):
    np_, r, c = parts.shape

    def body(p_ref, o_ref):
        g = p_ref[0]
        for d in range(1, np_):
            g = g + p_ref[d]
        o_ref[...] = g

    return pl.pallas_call(body, name=name, out_shape=jax.ShapeDtypeStruct((r, c), F32))(parts)


def _pad_w_in(w):
    d = w.shape[0]
    n_ba = 2 * DN_HEADS
    a = w[:, :SC_OFF]
    ba = w[:, SC_OFF:SC_OFF + n_ba]
    sc = w[:, SC_OFF + n_ba:]
    return jnp.concatenate([a, sc, ba, jnp.zeros((d, BA_W - n_ba), w.dtype)], axis=1)


def _unpad_w_in(wp):
    n_ba = 2 * DN_HEADS
    return jnp.concatenate([wp[:, :SC_OFF], wp[:, BA_OFF:BA_OFF + n_ba], wp[:, SC_OFF:BA_OFF]], axis=1)


def _lane_row(v, off):
    return jnp.pad(v.astype(F32), (off, LANES - off - v.shape[0]))[None]


TT = 256
NB_INTRA = 8
CB_SCAN = 8


def _layer_fwd(x, p, ride_proj, ride_ffn, late):
    t, d = x.shape
    got_proj = got_ffn = None
    if ride_proj:
        h, proj, got_proj = _norm_matmul(x, p["norm1_g"], p["w_in"], "proj_fwd_gather", ride=ride_proj)
    else:
        h, proj = _norm_matmul(x, p["norm1_g"], p["w_in"], "proj_fwd")
    p = {**p, **late(got_proj)}
    qkvn, bg = _tok_call(
        "dn_pre", lambda tv, cv: (_dn_pre_math(*tv, *cv), ()), t, TT,
        [(proj, QKV_W, 0, "cur"), (proj, QKV_W, 0, "prev"), (proj, LANES, BA_OFF // LANES, "cur")],
        [p["dn_conv_w"], p["alog_row"], p["dt_row"]],
        [(TT, QKV_W, F32), (TT, LANES, F32)], [])
    u, w, qd, kd, qk, glb, tinv = _delta_intra(qkvn, bg, NB_INTRA)
    o, s_all = _delta_scan(u, w, qd, kd, qk, glb, CB_SCAN)
    cb0 = SC_OFF // SC_WIDTH
    mix_in = [(o, DN_WIDTH, 0, "cur"), (proj, DN_WIDTH, Z_OFF // DN_WIDTH, "cur"),
              (proj, SC_WIDTH, cb0, "cur"), (proj, SC_WIDTH, cb0 + 1, "cur"), (proj, SC_WIDTH, cb0 + 1, "prev"),
              (proj, SC_WIDTH, cb0 + 2, "cur"), (proj, SC_WIDTH, cb0 + 2, "prev")]
    mix_const = [p["dn_norm_g"], p["sc_norm_g"], p["sc_conv_w"]]
    (cat,) = _tok_call("mix_post", lambda tv, cv: ((_mix_math(*tv, *cv),), ()), t, TT,
                       mix_in, mix_const, [(TT, 2 * DN_WIDTH, MXU_DTYPE)], [])
    x_mid = _matmul(cat, p["w_out"], "nn", F32, "out_proj", residual=x)
    if ride_ffn:
        h2, gu, act, got_ffn = _ffn_up_swiglu(x_mid, p["norm2_g"], p["w_gu"], "ffn_up_gather", ride=ride_ffn)
    else:
        h2, gu, act = _ffn_up_swiglu(x_mid, p["norm2_g"], p["w_gu"], "ffn_up")
    x_out = _matmul(act, p["w_down"], "nn", F32, "ffn_down", residual=x_mid)
    saved = dict(x=x, h=h, proj=proj, qkvn=qkvn, bg=bg, u=u, w=w, qd=qd, kd=kd, qk=qk, glb=glb, tinv=tinv, s_all=s_all, o=o,
                 cat=cat, x_mid=x_mid, h2=h2, gu=gu, act=act, mix_in=mix_in, mix_const=mix_const)
    return x_out, saved, p, got_proj, got_ffn


def _layer_bwd(dx_out, p, s, ride_prev, ride_gu, ride_down):
    t, d = dx_out.shape
    got = {}
    dgu = _ffn_down_dx_swiglu(dx_out, p["w_down"], s["gu"], "ffn_down_dx")
    d_w_down = _matmul(s["act"], dx_out, "tn", MXU_DTYPE, "ffn_down_dw")
    if ride_prev:
        dx_mid, d_norm2, got["prev"] = _matmul_norm_bwd(dgu, p["w_gu"], s["x_mid"], p["norm2_g"], dx_out,
                                                        "ffn_up_dx_scatter", ride=ride_prev)
    else:
        dx_mid, d_norm2 = _matmul_norm_bwd(dgu, p["w_gu"], s["x_mid"], p["norm2_g"], dx_out, "ffn_up_dx")
    d_w_gu = _matmul(s["h2"], dgu, "tn", MXU_DTYPE, "ffn_up_dw")
    dcat = _matmul(dx_mid, p["w_out"], "nt", F32, "out_proj_dx")
    d_w_out = _matmul(s["cat"], dx_mid, "tn", MXU_DTYPE, "out_proj_dw")

    def mix_bwd(tv, cv):
        prim = tuple(tv[:7]) + tuple(cv)
        _, vjp = jax.vjp(_mix_math, *prim)
        do, dz, dgb, dgc, dgch, dhv, dhvh, ddng, dscg, dscw = vjp(tv[7])
        return (do, dz, dgb, dgc, dgch, dhv, dhvh), (ddng, dscg, dscw)

    wide = (TT, DN_WIDTH, F32)
    halo = (HALO, SC_WIDTH, F32)
    do, dz, dgb, dgc, dgc_h, dhv, dhv_h, d_dn_norm, d_sc_norm, d_sc_conv = _tok_call(
        "mix_post_bwd", mix_bwd, t, TT, s["mix_in"] + [(dcat, 2 * DN_WIDTH, 0, "cur")], s["mix_const"],
        [wide, wide, wide, wide, halo, wide, halo],
        [((1, HEAD_DIM), F32), ((1, SC_WIDTH), F32), ((SC_CONV, SC_WIDTH), F32)])
    gn = _delta_scan_bwd_state(s["w"], s["qd"], s["kd"], s["qk"], s["glb"], do, CB_SCAN)
    cts = _delta_scan_bwd_inputs(s["u"], s["w"], s["qd"], s["kd"], s["qk"], s["glb"], s["s_all"], do, gn, NB_INTRA)
    dq, dk, dv, dbg = _delta_intra_bwd(s["qkvn"], s["bg"], s["tinv"], cts, NB_INTRA)
    proj = s["proj"]

    def dn_pre_bwd(tv, cv):
        cur, hal, ba, dq_, dk_, dv_, dbg_ = tv
        _, vjp = jax.vjp(_dn_pre_math, cur, hal, ba, *cv)
        dcur, dhal, dba, dcw, dal, ddt = vjp((jnp.concatenate([dq_, dk_, dv_], axis=1), dbg_))
        return (dcur, dhal, dba), (dcw, dal, ddt)

    dqkv, dqkv_h, dba, d_dn_conv, d_alog, d_dt = _tok_call(
        "dn_pre_bwd", dn_pre_bwd, t, TT,
        [(proj, QKV_W, 0, "cur"), (proj, QKV_W, 0, "prev"), (proj, LANES, BA_OFF // LANES, "cur"),
         (dq, DN_WIDTH, 0, "cur"), (dk, DN_WIDTH, 0, "cur"), (dv, DN_WIDTH, 0, "cur"), (dbg, LANES, 0, "cur")],
        [p["dn_conv_w"], p["alog_row"], p["dt_row"]],
        [(TT, QKV_W, F32), (HALO, QKV_W, F32), (TT, LANES, F32)],
        [((DN_CONV, QKV_W), F32), ((1, LANES), F32), ((1, LANES), F32)])

    def assemble(tv, cv):
        dqkv_, dqkv_n, dz_, dgb_, dgc_, dgc_n, dhv_, dhv_n, dba_ = tv

        def with_halo(cur, nxt):
            return cur + jnp.concatenate([jnp.zeros((TT - HALO, cur.shape[1]), F32), nxt], axis=0)

        out = jnp.concatenate([with_halo(dqkv_, dqkv_n), dz_, dgb_, with_halo(dgc_, dgc_n), with_halo(dhv_, dhv_n),
                               dba_, jnp.zeros((TT, BA_W - LANES), F32)], axis=1)
        return (out,), ()

    (dproj,) = _tok_call(
        "dproj_assemble", assemble, t, TT,
        [(dqkv, QKV_W, 0, "cur"), (dqkv_h, QKV_W, 0, "next8"), (dz, DN_WIDTH, 0, "cur"), (dgb, SC_WIDTH, 0, "cur"),
         (dgc, SC_WIDTH, 0, "cur"), (dgc_h, SC_WIDTH, 0, "next8"), (dhv, SC_WIDTH, 0, "cur"),
         (dhv_h, SC_WIDTH, 0, "next8"), (dba, LANES, 0, "cur")], [],
        [(TT, PROJ_W, MXU_DTYPE)], [])
    dx_in, d_norm1, got["gu"] = _matmul_norm_bwd(dproj, p["w_in"], s["x"], p["norm1_g"], dx_mid,
                                                 "proj_dx_scatter", ride=ride_gu(d_w_gu))
    d_w_in, got["down"] = _matmul(s["h"], dproj, "tn", MXU_DTYPE, "proj_dw_scatter", ride=ride_down(d_w_down))
    grads = dict(w_in=d_w_in, w_out=d_w_out, w_gu=d_w_gu, w_down=d_w_down, norm1_g=d_norm1, norm2_g=d_norm2,
                 dn_norm_g=d_dn_norm, sc_norm_g=d_sc_norm, sc_conv_w=d_sc_conv, dn_conv_w=d_dn_conv,
                 alog=d_alog, dt=d_dt)
    return dx_in, grads, got


def _final_loss(x, g, target):
    t, d = x.shape

    def fn(tv, cv):
        xv, tg = tv

        def loss_fn(xx, gg):
            err = jnp.square(_rms_norm(xx, gg) - tg)
            return 0.5 * jnp.sum(jnp.mean(err, axis=-1))

        loss, vjp = jax.vjp(loss_fn, xv, cv[0])
        dx, dg = vjp(jnp.ones((), F32))
        return (dx,), (jnp.full((1, LANES), loss, F32), dg)

    return _tok_call("final_loss", fn, t, TT, [(x, d, 0, "cur"), (target, d, 0, "cur")], [g],
                     [(TT, d, F32)], [((1, LANES), F32), ((1, d), F32)])


def _pack_rows(arrs):
    rows, offs, r0 = [], [], 0
    for a in arrs:
        n = a.size
        nr = -(-n // LANES)
        flat = jnp.pad(a.reshape(-1).astype(F32), (0, nr * LANES - n))
        rows.append(flat.reshape(nr, LANES))
        offs.append((r0, nr, a.shape))
        r0 += nr
    pad = (-r0) % 8
    if pad:
        rows.append(jnp.zeros((pad, LANES), F32))
    return jnp.concatenate(rows, axis=0), offs


def _unpack_rows(packed, offs):
    out = []
    for r0, nr, shp in offs:
        n = 1
        for s_ in shp:
            n *= s_
        out.append(packed[r0:r0 + nr].reshape(-1)[:n].reshape(shp))
    return out


def kernel(x, norm1_g, w_in, dn_conv_w, dn_a_log, dn_dt_bias, dn_norm_g, sc_conv_w, sc_norm_g, w_out, norm2_g, ffn_w_gate, ffn_w_up, ffn_w_down, final_norm_g, loss_target, m_norm1_g, m_w_in, m_dn_conv_w, m_dn_a_log, m_dn_dt_bias, m_dn_norm_g, m_sc_conv_w, m_sc_norm_g, m_w_out, m_norm2_g, m_ffn_w_gate, m_ffn_w_up, m_ffn_w_down, m_final_norm_g, v_norm1_g, v_w_in, v_dn_conv_w, v_dn_a_log, v_dn_dt_bias, v_dn_norm_g, v_sc_conv_w, v_sc_norm_g, v_w_out, v_norm2_g, v_ffn_w_gate, v_ffn_w_up, v_ffn_w_down, v_final_norm_g):
    depth, d, cin = w_in.shape
    t = x.shape[1]
    dff_s = ffn_w_gate.shape[2]
    me = _my_index()
    x2 = x.reshape(t, d)
    tgt = loss_target.reshape(t, d)

    conv_pack, conv_offs = _pack_rows([dn_conv_w, sc_conv_w])
    (conv_all,) = _all_gather([conv_pack], "gather_conv")
    dn_parts, sc_parts = zip(*[_unpack_rows(conv_all[j], conv_offs) for j in range(N_DEV)])
    dn_conv_full = jnp.concatenate(dn_parts, axis=2)
    sc_conv_full = jnp.concatenate(sc_parts, axis=2)

    def shards(l):
        return [a[l].astype(MXU_DTYPE) for a in (w_in, w_out, ffn_w_gate, ffn_w_up, ffn_w_down)]

    def mixer_params(l, g_in, g_out):
        full_in = g_in.transpose(1, 0, 2).reshape(d, N_DEV * cin)
        return dict(
            w_in=_pad_w_in(full_in), w_out=g_out.reshape(d, d),
            norm1_g=norm1_g[l][None], norm2_g=norm2_g[l][None], dn_norm_g=dn_norm_g[l][None],
            sc_norm_g=sc_norm_g[l][None], dn_conv_w=dn_conv_full[l], sc_conv_w=sc_conv_full[l],
            alog_row=_lane_row(dn_a_log[l], DN_HEADS), dt_row=_lane_row(dn_dt_bias[l], DN_HEADS))

    def ffn_params(g_gate, g_up, g_down):
        full_gate = g_gate.transpose(1, 0, 2).reshape(d, N_DEV * dff_s)
        full_up = g_up.transpose(1, 0, 2).reshape(d, N_DEV * dff_s)
        return dict(w_gu=_interleave_gu(full_gate, full_up), w_down=g_down.reshape(N_DEV * dff_s, d))

    nxt = mixer_params(0, *_all_gather(shards(0)[:2], "gather_first"))
    nxt_ffn = None
    params, saved = [], []
    xc = x2
    for l in range(depth):
        last = l + 1 == depth
        own_ffn = shards(l)[2:] if nxt_ffn is None else []
        ahead = [] if last else shards(l + 1)
        ride_proj = _Exchange("gather", own_ffn + ahead[:2]) if own_ffn or ahead else None
        ride_ffn = None if last else _Exchange("gather", ahead[2:])

        def late(got, own_ffn=own_ffn, nxt_ffn=nxt_ffn):
            return ffn_params(*got[:3]) if own_ffn else nxt_ffn

        xc, s, p_l, got_proj, got_ffn = _layer_fwd(xc, nxt, ride_proj, ride_ffn, late)
        params.append(p_l)
        saved.append(s)
        if not last:
            nxt = mixer_params(l + 1, *got_proj[len(own_ffn):])
            nxt_ffn = ffn_params(*got_ffn)
    dx, loss_part, d_final = _final_loss(xc, final_norm_g[None], tgt)

    names = ("w_in", "w_out", "ffn_w_gate", "ffn_w_up", "ffn_w_down")
    big_out = {n: {k: [None] * depth for k in ("g", "d", "m", "v")} for n in names}
    w_loc = dict(w_in=w_in, w_out=w_out, ffn_w_gate=ffn_w_gate, ffn_w_up=ffn_w_up, ffn_w_down=ffn_w_down)
    m_loc = dict(w_in=m_w_in, w_out=m_w_out, ffn_w_gate=m_ffn_w_gate, ffn_w_up=m_ffn_w_up, ffn_w_down=m_ffn_w_down)
    v_loc = dict(w_in=v_w_in, w_out=v_w_out, ffn_w_gate=v_ffn_w_gate, ffn_w_up=v_ffn_w_up, ffn_w_down=v_ffn_w_down)

    cols = lambda a, c: a.reshape(a.shape[0], N_DEV, c).transpose(1, 0, 2).astype(MXU_DTYPE)

    def ride_gu(d_w_gu):
        g_gate, g_up = _split_gu(d_w_gu)
        return _Exchange("scatter", [cols(g_gate, dff_s), cols(g_up, dff_s)])

    def ride_down(d_w_down):
        return _Exchange("scatter", [d_w_down.reshape(N_DEV, dff_s, d).astype(MXU_DTYPE)])

    def apply(l, which, recv):
        for n, r in zip(which, recv):
            res = _sum_adamw(r, w_loc[n][l], m_loc[n][l], v_loc[n][l], "adamw_" + n)
            for k, a in zip(("g", "d", "m", "v"), res):
                big_out[n][k][l] = a

    grads = [None] * depth
    pending = None
    for l in reversed(range(depth)):
        ride_prev = _Exchange("scatter", pending) if pending else None
        dx, grads[l], got = _layer_bwd(dx, params[l], saved[l], ride_prev, ride_gu, ride_down)
        if ride_prev:
            apply(l + 1, names[:2], got["prev"])
        apply(l, names[2:], [*got["gu"], *got["down"]])
        pending = [cols(_unpad_w_in(grads[l]["w_in"]), cin),
                   grads[l]["w_out"].reshape(N_DEV, d // N_DEV, d).astype(MXU_DTYPE)]
    apply(0, names[:2], _Exchange("scatter", pending).run("scatter_last"))
    grad_x = dx.reshape(x.shape)
    big_out = {n: {k: jnp.stack(v_) for k, v_ in o.items()} for n, o in big_out.items()}

    stack = lambda key: jnp.stack([grads[l][key] for l in range(depth)])
    small_parts = [stack("norm1_g").reshape(depth, d), stack("norm2_g").reshape(depth, d), d_final.reshape(d),
                   stack("dn_norm_g").reshape(depth, HEAD_DIM), stack("sc_norm_g").reshape(depth, SC_WIDTH),
                   stack("alog").reshape(depth, LANES), stack("dt").reshape(depth, LANES),
                   stack("dn_conv_w"), stack("sc_conv_w"), loss_part]
    small_pack, small_offs = _pack_rows(small_parts)
    (small_all,) = _all_gather([small_pack], "gather_small")
    total = _sum_rows(small_all, "sum_small")
    (g_n1, g_n2, g_fin, g_dnn, g_scn, g_alog, g_dt, g_dnc, g_scc, loss_row) = _unpack_rows(total, small_offs)
    loss = loss_row[0, 0]
    g_alog = g_alog[:, DN_HEADS:2 * DN_HEADS]
    g_dt = g_dt[:, DN_HEADS:2 * DN_HEADS]
    dnc_w = dn_conv_w.shape[2]
    scc_w = sc_conv_w.shape[2]
    g_dnc = lax.dynamic_slice_in_dim(g_dnc, me * dnc_w, dnc_w, axis=2)
    g_scc = lax.dynamic_slice_in_dim(g_scc, me * scc_w, scc_w, axis=2)
    sm_g = [g_n1, g_dnc, g_alog, g_dt, g_dnn, g_scc, g_scn, g_n2, g_fin]
    sm_w = [norm1_g, dn_conv_w, dn_a_log, dn_dt_bias, dn_norm_g, sc_conv_w, sc_norm_g, norm2_g, final_norm_g]
    sm_m = [m_norm1_g, m_dn_conv_w, m_dn_a_log, m_dn_dt_bias, m_dn_norm_g, m_sc_conv_w, m_sc_norm_g, m_norm2_g, m_final_norm_g]
    sm_v = [v_norm1_g, v_dn_conv_w, v_dn_a_log, v_dn_dt_bias, v_dn_norm_g, v_sc_conv_w, v_sc_norm_g, v_norm2_g, v_final_norm_g]
    pg, offs = _pack_rows(sm_g)
    pw, _ = _pack_rows(sm_w)
    pm, _ = _pack_rows(sm_m)
    pv, _ = _pack_rows(sm_v)
    sg, sd, sm_, sv = _sum_adamw(pg[None], pw, pm, pv, "adamw_small")
    small_out = {k: _unpack_rows(a, offs) for k, a in zip(("g", "d", "m", "v"), (sg, sd, sm_, sv))}

    def outputs(k):
        s_ = small_out[k]
        b = big_out
        return [s_[0], b["w_in"][k], s_[1], s_[2], s_[3], s_[4], s_[5], s_[6], b["w_out"][k], s_[7],
                b["ffn_w_gate"][k], b["ffn_w_up"][k], b["ffn_w_down"][k], s_[8]]

    return (loss, grad_x, *outputs("g"), *outputs("d"), *outputs("m"), *outputs("v"))
```

```python
import functools

import jax
import jax.numpy as jnp
from jax import lax
from jax.experimental import pallas as pl
from jax.experimental.pallas import tpu as pltpu

F32 = jnp.float32
MXU_DTYPE = jnp.bfloat16
MESH = pl.DeviceIdType.MESH

N_DEV = 8
EPS = 1e-6
DN_HEADS = 4
HEAD_DIM = 128
DN_WIDTH = DN_HEADS * HEAD_DIM
SC_WIDTH = 512
SC_GROUPS = 4
DN_CONV = 4
SC_CONV = 3
CHUNK = 64
HALO = 8
LANES = 128

QKV_W = 3 * DN_WIDTH
Z_OFF = QKV_W
SC_OFF = Z_OFF + DN_WIDTH
BA_OFF = SC_OFF + 3 * SC_WIDTH
BA_W = 256
PROJ_W = BA_OFF + BA_W

ADAM_LR = 0.001
ADAM_B1 = 0.9
ADAM_B2 = 0.999
ADAM_EPS = 1e-08
ADAM_WD = 0.01
ADAM_STEP = 10


def _pick(n, cands):
    for c in cands:
        if n % c == 0:
            return c
    return n


def _params(*sem):
    return pltpu.CompilerParams(dimension_semantics=sem)


def _rms_norm(x, g):
    return x * lax.rsqrt(jnp.mean(x * x, axis=-1, keepdims=True) + EPS) * g


def _dot(a, b, dims=(((1,), (0,)), ((), ()))):
    return lax.dot_general(a.astype(MXU_DTYPE), b.astype(MXU_DTYPE), dims, preferred_element_type=F32)


def _split_terms(x, terms):
    out = []
    for _ in range(terms):
        hi = x.astype(MXU_DTYPE)
        out.append(hi)
        x = x - hi.astype(F32)
    return out


def _ein_impl(spec, terms, a, b):
    ta, tb = terms
    if ta == 1 and tb == 1:
        return jnp.einsum(spec, a.astype(MXU_DTYPE), b.astype(MXU_DTYPE), preferred_element_type=F32)
    pa, pb = _split_terms(a, ta), _split_terms(b, tb)
    order = max(ta, tb) - 1
    acc = None
    for deg in range(order, -1, -1):
        for i in range(ta):
            j = deg - i
            if 0 <= j < tb:
                t = jnp.einsum(spec, pa[i], pb[j], preferred_element_type=F32)
                acc = t if acc is None else acc + t
    return acc


@functools.partial(jax.custom_vjp, nondiff_argnums=(0, 1))
def _ein(spec, terms, a, b):
    return _ein_impl(spec, terms, a, b)


def _ein_fwd(spec, terms, a, b):
    return _ein_impl(spec, terms, a, b), (a, b)


def _ein_bwd(spec, terms, res, ct):
    a, b = res
    xy, z = spec.split("->")
    x, y = xy.split(",")
    tc = min(max(terms), 2)
    da = _ein_impl(f"{z},{y}->{x}", (tc, terms[1]), ct, b)
    db = _ein_impl(f"{x},{z}->{y}", (terms[0], tc), a, ct)
    return da, db


_ein.defvjp(_ein_fwd, _ein_bwd)

FAST = (1, 1)
PRECISE = (2, 2)
LHS_EXACT = (1, 3)


def _causal_conv(cur, halo, w, k):
    tt = cur.shape[0]
    xp = jnp.concatenate([halo, cur], axis=0)
    y = None
    for j in range(k):
        start = HALO - (k - 1) + j
        term = xp[start:start + tt] * w[j:j + 1]
        y = term if y is None else y + term
    return y


def _matmul(a, b, mode, out_dtype, name, residual=None, ride=None):
    if mode == "nn":
        (m, k), (k2, n) = a.shape, b.shape
    elif mode == "nt":
        (m, k), (n, k2) = a.shape, b.shape
    else:
        (k, m), (k2, n) = a.shape, b.shape
    assert k == k2
    tm = _pick(m, (1024, 1408, 512, 256, 128))
    tn = _pick(n, (1280, 1408, 1024, 512, 256, 128))
    tk = k if k <= 2816 else _pick(k, (1024, 768, 512))
    gi, gj, nk = m // tm, n // tn, k // tk
    dims = {"nn": (((1,), (0,)), ((), ())), "nt": (((1,), (1,)), ((), ())), "tn": (((0,), (0,)), ((), ()))}[mode]
    a_spec = {"nn": pl.BlockSpec((tm, tk), lambda i, j, q: (i, q)),
              "nt": pl.BlockSpec((tm, tk), lambda i, j, q: (i, q)),
              "tn": pl.BlockSpec((tk, tm), lambda i, j, q: (q, i))}[mode]
    b_spec = {"nn": pl.BlockSpec((tk, tn), lambda i, j, q: (q, j)),
              "nt": pl.BlockSpec((tn, tk), lambda i, j, q: (j, q)),
              "tn": pl.BlockSpec((tk, tn), lambda i, j, q: (q, j))}[mode]
    o_spec = pl.BlockSpec((tm, tn), lambda i, j, q: (i, j))
    has_res = residual is not None
    n_in = 3 if has_res else 2
    nr = ride.na if ride else 0

    def body(*refs):
        a_ref, b_ref = refs[0], refs[1]
        r_ref = refs[2] if has_res else None
        ride_in = refs[n_in:n_in + nr]
        o_ref = refs[n_in + nr]
        ride_out = refs[n_in + nr + 1:n_in + 2 * nr + 1]
        acc = refs[n_in + 2 * nr + 1]
        ride_sems = refs[n_in + 2 * nr + 2:]
        i, j, q = pl.program_id(0), pl.program_id(1), pl.program_id(2)
        if ride:
            @pl.when((i == 0) & (j == 0) & (q == 0))
            def _():
                ride.start(ride_in, ride_out, ride_sems)

        @pl.when(q == 0)
        def _():
            acc[...] = jnp.zeros_like(acc)

        acc[...] += _dot(a_ref[...], b_ref[...], dims)

        @pl.when(q == nk - 1)
        def _():
            r = acc[...]
            if has_res:
                r = r + r_ref[...]
            o_ref[...] = r.astype(o_ref.dtype)

        if ride:
            @pl.when((i == gi - 1) & (j == gj - 1) & (q == nk - 1))
            def _():
                ride.finish(ride_in, ride_out, ride_sems)

    hbm = pl.BlockSpec(memory_space=pltpu.HBM)
    in_specs = [a_spec, b_spec] + ([o_spec] if has_res else []) + [hbm] * nr
    args = (a, b) + ((residual,) if has_res else ()) + (tuple(ride.arrays) if ride else ())
    res = pl.pallas_call(
        body, name=name, grid=(gi, gj, nk), in_specs=in_specs, out_specs=[o_spec] + [hbm] * nr,
        out_shape=[jax.ShapeDtypeStruct((m, n), out_dtype)] + (ride.out_shape if ride else []),
        scratch_shapes=[pltpu.VMEM((tm, tn), F32)] + (ride.scratch if ride else []),
        compiler_params=pltpu.CompilerParams(
            dimension_semantics=("arbitrary",) * 3 if ride else ("parallel", "parallel", "arbitrary"),
            has_side_effects=bool(ride)),
    )(*args)
    return (res[0], res[1:]) if ride else res[0]


def _norm_matmul(x, g, w, name, ride=None):
    t, d = x.shape
    n = w.shape[1]
    tm = _pick(t, (1024, 512, 256, 128))
    tn = _pick(n, (1280, 1408, 1024, 512, 256, 128))
    gi, gj = t // tm, n // tn
    nr = ride.na if ride else 0

    def body(*refs):
        x_ref, g_ref, w_ref = refs[:3]
        ride_in = refs[3:3 + nr]
        h_ref, y_ref = refs[3 + nr], refs[4 + nr]
        ride_out = refs[5 + nr:5 + 2 * nr]
        h_scr = refs[5 + 2 * nr]
        ride_sems = refs[6 + 2 * nr:]
        i, j = pl.program_id(0), pl.program_id(1)
        if ride:
            @pl.when((i == 0) & (j == 0))
            def _():
                ride.start(ride_in, ride_out, ride_sems)

        @pl.when(j == 0)
        def _():
            h = _rms_norm(x_ref[...], g_ref[...]).astype(MXU_DTYPE)
            h_scr[...] = h
            h_ref[...] = h

        y_ref[...] = _dot(h_scr[...], w_ref[...])

        if ride:
            @pl.when((i == gi - 1) & (j == gj - 1))
            def _():
                ride.finish(ride_in, ride_out, ride_sems)

    hbm = pl.BlockSpec(memory_space=pltpu.HBM)
    res = pl.pallas_call(
        body, name=name, grid=(gi, gj),
        in_specs=[pl.BlockSpec((tm, d), lambda i, j: (i, 0)), pl.BlockSpec((1, d), lambda i, j: (0, 0)),
                  pl.BlockSpec((d, tn), lambda i, j: (0, j))] + [hbm] * nr,
        out_specs=[pl.BlockSpec((tm, d), lambda i, j: (i, 0)), pl.BlockSpec((tm, tn), lambda i, j: (i, j))] + [hbm] * nr,
        out_shape=[jax.ShapeDtypeStruct((t, d), MXU_DTYPE), jax.ShapeDtypeStruct((t, n), F32)]
        + (ride.out_shape if ride else []),
        scratch_shapes=[pltpu.VMEM((tm, d), MXU_DTYPE)] + (ride.scratch if ride else []),
        compiler_params=pltpu.CompilerParams(
            dimension_semantics=("arbitrary",) * 2 if ride else ("parallel", "arbitrary"),
            has_side_effects=bool(ride)),
    )(x, g, w, *(ride.arrays if ride else ()))
    return (res[0], res[1], res[2:]) if ride else (res[0], res[1])


def _swiglu_math(g, u):
    return jax.nn.silu(g) * u


def _gu_tile(dff):
    return _pick(dff, (1408, 1024, 512, 256, 128))


def _interleave_gu(gate_t, up_t):
    tn = _gu_tile(gate_t.shape[0])
    pieces = []
    for j in range(gate_t.shape[0] // tn):
        pieces += [gate_t[j * tn:(j + 1) * tn], up_t[j * tn:(j + 1) * tn]]
    return jnp.concatenate(pieces, axis=0)


def _split_gu(gu_t):
    dff = gu_t.shape[0] // 2
    tn = _gu_tile(dff)
    tiles = [gu_t[j * tn:(j + 1) * tn] for j in range(2 * dff // tn)]
    return jnp.concatenate(tiles[0::2], axis=0), jnp.concatenate(tiles[1::2], axis=0)


def _ffn_up_swiglu(x, g, w_gu, name, ride=None):
    t, d = x.shape
    dff = w_gu.shape[0] // 2
    tn = _gu_tile(dff)
    tm = _pick(t, (512, 256, 128))
    gj, gi = dff // tn, t // tm
    nr = ride.na if ride else 0

    def body(*refs):
        x_ref, g_ref, w_ref = refs[:3]
        ride_in = refs[3:3 + nr]
        h_ref, gu_ref, act_ref = refs[3 + nr:6 + nr]
        ride_out = refs[6 + nr:6 + 2 * nr]
        ride_sems = refs[6 + 2 * nr:]
        j, i = pl.program_id(0), pl.program_id(1)
        if ride:
            @pl.when((i == 0) & (j == 0))
            def _():
                ride.start(ride_in, ride_out, ride_sems)

        h = _rms_norm(x_ref[...], g_ref[...]).astype(MXU_DTYPE)

        @pl.when(j == 0)
        def _():
            h_ref[...] = h

        y = _dot(h, w_ref[...], (((1,), (1,)), ((), ())))
        gu_ref[...] = y.astype(gu_ref.dtype)
        act_ref[...] = _swiglu_math(y[:, :tn], y[:, tn:]).astype(act_ref.dtype)

        if ride:
            @pl.when((i == gi - 1) & (j == gj - 1))
            def _():
                ride.finish(ride_in, ride_out, ride_sems)

    hbm = pl.BlockSpec(memory_space=pltpu.HBM)
    res = pl.pallas_call(
        body, name=name, grid=(gj, gi),
        in_specs=[pl.BlockSpec((tm, d), lambda j, i: (i, 0)), pl.BlockSpec((1, d), lambda j, i: (0, 0)),
                  pl.BlockSpec((2 * tn, d), lambda j, i: (j, 0))] + [hbm] * nr,
        out_specs=[pl.BlockSpec((tm, d), lambda j, i: (jnp.where(j == 0, i, gi - 1), 0)),
                   pl.BlockSpec((tm, 2 * tn), lambda j, i: (i, j)),
                   pl.BlockSpec((tm, tn), lambda j, i: (i, j))] + [hbm] * nr,
        out_shape=[jax.ShapeDtypeStruct((t, d), MXU_DTYPE), jax.ShapeDtypeStruct((t, 2 * dff), MXU_DTYPE),
                   jax.ShapeDtypeStruct((t, dff), MXU_DTYPE)] + (ride.out_shape if ride else []),
        scratch_shapes=(ride.scratch if ride else []),
        compiler_params=pltpu.CompilerParams(dimension_semantics=("arbitrary", "arbitrary"),
                                             has_side_effects=bool(ride)),
    )(x, g, w_gu, *(ride.arrays if ride else ()))
    return (res[0], res[1], res[2], res[3:]) if ride else tuple(res)


def _ffn_down_dx_swiglu(dx_out, w_down, gu, name):
    t, d = dx_out.shape
    dff = w_down.shape[0]
    tn = _gu_tile(dff)
    tm = _pick(t, (512, 256, 128))

    def body(dx_ref, w_ref, gu_ref, o_ref):
        dact = _dot(dx_ref[...], w_ref[...], (((1,), (1,)), ((), ())))
        gu_v = gu_ref[...].astype(F32)
        _, vjp = jax.vjp(_swiglu_math, gu_v[:, :tn], gu_v[:, tn:])
        dg, du = vjp(dact)
        o_ref[...] = jnp.concatenate([dg, du], axis=1).astype(o_ref.dtype)

    return pl.pallas_call(
        body, name=name, grid=(dff // tn, t // tm),
        in_specs=[pl.BlockSpec((tm, d), lambda j, i: (i, 0)), pl.BlockSpec((tn, d), lambda j, i: (j, 0)),
                  pl.BlockSpec((tm, 2 * tn), lambda j, i: (i, j))],
        out_specs=pl.BlockSpec((tm, 2 * tn), lambda j, i: (i, j)),
        out_shape=jax.ShapeDtypeStruct((t, 2 * dff), MXU_DTYPE),
        compiler_params=_params("parallel", "parallel"),
    )(dx_out, w_down, gu)


def _matmul_norm_bwd(dy, w, x, g, dres, name, ride=None, w_rows_k=False):
    t, k = dy.shape
    d = w.shape[1] if w_rows_k else w.shape[0]
    tm = _pick(t, (1024, 512, 256, 128))
    tk = k if k <= 2816 else _pick(k, (1408, 1280, 1024, 768, 512))
    tr = _pick(tm, (256, 128))
    gi, nk = t // tm, k // tk
    nr = ride.na if ride else 0

    def body(*refs):
        dy_ref, w_ref, x_ref, g_ref, dres_ref = refs[:5]
        ride_in = refs[5:5 + nr]
        dx_ref, dg_ref = refs[5 + nr], refs[6 + nr]
        ride_out = refs[7 + nr:7 + 2 * nr]
        acc = refs[7 + 2 * nr]
        ride_sems = refs[8 + 2 * nr:]
        i, q = pl.program_id(0), pl.program_id(1)

        @pl.when((i == 0) & (q == 0))
        def _():
            dg_ref[...] = jnp.zeros_like(dg_ref)
            if ride:
                ride.start(ride_in, ride_out, ride_sems)

        @pl.when(q == 0)
        def _():
            acc[...] = jnp.zeros_like(acc)

        acc[...] += _dot(dy_ref[...], w_ref[...], (((1,), (0 if w_rows_k else 1,)), ((), ())))

        @pl.when(q == nk - 1)
        def _():
            for r in range(tm // tr):
                rows = slice(r * tr, (r + 1) * tr)
                _, vjp = jax.vjp(_rms_norm, x_ref[rows], g_ref[...])
                dxn, dg = vjp(acc[rows])
                dx_ref[rows] = dres_ref[rows] + dxn
                dg_ref[...] += dg

        if ride:
            @pl.when((i == gi - 1) & (q == nk - 1))
            def _():
                ride.finish(ride_in, ride_out, ride_sems)

    hbm = pl.BlockSpec(memory_space=pltpu.HBM)
    row = pl.BlockSpec((tm, d), lambda i, q: (i, 0))
    res = pl.pallas_call(
        body, name=name, grid=(gi, nk),
        in_specs=[pl.BlockSpec((tm, tk), lambda i, q: (i, q)),
                  pl.BlockSpec((tk, d), lambda i, q: (q, 0)) if w_rows_k else pl.BlockSpec((d, tk), lambda i, q: (0, q)),
                  row,
                  pl.BlockSpec((1, d), lambda i, q: (0, 0)), row] + [hbm] * nr,
        out_specs=[row, pl.BlockSpec((1, d), lambda i, q: (0, 0))] + [hbm] * nr,
        out_shape=[jax.ShapeDtypeStruct((t, d), F32), jax.ShapeDtypeStruct((1, d), F32)]
        + (ride.out_shape if ride else []),
        scratch_shapes=[pltpu.VMEM((tm, d), F32)] + (ride.scratch if ride else []),
        compiler_params=pltpu.CompilerParams(dimension_semantics=("arbitrary", "arbitrary"),
                                             has_side_effects=bool(ride)),
    )(dy, w, x, g, dres, *(ride.arrays if ride else ()))
    return (res[0], res[1], res[2:]) if ride else (res[0], res[1])


def _tok_call(name, fn, t, tt, tok_in, const_in, tok_out, acc_out):
    nblk = t // tt
    hb = tt // HALO
    in_specs, args = [], []
    for arr, w, cb, mode in tok_in:
        if mode == "cur":
            spec = pl.BlockSpec((tt, w), lambda i, cb=cb: (i, cb))
        elif mode == "prev":
            spec = pl.BlockSpec((HALO, w), lambda i, cb=cb: (jnp.maximum(i * hb - 1, 0), cb))
        else:
            spec = pl.BlockSpec((HALO, w), lambda i, cb=cb: (jnp.minimum(i + 1, nblk - 1), cb))
        in_specs.append(spec)
        args.append(arr)
    for arr in const_in:
        in_specs.append(pl.BlockSpec(arr.shape, lambda i: (0, 0)))
        args.append(arr)
    out_specs, out_shape = [], []
    for rows, w, dt in tok_out:
        out_specs.append(pl.BlockSpec((rows, w), lambda i: (i, 0)))
        out_shape.append(jax.ShapeDtypeStruct((nblk * rows, w), dt))
    for shp, dt in acc_out:
        out_specs.append(pl.BlockSpec(shp, lambda i: (0, 0)))
        out_shape.append(jax.ShapeDtypeStruct(shp, dt))
    n_tok, n_const, n_out = len(tok_in), len(const_in), len(tok_out)

    def body(*refs):
        i = pl.program_id(0)
        tok_vals = []
        for (_, _, _, mode), r in zip(tok_in, refs[:n_tok]):
            v = r[...]
            if mode == "prev":
                v = jnp.where(i > 0, v, jnp.zeros_like(v))
            elif mode == "next8":
                v = jnp.where(i < nblk - 1, v, jnp.zeros_like(v))
            tok_vals.append(v)
        const_vals = [r[...] for r in refs[n_tok:n_tok + n_const]]
        outs, accs = fn(tok_vals, const_vals)
        o_refs = refs[n_tok + n_const:n_tok + n_const + n_out]
        a_refs = refs[n_tok + n_const + n_out:]
        for r, v in zip(o_refs, outs):
            r[...] = v.astype(r.dtype)
        if a_refs:
            @pl.when(i == 0)
            def _():
                for r in a_refs:
                    r[...] = jnp.zeros_like(r)

            for r, v in zip(a_refs, accs):
                r[...] += v.astype(r.dtype)

    res = pl.pallas_call(
        body, name=name, grid=(nblk,), in_specs=in_specs, out_specs=out_specs, out_shape=out_shape,
        compiler_params=_params("arbitrary" if acc_out else "parallel"),
    )(*args)
    return res


def _dn_pre_math(cur, halo, ba, cw, alog, dtb):
    tt = cur.shape[0]
    a = jax.nn.silu(_causal_conv(cur, halo, cw, DN_CONV))
    pieces = []
    for p in range(2 * DN_HEADS):
        xh = a[:, p * HEAD_DIM:(p + 1) * HEAD_DIM]
        xh = xh * lax.rsqrt(jnp.sum(xh * xh, axis=-1, keepdims=True) + EPS)
        if p < DN_HEADS:
            xh = xh * (HEAD_DIM ** -0.5)
        pieces.append(xh)
    pieces.append(a[:, 2 * DN_WIDTH:])
    qkvn = jnp.concatenate(pieces, axis=1)
    lane = lax.broadcasted_iota(jnp.int32, ba.shape, 1)
    raw = jnp.where(lane < DN_HEADS, jax.nn.sigmoid(ba), -jnp.exp(alog) * jax.nn.softplus(ba + dtb))
    r = lax.broadcasted_iota(jnp.int32, (tt, tt), 0)
    c = lax.broadcasted_iota(jnp.int32, (tt, tt), 1)
    tri = jnp.where((r // CHUNK == c // CHUNK) & (c <= r), 1.0, 0.0).astype(F32)
    cums = _ein("ij,jk->ik", LHS_EXACT, tri, raw)
    bg = jnp.where(lane < DN_HEADS, raw, cums)
    return qkvn, bg


def _mix_math(o, z, gb, gc, gc_halo, hv, hv_halo, dng, scg, scw):
    outs = []
    for h in range(DN_HEADS):
        sl = slice(h * HEAD_DIM, (h + 1) * HEAD_DIM)
        oh = o[:, sl]
        outs.append(oh * lax.rsqrt(jnp.mean(oh * oh, axis=-1, keepdims=True) + EPS) * dng * jax.nn.silu(z[:, sl]))
    y = gb * _causal_conv(gc * hv, gc_halo * hv_halo, scw, SC_CONV)
    gw = SC_WIDTH // SC_GROUPS
    for g in range(SC_GROUPS):
        sl = slice(g * gw, (g + 1) * gw)
        yg = y[:, sl]
        outs.append(yg * lax.rsqrt(jnp.mean(yg * yg, axis=-1, keepdims=True) + EPS) * scg[:, sl])
    return jnp.concatenate(outs, axis=1)


def _tri_inverse(a):
    c = a.shape[-1]
    r = lax.broadcasted_iota(jnp.int32, (c, c), 0)
    q = lax.broadcasted_iota(jnp.int32, (c, c), 1)
    eye = jnp.where(r == q, 1.0, 0.0).astype(F32)[None]
    blk = (r // 16 == q // 16)[None]
    d = jnp.where(blk, a, 0.0)
    o = a - d
    mm = functools.partial(_ein, "bij,bjk->bik", PRECISE)
    p = eye - d
    n = mm(d, d)
    for _ in range(2):
        both = mm(jnp.concatenate([n, p], axis=1), n)
        n = both[:, :c]
        p = p + both[:, c:]
    p = p + mm(p, n)
    e = mm(p, o)
    e2 = mm(e, e)
    left = eye - e + e2 - mm(e, e2)
    return mm(left, p)


@jax.custom_vjp
def _inverse_known(a, tinv):
    return tinv


def _inverse_known_fwd(a, tinv):
    return tinv, tinv


def _inverse_known_bwd(tinv, ct):
    left = _ein("bji,bjk->bik", PRECISE, tinv, ct)
    return -_ein("bik,bjk->bij", PRECISE, left, tinv), jnp.zeros_like(tinv)


_inverse_known.defvjp(_inverse_known_fwd, _inverse_known_bwd)


def _delta_intra_math(q, k, v, bg, head, tinv_known=None):
    n = q.shape[0]
    nb = n // CHUNK
    lane = lax.broadcasted_iota(jnp.int32, bg.shape, 1)
    beta = jnp.sum(jnp.where(lane == head, bg, 0.0), axis=1, keepdims=True).reshape(nb, CHUNK, 1)
    gc = jnp.sum(jnp.where(lane == head + DN_HEADS, bg, 0.0), axis=1, keepdims=True).reshape(nb, CHUNK, 1)
    q3, k3, v3 = (a.reshape(nb, CHUNK, HEAD_DIM) for a in (q, k, v))
    r = lax.broadcasted_iota(jnp.int32, (CHUNK, CHUNK), 0)
    c = lax.broadcasted_iota(jnp.int32, (CHUNK, CHUNK), 1)
    eye = jnp.where(r == c, 1.0, 0.0).astype(F32)[None]
    gcr = _ein("bik,bkj->bij", LHS_EXACT, jnp.ones((nb, CHUNK, CHUNK), F32), gc * eye)
    decay = jnp.exp(jnp.where((r >= c)[None], gc - gcr, -1e30))
    kb = k3 * beta
    vb = v3 * beta
    egc = jnp.exp(gc)
    on_k = _ein("bcd,bmd->bcm", FAST, jnp.concatenate([kb, q3], axis=1), k3)
    a = jnp.where((r > c)[None], on_k[:, :CHUNK] * decay, 0.0)
    tinv = _tri_inverse(a) if tinv_known is None else _inverse_known(a, tinv_known)
    uw = _ein("bcm,bmd->bcd", PRECISE, tinv, jnp.concatenate([vb, kb * egc], axis=2))
    u, w = uw[:, :, :HEAD_DIM], uw[:, :, HEAD_DIM:]
    qk = on_k[:, CHUNK:] * decay
    row = lax.broadcasted_iota(jnp.int32, (nb, CHUNK, 1), 1)
    glast = jnp.sum(jnp.where(row == CHUNK - 1, gc, 0.0), axis=1, keepdims=True)
    qd = q3 * egc
    kd = k3 * jnp.exp(glast - gc)
    glb = jnp.broadcast_to(jnp.exp(glast), (nb, HALO, LANES))
    flat = lambda x: x.reshape(n, HEAD_DIM)
    return flat(u), flat(w), flat(qd), flat(kd), qk, glb, tinv


def _delta_step_math(u, w, qd, kd, qk, gl, s):
    c = u.shape[0]
    on_s = _ein("ck,kv->cv", FAST, jnp.concatenate([w, qd], axis=0), s)
    vnew = u - on_s[:c]
    on_v = _ein("cm,mv->cv", FAST, jnp.concatenate([qk, kd.T], axis=0), vnew)
    o = on_s[c:] + on_v[:c]
    s2 = s * gl + on_v[c:]
    return o, s2


def _delta_intra(qkvn, bg, nb):
    t = qkvn.shape[0]
    n = t // CHUNK
    rows = nb * CHUNK

    def body(q_ref, k_ref, v_ref, bg_ref, u_ref, w_ref, qd_ref, kd_ref, qk_ref, gl_ref, ti_ref):
        outs = _delta_intra_math(q_ref[...], k_ref[...], v_ref[...], bg_ref[...], pl.program_id(1))
        for r, v in zip((u_ref, w_ref, qd_ref, kd_ref, qk_ref, gl_ref, ti_ref), outs):
            r[...] = v

    col = lambda off: pl.BlockSpec((rows, HEAD_DIM), lambda b, h, off=off: (b, off + h))
    tok = jax.ShapeDtypeStruct((t, DN_WIDTH), F32)
    return pl.pallas_call(
        body, name="delta_intra", grid=(n // nb, DN_HEADS),
        in_specs=[col(0), col(DN_HEADS), col(2 * DN_HEADS), pl.BlockSpec((rows, LANES), lambda b, h: (b, 0))],
        out_specs=[col(0)] * 4 + [pl.BlockSpec((nb, None, CHUNK, CHUNK), lambda b, h: (b, h, 0, 0)),
                                  pl.BlockSpec((nb, None, HALO, LANES), lambda b, h: (b, h, 0, 0)),
                                  pl.BlockSpec((nb, None, CHUNK, CHUNK), lambda b, h: (b, h, 0, 0))],
        out_shape=[tok] * 4 + [jax.ShapeDtypeStruct((n, DN_HEADS, CHUNK, CHUNK), F32),
                               jax.ShapeDtypeStruct((n, DN_HEADS, HALO, LANES), F32),
                               jax.ShapeDtypeStruct((n, DN_HEADS, CHUNK, CHUNK), F32)],
        compiler_params=_params("parallel", "arbitrary"),
    )(qkvn, qkvn, qkvn, bg)


def _delta_intra_bwd(qkvn, bg, tinv, cts, nb):
    t = qkvn.shape[0]
    n = t // CHUNK
    rows = nb * CHUNK

    def body(q_ref, k_ref, v_ref, bg_ref, ti_ref, du, dw, dqd, dkd, dqk, dgl, dq_ref, dk_ref, dv_ref, dbg_ref):
        h = pl.program_id(1)
        ti = ti_ref[...]
        _, vjp = jax.vjp(lambda q, k, v, b: _delta_intra_math(q, k, v, b, h, ti)[:6],
                         q_ref[...], k_ref[...], v_ref[...], bg_ref[...])
        dq, dk, dv, dbg = vjp((du[...], dw[...], dqd[...], dkd[...], dqk[...], dgl[...]))
        dq_ref[...] = dq
        dk_ref[...] = dk
        dv_ref[...] = dv

        @pl.when(h == 0)
        def _():
            dbg_ref[...] = jnp.zeros_like(dbg_ref)

        dbg_ref[...] += dbg

    col = lambda off: pl.BlockSpec((rows, HEAD_DIM), lambda b, h, off=off: (b, off + h))
    bgs = pl.BlockSpec((rows, LANES), lambda b, h: (b, 0))
    qks = pl.BlockSpec((nb, None, CHUNK, CHUNK), lambda b, h: (b, h, 0, 0))
    gls = pl.BlockSpec((nb, None, HALO, LANES), lambda b, h: (b, h, 0, 0))
    tok = jax.ShapeDtypeStruct((t, DN_WIDTH), F32)
    return pl.pallas_call(
        body, name="delta_intra_bwd", grid=(n // nb, DN_HEADS),
        in_specs=[col(0), col(DN_HEADS), col(2 * DN_HEADS), bgs, qks, col(0), col(0), col(0), col(0), qks, gls],
        out_specs=[col(0), col(0), col(0), bgs],
        out_shape=[tok, tok, tok, jax.ShapeDtypeStruct((t, LANES), F32)],
        compiler_params=_params("parallel", "arbitrary"),
    )(qkvn, qkvn, qkvn, bg, tinv, *cts)


def _delta_scan(u, w, qd, kd, qk, glb, cb):
    t = u.shape[0]
    n = t // CHUNK
    rows = cb * CHUNK

    def body(u_ref, w_ref, qd_ref, kd_ref, qk_ref, gl_ref, o_ref, s_ref, s_scr):
        @pl.when(pl.program_id(0) == 0)
        def _():
            s_scr[...] = jnp.zeros_like(s_scr)

        def chunk(c, carry):
            r0 = pl.multiple_of(c * CHUNK, CHUNK)
            for h in range(DN_HEADS):
                sl = (pl.ds(r0, CHUNK), slice(h * HEAD_DIM, (h + 1) * HEAD_DIM))
                s = s_scr[h]
                s_ref[c, h] = s
                o, s2 = _delta_step_math(u_ref[sl], w_ref[sl], qd_ref[sl], kd_ref[sl], qk_ref[c, h],
                                         gl_ref[c, h][0:1, :], s)
                o_ref[sl] = o
                s_scr[h] = s2
            return carry

        lax.fori_loop(0, cb, chunk, 0)

    tok = pl.BlockSpec((rows, DN_WIDTH), lambda i: (i, 0))
    return pl.pallas_call(
        body, name="delta_scan", grid=(n // cb,),
        in_specs=[tok] * 4 + [pl.BlockSpec((cb, DN_HEADS, CHUNK, CHUNK), lambda i: (i, 0, 0, 0)),
                              pl.BlockSpec((cb, DN_HEADS, HALO, LANES), lambda i: (i, 0, 0, 0))],
        out_specs=[tok, pl.BlockSpec((cb, DN_HEADS, HEAD_DIM, HEAD_DIM), lambda i: (i, 0, 0, 0))],
        out_shape=[jax.ShapeDtypeStruct((t, DN_WIDTH), F32),
                   jax.ShapeDtypeStruct((n, DN_HEADS, HEAD_DIM, HEAD_DIM), F32)],
        scratch_shapes=[pltpu.VMEM((DN_HEADS, HEAD_DIM, HEAD_DIM), F32)],
        compiler_params=_params("arbitrary"),
    )(u, w, qd, kd, qk, glb)


def _delta_scan_bwd(u, w, qd, kd, qk, glb, s_all, do, cb):
    t = u.shape[0]
    n = t // CHUNK
    nblk = n // cb
    rows = cb * CHUNK

    def body(u_ref, w_ref, qd_ref, kd_ref, qk_ref, gl_ref, s_ref, do_ref,
             du_ref, dw_ref, dqd_ref, dkd_ref, dqk_ref, dgl_ref, ds_scr):
        @pl.when(pl.program_id(0) == 0)
        def _():
            ds_scr[...] = jnp.zeros_like(ds_scr)

        def chunk(step, carry):
            c = cb - 1 - step
            r0 = pl.multiple_of(c * CHUNK, CHUNK)
            for h in range(DN_HEADS):
                sl = (pl.ds(r0, CHUNK), slice(h * HEAD_DIM, (h + 1) * HEAD_DIM))
                gl_tile = gl_ref[c, h]
                prim = (u_ref[sl], w_ref[sl], qd_ref[sl], kd_ref[sl], qk_ref[c, h], gl_tile, s_ref[c, h])
                _, vjp = jax.vjp(lambda a, b, cc, d, e, g, s: _delta_step_math(a, b, cc, d, e, g[0:1, :], s), *prim)
                du, dw, dqd, dkd, dqk, dgl, ds = vjp((do_ref[sl], ds_scr[h]))
                du_ref[sl] = du
                dw_ref[sl] = dw
                dqd_ref[sl] = dqd
                dkd_ref[sl] = dkd
                dqk_ref[c, h] = dqk
                dgl_ref[c, h] = dgl
                ds_scr[h] = ds
            return carry

        lax.fori_loop(0, cb, chunk, 0)

    rev = lambda i: nblk - 1 - i
    tok = pl.BlockSpec((rows, DN_WIDTH), lambda i: (rev(i), 0))
    qks = pl.BlockSpec((cb, DN_HEADS, CHUNK, CHUNK), lambda i: (rev(i), 0, 0, 0))
    gls = pl.BlockSpec((cb, DN_HEADS, HALO, LANES), lambda i: (rev(i), 0, 0, 0))
    ss = pl.BlockSpec((cb, DN_HEADS, HEAD_DIM, HEAD_DIM), lambda i: (rev(i), 0, 0, 0))
    tshape = jax.ShapeDtypeStruct((t, DN_WIDTH), F32)
    return pl.pallas_call(
        body, name="delta_scan_bwd", grid=(nblk,),
        in_specs=[tok] * 4 + [qks, gls, ss, tok],
        out_specs=[tok] * 4 + [qks, gls],
        out_shape=[tshape] * 4 + [jax.ShapeDtypeStruct(qk.shape, F32), jax.ShapeDtypeStruct(glb.shape, F32)],
        scratch_shapes=[pltpu.VMEM((DN_HEADS, HEAD_DIM, HEAD_DIM), F32)],
        compiler_params=_params("arbitrary"),
    )(u, w, qd, kd, qk, glb, s_all, do)


def _peer(mask):
    x, y, c = lax.axis_index("x"), lax.axis_index("y"), lax.axis_index("c")
    return (x ^ ((mask >> 2) & 1), y ^ ((mask >> 1) & 1), c ^ (mask & 1))


def _my_index():
    return 4 * lax.axis_index("x") + 2 * lax.axis_index("y") + lax.axis_index("c")


class _Exchange:
    CHIP_MASKS = (4, 2, 6)

    def __init__(self, kind, arrays):
        self.kind = kind
        self.arrays = list(arrays)
        self.na = na = len(self.arrays)
        if kind == "gather":
            self.out_shape = [jax.ShapeDtypeStruct((N_DEV,) + a.shape, a.dtype) for a in self.arrays]
        else:
            self.out_shape = [jax.ShapeDtypeStruct(a.shape, a.dtype) for a in self.arrays]
        self.scratch = [pltpu.SemaphoreType.DMA((na, 7)), pltpu.SemaphoreType.DMA((na, 7)),
                        pltpu.SemaphoreType.DMA((na,))]

    def _copies(self, ins, outs, sems):
        send_sems, recv_sems, local_sems = sems
        me = _my_index()
        local, first, passed, arrivals = [], [], [], []
        if self.kind == "gather":
            def rc(a, k, block, to, own=False):
                def make():
                    dst = outs[a].at[block]
                    return pltpu.make_async_remote_copy(src_ref=ins[a] if own else dst, dst_ref=dst,
                                                        send_sem=send_sems.at[a, k], recv_sem=recv_sems.at[a, k],
                                                        device_id=to, device_id_type=MESH)
                return make

            sib = _peer(1)
            for a in range(self.na):
                local.append(lambda a=a: pltpu.make_async_copy(ins[a], outs[a].at[me], local_sems.at[a]))
                first.append(rc(a, 0, me, sib, own=True))
                arrivals.append(rc(a, 0, me ^ 1, _peer(0)))
                for j, m in enumerate(self.CHIP_MASKS):
                    first.append(rc(a, 1 + j, me, _peer(m), own=True))
                    passed.append((rc(a, 1 + j, me ^ m, _peer(0)), rc(a, 4 + j, me ^ m, sib)))
                    arrivals.append(rc(a, 4 + j, me ^ m ^ 1, _peer(0)))
        else:
            for a in range(self.na):
                local.append(lambda a=a: pltpu.make_async_copy(ins[a].at[me], outs[a].at[me], local_sems.at[a]))
                for m in range(1, N_DEV):
                    def make(a=a, m=m):
                        return pltpu.make_async_remote_copy(
                            src_ref=ins[a].at[me ^ m], dst_ref=outs[a].at[me], send_sem=send_sems.at[a, m - 1],
                            recv_sem=recv_sems.at[a, m - 1], device_id=_peer(m), device_id_type=MESH)
                    first.append(make)
                    arrivals.append(make)
        return local, first, passed, arrivals

    def start(self, ins, outs, sems):
        local, first, _, _ = self._copies(ins, outs, sems)
        for make in local + first:
            make().start()

    def finish(self, ins, outs, sems):
        local, first, passed, arrivals = self._copies(ins, outs, sems)
        for landed, onward in passed:
            landed().wait_recv()
            onward().start()
        for make in arrivals:
            make().wait_recv()
        for make in first + [p for _, p in passed]:
            make().wait_send()
        for make in local:
            make().wait()

    def run(self, name):
        na = self.na

        def body(*refs):
            ins, outs, sems = refs[:na], refs[na:2 * na], refs[2 * na:]
            self.start(ins, outs, sems)
            self.finish(ins, outs, sems)

        hbm = pl.BlockSpec(memory_space=pltpu.HBM)
        return pl.pallas_call(
            body, name=name, in_specs=[hbm] * na, out_specs=[hbm] * na, out_shape=self.out_shape,
            scratch_shapes=self.scratch, compiler_params=pltpu.CompilerParams(has_side_effects=True),
        )(*self.arrays)


def _all_gather(shards, name):
    return _Exchange("gather", shards).run(name)


def _adamw_math(w, g, m, v):
    m2 = ADAM_B1 * m + (1.0 - ADAM_B1) * g
    v2 = ADAM_B2 * v + (1.0 - ADAM_B2) * jnp.square(g)
    m_hat = m2 / (1.0 - ADAM_B1 ** ADAM_STEP)
    v_hat = v2 / (1.0 - ADAM_B2 ** ADAM_STEP)
    delta = -ADAM_LR * (m_hat / (jnp.sqrt(v_hat) + ADAM_EPS) + ADAM_WD * w)
    return delta, m2, v2


def _sum_adamw(parts, w, m, v, name):
    r, c = w.shape
    tr = _pick(r, (512, 256, 352, 128, 64, 32, 16, 8))
    np_ = parts.shape[0]

    def body(p_ref, w_ref, m_ref, v_ref, g_ref, d_ref, m2_ref, v2_ref):
        g = p_ref[0].astype(F32)
        for d in range(1, np_):
            g = g + p_ref[d].astype(F32)
        delta, m2, v2 = _adamw_math(w_ref[...], g, m_ref[...], v_ref[...])
        g_ref[...] = g
        d_ref[...] = delta
        m2_ref[...] = m2
        v2_ref[...] = v2

    blk = pl.BlockSpec((tr, c), lambda i: (i, 0))
    shp = jax.ShapeDtypeStruct((r, c), F32)
    return pl.pallas_call(
        body, name=name, grid=(r // tr,),
        in_specs=[pl.BlockSpec((np_, tr, c), lambda i: (0, i, 0)), blk, blk, blk],
        out_specs=[blk] * 4, out_shape=[shp] * 4, compiler_params=_params("parallel"),
    )(parts, w, m, v)


def _sum_rows(parts, name):
    np_, r, c = parts.shape

    def body(p_ref, o_ref):
        g = p_ref[0]
        for d in range(1, np_):
            g = g + p_ref[d]
        o_ref[...] = g

    return pl.pallas_call(body, name=name, out_shape=jax.ShapeDtypeStruct((r, c), F32))(parts)


def _pad_w_in(w):
    d = w.shape[0]
    n_ba = 2 * DN_HEADS
    a = w[:, :SC_OFF]
    ba = w[:, SC_OFF:SC_OFF + n_ba]
    sc = w[:, SC_OFF + n_ba:]
    return jnp.concatenate([a, sc, ba, jnp.zeros((d, BA_W - n_ba), w.dtype)], axis=1)


def _unpad_w_in(wp):
    n_ba = 2 * DN_HEADS
    return jnp.concatenate([wp[:, :SC_OFF], wp[:, BA_OFF:BA_OFF + n_ba], wp[:, SC_OFF:BA_OFF]], axis=1)


def _lane_row(v, off):
    return jnp.pad(v.astype(F32), (off, LANES - off - v.shape[0]))[None]


TT = 256
NB_INTRA = 8
CB_SCAN = 8


def _layer_fwd(x, p, ride_proj, ride_ffn, late):
    t, d = x.shape
    got_proj = got_ffn = None
    if ride_proj:
        h, proj, got_proj = _norm_matmul(x, p["norm1_g"], p["w_in"], "proj_fwd_gather", ride=ride_proj)
    else:
        h, proj = _norm_matmul(x, p["norm1_g"], p["w_in"], "proj_fwd")
    p = {**p, **late(got_proj)}
    qkvn, bg = _tok_call(
        "dn_pre", lambda tv, cv: (_dn_pre_math(*tv, *cv), ()), t, TT,
        [(proj, QKV_W, 0, "cur"), (proj, QKV_W, 0, "prev"), (proj, LANES, BA_OFF // LANES, "cur")],
        [p["dn_conv_w"], p["alog_row"], p["dt_row"]],
        [(TT, QKV_W, F32), (TT, LANES, F32)], [])
    u, w, qd, kd, qk, glb, tinv = _delta_intra(qkvn, bg, NB_INTRA)
    o, s_all = _delta_scan(u, w, qd, kd, qk, glb, CB_SCAN)
    cb0 = SC_OFF // SC_WIDTH
    mix_in = [(o, DN_WIDTH, 0, "cur"), (proj, DN_WIDTH, Z_OFF // DN_WIDTH, "cur"),
              (proj, SC_WIDTH, cb0, "cur"), (proj, SC_WIDTH, cb0 + 1, "cur"), (proj, SC_WIDTH, cb0 + 1, "prev"),
              (proj, SC_WIDTH, cb0 + 2, "cur"), (proj, SC_WIDTH, cb0 + 2, "prev")]
    mix_const = [p["dn_norm_g"], p["sc_norm_g"], p["sc_conv_w"]]
    (cat,) = _tok_call("mix_post", lambda tv, cv: ((_mix_math(*tv, *cv),), ()), t, TT,
                       mix_in, mix_const, [(TT, 2 * DN_WIDTH, MXU_DTYPE)], [])
    x_mid = _matmul(cat, p["w_out"], "nn", F32, "out_proj", residual=x)
    if ride_ffn:
        h2, gu, act, got_ffn = _ffn_up_swiglu(x_mid, p["norm2_g"], p["w_gu"], "ffn_up_gather", ride=ride_ffn)
    else:
        h2, gu, act = _ffn_up_swiglu(x_mid, p["norm2_g"], p["w_gu"], "ffn_up")
    x_out = _matmul(act, p["w_down"], "nn", F32, "ffn_down", residual=x_mid)
    saved = dict(x=x, h=h, proj=proj, qkvn=qkvn, bg=bg, u=u, w=w, qd=qd, kd=kd, qk=qk, glb=glb, tinv=tinv, s_all=s_all, o=o,
                 cat=cat, x_mid=x_mid, h2=h2, gu=gu, act=act, mix_in=mix_in, mix_const=mix_const)
    return x_out, saved, p, got_proj, got_ffn


def _layer_bwd(dx_out, p, s, ride_prev, ride_gu, ride_down):
    t, d = dx_out.shape
    got = {}
    dgu = _ffn_down_dx_swiglu(dx_out, p["w_down"], s["gu"], "ffn_down_dx")
    d_w_down = _matmul(s["act"], dx_out, "tn", MXU_DTYPE, "ffn_down_dw")
    if ride_prev:
        dx_mid, d_norm2, got["prev"] = _matmul_norm_bwd(dgu, p["w_gu"], s["x_mid"], p["norm2_g"], dx_out,
                                                        "ffn_up_dx_scatter", ride=ride_prev, w_rows_k=True)
    else:
        dx_mid, d_norm2 = _matmul_norm_bwd(dgu, p["w_gu"], s["x_mid"], p["norm2_g"], dx_out, "ffn_up_dx",
                                           w_rows_k=True)
    d_w_gu = _matmul(dgu, s["h2"], "tn", MXU_DTYPE, "ffn_up_dw")
    dcat = _matmul(dx_mid, p["w_out"], "nt", F32, "out_proj_dx")
    d_w_out = _matmul(s["cat"], dx_mid, "tn", MXU_DTYPE, "out_proj_dw")

    def mix_bwd(tv, cv):
        prim = tuple(tv[:7]) + tuple(cv)
        _, vjp = jax.vjp(_mix_math, *prim)
        do, dz, dgb, dgc, dgch, dhv, dhvh, ddng, dscg, dscw = vjp(tv[7])
        return (do, dz, dgb, dgc, dgch, dhv, dhvh), (ddng, dscg, dscw)

    wide = (TT, DN_WIDTH, F32)
    halo = (HALO, SC_WIDTH, F32)
    do, dz, dgb, dgc, dgc_h, dhv, dhv_h, d_dn_norm, d_sc_norm, d_sc_conv = _tok_call(
        "mix_post_bwd", mix_bwd, t, TT, s["mix_in"] + [(dcat, 2 * DN_WIDTH, 0, "cur")], s["mix_const"],
        [wide, wide, wide, wide, halo, wide, halo],
        [((1, HEAD_DIM), F32), ((1, SC_WIDTH), F32), ((SC_CONV, SC_WIDTH), F32)])
    cts = _delta_scan_bwd(s["u"], s["w"], s["qd"], s["kd"], s["qk"], s["glb"], s["s_all"], do, CB_SCAN)
    dq, dk, dv, dbg = _delta_intra_bwd(s["qkvn"], s["bg"], s["tinv"], cts, NB_INTRA)
    proj = s["proj"]

    def dn_pre_bwd(tv, cv):
        cur, hal, ba, dq_, dk_, dv_, dbg_ = tv
        _, vjp = jax.vjp(_dn_pre_math, cur, hal, ba, *cv)
        dcur, dhal, dba, dcw, dal, ddt = vjp((jnp.concatenate([dq_, dk_, dv_], axis=1), dbg_))
        return (dcur, dhal, dba), (dcw, dal, ddt)

    dqkv, dqkv_h, dba, d_dn_conv, d_alog, d_dt = _tok_call(
        "dn_pre_bwd", dn_pre_bwd, t, TT,
        [(proj, QKV_W, 0, "cur"), (proj, QKV_W, 0, "prev"), (proj, LANES, BA_OFF // LANES, "cur"),
         (dq, DN_WIDTH, 0, "cur"), (dk, DN_WIDTH, 0, "cur"), (dv, DN_WIDTH, 0, "cur"), (dbg, LANES, 0, "cur")],
        [p["dn_conv_w"], p["alog_row"], p["dt_row"]],
        [(TT, QKV_W, F32), (HALO, QKV_W, F32), (TT, LANES, F32)],
        [((DN_CONV, QKV_W), F32), ((1, LANES), F32), ((1, LANES), F32)])

    def assemble(tv, cv):
        dqkv_, dqkv_n, dz_, dgb_, dgc_, dgc_n, dhv_, dhv_n, dba_ = tv

        def with_halo(cur, nxt):
            return cur + jnp.concatenate([jnp.zeros((TT - HALO, cur.shape[1]), F32), nxt], axis=0)

        out = jnp.concatenate([with_halo(dqkv_, dqkv_n), dz_, dgb_, with_halo(dgc_, dgc_n), with_halo(dhv_, dhv_n),
                               dba_, jnp.zeros((TT, BA_W - LANES), F32)], axis=1)
        return (out,), ()

    (dproj,) = _tok_call(
        "dproj_assemble", assemble, t, TT,
        [(dqkv, QKV_W, 0, "cur"), (dqkv_h, QKV_W, 0, "next8"), (dz, DN_WIDTH, 0, "cur"), (dgb, SC_WIDTH, 0, "cur"),
         (dgc, SC_WIDTH, 0, "cur"), (dgc_h, SC_WIDTH, 0, "next8"), (dhv, SC_WIDTH, 0, "cur"),
         (dhv_h, SC_WIDTH, 0, "next8"), (dba, LANES, 0, "cur")], [],
        [(TT, PROJ_W, MXU_DTYPE)], [])
    dx_in, d_norm1, got["gu"] = _matmul_norm_bwd(dproj, p["w_in"], s["x"], p["norm1_g"], dx_mid,
                                                 "proj_dx_scatter", ride=ride_gu(d_w_gu))
    d_w_in, got["down"] = _matmul(s["h"], dproj, "tn", MXU_DTYPE, "proj_dw_scatter", ride=ride_down(d_w_down))
    grads = dict(w_in=d_w_in, w_out=d_w_out, w_gu=d_w_gu, w_down=d_w_down, norm1_g=d_norm1, norm2_g=d_norm2,
                 dn_norm_g=d_dn_norm, sc_norm_g=d_sc_norm, sc_conv_w=d_sc_conv, dn_conv_w=d_dn_conv,
                 alog=d_alog, dt=d_dt)
    return dx_in, grads, got


def _final_loss(x, g, target):
    t, d = x.shape

    def fn(tv, cv):
        xv, tg = tv

        def loss_fn(xx, gg):
            err = jnp.square(_rms_norm(xx, gg) - tg)
            return 0.5 * jnp.sum(jnp.mean(err, axis=-1))

        loss, vjp = jax.vjp(loss_fn, xv, cv[0])
        dx, dg = vjp(jnp.ones((), F32))
        return (dx,), (jnp.full((1, LANES), loss, F32), dg)

    return _tok_call("final_loss", fn, t, TT, [(x, d, 0, "cur"), (target, d, 0, "cur")], [g],
                     [(TT, d, F32)], [((1, LANES), F32), ((1, d), F32)])


def _pack_rows(arrs):
    rows, offs, r0 = [], [], 0
    for a in arrs:
        n = a.size
        nr = -(-n // LANES)
        flat = jnp.pad(a.reshape(-1).astype(F32), (0, nr * LANES - n))
        rows.append(flat.reshape(nr, LANES))
        offs.append((r0, nr, a.shape))
        r0 += nr
    pad = (-r0) % 8
    if pad:
        rows.append(jnp.zeros((pad, LANES), F32))
    return jnp.concatenate(rows, axis=0), offs


def _unpack_rows(packed, offs):
    out = []
    for r0, nr, shp in offs:
        n = 1
        for s_ in shp:
            n *= s_
        out.append(packed[r0:r0 + nr].reshape(-1)[:n].reshape(shp))
    return out


def kernel(x, norm1_g, w_in, dn_conv_w, dn_a_log, dn_dt_bias, dn_norm_g, sc_conv_w, sc_norm_g, w_out, norm2_g, ffn_w_gate, ffn_w_up, ffn_w_down, final_norm_g, loss_target, m_norm1_g, m_w_in, m_dn_conv_w, m_dn_a_log, m_dn_dt_bias, m_dn_norm_g, m_sc_conv_w, m_sc_norm_g, m_w_out, m_norm2_g, m_ffn_w_gate, m_ffn_w_up, m_ffn_w_down, m_final_norm_g, v_norm1_g, v_w_in, v_dn_conv_w, v_dn_a_log, v_dn_dt_bias, v_dn_norm_g, v_sc_conv_w, v_sc_norm_g, v_w_out, v_norm2_g, v_ffn_w_gate, v_ffn_w_up, v_ffn_w_down, v_final_norm_g):
    depth, d, cin = w_in.shape
    t = x.shape[1]
    dff_s = ffn_w_gate.shape[2]
    tr = lambda a: a.transpose(0, 2, 1)
    gate_t, up_t = tr(ffn_w_gate), tr(ffn_w_up)
    me = _my_index()
    x2 = x.reshape(t, d)
    tgt = loss_target.reshape(t, d)

    conv_pack, conv_offs = _pack_rows([dn_conv_w, sc_conv_w])
    (conv_all,) = _all_gather([conv_pack], "gather_conv")
    dn_parts, sc_parts = zip(*[_unpack_rows(conv_all[j], conv_offs) for j in range(N_DEV)])
    dn_conv_full = jnp.concatenate(dn_parts, axis=2)
    sc_conv_full = jnp.concatenate(sc_parts, axis=2)

    def shards(l):
        return [a[l].astype(MXU_DTYPE) for a in (w_in, w_out, gate_t, up_t, ffn_w_down)]

    def mixer_params(l, g_in, g_out):
        full_in = g_in.transpose(1, 0, 2).reshape(d, N_DEV * cin)
        return dict(
            w_in=_pad_w_in(full_in), w_out=g_out.reshape(d, d),
            norm1_g=norm1_g[l][None], norm2_g=norm2_g[l][None], dn_norm_g=dn_norm_g[l][None],
            sc_norm_g=sc_norm_g[l][None], dn_conv_w=dn_conv_full[l], sc_conv_w=sc_conv_full[l],
            alog_row=_lane_row(dn_a_log[l], DN_HEADS), dt_row=_lane_row(dn_dt_bias[l], DN_HEADS))

    def ffn_params(g_gate, g_up, g_down):
        dff = N_DEV * dff_s
        return dict(w_gu=_interleave_gu(g_gate.reshape(dff, d), g_up.reshape(dff, d)), w_down=g_down.reshape(dff, d))

    nxt = mixer_params(0, *_all_gather(shards(0)[:2], "gather_first"))
    nxt_ffn = None
    params, saved = [], []
    xc = x2
    for l in range(depth):
        last = l + 1 == depth
        own_ffn = shards(l)[2:] if nxt_ffn is None else []
        ahead = [] if last else shards(l + 1)
        ride_proj = _Exchange("gather", own_ffn + ahead[:2]) if own_ffn or ahead else None
        ride_ffn = None if last else _Exchange("gather", ahead[2:])

        def late(got, own_ffn=own_ffn, nxt_ffn=nxt_ffn):
            return ffn_params(*got[:3]) if own_ffn else nxt_ffn

        xc, s, p_l, got_proj, got_ffn = _layer_fwd(xc, nxt, ride_proj, ride_ffn, late)
        params.append(p_l)
        saved.append(s)
        if not last:
            nxt = mixer_params(l + 1, *got_proj[len(own_ffn):])
            nxt_ffn = ffn_params(*got_ffn)
    dx, loss_part, d_final = _final_loss(xc, final_norm_g[None], tgt)

    names = ("w_in", "w_out", "ffn_w_gate", "ffn_w_up", "ffn_w_down")
    big_out = {n: {k: [None] * depth for k in ("g", "d", "m", "v")} for n in names}
    w_loc = dict(w_in=w_in, w_out=w_out, ffn_w_gate=gate_t, ffn_w_up=up_t, ffn_w_down=ffn_w_down)
    m_loc = dict(w_in=m_w_in, w_out=m_w_out, ffn_w_gate=tr(m_ffn_w_gate), ffn_w_up=tr(m_ffn_w_up), ffn_w_down=m_ffn_w_down)
    v_loc = dict(w_in=v_w_in, w_out=v_w_out, ffn_w_gate=tr(v_ffn_w_gate), ffn_w_up=tr(v_ffn_w_up), ffn_w_down=v_ffn_w_down)

    cols = lambda a, c: a.reshape(a.shape[0], N_DEV, c).transpose(1, 0, 2).astype(MXU_DTYPE)

    def ride_gu(d_w_gu):
        rows = lambda a: a.reshape(N_DEV, dff_s, d).astype(MXU_DTYPE)
        g_gate, g_up = _split_gu(d_w_gu)
        return _Exchange("scatter", [rows(g_gate), rows(g_up)])

    def ride_down(d_w_down):
        return _Exchange("scatter", [d_w_down.reshape(N_DEV, dff_s, d).astype(MXU_DTYPE)])

    def apply(l, which, recv):
        for n, r in zip(which, recv):
            res = _sum_adamw(r, w_loc[n][l], m_loc[n][l], v_loc[n][l], "adamw_" + n)
            for k, a in zip(("g", "d", "m", "v"), res):
                big_out[n][k][l] = a

    grads = [None] * depth
    pending = None
    for l in reversed(range(depth)):
        ride_prev = _Exchange("scatter", pending) if pending else None
        dx, grads[l], got = _layer_bwd(dx, params[l], saved[l], ride_prev, ride_gu, ride_down)
        if ride_prev:
            apply(l + 1, names[:2], got["prev"])
        apply(l, names[2:], [*got["gu"], *got["down"]])
        pending = [cols(_unpad_w_in(grads[l]["w_in"]), cin),
                   grads[l]["w_out"].reshape(N_DEV, d // N_DEV, d).astype(MXU_DTYPE)]
    apply(0, names[:2], _Exchange("scatter", pending).run("scatter_last"))
    grad_x = dx.reshape(x.shape)
    big_out = {n: {k: jnp.stack(v_) for k, v_ in o.items()} for n, o in big_out.items()}
    for n in ("ffn_w_gate", "ffn_w_up"):
        big_out[n] = {k: tr(a) for k, a in big_out[n].items()}

    stack = lambda key: jnp.stack([grads[l][key] for l in range(depth)])
    small_parts = [stack("norm1_g").reshape(depth, d), stack("norm2_g").reshape(depth, d), d_final.reshape(d),
                   stack("dn_norm_g").reshape(depth, HEAD_DIM), stack("sc_norm_g").reshape(depth, SC_WIDTH),
                   stack("alog").reshape(depth, LANES), stack("dt").reshape(depth, LANES),
                   stack("dn_conv_w"), stack("sc_conv_w"), loss_part]
    small_pack, small_offs = _pack_rows(small_parts)
    (small_all,) = _all_gather([small_pack], "gather_small")
    total = _sum_rows(small_all, "sum_small")
    (g_n1, g_n2, g_fin, g_dnn, g_scn, g_alog, g_dt, g_dnc, g_scc, loss_row) = _unpack_rows(total, small_offs)
    loss = loss_row[0, 0]
    g_alog = g_alog[:, DN_HEADS:2 * DN_HEADS]
    g_dt = g_dt[:, DN_HEADS:2 * DN_HEADS]
    dnc_w = dn_conv_w.shape[2]
    scc_w = sc_conv_w.shape[2]
    g_dnc = lax.dynamic_slice_in_dim(g_dnc, me * dnc_w, dnc_w, axis=2)
    g_scc = lax.dynamic_slice_in_dim(g_scc, me * scc_w, scc_w, axis=2)
    sm_g = [g_n1, g_dnc, g_alog, g_dt, g_dnn, g_scc, g_scn, g_n2, g_fin]
    sm_w = [norm1_g, dn_conv_w, dn_a_log, dn_dt_bias, dn_norm_g, sc_conv_w, sc_norm_g, norm2_g, final_norm_g]
    sm_m = [m_norm1_g, m_dn_conv_w, m_dn_a_log, m_dn_dt_bias, m_dn_norm_g, m_sc_conv_w, m_sc_norm_g, m_norm2_g, m_final_norm_g]
    sm_v = [v_norm1_g, v_dn_conv_w, v_dn_a_log, v_dn_dt_bias, v_dn_norm_g, v_sc_conv_w, v_sc_norm_g, v_norm2_g, v_final_norm_g]
    pg, offs = _pack_rows(sm_g)
    pw, _ = _pack_rows(sm_w)
    pm, _ = _pack_rows(sm_m)
    pv, _ = _pack_rows(sm_v)
    sg, sd, sm_, sv = _sum_adamw(pg[None], pw, pm, pv, "adamw_small")
    small_out = {k: _unpack_rows(a, offs) for k, a in zip(("g", "d", "m", "v"), (sg, sd, sm_, sv))}

    def outputs(k):
        s_ = small_out[k]
        b = big_out
        return [s_[0], b["w_in"][k], s_[1], s_[2], s_[3], s_[4], s_[5], s_[6], b["w_out"][k], s_[7],
                b["ffn_w_gate"][k], b["ffn_w_up"][k], b["ffn_w_down"][k], s_[8]]

    return (loss, grad_x, *outputs("g"), *outputs("d"), *outputs("m"), *outputs("v"))
```

```python
import functools

import jax
import jax.numpy as jnp
from jax import lax
from jax.experimental import pallas as pl
from jax.experimental.pallas import tpu as pltpu

F32 = jnp.float32
MXU_DTYPE = jnp.bfloat16
MESH = pl.DeviceIdType.MESH

N_DEV = 8
EPS = 1e-6
DN_HEADS = 4
HEAD_DIM = 128
DN_WIDTH = DN_HEADS * HEAD_DIM
SC_WIDTH = 512
SC_GROUPS = 4
DN_CONV = 4
SC_CONV = 3
CHUNK = 64
HALO = 8
LANES = 128

QKV_W = 3 * DN_WIDTH
Z_OFF = QKV_W
SC_OFF = Z_OFF + DN_WIDTH
BA_OFF = SC_OFF + 3 * SC_WIDTH
BA_W = 256
PROJ_W = BA_OFF + BA_W

ADAM_LR = 0.001
ADAM_B1 = 0.9
ADAM_B2 = 0.999
ADAM_EPS = 1e-08
ADAM_WD = 0.01
ADAM_STEP = 10


def _pick(n, cands):
    for c in cands:
        if n % c == 0:
            return c
    return n


def _params(*sem):
    return pltpu.CompilerParams(dimension_semantics=sem)


def _rms_norm(x, g):
    return x * lax.rsqrt(jnp.mean(x * x, axis=-1, keepdims=True) + EPS) * g


def _dot(a, b, dims=(((1,), (0,)), ((), ()))):
    return lax.dot_general(a.astype(MXU_DTYPE), b.astype(MXU_DTYPE), dims, preferred_element_type=F32)


def _split_terms(x, terms):
    out = []
    for _ in range(terms):
        hi = x.astype(MXU_DTYPE)
        out.append(hi)
        x = x - hi.astype(F32)
    return out


def _ein_impl(spec, terms, a, b):
    ta, tb = terms
    if ta == 1 and tb == 1:
        return jnp.einsum(spec, a.astype(MXU_DTYPE), b.astype(MXU_DTYPE), preferred_element_type=F32)
    pa, pb = _split_terms(a, ta), _split_terms(b, tb)
    order = max(ta, tb) - 1
    acc = None
    for deg in range(order, -1, -1):
        for i in range(ta):
            j = deg - i
            if 0 <= j < tb:
                t = jnp.einsum(spec, pa[i], pb[j], preferred_element_type=F32)
                acc = t if acc is None else acc + t
    return acc


@functools.partial(jax.custom_vjp, nondiff_argnums=(0, 1))
def _ein(spec, terms, a, b):
    return _ein_impl(spec, terms, a, b)


def _ein_fwd(spec, terms, a, b):
    return _ein_impl(spec, terms, a, b), (a, b)


def _ein_bwd(spec, terms, res, ct):
    a, b = res
    xy, z = spec.split("->")
    x, y = xy.split(",")
    tc = min(max(terms), 2)
    da = _ein_impl(f"{z},{y}->{x}", (tc, terms[1]), ct, b)
    db = _ein_impl(f"{x},{z}->{y}", (terms[0], tc), a, ct)
    return da, db


_ein.defvjp(_ein_fwd, _ein_bwd)

FAST = (1, 1)
PRECISE = (2, 2)
LHS_EXACT = (1, 3)


def _causal_conv(cur, halo, w, k):
    tt = cur.shape[0]
    xp = jnp.concatenate([halo, cur], axis=0)
    y = None
    for j in range(k):
        start = HALO - (k - 1) + j
        term = xp[start:start + tt] * w[j:j + 1]
        y = term if y is None else y + term
    return y


def _matmul(a, b, mode, out_dtype, name, residual=None, ride=None):
    if mode == "nn":
        (m, k), (k2, n) = a.shape, b.shape
    elif mode == "nt":
        (m, k), (n, k2) = a.shape, b.shape
    else:
        (k, m), (k2, n) = a.shape, b.shape
    assert k == k2
    tm = _pick(m, (1024, 1408, 512, 256, 128))
    tn = _pick(n, (1280, 1408, 1024, 512, 256, 128))
    tk = k if k <= 2816 else _pick(k, (1024, 768, 512))
    gi, gj, nk = m // tm, n // tn, k // tk
    dims = {"nn": (((1,), (0,)), ((), ())), "nt": (((1,), (1,)), ((), ())), "tn": (((0,), (0,)), ((), ()))}[mode]
    a_spec = {"nn": pl.BlockSpec((tm, tk), lambda i, j, q: (i, q)),
              "nt": pl.BlockSpec((tm, tk), lambda i, j, q: (i, q)),
              "tn": pl.BlockSpec((tk, tm), lambda i, j, q: (q, i))}[mode]
    b_spec = {"nn": pl.BlockSpec((tk, tn), lambda i, j, q: (q, j)),
              "nt": pl.BlockSpec((tn, tk), lambda i, j, q: (j, q)),
              "tn": pl.BlockSpec((tk, tn), lambda i, j, q: (q, j))}[mode]
    o_spec = pl.BlockSpec((tm, tn), lambda i, j, q: (i, j))
    has_res = residual is not None
    n_in = 3 if has_res else 2
    nr = ride.na if ride else 0

    def body(*refs):
        a_ref, b_ref = refs[0], refs[1]
        r_ref = refs[2] if has_res else None
        ride_in = refs[n_in:n_in + nr]
        o_ref = refs[n_in + nr]
        ride_out = refs[n_in + nr + 1:n_in + 2 * nr + 1]
        acc = refs[n_in + 2 * nr + 1]
        ride_sems = refs[n_in + 2 * nr + 2:]
        i, j, q = pl.program_id(0), pl.program_id(1), pl.program_id(2)
        if ride:
            @pl.when((i == 0) & (j == 0) & (q == 0))
            def _():
                ride.start(ride_in, ride_out, ride_sems)

        @pl.when(q == 0)
        def _():
            acc[...] = jnp.zeros_like(acc)

        acc[...] += _dot(a_ref[...], b_ref[...], dims)

        @pl.when(q == nk - 1)
        def _():
            r = acc[...]
            if has_res:
                r = r + r_ref[...]
            o_ref[...] = r.astype(o_ref.dtype)

        if ride:
            @pl.when((i == gi - 1) & (j == gj - 1) & (q == nk - 1))
            def _():
                ride.finish(ride_in, ride_out, ride_sems)

    hbm = pl.BlockSpec(memory_space=pltpu.HBM)
    in_specs = [a_spec, b_spec] + ([o_spec] if has_res else []) + [hbm] * nr
    args = (a, b) + ((residual,) if has_res else ()) + (tuple(ride.arrays) if ride else ())
    res = pl.pallas_call(
        body, name=name, grid=(gi, gj, nk), in_specs=in_specs, out_specs=[o_spec] + [hbm] * nr,
        out_shape=[jax.ShapeDtypeStruct((m, n), out_dtype)] + (ride.out_shape if ride else []),
        scratch_shapes=[pltpu.VMEM((tm, tn), F32)] + (ride.scratch if ride else []),
        compiler_params=pltpu.CompilerParams(
            dimension_semantics=("arbitrary",) * 3 if ride else ("parallel", "parallel", "arbitrary"),
            has_side_effects=bool(ride)),
    )(*args)
    return (res[0], res[1:]) if ride else res[0]


def _norm_matmul(x, g, w, name, ride=None):
    t, d = x.shape
    n = w.shape[0]
    tm = _pick(t, (1024, 512, 256, 128))
    tn = _pick(n, (1280, 1408, 1024, 512, 256, 128))
    gi, gj = t // tm, n // tn
    nr = ride.na if ride else 0

    def body(*refs):
        x_ref, g_ref, w_ref = refs[:3]
        ride_in = refs[3:3 + nr]
        h_ref, y_ref = refs[3 + nr], refs[4 + nr]
        ride_out = refs[5 + nr:5 + 2 * nr]
        h_scr = refs[5 + 2 * nr]
        ride_sems = refs[6 + 2 * nr:]
        i, j = pl.program_id(0), pl.program_id(1)
        if ride:
            @pl.when((i == 0) & (j == 0))
            def _():
                ride.start(ride_in, ride_out, ride_sems)

        @pl.when(j == 0)
        def _():
            h = _rms_norm(x_ref[...], g_ref[...]).astype(MXU_DTYPE)
            h_scr[...] = h
            h_ref[...] = h

        y_ref[...] = _dot(h_scr[...], w_ref[...], (((1,), (1,)), ((), ())))

        if ride:
            @pl.when((i == gi - 1) & (j == gj - 1))
            def _():
                ride.finish(ride_in, ride_out, ride_sems)

    hbm = pl.BlockSpec(memory_space=pltpu.HBM)
    res = pl.pallas_call(
        body, name=name, grid=(gi, gj),
        in_specs=[pl.BlockSpec((tm, d), lambda i, j: (i, 0)), pl.BlockSpec((1, d), lambda i, j: (0, 0)),
                  pl.BlockSpec((tn, d), lambda i, j: (j, 0))] + [hbm] * nr,
        out_specs=[pl.BlockSpec((tm, d), lambda i, j: (i, 0)), pl.BlockSpec((tm, tn), lambda i, j: (i, j))] + [hbm] * nr,
        out_shape=[jax.ShapeDtypeStruct((t, d), MXU_DTYPE), jax.ShapeDtypeStruct((t, n), F32)]
        + (ride.out_shape if ride else []),
        scratch_shapes=[pltpu.VMEM((tm, d), MXU_DTYPE)] + (ride.scratch if ride else []),
        compiler_params=pltpu.CompilerParams(
            dimension_semantics=("arbitrary",) * 2 if ride else ("parallel", "arbitrary"),
            has_side_effects=bool(ride)),
    )(x, g, w, *(ride.arrays if ride else ()))
    return (res[0], res[1], res[2:]) if ride else (res[0], res[1])


def _swiglu_math(g, u):
    return jax.nn.silu(g) * u


def _gu_tile(dff):
    return _pick(dff, (1408, 1024, 512, 256, 128))


def _interleave_gu(gate_t, up_t):
    tn = _gu_tile(gate_t.shape[0])
    pieces = []
    for j in range(gate_t.shape[0] // tn):
        pieces += [gate_t[j * tn:(j + 1) * tn], up_t[j * tn:(j + 1) * tn]]
    return jnp.concatenate(pieces, axis=0)


def _split_gu(gu_t):
    dff = gu_t.shape[0] // 2
    tn = _gu_tile(dff)
    tiles = [gu_t[j * tn:(j + 1) * tn] for j in range(2 * dff // tn)]
    return jnp.concatenate(tiles[0::2], axis=0), jnp.concatenate(tiles[1::2], axis=0)


def _ffn_up_swiglu(x, g, w_gu, name, ride=None):
    t, d = x.shape
    dff = w_gu.shape[0] // 2
    tn = _gu_tile(dff)
    tm = _pick(t, (512, 256, 128))
    gj, gi = dff // tn, t // tm
    nr = ride.na if ride else 0

    def body(*refs):
        x_ref, g_ref, w_ref = refs[:3]
        ride_in = refs[3:3 + nr]
        h_ref, gu_ref, act_ref = refs[3 + nr:6 + nr]
        ride_out = refs[6 + nr:6 + 2 * nr]
        ride_sems = refs[6 + 2 * nr:]
        j, i = pl.program_id(0), pl.program_id(1)
        if ride:
            @pl.when((i == 0) & (j == 0))
            def _():
                ride.start(ride_in, ride_out, ride_sems)

        h = _rms_norm(x_ref[...], g_ref[...]).astype(MXU_DTYPE)

        @pl.when(j == 0)
        def _():
            h_ref[...] = h

        y = _dot(h, w_ref[...], (((1,), (1,)), ((), ())))
        gu_ref[...] = y.astype(gu_ref.dtype)
        act_ref[...] = _swiglu_math(y[:, :tn], y[:, tn:]).astype(act_ref.dtype)

        if ride:
            @pl.when((i == gi - 1) & (j == gj - 1))
            def _():
                ride.finish(ride_in, ride_out, ride_sems)

    hbm = pl.BlockSpec(memory_space=pltpu.HBM)
    res = pl.pallas_call(
        body, name=name, grid=(gj, gi),
        in_specs=[pl.BlockSpec((tm, d), lambda j, i: (i, 0)), pl.BlockSpec((1, d), lambda j, i: (0, 0)),
                  pl.BlockSpec((2 * tn, d), lambda j, i: (j, 0))] + [hbm] * nr,
        out_specs=[pl.BlockSpec((tm, d), lambda j, i: (jnp.where(j == 0, i, gi - 1), 0)),
                   pl.BlockSpec((tm, 2 * tn), lambda j, i: (i, j)),
                   pl.BlockSpec((tm, tn), lambda j, i: (i, j))] + [hbm] * nr,
        out_shape=[jax.ShapeDtypeStruct((t, d), MXU_DTYPE), jax.ShapeDtypeStruct((t, 2 * dff), MXU_DTYPE),
                   jax.ShapeDtypeStruct((t, dff), MXU_DTYPE)] + (ride.out_shape if ride else []),
        scratch_shapes=(ride.scratch if ride else []),
        compiler_params=pltpu.CompilerParams(dimension_semantics=("arbitrary", "arbitrary"),
                                             has_side_effects=bool(ride)),
    )(x, g, w_gu, *(ride.arrays if ride else ()))
    return (res[0], res[1], res[2], res[3:]) if ride else tuple(res)


def _ffn_down_dx_swiglu(dx_out, w_down, gu, name):
    t, d = dx_out.shape
    dff = w_down.shape[0]
    tn = _gu_tile(dff)
    tm = _pick(t, (512, 256, 128))

    def body(dx_ref, w_ref, gu_ref, o_ref):
        dact = _dot(dx_ref[...], w_ref[...], (((1,), (1,)), ((), ())))
        gu_v = gu_ref[...].astype(F32)
        _, vjp = jax.vjp(_swiglu_math, gu_v[:, :tn], gu_v[:, tn:])
        dg, du = vjp(dact)
        o_ref[...] = jnp.concatenate([dg, du], axis=1).astype(o_ref.dtype)

    return pl.pallas_call(
        body, name=name, grid=(dff // tn, t // tm),
        in_specs=[pl.BlockSpec((tm, d), lambda j, i: (i, 0)), pl.BlockSpec((tn, d), lambda j, i: (j, 0)),
                  pl.BlockSpec((tm, 2 * tn), lambda j, i: (i, j))],
        out_specs=pl.BlockSpec((tm, 2 * tn), lambda j, i: (i, j)),
        out_shape=jax.ShapeDtypeStruct((t, 2 * dff), MXU_DTYPE),
        compiler_params=_params("parallel", "parallel"),
    )(dx_out, w_down, gu)


def _matmul_norm_bwd(dy, w, x, g, dres, name, ride=None, w_rows_k=False):
    t, k = dy.shape
    d = w.shape[1] if w_rows_k else w.shape[0]
    tm = _pick(t, (1024, 512, 256, 128))
    tk = k if k <= 2816 else _pick(k, (1408, 1280, 1024, 768, 512))
    tr = _pick(tm, (256, 128))
    gi, nk = t // tm, k // tk
    nr = ride.na if ride else 0

    def body(*refs):
        dy_ref, w_ref, x_ref, g_ref, dres_ref = refs[:5]
        ride_in = refs[5:5 + nr]
        dx_ref, dg_ref = refs[5 + nr], refs[6 + nr]
        ride_out = refs[7 + nr:7 + 2 * nr]
        acc = refs[7 + 2 * nr]
        ride_sems = refs[8 + 2 * nr:]
        i, q = pl.program_id(0), pl.program_id(1)

        @pl.when((i == 0) & (q == 0))
        def _():
            dg_ref[...] = jnp.zeros_like(dg_ref)
            if ride:
                ride.start(ride_in, ride_out, ride_sems)

        @pl.when(q == 0)
        def _():
            acc[...] = jnp.zeros_like(acc)

        acc[...] += _dot(dy_ref[...], w_ref[...], (((1,), (0 if w_rows_k else 1,)), ((), ())))

        @pl.when(q == nk - 1)
        def _():
            for r in range(tm // tr):
                rows = slice(r * tr, (r + 1) * tr)
                _, vjp = jax.vjp(_rms_norm, x_ref[rows], g_ref[...])
                dxn, dg = vjp(acc[rows])
                dx_ref[rows] = dres_ref[rows] + dxn
                dg_ref[...] += dg

        if ride:
            @pl.when((i == gi - 1) & (q == nk - 1))
            def _():
                ride.finish(ride_in, ride_out, ride_sems)

    hbm = pl.BlockSpec(memory_space=pltpu.HBM)
    row = pl.BlockSpec((tm, d), lambda i, q: (i, 0))
    res = pl.pallas_call(
        body, name=name, grid=(gi, nk),
        in_specs=[pl.BlockSpec((tm, tk), lambda i, q: (i, q)),
                  pl.BlockSpec((tk, d), lambda i, q: (q, 0)) if w_rows_k else pl.BlockSpec((d, tk), lambda i, q: (0, q)),
                  row,
                  pl.BlockSpec((1, d), lambda i, q: (0, 0)), row] + [hbm] * nr,
        out_specs=[row, pl.BlockSpec((1, d), lambda i, q: (0, 0))] + [hbm] * nr,
        out_shape=[jax.ShapeDtypeStruct((t, d), F32), jax.ShapeDtypeStruct((1, d), F32)]
        + (ride.out_shape if ride else []),
        scratch_shapes=[pltpu.VMEM((tm, d), F32)] + (ride.scratch if ride else []),
        compiler_params=pltpu.CompilerParams(dimension_semantics=("arbitrary", "arbitrary"),
                                             has_side_effects=bool(ride)),
    )(dy, w, x, g, dres, *(ride.arrays if ride else ()))
    return (res[0], res[1], res[2:]) if ride else (res[0], res[1])


def _tok_call(name, fn, t, tt, tok_in, const_in, tok_out, acc_out):
    nblk = t // tt
    hb = tt // HALO
    in_specs, args = [], []
    for arr, w, cb, mode in tok_in:
        if mode == "cur":
            spec = pl.BlockSpec((tt, w), lambda i, cb=cb: (i, cb))
        elif mode == "prev":
            spec = pl.BlockSpec((HALO, w), lambda i, cb=cb: (jnp.maximum(i * hb - 1, 0), cb))
        else:
            spec = pl.BlockSpec((HALO, w), lambda i, cb=cb: (jnp.minimum(i + 1, nblk - 1), cb))
        in_specs.append(spec)
        args.append(arr)
    for arr in const_in:
        in_specs.append(pl.BlockSpec(arr.shape, lambda i: (0, 0)))
        args.append(arr)
    out_specs, out_shape = [], []
    for rows, w, dt in tok_out:
        out_specs.append(pl.BlockSpec((rows, w), lambda i: (i, 0)))
        out_shape.append(jax.ShapeDtypeStruct((nblk * rows, w), dt))
    for shp, dt in acc_out:
        out_specs.append(pl.BlockSpec(shp, lambda i: (0, 0)))
        out_shape.append(jax.ShapeDtypeStruct(shp, dt))
    n_tok, n_const, n_out = len(tok_in), len(const_in), len(tok_out)

    def body(*refs):
        i = pl.program_id(0)
        tok_vals = []
        for (_, _, _, mode), r in zip(tok_in, refs[:n_tok]):
            v = r[...]
            if mode == "prev":
                v = jnp.where(i > 0, v, jnp.zeros_like(v))
            elif mode == "next8":
                v = jnp.where(i < nblk - 1, v, jnp.zeros_like(v))
            tok_vals.append(v)
        const_vals = [r[...] for r in refs[n_tok:n_tok + n_const]]
        outs, accs = fn(tok_vals, const_vals)
        o_refs = refs[n_tok + n_const:n_tok + n_const + n_out]
        a_refs = refs[n_tok + n_const + n_out:]
        for r, v in zip(o_refs, outs):
            r[...] = v.astype(r.dtype)
        if a_refs:
            @pl.when(i == 0)
            def _():
                for r in a_refs:
                    r[...] = jnp.zeros_like(r)

            for r, v in zip(a_refs, accs):
                r[...] += v.astype(r.dtype)

    res = pl.pallas_call(
        body, name=name, grid=(nblk,), in_specs=in_specs, out_specs=out_specs, out_shape=out_shape,
        compiler_params=_params("arbitrary" if acc_out else "parallel"),
    )(*args)
    return res


def _dn_pre_math(cur, halo, ba, cw, alog, dtb):
    tt = cur.shape[0]
    a = jax.nn.silu(_causal_conv(cur, halo, cw, DN_CONV))
    pieces = []
    for p in range(2 * DN_HEADS):
        xh = a[:, p * HEAD_DIM:(p + 1) * HEAD_DIM]
        xh = xh * lax.rsqrt(jnp.sum(xh * xh, axis=-1, keepdims=True) + EPS)
        if p < DN_HEADS:
            xh = xh * (HEAD_DIM ** -0.5)
        pieces.append(xh)
    pieces.append(a[:, 2 * DN_WIDTH:])
    qkvn = jnp.concatenate(pieces, axis=1)
    lane = lax.broadcasted_iota(jnp.int32, ba.shape, 1)
    raw = jnp.where(lane < DN_HEADS, jax.nn.sigmoid(ba), -jnp.exp(alog) * jax.nn.softplus(ba + dtb))
    r = lax.broadcasted_iota(jnp.int32, (tt, tt), 0)
    c = lax.broadcasted_iota(jnp.int32, (tt, tt), 1)
    tri = jnp.where((r // CHUNK == c // CHUNK) & (c <= r), 1.0, 0.0).astype(F32)
    cums = _ein("ij,jk->ik", LHS_EXACT, tri, raw)
    bg = jnp.where(lane < DN_HEADS, raw, cums)
    return qkvn, bg


def _mix_math(o, z, gb, gc, gc_halo, hv, hv_halo, dng, scg, scw):
    outs = []
    for h in range(DN_HEADS):
        sl = slice(h * HEAD_DIM, (h + 1) * HEAD_DIM)
        oh = o[:, sl]
        outs.append(oh * lax.rsqrt(jnp.mean(oh * oh, axis=-1, keepdims=True) + EPS) * dng * jax.nn.silu(z[:, sl]))
    y = gb * _causal_conv(gc * hv, gc_halo * hv_halo, scw, SC_CONV)
    gw = SC_WIDTH // SC_GROUPS
    for g in range(SC_GROUPS):
        sl = slice(g * gw, (g + 1) * gw)
        yg = y[:, sl]
        outs.append(yg * lax.rsqrt(jnp.mean(yg * yg, axis=-1, keepdims=True) + EPS) * scg[:, sl])
    return jnp.concatenate(outs, axis=1)


def _tri_inverse(a):
    c = a.shape[-1]
    r = lax.broadcasted_iota(jnp.int32, (c, c), 0)
    q = lax.broadcasted_iota(jnp.int32, (c, c), 1)
    eye = jnp.where(r == q, 1.0, 0.0).astype(F32)[None]
    blk = (r // 16 == q // 16)[None]
    d = jnp.where(blk, a, 0.0)
    o = a - d
    mm = functools.partial(_ein, "bij,bjk->bik", PRECISE)
    p = eye - d
    n = mm(d, d)
    for _ in range(2):
        both = mm(jnp.concatenate([n, p], axis=1), n)
        n = both[:, :c]
        p = p + both[:, c:]
    p = p + mm(p, n)
    e = mm(p, o)
    e2 = mm(e, e)
    left = eye - e + e2 - mm(e, e2)
    return mm(left, p)


@jax.custom_vjp
def _inverse_known(a, tinv):
    return tinv


def _inverse_known_fwd(a, tinv):
    return tinv, tinv


def _inverse_known_bwd(tinv, ct):
    left = _ein("bji,bjk->bik", PRECISE, tinv, ct)
    return -_ein("bik,bjk->bij", PRECISE, left, tinv), jnp.zeros_like(tinv)


_inverse_known.defvjp(_inverse_known_fwd, _inverse_known_bwd)


def _delta_intra_math(q, k, v, bg, head, tinv_known=None):
    n = q.shape[0]
    nb = n // CHUNK
    lane = lax.broadcasted_iota(jnp.int32, bg.shape, 1)
    beta = jnp.sum(jnp.where(lane == head, bg, 0.0), axis=1, keepdims=True).reshape(nb, CHUNK, 1)
    gc = jnp.sum(jnp.where(lane == head + DN_HEADS, bg, 0.0), axis=1, keepdims=True).reshape(nb, CHUNK, 1)
    q3, k3, v3 = (a.reshape(nb, CHUNK, HEAD_DIM) for a in (q, k, v))
    r = lax.broadcasted_iota(jnp.int32, (CHUNK, CHUNK), 0)
    c = lax.broadcasted_iota(jnp.int32, (CHUNK, CHUNK), 1)
    eye = jnp.where(r == c, 1.0, 0.0).astype(F32)[None]
    gcr = _ein("bik,bkj->bij", LHS_EXACT, jnp.ones((nb, CHUNK, CHUNK), F32), gc * eye)
    decay = jnp.exp(jnp.where((r >= c)[None], gc - gcr, -1e30))
    kb = k3 * beta
    vb = v3 * beta
    egc = jnp.exp(gc)
    on_k = _ein("bcd,bmd->bcm", FAST, jnp.concatenate([kb, q3], axis=1), k3)
    a = jnp.where((r > c)[None], on_k[:, :CHUNK] * decay, 0.0)
    tinv = _tri_inverse(a) if tinv_known is None else _inverse_known(a, tinv_known)
    uw = _ein("bcm,bmd->bcd", PRECISE, tinv, jnp.concatenate([vb, kb * egc], axis=2))
    u, w = uw[:, :, :HEAD_DIM], uw[:, :, HEAD_DIM:]
    qk = on_k[:, CHUNK:] * decay
    row = lax.broadcasted_iota(jnp.int32, (nb, CHUNK, 1), 1)
    glast = jnp.sum(jnp.where(row == CHUNK - 1, gc, 0.0), axis=1, keepdims=True)
    qd = q3 * egc
    kd = k3 * jnp.exp(glast - gc)
    glb = jnp.broadcast_to(jnp.exp(glast), (nb, HALO, LANES))
    flat = lambda x: x.reshape(n, HEAD_DIM)
    return flat(u), flat(w), flat(qd), flat(kd), qk, glb, tinv


def _delta_step_math(u, w, qd, kd, qk, gl, s):
    c = u.shape[0]
    on_s = _ein("ck,kv->cv", FAST, jnp.concatenate([w, qd], axis=0), s)
    vnew = u - on_s[:c]
    on_v = _ein("cm,mv->cv", FAST, jnp.concatenate([qk, kd.T], axis=0), vnew)
    o = on_s[c:] + on_v[:c]
    s2 = s * gl + on_v[c:]
    return o, s2


def _delta_intra(qkvn, bg, nb):
    t = qkvn.shape[0]
    n = t // CHUNK
    rows = nb * CHUNK

    def body(q_ref, k_ref, v_ref, bg_ref, u_ref, w_ref, qd_ref, kd_ref, qk_ref, gl_ref, ti_ref):
        outs = _delta_intra_math(q_ref[...], k_ref[...], v_ref[...], bg_ref[...], pl.program_id(1))
        for r, v in zip((u_ref, w_ref, qd_ref, kd_ref, qk_ref, gl_ref, ti_ref), outs):
            r[...] = v

    col = lambda off: pl.BlockSpec((rows, HEAD_DIM), lambda b, h, off=off: (b, off + h))
    tok = jax.ShapeDtypeStruct((t, DN_WIDTH), F32)
    return pl.pallas_call(
        body, name="delta_intra", grid=(n // nb, DN_HEADS),
        in_specs=[col(0), col(DN_HEADS), col(2 * DN_HEADS), pl.BlockSpec((rows, LANES), lambda b, h: (b, 0))],
        out_specs=[col(0)] * 4 + [pl.BlockSpec((nb, None, CHUNK, CHUNK), lambda b, h: (b, h, 0, 0)),
                                  pl.BlockSpec((nb, None, HALO, LANES), lambda b, h: (b, h, 0, 0)),
                                  pl.BlockSpec((nb, None, CHUNK, CHUNK), lambda b, h: (b, h, 0, 0))],
        out_shape=[tok] * 4 + [jax.ShapeDtypeStruct((n, DN_HEADS, CHUNK, CHUNK), F32),
                               jax.ShapeDtypeStruct((n, DN_HEADS, HALO, LANES), F32),
                               jax.ShapeDtypeStruct((n, DN_HEADS, CHUNK, CHUNK), F32)],
        compiler_params=_params("parallel", "arbitrary"),
    )(qkvn, qkvn, qkvn, bg)


def _delta_intra_bwd(qkvn, bg, tinv, cts, nb):
    t = qkvn.shape[0]
    n = t // CHUNK
    rows = nb * CHUNK

    def body(q_ref, k_ref, v_ref, bg_ref, ti_ref, du, dw, dqd, dkd, dqk, dgl, dq_ref, dk_ref, dv_ref, dbg_ref):
        h = pl.program_id(1)
        ti = ti_ref[...]
        _, vjp = jax.vjp(lambda q, k, v, b: _delta_intra_math(q, k, v, b, h, ti)[:6],
                         q_ref[...], k_ref[...], v_ref[...], bg_ref[...])
        dq, dk, dv, dbg = vjp((du[...], dw[...], dqd[...], dkd[...], dqk[...], dgl[...]))
        dq_ref[...] = dq
        dk_ref[...] = dk
        dv_ref[...] = dv

        @pl.when(h == 0)
        def _():
            dbg_ref[...] = jnp.zeros_like(dbg_ref)

        dbg_ref[...] += dbg

    col = lambda off: pl.BlockSpec((rows, HEAD_DIM), lambda b, h, off=off: (b, off + h))
    bgs = pl.BlockSpec((rows, LANES), lambda b, h: (b, 0))
    qks = pl.BlockSpec((nb, None, CHUNK, CHUNK), lambda b, h: (b, h, 0, 0))
    gls = pl.BlockSpec((nb, None, HALO, LANES), lambda b, h: (b, h, 0, 0))
    tok = jax.ShapeDtypeStruct((t, DN_WIDTH), F32)
    return pl.pallas_call(
        body, name="delta_intra_bwd", grid=(n // nb, DN_HEADS),
        in_specs=[col(0), col(DN_HEADS), col(2 * DN_HEADS), bgs, qks, col(0), col(0), col(0), col(0), qks, gls],
        out_specs=[col(0), col(0), col(0), bgs],
        out_shape=[tok, tok, tok, jax.ShapeDtypeStruct((t, LANES), F32)],
        compiler_params=_params("parallel", "arbitrary"),
    )(qkvn, qkvn, qkvn, bg, tinv, *cts)


def _delta_scan(u, w, qd, kd, qk, glb, cb):
    t = u.shape[0]
    n = t // CHUNK
    rows = cb * CHUNK

    def body(u_ref, w_ref, qd_ref, kd_ref, qk_ref, gl_ref, o_ref, s_ref, s_scr):
        @pl.when(pl.program_id(0) == 0)
        def _():
            s_scr[...] = jnp.zeros_like(s_scr)

        def chunk(c, carry):
            r0 = pl.multiple_of(c * CHUNK, CHUNK)
            for h in range(DN_HEADS):
                sl = (pl.ds(r0, CHUNK), slice(h * HEAD_DIM, (h + 1) * HEAD_DIM))
                s = s_scr[h]
                s_ref[c, h] = s
                o, s2 = _delta_step_math(u_ref[sl], w_ref[sl], qd_ref[sl], kd_ref[sl], qk_ref[c, h],
                                         gl_ref[c, h][0:1, :], s)
                o_ref[sl] = o
                s_scr[h] = s2
            return carry

        lax.fori_loop(0, cb, chunk, 0)

    tok = pl.BlockSpec((rows, DN_WIDTH), lambda i: (i, 0))
    return pl.pallas_call(
        body, name="delta_scan", grid=(n // cb,),
        in_specs=[tok] * 4 + [pl.BlockSpec((cb, DN_HEADS, CHUNK, CHUNK), lambda i: (i, 0, 0, 0)),
                              pl.BlockSpec((cb, DN_HEADS, HALO, LANES), lambda i: (i, 0, 0, 0))],
        out_specs=[tok, pl.BlockSpec((cb, DN_HEADS, HEAD_DIM, HEAD_DIM), lambda i: (i, 0, 0, 0))],
        out_shape=[jax.ShapeDtypeStruct((t, DN_WIDTH), F32),
                   jax.ShapeDtypeStruct((n, DN_HEADS, HEAD_DIM, HEAD_DIM), F32)],
        scratch_shapes=[pltpu.VMEM((DN_HEADS, HEAD_DIM, HEAD_DIM), F32)],
        compiler_params=_params("arbitrary"),
    )(u, w, qd, kd, qk, glb)


def _delta_scan_bwd(u, w, qd, kd, qk, glb, s_all, do, cb):
    t = u.shape[0]
    n = t // CHUNK
    nblk = n // cb
    rows = cb * CHUNK

    def body(u_ref, w_ref, qd_ref, kd_ref, qk_ref, gl_ref, s_ref, do_ref,
             du_ref, dw_ref, dqd_ref, dkd_ref, dqk_ref, dgl_ref, ds_scr):
        @pl.when(pl.program_id(0) == 0)
        def _():
            ds_scr[...] = jnp.zeros_like(ds_scr)

        def chunk(step, carry):
            c = cb - 1 - step
            r0 = pl.multiple_of(c * CHUNK, CHUNK)
            for h in range(DN_HEADS):
                sl = (pl.ds(r0, CHUNK), slice(h * HEAD_DIM, (h + 1) * HEAD_DIM))
                gl_tile = gl_ref[c, h]
                prim = (u_ref[sl], w_ref[sl], qd_ref[sl], kd_ref[sl], qk_ref[c, h], gl_tile, s_ref[c, h])
                _, vjp = jax.vjp(lambda a, b, cc, d, e, g, s: _delta_step_math(a, b, cc, d, e, g[0:1, :], s), *prim)
                du, dw, dqd, dkd, dqk, dgl, ds = vjp((do_ref[sl], ds_scr[h]))
                du_ref[sl] = du
                dw_ref[sl] = dw
                dqd_ref[sl] = dqd
                dkd_ref[sl] = dkd
                dqk_ref[c, h] = dqk
                dgl_ref[c, h] = dgl
                ds_scr[h] = ds
            return carry

        lax.fori_loop(0, cb, chunk, 0)

    rev = lambda i: nblk - 1 - i
    tok = pl.BlockSpec((rows, DN_WIDTH), lambda i: (rev(i), 0))
    qks = pl.BlockSpec((cb, DN_HEADS, CHUNK, CHUNK), lambda i: (rev(i), 0, 0, 0))
    gls = pl.BlockSpec((cb, DN_HEADS, HALO, LANES), lambda i: (rev(i), 0, 0, 0))
    ss = pl.BlockSpec((cb, DN_HEADS, HEAD_DIM, HEAD_DIM), lambda i: (rev(i), 0, 0, 0))
    tshape = jax.ShapeDtypeStruct((t, DN_WIDTH), F32)
    return pl.pallas_call(
        body, name="delta_scan_bwd", grid=(nblk,),
        in_specs=[tok] * 4 + [qks, gls, ss, tok],
        out_specs=[tok] * 4 + [qks, gls],
        out_shape=[tshape] * 4 + [jax.ShapeDtypeStruct(qk.shape, F32), jax.ShapeDtypeStruct(glb.shape, F32)],
        scratch_shapes=[pltpu.VMEM((DN_HEADS, HEAD_DIM, HEAD_DIM), F32)],
        compiler_params=_params("arbitrary"),
    )(u, w, qd, kd, qk, glb, s_all, do)


def _peer(mask):
    x, y, c = lax.axis_index("x"), lax.axis_index("y"), lax.axis_index("c")
    return (x ^ ((mask >> 2) & 1), y ^ ((mask >> 1) & 1), c ^ (mask & 1))


def _my_index():
    return 4 * lax.axis_index("x") + 2 * lax.axis_index("y") + lax.axis_index("c")


class _Exchange:
    CHIP_MASKS = (4, 2, 6)

    def __init__(self, kind, arrays):
        self.kind = kind
        self.arrays = list(arrays)
        self.na = na = len(self.arrays)
        if kind == "gather":
            self.out_shape = [jax.ShapeDtypeStruct((N_DEV,) + a.shape, a.dtype) for a in self.arrays]
        else:
            self.out_shape = [jax.ShapeDtypeStruct(a.shape, a.dtype) for a in self.arrays]
        self.scratch = [pltpu.SemaphoreType.DMA((na, 7)), pltpu.SemaphoreType.DMA((na, 7)),
                        pltpu.SemaphoreType.DMA((na,))]

    def _copies(self, ins, outs, sems):
        send_sems, recv_sems, local_sems = sems
        me = _my_index()
        local, first, passed, arrivals = [], [], [], []
        if self.kind == "gather":
            def rc(a, k, block, to, own=False):
                def make():
                    dst = outs[a].at[block]
                    return pltpu.make_async_remote_copy(src_ref=ins[a] if own else dst, dst_ref=dst,
                                                        send_sem=send_sems.at[a, k], recv_sem=recv_sems.at[a, k],
                                                        device_id=to, device_id_type=MESH)
                return make

            sib = _peer(1)
            for a in range(self.na):
                local.append(lambda a=a: pltpu.make_async_copy(ins[a], outs[a].at[me], local_sems.at[a]))
                first.append(rc(a, 0, me, sib, own=True))
                arrivals.append(rc(a, 0, me ^ 1, _peer(0)))
                for j, m in enumerate(self.CHIP_MASKS):
                    first.append(rc(a, 1 + j, me, _peer(m), own=True))
                    passed.append((rc(a, 1 + j, me ^ m, _peer(0)), rc(a, 4 + j, me ^ m, sib)))
                    arrivals.append(rc(a, 4 + j, me ^ m ^ 1, _peer(0)))
        else:
            for a in range(self.na):
                local.append(lambda a=a: pltpu.make_async_copy(ins[a].at[me], outs[a].at[me], local_sems.at[a]))
                for m in range(1, N_DEV):
                    def make(a=a, m=m):
                        return pltpu.make_async_remote_copy(
                            src_ref=ins[a].at[me ^ m], dst_ref=outs[a].at[me], send_sem=send_sems.at[a, m - 1],
                            recv_sem=recv_sems.at[a, m - 1], device_id=_peer(m), device_id_type=MESH)
                    first.append(make)
                    arrivals.append(make)
        return local, first, passed, arrivals

    def start(self, ins, outs, sems):
        local, first, _, _ = self._copies(ins, outs, sems)
        for make in local + first:
            make().start()

    def finish(self, ins, outs, sems):
        local, first, passed, arrivals = self._copies(ins, outs, sems)
        for landed, onward in passed:
            landed().wait_recv()
            onward().start()
        for make in arrivals:
            make().wait_recv()
        for make in first + [p for _, p in passed]:
            make().wait_send()
        for make in local:
            make().wait()

    def run(self, name):
        na = self.na

        def body(*refs):
            ins, outs, sems = refs[:na], refs[na:2 * na], refs[2 * na:]
            self.start(ins, outs, sems)
            self.finish(ins, outs, sems)

        hbm = pl.BlockSpec(memory_space=pltpu.HBM)
        return pl.pallas_call(
            body, name=name, in_specs=[hbm] * na, out_specs=[hbm] * na, out_shape=self.out_shape,
            scratch_shapes=self.scratch, compiler_params=pltpu.CompilerParams(has_side_effects=True),
        )(*self.arrays)


def _all_gather(shards, name):
    return _Exchange("gather", shards).run(name)


def _adamw_math(w, g, m, v):
    m2 = ADAM_B1 * m + (1.0 - ADAM_B1) * g
    v2 = ADAM_B2 * v + (1.0 - ADAM_B2) * jnp.square(g)
    m_hat = m2 / (1.0 - ADAM_B1 ** ADAM_STEP)
    v_hat = v2 / (1.0 - ADAM_B2 ** ADAM_STEP)
    delta = -ADAM_LR * (m_hat / (jnp.sqrt(v_hat) + ADAM_EPS) + ADAM_WD * w)
    return delta, m2, v2


def _sum_adamw(parts, w, m, v, name):
    r, c = w.shape
    tr = _pick(r, (512, 256, 352, 128, 64, 32, 16, 8))
    np_ = parts.shape[0]

    def body(p_ref, w_ref, m_ref, v_ref, g_ref, d_ref, m2_ref, v2_ref):
        g = p_ref[0].astype(F32)
        for d in range(1, np_):
            g = g + p_ref[d].astype(F32)
        delta, m2, v2 = _adamw_math(w_ref[...], g, m_ref[...], v_ref[...])
        g_ref[...] = g
        d_ref[...] = delta
        m2_ref[...] = m2
        v2_ref[...] = v2

    blk = pl.BlockSpec((tr, c), lambda i: (i, 0))
    shp = jax.ShapeDtypeStruct((r, c), F32)
    return pl.pallas_call(
        body, name=name, grid=(r // tr,),
        in_specs=[pl.BlockSpec((np_, tr, c), lambda i: (0, i, 0)), blk, blk, blk],
        out_specs=[blk] * 4, out_shape=[shp] * 4, compiler_params=_params("parallel"),
    )(parts, w, m, v)


def _sum_rows(parts, name):
    np_, r, c = parts.shape

    def body(p_ref, o_ref):
        g = p_ref[0]
        for d in range(1, np_):
            g = g + p_ref[d]
        o_ref[...] = g

    return pl.pallas_call(body, name=name, out_shape=jax.ShapeDtypeStruct((r, c), F32))(parts)


def _pad_w_in(w):
    d = w.shape[1]
    n_ba = 2 * DN_HEADS
    a = w[:SC_OFF]
    ba = w[SC_OFF:SC_OFF + n_ba]
    sc = w[SC_OFF + n_ba:]
    return jnp.concatenate([a, sc, ba, jnp.zeros((BA_W - n_ba, d), w.dtype)], axis=0)


def _unpad_w_in(wp):
    n_ba = 2 * DN_HEADS
    return jnp.concatenate([wp[:SC_OFF], wp[BA_OFF:BA_OFF + n_ba], wp[SC_OFF:BA_OFF]], axis=0)


def _lane_row(v, off):
    return jnp.pad(v.astype(F32), (off, LANES - off - v.shape[0]))[None]


TT = 256
NB_INTRA = 8
CB_SCAN = 8


def _layer_fwd(x, p, ride_proj, ride_ffn, late):
    t, d = x.shape
    got_proj = got_ffn = None
    if ride_proj:
        h, proj, got_proj = _norm_matmul(x, p["norm1_g"], p["w_in"], "proj_fwd_gather", ride=ride_proj)
    else:
        h, proj = _norm_matmul(x, p["norm1_g"], p["w_in"], "proj_fwd")
    p = {**p, **late(got_proj)}
    qkvn, bg = _tok_call(
        "dn_pre", lambda tv, cv: (_dn_pre_math(*tv, *cv), ()), t, TT,
        [(proj, QKV_W, 0, "cur"), (proj, QKV_W, 0, "prev"), (proj, LANES, BA_OFF // LANES, "cur")],
        [p["dn_conv_w"], p["alog_row"], p["dt_row"]],
        [(TT, QKV_W, F32), (TT, LANES, F32)], [])
    u, w, qd, kd, qk, glb, tinv = _delta_intra(qkvn, bg, NB_INTRA)
    o, s_all = _delta_scan(u, w, qd, kd, qk, glb, CB_SCAN)
    cb0 = SC_OFF // SC_WIDTH
    mix_in = [(o, DN_WIDTH, 0, "cur"), (proj, DN_WIDTH, Z_OFF // DN_WIDTH, "cur"),
              (proj, SC_WIDTH, cb0, "cur"), (proj, SC_WIDTH, cb0 + 1, "cur"), (proj, SC_WIDTH, cb0 + 1, "prev"),
              (proj, SC_WIDTH, cb0 + 2, "cur"), (proj, SC_WIDTH, cb0 + 2, "prev")]
    mix_const = [p["dn_norm_g"], p["sc_norm_g"], p["sc_conv_w"]]
    (cat,) = _tok_call("mix_post", lambda tv, cv: ((_mix_math(*tv, *cv),), ()), t, TT,
                       mix_in, mix_const, [(TT, 2 * DN_WIDTH, MXU_DTYPE)], [])
    x_mid = _matmul(cat, p["w_out"], "nn", F32, "out_proj", residual=x)
    if ride_ffn:
        h2, gu, act, got_ffn = _ffn_up_swiglu(x_mid, p["norm2_g"], p["w_gu"], "ffn_up_gather", ride=ride_ffn)
    else:
        h2, gu, act = _ffn_up_swiglu(x_mid, p["norm2_g"], p["w_gu"], "ffn_up")
    x_out = _matmul(act, p["w_down"], "nn", F32, "ffn_down", residual=x_mid)
    saved = dict(x=x, h=h, proj=proj, qkvn=qkvn, bg=bg, u=u, w=w, qd=qd, kd=kd, qk=qk, glb=glb, tinv=tinv, s_all=s_all, o=o,
                 cat=cat, x_mid=x_mid, h2=h2, gu=gu, act=act, mix_in=mix_in, mix_const=mix_const)
    return x_out, saved, p, got_proj, got_ffn


def _layer_bwd(dx_out, p, s, ride_prev, ride_gu, ride_down):
    t, d = dx_out.shape
    got = {}
    dgu = _ffn_down_dx_swiglu(dx_out, p["w_down"], s["gu"], "ffn_down_dx")
    d_w_down = _matmul(s["act"], dx_out, "tn", MXU_DTYPE, "ffn_down_dw")
    if ride_prev:
        dx_mid, d_norm2, got["prev"] = _matmul_norm_bwd(dgu, p["w_gu"], s["x_mid"], p["norm2_g"], dx_out,
                                                        "ffn_up_dx_scatter", ride=ride_prev, w_rows_k=True)
    else:
        dx_mid, d_norm2 = _matmul_norm_bwd(dgu, p["w_gu"], s["x_mid"], p["norm2_g"], dx_out, "ffn_up_dx",
                                           w_rows_k=True)
    d_w_gu = _matmul(dgu, s["h2"], "tn", MXU_DTYPE, "ffn_up_dw")
    dcat = _matmul(dx_mid, p["w_out"], "nt", F32, "out_proj_dx")
    d_w_out = _matmul(s["cat"], dx_mid, "tn", MXU_DTYPE, "out_proj_dw")

    def mix_bwd(tv, cv):
        prim = tuple(tv[:7]) + tuple(cv)
        _, vjp = jax.vjp(_mix_math, *prim)
        do, dz, dgb, dgc, dgch, dhv, dhvh, ddng, dscg, dscw = vjp(tv[7])
        return (do, dz, dgb, dgc, dgch, dhv, dhvh), (ddng, dscg, dscw)

    wide = (TT, DN_WIDTH, F32)
    halo = (HALO, SC_WIDTH, F32)
    do, dz, dgb, dgc, dgc_h, dhv, dhv_h, d_dn_norm, d_sc_norm, d_sc_conv = _tok_call(
        "mix_post_bwd", mix_bwd, t, TT, s["mix_in"] + [(dcat, 2 * DN_WIDTH, 0, "cur")], s["mix_const"],
        [wide, wide, wide, wide, halo, wide, halo],
        [((1, HEAD_DIM), F32), ((1, SC_WIDTH), F32), ((SC_CONV, SC_WIDTH), F32)])
    cts = _delta_scan_bwd(s["u"], s["w"], s["qd"], s["kd"], s["qk"], s["glb"], s["s_all"], do, CB_SCAN)
    dq, dk, dv, dbg = _delta_intra_bwd(s["qkvn"], s["bg"], s["tinv"], cts, NB_INTRA)
    proj = s["proj"]

    def dn_pre_bwd(tv, cv):
        cur, hal, ba, dq_, dk_, dv_, dbg_ = tv
        _, vjp = jax.vjp(_dn_pre_math, cur, hal, ba, *cv)
        dcur, dhal, dba, dcw, dal, ddt = vjp((jnp.concatenate([dq_, dk_, dv_], axis=1), dbg_))
        return (dcur, dhal, dba), (dcw, dal, ddt)

    dqkv, dqkv_h, dba, d_dn_conv, d_alog, d_dt = _tok_call(
        "dn_pre_bwd", dn_pre_bwd, t, TT,
        [(proj, QKV_W, 0, "cur"), (proj, QKV_W, 0, "prev"), (proj, LANES, BA_OFF // LANES, "cur"),
         (dq, DN_WIDTH, 0, "cur"), (dk, DN_WIDTH, 0, "cur"), (dv, DN_WIDTH, 0, "cur"), (dbg, LANES, 0, "cur")],
        [p["dn_conv_w"], p["alog_row"], p["dt_row"]],
        [(TT, QKV_W, F32), (HALO, QKV_W, F32), (TT, LANES, F32)],
        [((DN_CONV, QKV_W), F32), ((1, LANES), F32), ((1, LANES), F32)])

    def assemble(tv, cv):
        dqkv_, dqkv_n, dz_, dgb_, dgc_, dgc_n, dhv_, dhv_n, dba_ = tv

        def with_halo(cur, nxt):
            return cur + jnp.concatenate([jnp.zeros((TT - HALO, cur.shape[1]), F32), nxt], axis=0)

        out = jnp.concatenate([with_halo(dqkv_, dqkv_n), dz_, dgb_, with_halo(dgc_, dgc_n), with_halo(dhv_, dhv_n),
                               dba_, jnp.zeros((TT, BA_W - LANES), F32)], axis=1)
        return (out,), ()

    (dproj,) = _tok_call(
        "dproj_assemble", assemble, t, TT,
        [(dqkv, QKV_W, 0, "cur"), (dqkv_h, QKV_W, 0, "next8"), (dz, DN_WIDTH, 0, "cur"), (dgb, SC_WIDTH, 0, "cur"),
         (dgc, SC_WIDTH, 0, "cur"), (dgc_h, SC_WIDTH, 0, "next8"), (dhv, SC_WIDTH, 0, "cur"),
         (dhv_h, SC_WIDTH, 0, "next8"), (dba, LANES, 0, "cur")], [],
        [(TT, PROJ_W, MXU_DTYPE)], [])
    dx_in, d_norm1, got["gu"] = _matmul_norm_bwd(dproj, p["w_in"], s["x"], p["norm1_g"], dx_mid,
                                                 "proj_dx_scatter", ride=ride_gu(d_w_gu), w_rows_k=True)
    d_w_in, got["down"] = _matmul(dproj, s["h"], "tn", MXU_DTYPE, "proj_dw_scatter", ride=ride_down(d_w_down))
    grads = dict(w_in=d_w_in, w_out=d_w_out, w_gu=d_w_gu, w_down=d_w_down, norm1_g=d_norm1, norm2_g=d_norm2,
                 dn_norm_g=d_dn_norm, sc_norm_g=d_sc_norm, sc_conv_w=d_sc_conv, dn_conv_w=d_dn_conv,
                 alog=d_alog, dt=d_dt)
    return dx_in, grads, got


def _final_loss(x, g, target):
    t, d = x.shape

    def fn(tv, cv):
        xv, tg = tv

        def loss_fn(xx, gg):
            err = jnp.square(_rms_norm(xx, gg) - tg)
            return 0.5 * jnp.sum(jnp.mean(err, axis=-1))

        loss, vjp = jax.vjp(loss_fn, xv, cv[0])
        dx, dg = vjp(jnp.ones((), F32))
        return (dx,), (jnp.full((1, LANES), loss, F32), dg)

    return _tok_call("final_loss", fn, t, TT, [(x, d, 0, "cur"), (target, d, 0, "cur")], [g],
                     [(TT, d, F32)], [((1, LANES), F32), ((1, d), F32)])


def _pack_rows(arrs):
    rows, offs, r0 = [], [], 0
    for a in arrs:
        n = a.size
        nr = -(-n // LANES)
        flat = jnp.pad(a.reshape(-1).astype(F32), (0, nr * LANES - n))
        rows.append(flat.reshape(nr, LANES))
        offs.append((r0, nr, a.shape))
        r0 += nr
    pad = (-r0) % 8
    if pad:
        rows.append(jnp.zeros((pad, LANES), F32))
    return jnp.concatenate(rows, axis=0), offs


def _unpack_rows(packed, offs):
    out = []
    for r0, nr, shp in offs:
        n = 1
        for s_ in shp:
            n *= s_
        out.append(packed[r0:r0 + nr].reshape(-1)[:n].reshape(shp))
    return out


def kernel(x, norm1_g, w_in, dn_conv_w, dn_a_log, dn_dt_bias, dn_norm_g, sc_conv_w, sc_norm_g, w_out, norm2_g, ffn_w_gate, ffn_w_up, ffn_w_down, final_norm_g, loss_target, m_norm1_g, m_w_in, m_dn_conv_w, m_dn_a_log, m_dn_dt_bias, m_dn_norm_g, m_sc_conv_w, m_sc_norm_g, m_w_out, m_norm2_g, m_ffn_w_gate, m_ffn_w_up, m_ffn_w_down, m_final_norm_g, v_norm1_g, v_w_in, v_dn_conv_w, v_dn_a_log, v_dn_dt_bias, v_dn_norm_g, v_sc_conv_w, v_sc_norm_g, v_w_out, v_norm2_g, v_ffn_w_gate, v_ffn_w_up, v_ffn_w_down, v_final_norm_g):
    depth, d, cin = w_in.shape
    t = x.shape[1]
    dff_s = ffn_w_gate.shape[2]
    tr = lambda a: a.transpose(0, 2, 1)
    gate_t, up_t = tr(ffn_w_gate), tr(ffn_w_up)
    by_layer = lambda a: [a[l].T for l in range(depth)]
    win_t = by_layer(w_in)
    me = _my_index()
    x2 = x.reshape(t, d)
    tgt = loss_target.reshape(t, d)

    conv_pack, conv_offs = _pack_rows([dn_conv_w, sc_conv_w])
    (conv_all,) = _all_gather([conv_pack], "gather_conv")
    dn_parts, sc_parts = zip(*[_unpack_rows(conv_all[j], conv_offs) for j in range(N_DEV)])
    dn_conv_full = jnp.concatenate(dn_parts, axis=2)
    sc_conv_full = jnp.concatenate(sc_parts, axis=2)

    def shards(l):
        return [a[l].astype(MXU_DTYPE) for a in (win_t, w_out, gate_t, up_t, ffn_w_down)]

    def mixer_params(l, g_in, g_out):
        return dict(
            w_in=_pad_w_in(g_in.reshape(N_DEV * cin, d)), w_out=g_out.reshape(d, d),
            norm1_g=norm1_g[l][None], norm2_g=norm2_g[l][None], dn_norm_g=dn_norm_g[l][None],
            sc_norm_g=sc_norm_g[l][None], dn_conv_w=dn_conv_full[l], sc_conv_w=sc_conv_full[l],
            alog_row=_lane_row(dn_a_log[l], DN_HEADS), dt_row=_lane_row(dn_dt_bias[l], DN_HEADS))

    def ffn_params(g_gate, g_up, g_down):
        dff = N_DEV * dff_s
        return dict(w_gu=_interleave_gu(g_gate.reshape(dff, d), g_up.reshape(dff, d)), w_down=g_down.reshape(dff, d))

    nxt = mixer_params(0, *_all_gather(shards(0)[:2], "gather_first"))
    nxt_ffn = None
    params, saved = [], []
    xc = x2
    for l in range(depth):
        last = l + 1 == depth
        own_ffn = shards(l)[2:] if nxt_ffn is None else []
        ahead = [] if last else shards(l + 1)
        ride_proj = _Exchange("gather", own_ffn + ahead[:2]) if own_ffn or ahead else None
        ride_ffn = None if last else _Exchange("gather", ahead[2:])

        def late(got, own_ffn=own_ffn, nxt_ffn=nxt_ffn):
            return ffn_params(*got[:3]) if own_ffn else nxt_ffn

        xc, s, p_l, got_proj, got_ffn = _layer_fwd(xc, nxt, ride_proj, ride_ffn, late)
        params.append(p_l)
        saved.append(s)
        if not last:
            nxt = mixer_params(l + 1, *got_proj[len(own_ffn):])
            nxt_ffn = ffn_params(*got_ffn)
    dx, loss_part, d_final = _final_loss(xc, final_norm_g[None], tgt)

    names = ("w_in", "w_out", "ffn_w_gate", "ffn_w_up", "ffn_w_down")
    big_out = {n: {k: [None] * depth for k in ("g", "d", "m", "v")} for n in names}
    w_loc = dict(w_in=win_t, w_out=w_out, ffn_w_gate=gate_t, ffn_w_up=up_t, ffn_w_down=ffn_w_down)
    m_loc = dict(w_in=by_layer(m_w_in), w_out=m_w_out, ffn_w_gate=tr(m_ffn_w_gate), ffn_w_up=tr(m_ffn_w_up), ffn_w_down=m_ffn_w_down)
    v_loc = dict(w_in=by_layer(v_w_in), w_out=v_w_out, ffn_w_gate=tr(v_ffn_w_gate), ffn_w_up=tr(v_ffn_w_up), ffn_w_down=v_ffn_w_down)


    def ride_gu(d_w_gu):
        rows = lambda a: a.reshape(N_DEV, dff_s, d).astype(MXU_DTYPE)
        g_gate, g_up = _split_gu(d_w_gu)
        return _Exchange("scatter", [rows(g_gate), rows(g_up)])

    def ride_down(d_w_down):
        return _Exchange("scatter", [d_w_down.reshape(N_DEV, dff_s, d).astype(MXU_DTYPE)])

    def apply(l, which, recv):
        for n, r in zip(which, recv):
            res = _sum_adamw(r, w_loc[n][l], m_loc[n][l], v_loc[n][l], "adamw_" + n)
            for k, a in zip(("g", "d", "m", "v"), res):
                big_out[n][k][l] = a

    grads = [None] * depth
    pending = None
    for l in reversed(range(depth)):
        ride_prev = _Exchange("scatter", pending) if pending else None
        dx, grads[l], got = _layer_bwd(dx, params[l], saved[l], ride_prev, ride_gu, ride_down)
        if ride_prev:
            apply(l + 1, names[:2], got["prev"])
        apply(l, names[2:], [*got["gu"], *got["down"]])
        pending = [_unpad_w_in(grads[l]["w_in"]).reshape(N_DEV, cin, d).astype(MXU_DTYPE),
                   grads[l]["w_out"].reshape(N_DEV, d // N_DEV, d).astype(MXU_DTYPE)]
    apply(0, names[:2], _Exchange("scatter", pending).run("scatter_last"))
    grad_x = dx.reshape(x.shape)
    big_out = {n: {k: jnp.stack(v_, axis=1 if n == "w_in" else 0) for k, v_ in o.items()} for n, o in big_out.items()}
    for n in ("ffn_w_gate", "ffn_w_up"):
        big_out[n] = {k: tr(a) for k, a in big_out[n].items()}
    big_out["w_in"] = {k: a.transpose(1, 2, 0) for k, a in big_out["w_in"].items()}

    stack = lambda key: jnp.stack([grads[l][key] for l in range(depth)])
    small_parts = [stack("norm1_g").reshape(depth, d), stack("norm2_g").reshape(depth, d), d_final.reshape(d),
                   stack("dn_norm_g").reshape(depth, HEAD_DIM), stack("sc_norm_g").reshape(depth, SC_WIDTH),
                   stack("alog").reshape(depth, LANES), stack("dt").reshape(depth, LANES),
                   stack("dn_conv_w"), stack("sc_conv_w"), loss_part]
    small_pack, small_offs = _pack_rows(small_parts)
    (small_all,) = _all_gather([small_pack], "gather_small")
    total = _sum_rows(small_all, "sum_small")
    (g_n1, g_n2, g_fin, g_dnn, g_scn, g_alog, g_dt, g_dnc, g_scc, loss_row) = _unpack_rows(total, small_offs)
    loss = loss_row[0, 0]
    g_alog = g_alog[:, DN_HEADS:2 * DN_HEADS]
    g_dt = g_dt[:, DN_HEADS:2 * DN_HEADS]
    dnc_w = dn_conv_w.shape[2]
    scc_w = sc_conv_w.shape[2]
    g_dnc = lax.dynamic_slice_in_dim(g_dnc, me * dnc_w, dnc_w, axis=2)
    g_scc = lax.dynamic_slice_in_dim(g_scc, me * scc_w, scc_w, axis=2)
    sm_g = [g_n1, g_dnc, g_alog, g_dt, g_dnn, g_scc, g_scn, g_n2, g_fin]
    sm_w = [norm1_g, dn_conv_w, dn_a_log, dn_dt_bias, dn_norm_g, sc_conv_w, sc_norm_g, norm2_g, final_norm_g]
    sm_m = [m_norm1_g, m_dn_conv_w, m_dn_a_log, m_dn_dt_bias, m_dn_norm_g, m_sc_conv_w, m_sc_norm_g, m_norm2_g, m_final_norm_g]
    sm_v = [v_norm1_g, v_dn_conv_w, v_dn_a_log, v_dn_dt_bias, v_dn_norm_g, v_sc_conv_w, v_sc_norm_g, v_norm2_g, v_final_norm_g]
    pg, offs = _pack_rows(sm_g)
    pw, _ = _pack_rows(sm_w)
    pm, _ = _pack_rows(sm_m)
    pv, _ = _pack_rows(sm_v)
    sg, sd, sm_, sv = _sum_adamw(pg[None], pw, pm, pv, "adamw_small")
    small_out = {k: _unpack_rows(a, offs) for k, a in zip(("g", "d", "m", "v"), (sg, sd, sm_, sv))}

    def outputs(k):
        s_ = small_out[k]
        b = big_out
        return [s_[0], b["w_in"][k], s_[1], s_[2], s_[3], s_[4], s_[5], s_[6], b["w_out"][k], s_[7],
                b["ffn_w_gate"][k], b["ffn_w_up"][k], b["ffn_w_down"][k], s_[8]]

    return (loss, grad_x, *outputs("g"), *outputs("d"), *outputs("m"), *outputs("v"))
```

```python
import functools

import jax
import jax.numpy as jnp
from jax import lax
from jax.experimental import pallas as pl
from jax.experimental.pallas import tpu as pltpu

F32 = jnp.float32
MXU_DTYPE = jnp.bfloat16
MESH = pl.DeviceIdType.MESH

N_DEV = 8
EPS = 1e-6
DN_HEADS = 4
HEAD_DIM = 128
DN_WIDTH = DN_HEADS * HEAD_DIM
SC_WIDTH = 512
SC_GROUPS = 4
DN_CONV = 4
SC_CONV = 3
CHUNK = 64
HALO = 8
LANES = 128

QKV_W = 3 * DN_WIDTH
Z_OFF = QKV_W
SC_OFF = Z_OFF + DN_WIDTH
BA_OFF = SC_OFF + 3 * SC_WIDTH
BA_W = 256
PROJ_W = BA_OFF + BA_W

ADAM_LR = 0.001
ADAM_B1 = 0.9
ADAM_B2 = 0.999
ADAM_EPS = 1e-08
ADAM_WD = 0.01
ADAM_STEP = 10


def _pick(n, cands):
    for c in cands:
        if n % c == 0:
            return c
    return n


def _params(*sem):
    return pltpu.CompilerParams(dimension_semantics=sem)


def _rms_norm(x, g):
    return x * lax.rsqrt(jnp.mean(x * x, axis=-1, keepdims=True) + EPS) * g


def _dot(a, b, dims=(((1,), (0,)), ((), ()))):
    return lax.dot_general(a.astype(MXU_DTYPE), b.astype(MXU_DTYPE), dims, preferred_element_type=F32)


def _split_terms(x, terms):
    out = []
    for _ in range(terms):
        hi = x.astype(MXU_DTYPE)
        out.append(hi)
        x = x - hi.astype(F32)
    return out


def _ein_impl(spec, terms, a, b):
    ta, tb = terms
    if ta == 1 and tb == 1:
        return jnp.einsum(spec, a.astype(MXU_DTYPE), b.astype(MXU_DTYPE), preferred_element_type=F32)
    pa, pb = _split_terms(a, ta), _split_terms(b, tb)
    order = max(ta, tb) - 1
    acc = None
    for deg in range(order, -1, -1):
        for i in range(ta):
            j = deg - i
            if 0 <= j < tb:
                t = jnp.einsum(spec, pa[i], pb[j], preferred_element_type=F32)
                acc = t if acc is None else acc + t
    return acc


@functools.partial(jax.custom_vjp, nondiff_argnums=(0, 1))
def _ein(spec, terms, a, b):
    return _ein_impl(spec, terms, a, b)


def _ein_fwd(spec, terms, a, b):
    return _ein_impl(spec, terms, a, b), (a, b)


def _ein_bwd(spec, terms, res, ct):
    a, b = res
    xy, z = spec.split("->")
    x, y = xy.split(",")
    tc = min(max(terms), 2)
    da = _ein_impl(f"{z},{y}->{x}", (tc, terms[1]), ct, b)
    db = _ein_impl(f"{x},{z}->{y}", (terms[0], tc), a, ct)
    return da, db


_ein.defvjp(_ein_fwd, _ein_bwd)

FAST = (1, 1)
PRECISE = (2, 2)
LHS_EXACT = (1, 3)


def _causal_conv(cur, halo, w, k):
    tt = cur.shape[0]
    xp = jnp.concatenate([halo, cur], axis=0)
    y = None
    for j in range(k):
        start = HALO - (k - 1) + j
        term = xp[start:start + tt] * w[j:j + 1]
        y = term if y is None else y + term
    return y


def _matmul(a, b, mode, out_dtype, name, residual=None, ride=None):
    if mode == "nn":
        (m, k), (k2, n) = a.shape, b.shape
    elif mode == "nt":
        (m, k), (n, k2) = a.shape, b.shape
    else:
        (k, m), (k2, n) = a.shape, b.shape
    assert k == k2
    tm = _pick(m, (1024, 1408, 1280, 512, 256, 128))
    tn = _pick(n, (1280, 1408, 1024, 512, 256, 128))
    tk = k if k <= 2816 else _pick(k, (1024, 768, 512))
    gi, gj, nk = m // tm, n // tn, k // tk
    dims = {"nn": (((1,), (0,)), ((), ())), "nt": (((1,), (1,)), ((), ())), "tn": (((0,), (0,)), ((), ()))}[mode]
    a_spec = {"nn": pl.BlockSpec((tm, tk), lambda i, j, q: (i, q)),
              "nt": pl.BlockSpec((tm, tk), lambda i, j, q: (i, q)),
              "tn": pl.BlockSpec((tk, tm), lambda i, j, q: (q, i))}[mode]
    b_spec = {"nn": pl.BlockSpec((tk, tn), lambda i, j, q: (q, j)),
              "nt": pl.BlockSpec((tn, tk), lambda i, j, q: (j, q)),
              "tn": pl.BlockSpec((tk, tn), lambda i, j, q: (q, j))}[mode]
    o_spec = pl.BlockSpec((tm, tn), lambda i, j, q: (i, j))
    has_res = residual is not None
    n_in = 3 if has_res else 2
    nr = ride.na if ride else 0

    def body(*refs):
        a_ref, b_ref = refs[0], refs[1]
        r_ref = refs[2] if has_res else None
        ride_in = refs[n_in:n_in + nr]
        o_ref = refs[n_in + nr]
        ride_out = refs[n_in + nr + 1:n_in + 2 * nr + 1]
        acc = refs[n_in + 2 * nr + 1]
        ride_sems = refs[n_in + 2 * nr + 2:]
        i, j, q = pl.program_id(0), pl.program_id(1), pl.program_id(2)
        if ride:
            @pl.when((i == 0) & (j == 0) & (q == 0))
            def _():
                ride.start(ride_in, ride_out, ride_sems)

        @pl.when(q == 0)
        def _():
            acc[...] = jnp.zeros_like(acc)

        acc[...] += _dot(a_ref[...], b_ref[...], dims)

        @pl.when(q == nk - 1)
        def _():
            r = acc[...]
            if has_res:
                r = r + r_ref[...]
            o_ref[...] = r.astype(o_ref.dtype)

        if ride:
            @pl.when((i == gi - 1) & (j == gj - 1) & (q == nk - 1))
            def _():
                ride.finish(ride_in, ride_out, ride_sems)

    hbm = pl.BlockSpec(memory_space=pltpu.HBM)
    in_specs = [a_spec, b_spec] + ([o_spec] if has_res else []) + [hbm] * nr
    args = (a, b) + ((residual,) if has_res else ()) + (tuple(ride.arrays) if ride else ())
    res = pl.pallas_call(
        body, name=name, grid=(gi, gj, nk), in_specs=in_specs, out_specs=[o_spec] + [hbm] * nr,
        out_shape=[jax.ShapeDtypeStruct((m, n), out_dtype)] + (ride.out_shape if ride else []),
        scratch_shapes=[pltpu.VMEM((tm, tn), F32)] + (ride.scratch if ride else []),
        compiler_params=pltpu.CompilerParams(
            dimension_semantics=("arbitrary",) * 3 if ride else ("parallel", "parallel", "arbitrary"),
            has_side_effects=bool(ride)),
    )(*args)
    return (res[0], res[1:]) if ride else res[0]


def _norm_matmul(x, g, w, name, ride=None):
    t, d = x.shape
    n = w.shape[0]
    tm = _pick(t, (1024, 512, 256, 128))
    tn = _pick(n, (1280, 1408, 1024, 512, 256, 128))
    gi, gj = t // tm, n // tn
    nr = ride.na if ride else 0

    def body(*refs):
        x_ref, g_ref, w_ref = refs[:3]
        ride_in = refs[3:3 + nr]
        h_ref, y_ref = refs[3 + nr], refs[4 + nr]
        ride_out = refs[5 + nr:5 + 2 * nr]
        h_scr = refs[5 + 2 * nr]
        ride_sems = refs[6 + 2 * nr:]
        i, j = pl.program_id(0), pl.program_id(1)
        if ride:
            @pl.when((i == 0) & (j == 0))
            def _():
                ride.start(ride_in, ride_out, ride_sems)

        @pl.when(j == 0)
        def _():
            h = _rms_norm(x_ref[...], g_ref[...]).astype(MXU_DTYPE)
            h_scr[...] = h
            h_ref[...] = h

        y_ref[...] = _dot(h_scr[...], w_ref[...], (((1,), (1,)), ((), ())))

        if ride:
            @pl.when((i == gi - 1) & (j == gj - 1))
            def _():
                ride.finish(ride_in, ride_out, ride_sems)

    hbm = pl.BlockSpec(memory_space=pltpu.HBM)
    res = pl.pallas_call(
        body, name=name, grid=(gi, gj),
        in_specs=[pl.BlockSpec((tm, d), lambda i, j: (i, 0)), pl.BlockSpec((1, d), lambda i, j: (0, 0)),
                  pl.BlockSpec((tn, d), lambda i, j: (j, 0))] + [hbm] * nr,
        out_specs=[pl.BlockSpec((tm, d), lambda i, j: (i, 0)), pl.BlockSpec((tm, tn), lambda i, j: (i, j))] + [hbm] * nr,
        out_shape=[jax.ShapeDtypeStruct((t, d), MXU_DTYPE), jax.ShapeDtypeStruct((t, n), F32)]
        + (ride.out_shape if ride else []),
        scratch_shapes=[pltpu.VMEM((tm, d), MXU_DTYPE)] + (ride.scratch if ride else []),
        compiler_params=pltpu.CompilerParams(
            dimension_semantics=("arbitrary",) * 2 if ride else ("parallel", "arbitrary"),
            has_side_effects=bool(ride)),
    )(x, g, w, *(ride.arrays if ride else ()))
    return (res[0], res[1], res[2:]) if ride else (res[0], res[1])


def _swiglu_math(g, u):
    return jax.nn.silu(g) * u


def _gu_tile(dff):
    return _pick(dff, (1408, 1024, 512, 256, 128))


def _interleave_gu(gate_t, up_t):
    tn = _gu_tile(gate_t.shape[0])
    pieces = []
    for j in range(gate_t.shape[0] // tn):
        pieces += [gate_t[j * tn:(j + 1) * tn], up_t[j * tn:(j + 1) * tn]]
    return jnp.concatenate(pieces, axis=0)


def _split_gu(gu_t):
    dff = gu_t.shape[0] // 2
    tn = _gu_tile(dff)
    tiles = [gu_t[j * tn:(j + 1) * tn] for j in range(2 * dff // tn)]
    return jnp.concatenate(tiles[0::2], axis=0), jnp.concatenate(tiles[1::2], axis=0)


def _ffn_up_swiglu(x, g, w_gu, name, ride=None):
    t, d = x.shape
    dff = w_gu.shape[0] // 2
    tn = _gu_tile(dff)
    tm = _pick(t, (512, 256, 128))
    gj, gi = dff // tn, t // tm
    nr = ride.na if ride else 0

    def body(*refs):
        x_ref, g_ref, w_ref = refs[:3]
        ride_in = refs[3:3 + nr]
        h_ref, gu_ref, act_ref = refs[3 + nr:6 + nr]
        ride_out = refs[6 + nr:6 + 2 * nr]
        ride_sems = refs[6 + 2 * nr:]
        j, i = pl.program_id(0), pl.program_id(1)
        if ride:
            @pl.when((i == 0) & (j == 0))
            def _():
                ride.start(ride_in, ride_out, ride_sems)

        h = _rms_norm(x_ref[...], g_ref[...]).astype(MXU_DTYPE)

        @pl.when(j == 0)
        def _():
            h_ref[...] = h

        y = _dot(h, w_ref[...], (((1,), (1,)), ((), ())))
        gu_ref[...] = y.astype(gu_ref.dtype)
        act_ref[...] = _swiglu_math(y[:, :tn], y[:, tn:]).astype(act_ref.dtype)

        if ride:
            @pl.when((i == gi - 1) & (j == gj - 1))
            def _():
                ride.finish(ride_in, ride_out, ride_sems)

    hbm = pl.BlockSpec(memory_space=pltpu.HBM)
    res = pl.pallas_call(
        body, name=name, grid=(gj, gi),
        in_specs=[pl.BlockSpec((tm, d), lambda j, i: (i, 0)), pl.BlockSpec((1, d), lambda j, i: (0, 0)),
                  pl.BlockSpec((2 * tn, d), lambda j, i: (j, 0))] + [hbm] * nr,
        out_specs=[pl.BlockSpec((tm, d), lambda j, i: (jnp.where(j == 0, i, gi - 1), 0)),
                   pl.BlockSpec((tm, 2 * tn), lambda j, i: (i, j)),
                   pl.BlockSpec((tm, tn), lambda j, i: (i, j))] + [hbm] * nr,
        out_shape=[jax.ShapeDtypeStruct((t, d), MXU_DTYPE), jax.ShapeDtypeStruct((t, 2 * dff), MXU_DTYPE),
                   jax.ShapeDtypeStruct((t, dff), MXU_DTYPE)] + (ride.out_shape if ride else []),
        scratch_shapes=(ride.scratch if ride else []),
        compiler_params=pltpu.CompilerParams(dimension_semantics=("arbitrary", "arbitrary"),
                                             has_side_effects=bool(ride)),
    )(x, g, w_gu, *(ride.arrays if ride else ()))
    return (res[0], res[1], res[2], res[3:]) if ride else tuple(res)


def _ffn_down_dx_swiglu(dx_out, w_down, gu, name):
    t, d = dx_out.shape
    dff = w_down.shape[0]
    tn = _gu_tile(dff)
    tm = _pick(t, (512, 256, 128))

    def body(dx_ref, w_ref, gu_ref, o_ref):
        dact = _dot(dx_ref[...], w_ref[...], (((1,), (1,)), ((), ())))
        gu_v = gu_ref[...].astype(F32)
        _, vjp = jax.vjp(_swiglu_math, gu_v[:, :tn], gu_v[:, tn:])
        dg, du = vjp(dact)
        o_ref[...] = jnp.concatenate([dg, du], axis=1).astype(o_ref.dtype)

    return pl.pallas_call(
        body, name=name, grid=(dff // tn, t // tm),
        in_specs=[pl.BlockSpec((tm, d), lambda j, i: (i, 0)), pl.BlockSpec((tn, d), lambda j, i: (j, 0)),
                  pl.BlockSpec((tm, 2 * tn), lambda j, i: (i, j))],
        out_specs=pl.BlockSpec((tm, 2 * tn), lambda j, i: (i, j)),
        out_shape=jax.ShapeDtypeStruct((t, 2 * dff), MXU_DTYPE),
        compiler_params=_params("parallel", "parallel"),
    )(dx_out, w_down, gu)


def _matmul_norm_bwd(dy, w, x, g, dres, name, ride=None, w_rows_k=False):
    t, k = dy.shape
    d = w.shape[1] if w_rows_k else w.shape[0]
    tm = _pick(t, (1024, 512, 256, 128))
    tk = k if k <= 2816 else _pick(k, (1408, 1280, 1024, 768, 512))
    tr = _pick(tm, (256, 128))
    gi, nk = t // tm, k // tk
    nr = ride.na if ride else 0

    def body(*refs):
        dy_ref, w_ref, x_ref, g_ref, dres_ref = refs[:5]
        ride_in = refs[5:5 + nr]
        dx_ref, dg_ref = refs[5 + nr], refs[6 + nr]
        ride_out = refs[7 + nr:7 + 2 * nr]
        acc = refs[7 + 2 * nr]
        ride_sems = refs[8 + 2 * nr:]
        i, q = pl.program_id(0), pl.program_id(1)

        @pl.when((i == 0) & (q == 0))
        def _():
            dg_ref[...] = jnp.zeros_like(dg_ref)
            if ride:
                ride.start(ride_in, ride_out, ride_sems)

        @pl.when(q == 0)
        def _():
            acc[...] = jnp.zeros_like(acc)

        acc[...] += _dot(dy_ref[...], w_ref[...], (((1,), (0 if w_rows_k else 1,)), ((), ())))

        @pl.when(q == nk - 1)
        def _():
            for r in range(tm // tr):
                rows = slice(r * tr, (r + 1) * tr)
                _, vjp = jax.vjp(_rms_norm, x_ref[rows], g_ref[...])
                dxn, dg = vjp(acc[rows])
                dx_ref[rows] = dres_ref[rows] + dxn
                dg_ref[...] += dg

        if ride:
            @pl.when((i == gi - 1) & (q == nk - 1))
            def _():
                ride.finish(ride_in, ride_out, ride_sems)

    hbm = pl.BlockSpec(memory_space=pltpu.HBM)
    row = pl.BlockSpec((tm, d), lambda i, q: (i, 0))
    res = pl.pallas_call(
        body, name=name, grid=(gi, nk),
        in_specs=[pl.BlockSpec((tm, tk), lambda i, q: (i, q)),
                  pl.BlockSpec((tk, d), lambda i, q: (q, 0)) if w_rows_k else pl.BlockSpec((d, tk), lambda i, q: (0, q)),
                  row,
                  pl.BlockSpec((1, d), lambda i, q: (0, 0)), row] + [hbm] * nr,
        out_specs=[row, pl.BlockSpec((1, d), lambda i, q: (0, 0))] + [hbm] * nr,
        out_shape=[jax.ShapeDtypeStruct((t, d), F32), jax.ShapeDtypeStruct((1, d), F32)]
        + (ride.out_shape if ride else []),
        scratch_shapes=[pltpu.VMEM((tm, d), F32)] + (ride.scratch if ride else []),
        compiler_params=pltpu.CompilerParams(dimension_semantics=("arbitrary", "arbitrary"),
                                             has_side_effects=bool(ride)),
    )(dy, w, x, g, dres, *(ride.arrays if ride else ()))
    return (res[0], res[1], res[2:]) if ride else (res[0], res[1])


def _tok_call(name, fn, t, tt, tok_in, const_in, tok_out, acc_out):
    nblk = t // tt
    hb = tt // HALO
    in_specs, args = [], []
    for arr, w, cb, mode in tok_in:
        if mode == "cur":
            spec = pl.BlockSpec((tt, w), lambda i, cb=cb: (i, cb))
        elif mode == "prev":
            spec = pl.BlockSpec((HALO, w), lambda i, cb=cb: (jnp.maximum(i * hb - 1, 0), cb))
        else:
            spec = pl.BlockSpec((HALO, w), lambda i, cb=cb: (jnp.minimum(i + 1, nblk - 1), cb))
        in_specs.append(spec)
        args.append(arr)
    for arr in const_in:
        in_specs.append(pl.BlockSpec(arr.shape, lambda i: (0, 0)))
        args.append(arr)
    out_specs, out_shape = [], []
    for rows, w, dt in tok_out:
        out_specs.append(pl.BlockSpec((rows, w), lambda i: (i, 0)))
        out_shape.append(jax.ShapeDtypeStruct((nblk * rows, w), dt))
    for shp, dt in acc_out:
        out_specs.append(pl.BlockSpec(shp, lambda i: (0, 0)))
        out_shape.append(jax.ShapeDtypeStruct(shp, dt))
    n_tok, n_const, n_out = len(tok_in), len(const_in), len(tok_out)

    def body(*refs):
        i = pl.program_id(0)
        tok_vals = []
        for (_, _, _, mode), r in zip(tok_in, refs[:n_tok]):
            v = r[...]
            if mode == "prev":
                v = jnp.where(i > 0, v, jnp.zeros_like(v))
            elif mode == "next8":
                v = jnp.where(i < nblk - 1, v, jnp.zeros_like(v))
            tok_vals.append(v)
        const_vals = [r[...] for r in refs[n_tok:n_tok + n_const]]
        outs, accs = fn(tok_vals, const_vals)
        o_refs = refs[n_tok + n_const:n_tok + n_const + n_out]
        a_refs = refs[n_tok + n_const + n_out:]
        for r, v in zip(o_refs, outs):
            r[...] = v.astype(r.dtype)
        if a_refs:
            @pl.when(i == 0)
            def _():
                for r in a_refs:
                    r[...] = jnp.zeros_like(r)

            for r, v in zip(a_refs, accs):
                r[...] += v.astype(r.dtype)

    res = pl.pallas_call(
        body, name=name, grid=(nblk,), in_specs=in_specs, out_specs=out_specs, out_shape=out_shape,
        compiler_params=_params("arbitrary" if acc_out else "parallel"),
    )(*args)
    return res


def _dn_pre_math(cur, halo, ba, cw, alog, dtb):
    tt = cur.shape[0]
    a = jax.nn.silu(_causal_conv(cur, halo, cw, DN_CONV))
    pieces = []
    for p in range(2 * DN_HEADS):
        xh = a[:, p * HEAD_DIM:(p + 1) * HEAD_DIM]
        xh = xh * lax.rsqrt(jnp.sum(xh * xh, axis=-1, keepdims=True) + EPS)
        if p < DN_HEADS:
            xh = xh * (HEAD_DIM ** -0.5)
        pieces.append(xh)
    pieces.append(a[:, 2 * DN_WIDTH:])
    qkvn = jnp.concatenate(pieces, axis=1)
    lane = lax.broadcasted_iota(jnp.int32, ba.shape, 1)
    raw = jnp.where(lane < DN_HEADS, jax.nn.sigmoid(ba), -jnp.exp(alog) * jax.nn.softplus(ba + dtb))
    r = lax.broadcasted_iota(jnp.int32, (tt, tt), 0)
    c = lax.broadcasted_iota(jnp.int32, (tt, tt), 1)
    tri = jnp.where((r // CHUNK == c // CHUNK) & (c <= r), 1.0, 0.0).astype(F32)
    cums = _ein("ij,jk->ik", LHS_EXACT, tri, raw)
    bg = jnp.where(lane < DN_HEADS, raw, cums)
    return qkvn, bg


def _mix_math(o, z, gb, gc, gc_halo, hv, hv_halo, dng, scg, scw):
    outs = []
    for h in range(DN_HEADS):
        sl = slice(h * HEAD_DIM, (h + 1) * HEAD_DIM)
        oh = o[:, sl]
        outs.append(oh * lax.rsqrt(jnp.mean(oh * oh, axis=-1, keepdims=True) + EPS) * dng * jax.nn.silu(z[:, sl]))
    y = gb * _causal_conv(gc * hv, gc_halo * hv_halo, scw, SC_CONV)
    gw = SC_WIDTH // SC_GROUPS
    for g in range(SC_GROUPS):
        sl = slice(g * gw, (g + 1) * gw)
        yg = y[:, sl]
        outs.append(yg * lax.rsqrt(jnp.mean(yg * yg, axis=-1, keepdims=True) + EPS) * scg[:, sl])
    return jnp.concatenate(outs, axis=1)


def _tri_inverse(a):
    c = a.shape[-1]
    r = lax.broadcasted_iota(jnp.int32, (c, c), 0)
    q = lax.broadcasted_iota(jnp.int32, (c, c), 1)
    eye = jnp.where(r == q, 1.0, 0.0).astype(F32)[None]
    blk = (r // 16 == q // 16)[None]
    d = jnp.where(blk, a, 0.0)
    o = a - d
    mm = functools.partial(_ein, "bij,bjk->bik", PRECISE)
    p = eye - d
    n = mm(d, d)
    for _ in range(2):
        both = mm(jnp.concatenate([n, p], axis=1), n)
        n = both[:, :c]
        p = p + both[:, c:]
    p = p + mm(p, n)
    e = mm(p, o)
    e2 = mm(e, e)
    left = eye - e + e2 - mm(e, e2)
    return mm(left, p)


@jax.custom_vjp
def _inverse_known(a, tinv):
    return tinv


def _inverse_known_fwd(a, tinv):
    return tinv, tinv


def _inverse_known_bwd(tinv, ct):
    left = _ein("bji,bjk->bik", PRECISE, tinv, ct)
    return -_ein("bik,bjk->bij", PRECISE, left, tinv), jnp.zeros_like(tinv)


_inverse_known.defvjp(_inverse_known_fwd, _inverse_known_bwd)


def _delta_intra_math(q, k, v, bg, head, tinv_known=None):
    n = q.shape[0]
    nb = n // CHUNK
    lane = lax.broadcasted_iota(jnp.int32, bg.shape, 1)
    beta = jnp.sum(jnp.where(lane == head, bg, 0.0), axis=1, keepdims=True).reshape(nb, CHUNK, 1)
    gc = jnp.sum(jnp.where(lane == head + DN_HEADS, bg, 0.0), axis=1, keepdims=True).reshape(nb, CHUNK, 1)
    q3, k3, v3 = (a.reshape(nb, CHUNK, HEAD_DIM) for a in (q, k, v))
    r = lax.broadcasted_iota(jnp.int32, (CHUNK, CHUNK), 0)
    c = lax.broadcasted_iota(jnp.int32, (CHUNK, CHUNK), 1)
    eye = jnp.where(r == c, 1.0, 0.0).astype(F32)[None]
    gcr = _ein("bik,bkj->bij", LHS_EXACT, jnp.ones((nb, CHUNK, CHUNK), F32), gc * eye)
    decay = jnp.exp(jnp.where((r >= c)[None], gc - gcr, -1e30))
    kb = k3 * beta
    vb = v3 * beta
    egc = jnp.exp(gc)
    on_k = _ein("bcd,bmd->bcm", FAST, jnp.concatenate([kb, q3], axis=1), k3)
    a = jnp.where((r > c)[None], on_k[:, :CHUNK] * decay, 0.0)
    tinv = _tri_inverse(a) if tinv_known is None else _inverse_known(a, tinv_known)
    uw = _ein("bcm,bmd->bcd", PRECISE, tinv, jnp.concatenate([vb, kb * egc], axis=2))
    u, w = uw[:, :, :HEAD_DIM], uw[:, :, HEAD_DIM:]
    qk = on_k[:, CHUNK:] * decay
    row = lax.broadcasted_iota(jnp.int32, (nb, CHUNK, 1), 1)
    glast = jnp.sum(jnp.where(row == CHUNK - 1, gc, 0.0), axis=1, keepdims=True)
    qd = q3 * egc
    kd = k3 * jnp.exp(glast - gc)
    glb = jnp.broadcast_to(jnp.exp(glast), (nb, HALO, LANES))
    flat = lambda x: x.reshape(n, HEAD_DIM)
    return flat(u), flat(w), flat(qd), flat(kd), qk, glb, tinv


def _delta_step_math(u, w, qd, kd, qk, gl, s):
    c = u.shape[0]
    on_s = _ein("ck,kv->cv", FAST, jnp.concatenate([w, qd], axis=0), s)
    vnew = u - on_s[:c]
    on_v = _ein("cm,mv->cv", FAST, jnp.concatenate([qk, kd.T], axis=0), vnew)
    o = on_s[c:] + on_v[:c]
    s2 = s * gl + on_v[c:]
    return o, s2


def _delta_intra(qkvn, bg, nb):
    t = qkvn.shape[0]
    n = t // CHUNK
    rows = nb * CHUNK

    def body(q_ref, k_ref, v_ref, bg_ref, u_ref, w_ref, qd_ref, kd_ref, qk_ref, gl_ref, ti_ref):
        outs = _delta_intra_math(q_ref[...], k_ref[...], v_ref[...], bg_ref[...], pl.program_id(1))
        for r, v in zip((u_ref, w_ref, qd_ref, kd_ref, qk_ref, gl_ref, ti_ref), outs):
            r[...] = v

    col = lambda off: pl.BlockSpec((rows, HEAD_DIM), lambda b, h, off=off: (b, off + h))
    tok = jax.ShapeDtypeStruct((t, DN_WIDTH), F32)
    return pl.pallas_call(
        body, name="delta_intra", grid=(n // nb, DN_HEADS),
        in_specs=[col(0), col(DN_HEADS), col(2 * DN_HEADS), pl.BlockSpec((rows, LANES), lambda b, h: (b, 0))],
        out_specs=[col(0)] * 4 + [pl.BlockSpec((nb, None, CHUNK, CHUNK), lambda b, h: (b, h, 0, 0)),
                                  pl.BlockSpec((nb, None, HALO, LANES), lambda b, h: (b, h, 0, 0)),
                                  pl.BlockSpec((nb, None, CHUNK, CHUNK), lambda b, h: (b, h, 0, 0))],
        out_shape=[tok] * 4 + [jax.ShapeDtypeStruct((n, DN_HEADS, CHUNK, CHUNK), F32),
                               jax.ShapeDtypeStruct((n, DN_HEADS, HALO, LANES), F32),
                               jax.ShapeDtypeStruct((n, DN_HEADS, CHUNK, CHUNK), F32)],
        compiler_params=_params("parallel", "arbitrary"),
    )(qkvn, qkvn, qkvn, bg)


def _delta_intra_bwd(qkvn, bg, tinv, cts, nb):
    t = qkvn.shape[0]
    n = t // CHUNK
    rows = nb * CHUNK

    def body(q_ref, k_ref, v_ref, bg_ref, ti_ref, du, dw, dqd, dkd, dqk, dgl, dq_ref, dk_ref, dv_ref, dbg_ref):
        h = pl.program_id(1)
        ti = ti_ref[...]
        _, vjp = jax.vjp(lambda q, k, v, b: _delta_intra_math(q, k, v, b, h, ti)[:6],
                         q_ref[...], k_ref[...], v_ref[...], bg_ref[...])
        dq, dk, dv, dbg = vjp((du[...], dw[...], dqd[...], dkd[...], dqk[...], dgl[...]))
        dq_ref[...] = dq
        dk_ref[...] = dk
        dv_ref[...] = dv

        @pl.when(h == 0)
        def _():
            dbg_ref[...] = jnp.zeros_like(dbg_ref)

        dbg_ref[...] += dbg

    col = lambda off: pl.BlockSpec((rows, HEAD_DIM), lambda b, h, off=off: (b, off + h))
    bgs = pl.BlockSpec((rows, LANES), lambda b, h: (b, 0))
    qks = pl.BlockSpec((nb, None, CHUNK, CHUNK), lambda b, h: (b, h, 0, 0))
    gls = pl.BlockSpec((nb, None, HALO, LANES), lambda b, h: (b, h, 0, 0))
    tok = jax.ShapeDtypeStruct((t, DN_WIDTH), F32)
    return pl.pallas_call(
        body, name="delta_intra_bwd", grid=(n // nb, DN_HEADS),
        in_specs=[col(0), col(DN_HEADS), col(2 * DN_HEADS), bgs, qks, col(0), col(0), col(0), col(0), qks, gls],
        out_specs=[col(0), col(0), col(0), bgs],
        out_shape=[tok, tok, tok, jax.ShapeDtypeStruct((t, LANES), F32)],
        compiler_params=_params("parallel", "arbitrary"),
    )(qkvn, qkvn, qkvn, bg, tinv, *cts)


def _delta_scan(u, w, qd, kd, qk, glb, cb):
    t = u.shape[0]
    n = t // CHUNK
    rows = cb * CHUNK

    def body(u_ref, w_ref, qd_ref, kd_ref, qk_ref, gl_ref, o_ref, s_ref, s_scr):
        @pl.when(pl.program_id(0) == 0)
        def _():
            s_scr[...] = jnp.zeros_like(s_scr)

        def chunk(c, carry):
            r0 = pl.multiple_of(c * CHUNK, CHUNK)
            for h in range(DN_HEADS):
                sl = (pl.ds(r0, CHUNK), slice(h * HEAD_DIM, (h + 1) * HEAD_DIM))
                s = s_scr[h]
                s_ref[c, h] = s
                o, s2 = _delta_step_math(u_ref[sl], w_ref[sl], qd_ref[sl], kd_ref[sl], qk_ref[c, h],
                                         gl_ref[c, h][0:1, :], s)
                o_ref[sl] = o
                s_scr[h] = s2
            return carry

        lax.fori_loop(0, cb, chunk, 0)

    tok = pl.BlockSpec((rows, DN_WIDTH), lambda i: (i, 0))
    return pl.pallas_call(
        body, name="delta_scan", grid=(n // cb,),
        in_specs=[tok] * 4 + [pl.BlockSpec((cb, DN_HEADS, CHUNK, CHUNK), lambda i: (i, 0, 0, 0)),
                              pl.BlockSpec((cb, DN_HEADS, HALO, LANES), lambda i: (i, 0, 0, 0))],
        out_specs=[tok, pl.BlockSpec((cb, DN_HEADS, HEAD_DIM, HEAD_DIM), lambda i: (i, 0, 0, 0))],
        out_shape=[jax.ShapeDtypeStruct((t, DN_WIDTH), F32),
                   jax.ShapeDtypeStruct((n, DN_HEADS, HEAD_DIM, HEAD_DIM), F32)],
        scratch_shapes=[pltpu.VMEM((DN_HEADS, HEAD_DIM, HEAD_DIM), F32)],
        compiler_params=_params("arbitrary"),
    )(u, w, qd, kd, qk, glb)


def _delta_scan_bwd(u, w, qd, kd, qk, glb, s_all, do, cb):
    t = u.shape[0]
    n = t // CHUNK
    nblk = n // cb
    rows = cb * CHUNK

    def body(u_ref, w_ref, qd_ref, kd_ref, qk_ref, gl_ref, s_ref, do_ref,
             du_ref, dw_ref, dqd_ref, dkd_ref, dqk_ref, dgl_ref, ds_scr):
        @pl.when(pl.program_id(0) == 0)
        def _():
            ds_scr[...] = jnp.zeros_like(ds_scr)

        def chunk(step, carry):
            c = cb - 1 - step
            r0 = pl.multiple_of(c * CHUNK, CHUNK)
            for h in range(DN_HEADS):
                sl = (pl.ds(r0, CHUNK), slice(h * HEAD_DIM, (h + 1) * HEAD_DIM))
                gl_tile = gl_ref[c, h]
                prim = (u_ref[sl], w_ref[sl], qd_ref[sl], kd_ref[sl], qk_ref[c, h], gl_tile, s_ref[c, h])
                _, vjp = jax.vjp(lambda a, b, cc, d, e, g, s: _delta_step_math(a, b, cc, d, e, g[0:1, :], s), *prim)
                du, dw, dqd, dkd, dqk, dgl, ds = vjp((do_ref[sl], ds_scr[h]))
                du_ref[sl] = du
                dw_ref[sl] = dw
                dqd_ref[sl] = dqd
                dkd_ref[sl] = dkd
                dqk_ref[c, h] = dqk
                dgl_ref[c, h] = dgl
                ds_scr[h] = ds
            return carry

        lax.fori_loop(0, cb, chunk, 0)

    rev = lambda i: nblk - 1 - i
    tok = pl.BlockSpec((rows, DN_WIDTH), lambda i: (rev(i), 0))
    qks = pl.BlockSpec((cb, DN_HEADS, CHUNK, CHUNK), lambda i: (rev(i), 0, 0, 0))
    gls = pl.BlockSpec((cb, DN_HEADS, HALO, LANES), lambda i: (rev(i), 0, 0, 0))
    ss = pl.BlockSpec((cb, DN_HEADS, HEAD_DIM, HEAD_DIM), lambda i: (rev(i), 0, 0, 0))
    tshape = jax.ShapeDtypeStruct((t, DN_WIDTH), F32)
    return pl.pallas_call(
        body, name="delta_scan_bwd", grid=(nblk,),
        in_specs=[tok] * 4 + [qks, gls, ss, tok],
        out_specs=[tok] * 4 + [qks, gls],
        out_shape=[tshape] * 4 + [jax.ShapeDtypeStruct(qk.shape, F32), jax.ShapeDtypeStruct(glb.shape, F32)],
        scratch_shapes=[pltpu.VMEM((DN_HEADS, HEAD_DIM, HEAD_DIM), F32)],
        compiler_params=_params("arbitrary"),
    )(u, w, qd, kd, qk, glb, s_all, do)


def _peer(mask):
    x, y, c = lax.axis_index("x"), lax.axis_index("y"), lax.axis_index("c")
    return (x ^ ((mask >> 2) & 1), y ^ ((mask >> 1) & 1), c ^ (mask & 1))


def _my_index():
    return 4 * lax.axis_index("x") + 2 * lax.axis_index("y") + lax.axis_index("c")


class _Exchange:
    CHIP_MASKS = (4, 2, 6)

    def __init__(self, kind, arrays):
        self.kind = kind
        self.arrays = list(arrays)
        self.na = na = len(self.arrays)
        if kind == "gather":
            self.out_shape = [jax.ShapeDtypeStruct((N_DEV,) + a.shape, a.dtype) for a in self.arrays]
        else:
            self.out_shape = [jax.ShapeDtypeStruct(a.shape, a.dtype) for a in self.arrays]
        self.scratch = [pltpu.SemaphoreType.DMA((na, 7)), pltpu.SemaphoreType.DMA((na, 7)),
                        pltpu.SemaphoreType.DMA((na,))]

    def _copies(self, ins, outs, sems):
        send_sems, recv_sems, local_sems = sems
        me = _my_index()
        local, first, passed, arrivals = [], [], [], []
        if self.kind == "gather":
            def rc(a, k, block, to, own=False):
                def make():
                    dst = outs[a].at[block]
                    return pltpu.make_async_remote_copy(src_ref=ins[a] if own else dst, dst_ref=dst,
                                                        send_sem=send_sems.at[a, k], recv_sem=recv_sems.at[a, k],
                                                        device_id=to, device_id_type=MESH)
                return make

            sib = _peer(1)
            for a in range(self.na):
                local.append(lambda a=a: pltpu.make_async_copy(ins[a], outs[a].at[me], local_sems.at[a]))
                first.append(rc(a, 0, me, sib, own=True))
                arrivals.append(rc(a, 0, me ^ 1, _peer(0)))
                for j, m in enumerate(self.CHIP_MASKS):
                    first.append(rc(a, 1 + j, me, _peer(m), own=True))
                    passed.append((rc(a, 1 + j, me ^ m, _peer(0)), rc(a, 4 + j, me ^ m, sib)))
                    arrivals.append(rc(a, 4 + j, me ^ m ^ 1, _peer(0)))
        else:
            for a in range(self.na):
                local.append(lambda a=a: pltpu.make_async_copy(ins[a].at[me], outs[a].at[me], local_sems.at[a]))
                for m in range(1, N_DEV):
                    def make(a=a, m=m):
                        return pltpu.make_async_remote_copy(
                            src_ref=ins[a].at[me ^ m], dst_ref=outs[a].at[me], send_sem=send_sems.at[a, m - 1],
                            recv_sem=recv_sems.at[a, m - 1], device_id=_peer(m), device_id_type=MESH)
                    first.append(make)
                    arrivals.append(make)
        return local, first, passed, arrivals

    def start(self, ins, outs, sems):
        local, first, _, _ = self._copies(ins, outs, sems)
        for make in local + first:
            make().start()

    def finish(self, ins, outs, sems):
        local, first, passed, arrivals = self._copies(ins, outs, sems)
        for landed, onward in passed:
            landed().wait_recv()
            onward().start()
        for make in arrivals:
            make().wait_recv()
        for make in first + [p for _, p in passed]:
            make().wait_send()
        for make in local:
            make().wait()

    def run(self, name):
        na = self.na

        def body(*refs):
            ins, outs, sems = refs[:na], refs[na:2 * na], refs[2 * na:]
            self.start(ins, outs, sems)
            self.finish(ins, outs, sems)

        hbm = pl.BlockSpec(memory_space=pltpu.HBM)
        return pl.pallas_call(
            body, name=name, in_specs=[hbm] * na, out_specs=[hbm] * na, out_shape=self.out_shape,
            scratch_shapes=self.scratch, compiler_params=pltpu.CompilerParams(has_side_effects=True),
        )(*self.arrays)


def _all_gather(shards, name):
    return _Exchange("gather", shards).run(name)


def _adamw_math(w, g, m, v):
    m2 = ADAM_B1 * m + (1.0 - ADAM_B1) * g
    v2 = ADAM_B2 * v + (1.0 - ADAM_B2) * jnp.square(g)
    m_hat = m2 / (1.0 - ADAM_B1 ** ADAM_STEP)
    v_hat = v2 / (1.0 - ADAM_B2 ** ADAM_STEP)
    delta = -ADAM_LR * (m_hat / (jnp.sqrt(v_hat) + ADAM_EPS) + ADAM_WD * w)
    return delta, m2, v2


def _sum_adamw(parts, w, m, v, name):
    r, c = w.shape
    tr = _pick(r, (512, 256, 352, 128, 64, 32, 16, 8))
    np_ = parts.shape[0]

    def body(p_ref, w_ref, m_ref, v_ref, g_ref, d_ref, m2_ref, v2_ref):
        g = p_ref[0].astype(F32)
        for d in range(1, np_):
            g = g + p_ref[d].astype(F32)
        delta, m2, v2 = _adamw_math(w_ref[...], g, m_ref[...], v_ref[...])
        g_ref[...] = g
        d_ref[...] = delta
        m2_ref[...] = m2
        v2_ref[...] = v2

    blk = pl.BlockSpec((tr, c), lambda i: (i, 0))
    shp = jax.ShapeDtypeStruct((r, c), F32)
    return pl.pallas_call(
        body, name=name, grid=(r // tr,),
        in_specs=[pl.BlockSpec((np_, tr, c), lambda i: (0, i, 0)), blk, blk, blk],
        out_specs=[blk] * 4, out_shape=[shp] * 4, compiler_params=_params("parallel"),
    )(parts, w, m, v)


def _sum_rows(parts, name):
    np_, r, c = parts.shape

    def body(p_ref, o_ref):
        g = p_ref[0]
        for d in range(1, np_):
            g = g + p_ref[d]
        o_ref[...] = g

    return pl.pallas_call(body, name=name, out_shape=jax.ShapeDtypeStruct((r, c), F32))(parts)


def _pad_w_in(w):
    d = w.shape[1]
    n_ba = 2 * DN_HEADS
    a = w[:SC_OFF]
    ba = w[SC_OFF:SC_OFF + n_ba]
    sc = w[SC_OFF + n_ba:]
    return jnp.concatenate([a, sc, ba, jnp.zeros((BA_W - n_ba, d), w.dtype)], axis=0)


def _unpad_w_in(wp):
    n_ba = 2 * DN_HEADS
    return jnp.concatenate([wp[:SC_OFF], wp[BA_OFF:BA_OFF + n_ba], wp[SC_OFF:BA_OFF]], axis=0)


def _lane_row(v, off):
    return jnp.pad(v.astype(F32), (off, LANES - off - v.shape[0]))[None]


TT = 256
NB_INTRA = 8
CB_SCAN = 8


def _layer_fwd(x, p, ride_proj, ride_ffn, late):
    t, d = x.shape
    got_proj = got_ffn = None
    if ride_proj:
        h, proj, got_proj = _norm_matmul(x, p["norm1_g"], p["w_in"], "proj_fwd_gather", ride=ride_proj)
    else:
        h, proj = _norm_matmul(x, p["norm1_g"], p["w_in"], "proj_fwd")
    p = {**p, **late(got_proj)}
    qkvn, bg = _tok_call(
        "dn_pre", lambda tv, cv: (_dn_pre_math(*tv, *cv), ()), t, TT,
        [(proj, QKV_W, 0, "cur"), (proj, QKV_W, 0, "prev"), (proj, LANES, BA_OFF // LANES, "cur")],
        [p["dn_conv_w"], p["alog_row"], p["dt_row"]],
        [(TT, QKV_W, F32), (TT, LANES, F32)], [])
    u, w, qd, kd, qk, glb, tinv = _delta_intra(qkvn, bg, NB_INTRA)
    o, s_all = _delta_scan(u, w, qd, kd, qk, glb, CB_SCAN)
    cb0 = SC_OFF // SC_WIDTH
    mix_in = [(o, DN_WIDTH, 0, "cur"), (proj, DN_WIDTH, Z_OFF // DN_WIDTH, "cur"),
              (proj, SC_WIDTH, cb0, "cur"), (proj, SC_WIDTH, cb0 + 1, "cur"), (proj, SC_WIDTH, cb0 + 1, "prev"),
              (proj, SC_WIDTH, cb0 + 2, "cur"), (proj, SC_WIDTH, cb0 + 2, "prev")]
    mix_const = [p["dn_norm_g"], p["sc_norm_g"], p["sc_conv_w"]]
    (cat,) = _tok_call("mix_post", lambda tv, cv: ((_mix_math(*tv, *cv),), ()), t, TT,
                       mix_in, mix_const, [(TT, 2 * DN_WIDTH, MXU_DTYPE)], [])
    x_mid = _matmul(cat, p["w_out"], "nn", F32, "out_proj", residual=x)
    if ride_ffn:
        h2, gu, act, got_ffn = _ffn_up_swiglu(x_mid, p["norm2_g"], p["w_gu"], "ffn_up_gather", ride=ride_ffn)
    else:
        h2, gu, act = _ffn_up_swiglu(x_mid, p["norm2_g"], p["w_gu"], "ffn_up")
    x_out = _matmul(act, p["w_down"], "nn", F32, "ffn_down", residual=x_mid)
    saved = dict(x=x, h=h, proj=proj, qkvn=qkvn, bg=bg, u=u, w=w, qd=qd, kd=kd, qk=qk, glb=glb, tinv=tinv, s_all=s_all, o=o,
                 cat=cat, x_mid=x_mid, h2=h2, gu=gu, act=act, mix_in=mix_in, mix_const=mix_const)
    return x_out, saved, p, got_proj, got_ffn


def _layer_bwd(dx_out, p, s, ride_prev, ride_gu, ride_down):
    t, d = dx_out.shape
    got = {}
    dgu = _ffn_down_dx_swiglu(dx_out, p["w_down"], s["gu"], "ffn_down_dx")
    d_w_down = _matmul(s["act"], dx_out, "tn", MXU_DTYPE, "ffn_down_dw")
    if ride_prev:
        dx_mid, d_norm2, got["prev"] = _matmul_norm_bwd(dgu, p["w_gu"], s["x_mid"], p["norm2_g"], dx_out,
                                                        "ffn_up_dx_scatter", ride=ride_prev, w_rows_k=True)
    else:
        dx_mid, d_norm2 = _matmul_norm_bwd(dgu, p["w_gu"], s["x_mid"], p["norm2_g"], dx_out, "ffn_up_dx",
                                           w_rows_k=True)
    d_w_gu = _matmul(dgu, s["h2"], "tn", MXU_DTYPE, "ffn_up_dw")
    dcat = _matmul(dx_mid, p["w_out"], "nt", F32, "out_proj_dx")
    d_w_out = _matmul(s["cat"], dx_mid, "tn", MXU_DTYPE, "out_proj_dw")

    def mix_bwd(tv, cv):
        prim = tuple(tv[:7]) + tuple(cv)
        _, vjp = jax.vjp(_mix_math, *prim)
        do, dz, dgb, dgc, dgch, dhv, dhvh, ddng, dscg, dscw = vjp(tv[7])
        return (do, dz, dgb, dgc, dgch, dhv, dhvh), (ddng, dscg, dscw)

    wide = (TT, DN_WIDTH, F32)
    halo = (HALO, SC_WIDTH, F32)
    do, dz, dgb, dgc, dgc_h, dhv, dhv_h, d_dn_norm, d_sc_norm, d_sc_conv = _tok_call(
        "mix_post_bwd", mix_bwd, t, TT, s["mix_in"] + [(dcat, 2 * DN_WIDTH, 0, "cur")], s["mix_const"],
        [wide, wide, wide, wide, halo, wide, halo],
        [((1, HEAD_DIM), F32), ((1, SC_WIDTH), F32), ((SC_CONV, SC_WIDTH), F32)])
    cts = _delta_scan_bwd(s["u"], s["w"], s["qd"], s["kd"], s["qk"], s["glb"], s["s_all"], do, CB_SCAN)
    dq, dk, dv, dbg = _delta_intra_bwd(s["qkvn"], s["bg"], s["tinv"], cts, NB_INTRA)
    proj = s["proj"]

    def dn_pre_bwd(tv, cv):
        cur, hal, ba, dq_, dk_, dv_, dbg_ = tv
        _, vjp = jax.vjp(_dn_pre_math, cur, hal, ba, *cv)
        dcur, dhal, dba, dcw, dal, ddt = vjp((jnp.concatenate([dq_, dk_, dv_], axis=1), dbg_))
        return (dcur, dhal, dba), (dcw, dal, ddt)

    dqkv, dqkv_h, dba, d_dn_conv, d_alog, d_dt = _tok_call(
        "dn_pre_bwd", dn_pre_bwd, t, TT,
        [(proj, QKV_W, 0, "cur"), (proj, QKV_W, 0, "prev"), (proj, LANES, BA_OFF // LANES, "cur"),
         (dq, DN_WIDTH, 0, "cur"), (dk, DN_WIDTH, 0, "cur"), (dv, DN_WIDTH, 0, "cur"), (dbg, LANES, 0, "cur")],
        [p["dn_conv_w"], p["alog_row"], p["dt_row"]],
        [(TT, QKV_W, F32), (HALO, QKV_W, F32), (TT, LANES, F32)],
        [((DN_CONV, QKV_W), F32), ((1, LANES), F32), ((1, LANES), F32)])

    def assemble(tv, cv):
        dqkv_, dqkv_n, dz_, dgb_, dgc_, dgc_n, dhv_, dhv_n, dba_ = tv

        def with_halo(cur, nxt):
            return cur + jnp.concatenate([jnp.zeros((TT - HALO, cur.shape[1]), F32), nxt], axis=0)

        out = jnp.concatenate([with_halo(dqkv_, dqkv_n), dz_, dgb_, with_halo(dgc_, dgc_n), with_halo(dhv_, dhv_n),
                               dba_, jnp.zeros((TT, BA_W - LANES), F32)], axis=1)
        return (out,), ()

    (dproj,) = _tok_call(
        "dproj_assemble", assemble, t, TT,
        [(dqkv, QKV_W, 0, "cur"), (dqkv_h, QKV_W, 0, "next8"), (dz, DN_WIDTH, 0, "cur"), (dgb, SC_WIDTH, 0, "cur"),
         (dgc, SC_WIDTH, 0, "cur"), (dgc_h, SC_WIDTH, 0, "next8"), (dhv, SC_WIDTH, 0, "cur"),
         (dhv_h, SC_WIDTH, 0, "next8"), (dba, LANES, 0, "cur")], [],
        [(TT, PROJ_W, MXU_DTYPE)], [])
    dx_in, d_norm1, got["gu"] = _matmul_norm_bwd(dproj, p["w_in"], s["x"], p["norm1_g"], dx_mid,
                                                 "proj_dx_scatter", ride=ride_gu(d_w_gu), w_rows_k=True)
    d_w_in, got["down"] = _matmul(dproj, s["h"], "tn", MXU_DTYPE, "proj_dw_scatter", ride=ride_down(d_w_down))
    grads = dict(w_in=d_w_in, w_out=d_w_out, w_gu=d_w_gu, w_down=d_w_down, norm1_g=d_norm1, norm2_g=d_norm2,
                 dn_norm_g=d_dn_norm, sc_norm_g=d_sc_norm, sc_conv_w=d_sc_conv, dn_conv_w=d_dn_conv,
                 alog=d_alog, dt=d_dt)
    return dx_in, grads, got


def _final_loss(x, g, target):
    t, d = x.shape

    def fn(tv, cv):
        xv, tg = tv

        def loss_fn(xx, gg):
            err = jnp.square(_rms_norm(xx, gg) - tg)
            return 0.5 * jnp.sum(jnp.mean(err, axis=-1))

        loss, vjp = jax.vjp(loss_fn, xv, cv[0])
        dx, dg = vjp(jnp.ones((), F32))
        return (dx,), (jnp.full((1, LANES), loss, F32), dg)

    return _tok_call("final_loss", fn, t, TT, [(x, d, 0, "cur"), (target, d, 0, "cur")], [g],
                     [(TT, d, F32)], [((1, LANES), F32), ((1, d), F32)])


def _pack_rows(arrs):
    rows, offs, r0 = [], [], 0
    for a in arrs:
        n = a.size
        nr = -(-n // LANES)
        flat = jnp.pad(a.reshape(-1).astype(F32), (0, nr * LANES - n))
        rows.append(flat.reshape(nr, LANES))
        offs.append((r0, nr, a.shape))
        r0 += nr
    pad = (-r0) % 8
    if pad:
        rows.append(jnp.zeros((pad, LANES), F32))
    return jnp.concatenate(rows, axis=0), offs


def _unpack_rows(packed, offs):
    out = []
    for r0, nr, shp in offs:
        n = 1
        for s_ in shp:
            n *= s_
        out.append(packed[r0:r0 + nr].reshape(-1)[:n].reshape(shp))
    return out


def kernel(x, norm1_g, w_in, dn_conv_w, dn_a_log, dn_dt_bias, dn_norm_g, sc_conv_w, sc_norm_g, w_out, norm2_g, ffn_w_gate, ffn_w_up, ffn_w_down, final_norm_g, loss_target, m_norm1_g, m_w_in, m_dn_conv_w, m_dn_a_log, m_dn_dt_bias, m_dn_norm_g, m_sc_conv_w, m_sc_norm_g, m_w_out, m_norm2_g, m_ffn_w_gate, m_ffn_w_up, m_ffn_w_down, m_final_norm_g, v_norm1_g, v_w_in, v_dn_conv_w, v_dn_a_log, v_dn_dt_bias, v_dn_norm_g, v_sc_conv_w, v_sc_norm_g, v_w_out, v_norm2_g, v_ffn_w_gate, v_ffn_w_up, v_ffn_w_down, v_final_norm_g):
    depth, d, cin = w_in.shape
    t = x.shape[1]
    dff_s = ffn_w_gate.shape[2]
    tr = lambda a: a.transpose(0, 2, 1)
    gate_t, up_t = tr(ffn_w_gate), tr(ffn_w_up)
    by_layer = lambda a: [a[l].T for l in range(depth)]
    win_t = by_layer(w_in)
    me = _my_index()
    x2 = x.reshape(t, d)
    tgt = loss_target.reshape(t, d)

    conv_pack, conv_offs = _pack_rows([dn_conv_w, sc_conv_w])
    (conv_all,) = _all_gather([conv_pack], "gather_conv")
    dn_parts, sc_parts = zip(*[_unpack_rows(conv_all[j], conv_offs) for j in range(N_DEV)])
    dn_conv_full = jnp.concatenate(dn_parts, axis=2)
    sc_conv_full = jnp.concatenate(sc_parts, axis=2)

    def shards(l):
        return [a[l].astype(MXU_DTYPE) for a in (win_t, w_out, gate_t, up_t, ffn_w_down)]

    def mixer_params(l, g_in, g_out):
        return dict(
            w_in=_pad_w_in(g_in.reshape(N_DEV * cin, d)), w_out=g_out.reshape(d, d),
            norm1_g=norm1_g[l][None], norm2_g=norm2_g[l][None], dn_norm_g=dn_norm_g[l][None],
            sc_norm_g=sc_norm_g[l][None], dn_conv_w=dn_conv_full[l], sc_conv_w=sc_conv_full[l],
            alog_row=_lane_row(dn_a_log[l], DN_HEADS), dt_row=_lane_row(dn_dt_bias[l], DN_HEADS))

    def ffn_params(g_gate, g_up, g_down):
        dff = N_DEV * dff_s
        return dict(w_gu=_interleave_gu(g_gate.reshape(dff, d), g_up.reshape(dff, d)), w_down=g_down.reshape(dff, d))

    nxt = mixer_params(0, *_all_gather(shards(0)[:2], "gather_first"))
    nxt_ffn = None
    params, saved = [], []
    xc = x2
    for l in range(depth):
        last = l + 1 == depth
        own_ffn = shards(l)[2:] if nxt_ffn is None else []
        ahead = [] if last else shards(l + 1)
        ride_proj = _Exchange("gather", own_ffn + ahead[:2]) if own_ffn or ahead else None
        ride_ffn = None if last else _Exchange("gather", ahead[2:])

        def late(got, own_ffn=own_ffn, nxt_ffn=nxt_ffn):
            return ffn_params(*got[:3]) if own_ffn else nxt_ffn

        xc, s, p_l, got_proj, got_ffn = _layer_fwd(xc, nxt, ride_proj, ride_ffn, late)
        params.append(p_l)
        saved.append(s)
        if not last:
            nxt = mixer_params(l + 1, *got_proj[len(own_ffn):])
            nxt_ffn = ffn_params(*got_ffn)
    dx, loss_part, d_final = _final_loss(xc, final_norm_g[None], tgt)

    names = ("w_in", "w_out", "ffn_w_gate", "ffn_w_up", "ffn_w_down")
    big_out = {n: {k: [None] * depth for k in ("g", "d", "m", "v")} for n in names}
    w_loc = dict(w_in=win_t, w_out=w_out, ffn_w_gate=gate_t, ffn_w_up=up_t, ffn_w_down=ffn_w_down)
    m_loc = dict(w_in=by_layer(m_w_in), w_out=m_w_out, ffn_w_gate=tr(m_ffn_w_gate), ffn_w_up=tr(m_ffn_w_up), ffn_w_down=m_ffn_w_down)
    v_loc = dict(w_in=by_layer(v_w_in), w_out=v_w_out, ffn_w_gate=tr(v_ffn_w_gate), ffn_w_up=tr(v_ffn_w_up), ffn_w_down=v_ffn_w_down)


    def ride_gu(d_w_gu):
        rows = lambda a: a.reshape(N_DEV, dff_s, d).astype(MXU_DTYPE)
        g_gate, g_up = _split_gu(d_w_gu)
        return _Exchange("scatter", [rows(g_gate), rows(g_up)])

    def ride_down(d_w_down):
        return _Exchange("scatter", [d_w_down.reshape(N_DEV, dff_s, d).astype(MXU_DTYPE)])

    def apply(l, which, recv):
        for n, r in zip(which, recv):
            res = _sum_adamw(r, w_loc[n][l], m_loc[n][l], v_loc[n][l], "adamw_" + n)
            for k, a in zip(("g", "d", "m", "v"), res):
                big_out[n][k][l] = a

    grads = [None] * depth
    pending = None
    for l in reversed(range(depth)):
        ride_prev = _Exchange("scatter", pending) if pending else None
        dx, grads[l], got = _layer_bwd(dx, params[l], saved[l], ride_prev, ride_gu, ride_down)
        if ride_prev:
            apply(l + 1, names[:2], got["prev"])
        apply(l, names[2:], [*got["gu"], *got["down"]])
        pending = [_unpad_w_in(grads[l]["w_in"]).reshape(N_DEV, cin, d).astype(MXU_DTYPE),
                   grads[l]["w_out"].reshape(N_DEV, d // N_DEV, d).astype(MXU_DTYPE)]
    apply(0, names[:2], _Exchange("scatter", pending).run("scatter_last"))
    grad_x = dx.reshape(x.shape)
    big_out = {n: {k: jnp.stack(v_, axis=1 if n == "w_in" else 0) for k, v_ in o.items()} for n, o in big_out.items()}
    for n in ("ffn_w_gate", "ffn_w_up"):
        big_out[n] = {k: tr(a) for k, a in big_out[n].items()}
    big_out["w_in"] = {k: a.transpose(1, 2, 0) for k, a in big_out["w_in"].items()}

    stack = lambda key: jnp.stack([grads[l][key] for l in range(depth)])
    small_parts = [stack("norm1_g").reshape(depth, d), stack("norm2_g").reshape(depth, d), d_final.reshape(d),
                   stack("dn_norm_g").reshape(depth, HEAD_DIM), stack("sc_norm_g").reshape(depth, SC_WIDTH),
                   stack("alog").reshape(depth, LANES), stack("dt").reshape(depth, LANES),
                   stack("dn_conv_w"), stack("sc_conv_w"), loss_part]
    small_pack, small_offs = _pack_rows(small_parts)
    (small_all,) = _all_gather([small_pack], "gather_small")
    total = _sum_rows(small_all, "sum_small")
    (g_n1, g_n2, g_fin, g_dnn, g_scn, g_alog, g_dt, g_dnc, g_scc, loss_row) = _unpack_rows(total, small_offs)
    loss = loss_row[0, 0]
    g_alog = g_alog[:, DN_HEADS:2 * DN_HEADS]
    g_dt = g_dt[:, DN_HEADS:2 * DN_HEADS]
    dnc_w = dn_conv_w.shape[2]
    scc_w = sc_conv_w.shape[2]
    g_dnc = lax.dynamic_slice_in_dim(g_dnc, me * dnc_w, dnc_w, axis=2)
    g_scc = lax.dynamic_slice_in_dim(g_scc, me * scc_w, scc_w, axis=2)
    sm_g = [g_n1, g_dnc, g_alog, g_dt, g_dnn, g_scc, g_scn, g_n2, g_fin]
    sm_w = [norm1_g, dn_conv_w, dn_a_log, dn_dt_bias, dn_norm_g, sc_conv_w, sc_norm_g, norm2_g, final_norm_g]
    sm_m = [m_norm1_g, m_dn_conv_w, m_dn_a_log, m_dn_dt_bias, m_dn_norm_g, m_sc_conv_w, m_sc_norm_g, m_norm2_g, m_final_norm_g]
    sm_v = [v_norm1_g, v_dn_conv_w, v_dn_a_log, v_dn_dt_bias, v_dn_norm_g, v_sc_conv_w, v_sc_norm_g, v_norm2_g, v_final_norm_g]
    pg, offs = _pack_rows(sm_g)
    pw, _ = _pack_rows(sm_w)
    pm, _ = _pack_rows(sm_m)
    pv, _ = _pack_rows(sm_v)
    sg, sd, sm_, sv = _sum_adamw(pg[None], pw, pm, pv, "adamw_small")
    small_out = {k: _unpack_rows(a, offs) for k, a in zip(("g", "d", "m", "v"), (sg, sd, sm_, sv))}

    def outputs(k):
        s_ = small_out[k]
        b = big_out
        return [s_[0], b["w_in"][k], s_[1], s_[2], s_[3], s_[4], s_[5], s_[6], b["w_out"][k], s_[7],
                b["ffn_w_gate"][k], b["ffn_w_up"][k], b["ffn_w_down"][k], s_[8]]

    return (loss, grad_x, *outputs("g"), *outputs("d"), *outputs("m"), *outputs("v"))
```

```python
import functools

import jax
import jax.numpy as jnp
from jax import lax
from jax.experimental import pallas as pl
from jax.experimental.pallas import tpu as pltpu

F32 = jnp.float32
MXU_DTYPE = jnp.bfloat16
MESH = pl.DeviceIdType.MESH

N_DEV = 8
EPS = 1e-6
DN_HEADS = 4
HEAD_DIM = 128
DN_WIDTH = DN_HEADS * HEAD_DIM
SC_WIDTH = 512
SC_GROUPS = 4
DN_CONV = 4
SC_CONV = 3
CHUNK = 64
HALO = 8
LANES = 128

QKV_W = 3 * DN_WIDTH
Z_OFF = QKV_W
SC_OFF = Z_OFF + DN_WIDTH
BA_OFF = SC_OFF + 3 * SC_WIDTH
BA_W = 256
PROJ_W = BA_OFF + BA_W

ADAM_LR = 0.001
ADAM_B1 = 0.9
ADAM_B2 = 0.999
ADAM_EPS = 1e-08
ADAM_WD = 0.01
ADAM_STEP = 10


def _pick(n, cands):
    for c in cands:
        if n % c == 0:
            return c
    return n


def _params(*sem):
    return pltpu.CompilerParams(dimension_semantics=sem)


def _rms_norm(x, g):
    return x * lax.rsqrt(jnp.mean(x * x, axis=-1, keepdims=True) + EPS) * g


def _dot(a, b, dims=(((1,), (0,)), ((), ()))):
    return lax.dot_general(a.astype(MXU_DTYPE), b.astype(MXU_DTYPE), dims, preferred_element_type=F32)


def _split_terms(x, terms):
    out = []
    for _ in range(terms):
        hi = x.astype(MXU_DTYPE)
        out.append(hi)
        x = x - hi.astype(F32)
    return out


def _ein_impl(spec, terms, a, b):
    ta, tb = terms
    if ta == 1 and tb == 1:
        return jnp.einsum(spec, a.astype(MXU_DTYPE), b.astype(MXU_DTYPE), preferred_element_type=F32)
    pa, pb = _split_terms(a, ta), _split_terms(b, tb)
    order = max(ta, tb) - 1
    acc = None
    for deg in range(order, -1, -1):
        for i in range(ta):
            j = deg - i
            if 0 <= j < tb:
                t = jnp.einsum(spec, pa[i], pb[j], preferred_element_type=F32)
                acc = t if acc is None else acc + t
    return acc


@functools.partial(jax.custom_vjp, nondiff_argnums=(0, 1))
def _ein(spec, terms, a, b):
    return _ein_impl(spec, terms, a, b)


def _ein_fwd(spec, terms, a, b):
    return _ein_impl(spec, terms, a, b), (a, b)


def _ein_bwd(spec, terms, res, ct):
    a, b = res
    xy, z = spec.split("->")
    x, y = xy.split(",")
    tc = min(max(terms), 2)
    da = _ein_impl(f"{z},{y}->{x}", (tc, terms[1]), ct, b)
    db = _ein_impl(f"{x},{z}->{y}", (terms[0], tc), a, ct)
    return da, db


_ein.defvjp(_ein_fwd, _ein_bwd)

FAST = (1, 1)
PRECISE = (2, 2)
LHS_EXACT = (1, 3)


def _causal_conv(cur, halo, w, k):
    tt = cur.shape[0]
    xp = jnp.concatenate([halo, cur], axis=0)
    y = None
    for j in range(k):
        start = HALO - (k - 1) + j
        term = xp[start:start + tt] * w[j:j + 1]
        y = term if y is None else y + term
    return y


def _matmul(a, b, mode, out_dtype, name, residual=None, ride=None):
    if mode == "nn":
        (m, k), (k2, n) = a.shape, b.shape
    elif mode == "nt":
        (m, k), (n, k2) = a.shape, b.shape
    else:
        (k, m), (k2, n) = a.shape, b.shape
    assert k == k2
    tm = _pick(m, (1024, 1408, 1280, 512, 256, 128))
    tn = _pick(n, (1280, 1408, 1024, 512, 256, 128))
    tk = k if k <= 2816 else _pick(k, (1024, 768, 512))
    gi, gj, nk = m // tm, n // tn, k // tk
    dims = {"nn": (((1,), (0,)), ((), ())), "nt": (((1,), (1,)), ((), ())), "tn": (((0,), (0,)), ((), ()))}[mode]
    a_spec = {"nn": pl.BlockSpec((tm, tk), lambda i, j, q: (i, q)),
              "nt": pl.BlockSpec((tm, tk), lambda i, j, q: (i, q)),
              "tn": pl.BlockSpec((tk, tm), lambda i, j, q: (q, i))}[mode]
    b_spec = {"nn": pl.BlockSpec((tk, tn), lambda i, j, q: (q, j)),
              "nt": pl.BlockSpec((tn, tk), lambda i, j, q: (j, q)),
              "tn": pl.BlockSpec((tk, tn), lambda i, j, q: (q, j))}[mode]
    o_spec = pl.BlockSpec((tm, tn), lambda i, j, q: (i, j))
    has_res = residual is not None
    n_in = 3 if has_res else 2
    nr = ride.na if ride else 0

    def body(*refs):
        a_ref, b_ref = refs[0], refs[1]
        r_ref = refs[2] if has_res else None
        ride_in = refs[n_in:n_in + nr]
        o_ref = refs[n_in + nr]
        ride_out = refs[n_in + nr + 1:n_in + 2 * nr + 1]
        acc = refs[n_in + 2 * nr + 1]
        ride_sems = refs[n_in + 2 * nr + 2:]
        i, j, q = pl.program_id(0), pl.program_id(1), pl.program_id(2)
        if ride:
            @pl.when((i == 0) & (j == 0) & (q == 0))
            def _():
                ride.start(ride_in, ride_out, ride_sems)

        @pl.when(q == 0)
        def _():
            acc[...] = jnp.zeros_like(acc)

        acc[...] += _dot(a_ref[...], b_ref[...], dims)

        @pl.when(q == nk - 1)
        def _():
            r = acc[...]
            if has_res:
                r = r + r_ref[...]
            o_ref[...] = r.astype(o_ref.dtype)

        if ride:
            @pl.when((i == gi - 1) & (j == gj - 1) & (q == nk - 1))
            def _():
                ride.finish(ride_in, ride_out, ride_sems)

    hbm = pl.BlockSpec(memory_space=pltpu.HBM)
    in_specs = [a_spec, b_spec] + ([o_spec] if has_res else []) + [hbm] * nr
    args = (a, b) + ((residual,) if has_res else ()) + (tuple(ride.arrays) if ride else ())
    res = pl.pallas_call(
        body, name=name, grid=(gi, gj, nk), in_specs=in_specs, out_specs=[o_spec] + [hbm] * nr,
        out_shape=[jax.ShapeDtypeStruct((m, n), out_dtype)] + (ride.out_shape if ride else []),
        scratch_shapes=[pltpu.VMEM((tm, tn), F32)] + (ride.scratch if ride else []),
        compiler_params=pltpu.CompilerParams(
            dimension_semantics=("arbitrary",) * 3 if ride else ("parallel", "parallel", "arbitrary"),
            has_side_effects=bool(ride)),
    )(*args)
    return (res[0], res[1:]) if ride else res[0]


def _norm_matmul(x, g, w, name, ride=None):
    t, d = x.shape
    n = w.shape[0]
    tm = _pick(t, (1024, 512, 256, 128))
    tn = _pick(n, (1280, 1408, 1024, 512, 256, 128))
    gi, gj = t // tm, n // tn
    nr = ride.na if ride else 0

    def body(*refs):
        x_ref, g_ref, w_ref = refs[:3]
        ride_in = refs[3:3 + nr]
        h_ref, y_ref = refs[3 + nr], refs[4 + nr]
        ride_out = refs[5 + nr:5 + 2 * nr]
        h_scr = refs[5 + 2 * nr]
        ride_sems = refs[6 + 2 * nr:]
        i, j = pl.program_id(0), pl.program_id(1)
        if ride:
            @pl.when((i == 0) & (j == 0))
            def _():
                ride.start(ride_in, ride_out, ride_sems)

        @pl.when(j == 0)
        def _():
            h = _rms_norm(x_ref[...], g_ref[...]).astype(MXU_DTYPE)
            h_scr[...] = h
            h_ref[...] = h

        y_ref[...] = _dot(h_scr[...], w_ref[...], (((1,), (1,)), ((), ())))

        if ride:
            @pl.when((i == gi - 1) & (j == gj - 1))
            def _():
                ride.finish(ride_in, ride_out, ride_sems)

    hbm = pl.BlockSpec(memory_space=pltpu.HBM)
    res = pl.pallas_call(
        body, name=name, grid=(gi, gj),
        in_specs=[pl.BlockSpec((tm, d), lambda i, j: (i, 0)), pl.BlockSpec((1, d), lambda i, j: (0, 0)),
                  pl.BlockSpec((tn, d), lambda i, j: (j, 0))] + [hbm] * nr,
        out_specs=[pl.BlockSpec((tm, d), lambda i, j: (i, 0)), pl.BlockSpec((tm, tn), lambda i, j: (i, j))] + [hbm] * nr,
        out_shape=[jax.ShapeDtypeStruct((t, d), MXU_DTYPE), jax.ShapeDtypeStruct((t, n), F32)]
        + (ride.out_shape if ride else []),
        scratch_shapes=[pltpu.VMEM((tm, d), MXU_DTYPE)] + (ride.scratch if ride else []),
        compiler_params=pltpu.CompilerParams(
            dimension_semantics=("arbitrary",) * 2 if ride else ("parallel", "arbitrary"),
            has_side_effects=bool(ride)),
    )(x, g, w, *(ride.arrays if ride else ()))
    return (res[0], res[1], res[2:]) if ride else (res[0], res[1])


def _swiglu_math(g, u):
    return jax.nn.silu(g) * u


def _gu_tile(dff):
    return _pick(dff, (1408, 1024, 512, 256, 128))


def _interleave_gu(gate_t, up_t):
    tn = _gu_tile(gate_t.shape[0])
    pieces = []
    for j in range(gate_t.shape[0] // tn):
        pieces += [gate_t[j * tn:(j + 1) * tn], up_t[j * tn:(j + 1) * tn]]
    return jnp.concatenate(pieces, axis=0)


def _split_gu(gu_t):
    dff = gu_t.shape[0] // 2
    tn = _gu_tile(dff)
    tiles = [gu_t[j * tn:(j + 1) * tn] for j in range(2 * dff // tn)]
    return jnp.concatenate(tiles[0::2], axis=0), jnp.concatenate(tiles[1::2], axis=0)


def _ffn_up_swiglu(x, g, w_gu, name, ride=None):
    t, d = x.shape
    dff = w_gu.shape[0] // 2
    tn = _gu_tile(dff)
    tm = _pick(t, (512, 256, 128))
    gj, gi = dff // tn, t // tm
    nr = ride.na if ride else 0

    def body(*refs):
        x_ref, g_ref, w_ref = refs[:3]
        ride_in = refs[3:3 + nr]
        h_ref, gu_ref, act_ref = refs[3 + nr:6 + nr]
        ride_out = refs[6 + nr:6 + 2 * nr]
        ride_sems = refs[6 + 2 * nr:]
        j, i = pl.program_id(0), pl.program_id(1)
        if ride:
            @pl.when((i == 0) & (j == 0))
            def _():
                ride.start(ride_in, ride_out, ride_sems)

        h = _rms_norm(x_ref[...], g_ref[...]).astype(MXU_DTYPE)

        @pl.when(j == 0)
        def _():
            h_ref[...] = h

        y = _dot(h, w_ref[...], (((1,), (1,)), ((), ())))
        gu_ref[...] = y.astype(gu_ref.dtype)
        act_ref[...] = _swiglu_math(y[:, :tn], y[:, tn:]).astype(act_ref.dtype)

        if ride:
            @pl.when((i == gi - 1) & (j == gj - 1))
            def _():
                ride.finish(ride_in, ride_out, ride_sems)

    hbm = pl.BlockSpec(memory_space=pltpu.HBM)
    res = pl.pallas_call(
        body, name=name, grid=(gj, gi),
        in_specs=[pl.BlockSpec((tm, d), lambda j, i: (i, 0)), pl.BlockSpec((1, d), lambda j, i: (0, 0)),
                  pl.BlockSpec((2 * tn, d), lambda j, i: (j, 0))] + [hbm] * nr,
        out_specs=[pl.BlockSpec((tm, d), lambda j, i: (jnp.where(j == 0, i, gi - 1), 0)),
                   pl.BlockSpec((tm, 2 * tn), lambda j, i: (i, j)),
                   pl.BlockSpec((tm, tn), lambda j, i: (i, j))] + [hbm] * nr,
        out_shape=[jax.ShapeDtypeStruct((t, d), MXU_DTYPE), jax.ShapeDtypeStruct((t, 2 * dff), MXU_DTYPE),
                   jax.ShapeDtypeStruct((t, dff), MXU_DTYPE)] + (ride.out_shape if ride else []),
        scratch_shapes=(ride.scratch if ride else []),
        compiler_params=pltpu.CompilerParams(dimension_semantics=("arbitrary", "arbitrary"),
                                             has_side_effects=bool(ride)),
    )(x, g, w_gu, *(ride.arrays if ride else ()))
    return (res[0], res[1], res[2], res[3:]) if ride else tuple(res)


def _ffn_down_dx_swiglu(dx_out, w_down, gu, name):
    t, d = dx_out.shape
    dff = w_down.shape[0]
    tn = _gu_tile(dff)
    tm = _pick(t, (512, 256, 128))

    def body(dx_ref, w_ref, gu_ref, o_ref):
        dact = _dot(dx_ref[...], w_ref[...], (((1,), (1,)), ((), ())))
        gu_v = gu_ref[...].astype(F32)
        _, vjp = jax.vjp(_swiglu_math, gu_v[:, :tn], gu_v[:, tn:])
        dg, du = vjp(dact)
        o_ref[...] = jnp.concatenate([dg, du], axis=1).astype(o_ref.dtype)

    return pl.pallas_call(
        body, name=name, grid=(dff // tn, t // tm),
        in_specs=[pl.BlockSpec((tm, d), lambda j, i: (i, 0)), pl.BlockSpec((tn, d), lambda j, i: (j, 0)),
                  pl.BlockSpec((tm, 2 * tn), lambda j, i: (i, j))],
        out_specs=pl.BlockSpec((tm, 2 * tn), lambda j, i: (i, j)),
        out_shape=jax.ShapeDtypeStruct((t, 2 * dff), MXU_DTYPE),
        compiler_params=_params("parallel", "parallel"),
    )(dx_out, w_down, gu)


def _matmul_norm_bwd(dy, w, x, g, dres, name, ride=None, w_rows_k=False):
    t, k = dy.shape
    d = w.shape[1] if w_rows_k else w.shape[0]
    tm = _pick(t, (1024, 512, 256, 128))
    tk = k if k <= 2816 else _pick(k, (1408, 1280, 1024, 768, 512))
    tr = _pick(tm, (256, 128))
    gi, nk = t // tm, k // tk
    nr = ride.na if ride else 0

    def body(*refs):
        dy_ref, w_ref, x_ref, g_ref, dres_ref = refs[:5]
        ride_in = refs[5:5 + nr]
        dx_ref, dg_ref = refs[5 + nr], refs[6 + nr]
        ride_out = refs[7 + nr:7 + 2 * nr]
        acc = refs[7 + 2 * nr]
        ride_sems = refs[8 + 2 * nr:]
        i, q = pl.program_id(0), pl.program_id(1)

        @pl.when((i == 0) & (q == 0))
        def _():
            dg_ref[...] = jnp.zeros_like(dg_ref)
            if ride:
                ride.start(ride_in, ride_out, ride_sems)

        @pl.when(q == 0)
        def _():
            acc[...] = jnp.zeros_like(acc)

        acc[...] += _dot(dy_ref[...], w_ref[...], (((1,), (0 if w_rows_k else 1,)), ((), ())))

        @pl.when(q == nk - 1)
        def _():
            for r in range(tm // tr):
                rows = slice(r * tr, (r + 1) * tr)
                _, vjp = jax.vjp(_rms_norm, x_ref[rows], g_ref[...])
                dxn, dg = vjp(acc[rows])
                dx_ref[rows] = dres_ref[rows] + dxn
                dg_ref[...] += dg

        if ride:
            @pl.when((i == gi - 1) & (q == nk - 1))
            def _():
                ride.finish(ride_in, ride_out, ride_sems)

    hbm = pl.BlockSpec(memory_space=pltpu.HBM)
    row = pl.BlockSpec((tm, d), lambda i, q: (i, 0))
    res = pl.pallas_call(
        body, name=name, grid=(gi, nk),
        in_specs=[pl.BlockSpec((tm, tk), lambda i, q: (i, q)),
                  pl.BlockSpec((tk, d), lambda i, q: (q, 0)) if w_rows_k else pl.BlockSpec((d, tk), lambda i, q: (0, q)),
                  row,
                  pl.BlockSpec((1, d), lambda i, q: (0, 0)), row] + [hbm] * nr,
        out_specs=[row, pl.BlockSpec((1, d), lambda i, q: (0, 0))] + [hbm] * nr,
        out_shape=[jax.ShapeDtypeStruct((t, d), F32), jax.ShapeDtypeStruct((1, d), F32)]
        + (ride.out_shape if ride else []),
        scratch_shapes=[pltpu.VMEM((tm, d), F32)] + (ride.scratch if ride else []),
        compiler_params=pltpu.CompilerParams(dimension_semantics=("arbitrary", "arbitrary"),
                                             has_side_effects=bool(ride)),
    )(dy, w, x, g, dres, *(ride.arrays if ride else ()))
    return (res[0], res[1], res[2:]) if ride else (res[0], res[1])


def _tok_call(name, fn, t, tt, tok_in, const_in, tok_out, acc_out, carry=()):
    nblk = t // tt
    hb = tt // HALO
    blk = (lambda i: nblk - 1 - i) if carry else (lambda i: i)
    in_specs, args = [], []
    for arr, w, cb, mode in tok_in:
        if mode == "cur":
            spec = pl.BlockSpec((tt, w), lambda i, cb=cb: (blk(i), cb))
        elif mode == "prev":
            spec = pl.BlockSpec((HALO, w), lambda i, cb=cb: (jnp.maximum(blk(i) * hb - 1, 0), cb))
        else:
            spec = pl.BlockSpec((HALO, w), lambda i, cb=cb: (jnp.minimum(blk(i) + 1, nblk - 1), cb))
        in_specs.append(spec)
        args.append(arr)
    for arr in const_in:
        in_specs.append(pl.BlockSpec(arr.shape, lambda i: (0, 0)))
        args.append(arr)
    out_specs, out_shape = [], []
    for rows, w, dt in tok_out:
        out_specs.append(pl.BlockSpec((rows, w), lambda i: (blk(i), 0)))
        out_shape.append(jax.ShapeDtypeStruct((nblk * rows, w), dt))
    for shp, dt in acc_out:
        out_specs.append(pl.BlockSpec(shp, lambda i: (0, 0)))
        out_shape.append(jax.ShapeDtypeStruct(shp, dt))
    n_tok, n_const, n_out, n_acc = len(tok_in), len(const_in), len(tok_out), len(acc_out)

    def body(*refs):
        i = pl.program_id(0)
        b = blk(i)
        tok_vals = []
        for (_, _, _, mode), r in zip(tok_in, refs[:n_tok]):
            v = r[...]
            if mode == "prev":
                v = jnp.where(b > 0, v, jnp.zeros_like(v))
            elif mode == "next8":
                v = jnp.where(b < nblk - 1, v, jnp.zeros_like(v))
            tok_vals.append(v)
        const_vals = [r[...] for r in refs[n_tok:n_tok + n_const]]
        o_refs = refs[n_tok + n_const:n_tok + n_const + n_out]
        a_refs = refs[n_tok + n_const + n_out:n_tok + n_const + n_out + n_acc]
        c_refs = refs[n_tok + n_const + n_out + n_acc:]
        if a_refs or c_refs:
            @pl.when(i == 0)
            def _():
                for r in (*a_refs, *c_refs):
                    r[...] = jnp.zeros_like(r)

        if c_refs:
            outs, accs, carried = fn(tok_vals, const_vals, [r[...] for r in c_refs])
            for r, v in zip(c_refs, carried):
                r[...] = v
        else:
            outs, accs = fn(tok_vals, const_vals)
        for r, v in zip(o_refs, outs):
            r[...] = v.astype(r.dtype)
        for r, v in zip(a_refs, accs):
            r[...] += v.astype(r.dtype)

    res = pl.pallas_call(
        body, name=name, grid=(nblk,), in_specs=in_specs, out_specs=out_specs, out_shape=out_shape,
        scratch_shapes=[pltpu.VMEM(shp, dt) for shp, dt in carry],
        compiler_params=_params("arbitrary" if acc_out or carry else "parallel"),
    )(*args)
    return res


def _dn_pre_math(cur, halo, ba, cw, alog, dtb):
    tt = cur.shape[0]
    a = jax.nn.silu(_causal_conv(cur, halo, cw, DN_CONV))
    pieces = []
    for p in range(2 * DN_HEADS):
        xh = a[:, p * HEAD_DIM:(p + 1) * HEAD_DIM]
        xh = xh * lax.rsqrt(jnp.sum(xh * xh, axis=-1, keepdims=True) + EPS)
        if p < DN_HEADS:
            xh = xh * (HEAD_DIM ** -0.5)
        pieces.append(xh)
    pieces.append(a[:, 2 * DN_WIDTH:])
    qkvn = jnp.concatenate(pieces, axis=1)
    lane = lax.broadcasted_iota(jnp.int32, ba.shape, 1)
    raw = jnp.where(lane < DN_HEADS, jax.nn.sigmoid(ba), -jnp.exp(alog) * jax.nn.softplus(ba + dtb))
    r = lax.broadcasted_iota(jnp.int32, (tt, tt), 0)
    c = lax.broadcasted_iota(jnp.int32, (tt, tt), 1)
    tri = jnp.where((r // CHUNK == c // CHUNK) & (c <= r), 1.0, 0.0).astype(F32)
    cums = _ein("ij,jk->ik", LHS_EXACT, tri, raw)
    bg = jnp.where(lane < DN_HEADS, raw, cums)
    return qkvn, bg


def _mix_math(o, z, gb, gc, gc_halo, hv, hv_halo, dng, scg, scw):
    outs = []
    for h in range(DN_HEADS):
        sl = slice(h * HEAD_DIM, (h + 1) * HEAD_DIM)
        oh = o[:, sl]
        outs.append(oh * lax.rsqrt(jnp.mean(oh * oh, axis=-1, keepdims=True) + EPS) * dng * jax.nn.silu(z[:, sl]))
    y = gb * _causal_conv(gc * hv, gc_halo * hv_halo, scw, SC_CONV)
    gw = SC_WIDTH // SC_GROUPS
    for g in range(SC_GROUPS):
        sl = slice(g * gw, (g + 1) * gw)
        yg = y[:, sl]
        outs.append(yg * lax.rsqrt(jnp.mean(yg * yg, axis=-1, keepdims=True) + EPS) * scg[:, sl])
    return jnp.concatenate(outs, axis=1)


def _tri_inverse(a):
    c = a.shape[-1]
    r = lax.broadcasted_iota(jnp.int32, (c, c), 0)
    q = lax.broadcasted_iota(jnp.int32, (c, c), 1)
    eye = jnp.where(r == q, 1.0, 0.0).astype(F32)[None]
    blk = (r // 16 == q // 16)[None]
    d = jnp.where(blk, a, 0.0)
    o = a - d
    mm = functools.partial(_ein, "bij,bjk->bik", PRECISE)
    p = eye - d
    n = mm(d, d)
    for _ in range(2):
        both = mm(jnp.concatenate([n, p], axis=1), n)
        n = both[:, :c]
        p = p + both[:, c:]
    p = p + mm(p, n)
    e = mm(p, o)
    e2 = mm(e, e)
    left = eye - e + e2 - mm(e, e2)
    return mm(left, p)


@jax.custom_vjp
def _inverse_known(a, tinv):
    return tinv


def _inverse_known_fwd(a, tinv):
    return tinv, tinv


def _inverse_known_bwd(tinv, ct):
    left = _ein("bji,bjk->bik", PRECISE, tinv, ct)
    return -_ein("bik,bjk->bij", PRECISE, left, tinv), jnp.zeros_like(tinv)


_inverse_known.defvjp(_inverse_known_fwd, _inverse_known_bwd)


def _delta_intra_math(q, k, v, bg, head, tinv_known=None):
    n = q.shape[0]
    nb = n // CHUNK
    lane = lax.broadcasted_iota(jnp.int32, bg.shape, 1)
    beta = jnp.sum(jnp.where(lane == head, bg, 0.0), axis=1, keepdims=True).reshape(nb, CHUNK, 1)
    gc = jnp.sum(jnp.where(lane == head + DN_HEADS, bg, 0.0), axis=1, keepdims=True).reshape(nb, CHUNK, 1)
    q3, k3, v3 = (a.reshape(nb, CHUNK, HEAD_DIM) for a in (q, k, v))
    r = lax.broadcasted_iota(jnp.int32, (CHUNK, CHUNK), 0)
    c = lax.broadcasted_iota(jnp.int32, (CHUNK, CHUNK), 1)
    eye = jnp.where(r == c, 1.0, 0.0).astype(F32)[None]
    gcr = _ein("bik,bkj->bij", LHS_EXACT, jnp.ones((nb, CHUNK, CHUNK), F32), gc * eye)
    decay = jnp.exp(jnp.where((r >= c)[None], gc - gcr, -1e30))
    kb = k3 * beta
    vb = v3 * beta
    egc = jnp.exp(gc)
    on_k = _ein("bcd,bmd->bcm", FAST, jnp.concatenate([kb, q3], axis=1), k3)
    a = jnp.where((r > c)[None], on_k[:, :CHUNK] * decay, 0.0)
    tinv = _tri_inverse(a) if tinv_known is None else _inverse_known(a, tinv_known)
    uw = _ein("bcm,bmd->bcd", PRECISE, tinv, jnp.concatenate([vb, kb * egc], axis=2))
    u, w = uw[:, :, :HEAD_DIM], uw[:, :, HEAD_DIM:]
    qk = on_k[:, CHUNK:] * decay
    row = lax.broadcasted_iota(jnp.int32, (nb, CHUNK, 1), 1)
    glast = jnp.sum(jnp.where(row == CHUNK - 1, gc, 0.0), axis=1, keepdims=True)
    qd = q3 * egc
    kd = k3 * jnp.exp(glast - gc)
    glb = jnp.broadcast_to(jnp.exp(glast), (nb, HALO, LANES))
    flat = lambda x: x.reshape(n, HEAD_DIM)
    return flat(u), flat(w), flat(qd), flat(kd), qk, glb, tinv


def _delta_step_math(u, w, qd, kd, qk, gl, s):
    c = u.shape[0]
    on_s = _ein("ck,kv->cv", FAST, jnp.concatenate([w, qd], axis=0), s)
    vnew = u - on_s[:c]
    on_v = _ein("cm,mv->cv", FAST, jnp.concatenate([qk, kd.T], axis=0), vnew)
    o = on_s[c:] + on_v[:c]
    s2 = s * gl + on_v[c:]
    return o, s2


def _delta_intra(qkvn, bg, nb):
    t = qkvn.shape[0]
    n = t // CHUNK
    rows = nb * CHUNK

    def body(q_ref, k_ref, v_ref, bg_ref, u_ref, w_ref, qd_ref, kd_ref, qk_ref, gl_ref, ti_ref):
        outs = _delta_intra_math(q_ref[...], k_ref[...], v_ref[...], bg_ref[...], pl.program_id(1))
        for r, v in zip((u_ref, w_ref, qd_ref, kd_ref, qk_ref, gl_ref, ti_ref), outs):
            r[...] = v

    col = lambda off: pl.BlockSpec((rows, HEAD_DIM), lambda b, h, off=off: (b, off + h))
    tok = jax.ShapeDtypeStruct((t, DN_WIDTH), F32)
    return pl.pallas_call(
        body, name="delta_intra", grid=(n // nb, DN_HEADS),
        in_specs=[col(0), col(DN_HEADS), col(2 * DN_HEADS), pl.BlockSpec((rows, LANES), lambda b, h: (b, 0))],
        out_specs=[col(0)] * 4 + [pl.BlockSpec((nb, None, CHUNK, CHUNK), lambda b, h: (b, h, 0, 0)),
                                  pl.BlockSpec((nb, None, HALO, LANES), lambda b, h: (b, h, 0, 0)),
                                  pl.BlockSpec((nb, None, CHUNK, CHUNK), lambda b, h: (b, h, 0, 0))],
        out_shape=[tok] * 4 + [jax.ShapeDtypeStruct((n, DN_HEADS, CHUNK, CHUNK), F32),
                               jax.ShapeDtypeStruct((n, DN_HEADS, HALO, LANES), F32),
                               jax.ShapeDtypeStruct((n, DN_HEADS, CHUNK, CHUNK), F32)],
        compiler_params=_params("parallel", "arbitrary"),
    )(qkvn, qkvn, qkvn, bg)


def _delta_intra_bwd(qkvn, bg, tinv, cts, nb):
    t = qkvn.shape[0]
    n = t // CHUNK
    rows = nb * CHUNK

    def body(q_ref, k_ref, v_ref, bg_ref, ti_ref, du, dw, dqd, dkd, dqk, dgl, dq_ref, dk_ref, dv_ref, dbg_ref):
        h = pl.program_id(1)
        ti = ti_ref[...]
        _, vjp = jax.vjp(lambda q, k, v, b: _delta_intra_math(q, k, v, b, h, ti)[:6],
                         q_ref[...], k_ref[...], v_ref[...], bg_ref[...])
        dq, dk, dv, dbg = vjp((du[...], dw[...], dqd[...], dkd[...], dqk[...], dgl[...]))
        dq_ref[...] = dq
        dk_ref[...] = dk
        dv_ref[...] = dv

        @pl.when(h == 0)
        def _():
            dbg_ref[...] = jnp.zeros_like(dbg_ref)

        dbg_ref[...] += dbg

    col = lambda off: pl.BlockSpec((rows, HEAD_DIM), lambda b, h, off=off: (b, off + h))
    bgs = pl.BlockSpec((rows, LANES), lambda b, h: (b, 0))
    qks = pl.BlockSpec((nb, None, CHUNK, CHUNK), lambda b, h: (b, h, 0, 0))
    gls = pl.BlockSpec((nb, None, HALO, LANES), lambda b, h: (b, h, 0, 0))
    tok = jax.ShapeDtypeStruct((t, DN_WIDTH), F32)
    return pl.pallas_call(
        body, name="delta_intra_bwd", grid=(n // nb, DN_HEADS),
        in_specs=[col(0), col(DN_HEADS), col(2 * DN_HEADS), bgs, qks, col(0), col(0), col(0), col(0), qks, gls],
        out_specs=[col(0), col(0), col(0), bgs],
        out_shape=[tok, tok, tok, jax.ShapeDtypeStruct((t, LANES), F32)],
        compiler_params=_params("parallel", "arbitrary"),
    )(qkvn, qkvn, qkvn, bg, tinv, *cts)


def _delta_scan(u, w, qd, kd, qk, glb, cb):
    t = u.shape[0]
    n = t // CHUNK
    rows = cb * CHUNK

    def body(u_ref, w_ref, qd_ref, kd_ref, qk_ref, gl_ref, o_ref, s_ref, s_scr):
        @pl.when(pl.program_id(0) == 0)
        def _():
            s_scr[...] = jnp.zeros_like(s_scr)

        def chunk(c, carry):
            r0 = pl.multiple_of(c * CHUNK, CHUNK)
            for h in range(DN_HEADS):
                sl = (pl.ds(r0, CHUNK), slice(h * HEAD_DIM, (h + 1) * HEAD_DIM))
                s = s_scr[h]
                s_ref[c, h] = s
                o, s2 = _delta_step_math(u_ref[sl], w_ref[sl], qd_ref[sl], kd_ref[sl], qk_ref[c, h],
                                         gl_ref[c, h][0:1, :], s)
                o_ref[sl] = o
                s_scr[h] = s2
            return carry

        lax.fori_loop(0, cb, chunk, 0)

    tok = pl.BlockSpec((rows, DN_WIDTH), lambda i: (i, 0))
    return pl.pallas_call(
        body, name="delta_scan", grid=(n // cb,),
        in_specs=[tok] * 4 + [pl.BlockSpec((cb, DN_HEADS, CHUNK, CHUNK), lambda i: (i, 0, 0, 0)),
                              pl.BlockSpec((cb, DN_HEADS, HALO, LANES), lambda i: (i, 0, 0, 0))],
        out_specs=[tok, pl.BlockSpec((cb, DN_HEADS, HEAD_DIM, HEAD_DIM), lambda i: (i, 0, 0, 0))],
        out_shape=[jax.ShapeDtypeStruct((t, DN_WIDTH), F32),
                   jax.ShapeDtypeStruct((n, DN_HEADS, HEAD_DIM, HEAD_DIM), F32)],
        scratch_shapes=[pltpu.VMEM((DN_HEADS, HEAD_DIM, HEAD_DIM), F32)],
        compiler_params=_params("arbitrary"),
    )(u, w, qd, kd, qk, glb)


def _delta_scan_bwd(u, w, qd, kd, qk, glb, s_all, do, cb):
    t = u.shape[0]
    n = t // CHUNK
    nblk = n // cb
    rows = cb * CHUNK

    def body(u_ref, w_ref, qd_ref, kd_ref, qk_ref, gl_ref, s_ref, do_ref,
             du_ref, dw_ref, dqd_ref, dkd_ref, dqk_ref, dgl_ref, ds_scr):
        @pl.when(pl.program_id(0) == 0)
        def _():
            ds_scr[...] = jnp.zeros_like(ds_scr)

        def chunk(step, carry):
            c = cb - 1 - step
            r0 = pl.multiple_of(c * CHUNK, CHUNK)
            for h in range(DN_HEADS):
                sl = (pl.ds(r0, CHUNK), slice(h * HEAD_DIM, (h + 1) * HEAD_DIM))
                gl_tile = gl_ref[c, h]
                prim = (u_ref[sl], w_ref[sl], qd_ref[sl], kd_ref[sl], qk_ref[c, h], gl_tile, s_ref[c, h])
                _, vjp = jax.vjp(lambda a, b, cc, d, e, g, s: _delta_step_math(a, b, cc, d, e, g[0:1, :], s), *prim)
                du, dw, dqd, dkd, dqk, dgl, ds = vjp((do_ref[sl], ds_scr[h]))
                du_ref[sl] = du
                dw_ref[sl] = dw
                dqd_ref[sl] = dqd
                dkd_ref[sl] = dkd
                dqk_ref[c, h] = dqk
                dgl_ref[c, h] = dgl
                ds_scr[h] = ds
            return carry

        lax.fori_loop(0, cb, chunk, 0)

    rev = lambda i: nblk - 1 - i
    tok = pl.BlockSpec((rows, DN_WIDTH), lambda i: (rev(i), 0))
    qks = pl.BlockSpec((cb, DN_HEADS, CHUNK, CHUNK), lambda i: (rev(i), 0, 0, 0))
    gls = pl.BlockSpec((cb, DN_HEADS, HALO, LANES), lambda i: (rev(i), 0, 0, 0))
    ss = pl.BlockSpec((cb, DN_HEADS, HEAD_DIM, HEAD_DIM), lambda i: (rev(i), 0, 0, 0))
    tshape = jax.ShapeDtypeStruct((t, DN_WIDTH), F32)
    return pl.pallas_call(
        body, name="delta_scan_bwd", grid=(nblk,),
        in_specs=[tok] * 4 + [qks, gls, ss, tok],
        out_specs=[tok] * 4 + [qks, gls],
        out_shape=[tshape] * 4 + [jax.ShapeDtypeStruct(qk.shape, F32), jax.ShapeDtypeStruct(glb.shape, F32)],
        scratch_shapes=[pltpu.VMEM((DN_HEADS, HEAD_DIM, HEAD_DIM), F32)],
        compiler_params=_params("arbitrary"),
    )(u, w, qd, kd, qk, glb, s_all, do)


def _peer(mask):
    x, y, c = lax.axis_index("x"), lax.axis_index("y"), lax.axis_index("c")
    return (x ^ ((mask >> 2) & 1), y ^ ((mask >> 1) & 1), c ^ (mask & 1))


def _my_index():
    return 4 * lax.axis_index("x") + 2 * lax.axis_index("y") + lax.axis_index("c")


class _Exchange:
    CHIP_MASKS = (4, 2, 6)

    def __init__(self, kind, arrays):
        self.kind = kind
        self.arrays = list(arrays)
        self.na = na = len(self.arrays)
        if kind == "gather":
            self.out_shape = [jax.ShapeDtypeStruct((N_DEV,) + a.shape, a.dtype) for a in self.arrays]
        else:
            self.out_shape = [jax.ShapeDtypeStruct(a.shape, a.dtype) for a in self.arrays]
        self.scratch = [pltpu.SemaphoreType.DMA((na, 7)), pltpu.SemaphoreType.DMA((na, 7)),
                        pltpu.SemaphoreType.DMA((na,))]

    def _copies(self, ins, outs, sems):
        send_sems, recv_sems, local_sems = sems
        me = _my_index()
        local, first, passed, arrivals = [], [], [], []
        if self.kind == "gather":
            def rc(a, k, block, to, own=False):
                def make():
                    dst = outs[a].at[block]
                    return pltpu.make_async_remote_copy(src_ref=ins[a] if own else dst, dst_ref=dst,
                                                        send_sem=send_sems.at[a, k], recv_sem=recv_sems.at[a, k],
                                                        device_id=to, device_id_type=MESH)
                return make

            sib = _peer(1)
            for a in range(self.na):
                local.append(lambda a=a: pltpu.make_async_copy(ins[a], outs[a].at[me], local_sems.at[a]))
                first.append(rc(a, 0, me, sib, own=True))
                arrivals.append(rc(a, 0, me ^ 1, _peer(0)))
                for j, m in enumerate(self.CHIP_MASKS):
                    first.append(rc(a, 1 + j, me, _peer(m), own=True))
                    passed.append((rc(a, 1 + j, me ^ m, _peer(0)), rc(a, 4 + j, me ^ m, sib)))
                    arrivals.append(rc(a, 4 + j, me ^ m ^ 1, _peer(0)))
        else:
            for a in range(self.na):
                local.append(lambda a=a: pltpu.make_async_copy(ins[a].at[me], outs[a].at[me], local_sems.at[a]))
                for m in range(1, N_DEV):
                    def make(a=a, m=m):
                        return pltpu.make_async_remote_copy(
                            src_ref=ins[a].at[me ^ m], dst_ref=outs[a].at[me], send_sem=send_sems.at[a, m - 1],
                            recv_sem=recv_sems.at[a, m - 1], device_id=_peer(m), device_id_type=MESH)
                    first.append(make)
                    arrivals.append(make)
        return local, first, passed, arrivals

    def start(self, ins, outs, sems):
        local, first, _, _ = self._copies(ins, outs, sems)
        for make in local + first:
            make().start()

    def finish(self, ins, outs, sems):
        local, first, passed, arrivals = self._copies(ins, outs, sems)
        for landed, onward in passed:
            landed().wait_recv()
            onward().start()
        for make in arrivals:
            make().wait_recv()
        for make in first + [p for _, p in passed]:
            make().wait_send()
        for make in local:
            make().wait()

    def run(self, name):
        na = self.na

        def body(*refs):
            ins, outs, sems = refs[:na], refs[na:2 * na], refs[2 * na:]
            self.start(ins, outs, sems)
            self.finish(ins, outs, sems)

        hbm = pl.BlockSpec(memory_space=pltpu.HBM)
        return pl.pallas_call(
            body, name=name, in_specs=[hbm] * na, out_specs=[hbm] * na, out_shape=self.out_shape,
            scratch_shapes=self.scratch, compiler_params=pltpu.CompilerParams(has_side_effects=True),
        )(*self.arrays)


def _all_gather(shards, name):
    return _Exchange("gather", shards).run(name)


def _adamw_math(w, g, m, v):
    m2 = ADAM_B1 * m + (1.0 - ADAM_B1) * g
    v2 = ADAM_B2 * v + (1.0 - ADAM_B2) * jnp.square(g)
    m_hat = m2 / (1.0 - ADAM_B1 ** ADAM_STEP)
    v_hat = v2 / (1.0 - ADAM_B2 ** ADAM_STEP)
    delta = -ADAM_LR * (m_hat / (jnp.sqrt(v_hat) + ADAM_EPS) + ADAM_WD * w)
    return delta, m2, v2


def _sum_adamw(parts, w, m, v, name):
    r, c = w.shape
    tr = _pick(r, (512, 256, 352, 128, 64, 32, 16, 8))
    np_ = parts.shape[0]

    def body(p_ref, w_ref, m_ref, v_ref, g_ref, d_ref, m2_ref, v2_ref):
        g = p_ref[0].astype(F32)
        for d in range(1, np_):
            g = g + p_ref[d].astype(F32)
        delta, m2, v2 = _adamw_math(w_ref[...], g, m_ref[...], v_ref[...])
        g_ref[...] = g
        d_ref[...] = delta
        m2_ref[...] = m2
        v2_ref[...] = v2

    blk = pl.BlockSpec((tr, c), lambda i: (i, 0))
    shp = jax.ShapeDtypeStruct((r, c), F32)
    return pl.pallas_call(
        body, name=name, grid=(r // tr,),
        in_specs=[pl.BlockSpec((np_, tr, c), lambda i: (0, i, 0)), blk, blk, blk],
        out_specs=[blk] * 4, out_shape=[shp] * 4, compiler_params=_params("parallel"),
    )(parts, w, m, v)


def _sum_rows(parts, name):
    np_, r, c = parts.shape

    def body(p_ref, o_ref):
        g = p_ref[0]
        for d in range(1, np_):
            g = g + p_ref[d]
        o_ref[...] = g

    return pl.pallas_call(body, name=name, out_shape=jax.ShapeDtypeStruct((r, c), F32))(parts)


def _pad_w_in(w):
    d = w.shape[1]
    n_ba = 2 * DN_HEADS
    a = w[:SC_OFF]
    ba = w[SC_OFF:SC_OFF + n_ba]
    sc = w[SC_OFF + n_ba:]
    return jnp.concatenate([a, sc, ba, jnp.zeros((BA_W - n_ba, d), w.dtype)], axis=0)


def _unpad_w_in(wp):
    n_ba = 2 * DN_HEADS
    return jnp.concatenate([wp[:SC_OFF], wp[BA_OFF:BA_OFF + n_ba], wp[SC_OFF:BA_OFF]], axis=0)


def _lane_row(v, off):
    return jnp.pad(v.astype(F32), (off, LANES - off - v.shape[0]))[None]


TT = 256
NB_INTRA = 8
CB_SCAN = 8


def _layer_fwd(x, p, ride_proj, ride_ffn, late):
    t, d = x.shape
    got_proj = got_ffn = None
    if ride_proj:
        h, proj, got_proj = _norm_matmul(x, p["norm1_g"], p["w_in"], "proj_fwd_gather", ride=ride_proj)
    else:
        h, proj = _norm_matmul(x, p["norm1_g"], p["w_in"], "proj_fwd")
    p = {**p, **late(got_proj)}
    qkvn, bg = _tok_call(
        "dn_pre", lambda tv, cv: (_dn_pre_math(*tv, *cv), ()), t, TT,
        [(proj, QKV_W, 0, "cur"), (proj, QKV_W, 0, "prev"), (proj, LANES, BA_OFF // LANES, "cur")],
        [p["dn_conv_w"], p["alog_row"], p["dt_row"]],
        [(TT, QKV_W, F32), (TT, LANES, F32)], [])
    u, w, qd, kd, qk, glb, tinv = _delta_intra(qkvn, bg, NB_INTRA)
    o, s_all = _delta_scan(u, w, qd, kd, qk, glb, CB_SCAN)
    cb0 = SC_OFF // SC_WIDTH
    mix_in = [(o, DN_WIDTH, 0, "cur"), (proj, DN_WIDTH, Z_OFF // DN_WIDTH, "cur"),
              (proj, SC_WIDTH, cb0, "cur"), (proj, SC_WIDTH, cb0 + 1, "cur"), (proj, SC_WIDTH, cb0 + 1, "prev"),
              (proj, SC_WIDTH, cb0 + 2, "cur"), (proj, SC_WIDTH, cb0 + 2, "prev")]
    mix_const = [p["dn_norm_g"], p["sc_norm_g"], p["sc_conv_w"]]
    (cat,) = _tok_call("mix_post", lambda tv, cv: ((_mix_math(*tv, *cv),), ()), t, TT,
                       mix_in, mix_const, [(TT, 2 * DN_WIDTH, MXU_DTYPE)], [])
    x_mid = _matmul(cat, p["w_out"], "nn", F32, "out_proj", residual=x)
    if ride_ffn:
        h2, gu, act, got_ffn = _ffn_up_swiglu(x_mid, p["norm2_g"], p["w_gu"], "ffn_up_gather", ride=ride_ffn)
    else:
        h2, gu, act = _ffn_up_swiglu(x_mid, p["norm2_g"], p["w_gu"], "ffn_up")
    x_out = _matmul(act, p["w_down"], "nn", F32, "ffn_down", residual=x_mid)
    saved = dict(x=x, h=h, proj=proj, qkvn=qkvn, bg=bg, u=u, w=w, qd=qd, kd=kd, qk=qk, glb=glb, tinv=tinv, s_all=s_all, o=o,
                 cat=cat, x_mid=x_mid, h2=h2, gu=gu, act=act, mix_in=mix_in, mix_const=mix_const)
    return x_out, saved, p, got_proj, got_ffn


def _layer_bwd(dx_out, p, s, ride_prev, ride_gu, ride_down):
    t, d = dx_out.shape
    got = {}
    dgu = _ffn_down_dx_swiglu(dx_out, p["w_down"], s["gu"], "ffn_down_dx")
    d_w_down = _matmul(s["act"], dx_out, "tn", MXU_DTYPE, "ffn_down_dw")
    if ride_prev:
        dx_mid, d_norm2, got["prev"] = _matmul_norm_bwd(dgu, p["w_gu"], s["x_mid"], p["norm2_g"], dx_out,
                                                        "ffn_up_dx_scatter", ride=ride_prev, w_rows_k=True)
    else:
        dx_mid, d_norm2 = _matmul_norm_bwd(dgu, p["w_gu"], s["x_mid"], p["norm2_g"], dx_out, "ffn_up_dx",
                                           w_rows_k=True)
    d_w_gu = _matmul(dgu, s["h2"], "tn", MXU_DTYPE, "ffn_up_dw")
    dcat = _matmul(dx_mid, p["w_out"], "nt", F32, "out_proj_dx")
    d_w_out = _matmul(s["cat"], dx_mid, "tn", MXU_DTYPE, "out_proj_dw")

    def mix_bwd(tv, cv):
        prim = tuple(tv[:7]) + tuple(cv)
        _, vjp = jax.vjp(_mix_math, *prim)
        do, dz, dgb, dgc, dgch, dhv, dhvh, ddng, dscg, dscw = vjp(tv[7])
        return (do, dz, dgb, dgc, dgch, dhv, dhvh), (ddng, dscg, dscw)

    wide = (TT, DN_WIDTH, F32)
    halo = (HALO, SC_WIDTH, F32)
    do, dz, dgb, dgc, dgc_h, dhv, dhv_h, d_dn_norm, d_sc_norm, d_sc_conv = _tok_call(
        "mix_post_bwd", mix_bwd, t, TT, s["mix_in"] + [(dcat, 2 * DN_WIDTH, 0, "cur")], s["mix_const"],
        [wide, wide, wide, wide, halo, wide, halo],
        [((1, HEAD_DIM), F32), ((1, SC_WIDTH), F32), ((SC_CONV, SC_WIDTH), F32)])
    cts = _delta_scan_bwd(s["u"], s["w"], s["qd"], s["kd"], s["qk"], s["glb"], s["s_all"], do, CB_SCAN)
    dq, dk, dv, dbg = _delta_intra_bwd(s["qkvn"], s["bg"], s["tinv"], cts, NB_INTRA)
    proj = s["proj"]

    def with_halo(cur, nxt):
        return cur + jnp.concatenate([jnp.zeros((TT - HALO, cur.shape[1]), F32), nxt], axis=0)

    def dn_pre_bwd(tv, cv, carried):
        cur, hal, ba, dq_, dk_, dv_, dbg_, dz_, dgb_, dgc_, dgc_n, dhv_, dhv_n = tv
        _, vjp = jax.vjp(_dn_pre_math, cur, hal, ba, *cv)
        dcur, dhal, dba, dcw, dal, ddt = vjp((jnp.concatenate([dq_, dk_, dv_], axis=1), dbg_))
        dproj_rows = jnp.concatenate([with_halo(dcur, carried[0]), dz_, dgb_, with_halo(dgc_, dgc_n),
                                      with_halo(dhv_, dhv_n), dba, jnp.zeros((TT, BA_W - LANES), F32)], axis=1)
        return (dproj_rows,), (dcw, dal, ddt), (dhal,)

    dproj, d_dn_conv, d_alog, d_dt = _tok_call(
        "dn_pre_bwd", dn_pre_bwd, t, TT,
        [(proj, QKV_W, 0, "cur"), (proj, QKV_W, 0, "prev"), (proj, LANES, BA_OFF // LANES, "cur"),
         (dq, DN_WIDTH, 0, "cur"), (dk, DN_WIDTH, 0, "cur"), (dv, DN_WIDTH, 0, "cur"), (dbg, LANES, 0, "cur"),
         (dz, DN_WIDTH, 0, "cur"), (dgb, SC_WIDTH, 0, "cur"), (dgc, SC_WIDTH, 0, "cur"), (dgc_h, SC_WIDTH, 0, "next8"),
         (dhv, SC_WIDTH, 0, "cur"), (dhv_h, SC_WIDTH, 0, "next8")],
        [p["dn_conv_w"], p["alog_row"], p["dt_row"]],
        [(TT, PROJ_W, MXU_DTYPE)],
        [((DN_CONV, QKV_W), F32), ((1, LANES), F32), ((1, LANES), F32)],
        carry=[((HALO, QKV_W), F32)])
    dx_in, d_norm1, got["gu"] = _matmul_norm_bwd(dproj, p["w_in"], s["x"], p["norm1_g"], dx_mid,
                                                 "proj_dx_scatter", ride=ride_gu(d_w_gu), w_rows_k=True)
    d_w_in, got["down"] = _matmul(dproj, s["h"], "tn", MXU_DTYPE, "proj_dw_scatter", ride=ride_down(d_w_down))
    grads = dict(w_in=d_w_in, w_out=d_w_out, w_gu=d_w_gu, w_down=d_w_down, norm1_g=d_norm1, norm2_g=d_norm2,
                 dn_norm_g=d_dn_norm, sc_norm_g=d_sc_norm, sc_conv_w=d_sc_conv, dn_conv_w=d_dn_conv,
                 alog=d_alog, dt=d_dt)
    return dx_in, grads, got


def _final_loss(x, g, target):
    t, d = x.shape

    def fn(tv, cv):
        xv, tg = tv

        def loss_fn(xx, gg):
            err = jnp.square(_rms_norm(xx, gg) - tg)
            return 0.5 * jnp.sum(jnp.mean(err, axis=-1))

        loss, vjp = jax.vjp(loss_fn, xv, cv[0])
        dx, dg = vjp(jnp.ones((), F32))
        return (dx,), (jnp.full((1, LANES), loss, F32), dg)

    return _tok_call("final_loss", fn, t, TT, [(x, d, 0, "cur"), (target, d, 0, "cur")], [g],
                     [(TT, d, F32)], [((1, LANES), F32), ((1, d), F32)])


def _pack_rows(arrs):
    rows, offs, r0 = [], [], 0
    for a in arrs:
        n = a.size
        nr = -(-n // LANES)
        flat = jnp.pad(a.reshape(-1).astype(F32), (0, nr * LANES - n))
        rows.append(flat.reshape(nr, LANES))
        offs.append((r0, nr, a.shape))
        r0 += nr
    pad = (-r0) % 8
    if pad:
        rows.append(jnp.zeros((pad, LANES), F32))
    return jnp.concatenate(rows, axis=0), offs


def _unpack_rows(packed, offs):
    out = []
    for r0, nr, shp in offs:
        n = 1
        for s_ in shp:
            n *= s_
        out.append(packed[r0:r0 + nr].reshape(-1)[:n].reshape(shp))
    return out


def kernel(x, norm1_g, w_in, dn_conv_w, dn_a_log, dn_dt_bias, dn_norm_g, sc_conv_w, sc_norm_g, w_out, norm2_g, ffn_w_gate, ffn_w_up, ffn_w_down, final_norm_g, loss_target, m_norm1_g, m_w_in, m_dn_conv_w, m_dn_a_log, m_dn_dt_bias, m_dn_norm_g, m_sc_conv_w, m_sc_norm_g, m_w_out, m_norm2_g, m_ffn_w_gate, m_ffn_w_up, m_ffn_w_down, m_final_norm_g, v_norm1_g, v_w_in, v_dn_conv_w, v_dn_a_log, v_dn_dt_bias, v_dn_norm_g, v_sc_conv_w, v_sc_norm_g, v_w_out, v_norm2_g, v_ffn_w_gate, v_ffn_w_up, v_ffn_w_down, v_final_norm_g):
    depth, d, cin = w_in.shape
    t = x.shape[1]
    dff_s = ffn_w_gate.shape[2]
    tr = lambda a: a.transpose(0, 2, 1)
    gate_t, up_t = tr(ffn_w_gate), tr(ffn_w_up)
    by_layer = lambda a: [a[l].T for l in range(depth)]
    win_t = by_layer(w_in)
    me = _my_index()
    x2 = x.reshape(t, d)
    tgt = loss_target.reshape(t, d)

    conv_pack, conv_offs = _pack_rows([dn_conv_w, sc_conv_w])
    (conv_all,) = _all_gather([conv_pack], "gather_conv")
    dn_parts, sc_parts = zip(*[_unpack_rows(conv_all[j], conv_offs) for j in range(N_DEV)])
    dn_conv_full = jnp.concatenate(dn_parts, axis=2)
    sc_conv_full = jnp.concatenate(sc_parts, axis=2)

    def shards(l):
        return [a[l].astype(MXU_DTYPE) for a in (win_t, w_out, gate_t, up_t, ffn_w_down)]

    def mixer_params(l, g_in, g_out):
        return dict(
            w_in=_pad_w_in(g_in.reshape(N_DEV * cin, d)), w_out=g_out.reshape(d, d),
            norm1_g=norm1_g[l][None], norm2_g=norm2_g[l][None], dn_norm_g=dn_norm_g[l][None],
            sc_norm_g=sc_norm_g[l][None], dn_conv_w=dn_conv_full[l], sc_conv_w=sc_conv_full[l],
            alog_row=_lane_row(dn_a_log[l], DN_HEADS), dt_row=_lane_row(dn_dt_bias[l], DN_HEADS))

    def ffn_params(g_gate, g_up, g_down):
        dff = N_DEV * dff_s
        return dict(w_gu=_interleave_gu(g_gate.reshape(dff, d), g_up.reshape(dff, d)), w_down=g_down.reshape(dff, d))

    nxt = mixer_params(0, *_all_gather(shards(0)[:2], "gather_first"))
    nxt_ffn = None
    params, saved = [], []
    xc = x2
    for l in range(depth):
        last = l + 1 == depth
        own_ffn = shards(l)[2:] if nxt_ffn is None else []
        ahead = [] if last else shards(l + 1)
        ride_proj = _Exchange("gather", own_ffn + ahead[:2]) if own_ffn or ahead else None
        ride_ffn = None if last else _Exchange("gather", ahead[2:])

        def late(got, own_ffn=own_ffn, nxt_ffn=nxt_ffn):
            return ffn_params(*got[:3]) if own_ffn else nxt_ffn

        xc, s, p_l, got_proj, got_ffn = _layer_fwd(xc, nxt, ride_proj, ride_ffn, late)
        params.append(p_l)
        saved.append(s)
        if not last:
            nxt = mixer_params(l + 1, *got_proj[len(own_ffn):])
            nxt_ffn = ffn_params(*got_ffn)
    dx, loss_part, d_final = _final_loss(xc, final_norm_g[None], tgt)

    names = ("w_in", "w_out", "ffn_w_gate", "ffn_w_up", "ffn_w_down")
    big_out = {n: {k: [None] * depth for k in ("g", "d", "m", "v")} for n in names}
    w_loc = dict(w_in=win_t, w_out=w_out, ffn_w_gate=gate_t, ffn_w_up=up_t, ffn_w_down=ffn_w_down)
    m_loc = dict(w_in=by_layer(m_w_in), w_out=m_w_out, ffn_w_gate=tr(m_ffn_w_gate), ffn_w_up=tr(m_ffn_w_up), ffn_w_down=m_ffn_w_down)
    v_loc = dict(w_in=by_layer(v_w_in), w_out=v_w_out, ffn_w_gate=tr(v_ffn_w_gate), ffn_w_up=tr(v_ffn_w_up), ffn_w_down=v_ffn_w_down)


    def ride_gu(d_w_gu):
        rows = lambda a: a.reshape(N_DEV, dff_s, d).astype(MXU_DTYPE)
        g_gate, g_up = _split_gu(d_w_gu)
        return _Exchange("scatter", [rows(g_gate), rows(g_up)])

    def ride_down(d_w_down):
        return _Exchange("scatter", [d_w_down.reshape(N_DEV, dff_s, d).astype(MXU_DTYPE)])

    def apply(l, which, recv):
        for n, r in zip(which, recv):
            res = _sum_adamw(r, w_loc[n][l], m_loc[n][l], v_loc[n][l], "adamw_" + n)
            for k, a in zip(("g", "d", "m", "v"), res):
                big_out[n][k][l] = a

    grads = [None] * depth
    pending = None
    for l in reversed(range(depth)):
        ride_prev = _Exchange("scatter", pending) if pending else None
        dx, grads[l], got = _layer_bwd(dx, params[l], saved[l], ride_prev, ride_gu, ride_down)
        if ride_prev:
            apply(l + 1, names[:2], got["prev"])
        apply(l, names[2:], [*got["gu"], *got["down"]])
        pending = [_unpad_w_in(grads[l]["w_in"]).reshape(N_DEV, cin, d).astype(MXU_DTYPE),
                   grads[l]["w_out"].reshape(N_DEV, d // N_DEV, d).astype(MXU_DTYPE)]
    apply(0, names[:2], _Exchange("scatter", pending).run("scatter_last"))
    grad_x = dx.reshape(x.shape)
    big_out = {n: {k: jnp.stack(v_, axis=1 if n == "w_in" else 0) for k, v_ in o.items()} for n, o in big_out.items()}
    for n in ("ffn_w_gate", "ffn_w_up"):
        big_out[n] = {k: tr(a) for k, a in big_out[n].items()}
    big_out["w_in"] = {k: a.transpose(1, 2, 0) for k, a in big_out["w_in"].items()}

    stack = lambda key: jnp.stack([grads[l][key] for l in range(depth)])
    small_parts = [stack("norm1_g").reshape(depth, d), stack("norm2_g").reshape(depth, d), d_final.reshape(d),
                   stack("dn_norm_g").reshape(depth, HEAD_DIM), stack("sc_norm_g").reshape(depth, SC_WIDTH),
                   stack("alog").reshape(depth, LANES), stack("dt").reshape(depth, LANES),
                   stack("dn_conv_w"), stack("sc_conv_w"), loss_part]
    small_pack, small_offs = _pack_rows(small_parts)
    (small_all,) = _all_gather([small_pack], "gather_small")
    total = _sum_rows(small_all, "sum_small")
    (g_n1, g_n2, g_fin, g_dnn, g_scn, g_alog, g_dt, g_dnc, g_scc, loss_row) = _unpack_rows(total, small_offs)
    loss = loss_row[0, 0]
    g_alog = g_alog[:, DN_HEADS:2 * DN_HEADS]
    g_dt = g_dt[:, DN_HEADS:2 * DN_HEADS]
    dnc_w = dn_conv_w.shape[2]
    scc_w = sc_conv_w.shape[2]
    g_dnc = lax.dynamic_slice_in_dim(g_dnc, me * dnc_w, dnc_w, axis=2)
    g_scc = lax.dynamic_slice_in_dim(g_scc, me * scc_w, scc_w, axis=2)
    sm_g = [g_n1, g_dnc, g_alog, g_dt, g_dnn, g_scc, g_scn, g_n2, g_fin]
    sm_w = [norm1_g, dn_conv_w, dn_a_log, dn_dt_bias, dn_norm_g, sc_conv_w, sc_norm_g, norm2_g, final_norm_g]
    sm_m = [m_norm1_g, m_dn_conv_w, m_dn_a_log, m_dn_dt_bias, m_dn_norm_g, m_sc_conv_w, m_sc_norm_g, m_norm2_g, m_final_norm_g]
    sm_v = [v_norm1_g, v_dn_conv_w, v_dn_a_log, v_dn_dt_bias, v_dn_norm_g, v_sc_conv_w, v_sc_norm_g, v_norm2_g, v_final_norm_g]
    pg, offs = _pack_rows(sm_g)
    pw, _ = _pack_rows(sm_w)
    pm, _ = _pack_rows(sm_m)
    pv, _ = _pack_rows(sm_v)
    sg, sd, sm_, sv = _sum_adamw(pg[None], pw, pm, pv, "adamw_small")
    small_out = {k: _unpack_rows(a, offs) for k, a in zip(("g", "d", "m", "v"), (sg, sd, sm_, sv))}

    def outputs(k):
        s_ = small_out[k]
        b = big_out
        return [s_[0], b["w_in"][k], s_[1], s_[2], s_[3], s_[4], s_[5], s_[6], b["w_out"][k], s_[7],
                b["ffn_w_gate"][k], b["ffn_w_up"][k], b["ffn_w_down"][k], s_[8]]

    return (loss, grad_x, *outputs("g"), *outputs("d"), *outputs("m"), *outputs("v"))
```

```python
import functools

import jax
import jax.numpy as jnp
from jax import lax
from jax.experimental import pallas as pl
from jax.experimental.pallas import tpu as pltpu

F32 = jnp.float32
MXU_DTYPE = jnp.bfloat16
MESH = pl.DeviceIdType.MESH

N_DEV = 8
EPS = 1e-6
DN_HEADS = 4
HEAD_DIM = 128
DN_WIDTH = DN_HEADS * HEAD_DIM
SC_WIDTH = 512
SC_GROUPS = 4
DN_CONV = 4
SC_CONV = 3
CHUNK = 64
HALO = 8
LANES = 128

QKV_W = 3 * DN_WIDTH
Z_OFF = QKV_W
SC_OFF = Z_OFF + DN_WIDTH
BA_OFF = SC_OFF + 3 * SC_WIDTH
BA_W = 256
PROJ_W = BA_OFF + BA_W

ADAM_LR = 0.001
ADAM_B1 = 0.9
ADAM_B2 = 0.999
ADAM_EPS = 1e-08
ADAM_WD = 0.01
ADAM_STEP = 10


def _pick(n, cands):
    for c in cands:
        if n % c == 0:
            return c
    return n


def _params(*sem):
    return pltpu.CompilerParams(dimension_semantics=sem)


def _rms_norm(x, g):
    return x * lax.rsqrt(jnp.mean(x * x, axis=-1, keepdims=True) + EPS) * g


def _dot(a, b, dims=(((1,), (0,)), ((), ()))):
    return lax.dot_general(a.astype(MXU_DTYPE), b.astype(MXU_DTYPE), dims, preferred_element_type=F32)


def _split_terms(x, terms):
    out = []
    for _ in range(terms):
        hi = x.astype(MXU_DTYPE)
        out.append(hi)
        x = x - hi.astype(F32)
    return out


def _ein_impl(spec, terms, a, b):
    ta, tb = terms
    if ta == 1 and tb == 1:
        return jnp.einsum(spec, a.astype(MXU_DTYPE), b.astype(MXU_DTYPE), preferred_element_type=F32)
    pa, pb = _split_terms(a, ta), _split_terms(b, tb)
    order = max(ta, tb) - 1
    acc = None
    for deg in range(order, -1, -1):
        for i in range(ta):
            j = deg - i
            if 0 <= j < tb:
                t = jnp.einsum(spec, pa[i], pb[j], preferred_element_type=F32)
                acc = t if acc is None else acc + t
    return acc


@functools.partial(jax.custom_vjp, nondiff_argnums=(0, 1))
def _ein(spec, terms, a, b):
    return _ein_impl(spec, terms, a, b)


def _ein_fwd(spec, terms, a, b):
    return _ein_impl(spec, terms, a, b), (a, b)


def _ein_bwd(spec, terms, res, ct):
    a, b = res
    xy, z = spec.split("->")
    x, y = xy.split(",")
    tc = min(max(terms), 2)
    da = _ein_impl(f"{z},{y}->{x}", (tc, terms[1]), ct, b)
    db = _ein_impl(f"{x},{z}->{y}", (terms[0], tc), a, ct)
    return da, db


_ein.defvjp(_ein_fwd, _ein_bwd)

FAST = (1, 1)
PRECISE = (2, 2)
LHS_EXACT = (1, 3)


def _causal_conv(cur, halo, w, k):
    tt = cur.shape[0]
    xp = jnp.concatenate([halo, cur], axis=0)
    y = None
    for j in range(k):
        start = HALO - (k - 1) + j
        term = xp[start:start + tt] * w[j:j + 1]
        y = term if y is None else y + term
    return y


def _matmul(a, b, mode, out_dtype, name, residual=None, ride=None):
    if mode == "nn":
        (m, k), (k2, n) = a.shape, b.shape
    elif mode == "nt":
        (m, k), (n, k2) = a.shape, b.shape
    else:
        (k, m), (k2, n) = a.shape, b.shape
    assert k == k2
    tm = _pick(m, (1024, 1408, 1280, 512, 256, 128))
    tn = _pick(n, (1280, 1408, 1024, 512, 256, 128))
    tk = k if k <= 2816 else _pick(k, (1024, 768, 512))
    gi, gj, nk = m // tm, n // tn, k // tk
    dims = {"nn": (((1,), (0,)), ((), ())), "nt": (((1,), (1,)), ((), ())), "tn": (((0,), (0,)), ((), ()))}[mode]
    a_spec = {"nn": pl.BlockSpec((tm, tk), lambda i, j, q: (i, q)),
              "nt": pl.BlockSpec((tm, tk), lambda i, j, q: (i, q)),
              "tn": pl.BlockSpec((tk, tm), lambda i, j, q: (q, i))}[mode]
    b_spec = {"nn": pl.BlockSpec((tk, tn), lambda i, j, q: (q, j)),
              "nt": pl.BlockSpec((tn, tk), lambda i, j, q: (j, q)),
              "tn": pl.BlockSpec((tk, tn), lambda i, j, q: (q, j))}[mode]
    o_spec = pl.BlockSpec((tm, tn), lambda i, j, q: (i, j))
    has_res = residual is not None
    n_in = 3 if has_res else 2
    nr = ride.na if ride else 0

    def body(*refs):
        a_ref, b_ref = refs[0], refs[1]
        r_ref = refs[2] if has_res else None
        ride_in = refs[n_in:n_in + nr]
        o_ref = refs[n_in + nr]
        ride_out = refs[n_in + nr + 1:n_in + 2 * nr + 1]
        acc = refs[n_in + 2 * nr + 1]
        ride_sems = refs[n_in + 2 * nr + 2:]
        i, j, q = pl.program_id(0), pl.program_id(1), pl.program_id(2)
        if ride:
            @pl.when((i == 0) & (j == 0) & (q == 0))
            def _():
                ride.start(ride_in, ride_out, ride_sems)

        @pl.when(q == 0)
        def _():
            acc[...] = jnp.zeros_like(acc)

        acc[...] += _dot(a_ref[...], b_ref[...], dims)

        @pl.when(q == nk - 1)
        def _():
            r = acc[...]
            if has_res:
                r = r + r_ref[...]
            o_ref[...] = r.astype(o_ref.dtype)

        if ride:
            @pl.when((i == gi - 1) & (j == gj - 1) & (q == nk - 1))
            def _():
                ride.finish(ride_in, ride_out, ride_sems)

    hbm = pl.BlockSpec(memory_space=pltpu.HBM)
    in_specs = [a_spec, b_spec] + ([o_spec] if has_res else []) + [hbm] * nr
    args = (a, b) + ((residual,) if has_res else ()) + (tuple(ride.arrays) if ride else ())
    res = pl.pallas_call(
        body, name=name, grid=(gi, gj, nk), in_specs=in_specs, out_specs=[o_spec] + [hbm] * nr,
        out_shape=[jax.ShapeDtypeStruct((m, n), out_dtype)] + (ride.out_shape if ride else []),
        scratch_shapes=[pltpu.VMEM((tm, tn), F32)] + (ride.scratch if ride else []),
        compiler_params=pltpu.CompilerParams(
            dimension_semantics=("arbitrary",) * 3 if ride else ("parallel", "parallel", "arbitrary"),
            has_side_effects=bool(ride)),
    )(*args)
    return (res[0], res[1:]) if ride else res[0]


def _norm_matmul(x, g, w, name, ride=None):
    t, d = x.shape
    n = w.shape[0]
    tm = _pick(t, (1024, 512, 256, 128))
    tn = _pick(n, (1280, 1408, 1024, 512, 256, 128))
    gi, gj = t // tm, n // tn
    nr = ride.na if ride else 0

    def body(*refs):
        x_ref, g_ref, w_ref = refs[:3]
        ride_in = refs[3:3 + nr]
        h_ref, y_ref = refs[3 + nr], refs[4 + nr]
        ride_out = refs[5 + nr:5 + 2 * nr]
        h_scr = refs[5 + 2 * nr]
        ride_sems = refs[6 + 2 * nr:]
        i, j = pl.program_id(0), pl.program_id(1)
        if ride:
            @pl.when((i == 0) & (j == 0))
            def _():
                ride.start(ride_in, ride_out, ride_sems)

        @pl.when(j == 0)
        def _():
            h = _rms_norm(x_ref[...], g_ref[...]).astype(MXU_DTYPE)
            h_scr[...] = h
            h_ref[...] = h

        y_ref[...] = _dot(h_scr[...], w_ref[...], (((1,), (1,)), ((), ())))

        if ride:
            @pl.when((i == gi - 1) & (j == gj - 1))
            def _():
                ride.finish(ride_in, ride_out, ride_sems)

    hbm = pl.BlockSpec(memory_space=pltpu.HBM)
    res = pl.pallas_call(
        body, name=name, grid=(gi, gj),
        in_specs=[pl.BlockSpec((tm, d), lambda i, j: (i, 0)), pl.BlockSpec((1, d), lambda i, j: (0, 0)),
                  pl.BlockSpec((tn, d), lambda i, j: (j, 0))] + [hbm] * nr,
        out_specs=[pl.BlockSpec((tm, d), lambda i, j: (i, 0)), pl.BlockSpec((tm, tn), lambda i, j: (i, j))] + [hbm] * nr,
        out_shape=[jax.ShapeDtypeStruct((t, d), MXU_DTYPE), jax.ShapeDtypeStruct((t, n), F32)]
        + (ride.out_shape if ride else []),
        scratch_shapes=[pltpu.VMEM((tm, d), MXU_DTYPE)] + (ride.scratch if ride else []),
        compiler_params=pltpu.CompilerParams(
            dimension_semantics=("arbitrary",) * 2 if ride else ("parallel", "arbitrary"),
            has_side_effects=bool(ride)),
    )(x, g, w, *(ride.arrays if ride else ()))
    return (res[0], res[1], res[2:]) if ride else (res[0], res[1])


def _swiglu_math(g, u):
    return jax.nn.silu(g) * u


def _gu_tile(dff):
    return _pick(dff, (1408, 1024, 512, 256, 128))


def _interleave_gu(gate_t, up_t):
    tn = _gu_tile(gate_t.shape[0])
    pieces = []
    for j in range(gate_t.shape[0] // tn):
        pieces += [gate_t[j * tn:(j + 1) * tn], up_t[j * tn:(j + 1) * tn]]
    return jnp.concatenate(pieces, axis=0)


def _split_gu(gu_t):
    dff = gu_t.shape[0] // 2
    tn = _gu_tile(dff)
    tiles = [gu_t[j * tn:(j + 1) * tn] for j in range(2 * dff // tn)]
    return jnp.concatenate(tiles[0::2], axis=0), jnp.concatenate(tiles[1::2], axis=0)


def _ffn_up_swiglu(x, g, w_gu, name, ride=None):
    t, d = x.shape
    dff = w_gu.shape[0] // 2
    tn = _gu_tile(dff)
    tm = _pick(t, (512, 256, 128))
    gj, gi = dff // tn, t // tm
    nr = ride.na if ride else 0

    def body(*refs):
        x_ref, g_ref, w_ref = refs[:3]
        ride_in = refs[3:3 + nr]
        h_ref, gu_ref, act_ref = refs[3 + nr:6 + nr]
        ride_out = refs[6 + nr:6 + 2 * nr]
        ride_sems = refs[6 + 2 * nr:]
        j, i = pl.program_id(0), pl.program_id(1)
        if ride:
            @pl.when((i == 0) & (j == 0))
            def _():
                ride.start(ride_in, ride_out, ride_sems)

        h = _rms_norm(x_ref[...], g_ref[...]).astype(MXU_DTYPE)

        @pl.when(j == 0)
        def _():
            h_ref[...] = h

        y = _dot(h, w_ref[...], (((1,), (1,)), ((), ())))
        gu_ref[...] = y.astype(gu_ref.dtype)
        act_ref[...] = _swiglu_math(y[:, :tn], y[:, tn:]).astype(act_ref.dtype)

        if ride:
            @pl.when((i == gi - 1) & (j == gj - 1))
            def _():
                ride.finish(ride_in, ride_out, ride_sems)

    hbm = pl.BlockSpec(memory_space=pltpu.HBM)
    res = pl.pallas_call(
        body, name=name, grid=(gj, gi),
        in_specs=[pl.BlockSpec((tm, d), lambda j, i: (i, 0)), pl.BlockSpec((1, d), lambda j, i: (0, 0)),
                  pl.BlockSpec((2 * tn, d), lambda j, i: (j, 0))] + [hbm] * nr,
        out_specs=[pl.BlockSpec((tm, d), lambda j, i: (jnp.where(j == 0, i, gi - 1), 0)),
                   pl.BlockSpec((tm, 2 * tn), lambda j, i: (i, j)),
                   pl.BlockSpec((tm, tn), lambda j, i: (i, j))] + [hbm] * nr,
        out_shape=[jax.ShapeDtypeStruct((t, d), MXU_DTYPE), jax.ShapeDtypeStruct((t, 2 * dff), MXU_DTYPE),
                   jax.ShapeDtypeStruct((t, dff), MXU_DTYPE)] + (ride.out_shape if ride else []),
        scratch_shapes=(ride.scratch if ride else []),
        compiler_params=pltpu.CompilerParams(dimension_semantics=("arbitrary", "arbitrary"),
                                             has_side_effects=bool(ride)),
    )(x, g, w_gu, *(ride.arrays if ride else ()))
    return (res[0], res[1], res[2], res[3:]) if ride else tuple(res)


def _ffn_down_dx_swiglu(dx_out, w_down, gu, name):
    t, d = dx_out.shape
    dff = w_down.shape[0]
    tn = _gu_tile(dff)
    tm = _pick(t, (512, 256, 128))

    def body(dx_ref, w_ref, gu_ref, o_ref):
        dact = _dot(dx_ref[...], w_ref[...], (((1,), (1,)), ((), ())))
        gu_v = gu_ref[...].astype(F32)
        _, vjp = jax.vjp(_swiglu_math, gu_v[:, :tn], gu_v[:, tn:])
        dg, du = vjp(dact)
        o_ref[...] = jnp.concatenate([dg, du], axis=1).astype(o_ref.dtype)

    return pl.pallas_call(
        body, name=name, grid=(dff // tn, t // tm),
        in_specs=[pl.BlockSpec((tm, d), lambda j, i: (i, 0)), pl.BlockSpec((tn, d), lambda j, i: (j, 0)),
                  pl.BlockSpec((tm, 2 * tn), lambda j, i: (i, j))],
        out_specs=pl.BlockSpec((tm, 2 * tn), lambda j, i: (i, j)),
        out_shape=jax.ShapeDtypeStruct((t, 2 * dff), MXU_DTYPE),
        compiler_params=_params("parallel", "parallel"),
    )(dx_out, w_down, gu)


def _matmul_norm_bwd(dy, w, x, g, dres, name, ride=None, w_rows_k=False):
    t, k = dy.shape
    d = w.shape[1] if w_rows_k else w.shape[0]
    tm = _pick(t, (1024, 512, 256, 128))
    tk = k if k <= 2816 else _pick(k, (1408, 1280, 1024, 768, 512))
    tr = _pick(tm, (256, 128))
    gi, nk = t // tm, k // tk
    nr = ride.na if ride else 0

    def body(*refs):
        dy_ref, w_ref, x_ref, g_ref, dres_ref = refs[:5]
        ride_in = refs[5:5 + nr]
        dx_ref, dg_ref = refs[5 + nr], refs[6 + nr]
        ride_out = refs[7 + nr:7 + 2 * nr]
        acc = refs[7 + 2 * nr]
        ride_sems = refs[8 + 2 * nr:]
        i, q = pl.program_id(0), pl.program_id(1)

        @pl.when((i == 0) & (q == 0))
        def _():
            dg_ref[...] = jnp.zeros_like(dg_ref)
            if ride:
                ride.start(ride_in, ride_out, ride_sems)

        @pl.when(q == 0)
        def _():
            acc[...] = jnp.zeros_like(acc)

        acc[...] += _dot(dy_ref[...], w_ref[...], (((1,), (0 if w_rows_k else 1,)), ((), ())))

        @pl.when(q == nk - 1)
        def _():
            for r in range(tm // tr):
                rows = slice(r * tr, (r + 1) * tr)
                _, vjp = jax.vjp(_rms_norm, x_ref[rows], g_ref[...])
                dxn, dg = vjp(acc[rows])
                dx_ref[rows] = dres_ref[rows] + dxn
                dg_ref[...] += dg

        if ride:
            @pl.when((i == gi - 1) & (q == nk - 1))
            def _():
                ride.finish(ride_in, ride_out, ride_sems)

    hbm = pl.BlockSpec(memory_space=pltpu.HBM)
    row = pl.BlockSpec((tm, d), lambda i, q: (i, 0))
    res = pl.pallas_call(
        body, name=name, grid=(gi, nk),
        in_specs=[pl.BlockSpec((tm, tk), lambda i, q: (i, q)),
                  pl.BlockSpec((tk, d), lambda i, q: (q, 0)) if w_rows_k else pl.BlockSpec((d, tk), lambda i, q: (0, q)),
                  row,
                  pl.BlockSpec((1, d), lambda i, q: (0, 0)), row] + [hbm] * nr,
        out_specs=[row, pl.BlockSpec((1, d), lambda i, q: (0, 0))] + [hbm] * nr,
        out_shape=[jax.ShapeDtypeStruct((t, d), F32), jax.ShapeDtypeStruct((1, d), F32)]
        + (ride.out_shape if ride else []),
        scratch_shapes=[pltpu.VMEM((tm, d), F32)] + (ride.scratch if ride else []),
        compiler_params=pltpu.CompilerParams(dimension_semantics=("arbitrary", "arbitrary"),
                                             has_side_effects=bool(ride)),
    )(dy, w, x, g, dres, *(ride.arrays if ride else ()))
    return (res[0], res[1], res[2:]) if ride else (res[0], res[1])


def _tok_call(name, fn, t, tt, tok_in, const_in, tok_out, acc_out, carry=()):
    nblk = t // tt
    hb = tt // HALO
    blk = (lambda i: nblk - 1 - i) if carry else (lambda i: i)
    in_specs, args = [], []
    for arr, w, cb, mode in tok_in:
        if mode == "cur":
            spec = pl.BlockSpec((tt, w), lambda i, cb=cb: (blk(i), cb))
        elif mode == "prev":
            spec = pl.BlockSpec((HALO, w), lambda i, cb=cb: (jnp.maximum(blk(i) * hb - 1, 0), cb))
        else:
            spec = pl.BlockSpec((HALO, w), lambda i, cb=cb: (jnp.minimum(blk(i) + 1, nblk - 1), cb))
        in_specs.append(spec)
        args.append(arr)
    for arr in const_in:
        in_specs.append(pl.BlockSpec(arr.shape, lambda i: (0, 0)))
        args.append(arr)
    out_specs, out_shape = [], []
    for rows, w, dt in tok_out:
        out_specs.append(pl.BlockSpec((rows, w), lambda i: (blk(i), 0)))
        out_shape.append(jax.ShapeDtypeStruct((nblk * rows, w), dt))
    for shp, dt in acc_out:
        out_specs.append(pl.BlockSpec(shp, lambda i: (0, 0)))
        out_shape.append(jax.ShapeDtypeStruct(shp, dt))
    n_tok, n_const, n_out, n_acc = len(tok_in), len(const_in), len(tok_out), len(acc_out)

    def body(*refs):
        i = pl.program_id(0)
        b = blk(i)
        tok_vals = []
        for (_, _, _, mode), r in zip(tok_in, refs[:n_tok]):
            v = r[...]
            if mode == "prev":
                v = jnp.where(b > 0, v, jnp.zeros_like(v))
            elif mode == "next8":
                v = jnp.where(b < nblk - 1, v, jnp.zeros_like(v))
            tok_vals.append(v)
        const_vals = [r[...] for r in refs[n_tok:n_tok + n_const]]
        o_refs = refs[n_tok + n_const:n_tok + n_const + n_out]
        a_refs = refs[n_tok + n_const + n_out:n_tok + n_const + n_out + n_acc]
        c_refs = refs[n_tok + n_const + n_out + n_acc:]
        if a_refs or c_refs:
            @pl.when(i == 0)
            def _():
                for r in (*a_refs, *c_refs):
                    r[...] = jnp.zeros_like(r)

        if c_refs:
            outs, accs, carried = fn(tok_vals, const_vals, [r[...] for r in c_refs])
            for r, v in zip(c_refs, carried):
                r[...] = v
        else:
            outs, accs = fn(tok_vals, const_vals)
        for r, v in zip(o_refs, outs):
            r[...] = v.astype(r.dtype)
        for r, v in zip(a_refs, accs):
            r[...] += v.astype(r.dtype)

    res = pl.pallas_call(
        body, name=name, grid=(nblk,), in_specs=in_specs, out_specs=out_specs, out_shape=out_shape,
        scratch_shapes=[pltpu.VMEM(shp, dt) for shp, dt in carry],
        compiler_params=_params("arbitrary" if acc_out or carry else "parallel"),
    )(*args)
    return res


def _dn_pre_math(cur, halo, ba, cw, alog, dtb):
    tt = cur.shape[0]
    a = jax.nn.silu(_causal_conv(cur, halo, cw, DN_CONV))
    pieces = []
    for p in range(2 * DN_HEADS):
        xh = a[:, p * HEAD_DIM:(p + 1) * HEAD_DIM]
        xh = xh * lax.rsqrt(jnp.sum(xh * xh, axis=-1, keepdims=True) + EPS)
        if p < DN_HEADS:
            xh = xh * (HEAD_DIM ** -0.5)
        pieces.append(xh)
    pieces.append(a[:, 2 * DN_WIDTH:])
    qkvn = jnp.concatenate(pieces, axis=1)
    lane = lax.broadcasted_iota(jnp.int32, ba.shape, 1)
    raw = jnp.where(lane < DN_HEADS, jax.nn.sigmoid(ba), -jnp.exp(alog) * jax.nn.softplus(ba + dtb))
    r = lax.broadcasted_iota(jnp.int32, (tt, tt), 0)
    c = lax.broadcasted_iota(jnp.int32, (tt, tt), 1)
    tri = jnp.where((r // CHUNK == c // CHUNK) & (c <= r), 1.0, 0.0).astype(F32)
    cums = _ein("ij,jk->ik", LHS_EXACT, tri, raw)
    bg = jnp.where(lane < DN_HEADS, raw, cums)
    return qkvn, bg


def _mix_math(o, z, gb, gc, gc_halo, hv, hv_halo, dng, scg, scw):
    outs = []
    for h in range(DN_HEADS):
        sl = slice(h * HEAD_DIM, (h + 1) * HEAD_DIM)
        oh = o[:, sl]
        outs.append(oh * lax.rsqrt(jnp.mean(oh * oh, axis=-1, keepdims=True) + EPS) * dng * jax.nn.silu(z[:, sl]))
    y = gb * _causal_conv(gc * hv, gc_halo * hv_halo, scw, SC_CONV)
    gw = SC_WIDTH // SC_GROUPS
    for g in range(SC_GROUPS):
        sl = slice(g * gw, (g + 1) * gw)
        yg = y[:, sl]
        outs.append(yg * lax.rsqrt(jnp.mean(yg * yg, axis=-1, keepdims=True) + EPS) * scg[:, sl])
    return jnp.concatenate(outs, axis=1)


def _tri_inverse(a):
    c = a.shape[-1]
    r = lax.broadcasted_iota(jnp.int32, (c, c), 0)
    q = lax.broadcasted_iota(jnp.int32, (c, c), 1)
    eye = jnp.where(r == q, 1.0, 0.0).astype(F32)[None]
    blk = (r // 16 == q // 16)[None]
    d = jnp.where(blk, a, 0.0)
    o = a - d
    mm = functools.partial(_ein, "bij,bjk->bik", PRECISE)
    p = eye - d
    n = mm(d, d)
    for _ in range(2):
        both = mm(jnp.concatenate([n, p], axis=1), n)
        n = both[:, :c]
        p = p + both[:, c:]
    p = p + mm(p, n)
    e = mm(p, o)
    e2 = mm(e, e)
    left = eye - e + e2 - mm(e, e2)
    return mm(left, p)


@jax.custom_vjp
def _inverse_known(a, tinv):
    return tinv


def _inverse_known_fwd(a, tinv):
    return tinv, tinv


def _inverse_known_bwd(tinv, ct):
    left = _ein("bji,bjk->bik", PRECISE, tinv, ct)
    return -_ein("bik,bjk->bij", PRECISE, left, tinv), jnp.zeros_like(tinv)


_inverse_known.defvjp(_inverse_known_fwd, _inverse_known_bwd)


def _delta_intra_math(q, k, v, bg, head, tinv_known=None):
    n = q.shape[0]
    nb = n // CHUNK
    lane = lax.broadcasted_iota(jnp.int32, bg.shape, 1)
    beta = jnp.sum(jnp.where(lane == head, bg, 0.0), axis=1, keepdims=True).reshape(nb, CHUNK, 1)
    gc = jnp.sum(jnp.where(lane == head + DN_HEADS, bg, 0.0), axis=1, keepdims=True).reshape(nb, CHUNK, 1)
    q3, k3, v3 = (a.reshape(nb, CHUNK, HEAD_DIM) for a in (q, k, v))
    r = lax.broadcasted_iota(jnp.int32, (CHUNK, CHUNK), 0)
    c = lax.broadcasted_iota(jnp.int32, (CHUNK, CHUNK), 1)
    eye = jnp.where(r == c, 1.0, 0.0).astype(F32)[None]
    gcr = _ein("bik,bkj->bij", LHS_EXACT, jnp.ones((nb, CHUNK, CHUNK), F32), gc * eye)
    decay = jnp.exp(jnp.where((r >= c)[None], gc - gcr, -1e30))
    kb = k3 * beta
    vb = v3 * beta
    egc = jnp.exp(gc)
    on_k = _ein("bcd,bmd->bcm", FAST, jnp.concatenate([kb, q3], axis=1), k3)
    a = jnp.where((r > c)[None], on_k[:, :CHUNK] * decay, 0.0)
    tinv = _tri_inverse(a) if tinv_known is None else _inverse_known(a, tinv_known)
    uw = _ein("bcm,bmd->bcd", PRECISE, tinv, jnp.concatenate([vb, kb * egc], axis=2))
    u, w = uw[:, :, :HEAD_DIM], uw[:, :, HEAD_DIM:]
    qk = on_k[:, CHUNK:] * decay
    row = lax.broadcasted_iota(jnp.int32, (nb, CHUNK, 1), 1)
    glast = jnp.sum(jnp.where(row == CHUNK - 1, gc, 0.0), axis=1, keepdims=True)
    qd = q3 * egc
    kd = k3 * jnp.exp(glast - gc)
    glb = jnp.broadcast_to(jnp.exp(glast), (nb, HALO, LANES))
    flat = lambda x: x.reshape(n, HEAD_DIM)
    return flat(u), flat(w), flat(qd), flat(kd), qk, glb, tinv


def _delta_step_math(u, w, qd, kd, qk, gl, s):
    c = u.shape[0]
    on_s = _ein("ck,kv->cv", FAST, jnp.concatenate([w, qd], axis=0), s)
    vnew = u - on_s[:c]
    on_v = _ein("cm,mv->cv", FAST, jnp.concatenate([qk, kd.T], axis=0), vnew)
    o = on_s[c:] + on_v[:c]
    s2 = s * gl + on_v[c:]
    return o, s2


def _delta_intra(qkvn, bg, nb):
    t = qkvn.shape[0]
    n = t // CHUNK
    rows = nb * CHUNK

    def body(q_ref, k_ref, v_ref, bg_ref, u_ref, w_ref, qd_ref, kd_ref, qk_ref, gl_ref, ti_ref):
        outs = _delta_intra_math(q_ref[...], k_ref[...], v_ref[...], bg_ref[...], pl.program_id(1))
        for r, v in zip((u_ref, w_ref, qd_ref, kd_ref, qk_ref, gl_ref, ti_ref), outs):
            r[...] = v

    col = lambda off: pl.BlockSpec((rows, HEAD_DIM), lambda b, h, off=off: (b, off + h))
    tok = jax.ShapeDtypeStruct((t, DN_WIDTH), F32)
    return pl.pallas_call(
        body, name="delta_intra", grid=(n // nb, DN_HEADS),
        in_specs=[col(0), col(DN_HEADS), col(2 * DN_HEADS), pl.BlockSpec((rows, LANES), lambda b, h: (b, 0))],
        out_specs=[col(0)] * 4 + [pl.BlockSpec((nb, None, CHUNK, CHUNK), lambda b, h: (b, h, 0, 0)),
                                  pl.BlockSpec((nb, None, HALO, LANES), lambda b, h: (b, h, 0, 0)),
                                  pl.BlockSpec((nb, None, CHUNK, CHUNK), lambda b, h: (b, h, 0, 0))],
        out_shape=[tok] * 4 + [jax.ShapeDtypeStruct((n, DN_HEADS, CHUNK, CHUNK), F32),
                               jax.ShapeDtypeStruct((n, DN_HEADS, HALO, LANES), F32),
                               jax.ShapeDtypeStruct((n, DN_HEADS, CHUNK, CHUNK), F32)],
        compiler_params=_params("parallel", "arbitrary"),
    )(qkvn, qkvn, qkvn, bg)


def _delta_intra_bwd(qkvn, bg, tinv, cts, nb):
    t = qkvn.shape[0]
    n = t // CHUNK
    rows = nb * CHUNK

    def body(q_ref, k_ref, v_ref, bg_ref, ti_ref, du, dw, dqd, dkd, dqk, dgl, dq_ref, dk_ref, dv_ref, dbg_ref):
        h = pl.program_id(1)
        ti = ti_ref[...]
        _, vjp = jax.vjp(lambda q, k, v, b: _delta_intra_math(q, k, v, b, h, ti)[:6],
                         q_ref[...], k_ref[...], v_ref[...], bg_ref[...])
        dq, dk, dv, dbg = vjp((du[...], dw[...], dqd[...], dkd[...], dqk[...], dgl[...]))
        dq_ref[...] = dq
        dk_ref[...] = dk
        dv_ref[...] = dv

        @pl.when(h == 0)
        def _():
            dbg_ref[...] = jnp.zeros_like(dbg_ref)

        dbg_ref[...] += dbg

    col = lambda off: pl.BlockSpec((rows, HEAD_DIM), lambda b, h, off=off: (b, off + h))
    bgs = pl.BlockSpec((rows, LANES), lambda b, h: (b, 0))
    qks = pl.BlockSpec((nb, None, CHUNK, CHUNK), lambda b, h: (b, h, 0, 0))
    gls = pl.BlockSpec((nb, None, HALO, LANES), lambda b, h: (b, h, 0, 0))
    tok = jax.ShapeDtypeStruct((t, DN_WIDTH), F32)
    return pl.pallas_call(
        body, name="delta_intra_bwd", grid=(n // nb, DN_HEADS),
        in_specs=[col(0), col(DN_HEADS), col(2 * DN_HEADS), bgs, qks, col(0), col(0), col(0), col(0), qks, gls],
        out_specs=[col(0), col(0), col(0), bgs],
        out_shape=[tok, tok, tok, jax.ShapeDtypeStruct((t, LANES), F32)],
        compiler_params=_params("parallel", "arbitrary"),
    )(qkvn, qkvn, qkvn, bg, tinv, *cts)


def _delta_scan(u, w, qd, kd, qk, glb, cb):
    t = u.shape[0]
    n = t // CHUNK
    rows = cb * CHUNK

    def body(u_ref, w_ref, qd_ref, kd_ref, qk_ref, gl_ref, o_ref, s_ref, s_scr):
        @pl.when(pl.program_id(0) == 0)
        def _():
            s_scr[...] = jnp.zeros_like(s_scr)

        def chunk(c, carry):
            r0 = pl.multiple_of(c * CHUNK, CHUNK)
            for h in range(DN_HEADS):
                sl = (pl.ds(r0, CHUNK), slice(h * HEAD_DIM, (h + 1) * HEAD_DIM))
                s = s_scr[h]
                s_ref[c, h] = s
                o, s2 = _delta_step_math(u_ref[sl], w_ref[sl], qd_ref[sl], kd_ref[sl], qk_ref[c, h],
                                         gl_ref[c, h][0:1, :], s)
                o_ref[sl] = o
                s_scr[h] = s2
            return carry

        lax.fori_loop(0, cb, chunk, 0)

    tok = pl.BlockSpec((rows, DN_WIDTH), lambda i: (i, 0))
    return pl.pallas_call(
        body, name="delta_scan", grid=(n // cb,),
        in_specs=[tok] * 4 + [pl.BlockSpec((cb, DN_HEADS, CHUNK, CHUNK), lambda i: (i, 0, 0, 0)),
                              pl.BlockSpec((cb, DN_HEADS, HALO, LANES), lambda i: (i, 0, 0, 0))],
        out_specs=[tok, pl.BlockSpec((cb, DN_HEADS, HEAD_DIM, HEAD_DIM), lambda i: (i, 0, 0, 0))],
        out_shape=[jax.ShapeDtypeStruct((t, DN_WIDTH), F32),
                   jax.ShapeDtypeStruct((n, DN_HEADS, HEAD_DIM, HEAD_DIM), F32)],
        scratch_shapes=[pltpu.VMEM((DN_HEADS, HEAD_DIM, HEAD_DIM), F32)],
        compiler_params=_params("arbitrary"),
    )(u, w, qd, kd, qk, glb)


def _delta_scan_bwd(u, w, qd, kd, qk, glb, s_all, do, cb):
    t = u.shape[0]
    n = t // CHUNK
    nblk = n // cb
    rows = cb * CHUNK

    def body(u_ref, w_ref, qd_ref, kd_ref, qk_ref, gl_ref, s_ref, do_ref,
             du_ref, dw_ref, dqd_ref, dkd_ref, dqk_ref, dgl_ref, ds_scr):
        @pl.when(pl.program_id(0) == 0)
        def _():
            ds_scr[...] = jnp.zeros_like(ds_scr)

        def chunk(step, carry):
            c = cb - 1 - step
            r0 = pl.multiple_of(c * CHUNK, CHUNK)
            for h in range(DN_HEADS):
                sl = (pl.ds(r0, CHUNK), slice(h * HEAD_DIM, (h + 1) * HEAD_DIM))
                gl_tile = gl_ref[c, h]
                prim = (u_ref[sl], w_ref[sl], qd_ref[sl], kd_ref[sl], qk_ref[c, h], gl_tile, s_ref[c, h])
                _, vjp = jax.vjp(lambda a, b, cc, d, e, g, s: _delta_step_math(a, b, cc, d, e, g[0:1, :], s), *prim)
                du, dw, dqd, dkd, dqk, dgl, ds = vjp((do_ref[sl], ds_scr[h]))
                du_ref[sl] = du
                dw_ref[sl] = dw
                dqd_ref[sl] = dqd
                dkd_ref[sl] = dkd
                dqk_ref[c, h] = dqk
                dgl_ref[c, h] = dgl
                ds_scr[h] = ds
            return carry

        lax.fori_loop(0, cb, chunk, 0)

    rev = lambda i: nblk - 1 - i
    tok = pl.BlockSpec((rows, DN_WIDTH), lambda i: (rev(i), 0))
    qks = pl.BlockSpec((cb, DN_HEADS, CHUNK, CHUNK), lambda i: (rev(i), 0, 0, 0))
    gls = pl.BlockSpec((cb, DN_HEADS, HALO, LANES), lambda i: (rev(i), 0, 0, 0))
    ss = pl.BlockSpec((cb, DN_HEADS, HEAD_DIM, HEAD_DIM), lambda i: (rev(i), 0, 0, 0))
    tshape = jax.ShapeDtypeStruct((t, DN_WIDTH), F32)
    return pl.pallas_call(
        body, name="delta_scan_bwd", grid=(nblk,),
        in_specs=[tok] * 4 + [qks, gls, ss, tok],
        out_specs=[tok] * 4 + [qks, gls],
        out_shape=[tshape] * 4 + [jax.ShapeDtypeStruct(qk.shape, F32), jax.ShapeDtypeStruct(glb.shape, F32)],
        scratch_shapes=[pltpu.VMEM((DN_HEADS, HEAD_DIM, HEAD_DIM), F32)],
        compiler_params=_params("arbitrary"),
    )(u, w, qd, kd, qk, glb, s_all, do)


def _peer(mask):
    x, y, c = lax.axis_index("x"), lax.axis_index("y"), lax.axis_index("c")
    return (x ^ ((mask >> 2) & 1), y ^ ((mask >> 1) & 1), c ^ (mask & 1))


def _my_index():
    return 4 * lax.axis_index("x") + 2 * lax.axis_index("y") + lax.axis_index("c")


class _Exchange:
    CHIP_MASKS = (4, 2, 6)

    def __init__(self, kind, arrays):
        self.kind = kind
        self.arrays = list(arrays)
        self.na = na = len(self.arrays)
        if kind == "gather":
            self.out_shape = [jax.ShapeDtypeStruct((N_DEV,) + a.shape, a.dtype) for a in self.arrays]
        else:
            self.out_shape = [jax.ShapeDtypeStruct(a.shape, a.dtype) for a in self.arrays]
        self.scratch = [pltpu.SemaphoreType.DMA((na, 7)), pltpu.SemaphoreType.DMA((na, 7)),
                        pltpu.SemaphoreType.DMA((na,))]

    def _copies(self, ins, outs, sems):
        send_sems, recv_sems, local_sems = sems
        me = _my_index()
        local, first, passed, arrivals = [], [], [], []
        if self.kind == "gather":
            def rc(a, k, block, to, own=False):
                def make():
                    dst = outs[a].at[block]
                    return pltpu.make_async_remote_copy(src_ref=ins[a] if own else dst, dst_ref=dst,
                                                        send_sem=send_sems.at[a, k], recv_sem=recv_sems.at[a, k],
                                                        device_id=to, device_id_type=MESH)
                return make

            sib = _peer(1)
            for a in range(self.na):
                local.append(lambda a=a: pltpu.make_async_copy(ins[a], outs[a].at[me], local_sems.at[a]))
                first.append(rc(a, 0, me, sib, own=True))
                arrivals.append(rc(a, 0, me ^ 1, _peer(0)))
                for j, m in enumerate(self.CHIP_MASKS):
                    first.append(rc(a, 1 + j, me, _peer(m), own=True))
                    passed.append((rc(a, 1 + j, me ^ m, _peer(0)), rc(a, 4 + j, me ^ m, sib)))
                    arrivals.append(rc(a, 4 + j, me ^ m ^ 1, _peer(0)))
        else:
            for a in range(self.na):
                local.append(lambda a=a: pltpu.make_async_copy(ins[a].at[me], outs[a].at[me], local_sems.at[a]))
                for m in range(1, N_DEV):
                    def make(a=a, m=m):
                        return pltpu.make_async_remote_copy(
                            src_ref=ins[a].at[me ^ m], dst_ref=outs[a].at[me], send_sem=send_sems.at[a, m - 1],
                            recv_sem=recv_sems.at[a, m - 1], device_id=_peer(m), device_id_type=MESH)
                    first.append(make)
                    arrivals.append(make)
        return local, first, passed, arrivals

    def start(self, ins, outs, sems):
        local, first, _, _ = self._copies(ins, outs, sems)
        for make in local + first:
            make().start()

    def finish(self, ins, outs, sems):
        local, first, passed, arrivals = self._copies(ins, outs, sems)
        for landed, onward in passed:
            landed().wait_recv()
            onward().start()
        for make in arrivals:
            make().wait_recv()
        for make in first + [p for _, p in passed]:
            make().wait_send()
        for make in local:
            make().wait()

    def run(self, name):
        na = self.na

        def body(*refs):
            ins, outs, sems = refs[:na], refs[na:2 * na], refs[2 * na:]
            self.start(ins, outs, sems)
            self.finish(ins, outs, sems)

        hbm = pl.BlockSpec(memory_space=pltpu.HBM)
        return pl.pallas_call(
            body, name=name, in_specs=[hbm] * na, out_specs=[hbm] * na, out_shape=self.out_shape,
            scratch_shapes=self.scratch, compiler_params=pltpu.CompilerParams(has_side_effects=True),
        )(*self.arrays)


def _all_gather(shards, name):
    return _Exchange("gather", shards).run(name)


def _adamw_math(w, g, m, v):
    m2 = ADAM_B1 * m + (1.0 - ADAM_B1) * g
    v2 = ADAM_B2 * v + (1.0 - ADAM_B2) * jnp.square(g)
    m_hat = m2 / (1.0 - ADAM_B1 ** ADAM_STEP)
    v_hat = v2 / (1.0 - ADAM_B2 ** ADAM_STEP)
    delta = -ADAM_LR * (m_hat / (jnp.sqrt(v_hat) + ADAM_EPS) + ADAM_WD * w)
    return delta, m2, v2


def _sum_adamw(parts, w, m, v, name):
    r, c = w.shape
    tr = _pick(r, (512, 256, 352, 128, 64, 32, 16, 8))
    np_ = parts.shape[0]

    def body(p_ref, w_ref, m_ref, v_ref, g_ref, d_ref, m2_ref, v2_ref):
        g = p_ref[0].astype(F32)
        for d in range(1, np_):
            g = g + p_ref[d].astype(F32)
        delta, m2, v2 = _adamw_math(w_ref[...], g, m_ref[...], v_ref[...])
        g_ref[...] = g
        d_ref[...] = delta
        m2_ref[...] = m2
        v2_ref[...] = v2

    blk = pl.BlockSpec((tr, c), lambda i: (i, 0))
    shp = jax.ShapeDtypeStruct((r, c), F32)
    return pl.pallas_call(
        body, name=name, grid=(r // tr,),
        in_specs=[pl.BlockSpec((np_, tr, c), lambda i: (0, i, 0)), blk, blk, blk],
        out_specs=[blk] * 4, out_shape=[shp] * 4, compiler_params=_params("parallel"),
    )(parts, w, m, v)


def _sum_rows(parts, name):
    np_, r, c = parts.shape

    def body(p_ref, o_ref):
        g = p_ref[0]
        for d in range(1, np_):
            g = g + p_ref[d]
        o_ref[...] = g

    return pl.pallas_call(body, name=name, out_shape=jax.ShapeDtypeStruct((r, c), F32))(parts)


def _pad_w_in(w):
    d = w.shape[1]
    n_ba = 2 * DN_HEADS
    a = w[:SC_OFF]
    ba = w[SC_OFF:SC_OFF + n_ba]
    sc = w[SC_OFF + n_ba:]
    return jnp.concatenate([a, sc, ba, jnp.zeros((BA_W - n_ba, d), w.dtype)], axis=0)


def _unpad_w_in(wp):
    n_ba = 2 * DN_HEADS
    return jnp.concatenate([wp[:SC_OFF], wp[BA_OFF:BA_OFF + n_ba], wp[SC_OFF:BA_OFF]], axis=0)


def _lane_row(v, off):
    return jnp.pad(v.astype(F32), (off, LANES - off - v.shape[0]))[None]


TT = 256
NB_INTRA = 8
CB_SCAN = 8


def _layer_fwd(x, p, ride_proj, ride_ffn, late):
    t, d = x.shape
    got_proj = got_ffn = None
    if ride_proj:
        h, proj, got_proj = _norm_matmul(x, p["norm1_g"], p["w_in"], "proj_fwd_gather", ride=ride_proj)
    else:
        h, proj = _norm_matmul(x, p["norm1_g"], p["w_in"], "proj_fwd")
    p = {**p, **late(got_proj)}
    qkvn, bg = _tok_call(
        "dn_pre", lambda tv, cv: (_dn_pre_math(*tv, *cv), ()), t, TT,
        [(proj, QKV_W, 0, "cur"), (proj, QKV_W, 0, "prev"), (proj, LANES, BA_OFF // LANES, "cur")],
        [p["dn_conv_w"], p["alog_row"], p["dt_row"]],
        [(TT, QKV_W, F32), (TT, LANES, F32)], [])
    u, w, qd, kd, qk, glb, tinv = _delta_intra(qkvn, bg, NB_INTRA)
    o, s_all = _delta_scan(u, w, qd, kd, qk, glb, CB_SCAN)
    cb0 = SC_OFF // SC_WIDTH
    mix_in = [(o, DN_WIDTH, 0, "cur"), (proj, DN_WIDTH, Z_OFF // DN_WIDTH, "cur"),
              (proj, SC_WIDTH, cb0, "cur"), (proj, SC_WIDTH, cb0 + 1, "cur"), (proj, SC_WIDTH, cb0 + 1, "prev"),
              (proj, SC_WIDTH, cb0 + 2, "cur"), (proj, SC_WIDTH, cb0 + 2, "prev")]
    mix_const = [p["dn_norm_g"], p["sc_norm_g"], p["sc_conv_w"]]
    (cat,) = _tok_call("mix_post", lambda tv, cv: ((_mix_math(*tv, *cv),), ()), t, TT,
                       mix_in, mix_const, [(TT, 2 * DN_WIDTH, MXU_DTYPE)], [])
    x_mid = _matmul(cat, p["w_out"], "nn", F32, "out_proj", residual=x)
    if ride_ffn:
        h2, gu, act, got_ffn = _ffn_up_swiglu(x_mid, p["norm2_g"], p["w_gu"], "ffn_up_gather", ride=ride_ffn)
    else:
        h2, gu, act = _ffn_up_swiglu(x_mid, p["norm2_g"], p["w_gu"], "ffn_up")
    x_out = _matmul(act, p["w_down"], "nn", F32, "ffn_down", residual=x_mid)
    saved = dict(x=x, h=h, proj=proj, qkvn=qkvn, bg=bg, u=u, w=w, qd=qd, kd=kd, qk=qk, glb=glb, tinv=tinv, s_all=s_all, o=o,
                 cat=cat, x_mid=x_mid, h2=h2, gu=gu, act=act, mix_in=mix_in, mix_const=mix_const)
    return x_out, saved, p, got_proj, got_ffn


def _layer_bwd(dx_out, p, s, ride_prev, scatter_of):
    t, d = dx_out.shape
    got = {}
    dgu = _ffn_down_dx_swiglu(dx_out, p["w_down"], s["gu"], "ffn_down_dx")
    d_w_down = _matmul(s["act"], dx_out, "tn", MXU_DTYPE, "ffn_down_dw")
    if ride_prev:
        dx_mid, d_norm2, got["prev"] = _matmul_norm_bwd(dgu, p["w_gu"], s["x_mid"], p["norm2_g"], dx_out,
                                                        "ffn_up_dx_scatter", ride=ride_prev, w_rows_k=True)
    else:
        dx_mid, d_norm2 = _matmul_norm_bwd(dgu, p["w_gu"], s["x_mid"], p["norm2_g"], dx_out, "ffn_up_dx",
                                           w_rows_k=True)
    d_w_gu, got["down"] = _matmul(dgu, s["h2"], "tn", MXU_DTYPE, "ffn_up_dw_scatter", ride=scatter_of(d_w_down))
    d_gate, d_up = _split_gu(d_w_gu)
    dcat = _matmul(dx_mid, p["w_out"], "nt", F32, "out_proj_dx")
    d_w_out = _matmul(s["cat"], dx_mid, "tn", MXU_DTYPE, "out_proj_dw")

    def mix_bwd(tv, cv):
        prim = tuple(tv[:7]) + tuple(cv)
        _, vjp = jax.vjp(_mix_math, *prim)
        do, dz, dgb, dgc, dgch, dhv, dhvh, ddng, dscg, dscw = vjp(tv[7])
        return (do, dz, dgb, dgc, dgch, dhv, dhvh), (ddng, dscg, dscw)

    wide = (TT, DN_WIDTH, F32)
    final = (TT, DN_WIDTH, MXU_DTYPE)
    halo = (HALO, SC_WIDTH, F32)
    do, dz, dgb, dgc, dgc_h, dhv, dhv_h, d_dn_norm, d_sc_norm, d_sc_conv = _tok_call(
        "mix_post_bwd", mix_bwd, t, TT, s["mix_in"] + [(dcat, 2 * DN_WIDTH, 0, "cur")], s["mix_const"],
        [wide, final, final, wide, halo, wide, halo],
        [((1, HEAD_DIM), F32), ((1, SC_WIDTH), F32), ((SC_CONV, SC_WIDTH), F32)])
    cts = _delta_scan_bwd(s["u"], s["w"], s["qd"], s["kd"], s["qk"], s["glb"], s["s_all"], do, CB_SCAN)
    dq, dk, dv, dbg = _delta_intra_bwd(s["qkvn"], s["bg"], s["tinv"], cts, NB_INTRA)
    proj = s["proj"]

    def with_halo(cur, nxt):
        return cur + jnp.concatenate([jnp.zeros((TT - HALO, cur.shape[1]), F32), nxt], axis=0)

    def dn_pre_bwd(tv, cv, carried):
        cur, hal, ba, dq_, dk_, dv_, dbg_, dz_, dgb_, dgc_, dgc_n, dhv_, dhv_n = tv
        _, vjp = jax.vjp(_dn_pre_math, cur, hal, ba, *cv)
        dcur, dhal, dba, dcw, dal, ddt = vjp((jnp.concatenate([dq_, dk_, dv_], axis=1), dbg_))
        dproj_rows = jnp.concatenate([with_halo(dcur, carried[0]), dz_.astype(F32), dgb_.astype(F32), with_halo(dgc_, dgc_n),
                                      with_halo(dhv_, dhv_n), dba, jnp.zeros((TT, BA_W - LANES), F32)], axis=1)
        return (dproj_rows,), (dcw, dal, ddt), (dhal,)

    dproj, d_dn_conv, d_alog, d_dt = _tok_call(
        "dn_pre_bwd", dn_pre_bwd, t, TT,
        [(proj, QKV_W, 0, "cur"), (proj, QKV_W, 0, "prev"), (proj, LANES, BA_OFF // LANES, "cur"),
         (dq, DN_WIDTH, 0, "cur"), (dk, DN_WIDTH, 0, "cur"), (dv, DN_WIDTH, 0, "cur"), (dbg, LANES, 0, "cur"),
         (dz, DN_WIDTH, 0, "cur"), (dgb, SC_WIDTH, 0, "cur"), (dgc, SC_WIDTH, 0, "cur"), (dgc_h, SC_WIDTH, 0, "next8"),
         (dhv, SC_WIDTH, 0, "cur"), (dhv_h, SC_WIDTH, 0, "next8")],
        [p["dn_conv_w"], p["alog_row"], p["dt_row"]],
        [(TT, PROJ_W, MXU_DTYPE)],
        [((DN_CONV, QKV_W), F32), ((1, LANES), F32), ((1, LANES), F32)],
        carry=[((HALO, QKV_W), F32)])
    dx_in, d_norm1, got["gate"] = _matmul_norm_bwd(dproj, p["w_in"], s["x"], p["norm1_g"], dx_mid,
                                                   "proj_dx_scatter", ride=scatter_of(d_gate), w_rows_k=True)
    d_w_in, got["up"] = _matmul(dproj, s["h"], "tn", MXU_DTYPE, "proj_dw_scatter", ride=scatter_of(d_up))
    grads = dict(w_in=d_w_in, w_out=d_w_out, w_gu=d_w_gu, w_down=d_w_down, norm1_g=d_norm1, norm2_g=d_norm2,
                 dn_norm_g=d_dn_norm, sc_norm_g=d_sc_norm, sc_conv_w=d_sc_conv, dn_conv_w=d_dn_conv,
                 alog=d_alog, dt=d_dt)
    return dx_in, grads, got


def _final_loss(x, g, target):
    t, d = x.shape

    def fn(tv, cv):
        xv, tg = tv

        def loss_fn(xx, gg):
            err = jnp.square(_rms_norm(xx, gg) - tg)
            return 0.5 * jnp.sum(jnp.mean(err, axis=-1))

        loss, vjp = jax.vjp(loss_fn, xv, cv[0])
        dx, dg = vjp(jnp.ones((), F32))
        return (dx,), (jnp.full((1, LANES), loss, F32), dg)

    return _tok_call("final_loss", fn, t, TT, [(x, d, 0, "cur"), (target, d, 0, "cur")], [g],
                     [(TT, d, F32)], [((1, LANES), F32), ((1, d), F32)])


def _pack_rows(arrs):
    rows, offs, r0 = [], [], 0
    for a in arrs:
        n = a.size
        nr = -(-n // LANES)
        flat = jnp.pad(a.reshape(-1).astype(F32), (0, nr * LANES - n))
        rows.append(flat.reshape(nr, LANES))
        offs.append((r0, nr, a.shape))
        r0 += nr
    pad = (-r0) % 8
    if pad:
        rows.append(jnp.zeros((pad, LANES), F32))
    return jnp.concatenate(rows, axis=0), offs


def _unpack_rows(packed, offs):
    out = []
    for r0, nr, shp in offs:
        n = 1
        for s_ in shp:
            n *= s_
        out.append(packed[r0:r0 + nr].reshape(-1)[:n].reshape(shp))
    return out


def kernel(x, norm1_g, w_in, dn_conv_w, dn_a_log, dn_dt_bias, dn_norm_g, sc_conv_w, sc_norm_g, w_out, norm2_g, ffn_w_gate, ffn_w_up, ffn_w_down, final_norm_g, loss_target, m_norm1_g, m_w_in, m_dn_conv_w, m_dn_a_log, m_dn_dt_bias, m_dn_norm_g, m_sc_conv_w, m_sc_norm_g, m_w_out, m_norm2_g, m_ffn_w_gate, m_ffn_w_up, m_ffn_w_down, m_final_norm_g, v_norm1_g, v_w_in, v_dn_conv_w, v_dn_a_log, v_dn_dt_bias, v_dn_norm_g, v_sc_conv_w, v_sc_norm_g, v_w_out, v_norm2_g, v_ffn_w_gate, v_ffn_w_up, v_ffn_w_down, v_final_norm_g):
    depth, d, cin = w_in.shape
    t = x.shape[1]
    dff_s = ffn_w_gate.shape[2]
    tr = lambda a: a.transpose(0, 2, 1)
    gate_t, up_t = tr(ffn_w_gate), tr(ffn_w_up)
    by_layer = lambda a: [a[l].T for l in range(depth)]
    win_t = by_layer(w_in)
    me = _my_index()
    x2 = x.reshape(t, d)
    tgt = loss_target.reshape(t, d)

    def shards(l):
        return [a[l].astype(MXU_DTYPE) for a in (win_t, w_out, gate_t, up_t, ffn_w_down)]

    conv_pack, conv_offs = _pack_rows([dn_conv_w, sc_conv_w])
    conv_all, *first = _all_gather([conv_pack] + shards(0)[:2], "gather_first")
    dn_parts, sc_parts = zip(*[_unpack_rows(conv_all[j], conv_offs) for j in range(N_DEV)])
    dn_conv_full = jnp.concatenate(dn_parts, axis=2)
    sc_conv_full = jnp.concatenate(sc_parts, axis=2)

    def mixer_params(l, g_in, g_out):
        return dict(
            w_in=_pad_w_in(g_in.reshape(N_DEV * cin, d)), w_out=g_out.reshape(d, d),
            norm1_g=norm1_g[l][None], norm2_g=norm2_g[l][None], dn_norm_g=dn_norm_g[l][None],
            sc_norm_g=sc_norm_g[l][None], dn_conv_w=dn_conv_full[l], sc_conv_w=sc_conv_full[l],
            alog_row=_lane_row(dn_a_log[l], DN_HEADS), dt_row=_lane_row(dn_dt_bias[l], DN_HEADS))

    def ffn_params(g_gate, g_up, g_down):
        dff = N_DEV * dff_s
        return dict(w_gu=_interleave_gu(g_gate.reshape(dff, d), g_up.reshape(dff, d)), w_down=g_down.reshape(dff, d))

    nxt = mixer_params(0, *first)
    nxt_ffn = None
    params, saved = [], []
    xc = x2
    for l in range(depth):
        last = l + 1 == depth
        own_ffn = shards(l)[2:] if nxt_ffn is None else []
        ahead = [] if last else shards(l + 1)
        ride_proj = _Exchange("gather", own_ffn + ahead[:2]) if own_ffn or ahead else None
        ride_ffn = None if last else _Exchange("gather", ahead[2:])

        def late(got, own_ffn=own_ffn, nxt_ffn=nxt_ffn):
            return ffn_params(*got[:3]) if own_ffn else nxt_ffn

        xc, s, p_l, got_proj, got_ffn = _layer_fwd(xc, nxt, ride_proj, ride_ffn, late)
        params.append(p_l)
        saved.append(s)
        if not last:
            nxt = mixer_params(l + 1, *got_proj[len(own_ffn):])
            nxt_ffn = ffn_params(*got_ffn)
    dx, loss_part, d_final = _final_loss(xc, final_norm_g[None], tgt)

    names = ("w_in", "w_out", "ffn_w_gate", "ffn_w_up", "ffn_w_down")
    big_out = {n: {k: [None] * depth for k in ("g", "d", "m", "v")} for n in names}
    w_loc = dict(w_in=win_t, w_out=w_out, ffn_w_gate=gate_t, ffn_w_up=up_t, ffn_w_down=ffn_w_down)
    m_loc = dict(w_in=by_layer(m_w_in), w_out=m_w_out, ffn_w_gate=tr(m_ffn_w_gate), ffn_w_up=tr(m_ffn_w_up), ffn_w_down=m_ffn_w_down)
    v_loc = dict(w_in=by_layer(v_w_in), w_out=v_w_out, ffn_w_gate=tr(v_ffn_w_gate), ffn_w_up=tr(v_ffn_w_up), ffn_w_down=v_ffn_w_down)


    def scatter_of(grad):
        return _Exchange("scatter", [grad.reshape(N_DEV, dff_s, d).astype(MXU_DTYPE)])

    def apply(l, which, recv):
        for n, r in zip(which, recv):
            res = _sum_adamw(r, w_loc[n][l], m_loc[n][l], v_loc[n][l], "adamw_" + n)
            for k, a in zip(("g", "d", "m", "v"), res):
                big_out[n][k][l] = a

    grads = [None] * depth
    pending = None
    for l in reversed(range(depth)):
        ride_prev = _Exchange("scatter", pending) if pending else None
        dx, grads[l], got = _layer_bwd(dx, params[l], saved[l], ride_prev, scatter_of)
        if ride_prev:
            apply(l + 1, names[:2], got["prev"])
        apply(l, names[2:], [*got["gate"], *got["up"], *got["down"]])
        pending = [_unpad_w_in(grads[l]["w_in"]).reshape(N_DEV, cin, d).astype(MXU_DTYPE),
                   grads[l]["w_out"].reshape(N_DEV, d // N_DEV, d).astype(MXU_DTYPE)]
    apply(0, names[:2], _Exchange("scatter", pending).run("scatter_last"))
    grad_x = dx.reshape(x.shape)
    big_out = {n: {k: jnp.stack(v_, axis=1 if n == "w_in" else 0) for k, v_ in o.items()} for n, o in big_out.items()}
    for n in ("ffn_w_gate", "ffn_w_up"):
        big_out[n] = {k: tr(a) for k, a in big_out[n].items()}
    big_out["w_in"] = {k: a.transpose(1, 2, 0) for k, a in big_out["w_in"].items()}

    stack = lambda key: jnp.stack([grads[l][key] for l in range(depth)])
    small_parts = [stack("norm1_g").reshape(depth, d), stack("norm2_g").reshape(depth, d), d_final.reshape(d),
                   stack("dn_norm_g").reshape(depth, HEAD_DIM), stack("sc_norm_g").reshape(depth, SC_WIDTH),
                   stack("alog").reshape(depth, LANES), stack("dt").reshape(depth, LANES),
                   stack("dn_conv_w"), stack("sc_conv_w"), loss_part]
    small_pack, small_offs = _pack_rows(small_parts)
    (small_all,) = _all_gather([small_pack], "gather_small")
    total = _sum_rows(small_all, "sum_small")
    (g_n1, g_n2, g_fin, g_dnn, g_scn, g_alog, g_dt, g_dnc, g_scc, loss_row) = _unpack_rows(total, small_offs)
    loss = loss_row[0, 0]
    g_alog = g_alog[:, DN_HEADS:2 * DN_HEADS]
    g_dt = g_dt[:, DN_HEADS:2 * DN_HEADS]
    dnc_w = dn_conv_w.shape[2]
    scc_w = sc_conv_w.shape[2]
    g_dnc = lax.dynamic_slice_in_dim(g_dnc, me * dnc_w, dnc_w, axis=2)
    g_scc = lax.dynamic_slice_in_dim(g_scc, me * scc_w, scc_w, axis=2)
    sm_g = [g_n1, g_dnc, g_alog, g_dt, g_dnn, g_scc, g_scn, g_n2, g_fin]
    sm_w = [norm1_g, dn_conv_w, dn_a_log, dn_dt_bias, dn_norm_g, sc_conv_w, sc_norm_g, norm2_g, final_norm_g]
    sm_m = [m_norm1_g, m_dn_conv_w, m_dn_a_log, m_dn_dt_bias, m_dn_norm_g, m_sc_conv_w, m_sc_norm_g, m_norm2_g, m_final_norm_g]
    sm_v = [v_norm1_g, v_dn_conv_w, v_dn_a_log, v_dn_dt_bias, v_dn_norm_g, v_sc_conv_w, v_sc_norm_g, v_norm2_g, v_final_norm_g]
    pg, offs = _pack_rows(sm_g)
    pw, _ = _pack_rows(sm_w)
    pm, _ = _pack_rows(sm_m)
    pv, _ = _pack_rows(sm_v)
    sg, sd, sm_, sv = _sum_adamw(pg[None], pw, pm, pv, "adamw_small")
    small_out = {k: _unpack_rows(a, offs) for k, a in zip(("g", "d", "m", "v"), (sg, sd, sm_, sv))}

    def outputs(k):
        s_ = small_out[k]
        b = big_out
        return [s_[0], b["w_in"][k], s_[1], s_[2], s_[3], s_[4], s_[5], s_[6], b["w_out"][k], s_[7],
                b["ffn_w_gate"][k], b["ffn_w_up"][k], b["ffn_w_down"][k], s_[8]]

    return (loss, grad_x, *outputs("g"), *outputs("d"), *outputs("m"), *outputs("v"))
```

```python
import functools

import jax
import jax.numpy as jnp
from jax import lax
from jax.experimental import pallas as pl
from jax.experimental.pallas import tpu as pltpu

F32 = jnp.float32
MXU_DTYPE = jnp.bfloat16
MESH = pl.DeviceIdType.MESH

N_DEV = 8
EPS = 1e-6
DN_HEADS = 4
HEAD_DIM = 128
DN_WIDTH = DN_HEADS * HEAD_DIM
SC_WIDTH = 512
SC_GROUPS = 4
DN_CONV = 4
SC_CONV = 3
CHUNK = 64
HALO = 8
LANES = 128

QKV_W = 3 * DN_WIDTH
Z_OFF = QKV_W
SC_OFF = Z_OFF + DN_WIDTH
BA_OFF = SC_OFF + 3 * SC_WIDTH
BA_W = 256
PROJ_W = BA_OFF + BA_W

ADAM_LR = 0.001
ADAM_B1 = 0.9
ADAM_B2 = 0.999
ADAM_EPS = 1e-08
ADAM_WD = 0.01
ADAM_STEP = 10


def _pick(n, cands):
    for c in cands:
        if n % c == 0:
            return c
    return n


def _params(*sem):
    return pltpu.CompilerParams(dimension_semantics=sem)


def _rms_norm(x, g):
    return x * lax.rsqrt(jnp.mean(x * x, axis=-1, keepdims=True) + EPS) * g


def _dot(a, b, dims=(((1,), (0,)), ((), ()))):
    return lax.dot_general(a.astype(MXU_DTYPE), b.astype(MXU_DTYPE), dims, preferred_element_type=F32)


def _split_terms(x, terms):
    out = []
    for _ in range(terms):
        hi = x.astype(MXU_DTYPE)
        out.append(hi)
        x = x - hi.astype(F32)
    return out


def _ein_impl(spec, terms, a, b):
    ta, tb = terms
    if ta == 1 and tb == 1:
        return jnp.einsum(spec, a.astype(MXU_DTYPE), b.astype(MXU_DTYPE), preferred_element_type=F32)
    pa, pb = _split_terms(a, ta), _split_terms(b, tb)
    order = max(ta, tb) - 1
    acc = None
    for deg in range(order, -1, -1):
        for i in range(ta):
            j = deg - i
            if 0 <= j < tb:
                t = jnp.einsum(spec, pa[i], pb[j], preferred_element_type=F32)
                acc = t if acc is None else acc + t
    return acc


@functools.partial(jax.custom_vjp, nondiff_argnums=(0, 1))
def _ein(spec, terms, a, b):
    return _ein_impl(spec, terms, a, b)


def _ein_fwd(spec, terms, a, b):
    return _ein_impl(spec, terms, a, b), (a, b)


def _ein_bwd(spec, terms, res, ct):
    a, b = res
    xy, z = spec.split("->")
    x, y = xy.split(",")
    tc = min(max(terms), 2)
    da = _ein_impl(f"{z},{y}->{x}", (tc, terms[1]), ct, b)
    db = _ein_impl(f"{x},{z}->{y}", (terms[0], tc), a, ct)
    return da, db


_ein.defvjp(_ein_fwd, _ein_bwd)

FAST = (1, 1)
PRECISE = (2, 2)
LHS_EXACT = (1, 3)


def _causal_conv(cur, halo, w, k):
    tt = cur.shape[0]
    xp = jnp.concatenate([halo, cur], axis=0)
    y = None
    for j in range(k):
        start = HALO - (k - 1) + j
        term = xp[start:start + tt] * w[j:j + 1]
        y = term if y is None else y + term
    return y


def _matmul(a, b, mode, out_dtype, name, residual=None, ride=None):
    if mode == "nn":
        (m, k), (k2, n) = a.shape, b.shape
    elif mode == "nt":
        (m, k), (n, k2) = a.shape, b.shape
    else:
        (k, m), (k2, n) = a.shape, b.shape
    assert k == k2
    tm = _pick(m, (1024, 1408, 1280, 512, 256, 128))
    tn = _pick(n, (1280, 1408, 1024, 512, 256, 128))
    tk = k if k <= 2816 else _pick(k, (1024, 768, 512))
    gi, gj, nk = m // tm, n // tn, k // tk
    dims = {"nn": (((1,), (0,)), ((), ())), "nt": (((1,), (1,)), ((), ())), "tn": (((0,), (0,)), ((), ()))}[mode]
    a_spec = {"nn": pl.BlockSpec((tm, tk), lambda i, j, q: (i, q)),
              "nt": pl.BlockSpec((tm, tk), lambda i, j, q: (i, q)),
              "tn": pl.BlockSpec((tk, tm), lambda i, j, q: (q, i))}[mode]
    b_spec = {"nn": pl.BlockSpec((tk, tn), lambda i, j, q: (q, j)),
              "nt": pl.BlockSpec((tn, tk), lambda i, j, q: (j, q)),
              "tn": pl.BlockSpec((tk, tn), lambda i, j, q: (q, j))}[mode]
    o_spec = pl.BlockSpec((tm, tn), lambda i, j, q: (i, j))
    has_res = residual is not None
    n_in = 3 if has_res else 2
    nr = ride.na if ride else 0

    def body(*refs):
        a_ref, b_ref = refs[0], refs[1]
        r_ref = refs[2] if has_res else None
        ride_in = refs[n_in:n_in + nr]
        o_ref = refs[n_in + nr]
        ride_out = refs[n_in + nr + 1:n_in + 2 * nr + 1]
        acc = refs[n_in + 2 * nr + 1]
        ride_sems = refs[n_in + 2 * nr + 2:]
        i, j, q = pl.program_id(0), pl.program_id(1), pl.program_id(2)
        if ride:
            @pl.when((i == 0) & (j == 0) & (q == 0))
            def _():
                ride.start(ride_in, ride_out, ride_sems)

        @pl.when(q == 0)
        def _():
            acc[...] = jnp.zeros_like(acc)

        acc[...] += _dot(a_ref[...], b_ref[...], dims)

        @pl.when(q == nk - 1)
        def _():
            r = acc[...]
            if has_res:
                r = r + r_ref[...]
            o_ref[...] = r.astype(o_ref.dtype)

        if ride:
            @pl.when((i == gi - 1) & (j == gj - 1) & (q == nk - 1))
            def _():
                ride.finish(ride_in, ride_out, ride_sems)

    hbm = pl.BlockSpec(memory_space=pltpu.HBM)
    in_specs = [a_spec, b_spec] + ([o_spec] if has_res else []) + [hbm] * nr
    args = (a, b) + ((residual,) if has_res else ()) + (tuple(ride.arrays) if ride else ())
    res = pl.pallas_call(
        body, name=name, grid=(gi, gj, nk), in_specs=in_specs, out_specs=[o_spec] + [hbm] * nr,
        out_shape=[jax.ShapeDtypeStruct((m, n), out_dtype)] + (ride.out_shape if ride else []),
        scratch_shapes=[pltpu.VMEM((tm, tn), F32)] + (ride.scratch if ride else []),
        compiler_params=pltpu.CompilerParams(
            dimension_semantics=("arbitrary",) * 3 if ride else ("parallel", "parallel", "arbitrary"),
            has_side_effects=bool(ride)),
    )(*args)
    return (res[0], res[1:]) if ride else res[0]


def _norm_matmul(x, g, w, name, ride=None):
    t, d = x.shape
    n = w.shape[0]
    tm = _pick(t, (1024, 512, 256, 128))
    tn = _pick(n, (1280, 1408, 1024, 512, 256, 128))
    gi, gj = t // tm, n // tn
    nr = ride.na if ride else 0

    def body(*refs):
        x_ref, g_ref, w_ref = refs[:3]
        ride_in = refs[3:3 + nr]
        h_ref, y_ref = refs[3 + nr], refs[4 + nr]
        ride_out = refs[5 + nr:5 + 2 * nr]
        h_scr = refs[5 + 2 * nr]
        ride_sems = refs[6 + 2 * nr:]
        i, j = pl.program_id(0), pl.program_id(1)
        if ride:
            @pl.when((i == 0) & (j == 0))
            def _():
                ride.start(ride_in, ride_out, ride_sems)

        @pl.when(j == 0)
        def _():
            h = _rms_norm(x_ref[...], g_ref[...]).astype(MXU_DTYPE)
            h_scr[...] = h
            h_ref[...] = h

        y_ref[...] = _dot(h_scr[...], w_ref[...], (((1,), (1,)), ((), ())))

        if ride:
            @pl.when((i == gi - 1) & (j == gj - 1))
            def _():
                ride.finish(ride_in, ride_out, ride_sems)

    hbm = pl.BlockSpec(memory_space=pltpu.HBM)
    res = pl.pallas_call(
        body, name=name, grid=(gi, gj),
        in_specs=[pl.BlockSpec((tm, d), lambda i, j: (i, 0)), pl.BlockSpec((1, d), lambda i, j: (0, 0)),
                  pl.BlockSpec((tn, d), lambda i, j: (j, 0))] + [hbm] * nr,
        out_specs=[pl.BlockSpec((tm, d), lambda i, j: (i, 0)), pl.BlockSpec((tm, tn), lambda i, j: (i, j))] + [hbm] * nr,
        out_shape=[jax.ShapeDtypeStruct((t, d), MXU_DTYPE), jax.ShapeDtypeStruct((t, n), F32)]
        + (ride.out_shape if ride else []),
        scratch_shapes=[pltpu.VMEM((tm, d), MXU_DTYPE)] + (ride.scratch if ride else []),
        compiler_params=pltpu.CompilerParams(
            dimension_semantics=("arbitrary",) * 2 if ride else ("parallel", "arbitrary"),
            has_side_effects=bool(ride)),
    )(x, g, w, *(ride.arrays if ride else ()))
    return (res[0], res[1], res[2:]) if ride else (res[0], res[1])


def _swiglu_math(g, u):
    return jax.nn.silu(g) * u


def _gu_tile(dff):
    return _pick(dff, (1408, 1024, 512, 256, 128))


def _interleave_gu(gate_t, up_t):
    tn = _gu_tile(gate_t.shape[0])
    pieces = []
    for j in range(gate_t.shape[0] // tn):
        pieces += [gate_t[j * tn:(j + 1) * tn], up_t[j * tn:(j + 1) * tn]]
    return jnp.concatenate(pieces, axis=0)


def _split_gu(gu_t):
    dff = gu_t.shape[0] // 2
    tn = _gu_tile(dff)
    tiles = [gu_t[j * tn:(j + 1) * tn] for j in range(2 * dff // tn)]
    return jnp.concatenate(tiles[0::2], axis=0), jnp.concatenate(tiles[1::2], axis=0)


def _ffn_up_swiglu(x, g, w_gu, name, ride=None):
    t, d = x.shape
    dff = w_gu.shape[0] // 2
    tn = _gu_tile(dff)
    tm = _pick(t, (512, 256, 128))
    gj, gi = dff // tn, t // tm
    nr = ride.na if ride else 0

    def body(*refs):
        x_ref, g_ref, w_ref = refs[:3]
        ride_in = refs[3:3 + nr]
        h_ref, gu_ref, act_ref = refs[3 + nr:6 + nr]
        ride_out = refs[6 + nr:6 + 2 * nr]
        ride_sems = refs[6 + 2 * nr:]
        j, i = pl.program_id(0), pl.program_id(1)
        if ride:
            @pl.when((i == 0) & (j == 0))
            def _():
                ride.start(ride_in, ride_out, ride_sems)

        h = _rms_norm(x_ref[...], g_ref[...]).astype(MXU_DTYPE)

        @pl.when(j == 0)
        def _():
            h_ref[...] = h

        y = _dot(h, w_ref[...], (((1,), (1,)), ((), ())))
        gu_ref[...] = y.astype(gu_ref.dtype)
        act_ref[...] = _swiglu_math(y[:, :tn], y[:, tn:]).astype(act_ref.dtype)

        if ride:
            @pl.when((i == gi - 1) & (j == gj - 1))
            def _():
                ride.finish(ride_in, ride_out, ride_sems)

    hbm = pl.BlockSpec(memory_space=pltpu.HBM)
    res = pl.pallas_call(
        body, name=name, grid=(gj, gi),
        in_specs=[pl.BlockSpec((tm, d), lambda j, i: (i, 0)), pl.BlockSpec((1, d), lambda j, i: (0, 0)),
                  pl.BlockSpec((2 * tn, d), lambda j, i: (j, 0))] + [hbm] * nr,
        out_specs=[pl.BlockSpec((tm, d), lambda j, i: (jnp.where(j == 0, i, gi - 1), 0)),
                   pl.BlockSpec((tm, 2 * tn), lambda j, i: (i, j)),
                   pl.BlockSpec((tm, tn), lambda j, i: (i, j))] + [hbm] * nr,
        out_shape=[jax.ShapeDtypeStruct((t, d), MXU_DTYPE), jax.ShapeDtypeStruct((t, 2 * dff), MXU_DTYPE),
                   jax.ShapeDtypeStruct((t, dff), MXU_DTYPE)] + (ride.out_shape if ride else []),
        scratch_shapes=(ride.scratch if ride else []),
        compiler_params=pltpu.CompilerParams(dimension_semantics=("arbitrary", "arbitrary"),
                                             has_side_effects=bool(ride)),
    )(x, g, w_gu, *(ride.arrays if ride else ()))
    return (res[0], res[1], res[2], res[3:]) if ride else tuple(res)


def _ffn_down_dx_swiglu(dx_out, w_down, gu, name):
    t, d = dx_out.shape
    dff = w_down.shape[0]
    tn = _gu_tile(dff)
    tm = _pick(t, (512, 256, 128))

    def body(dx_ref, w_ref, gu_ref, o_ref):
        dact = _dot(dx_ref[...], w_ref[...], (((1,), (1,)), ((), ())))
        gu_v = gu_ref[...].astype(F32)
        _, vjp = jax.vjp(_swiglu_math, gu_v[:, :tn], gu_v[:, tn:])
        dg, du = vjp(dact)
        o_ref[...] = jnp.concatenate([dg, du], axis=1).astype(o_ref.dtype)

    return pl.pallas_call(
        body, name=name, grid=(dff // tn, t // tm),
        in_specs=[pl.BlockSpec((tm, d), lambda j, i: (i, 0)), pl.BlockSpec((tn, d), lambda j, i: (j, 0)),
                  pl.BlockSpec((tm, 2 * tn), lambda j, i: (i, j))],
        out_specs=pl.BlockSpec((tm, 2 * tn), lambda j, i: (i, j)),
        out_shape=jax.ShapeDtypeStruct((t, 2 * dff), MXU_DTYPE),
        compiler_params=_params("parallel", "parallel"),
    )(dx_out, w_down, gu)


def _matmul_norm_bwd(dy, w, x, g, dres, name, ride=None, w_rows_k=False):
    t, k = dy.shape
    d = w.shape[1] if w_rows_k else w.shape[0]
    tm = _pick(t, (1024, 512, 256, 128))
    tk = k if k <= 2816 else _pick(k, (1408, 1280, 1024, 768, 512))
    tr = _pick(tm, (256, 128))
    gi, nk = t // tm, k // tk
    nr = ride.na if ride else 0

    def body(*refs):
        dy_ref, w_ref, x_ref, g_ref, dres_ref = refs[:5]
        ride_in = refs[5:5 + nr]
        dx_ref, dg_ref = refs[5 + nr], refs[6 + nr]
        ride_out = refs[7 + nr:7 + 2 * nr]
        acc = refs[7 + 2 * nr]
        ride_sems = refs[8 + 2 * nr:]
        i, q = pl.program_id(0), pl.program_id(1)

        @pl.when((i == 0) & (q == 0))
        def _():
            dg_ref[...] = jnp.zeros_like(dg_ref)
            if ride:
                ride.start(ride_in, ride_out, ride_sems)

        @pl.when(q == 0)
        def _():
            acc[...] = jnp.zeros_like(acc)

        acc[...] += _dot(dy_ref[...], w_ref[...], (((1,), (0 if w_rows_k else 1,)), ((), ())))

        @pl.when(q == nk - 1)
        def _():
            for r in range(tm // tr):
                rows = slice(r * tr, (r + 1) * tr)
                _, vjp = jax.vjp(_rms_norm, x_ref[rows], g_ref[...])
                dxn, dg = vjp(acc[rows])
                dx_ref[rows] = dres_ref[rows] + dxn
                dg_ref[...] += dg

        if ride:
            @pl.when((i == gi - 1) & (q == nk - 1))
            def _():
                ride.finish(ride_in, ride_out, ride_sems)

    hbm = pl.BlockSpec(memory_space=pltpu.HBM)
    row = pl.BlockSpec((tm, d), lambda i, q: (i, 0))
    res = pl.pallas_call(
        body, name=name, grid=(gi, nk),
        in_specs=[pl.BlockSpec((tm, tk), lambda i, q: (i, q)),
                  pl.BlockSpec((tk, d), lambda i, q: (q, 0)) if w_rows_k else pl.BlockSpec((d, tk), lambda i, q: (0, q)),
                  row,
                  pl.BlockSpec((1, d), lambda i, q: (0, 0)), row] + [hbm] * nr,
        out_specs=[row, pl.BlockSpec((1, d), lambda i, q: (0, 0))] + [hbm] * nr,
        out_shape=[jax.ShapeDtypeStruct((t, d), F32), jax.ShapeDtypeStruct((1, d), F32)]
        + (ride.out_shape if ride else []),
        scratch_shapes=[pltpu.VMEM((tm, d), F32)] + (ride.scratch if ride else []),
        compiler_params=pltpu.CompilerParams(dimension_semantics=("arbitrary", "arbitrary"),
                                             has_side_effects=bool(ride)),
    )(dy, w, x, g, dres, *(ride.arrays if ride else ()))
    return (res[0], res[1], res[2:]) if ride else (res[0], res[1])


def _tok_call(name, fn, t, tt, tok_in, const_in, tok_out, acc_out, carry=()):
    nblk = t // tt
    hb = tt // HALO
    blk = (lambda i: nblk - 1 - i) if carry else (lambda i: i)
    in_specs, args = [], []
    for arr, w, cb, mode in tok_in:
        if mode == "cur":
            spec = pl.BlockSpec((tt, w), lambda i, cb=cb: (blk(i), cb))
        elif mode == "prev":
            spec = pl.BlockSpec((HALO, w), lambda i, cb=cb: (jnp.maximum(blk(i) * hb - 1, 0), cb))
        else:
            spec = pl.BlockSpec((HALO, w), lambda i, cb=cb: (jnp.minimum(blk(i) + 1, nblk - 1), cb))
        in_specs.append(spec)
        args.append(arr)
    for arr in const_in:
        in_specs.append(pl.BlockSpec(arr.shape, lambda i: (0, 0)))
        args.append(arr)
    out_specs, out_shape = [], []
    for rows, w, dt in tok_out:
        out_specs.append(pl.BlockSpec((rows, w), lambda i: (blk(i), 0)))
        out_shape.append(jax.ShapeDtypeStruct((nblk * rows, w), dt))
    for shp, dt in acc_out:
        out_specs.append(pl.BlockSpec(shp, lambda i: (0, 0)))
        out_shape.append(jax.ShapeDtypeStruct(shp, dt))
    n_tok, n_const, n_out, n_acc = len(tok_in), len(const_in), len(tok_out), len(acc_out)

    def body(*refs):
        i = pl.program_id(0)
        b = blk(i)
        tok_vals = []
        for (_, _, _, mode), r in zip(tok_in, refs[:n_tok]):
            v = r[...]
            if mode == "prev":
                v = jnp.where(b > 0, v, jnp.zeros_like(v))
            elif mode == "next8":
                v = jnp.where(b < nblk - 1, v, jnp.zeros_like(v))
            tok_vals.append(v)
        const_vals = [r[...] for r in refs[n_tok:n_tok + n_const]]
        o_refs = refs[n_tok + n_const:n_tok + n_const + n_out]
        a_refs = refs[n_tok + n_const + n_out:n_tok + n_const + n_out + n_acc]
        c_refs = refs[n_tok + n_const + n_out + n_acc:]
        if a_refs or c_refs:
            @pl.when(i == 0)
            def _():
                for r in (*a_refs, *c_refs):
                    r[...] = jnp.zeros_like(r)

        if c_refs:
            outs, accs, carried = fn(tok_vals, const_vals, [r[...] for r in c_refs])
            for r, v in zip(c_refs, carried):
                r[...] = v
        else:
            outs, accs = fn(tok_vals, const_vals)
        for r, v in zip(o_refs, outs):
            r[...] = v.astype(r.dtype)
        for r, v in zip(a_refs, accs):
            r[...] += v.astype(r.dtype)

    res = pl.pallas_call(
        body, name=name, grid=(nblk,), in_specs=in_specs, out_specs=out_specs, out_shape=out_shape,
        scratch_shapes=[pltpu.VMEM(shp, dt) for shp, dt in carry],
        compiler_params=_params("arbitrary" if acc_out or carry else "parallel"),
    )(*args)
    return res


def _dn_pre_math(cur, halo, ba, cw, alog, dtb):
    tt = cur.shape[0]
    a = jax.nn.silu(_causal_conv(cur, halo, cw, DN_CONV))
    pieces = []
    for p in range(2 * DN_HEADS):
        xh = a[:, p * HEAD_DIM:(p + 1) * HEAD_DIM]
        xh = xh * lax.rsqrt(jnp.sum(xh * xh, axis=-1, keepdims=True) + EPS)
        if p < DN_HEADS:
            xh = xh * (HEAD_DIM ** -0.5)
        pieces.append(xh)
    pieces.append(a[:, 2 * DN_WIDTH:])
    qkvn = jnp.concatenate(pieces, axis=1)
    lane = lax.broadcasted_iota(jnp.int32, ba.shape, 1)
    raw = jnp.where(lane < DN_HEADS, jax.nn.sigmoid(ba), -jnp.exp(alog) * jax.nn.softplus(ba + dtb))
    r = lax.broadcasted_iota(jnp.int32, (tt, tt), 0)
    c = lax.broadcasted_iota(jnp.int32, (tt, tt), 1)
    tri = jnp.where((r // CHUNK == c // CHUNK) & (c <= r), 1.0, 0.0).astype(F32)
    cums = _ein("ij,jk->ik", LHS_EXACT, tri, raw)
    bg = jnp.where(lane < DN_HEADS, raw, cums)
    return qkvn, bg


def _mix_math(o, z, gb, gc, gc_halo, hv, hv_halo, dng, scg, scw):
    outs = []
    for h in range(DN_HEADS):
        sl = slice(h * HEAD_DIM, (h + 1) * HEAD_DIM)
        oh = o[:, sl]
        outs.append(oh * lax.rsqrt(jnp.mean(oh * oh, axis=-1, keepdims=True) + EPS) * dng * jax.nn.silu(z[:, sl]))
    y = gb * _causal_conv(gc * hv, gc_halo * hv_halo, scw, SC_CONV)
    gw = SC_WIDTH // SC_GROUPS
    for g in range(SC_GROUPS):
        sl = slice(g * gw, (g + 1) * gw)
        yg = y[:, sl]
        outs.append(yg * lax.rsqrt(jnp.mean(yg * yg, axis=-1, keepdims=True) + EPS) * scg[:, sl])
    return jnp.concatenate(outs, axis=1)


def _tri_inverse(a):
    c = a.shape[-1]
    r = lax.broadcasted_iota(jnp.int32, (c, c), 0)
    q = lax.broadcasted_iota(jnp.int32, (c, c), 1)
    eye = jnp.where(r == q, 1.0, 0.0).astype(F32)[None]
    blk = (r // 16 == q // 16)[None]
    d = jnp.where(blk, a, 0.0)
    o = a - d
    mm = functools.partial(_ein, "bij,bjk->bik", PRECISE)
    p = eye - d
    n = mm(d, d)
    for _ in range(2):
        both = mm(jnp.concatenate([n, p], axis=1), n)
        n = both[:, :c]
        p = p + both[:, c:]
    p = p + mm(p, n)
    e = mm(p, o)
    e2 = mm(e, e)
    left = eye - e + e2 - mm(e, e2)
    return mm(left, p)


@jax.custom_vjp
def _inverse_known(a, tinv):
    return tinv


def _inverse_known_fwd(a, tinv):
    return tinv, tinv


def _inverse_known_bwd(tinv, ct):
    left = _ein("bji,bjk->bik", PRECISE, tinv, ct)
    return -_ein("bik,bjk->bij", PRECISE, left, tinv), jnp.zeros_like(tinv)


_inverse_known.defvjp(_inverse_known_fwd, _inverse_known_bwd)


def _delta_intra_math(q, k, v, bg, head, tinv_known=None):
    n = q.shape[0]
    nb = n // CHUNK
    lane = lax.broadcasted_iota(jnp.int32, bg.shape, 1)
    beta = jnp.sum(jnp.where(lane == head, bg, 0.0), axis=1, keepdims=True).reshape(nb, CHUNK, 1)
    gc = jnp.sum(jnp.where(lane == head + DN_HEADS, bg, 0.0), axis=1, keepdims=True).reshape(nb, CHUNK, 1)
    q3, k3, v3 = (a.reshape(nb, CHUNK, HEAD_DIM) for a in (q, k, v))
    r = lax.broadcasted_iota(jnp.int32, (CHUNK, CHUNK), 0)
    c = lax.broadcasted_iota(jnp.int32, (CHUNK, CHUNK), 1)
    eye = jnp.where(r == c, 1.0, 0.0).astype(F32)[None]
    gcr = _ein("bik,bkj->bij", LHS_EXACT, jnp.ones((nb, CHUNK, CHUNK), F32), gc * eye)
    decay = jnp.exp(jnp.where((r >= c)[None], gc - gcr, -1e30))
    kb = k3 * beta
    vb = v3 * beta
    egc = jnp.exp(gc)
    on_k = _ein("bcd,bmd->bcm", FAST, jnp.concatenate([kb, q3], axis=1), k3)
    a = jnp.where((r > c)[None], on_k[:, :CHUNK] * decay, 0.0)
    tinv = _tri_inverse(a) if tinv_known is None else _inverse_known(a, tinv_known)
    uw = _ein("bcm,bmd->bcd", PRECISE, tinv, jnp.concatenate([vb, kb * egc], axis=2))
    u, w = uw[:, :, :HEAD_DIM], uw[:, :, HEAD_DIM:]
    qk = on_k[:, CHUNK:] * decay
    row = lax.broadcasted_iota(jnp.int32, (nb, CHUNK, 1), 1)
    glast = jnp.sum(jnp.where(row == CHUNK - 1, gc, 0.0), axis=1, keepdims=True)
    qd = q3 * egc
    kd = k3 * jnp.exp(glast - gc)
    glb = jnp.broadcast_to(jnp.exp(glast), (nb, HALO, LANES))
    flat = lambda x: x.reshape(n, HEAD_DIM)
    return flat(u), flat(w), flat(qd), flat(kd), qk, glb, tinv


def _delta_step_math(u, w, qd, kd, qk, gl, s):
    c = u.shape[0]
    on_s = _ein("ck,kv->cv", FAST, jnp.concatenate([w, qd], axis=0), s)
    vnew = u - on_s[:c]
    on_v = _ein("cm,mv->cv", FAST, jnp.concatenate([qk, kd.T], axis=0), vnew)
    o = on_s[c:] + on_v[:c]
    s2 = s * gl + on_v[c:]
    return o, s2


def _delta_intra(qkvn, bg, nb):
    t = qkvn.shape[0]
    n = t // CHUNK
    rows = nb * CHUNK

    def body(q_ref, k_ref, v_ref, bg_ref, u_ref, w_ref, qd_ref, kd_ref, qk_ref, gl_ref, ti_ref):
        outs = _delta_intra_math(q_ref[...], k_ref[...], v_ref[...], bg_ref[...], pl.program_id(1))
        for r, v in zip((u_ref, w_ref, qd_ref, kd_ref, qk_ref, gl_ref, ti_ref), outs):
            r[...] = v

    col = lambda off: pl.BlockSpec((rows, HEAD_DIM), lambda b, h, off=off: (b, off + h))
    tok = jax.ShapeDtypeStruct((t, DN_WIDTH), F32)
    return pl.pallas_call(
        body, name="delta_intra", grid=(n // nb, DN_HEADS),
        in_specs=[col(0), col(DN_HEADS), col(2 * DN_HEADS), pl.BlockSpec((rows, LANES), lambda b, h: (b, 0))],
        out_specs=[col(0)] * 4 + [pl.BlockSpec((nb, None, CHUNK, CHUNK), lambda b, h: (b, h, 0, 0)),
                                  pl.BlockSpec((nb, None, HALO, LANES), lambda b, h: (b, h, 0, 0)),
                                  pl.BlockSpec((nb, None, CHUNK, CHUNK), lambda b, h: (b, h, 0, 0))],
        out_shape=[tok] * 4 + [jax.ShapeDtypeStruct((n, DN_HEADS, CHUNK, CHUNK), F32),
                               jax.ShapeDtypeStruct((n, DN_HEADS, HALO, LANES), F32),
                               jax.ShapeDtypeStruct((n, DN_HEADS, CHUNK, CHUNK), F32)],
        compiler_params=_params("parallel", "arbitrary"),
    )(qkvn, qkvn, qkvn, bg)


def _delta_intra_bwd(qkvn, bg, tinv, cts, nb):
    t = qkvn.shape[0]
    n = t // CHUNK
    rows = nb * CHUNK

    def body(q_ref, k_ref, v_ref, bg_ref, ti_ref, du, dw, dqd, dkd, dqk, dgl, dq_ref, dk_ref, dv_ref, dbg_ref):
        h = pl.program_id(1)
        ti = ti_ref[...]
        _, vjp = jax.vjp(lambda q, k, v, b: _delta_intra_math(q, k, v, b, h, ti)[:6],
                         q_ref[...], k_ref[...], v_ref[...], bg_ref[...])
        dq, dk, dv, dbg = vjp((du[...], dw[...], dqd[...], dkd[...], dqk[...], dgl[...]))
        dq_ref[...] = dq
        dk_ref[...] = dk
        dv_ref[...] = dv

        @pl.when(h == 0)
        def _():
            dbg_ref[...] = jnp.zeros_like(dbg_ref)

        dbg_ref[...] += dbg

    col = lambda off: pl.BlockSpec((rows, HEAD_DIM), lambda b, h, off=off: (b, off + h))
    bgs = pl.BlockSpec((rows, LANES), lambda b, h: (b, 0))
    qks = pl.BlockSpec((nb, None, CHUNK, CHUNK), lambda b, h: (b, h, 0, 0))
    gls = pl.BlockSpec((nb, None, HALO, LANES), lambda b, h: (b, h, 0, 0))
    tok = jax.ShapeDtypeStruct((t, DN_WIDTH), F32)
    return pl.pallas_call(
        body, name="delta_intra_bwd", grid=(n // nb, DN_HEADS),
        in_specs=[col(0), col(DN_HEADS), col(2 * DN_HEADS), bgs, qks, col(0), col(0), col(0), col(0), qks, gls],
        out_specs=[col(0), col(0), col(0), bgs],
        out_shape=[tok, tok, tok, jax.ShapeDtypeStruct((t, LANES), F32)],
        compiler_params=_params("parallel", "arbitrary"),
    )(qkvn, qkvn, qkvn, bg, tinv, *cts)


def _delta_scan(u, w, qd, kd, qk, glb, cb):
    t = u.shape[0]
    n = t // CHUNK
    rows = cb * CHUNK

    def body(u_ref, w_ref, qd_ref, kd_ref, qk_ref, gl_ref, o_ref, s_ref, s_scr):
        @pl.when(pl.program_id(0) == 0)
        def _():
            s_scr[...] = jnp.zeros_like(s_scr)

        def chunk(c, carry):
            r0 = pl.multiple_of(c * CHUNK, CHUNK)
            for h in range(DN_HEADS):
                sl = (pl.ds(r0, CHUNK), slice(h * HEAD_DIM, (h + 1) * HEAD_DIM))
                s = s_scr[h]
                s_ref[c, h] = s
                o, s2 = _delta_step_math(u_ref[sl], w_ref[sl], qd_ref[sl], kd_ref[sl], qk_ref[c, h],
                                         gl_ref[c, h][0:1, :], s)
                o_ref[sl] = o
                s_scr[h] = s2
            return carry

        lax.fori_loop(0, cb, chunk, 0)

    tok = pl.BlockSpec((rows, DN_WIDTH), lambda i: (i, 0))
    return pl.pallas_call(
        body, name="delta_scan", grid=(n // cb,),
        in_specs=[tok] * 4 + [pl.BlockSpec((cb, DN_HEADS, CHUNK, CHUNK), lambda i: (i, 0, 0, 0)),
                              pl.BlockSpec((cb, DN_HEADS, HALO, LANES), lambda i: (i, 0, 0, 0))],
        out_specs=[tok, pl.BlockSpec((cb, DN_HEADS, HEAD_DIM, HEAD_DIM), lambda i: (i, 0, 0, 0))],
        out_shape=[jax.ShapeDtypeStruct((t, DN_WIDTH), F32),
                   jax.ShapeDtypeStruct((n, DN_HEADS, HEAD_DIM, HEAD_DIM), F32)],
        scratch_shapes=[pltpu.VMEM((DN_HEADS, HEAD_DIM, HEAD_DIM), F32)],
        compiler_params=_params("arbitrary"),
    )(u, w, qd, kd, qk, glb)


def _delta_scan_bwd(u, w, qd, kd, qk, glb, s_all, do, cb):
    t = u.shape[0]
    n = t // CHUNK
    nblk = n // cb
    rows = cb * CHUNK

    def body(u_ref, w_ref, qd_ref, kd_ref, qk_ref, gl_ref, s_ref, do_ref,
             du_ref, dw_ref, dqd_ref, dkd_ref, dqk_ref, dgl_ref, ds_scr):
        @pl.when(pl.program_id(0) == 0)
        def _():
            ds_scr[...] = jnp.zeros_like(ds_scr)

        def chunk(step, carry):
            c = cb - 1 - step
            r0 = pl.multiple_of(c * CHUNK, CHUNK)
            for h in range(DN_HEADS):
                sl = (pl.ds(r0, CHUNK), slice(h * HEAD_DIM, (h + 1) * HEAD_DIM))
                gl_tile = gl_ref[c, h]
                prim = (u_ref[sl], w_ref[sl], qd_ref[sl], kd_ref[sl], qk_ref[c, h], gl_tile, s_ref[c, h])
                _, vjp = jax.vjp(lambda a, b, cc, d, e, g, s: _delta_step_math(a, b, cc, d, e, g[0:1, :], s), *prim)
                du, dw, dqd, dkd, dqk, dgl, ds = vjp((do_ref[sl], ds_scr[h]))
                du_ref[sl] = du
                dw_ref[sl] = dw
                dqd_ref[sl] = dqd
                dkd_ref[sl] = dkd
                dqk_ref[c, h] = dqk
                dgl_ref[c, h] = dgl
                ds_scr[h] = ds
            return carry

        lax.fori_loop(0, cb, chunk, 0)

    rev = lambda i: nblk - 1 - i
    tok = pl.BlockSpec((rows, DN_WIDTH), lambda i: (rev(i), 0))
    qks = pl.BlockSpec((cb, DN_HEADS, CHUNK, CHUNK), lambda i: (rev(i), 0, 0, 0))
    gls = pl.BlockSpec((cb, DN_HEADS, HALO, LANES), lambda i: (rev(i), 0, 0, 0))
    ss = pl.BlockSpec((cb, DN_HEADS, HEAD_DIM, HEAD_DIM), lambda i: (rev(i), 0, 0, 0))
    tshape = jax.ShapeDtypeStruct((t, DN_WIDTH), F32)
    return pl.pallas_call(
        body, name="delta_scan_bwd", grid=(nblk,),
        in_specs=[tok] * 4 + [qks, gls, ss, tok],
        out_specs=[tok] * 4 + [qks, gls],
        out_shape=[tshape] * 4 + [jax.ShapeDtypeStruct(qk.shape, F32), jax.ShapeDtypeStruct(glb.shape, F32)],
        scratch_shapes=[pltpu.VMEM((DN_HEADS, HEAD_DIM, HEAD_DIM), F32)],
        compiler_params=_params("arbitrary"),
    )(u, w, qd, kd, qk, glb, s_all, do)


def _peer(mask):
    x, y, c = lax.axis_index("x"), lax.axis_index("y"), lax.axis_index("c")
    return (x ^ ((mask >> 2) & 1), y ^ ((mask >> 1) & 1), c ^ (mask & 1))


def _my_index():
    return 4 * lax.axis_index("x") + 2 * lax.axis_index("y") + lax.axis_index("c")


class _Exchange:
    CHIP_MASKS = (4, 2, 6)

    def __init__(self, kind, arrays):
        self.kind = kind
        self.arrays = list(arrays)
        self.na = na = len(self.arrays)
        if kind == "gather":
            self.out_shape = [jax.ShapeDtypeStruct((N_DEV,) + a.shape, a.dtype) for a in self.arrays]
        else:
            self.out_shape = [jax.ShapeDtypeStruct(a.shape, a.dtype) for a in self.arrays]
        self.scratch = [pltpu.SemaphoreType.DMA((na, 7)), pltpu.SemaphoreType.DMA((na, 7)),
                        pltpu.SemaphoreType.DMA((na,))]

    def _copies(self, ins, outs, sems):
        send_sems, recv_sems, local_sems = sems
        me = _my_index()
        local, first, passed, arrivals = [], [], [], []
        if self.kind == "gather":
            def rc(a, k, block, to, own=False):
                def make():
                    dst = outs[a].at[block]
                    return pltpu.make_async_remote_copy(src_ref=ins[a] if own else dst, dst_ref=dst,
                                                        send_sem=send_sems.at[a, k], recv_sem=recv_sems.at[a, k],
                                                        device_id=to, device_id_type=MESH)
                return make

            sib = _peer(1)
            for a in range(self.na):
                local.append(lambda a=a: pltpu.make_async_copy(ins[a], outs[a].at[me], local_sems.at[a]))
                first.append(rc(a, 0, me, sib, own=True))
                arrivals.append(rc(a, 0, me ^ 1, _peer(0)))
                for j, m in enumerate(self.CHIP_MASKS):
                    first.append(rc(a, 1 + j, me, _peer(m), own=True))
                    passed.append((rc(a, 1 + j, me ^ m, _peer(0)), rc(a, 4 + j, me ^ m, sib)))
                    arrivals.append(rc(a, 4 + j, me ^ m ^ 1, _peer(0)))
        else:
            for a in range(self.na):
                local.append(lambda a=a: pltpu.make_async_copy(ins[a].at[me], outs[a].at[me], local_sems.at[a]))
                for m in range(1, N_DEV):
                    def make(a=a, m=m):
                        return pltpu.make_async_remote_copy(
                            src_ref=ins[a].at[me ^ m], dst_ref=outs[a].at[me], send_sem=send_sems.at[a, m - 1],
                            recv_sem=recv_sems.at[a, m - 1], device_id=_peer(m), device_id_type=MESH)
                    first.append(make)
                    arrivals.append(make)
        return local, first, passed, arrivals

    def start(self, ins, outs, sems):
        local, first, _, _ = self._copies(ins, outs, sems)
        for make in local + first:
            make().start()

    def finish(self, ins, outs, sems):
        local, first, passed, arrivals = self._copies(ins, outs, sems)
        for landed, onward in passed:
            landed().wait_recv()
            onward().start()
        for make in arrivals:
            make().wait_recv()
        for make in first + [p for _, p in passed]:
            make().wait_send()
        for make in local:
            make().wait()

    def run(self, name):
        na = self.na

        def body(*refs):
            ins, outs, sems = refs[:na], refs[na:2 * na], refs[2 * na:]
            self.start(ins, outs, sems)
            self.finish(ins, outs, sems)

        hbm = pl.BlockSpec(memory_space=pltpu.HBM)
        return pl.pallas_call(
            body, name=name, in_specs=[hbm] * na, out_specs=[hbm] * na, out_shape=self.out_shape,
            scratch_shapes=self.scratch, compiler_params=pltpu.CompilerParams(has_side_effects=True),
        )(*self.arrays)


def _all_gather(shards, name):
    return _Exchange("gather", shards).run(name)


def _adamw_math(w, g, m, v):
    m2 = ADAM_B1 * m + (1.0 - ADAM_B1) * g
    v2 = ADAM_B2 * v + (1.0 - ADAM_B2) * jnp.square(g)
    m_hat = m2 / (1.0 - ADAM_B1 ** ADAM_STEP)
    v_hat = v2 / (1.0 - ADAM_B2 ** ADAM_STEP)
    delta = -ADAM_LR * (m_hat / (jnp.sqrt(v_hat) + ADAM_EPS) + ADAM_WD * w)
    return delta, m2, v2


def _sum_adamw(parts, w, m, v, name):
    r, c = w.shape
    tr = _pick(r, (512, 256, 352, 128, 64, 32, 16, 8))
    np_ = parts.shape[0]

    def body(p_ref, w_ref, m_ref, v_ref, g_ref, d_ref, m2_ref, v2_ref):
        g = p_ref[0].astype(F32)
        for d in range(1, np_):
            g = g + p_ref[d].astype(F32)
        delta, m2, v2 = _adamw_math(w_ref[...], g, m_ref[...], v_ref[...])
        g_ref[...] = g
        d_ref[...] = delta
        m2_ref[...] = m2
        v2_ref[...] = v2

    blk = pl.BlockSpec((tr, c), lambda i: (i, 0))
    shp = jax.ShapeDtypeStruct((r, c), F32)
    return pl.pallas_call(
        body, name=name, grid=(r // tr,),
        in_specs=[pl.BlockSpec((np_, tr, c), lambda i: (0, i, 0)), blk, blk, blk],
        out_specs=[blk] * 4, out_shape=[shp] * 4, compiler_params=_params("parallel"),
    )(parts, w, m, v)


def _sum_rows(parts, name):
    np_, r, c = parts.shape

    def body(p_ref, o_ref):
        g = p_ref[0]
        for d in range(1, np_):
            g = g + p_ref[d]
        o_ref[...] = g

    return pl.pallas_call(body, name=name, out_shape=jax.ShapeDtypeStruct((r, c), F32))(parts)


def _pad_w_in(w):
    d = w.shape[1]
    n_ba = 2 * DN_HEADS
    a = w[:SC_OFF]
    ba = w[SC_OFF:SC_OFF + n_ba]
    sc = w[SC_OFF + n_ba:]
    return jnp.concatenate([a, sc, ba, jnp.zeros((BA_W - n_ba, d), w.dtype)], axis=0)


def _unpad_w_in(wp):
    n_ba = 2 * DN_HEADS
    return jnp.concatenate([wp[:SC_OFF], wp[BA_OFF:BA_OFF + n_ba], wp[SC_OFF:BA_OFF]], axis=0)


def _lane_row(v, off):
    return jnp.pad(v.astype(F32), (off, LANES - off - v.shape[0]))[None]


TT = 256
NB_INTRA = 16
CB_SCAN = 8


def _layer_fwd(x, p, ride_proj, ride_ffn, late):
    t, d = x.shape
    got_proj = got_ffn = None
    if ride_proj:
        h, proj, got_proj = _norm_matmul(x, p["norm1_g"], p["w_in"], "proj_fwd_gather", ride=ride_proj)
    else:
        h, proj = _norm_matmul(x, p["norm1_g"], p["w_in"], "proj_fwd")
    p = {**p, **late(got_proj)}
    qkvn, bg = _tok_call(
        "dn_pre", lambda tv, cv: (_dn_pre_math(*tv, *cv), ()), t, TT,
        [(proj, QKV_W, 0, "cur"), (proj, QKV_W, 0, "prev"), (proj, LANES, BA_OFF // LANES, "cur")],
        [p["dn_conv_w"], p["alog_row"], p["dt_row"]],
        [(TT, QKV_W, F32), (TT, LANES, F32)], [])
    u, w, qd, kd, qk, glb, tinv = _delta_intra(qkvn, bg, min(NB_INTRA, t // CHUNK))
    o, s_all = _delta_scan(u, w, qd, kd, qk, glb, CB_SCAN)
    cb0 = SC_OFF // SC_WIDTH
    mix_in = [(o, DN_WIDTH, 0, "cur"), (proj, DN_WIDTH, Z_OFF // DN_WIDTH, "cur"),
              (proj, SC_WIDTH, cb0, "cur"), (proj, SC_WIDTH, cb0 + 1, "cur"), (proj, SC_WIDTH, cb0 + 1, "prev"),
              (proj, SC_WIDTH, cb0 + 2, "cur"), (proj, SC_WIDTH, cb0 + 2, "prev")]
    mix_const = [p["dn_norm_g"], p["sc_norm_g"], p["sc_conv_w"]]
    (cat,) = _tok_call("mix_post", lambda tv, cv: ((_mix_math(*tv, *cv),), ()), t, TT,
                       mix_in, mix_const, [(TT, 2 * DN_WIDTH, MXU_DTYPE)], [])
    x_mid = _matmul(cat, p["w_out"], "nn", F32, "out_proj", residual=x)
    if ride_ffn:
        h2, gu, act, got_ffn = _ffn_up_swiglu(x_mid, p["norm2_g"], p["w_gu"], "ffn_up_gather", ride=ride_ffn)
    else:
        h2, gu, act = _ffn_up_swiglu(x_mid, p["norm2_g"], p["w_gu"], "ffn_up")
    x_out = _matmul(act, p["w_down"], "nn", F32, "ffn_down", residual=x_mid)
    saved = dict(x=x, h=h, proj=proj, qkvn=qkvn, bg=bg, u=u, w=w, qd=qd, kd=kd, qk=qk, glb=glb, tinv=tinv, s_all=s_all, o=o,
                 cat=cat, x_mid=x_mid, h2=h2, gu=gu, act=act, mix_in=mix_in, mix_const=mix_const)
    return x_out, saved, p, got_proj, got_ffn


def _layer_bwd(dx_out, p, s, ride_prev, scatter_of):
    t, d = dx_out.shape
    got = {}
    dgu = _ffn_down_dx_swiglu(dx_out, p["w_down"], s["gu"], "ffn_down_dx")
    d_w_down = _matmul(s["act"], dx_out, "tn", MXU_DTYPE, "ffn_down_dw")
    if ride_prev:
        dx_mid, d_norm2, got["prev"] = _matmul_norm_bwd(dgu, p["w_gu"], s["x_mid"], p["norm2_g"], dx_out,
                                                        "ffn_up_dx_scatter", ride=ride_prev, w_rows_k=True)
    else:
        dx_mid, d_norm2 = _matmul_norm_bwd(dgu, p["w_gu"], s["x_mid"], p["norm2_g"], dx_out, "ffn_up_dx",
                                           w_rows_k=True)
    d_w_gu, got["down"] = _matmul(dgu, s["h2"], "tn", MXU_DTYPE, "ffn_up_dw_scatter", ride=scatter_of(d_w_down))
    d_gate, d_up = _split_gu(d_w_gu)
    dcat = _matmul(dx_mid, p["w_out"], "nt", F32, "out_proj_dx")
    d_w_out = _matmul(s["cat"], dx_mid, "tn", MXU_DTYPE, "out_proj_dw")

    def mix_bwd(tv, cv):
        prim = tuple(tv[:7]) + tuple(cv)
        _, vjp = jax.vjp(_mix_math, *prim)
        do, dz, dgb, dgc, dgch, dhv, dhvh, ddng, dscg, dscw = vjp(tv[7])
        return (do, dz, dgb, dgc, dgch, dhv, dhvh), (ddng, dscg, dscw)

    wide = (TT, DN_WIDTH, F32)
    final = (TT, DN_WIDTH, MXU_DTYPE)
    halo = (HALO, SC_WIDTH, F32)
    do, dz, dgb, dgc, dgc_h, dhv, dhv_h, d_dn_norm, d_sc_norm, d_sc_conv = _tok_call(
        "mix_post_bwd", mix_bwd, t, TT, s["mix_in"] + [(dcat, 2 * DN_WIDTH, 0, "cur")], s["mix_const"],
        [wide, final, final, wide, halo, wide, halo],
        [((1, HEAD_DIM), F32), ((1, SC_WIDTH), F32), ((SC_CONV, SC_WIDTH), F32)])
    cts = _delta_scan_bwd(s["u"], s["w"], s["qd"], s["kd"], s["qk"], s["glb"], s["s_all"], do, CB_SCAN)
    dq, dk, dv, dbg = _delta_intra_bwd(s["qkvn"], s["bg"], s["tinv"], cts, min(NB_INTRA, t // CHUNK))
    proj = s["proj"]

    def with_halo(cur, nxt):
        return cur + jnp.concatenate([jnp.zeros((TT - HALO, cur.shape[1]), F32), nxt], axis=0)

    def dn_pre_bwd(tv, cv, carried):
        cur, hal, ba, dq_, dk_, dv_, dbg_, dz_, dgb_, dgc_, dgc_n, dhv_, dhv_n = tv
        _, vjp = jax.vjp(_dn_pre_math, cur, hal, ba, *cv)
        dcur, dhal, dba, dcw, dal, ddt = vjp((jnp.concatenate([dq_, dk_, dv_], axis=1), dbg_))
        dproj_rows = jnp.concatenate([with_halo(dcur, carried[0]), dz_.astype(F32), dgb_.astype(F32), with_halo(dgc_, dgc_n),
                                      with_halo(dhv_, dhv_n), dba, jnp.zeros((TT, BA_W - LANES), F32)], axis=1)
        return (dproj_rows,), (dcw, dal, ddt), (dhal,)

    dproj, d_dn_conv, d_alog, d_dt = _tok_call(
        "dn_pre_bwd", dn_pre_bwd, t, TT,
        [(proj, QKV_W, 0, "cur"), (proj, QKV_W, 0, "prev"), (proj, LANES, BA_OFF // LANES, "cur"),
         (dq, DN_WIDTH, 0, "cur"), (dk, DN_WIDTH, 0, "cur"), (dv, DN_WIDTH, 0, "cur"), (dbg, LANES, 0, "cur"),
         (dz, DN_WIDTH, 0, "cur"), (dgb, SC_WIDTH, 0, "cur"), (dgc, SC_WIDTH, 0, "cur"), (dgc_h, SC_WIDTH, 0, "next8"),
         (dhv, SC_WIDTH, 0, "cur"), (dhv_h, SC_WIDTH, 0, "next8")],
        [p["dn_conv_w"], p["alog_row"], p["dt_row"]],
        [(TT, PROJ_W, MXU_DTYPE)],
        [((DN_CONV, QKV_W), F32), ((1, LANES), F32), ((1, LANES), F32)],
        carry=[((HALO, QKV_W), F32)])
    dx_in, d_norm1, got["gate"] = _matmul_norm_bwd(dproj, p["w_in"], s["x"], p["norm1_g"], dx_mid,
                                                   "proj_dx_scatter", ride=scatter_of(d_gate), w_rows_k=True)
    d_w_in, got["up"] = _matmul(dproj, s["h"], "tn", MXU_DTYPE, "proj_dw_scatter", ride=scatter_of(d_up))
    grads = dict(w_in=d_w_in, w_out=d_w_out, w_gu=d_w_gu, w_down=d_w_down, norm1_g=d_norm1, norm2_g=d_norm2,
                 dn_norm_g=d_dn_norm, sc_norm_g=d_sc_norm, sc_conv_w=d_sc_conv, dn_conv_w=d_dn_conv,
                 alog=d_alog, dt=d_dt)
    return dx_in, grads, got


def _final_loss(x, g, target):
    t, d = x.shape

    def fn(tv, cv):
        xv, tg = tv

        def loss_fn(xx, gg):
            err = jnp.square(_rms_norm(xx, gg) - tg)
            return 0.5 * jnp.sum(jnp.mean(err, axis=-1))

        loss, vjp = jax.vjp(loss_fn, xv, cv[0])
        dx, dg = vjp(jnp.ones((), F32))
        return (dx,), (jnp.full((1, LANES), loss, F32), dg)

    return _tok_call("final_loss", fn, t, TT, [(x, d, 0, "cur"), (target, d, 0, "cur")], [g],
                     [(TT, d, F32)], [((1, LANES), F32), ((1, d), F32)])


def _pack_rows(arrs):
    rows, offs, r0 = [], [], 0
    for a in arrs:
        n = a.size
        nr = -(-n // LANES)
        flat = jnp.pad(a.reshape(-1).astype(F32), (0, nr * LANES - n))
        rows.append(flat.reshape(nr, LANES))
        offs.append((r0, nr, a.shape))
        r0 += nr
    pad = (-r0) % 8
    if pad:
        rows.append(jnp.zeros((pad, LANES), F32))
    return jnp.concatenate(rows, axis=0), offs


def _unpack_rows(packed, offs):
    out = []
    for r0, nr, shp in offs:
        n = 1
        for s_ in shp:
            n *= s_
        out.append(packed[r0:r0 + nr].reshape(-1)[:n].reshape(shp))
    return out


def kernel(x, norm1_g, w_in, dn_conv_w, dn_a_log, dn_dt_bias, dn_norm_g, sc_conv_w, sc_norm_g, w_out, norm2_g, ffn_w_gate, ffn_w_up, ffn_w_down, final_norm_g, loss_target, m_norm1_g, m_w_in, m_dn_conv_w, m_dn_a_log, m_dn_dt_bias, m_dn_norm_g, m_sc_conv_w, m_sc_norm_g, m_w_out, m_norm2_g, m_ffn_w_gate, m_ffn_w_up, m_ffn_w_down, m_final_norm_g, v_norm1_g, v_w_in, v_dn_conv_w, v_dn_a_log, v_dn_dt_bias, v_dn_norm_g, v_sc_conv_w, v_sc_norm_g, v_w_out, v_norm2_g, v_ffn_w_gate, v_ffn_w_up, v_ffn_w_down, v_final_norm_g):
    depth, d, cin = w_in.shape
    t = x.shape[1]
    dff_s = ffn_w_gate.shape[2]
    tr = lambda a: a.transpose(0, 2, 1)
    gate_t, up_t = tr(ffn_w_gate), tr(ffn_w_up)
    by_layer = lambda a: [a[l].T for l in range(depth)]
    win_t = by_layer(w_in)
    me = _my_index()
    x2 = x.reshape(t, d)
    tgt = loss_target.reshape(t, d)

    def shards(l):
        return [a[l].astype(MXU_DTYPE) for a in (win_t, w_out, gate_t, up_t, ffn_w_down)]

    conv_pack, conv_offs = _pack_rows([dn_conv_w, sc_conv_w])
    conv_all, *first = _all_gather([conv_pack] + shards(0)[:2], "gather_first")
    dn_parts, sc_parts = zip(*[_unpack_rows(conv_all[j], conv_offs) for j in range(N_DEV)])
    dn_conv_full = jnp.concatenate(dn_parts, axis=2)
    sc_conv_full = jnp.concatenate(sc_parts, axis=2)

    def mixer_params(l, g_in, g_out):
        return dict(
            w_in=_pad_w_in(g_in.reshape(N_DEV * cin, d)), w_out=g_out.reshape(d, d),
            norm1_g=norm1_g[l][None], norm2_g=norm2_g[l][None], dn_norm_g=dn_norm_g[l][None],
            sc_norm_g=sc_norm_g[l][None], dn_conv_w=dn_conv_full[l], sc_conv_w=sc_conv_full[l],
            alog_row=_lane_row(dn_a_log[l], DN_HEADS), dt_row=_lane_row(dn_dt_bias[l], DN_HEADS))

    def ffn_params(g_gate, g_up, g_down):
        dff = N_DEV * dff_s
        return dict(w_gu=_interleave_gu(g_gate.reshape(dff, d), g_up.reshape(dff, d)), w_down=g_down.reshape(dff, d))

    nxt = mixer_params(0, *first)
    nxt_ffn = None
    params, saved = [], []
    xc = x2
    for l in range(depth):
        last = l + 1 == depth
        own_ffn = shards(l)[2:] if nxt_ffn is None else []
        ahead = [] if last else shards(l + 1)
        ride_proj = _Exchange("gather", own_ffn + ahead[:2]) if own_ffn or ahead else None
        ride_ffn = None if last else _Exchange("gather", ahead[2:])

        def late(got, own_ffn=own_ffn, nxt_ffn=nxt_ffn):
            return ffn_params(*got[:3]) if own_ffn else nxt_ffn

        xc, s, p_l, got_proj, got_ffn = _layer_fwd(xc, nxt, ride_proj, ride_ffn, late)
        params.append(p_l)
        saved.append(s)
        if not last:
            nxt = mixer_params(l + 1, *got_proj[len(own_ffn):])
            nxt_ffn = ffn_params(*got_ffn)
    dx, loss_part, d_final = _final_loss(xc, final_norm_g[None], tgt)

    names = ("w_in", "w_out", "ffn_w_gate", "ffn_w_up", "ffn_w_down")
    big_out = {n: {k: [None] * depth for k in ("g", "d", "m", "v")} for n in names}
    w_loc = dict(w_in=win_t, w_out=w_out, ffn_w_gate=gate_t, ffn_w_up=up_t, ffn_w_down=ffn_w_down)
    m_loc = dict(w_in=by_layer(m_w_in), w_out=m_w_out, ffn_w_gate=tr(m_ffn_w_gate), ffn_w_up=tr(m_ffn_w_up), ffn_w_down=m_ffn_w_down)
    v_loc = dict(w_in=by_layer(v_w_in), w_out=v_w_out, ffn_w_gate=tr(v_ffn_w_gate), ffn_w_up=tr(v_ffn_w_up), ffn_w_down=v_ffn_w_down)


    def scatter_of(grad):
        return _Exchange("scatter", [grad.reshape(N_DEV, dff_s, d).astype(MXU_DTYPE)])

    def apply(l, which, recv):
        for n, r in zip(which, recv):
            res = _sum_adamw(r, w_loc[n][l], m_loc[n][l], v_loc[n][l], "adamw_" + n)
            for k, a in zip(("g", "d", "m", "v"), res):
                big_out[n][k][l] = a

    grads = [None] * depth
    pending = None
    for l in reversed(range(depth)):
        ride_prev = _Exchange("scatter", pending) if pending else None
        dx, grads[l], got = _layer_bwd(dx, params[l], saved[l], ride_prev, scatter_of)
        if ride_prev:
            apply(l + 1, names[:2], got["prev"])
        apply(l, names[2:], [*got["gate"], *got["up"], *got["down"]])
        pending = [_unpad_w_in(grads[l]["w_in"]).reshape(N_DEV, cin, d).astype(MXU_DTYPE),
                   grads[l]["w_out"].reshape(N_DEV, d // N_DEV, d).astype(MXU_DTYPE)]
    apply(0, names[:2], _Exchange("scatter", pending).run("scatter_last"))
    grad_x = dx.reshape(x.shape)
    big_out = {n: {k: jnp.stack(v_, axis=1 if n == "w_in" else 0) for k, v_ in o.items()} for n, o in big_out.items()}
    for n in ("ffn_w_gate", "ffn_w_up"):
        big_out[n] = {k: tr(a) for k, a in big_out[n].items()}
    big_out["w_in"] = {k: a.transpose(1, 2, 0) for k, a in big_out["w_in"].items()}

    stack = lambda key: jnp.stack([grads[l][key] for l in range(depth)])
    small_parts = [stack("norm1_g").reshape(depth, d), stack("norm2_g").reshape(depth, d), d_final.reshape(d),
                   stack("dn_norm_g").reshape(depth, HEAD_DIM), stack("sc_norm_g").reshape(depth, SC_WIDTH),
                   stack("alog").reshape(depth, LANES), stack("dt").reshape(depth, LANES),
                   stack("dn_conv_w"), stack("sc_conv_w"), loss_part]
    small_pack, small_offs = _pack_rows(small_parts)
    (small_all,) = _all_gather([small_pack], "gather_small")
    total = _sum_rows(small_all, "sum_small")
    (g_n1, g_n2, g_fin, g_dnn, g_scn, g_alog, g_dt, g_dnc, g_scc, loss_row) = _unpack_rows(total, small_offs)
    loss = loss_row[0, 0]
    g_alog = g_alog[:, DN_HEADS:2 * DN_HEADS]
    g_dt = g_dt[:, DN_HEADS:2 * DN_HEADS]
    dnc_w = dn_conv_w.shape[2]
    scc_w = sc_conv_w.shape[2]
    g_dnc = lax.dynamic_slice_in_dim(g_dnc, me * dnc_w, dnc_w, axis=2)
    g_scc = lax.dynamic_slice_in_dim(g_scc, me * scc_w, scc_w, axis=2)
    sm_g = [g_n1, g_dnc, g_alog, g_dt, g_dnn, g_scc, g_scn, g_n2, g_fin]
    sm_w = [norm1_g, dn_conv_w, dn_a_log, dn_dt_bias, dn_norm_g, sc_conv_w, sc_norm_g, norm2_g, final_norm_g]
    sm_m = [m_norm1_g, m_dn_conv_w, m_dn_a_log, m_dn_dt_bias, m_dn_norm_g, m_sc_conv_w, m_sc_norm_g, m_norm2_g, m_final_norm_g]
    sm_v = [v_norm1_g, v_dn_conv_w, v_dn_a_log, v_dn_dt_bias, v_dn_norm_g, v_sc_conv_w, v_sc_norm_g, v_norm2_g, v_final_norm_g]
    pg, offs = _pack_rows(sm_g)
    pw, _ = _pack_rows(sm_w)
    pm, _ = _pack_rows(sm_m)
    pv, _ = _pack_rows(sm_v)
    sg, sd, sm_, sv = _sum_adamw(pg[None], pw, pm, pv, "adamw_small")
    small_out = {k: _unpack_rows(a, offs) for k, a in zip(("g", "d", "m", "v"), (sg, sd, sm_, sv))}

    def outputs(k):
        s_ = small_out[k]
        b = big_out
        return [s_[0], b["w_in"][k], s_[1], s_[2], s_[3], s_[4], s_[5], s_[6], b["w_out"][k], s_[7],
                b["ffn_w_gate"][k], b["ffn_w_up"][k], b["ffn_w_down"][k], s_[8]]

    return (loss, grad_x, *outputs("g"), *outputs("d"), *outputs("m"), *outputs("v"))
```

```python
import functools

import jax
import jax.numpy as jnp
from jax import lax
from jax.experimental import pallas as pl
from jax.experimental.pallas import tpu as pltpu

F32 = jnp.float32
MXU_DTYPE = jnp.bfloat16
MESH = pl.DeviceIdType.MESH

N_DEV = 8
EPS = 1e-6
DN_HEADS = 4
HEAD_DIM = 128
DN_WIDTH = DN_HEADS * HEAD_DIM
SC_WIDTH = 512
SC_GROUPS = 4
DN_CONV = 4
SC_CONV = 3
CHUNK = 64
HALO = 8
LANES = 128

QKV_W = 3 * DN_WIDTH
Z_OFF = QKV_W
SC_OFF = Z_OFF + DN_WIDTH
BA_OFF = SC_OFF + 3 * SC_WIDTH
BA_W = 256
PROJ_W = BA_OFF + BA_W

ADAM_LR = 0.001
ADAM_B1 = 0.9
ADAM_B2 = 0.999
ADAM_EPS = 1e-08
ADAM_WD = 0.01
ADAM_STEP = 10


def _pick(n, cands):
    for c in cands:
        if n % c == 0:
            return c
    return n


def _params(*sem):
    return pltpu.CompilerParams(dimension_semantics=sem)


def _rms_norm(x, g):
    return x * lax.rsqrt(jnp.mean(x * x, axis=-1, keepdims=True) + EPS) * g


def _dot(a, b, dims=(((1,), (0,)), ((), ()))):
    return lax.dot_general(a.astype(MXU_DTYPE), b.astype(MXU_DTYPE), dims, preferred_element_type=F32)


def _split_terms(x, terms):
    out = []
    for _ in range(terms):
        hi = x.astype(MXU_DTYPE)
        out.append(hi)
        x = x - hi.astype(F32)
    return out


def _ein_impl(spec, terms, a, b):
    ta, tb = terms
    if ta == 1 and tb == 1:
        return jnp.einsum(spec, a.astype(MXU_DTYPE), b.astype(MXU_DTYPE), preferred_element_type=F32)
    pa, pb = _split_terms(a, ta), _split_terms(b, tb)
    order = max(ta, tb) - 1
    acc = None
    for deg in range(order, -1, -1):
        for i in range(ta):
            j = deg - i
            if 0 <= j < tb:
                t = jnp.einsum(spec, pa[i], pb[j], preferred_element_type=F32)
                acc = t if acc is None else acc + t
    return acc


@functools.partial(jax.custom_vjp, nondiff_argnums=(0, 1))
def _ein(spec, terms, a, b):
    return _ein_impl(spec, terms, a, b)


def _ein_fwd(spec, terms, a, b):
    return _ein_impl(spec, terms, a, b), (a, b)


def _ein_bwd(spec, terms, res, ct):
    a, b = res
    xy, z = spec.split("->")
    x, y = xy.split(",")
    tc = min(max(terms), 2)
    da = _ein_impl(f"{z},{y}->{x}", (tc, terms[1]), ct, b)
    db = _ein_impl(f"{x},{z}->{y}", (terms[0], tc), a, ct)
    return da, db


_ein.defvjp(_ein_fwd, _ein_bwd)

FAST = (1, 1)
PRECISE = (2, 2)
LHS_EXACT = (1, 3)


def _causal_conv(cur, halo, w, k):
    tt = cur.shape[0]
    xp = jnp.concatenate([halo, cur], axis=0)
    y = None
    for j in range(k):
        start = HALO - (k - 1) + j
        term = xp[start:start + tt] * w[j:j + 1]
        y = term if y is None else y + term
    return y


def _matmul(a, b, mode, out_dtype, name, residual=None, ride=None):
    if mode == "nn":
        (m, k), (k2, n) = a.shape, b.shape
    elif mode == "nt":
        (m, k), (n, k2) = a.shape, b.shape
    else:
        (k, m), (k2, n) = a.shape, b.shape
    assert k == k2
    tm = _pick(m, (1024, 1408, 1280, 512, 256, 128))
    tn = _pick(n, (1280, 1408, 1024, 512, 256, 128))
    tk = k if k <= 2816 else _pick(k, (1024, 768, 512))
    gi, gj, nk = m // tm, n // tn, k // tk
    dims = {"nn": (((1,), (0,)), ((), ())), "nt": (((1,), (1,)), ((), ())), "tn": (((0,), (0,)), ((), ()))}[mode]
    a_spec = {"nn": pl.BlockSpec((tm, tk), lambda i, j, q: (i, q)),
              "nt": pl.BlockSpec((tm, tk), lambda i, j, q: (i, q)),
              "tn": pl.BlockSpec((tk, tm), lambda i, j, q: (q, i))}[mode]
    b_spec = {"nn": pl.BlockSpec((tk, tn), lambda i, j, q: (q, j)),
              "nt": pl.BlockSpec((tn, tk), lambda i, j, q: (j, q)),
              "tn": pl.BlockSpec((tk, tn), lambda i, j, q: (q, j))}[mode]
    o_spec = pl.BlockSpec((tm, tn), lambda i, j, q: (i, j))
    has_res = residual is not None
    n_in = 3 if has_res else 2
    nr = ride.na if ride else 0

    def body(*refs):
        a_ref, b_ref = refs[0], refs[1]
        r_ref = refs[2] if has_res else None
        ride_in = refs[n_in:n_in + nr]
        o_ref = refs[n_in + nr]
        ride_out = refs[n_in + nr + 1:n_in + 2 * nr + 1]
        acc = refs[n_in + 2 * nr + 1]
        ride_sems = refs[n_in + 2 * nr + 2:]
        i, j, q = pl.program_id(0), pl.program_id(1), pl.program_id(2)
        if ride:
            @pl.when((i == 0) & (j == 0) & (q == 0))
            def _():
                ride.start(ride_in, ride_out, ride_sems)

        @pl.when(q == 0)
        def _():
            acc[...] = jnp.zeros_like(acc)

        acc[...] += _dot(a_ref[...], b_ref[...], dims)

        @pl.when(q == nk - 1)
        def _():
            r = acc[...]
            if has_res:
                r = r + r_ref[...]
            o_ref[...] = r.astype(o_ref.dtype)

        if ride:
            @pl.when((i == gi - 1) & (j == gj - 1) & (q == nk - 1))
            def _():
                ride.finish(ride_in, ride_out, ride_sems)

    hbm = pl.BlockSpec(memory_space=pltpu.HBM)
    in_specs = [a_spec, b_spec] + ([o_spec] if has_res else []) + [hbm] * nr
    args = (a, b) + ((residual,) if has_res else ()) + (tuple(ride.arrays) if ride else ())
    res = pl.pallas_call(
        body, name=name, grid=(gi, gj, nk), in_specs=in_specs, out_specs=[o_spec] + [hbm] * nr,
        out_shape=[jax.ShapeDtypeStruct((m, n), out_dtype)] + (ride.out_shape if ride else []),
        scratch_shapes=[pltpu.VMEM((tm, tn), F32)] + (ride.scratch if ride else []),
        compiler_params=pltpu.CompilerParams(
            dimension_semantics=("arbitrary",) * 3 if ride else ("parallel", "parallel", "arbitrary"),
            has_side_effects=bool(ride)),
    )(*args)
    return (res[0], res[1:]) if ride else res[0]


def _norm_matmul(x, g, w, name, ride=None):
    t, d = x.shape
    n = w.shape[0]
    tm = _pick(t, (1024, 512, 256, 128))
    tn = _pick(n, (1280, 1408, 1024, 512, 256, 128))
    gi, gj = t // tm, n // tn
    nr = ride.na if ride else 0

    def body(*refs):
        x_ref, g_ref, w_ref = refs[:3]
        ride_in = refs[3:3 + nr]
        h_ref, y_ref = refs[3 + nr], refs[4 + nr]
        ride_out = refs[5 + nr:5 + 2 * nr]
        h_scr = refs[5 + 2 * nr]
        ride_sems = refs[6 + 2 * nr:]
        i, j = pl.program_id(0), pl.program_id(1)
        if ride:
            @pl.when((i == 0) & (j == 0))
            def _():
                ride.start(ride_in, ride_out, ride_sems)

        @pl.when(j == 0)
        def _():
            h = _rms_norm(x_ref[...], g_ref[...]).astype(MXU_DTYPE)
            h_scr[...] = h
            h_ref[...] = h

        y_ref[...] = _dot(h_scr[...], w_ref[...], (((1,), (1,)), ((), ())))

        if ride:
            @pl.when((i == gi - 1) & (j == gj - 1))
            def _():
                ride.finish(ride_in, ride_out, ride_sems)

    hbm = pl.BlockSpec(memory_space=pltpu.HBM)
    res = pl.pallas_call(
        body, name=name, grid=(gi, gj),
        in_specs=[pl.BlockSpec((tm, d), lambda i, j: (i, 0)), pl.BlockSpec((1, d), lambda i, j: (0, 0)),
                  pl.BlockSpec((tn, d), lambda i, j: (j, 0))] + [hbm] * nr,
        out_specs=[pl.BlockSpec((tm, d), lambda i, j: (i, 0)), pl.BlockSpec((tm, tn), lambda i, j: (i, j))] + [hbm] * nr,
        out_shape=[jax.ShapeDtypeStruct((t, d), MXU_DTYPE), jax.ShapeDtypeStruct((t, n), F32)]
        + (ride.out_shape if ride else []),
        scratch_shapes=[pltpu.VMEM((tm, d), MXU_DTYPE)] + (ride.scratch if ride else []),
        compiler_params=pltpu.CompilerParams(
            dimension_semantics=("arbitrary",) * 2 if ride else ("parallel", "arbitrary"),
            has_side_effects=bool(ride)),
    )(x, g, w, *(ride.arrays if ride else ()))
    return (res[0], res[1], res[2:]) if ride else (res[0], res[1])


def _swiglu_math(g, u):
    return jax.nn.silu(g) * u


def _gu_tile(dff):
    return _pick(dff, (1408, 1024, 512, 256, 128))


def _interleave_gu(gate_t, up_t):
    tn = _gu_tile(gate_t.shape[0])
    pieces = []
    for j in range(gate_t.shape[0] // tn):
        pieces += [gate_t[j * tn:(j + 1) * tn], up_t[j * tn:(j + 1) * tn]]
    return jnp.concatenate(pieces, axis=0)


def _split_gu(gu_t):
    dff = gu_t.shape[0] // 2
    tn = _gu_tile(dff)
    tiles = [gu_t[j * tn:(j + 1) * tn] for j in range(2 * dff // tn)]
    return jnp.concatenate(tiles[0::2], axis=0), jnp.concatenate(tiles[1::2], axis=0)


def _ffn_up_swiglu(x, g, w_gu, name, ride=None):
    t, d = x.shape
    dff = w_gu.shape[0] // 2
    tn = _gu_tile(dff)
    tm = _pick(t, (512, 256, 128))
    gj, gi = dff // tn, t // tm
    nr = ride.na if ride else 0

    def body(*refs):
        x_ref, g_ref, w_ref = refs[:3]
        ride_in = refs[3:3 + nr]
        h_ref, gu_ref, act_ref = refs[3 + nr:6 + nr]
        ride_out = refs[6 + nr:6 + 2 * nr]
        ride_sems = refs[6 + 2 * nr:]
        j, i = pl.program_id(0), pl.program_id(1)
        if ride:
            @pl.when((i == 0) & (j == 0))
            def _():
                ride.start(ride_in, ride_out, ride_sems)

        h = _rms_norm(x_ref[...], g_ref[...]).astype(MXU_DTYPE)

        @pl.when(j == 0)
        def _():
            h_ref[...] = h

        y = _dot(h, w_ref[...], (((1,), (1,)), ((), ())))
        gu_ref[...] = y.astype(gu_ref.dtype)
        act_ref[...] = _swiglu_math(y[:, :tn], y[:, tn:]).astype(act_ref.dtype)

        if ride:
            @pl.when((i == gi - 1) & (j == gj - 1))
            def _():
                ride.finish(ride_in, ride_out, ride_sems)

    hbm = pl.BlockSpec(memory_space=pltpu.HBM)
    res = pl.pallas_call(
        body, name=name, grid=(gj, gi),
        in_specs=[pl.BlockSpec((tm, d), lambda j, i: (i, 0)), pl.BlockSpec((1, d), lambda j, i: (0, 0)),
                  pl.BlockSpec((2 * tn, d), lambda j, i: (j, 0))] + [hbm] * nr,
        out_specs=[pl.BlockSpec((tm, d), lambda j, i: (jnp.where(j == 0, i, gi - 1), 0)),
                   pl.BlockSpec((tm, 2 * tn), lambda j, i: (i, j)),
                   pl.BlockSpec((tm, tn), lambda j, i: (i, j))] + [hbm] * nr,
        out_shape=[jax.ShapeDtypeStruct((t, d), MXU_DTYPE), jax.ShapeDtypeStruct((t, 2 * dff), MXU_DTYPE),
                   jax.ShapeDtypeStruct((t, dff), MXU_DTYPE)] + (ride.out_shape if ride else []),
        scratch_shapes=(ride.scratch if ride else []),
        compiler_params=pltpu.CompilerParams(dimension_semantics=("arbitrary", "arbitrary"),
                                             has_side_effects=bool(ride)),
    )(x, g, w_gu, *(ride.arrays if ride else ()))
    return (res[0], res[1], res[2], res[3:]) if ride else tuple(res)


def _ffn_down_dx_swiglu(dx_out, w_down, gu, name):
    t, d = dx_out.shape
    dff = w_down.shape[0]
    tn = _gu_tile(dff)
    tm = _pick(t, (512, 256, 128))

    def body(dx_ref, w_ref, gu_ref, o_ref):
        dact = _dot(dx_ref[...], w_ref[...], (((1,), (1,)), ((), ())))
        gu_v = gu_ref[...].astype(F32)
        _, vjp = jax.vjp(_swiglu_math, gu_v[:, :tn], gu_v[:, tn:])
        dg, du = vjp(dact)
        o_ref[...] = jnp.concatenate([dg, du], axis=1).astype(o_ref.dtype)

    return pl.pallas_call(
        body, name=name, grid=(dff // tn, t // tm),
        in_specs=[pl.BlockSpec((tm, d), lambda j, i: (i, 0)), pl.BlockSpec((tn, d), lambda j, i: (j, 0)),
                  pl.BlockSpec((tm, 2 * tn), lambda j, i: (i, j))],
        out_specs=pl.BlockSpec((tm, 2 * tn), lambda j, i: (i, j)),
        out_shape=jax.ShapeDtypeStruct((t, 2 * dff), MXU_DTYPE),
        compiler_params=_params("parallel", "parallel"),
    )(dx_out, w_down, gu)


def _matmul_norm_bwd(dy, w, x, g, dres, name, ride=None, w_rows_k=False):
    t, k = dy.shape
    d = w.shape[1] if w_rows_k else w.shape[0]
    tm = _pick(t, (1024, 512, 256, 128))
    tk = k if k <= 2816 else _pick(k, (1408, 1280, 1024, 768, 512))
    tr = _pick(tm, (256, 128))
    gi, nk = t // tm, k // tk
    nr = ride.na if ride else 0

    def body(*refs):
        dy_ref, w_ref, x_ref, g_ref, dres_ref = refs[:5]
        ride_in = refs[5:5 + nr]
        dx_ref, dg_ref = refs[5 + nr], refs[6 + nr]
        ride_out = refs[7 + nr:7 + 2 * nr]
        acc = refs[7 + 2 * nr]
        ride_sems = refs[8 + 2 * nr:]
        i, q = pl.program_id(0), pl.program_id(1)

        @pl.when((i == 0) & (q == 0))
        def _():
            dg_ref[...] = jnp.zeros_like(dg_ref)
            if ride:
                ride.start(ride_in, ride_out, ride_sems)

        @pl.when(q == 0)
        def _():
            acc[...] = jnp.zeros_like(acc)

        acc[...] += _dot(dy_ref[...], w_ref[...], (((1,), (0 if w_rows_k else 1,)), ((), ())))

        @pl.when(q == nk - 1)
        def _():
            for r in range(tm // tr):
                rows = slice(r * tr, (r + 1) * tr)
                _, vjp = jax.vjp(_rms_norm, x_ref[rows], g_ref[...])
                dxn, dg = vjp(acc[rows])
                dx_ref[rows] = dres_ref[rows] + dxn
                dg_ref[...] += dg

        if ride:
            @pl.when((i == gi - 1) & (q == nk - 1))
            def _():
                ride.finish(ride_in, ride_out, ride_sems)

    hbm = pl.BlockSpec(memory_space=pltpu.HBM)
    row = pl.BlockSpec((tm, d), lambda i, q: (i, 0))
    res = pl.pallas_call(
        body, name=name, grid=(gi, nk),
        in_specs=[pl.BlockSpec((tm, tk), lambda i, q: (i, q)),
                  pl.BlockSpec((tk, d), lambda i, q: (q, 0)) if w_rows_k else pl.BlockSpec((d, tk), lambda i, q: (0, q)),
                  row,
                  pl.BlockSpec((1, d), lambda i, q: (0, 0)), row] + [hbm] * nr,
        out_specs=[row, pl.BlockSpec((1, d), lambda i, q: (0, 0))] + [hbm] * nr,
        out_shape=[jax.ShapeDtypeStruct((t, d), F32), jax.ShapeDtypeStruct((1, d), F32)]
        + (ride.out_shape if ride else []),
        scratch_shapes=[pltpu.VMEM((tm, d), F32)] + (ride.scratch if ride else []),
        compiler_params=pltpu.CompilerParams(dimension_semantics=("arbitrary", "arbitrary"),
                                             has_side_effects=bool(ride)),
    )(dy, w, x, g, dres, *(ride.arrays if ride else ()))
    return (res[0], res[1], res[2:]) if ride else (res[0], res[1])


def _tok_call(name, fn, t, tt, tok_in, const_in, tok_out, acc_out, carry=()):
    nblk = t // tt
    hb = tt // HALO
    blk = (lambda i: nblk - 1 - i) if carry else (lambda i: i)
    in_specs, args = [], []
    for arr, w, cb, mode in tok_in:
        if mode == "cur":
            spec = pl.BlockSpec((tt, w), lambda i, cb=cb: (blk(i), cb))
        elif mode == "prev":
            spec = pl.BlockSpec((HALO, w), lambda i, cb=cb: (jnp.maximum(blk(i) * hb - 1, 0), cb))
        else:
            spec = pl.BlockSpec((HALO, w), lambda i, cb=cb: (jnp.minimum(blk(i) + 1, nblk - 1), cb))
        in_specs.append(spec)
        args.append(arr)
    for arr in const_in:
        in_specs.append(pl.BlockSpec(arr.shape, lambda i: (0, 0)))
        args.append(arr)
    out_specs, out_shape = [], []
    for rows, w, dt in tok_out:
        out_specs.append(pl.BlockSpec((rows, w), lambda i: (blk(i), 0)))
        out_shape.append(jax.ShapeDtypeStruct((nblk * rows, w), dt))
    for shp, dt in acc_out:
        out_specs.append(pl.BlockSpec(shp, lambda i: (0, 0)))
        out_shape.append(jax.ShapeDtypeStruct(shp, dt))
    n_tok, n_const, n_out, n_acc = len(tok_in), len(const_in), len(tok_out), len(acc_out)

    def body(*refs):
        i = pl.program_id(0)
        b = blk(i)
        tok_vals = []
        for (_, _, _, mode), r in zip(tok_in, refs[:n_tok]):
            v = r[...]
            if mode == "prev":
                v = jnp.where(b > 0, v, jnp.zeros_like(v))
            elif mode == "next8":
                v = jnp.where(b < nblk - 1, v, jnp.zeros_like(v))
            tok_vals.append(v)
        const_vals = [r[...] for r in refs[n_tok:n_tok + n_const]]
        o_refs = refs[n_tok + n_const:n_tok + n_const + n_out]
        a_refs = refs[n_tok + n_const + n_out:n_tok + n_const + n_out + n_acc]
        c_refs = refs[n_tok + n_const + n_out + n_acc:]
        if a_refs or c_refs:
            @pl.when(i == 0)
            def _():
                for r in (*a_refs, *c_refs):
                    r[...] = jnp.zeros_like(r)

        if c_refs:
            outs, accs, carried = fn(tok_vals, const_vals, [r[...] for r in c_refs])
            for r, v in zip(c_refs, carried):
                r[...] = v
        else:
            outs, accs = fn(tok_vals, const_vals)
        for r, v in zip(o_refs, outs):
            r[...] = v.astype(r.dtype)
        for r, v in zip(a_refs, accs):
            r[...] += v.astype(r.dtype)

    res = pl.pallas_call(
        body, name=name, grid=(nblk,), in_specs=in_specs, out_specs=out_specs, out_shape=out_shape,
        scratch_shapes=[pltpu.VMEM(shp, dt) for shp, dt in carry],
        compiler_params=_params("arbitrary" if acc_out or carry else "parallel"),
    )(*args)
    return res


def _dn_pre_math(cur, halo, ba, cw, alog, dtb):
    tt = cur.shape[0]
    a = jax.nn.silu(_causal_conv(cur, halo, cw, DN_CONV))
    pieces = []
    for p in range(2 * DN_HEADS):
        xh = a[:, p * HEAD_DIM:(p + 1) * HEAD_DIM]
        xh = xh * lax.rsqrt(jnp.sum(xh * xh, axis=-1, keepdims=True) + EPS)
        if p < DN_HEADS:
            xh = xh * (HEAD_DIM ** -0.5)
        pieces.append(xh)
    pieces.append(a[:, 2 * DN_WIDTH:])
    qkvn = jnp.concatenate(pieces, axis=1)
    lane = lax.broadcasted_iota(jnp.int32, ba.shape, 1)
    raw = jnp.where(lane < DN_HEADS, jax.nn.sigmoid(ba), -jnp.exp(alog) * jax.nn.softplus(ba + dtb))
    r = lax.broadcasted_iota(jnp.int32, (tt, tt), 0)
    c = lax.broadcasted_iota(jnp.int32, (tt, tt), 1)
    tri = jnp.where((r // CHUNK == c // CHUNK) & (c <= r), 1.0, 0.0).astype(F32)
    cums = _ein("ij,jk->ik", LHS_EXACT, tri, raw)
    bg = jnp.where(lane < DN_HEADS, raw, cums)
    return qkvn, bg


def _mix_math(o, z, gb, gc, gc_halo, hv, hv_halo, dng, scg, scw):
    outs = []
    for h in range(DN_HEADS):
        sl = slice(h * HEAD_DIM, (h + 1) * HEAD_DIM)
        oh = o[:, sl]
        outs.append(oh * lax.rsqrt(jnp.mean(oh * oh, axis=-1, keepdims=True) + EPS) * dng * jax.nn.silu(z[:, sl]))
    y = gb * _causal_conv(gc * hv, gc_halo * hv_halo, scw, SC_CONV)
    gw = SC_WIDTH // SC_GROUPS
    for g in range(SC_GROUPS):
        sl = slice(g * gw, (g + 1) * gw)
        yg = y[:, sl]
        outs.append(yg * lax.rsqrt(jnp.mean(yg * yg, axis=-1, keepdims=True) + EPS) * scg[:, sl])
    return jnp.concatenate(outs, axis=1)


def _tri_inverse(a):
    c = a.shape[-1]
    r = lax.broadcasted_iota(jnp.int32, (c, c), 0)
    q = lax.broadcasted_iota(jnp.int32, (c, c), 1)
    eye = jnp.where(r == q, 1.0, 0.0).astype(F32)[None]
    blk = (r // 16 == q // 16)[None]
    d = jnp.where(blk, a, 0.0)
    o = a - d
    mm = functools.partial(_ein, "bij,bjk->bik", PRECISE)
    p = eye - d
    n = mm(d, d)
    for _ in range(2):
        both = mm(jnp.concatenate([n, p], axis=1), n)
        n = both[:, :c]
        p = p + both[:, c:]
    p = p + mm(p, n)
    e = mm(p, o)
    e2 = mm(e, e)
    left = eye - e + e2 - mm(e, e2)
    return mm(left, p)


@jax.custom_vjp
def _inverse_known(a, tinv):
    return tinv


def _inverse_known_fwd(a, tinv):
    return tinv, tinv


def _inverse_known_bwd(tinv, ct):
    left = _ein("bji,bjk->bik", PRECISE, tinv, ct)
    return -_ein("bik,bjk->bij", PRECISE, left, tinv), jnp.zeros_like(tinv)


_inverse_known.defvjp(_inverse_known_fwd, _inverse_known_bwd)


def _delta_intra_math(q, k, v, bg, head, tinv_known=None):
    n = q.shape[0]
    nb = n // CHUNK
    lane = lax.broadcasted_iota(jnp.int32, bg.shape, 1)
    beta = jnp.sum(jnp.where(lane == head, bg, 0.0), axis=1, keepdims=True).reshape(nb, CHUNK, 1)
    gc = jnp.sum(jnp.where(lane == head + DN_HEADS, bg, 0.0), axis=1, keepdims=True).reshape(nb, CHUNK, 1)
    q3, k3, v3 = (a.reshape(nb, CHUNK, HEAD_DIM) for a in (q, k, v))
    r = lax.broadcasted_iota(jnp.int32, (CHUNK, CHUNK), 0)
    c = lax.broadcasted_iota(jnp.int32, (CHUNK, CHUNK), 1)
    eye = jnp.where(r == c, 1.0, 0.0).astype(F32)[None]
    gcr = _ein("bik,bkj->bij", LHS_EXACT, jnp.ones((nb, CHUNK, CHUNK), F32), gc * eye)
    decay = jnp.exp(jnp.where((r >= c)[None], gc - gcr, -1e30))
    kb = k3 * beta
    vb = v3 * beta
    egc = jnp.exp(gc)
    on_k = _ein("bcd,bmd->bcm", FAST, jnp.concatenate([kb, q3], axis=1), k3)
    a = jnp.where((r > c)[None], on_k[:, :CHUNK] * decay, 0.0)
    tinv = _tri_inverse(a) if tinv_known is None else _inverse_known(a, tinv_known)
    uw = _ein("bcm,bmd->bcd", PRECISE, tinv, jnp.concatenate([vb, kb * egc], axis=2))
    u, w = uw[:, :, :HEAD_DIM], uw[:, :, HEAD_DIM:]
    qk = on_k[:, CHUNK:] * decay
    row = lax.broadcasted_iota(jnp.int32, (nb, CHUNK, 1), 1)
    glast = jnp.sum(jnp.where(row == CHUNK - 1, gc, 0.0), axis=1, keepdims=True)
    qd = q3 * egc
    kd = k3 * jnp.exp(glast - gc)
    glb = jnp.broadcast_to(jnp.exp(glast), (nb, HALO, LANES))
    flat = lambda x: x.reshape(n, HEAD_DIM)
    return flat(u), flat(w), flat(qd), flat(kd), qk, glb, tinv


def _delta_step_math(u, w, qd, kd, qk, gl, s):
    c = u.shape[0]
    on_s = _ein("ck,kv->cv", FAST, jnp.concatenate([w, qd], axis=0), s)
    vnew = u - on_s[:c]
    on_v = _ein("cm,mv->cv", FAST, jnp.concatenate([qk, kd.T], axis=0), vnew)
    o = on_s[c:] + on_v[:c]
    s2 = s * gl + on_v[c:]
    return o, s2


def _delta_intra(qkvn, bg, nb):
    t = qkvn.shape[0]
    n = t // CHUNK
    rows = nb * CHUNK

    def body(q_ref, k_ref, v_ref, bg_ref, u_ref, w_ref, qd_ref, kd_ref, qk_ref, gl_ref, ti_ref):
        outs = _delta_intra_math(q_ref[...], k_ref[...], v_ref[...], bg_ref[...], pl.program_id(1))
        for r, v in zip((u_ref, w_ref, qd_ref, kd_ref, qk_ref, gl_ref, ti_ref), outs):
            r[...] = v

    col = lambda off: pl.BlockSpec((rows, HEAD_DIM), lambda b, h, off=off: (b, off + h))
    tok = jax.ShapeDtypeStruct((t, DN_WIDTH), F32)
    return pl.pallas_call(
        body, name="delta_intra", grid=(n // nb, DN_HEADS),
        in_specs=[col(0), col(DN_HEADS), col(2 * DN_HEADS), pl.BlockSpec((rows, LANES), lambda b, h: (b, 0))],
        out_specs=[col(0)] * 4 + [pl.BlockSpec((nb, None, CHUNK, CHUNK), lambda b, h: (b, h, 0, 0)),
                                  pl.BlockSpec((nb, None, HALO, LANES), lambda b, h: (b, h, 0, 0)),
                                  pl.BlockSpec((nb, None, CHUNK, CHUNK), lambda b, h: (b, h, 0, 0))],
        out_shape=[tok] * 4 + [jax.ShapeDtypeStruct((n, DN_HEADS, CHUNK, CHUNK), F32),
                               jax.ShapeDtypeStruct((n, DN_HEADS, HALO, LANES), F32),
                               jax.ShapeDtypeStruct((n, DN_HEADS, CHUNK, CHUNK), F32)],
        compiler_params=_params("parallel", "arbitrary"),
    )(qkvn, qkvn, qkvn, bg)


def _delta_intra_bwd(qkvn, bg, tinv, cts, nb):
    t = qkvn.shape[0]
    n = t // CHUNK
    rows = nb * CHUNK

    def body(q_ref, k_ref, v_ref, bg_ref, ti_ref, du, dw, dqd, dkd, dqk, dgl, dq_ref, dk_ref, dv_ref, dbg_ref):
        h = pl.program_id(1)
        ti = ti_ref[...]
        _, vjp = jax.vjp(lambda q, k, v, b: _delta_intra_math(q, k, v, b, h, ti)[:6],
                         q_ref[...], k_ref[...], v_ref[...], bg_ref[...])
        dq, dk, dv, dbg = vjp((du[...], dw[...], dqd[...], dkd[...], dqk[...], dgl[...]))
        dq_ref[...] = dq
        dk_ref[...] = dk
        dv_ref[...] = dv

        @pl.when(h == 0)
        def _():
            dbg_ref[...] = jnp.zeros_like(dbg_ref)

        dbg_ref[...] += dbg

    col = lambda off: pl.BlockSpec((rows, HEAD_DIM), lambda b, h, off=off: (b, off + h))
    bgs = pl.BlockSpec((rows, LANES), lambda b, h: (b, 0))
    qks = pl.BlockSpec((nb, None, CHUNK, CHUNK), lambda b, h: (b, h, 0, 0))
    gls = pl.BlockSpec((nb, None, HALO, LANES), lambda b, h: (b, h, 0, 0))
    tok = jax.ShapeDtypeStruct((t, DN_WIDTH), F32)
    return pl.pallas_call(
        body, name="delta_intra_bwd", grid=(n // nb, DN_HEADS),
        in_specs=[col(0), col(DN_HEADS), col(2 * DN_HEADS), bgs, qks, col(0), col(0), col(0), col(0), qks, gls],
        out_specs=[col(0), col(0), col(0), bgs],
        out_shape=[tok, tok, tok, jax.ShapeDtypeStruct((t, LANES), F32)],
        compiler_params=_params("parallel", "arbitrary"),
    )(qkvn, qkvn, qkvn, bg, tinv, *cts)


def _delta_scan(u, w, qd, kd, qk, glb, cb):
    t = u.shape[0]
    n = t // CHUNK
    rows = cb * CHUNK

    def body(u_ref, w_ref, qd_ref, kd_ref, qk_ref, gl_ref, o_ref, s_ref, s_scr):
        @pl.when(pl.program_id(0) == 0)
        def _():
            s_scr[...] = jnp.zeros_like(s_scr)

        def chunk(c, carry):
            r0 = pl.multiple_of(c * CHUNK, CHUNK)
            for h in range(DN_HEADS):
                sl = (pl.ds(r0, CHUNK), slice(h * HEAD_DIM, (h + 1) * HEAD_DIM))
                s = s_scr[h]
                s_ref[c, h] = s
                o, s2 = _delta_step_math(u_ref[sl], w_ref[sl], qd_ref[sl], kd_ref[sl], qk_ref[c, h],
                                         gl_ref[c, h][0:1, :], s)
                o_ref[sl] = o
                s_scr[h] = s2
            return carry

        lax.fori_loop(0, cb, chunk, 0)

    tok = pl.BlockSpec((rows, DN_WIDTH), lambda i: (i, 0))
    return pl.pallas_call(
        body, name="delta_scan", grid=(n // cb,),
        in_specs=[tok] * 4 + [pl.BlockSpec((cb, DN_HEADS, CHUNK, CHUNK), lambda i: (i, 0, 0, 0)),
                              pl.BlockSpec((cb, DN_HEADS, HALO, LANES), lambda i: (i, 0, 0, 0))],
        out_specs=[tok, pl.BlockSpec((cb, DN_HEADS, HEAD_DIM, HEAD_DIM), lambda i: (i, 0, 0, 0))],
        out_shape=[jax.ShapeDtypeStruct((t, DN_WIDTH), F32),
                   jax.ShapeDtypeStruct((n, DN_HEADS, HEAD_DIM, HEAD_DIM), F32)],
        scratch_shapes=[pltpu.VMEM((DN_HEADS, HEAD_DIM, HEAD_DIM), F32)],
        compiler_params=_params("arbitrary"),
    )(u, w, qd, kd, qk, glb)


def _delta_scan_bwd(u, w, qd, kd, qk, glb, s_all, do, cb):
    t = u.shape[0]
    n = t // CHUNK
    nblk = n // cb
    rows = cb * CHUNK

    def body(u_ref, w_ref, qd_ref, kd_ref, qk_ref, gl_ref, s_ref, do_ref,
             du_ref, dw_ref, dqd_ref, dkd_ref, dqk_ref, dgl_ref, ds_scr):
        @pl.when(pl.program_id(0) == 0)
        def _():
            ds_scr[...] = jnp.zeros_like(ds_scr)

        def chunk(step, carry):
            c = cb - 1 - step
            r0 = pl.multiple_of(c * CHUNK, CHUNK)
            for h in range(DN_HEADS):
                sl = (pl.ds(r0, CHUNK), slice(h * HEAD_DIM, (h + 1) * HEAD_DIM))
                gl_tile = gl_ref[c, h]
                prim = (u_ref[sl], w_ref[sl], qd_ref[sl], kd_ref[sl], qk_ref[c, h], gl_tile, s_ref[c, h])
                _, vjp = jax.vjp(lambda a, b, cc, d, e, g, s: _delta_step_math(a, b, cc, d, e, g[0:1, :], s), *prim)
                du, dw, dqd, dkd, dqk, dgl, ds = vjp((do_ref[sl], ds_scr[h]))
                du_ref[sl] = du
                dw_ref[sl] = dw
                dqd_ref[sl] = dqd
                dkd_ref[sl] = dkd
                dqk_ref[c, h] = dqk
                dgl_ref[c, h] = dgl
                ds_scr[h] = ds
            return carry

        lax.fori_loop(0, cb, chunk, 0)

    rev = lambda i: nblk - 1 - i
    tok = pl.BlockSpec((rows, DN_WIDTH), lambda i: (rev(i), 0))
    qks = pl.BlockSpec((cb, DN_HEADS, CHUNK, CHUNK), lambda i: (rev(i), 0, 0, 0))
    gls = pl.BlockSpec((cb, DN_HEADS, HALO, LANES), lambda i: (rev(i), 0, 0, 0))
    ss = pl.BlockSpec((cb, DN_HEADS, HEAD_DIM, HEAD_DIM), lambda i: (rev(i), 0, 0, 0))
    tshape = jax.ShapeDtypeStruct((t, DN_WIDTH), F32)
    return pl.pallas_call(
        body, name="delta_scan_bwd", grid=(nblk,),
        in_specs=[tok] * 4 + [qks, gls, ss, tok],
        out_specs=[tok] * 4 + [qks, gls],
        out_shape=[tshape] * 4 + [jax.ShapeDtypeStruct(qk.shape, F32), jax.ShapeDtypeStruct(glb.shape, F32)],
        scratch_shapes=[pltpu.VMEM((DN_HEADS, HEAD_DIM, HEAD_DIM), F32)],
        compiler_params=_params("arbitrary"),
    )(u, w, qd, kd, qk, glb, s_all, do)


def _peer(mask):
    x, y, c = lax.axis_index("x"), lax.axis_index("y"), lax.axis_index("c")
    return (x ^ ((mask >> 2) & 1), y ^ ((mask >> 1) & 1), c ^ (mask & 1))


def _my_index():
    return 4 * lax.axis_index("x") + 2 * lax.axis_index("y") + lax.axis_index("c")


class _Exchange:
    CHIP_MASKS = (4, 2, 6)

    def __init__(self, kind, arrays):
        self.kind = kind
        self.arrays = list(arrays)
        self.na = na = len(self.arrays)
        if kind == "gather":
            self.out_shape = [jax.ShapeDtypeStruct((N_DEV,) + a.shape, a.dtype) for a in self.arrays]
        else:
            self.out_shape = [jax.ShapeDtypeStruct(a.shape, a.dtype) for a in self.arrays]
        self.scratch = [pltpu.SemaphoreType.DMA((na, 7)), pltpu.SemaphoreType.DMA((na, 7)),
                        pltpu.SemaphoreType.DMA((na,))]

    def _copies(self, ins, outs, sems):
        send_sems, recv_sems, local_sems = sems
        me = _my_index()
        local, first, passed, arrivals = [], [], [], []
        if self.kind == "gather":
            def rc(a, k, block, to, own=False):
                def make():
                    dst = outs[a].at[block]
                    return pltpu.make_async_remote_copy(src_ref=ins[a] if own else dst, dst_ref=dst,
                                                        send_sem=send_sems.at[a, k], recv_sem=recv_sems.at[a, k],
                                                        device_id=to, device_id_type=MESH)
                return make

            sib = _peer(1)
            for a in range(self.na):
                local.append(lambda a=a: pltpu.make_async_copy(ins[a], outs[a].at[me], local_sems.at[a]))
                first.append(rc(a, 0, me, sib, own=True))
                arrivals.append(rc(a, 0, me ^ 1, _peer(0)))
                for j, m in enumerate(self.CHIP_MASKS):
                    first.append(rc(a, 1 + j, me, _peer(m), own=True))
                    passed.append((rc(a, 1 + j, me ^ m, _peer(0)), rc(a, 4 + j, me ^ m, sib)))
                    arrivals.append(rc(a, 4 + j, me ^ m ^ 1, _peer(0)))
        else:
            for a in range(self.na):
                local.append(lambda a=a: pltpu.make_async_copy(ins[a].at[me], outs[a].at[me], local_sems.at[a]))
                for m in range(1, N_DEV):
                    def make(a=a, m=m):
                        return pltpu.make_async_remote_copy(
                            src_ref=ins[a].at[me ^ m], dst_ref=outs[a].at[me], send_sem=send_sems.at[a, m - 1],
                            recv_sem=recv_sems.at[a, m - 1], device_id=_peer(m), device_id_type=MESH)
                    first.append(make)
                    arrivals.append(make)
        return local, first, passed, arrivals

    def start(self, ins, outs, sems):
        local, first, _, _ = self._copies(ins, outs, sems)
        for make in local + first:
            make().start()

    def finish(self, ins, outs, sems):
        local, first, passed, arrivals = self._copies(ins, outs, sems)
        for landed, onward in passed:
            landed().wait_recv()
            onward().start()
        for make in arrivals:
            make().wait_recv()
        for make in first + [p for _, p in passed]:
            make().wait_send()
        for make in local:
            make().wait()

    def run(self, name):
        na = self.na

        def body(*refs):
            ins, outs, sems = refs[:na], refs[na:2 * na], refs[2 * na:]
            self.start(ins, outs, sems)
            self.finish(ins, outs, sems)

        hbm = pl.BlockSpec(memory_space=pltpu.HBM)
        return pl.pallas_call(
            body, name=name, in_specs=[hbm] * na, out_specs=[hbm] * na, out_shape=self.out_shape,
            scratch_shapes=self.scratch, compiler_params=pltpu.CompilerParams(has_side_effects=True),
        )(*self.arrays)


def _all_gather(shards, name):
    return _Exchange("gather", shards).run(name)


def _adamw_math(w, g, m, v):
    m2 = ADAM_B1 * m + (1.0 - ADAM_B1) * g
    v2 = ADAM_B2 * v + (1.0 - ADAM_B2) * jnp.square(g)
    m_hat = m2 / (1.0 - ADAM_B1 ** ADAM_STEP)
    v_hat = v2 / (1.0 - ADAM_B2 ** ADAM_STEP)
    delta = -ADAM_LR * (m_hat / (jnp.sqrt(v_hat) + ADAM_EPS) + ADAM_WD * w)
    return delta, m2, v2


def _sum_adamw(parts, w, m, v, name):
    r, c = w.shape
    tr = _pick(r, (512, 256, 352, 128, 64, 32, 16, 8))
    np_ = parts.shape[0]

    def body(p_ref, w_ref, m_ref, v_ref, g_ref, d_ref, m2_ref, v2_ref):
        g = p_ref[0].astype(F32)
        for d in range(1, np_):
            g = g + p_ref[d].astype(F32)
        delta, m2, v2 = _adamw_math(w_ref[...], g, m_ref[...], v_ref[...])
        g_ref[...] = g
        d_ref[...] = delta
        m2_ref[...] = m2
        v2_ref[...] = v2

    blk = pl.BlockSpec((tr, c), lambda i: (i, 0))
    shp = jax.ShapeDtypeStruct((r, c), F32)
    return pl.pallas_call(
        body, name=name, grid=(r // tr,),
        in_specs=[pl.BlockSpec((np_, tr, c), lambda i: (0, i, 0)), blk, blk, blk],
        out_specs=[blk] * 4, out_shape=[shp] * 4, compiler_params=_params("parallel"),
    )(parts, w, m, v)


def _sum_rows(parts, name):
    np_, r, c = parts.shape

    def body(p_ref, o_ref):
        g = p_ref[0]
        for d in range(1, np_):
            g = g + p_ref[d]
        o_ref[...] = g

    return pl.pallas_call(body, name=name, out_shape=jax.ShapeDtypeStruct((r, c), F32))(parts)


def _pad_w_in(w):
    d = w.shape[1]
    n_ba = 2 * DN_HEADS
    a = w[:SC_OFF]
    ba = w[SC_OFF:SC_OFF + n_ba]
    sc = w[SC_OFF + n_ba:]
    return jnp.concatenate([a, sc, ba, jnp.zeros((BA_W - n_ba, d), w.dtype)], axis=0)


def _unpad_w_in(wp):
    n_ba = 2 * DN_HEADS
    return jnp.concatenate([wp[:SC_OFF], wp[BA_OFF:BA_OFF + n_ba], wp[SC_OFF:BA_OFF]], axis=0)


def _lane_row(v, off):
    return jnp.pad(v.astype(F32), (off, LANES - off - v.shape[0]))[None]


TT = 256
NB_INTRA = 32
CB_SCAN = 8


def _layer_fwd(x, p, ride_proj, ride_ffn, late):
    t, d = x.shape
    got_proj = got_ffn = None
    if ride_proj:
        h, proj, got_proj = _norm_matmul(x, p["norm1_g"], p["w_in"], "proj_fwd_gather", ride=ride_proj)
    else:
        h, proj = _norm_matmul(x, p["norm1_g"], p["w_in"], "proj_fwd")
    p = {**p, **late(got_proj)}
    qkvn, bg = _tok_call(
        "dn_pre", lambda tv, cv: (_dn_pre_math(*tv, *cv), ()), t, TT,
        [(proj, QKV_W, 0, "cur"), (proj, QKV_W, 0, "prev"), (proj, LANES, BA_OFF // LANES, "cur")],
        [p["dn_conv_w"], p["alog_row"], p["dt_row"]],
        [(TT, QKV_W, F32), (TT, LANES, F32)], [])
    u, w, qd, kd, qk, glb, tinv = _delta_intra(qkvn, bg, min(NB_INTRA, t // CHUNK))
    o, s_all = _delta_scan(u, w, qd, kd, qk, glb, CB_SCAN)
    cb0 = SC_OFF // SC_WIDTH
    mix_in = [(o, DN_WIDTH, 0, "cur"), (proj, DN_WIDTH, Z_OFF // DN_WIDTH, "cur"),
              (proj, SC_WIDTH, cb0, "cur"), (proj, SC_WIDTH, cb0 + 1, "cur"), (proj, SC_WIDTH, cb0 + 1, "prev"),
              (proj, SC_WIDTH, cb0 + 2, "cur"), (proj, SC_WIDTH, cb0 + 2, "prev")]
    mix_const = [p["dn_norm_g"], p["sc_norm_g"], p["sc_conv_w"]]
    (cat,) = _tok_call("mix_post", lambda tv, cv: ((_mix_math(*tv, *cv),), ()), t, TT,
                       mix_in, mix_const, [(TT, 2 * DN_WIDTH, MXU_DTYPE)], [])
    x_mid = _matmul(cat, p["w_out"], "nn", F32, "out_proj", residual=x)
    if ride_ffn:
        h2, gu, act, got_ffn = _ffn_up_swiglu(x_mid, p["norm2_g"], p["w_gu"], "ffn_up_gather", ride=ride_ffn)
    else:
        h2, gu, act = _ffn_up_swiglu(x_mid, p["norm2_g"], p["w_gu"], "ffn_up")
    x_out = _matmul(act, p["w_down"], "nn", F32, "ffn_down", residual=x_mid)
    saved = dict(x=x, h=h, proj=proj, qkvn=qkvn, bg=bg, u=u, w=w, qd=qd, kd=kd, qk=qk, glb=glb, tinv=tinv, s_all=s_all, o=o,
                 cat=cat, x_mid=x_mid, h2=h2, gu=gu, act=act, mix_in=mix_in, mix_const=mix_const)
    return x_out, saved, p, got_proj, got_ffn


def _layer_bwd(dx_out, p, s, ride_prev, scatter_of):
    t, d = dx_out.shape
    got = {}
    dgu = _ffn_down_dx_swiglu(dx_out, p["w_down"], s["gu"], "ffn_down_dx")
    d_w_down = _matmul(s["act"], dx_out, "tn", MXU_DTYPE, "ffn_down_dw")
    if ride_prev:
        dx_mid, d_norm2, got["prev"] = _matmul_norm_bwd(dgu, p["w_gu"], s["x_mid"], p["norm2_g"], dx_out,
                                                        "ffn_up_dx_scatter", ride=ride_prev, w_rows_k=True)
    else:
        dx_mid, d_norm2 = _matmul_norm_bwd(dgu, p["w_gu"], s["x_mid"], p["norm2_g"], dx_out, "ffn_up_dx",
                                           w_rows_k=True)
    d_w_gu, got["down"] = _matmul(dgu, s["h2"], "tn", MXU_DTYPE, "ffn_up_dw_scatter", ride=scatter_of(d_w_down))
    d_gate, d_up = _split_gu(d_w_gu)
    dcat = _matmul(dx_mid, p["w_out"], "nt", F32, "out_proj_dx")
    d_w_out = _matmul(s["cat"], dx_mid, "tn", MXU_DTYPE, "out_proj_dw")

    def mix_bwd(tv, cv):
        prim = tuple(tv[:7]) + tuple(cv)
        _, vjp = jax.vjp(_mix_math, *prim)
        do, dz, dgb, dgc, dgch, dhv, dhvh, ddng, dscg, dscw = vjp(tv[7])
        return (do, dz, dgb, dgc, dgch, dhv, dhvh), (ddng, dscg, dscw)

    wide = (TT, DN_WIDTH, F32)
    final = (TT, DN_WIDTH, MXU_DTYPE)
    halo = (HALO, SC_WIDTH, F32)
    do, dz, dgb, dgc, dgc_h, dhv, dhv_h, d_dn_norm, d_sc_norm, d_sc_conv = _tok_call(
        "mix_post_bwd", mix_bwd, t, TT, s["mix_in"] + [(dcat, 2 * DN_WIDTH, 0, "cur")], s["mix_const"],
        [wide, final, final, wide, halo, wide, halo],
        [((1, HEAD_DIM), F32), ((1, SC_WIDTH), F32), ((SC_CONV, SC_WIDTH), F32)])
    cts = _delta_scan_bwd(s["u"], s["w"], s["qd"], s["kd"], s["qk"], s["glb"], s["s_all"], do, CB_SCAN)
    dq, dk, dv, dbg = _delta_intra_bwd(s["qkvn"], s["bg"], s["tinv"], cts, min(NB_INTRA, t // CHUNK))
    proj = s["proj"]

    def with_halo(cur, nxt):
        return cur + jnp.concatenate([jnp.zeros((TT - HALO, cur.shape[1]), F32), nxt], axis=0)

    def dn_pre_bwd(tv, cv, carried):
        cur, hal, ba, dq_, dk_, dv_, dbg_, dz_, dgb_, dgc_, dgc_n, dhv_, dhv_n = tv
        _, vjp = jax.vjp(_dn_pre_math, cur, hal, ba, *cv)
        dcur, dhal, dba, dcw, dal, ddt = vjp((jnp.concatenate([dq_, dk_, dv_], axis=1), dbg_))
        dproj_rows = jnp.concatenate([with_halo(dcur, carried[0]), dz_.astype(F32), dgb_.astype(F32), with_halo(dgc_, dgc_n),
                                      with_halo(dhv_, dhv_n), dba, jnp.zeros((TT, BA_W - LANES), F32)], axis=1)
        return (dproj_rows,), (dcw, dal, ddt), (dhal,)

    dproj, d_dn_conv, d_alog, d_dt = _tok_call(
        "dn_pre_bwd", dn_pre_bwd, t, TT,
        [(proj, QKV_W, 0, "cur"), (proj, QKV_W, 0, "prev"), (proj, LANES, BA_OFF // LANES, "cur"),
         (dq, DN_WIDTH, 0, "cur"), (dk, DN_WIDTH, 0, "cur"), (dv, DN_WIDTH, 0, "cur"), (dbg, LANES, 0, "cur"),
         (dz, DN_WIDTH, 0, "cur"), (dgb, SC_WIDTH, 0, "cur"), (dgc, SC_WIDTH, 0, "cur"), (dgc_h, SC_WIDTH, 0, "next8"),
         (dhv, SC_WIDTH, 0, "cur"), (dhv_h, SC_WIDTH, 0, "next8")],
        [p["dn_conv_w"], p["alog_row"], p["dt_row"]],
        [(TT, PROJ_W, MXU_DTYPE)],
        [((DN_CONV, QKV_W), F32), ((1, LANES), F32), ((1, LANES), F32)],
        carry=[((HALO, QKV_W), F32)])
    dx_in, d_norm1, got["gate"] = _matmul_norm_bwd(dproj, p["w_in"], s["x"], p["norm1_g"], dx_mid,
                                                   "proj_dx_scatter", ride=scatter_of(d_gate), w_rows_k=True)
    d_w_in, got["up"] = _matmul(dproj, s["h"], "tn", MXU_DTYPE, "proj_dw_scatter", ride=scatter_of(d_up))
    grads = dict(w_in=d_w_in, w_out=d_w_out, w_gu=d_w_gu, w_down=d_w_down, norm1_g=d_norm1, norm2_g=d_norm2,
                 dn_norm_g=d_dn_norm, sc_norm_g=d_sc_norm, sc_conv_w=d_sc_conv, dn_conv_w=d_dn_conv,
                 alog=d_alog, dt=d_dt)
    return dx_in, grads, got


def _final_loss(x, g, target):
    t, d = x.shape

    def fn(tv, cv):
        xv, tg = tv

        def loss_fn(xx, gg):
            err = jnp.square(_rms_norm(xx, gg) - tg)
            return 0.5 * jnp.sum(jnp.mean(err, axis=-1))

        loss, vjp = jax.vjp(loss_fn, xv, cv[0])
        dx, dg = vjp(jnp.ones((), F32))
        return (dx,), (jnp.full((1, LANES), loss, F32), dg)

    return _tok_call("final_loss", fn, t, TT, [(x, d, 0, "cur"), (target, d, 0, "cur")], [g],
                     [(TT, d, F32)], [((1, LANES), F32), ((1, d), F32)])


def _pack_rows(arrs):
    rows, offs, r0 = [], [], 0
    for a in arrs:
        n = a.size
        nr = -(-n // LANES)
        flat = jnp.pad(a.reshape(-1).astype(F32), (0, nr * LANES - n))
        rows.append(flat.reshape(nr, LANES))
        offs.append((r0, nr, a.shape))
        r0 += nr
    pad = (-r0) % 8
    if pad:
        rows.append(jnp.zeros((pad, LANES), F32))
    return jnp.concatenate(rows, axis=0), offs


def _unpack_rows(packed, offs):
    out = []
    for r0, nr, shp in offs:
        n = 1
        for s_ in shp:
            n *= s_
        out.append(packed[r0:r0 + nr].reshape(-1)[:n].reshape(shp))
    return out


def kernel(x, norm1_g, w_in, dn_conv_w, dn_a_log, dn_dt_bias, dn_norm_g, sc_conv_w, sc_norm_g, w_out, norm2_g, ffn_w_gate, ffn_w_up, ffn_w_down, final_norm_g, loss_target, m_norm1_g, m_w_in, m_dn_conv_w, m_dn_a_log, m_dn_dt_bias, m_dn_norm_g, m_sc_conv_w, m_sc_norm_g, m_w_out, m_norm2_g, m_ffn_w_gate, m_ffn_w_up, m_ffn_w_down, m_final_norm_g, v_norm1_g, v_w_in, v_dn_conv_w, v_dn_a_log, v_dn_dt_bias, v_dn_norm_g, v_sc_conv_w, v_sc_norm_g, v_w_out, v_norm2_g, v_ffn_w_gate, v_ffn_w_up, v_ffn_w_down, v_final_norm_g):
    depth, d, cin = w_in.shape
    t = x.shape[1]
    dff_s = ffn_w_gate.shape[2]
    tr = lambda a: a.transpose(0, 2, 1)
    gate_t, up_t = tr(ffn_w_gate), tr(ffn_w_up)
    by_layer = lambda a: [a[l].T for l in range(depth)]
    win_t = by_layer(w_in)
    me = _my_index()
    x2 = x.reshape(t, d)
    tgt = loss_target.reshape(t, d)

    def shards(l):
        return [a[l].astype(MXU_DTYPE) for a in (win_t, w_out, gate_t, up_t, ffn_w_down)]

    conv_pack, conv_offs = _pack_rows([dn_conv_w, sc_conv_w])
    conv_all, *first = _all_gather([conv_pack] + shards(0)[:2], "gather_first")
    dn_parts, sc_parts = zip(*[_unpack_rows(conv_all[j], conv_offs) for j in range(N_DEV)])
    dn_conv_full = jnp.concatenate(dn_parts, axis=2)
    sc_conv_full = jnp.concatenate(sc_parts, axis=2)

    def mixer_params(l, g_in, g_out):
        return dict(
            w_in=_pad_w_in(g_in.reshape(N_DEV * cin, d)), w_out=g_out.reshape(d, d),
            norm1_g=norm1_g[l][None], norm2_g=norm2_g[l][None], dn_norm_g=dn_norm_g[l][None],
            sc_norm_g=sc_norm_g[l][None], dn_conv_w=dn_conv_full[l], sc_conv_w=sc_conv_full[l],
            alog_row=_lane_row(dn_a_log[l], DN_HEADS), dt_row=_lane_row(dn_dt_bias[l], DN_HEADS))

    def ffn_params(g_gate, g_up, g_down):
        dff = N_DEV * dff_s
        return dict(w_gu=_interleave_gu(g_gate.reshape(dff, d), g_up.reshape(dff, d)), w_down=g_down.reshape(dff, d))

    nxt = mixer_params(0, *first)
    nxt_ffn = None
    params, saved = [], []
    xc = x2
    for l in range(depth):
        last = l + 1 == depth
        own_ffn = shards(l)[2:] if nxt_ffn is None else []
        ahead = [] if last else shards(l + 1)
        ride_proj = _Exchange("gather", own_ffn + ahead[:2]) if own_ffn or ahead else None
        ride_ffn = None if last else _Exchange("gather", ahead[2:])

        def late(got, own_ffn=own_ffn, nxt_ffn=nxt_ffn):
            return ffn_params(*got[:3]) if own_ffn else nxt_ffn

        xc, s, p_l, got_proj, got_ffn = _layer_fwd(xc, nxt, ride_proj, ride_ffn, late)
        params.append(p_l)
        saved.append(s)
        if not last:
            nxt = mixer_params(l + 1, *got_proj[len(own_ffn):])
            nxt_ffn = ffn_params(*got_ffn)
    dx, loss_part, d_final = _final_loss(xc, final_norm_g[None], tgt)

    names = ("w_in", "w_out", "ffn_w_gate", "ffn_w_up", "ffn_w_down")
    big_out = {n: {k: [None] * depth for k in ("g", "d", "m", "v")} for n in names}
    w_loc = dict(w_in=win_t, w_out=w_out, ffn_w_gate=gate_t, ffn_w_up=up_t, ffn_w_down=ffn_w_down)
    m_loc = dict(w_in=by_layer(m_w_in), w_out=m_w_out, ffn_w_gate=tr(m_ffn_w_gate), ffn_w_up=tr(m_ffn_w_up), ffn_w_down=m_ffn_w_down)
    v_loc = dict(w_in=by_layer(v_w_in), w_out=v_w_out, ffn_w_gate=tr(v_ffn_w_gate), ffn_w_up=tr(v_ffn_w_up), ffn_w_down=v_ffn_w_down)


    def scatter_of(grad):
        return _Exchange("scatter", [grad.reshape(N_DEV, dff_s, d).astype(MXU_DTYPE)])

    def apply(l, which, recv):
        for n, r in zip(which, recv):
            res = _sum_adamw(r, w_loc[n][l], m_loc[n][l], v_loc[n][l], "adamw_" + n)
            for k, a in zip(("g", "d", "m", "v"), res):
                big_out[n][k][l] = a

    grads = [None] * depth
    pending = None
    for l in reversed(range(depth)):
        ride_prev = _Exchange("scatter", pending) if pending else None
        dx, grads[l], got = _layer_bwd(dx, params[l], saved[l], ride_prev, scatter_of)
        if ride_prev:
            apply(l + 1, names[:2], got["prev"])
        apply(l, names[2:], [*got["gate"], *got["up"], *got["down"]])
        pending = [_unpad_w_in(grads[l]["w_in"]).reshape(N_DEV, cin, d).astype(MXU_DTYPE),
                   grads[l]["w_out"].reshape(N_DEV, d // N_DEV, d).astype(MXU_DTYPE)]
    apply(0, names[:2], _Exchange("scatter", pending).run("scatter_last"))
    grad_x = dx.reshape(x.shape)
    big_out = {n: {k: jnp.stack(v_, axis=1 if n == "w_in" else 0) for k, v_ in o.items()} for n, o in big_out.items()}
    for n in ("ffn_w_gate", "ffn_w_up"):
        big_out[n] = {k: tr(a) for k, a in big_out[n].items()}
    big_out["w_in"] = {k: a.transpose(1, 2, 0) for k, a in big_out["w_in"].items()}

    stack = lambda key: jnp.stack([grads[l][key] for l in range(depth)])
    small_parts = [stack("norm1_g").reshape(depth, d), stack("norm2_g").reshape(depth, d), d_final.reshape(d),
                   stack("dn_norm_g").reshape(depth, HEAD_DIM), stack("sc_norm_g").reshape(depth, SC_WIDTH),
                   stack("alog").reshape(depth, LANES), stack("dt").reshape(depth, LANES),
                   stack("dn_conv_w"), stack("sc_conv_w"), loss_part]
    small_pack, small_offs = _pack_rows(small_parts)
    (small_all,) = _all_gather([small_pack], "gather_small")
    total = _sum_rows(small_all, "sum_small")
    (g_n1, g_n2, g_fin, g_dnn, g_scn, g_alog, g_dt, g_dnc, g_scc, loss_row) = _unpack_rows(total, small_offs)
    loss = loss_row[0, 0]
    g_alog = g_alog[:, DN_HEADS:2 * DN_HEADS]
    g_dt = g_dt[:, DN_HEADS:2 * DN_HEADS]
    dnc_w = dn_conv_w.shape[2]
    scc_w = sc_conv_w.shape[2]
    g_dnc = lax.dynamic_slice_in_dim(g_dnc, me * dnc_w, dnc_w, axis=2)
    g_scc = lax.dynamic_slice_in_dim(g_scc, me * scc_w, scc_w, axis=2)
    sm_g = [g_n1, g_dnc, g_alog, g_dt, g_dnn, g_scc, g_scn, g_n2, g_fin]
    sm_w = [norm1_g, dn_conv_w, dn_a_log, dn_dt_bias, dn_norm_g, sc_conv_w, sc_norm_g, norm2_g, final_norm_g]
    sm_m = [m_norm1_g, m_dn_conv_w, m_dn_a_log, m_dn_dt_bias, m_dn_norm_g, m_sc_conv_w, m_sc_norm_g, m_norm2_g, m_final_norm_g]
    sm_v = [v_norm1_g, v_dn_conv_w, v_dn_a_log, v_dn_dt_bias, v_dn_norm_g, v_sc_conv_w, v_sc_norm_g, v_norm2_g, v_final_norm_g]
    pg, offs = _pack_rows(sm_g)
    pw, _ = _pack_rows(sm_w)
    pm, _ = _pack_rows(sm_m)
    pv, _ = _pack_rows(sm_v)
    sg, sd, sm_, sv = _sum_adamw(pg[None], pw, pm, pv, "adamw_small")
    small_out = {k: _unpack_rows(a, offs) for k, a in zip(("g", "d", "m", "v"), (sg, sd, sm_, sv))}

    def outputs(k):
        s_ = small_out[k]
        b = big_out
        return [s_[0], b["w_in"][k], s_[1], s_[2], s_[3], s_[4], s_[5], s_[6], b["w_out"][k], s_[7],
                b["ffn_w_gate"][k], b["ffn_w_up"][k], b["ffn_w_down"][k], s_[8]]

    return (loss, grad_x, *outputs("g"), *outputs("d"), *outputs("m"), *outputs("v"))
```

```python
import functools

import jax
import jax.numpy as jnp
from jax import lax
from jax.experimental import pallas as pl
from jax.experimental.pallas import tpu as pltpu

F32 = jnp.float32
MXU_DTYPE = jnp.bfloat16
MESH = pl.DeviceIdType.MESH

N_DEV = 8
EPS = 1e-6
DN_HEADS = 4
HEAD_DIM = 128
DN_WIDTH = DN_HEADS * HEAD_DIM
SC_WIDTH = 512
SC_GROUPS = 4
DN_CONV = 4
SC_CONV = 3
CHUNK = 64
HALO = 8
LANES = 128

QKV_W = 3 * DN_WIDTH
Z_OFF = QKV_W
SC_OFF = Z_OFF + DN_WIDTH
BA_OFF = SC_OFF + 3 * SC_WIDTH
BA_W = 256
PROJ_W = BA_OFF + BA_W

ADAM_LR = 0.001
ADAM_B1 = 0.9
ADAM_B2 = 0.999
ADAM_EPS = 1e-08
ADAM_WD = 0.01
ADAM_STEP = 10


def _pick(n, cands):
    for c in cands:
        if n % c == 0:
            return c
    return n


def _params(*sem):
    return pltpu.CompilerParams(dimension_semantics=sem)


def _rms_norm(x, g):
    return x * lax.rsqrt(jnp.mean(x * x, axis=-1, keepdims=True) + EPS) * g


def _dot(a, b, dims=(((1,), (0,)), ((), ()))):
    return lax.dot_general(a.astype(MXU_DTYPE), b.astype(MXU_DTYPE), dims, preferred_element_type=F32)


def _split_terms(x, terms):
    out = []
    for _ in range(terms):
        hi = x.astype(MXU_DTYPE)
        out.append(hi)
        x = x - hi.astype(F32)
    return out


def _ein_impl(spec, terms, a, b):
    ta, tb = terms
    if ta == 1 and tb == 1:
        return jnp.einsum(spec, a.astype(MXU_DTYPE), b.astype(MXU_DTYPE), preferred_element_type=F32)
    pa, pb = _split_terms(a, ta), _split_terms(b, tb)
    order = max(ta, tb) - 1
    acc = None
    for deg in range(order, -1, -1):
        for i in range(ta):
            j = deg - i
            if 0 <= j < tb:
                t = jnp.einsum(spec, pa[i], pb[j], preferred_element_type=F32)
                acc = t if acc is None else acc + t
    return acc


@functools.partial(jax.custom_vjp, nondiff_argnums=(0, 1))
def _ein(spec, terms, a, b):
    return _ein_impl(spec, terms, a, b)


def _ein_fwd(spec, terms, a, b):
    return _ein_impl(spec, terms, a, b), (a, b)


def _ein_bwd(spec, terms, res, ct):
    a, b = res
    xy, z = spec.split("->")
    x, y = xy.split(",")
    tc = min(max(terms), 2)
    da = _ein_impl(f"{z},{y}->{x}", (tc, terms[1]), ct, b)
    db = _ein_impl(f"{x},{z}->{y}", (terms[0], tc), a, ct)
    return da, db


_ein.defvjp(_ein_fwd, _ein_bwd)

FAST = (1, 1)
PRECISE = (2, 2)
LHS_EXACT = (1, 3)


def _causal_conv(cur, halo, w, k):
    tt = cur.shape[0]
    xp = jnp.concatenate([halo, cur], axis=0)
    y = None
    for j in range(k):
        start = HALO - (k - 1) + j
        term = xp[start:start + tt] * w[j:j + 1]
        y = term if y is None else y + term
    return y


def _matmul(a, b, mode, out_dtype, name, residual=None, ride=None):
    if mode == "nn":
        (m, k), (k2, n) = a.shape, b.shape
    elif mode == "nt":
        (m, k), (n, k2) = a.shape, b.shape
    else:
        (k, m), (k2, n) = a.shape, b.shape
    assert k == k2
    tm = _pick(m, (1024, 1408, 1280, 512, 256, 128))
    tn = _pick(n, (1280, 1408, 1024, 512, 256, 128))
    tk = k if k <= 2816 else _pick(k, (2048, 1024, 768, 512))
    gi, gj, nk = m // tm, n // tn, k // tk
    dims = {"nn": (((1,), (0,)), ((), ())), "nt": (((1,), (1,)), ((), ())), "tn": (((0,), (0,)), ((), ()))}[mode]
    a_spec = {"nn": pl.BlockSpec((tm, tk), lambda i, j, q: (i, q)),
              "nt": pl.BlockSpec((tm, tk), lambda i, j, q: (i, q)),
              "tn": pl.BlockSpec((tk, tm), lambda i, j, q: (q, i))}[mode]
    b_spec = {"nn": pl.BlockSpec((tk, tn), lambda i, j, q: (q, j)),
              "nt": pl.BlockSpec((tn, tk), lambda i, j, q: (j, q)),
              "tn": pl.BlockSpec((tk, tn), lambda i, j, q: (q, j))}[mode]
    o_spec = pl.BlockSpec((tm, tn), lambda i, j, q: (i, j))
    has_res = residual is not None
    n_in = 3 if has_res else 2
    nr = ride.na if ride else 0

    def body(*refs):
        a_ref, b_ref = refs[0], refs[1]
        r_ref = refs[2] if has_res else None
        ride_in = refs[n_in:n_in + nr]
        o_ref = refs[n_in + nr]
        ride_out = refs[n_in + nr + 1:n_in + 2 * nr + 1]
        acc = refs[n_in + 2 * nr + 1]
        ride_sems = refs[n_in + 2 * nr + 2:]
        i, j, q = pl.program_id(0), pl.program_id(1), pl.program_id(2)
        if ride:
            @pl.when((i == 0) & (j == 0) & (q == 0))
            def _():
                ride.start(ride_in, ride_out, ride_sems)

        @pl.when(q == 0)
        def _():
            acc[...] = jnp.zeros_like(acc)

        acc[...] += _dot(a_ref[...], b_ref[...], dims)

        @pl.when(q == nk - 1)
        def _():
            r = acc[...]
            if has_res:
                r = r + r_ref[...]
            o_ref[...] = r.astype(o_ref.dtype)

        if ride:
            @pl.when((i == gi - 1) & (j == gj - 1) & (q == nk - 1))
            def _():
                ride.finish(ride_in, ride_out, ride_sems)

    hbm = pl.BlockSpec(memory_space=pltpu.HBM)
    in_specs = [a_spec, b_spec] + ([o_spec] if has_res else []) + [hbm] * nr
    args = (a, b) + ((residual,) if has_res else ()) + (tuple(ride.arrays) if ride else ())
    res = pl.pallas_call(
        body, name=name, grid=(gi, gj, nk), in_specs=in_specs, out_specs=[o_spec] + [hbm] * nr,
        out_shape=[jax.ShapeDtypeStruct((m, n), out_dtype)] + (ride.out_shape if ride else []),
        scratch_shapes=[pltpu.VMEM((tm, tn), F32)] + (ride.scratch if ride else []),
        compiler_params=pltpu.CompilerParams(
            dimension_semantics=("arbitrary",) * 3 if ride else ("parallel", "parallel", "arbitrary"),
            has_side_effects=bool(ride)),
    )(*args)
    return (res[0], res[1:]) if ride else res[0]


def _norm_matmul(x, g, w, name, ride=None):
    t, d = x.shape
    n = w.shape[0]
    tm = _pick(t, (1024, 512, 256, 128))
    tn = _pick(n, (1280, 1408, 1024, 512, 256, 128))
    gi, gj = t // tm, n // tn
    nr = ride.na if ride else 0

    def body(*refs):
        x_ref, g_ref, w_ref = refs[:3]
        ride_in = refs[3:3 + nr]
        h_ref, y_ref = refs[3 + nr], refs[4 + nr]
        ride_out = refs[5 + nr:5 + 2 * nr]
        h_scr = refs[5 + 2 * nr]
        ride_sems = refs[6 + 2 * nr:]
        i, j = pl.program_id(0), pl.program_id(1)
        if ride:
            @pl.when((i == 0) & (j == 0))
            def _():
                ride.start(ride_in, ride_out, ride_sems)

        @pl.when(j == 0)
        def _():
            h = _rms_norm(x_ref[...], g_ref[...]).astype(MXU_DTYPE)
            h_scr[...] = h
            h_ref[...] = h

        y_ref[...] = _dot(h_scr[...], w_ref[...], (((1,), (1,)), ((), ())))

        if ride:
            @pl.when((i == gi - 1) & (j == gj - 1))
            def _():
                ride.finish(ride_in, ride_out, ride_sems)

    hbm = pl.BlockSpec(memory_space=pltpu.HBM)
    res = pl.pallas_call(
        body, name=name, grid=(gi, gj),
        in_specs=[pl.BlockSpec((tm, d), lambda i, j: (i, 0)), pl.BlockSpec((1, d), lambda i, j: (0, 0)),
                  pl.BlockSpec((tn, d), lambda i, j: (j, 0))] + [hbm] * nr,
        out_specs=[pl.BlockSpec((tm, d), lambda i, j: (i, 0)), pl.BlockSpec((tm, tn), lambda i, j: (i, j))] + [hbm] * nr,
        out_shape=[jax.ShapeDtypeStruct((t, d), MXU_DTYPE), jax.ShapeDtypeStruct((t, n), F32)]
        + (ride.out_shape if ride else []),
        scratch_shapes=[pltpu.VMEM((tm, d), MXU_DTYPE)] + (ride.scratch if ride else []),
        compiler_params=pltpu.CompilerParams(
            dimension_semantics=("arbitrary",) * 2 if ride else ("parallel", "arbitrary"),
            has_side_effects=bool(ride)),
    )(x, g, w, *(ride.arrays if ride else ()))
    return (res[0], res[1], res[2:]) if ride else (res[0], res[1])


def _swiglu_math(g, u):
    return jax.nn.silu(g) * u


def _gu_tile(dff):
    return _pick(dff, (1408, 1024, 512, 256, 128))


def _interleave_gu(gate_t, up_t):
    tn = _gu_tile(gate_t.shape[0])
    pieces = []
    for j in range(gate_t.shape[0] // tn):
        pieces += [gate_t[j * tn:(j + 1) * tn], up_t[j * tn:(j + 1) * tn]]
    return jnp.concatenate(pieces, axis=0)


def _split_gu(gu_t):
    dff = gu_t.shape[0] // 2
    tn = _gu_tile(dff)
    tiles = [gu_t[j * tn:(j + 1) * tn] for j in range(2 * dff // tn)]
    return jnp.concatenate(tiles[0::2], axis=0), jnp.concatenate(tiles[1::2], axis=0)


def _ffn_up_swiglu(x, g, w_gu, name, ride=None):
    t, d = x.shape
    dff = w_gu.shape[0] // 2
    tn = _gu_tile(dff)
    tm = _pick(t, (512, 256, 128))
    gj, gi = dff // tn, t // tm
    nr = ride.na if ride else 0

    def body(*refs):
        x_ref, g_ref, w_ref = refs[:3]
        ride_in = refs[3:3 + nr]
        h_ref, gu_ref, act_ref = refs[3 + nr:6 + nr]
        ride_out = refs[6 + nr:6 + 2 * nr]
        ride_sems = refs[6 + 2 * nr:]
        j, i = pl.program_id(0), pl.program_id(1)
        if ride:
            @pl.when((i == 0) & (j == 0))
            def _():
                ride.start(ride_in, ride_out, ride_sems)

        h = _rms_norm(x_ref[...], g_ref[...]).astype(MXU_DTYPE)

        @pl.when(j == 0)
        def _():
            h_ref[...] = h

        y = _dot(h, w_ref[...], (((1,), (1,)), ((), ())))
        gu_ref[...] = y.astype(gu_ref.dtype)
        act_ref[...] = _swiglu_math(y[:, :tn], y[:, tn:]).astype(act_ref.dtype)

        if ride:
            @pl.when((i == gi - 1) & (j == gj - 1))
            def _():
                ride.finish(ride_in, ride_out, ride_sems)

    hbm = pl.BlockSpec(memory_space=pltpu.HBM)
    res = pl.pallas_call(
        body, name=name, grid=(gj, gi),
        in_specs=[pl.BlockSpec((tm, d), lambda j, i: (i, 0)), pl.BlockSpec((1, d), lambda j, i: (0, 0)),
                  pl.BlockSpec((2 * tn, d), lambda j, i: (j, 0))] + [hbm] * nr,
        out_specs=[pl.BlockSpec((tm, d), lambda j, i: (jnp.where(j == 0, i, gi - 1), 0)),
                   pl.BlockSpec((tm, 2 * tn), lambda j, i: (i, j)),
                   pl.BlockSpec((tm, tn), lambda j, i: (i, j))] + [hbm] * nr,
        out_shape=[jax.ShapeDtypeStruct((t, d), MXU_DTYPE), jax.ShapeDtypeStruct((t, 2 * dff), MXU_DTYPE),
                   jax.ShapeDtypeStruct((t, dff), MXU_DTYPE)] + (ride.out_shape if ride else []),
        scratch_shapes=(ride.scratch if ride else []),
        compiler_params=pltpu.CompilerParams(dimension_semantics=("arbitrary", "arbitrary"),
                                             has_side_effects=bool(ride)),
    )(x, g, w_gu, *(ride.arrays if ride else ()))
    return (res[0], res[1], res[2], res[3:]) if ride else tuple(res)


def _ffn_down_dx_swiglu(dx_out, w_down, gu, name):
    t, d = dx_out.shape
    dff = w_down.shape[0]
    tn = _gu_tile(dff)
    tm = _pick(t, (512, 256, 128))

    def body(dx_ref, w_ref, gu_ref, o_ref):
        dact = _dot(dx_ref[...], w_ref[...], (((1,), (1,)), ((), ())))
        gu_v = gu_ref[...].astype(F32)
        _, vjp = jax.vjp(_swiglu_math, gu_v[:, :tn], gu_v[:, tn:])
        dg, du = vjp(dact)
        o_ref[...] = jnp.concatenate([dg, du], axis=1).astype(o_ref.dtype)

    return pl.pallas_call(
        body, name=name, grid=(dff // tn, t // tm),
        in_specs=[pl.BlockSpec((tm, d), lambda j, i: (i, 0)), pl.BlockSpec((tn, d), lambda j, i: (j, 0)),
                  pl.BlockSpec((tm, 2 * tn), lambda j, i: (i, j))],
        out_specs=pl.BlockSpec((tm, 2 * tn), lambda j, i: (i, j)),
        out_shape=jax.ShapeDtypeStruct((t, 2 * dff), MXU_DTYPE),
        compiler_params=_params("parallel", "parallel"),
    )(dx_out, w_down, gu)


def _matmul_norm_bwd(dy, w, x, g, dres, name, ride=None, w_rows_k=False):
    t, k = dy.shape
    d = w.shape[1] if w_rows_k else w.shape[0]
    tm = _pick(t, (1024, 512, 256, 128))
    tk = k if k <= 2816 else _pick(k, (1408, 1280, 1024, 768, 512))
    tr = _pick(tm, (256, 128))
    gi, nk = t // tm, k // tk
    nr = ride.na if ride else 0

    def body(*refs):
        dy_ref, w_ref, x_ref, g_ref, dres_ref = refs[:5]
        ride_in = refs[5:5 + nr]
        dx_ref, dg_ref = refs[5 + nr], refs[6 + nr]
        ride_out = refs[7 + nr:7 + 2 * nr]
        acc = refs[7 + 2 * nr]
        ride_sems = refs[8 + 2 * nr:]
        i, q = pl.program_id(0), pl.program_id(1)

        @pl.when((i == 0) & (q == 0))
        def _():
            dg_ref[...] = jnp.zeros_like(dg_ref)
            if ride:
                ride.start(ride_in, ride_out, ride_sems)

        @pl.when(q == 0)
        def _():
            acc[...] = jnp.zeros_like(acc)

        acc[...] += _dot(dy_ref[...], w_ref[...], (((1,), (0 if w_rows_k else 1,)), ((), ())))

        @pl.when(q == nk - 1)
        def _():
            for r in range(tm // tr):
                rows = slice(r * tr, (r + 1) * tr)
                _, vjp = jax.vjp(_rms_norm, x_ref[rows], g_ref[...])
                dxn, dg = vjp(acc[rows])
                dx_ref[rows] = dres_ref[rows] + dxn
                dg_ref[...] += dg

        if ride:
            @pl.when((i == gi - 1) & (q == nk - 1))
            def _():
                ride.finish(ride_in, ride_out, ride_sems)

    hbm = pl.BlockSpec(memory_space=pltpu.HBM)
    row = pl.BlockSpec((tm, d), lambda i, q: (i, 0))
    res = pl.pallas_call(
        body, name=name, grid=(gi, nk),
        in_specs=[pl.BlockSpec((tm, tk), lambda i, q: (i, q)),
                  pl.BlockSpec((tk, d), lambda i, q: (q, 0)) if w_rows_k else pl.BlockSpec((d, tk), lambda i, q: (0, q)),
                  row,
                  pl.BlockSpec((1, d), lambda i, q: (0, 0)), row] + [hbm] * nr,
        out_specs=[row, pl.BlockSpec((1, d), lambda i, q: (0, 0))] + [hbm] * nr,
        out_shape=[jax.ShapeDtypeStruct((t, d), F32), jax.ShapeDtypeStruct((1, d), F32)]
        + (ride.out_shape if ride else []),
        scratch_shapes=[pltpu.VMEM((tm, d), F32)] + (ride.scratch if ride else []),
        compiler_params=pltpu.CompilerParams(dimension_semantics=("arbitrary", "arbitrary"),
                                             has_side_effects=bool(ride)),
    )(dy, w, x, g, dres, *(ride.arrays if ride else ()))
    return (res[0], res[1], res[2:]) if ride else (res[0], res[1])


def _tok_call(name, fn, t, tt, tok_in, const_in, tok_out, acc_out, carry=()):
    nblk = t // tt
    hb = tt // HALO
    blk = (lambda i: nblk - 1 - i) if carry else (lambda i: i)
    in_specs, args = [], []
    for arr, w, cb, mode in tok_in:
        if mode == "cur":
            spec = pl.BlockSpec((tt, w), lambda i, cb=cb: (blk(i), cb))
        elif mode == "prev":
            spec = pl.BlockSpec((HALO, w), lambda i, cb=cb: (jnp.maximum(blk(i) * hb - 1, 0), cb))
        else:
            spec = pl.BlockSpec((HALO, w), lambda i, cb=cb: (jnp.minimum(blk(i) + 1, nblk - 1), cb))
        in_specs.append(spec)
        args.append(arr)
    for arr in const_in:
        in_specs.append(pl.BlockSpec(arr.shape, lambda i: (0, 0)))
        args.append(arr)
    out_specs, out_shape = [], []
    for rows, w, dt in tok_out:
        out_specs.append(pl.BlockSpec((rows, w), lambda i: (blk(i), 0)))
        out_shape.append(jax.ShapeDtypeStruct((nblk * rows, w), dt))
    for shp, dt in acc_out:
        out_specs.append(pl.BlockSpec(shp, lambda i: (0, 0)))
        out_shape.append(jax.ShapeDtypeStruct(shp, dt))
    n_tok, n_const, n_out, n_acc = len(tok_in), len(const_in), len(tok_out), len(acc_out)

    def body(*refs):
        i = pl.program_id(0)
        b = blk(i)
        tok_vals = []
        for (_, _, _, mode), r in zip(tok_in, refs[:n_tok]):
            v = r[...]
            if mode == "prev":
                v = jnp.where(b > 0, v, jnp.zeros_like(v))
            elif mode == "next8":
                v = jnp.where(b < nblk - 1, v, jnp.zeros_like(v))
            tok_vals.append(v)
        const_vals = [r[...] for r in refs[n_tok:n_tok + n_const]]
        o_refs = refs[n_tok + n_const:n_tok + n_const + n_out]
        a_refs = refs[n_tok + n_const + n_out:n_tok + n_const + n_out + n_acc]
        c_refs = refs[n_tok + n_const + n_out + n_acc:]
        if a_refs or c_refs:
            @pl.when(i == 0)
            def _():
                for r in (*a_refs, *c_refs):
                    r[...] = jnp.zeros_like(r)

        if c_refs:
            outs, accs, carried = fn(tok_vals, const_vals, [r[...] for r in c_refs])
            for r, v in zip(c_refs, carried):
                r[...] = v
        else:
            outs, accs = fn(tok_vals, const_vals)
        for r, v in zip(o_refs, outs):
            r[...] = v.astype(r.dtype)
        for r, v in zip(a_refs, accs):
            r[...] += v.astype(r.dtype)

    res = pl.pallas_call(
        body, name=name, grid=(nblk,), in_specs=in_specs, out_specs=out_specs, out_shape=out_shape,
        scratch_shapes=[pltpu.VMEM(shp, dt) for shp, dt in carry],
        compiler_params=_params("arbitrary" if acc_out or carry else "parallel"),
    )(*args)
    return res


def _dn_pre_math(cur, halo, ba, cw, alog, dtb):
    tt = cur.shape[0]
    a = jax.nn.silu(_causal_conv(cur, halo, cw, DN_CONV))
    pieces = []
    for p in range(2 * DN_HEADS):
        xh = a[:, p * HEAD_DIM:(p + 1) * HEAD_DIM]
        xh = xh * lax.rsqrt(jnp.sum(xh * xh, axis=-1, keepdims=True) + EPS)
        if p < DN_HEADS:
            xh = xh * (HEAD_DIM ** -0.5)
        pieces.append(xh)
    pieces.append(a[:, 2 * DN_WIDTH:])
    qkvn = jnp.concatenate(pieces, axis=1)
    lane = lax.broadcasted_iota(jnp.int32, ba.shape, 1)
    raw = jnp.where(lane < DN_HEADS, jax.nn.sigmoid(ba), -jnp.exp(alog) * jax.nn.softplus(ba + dtb))
    r = lax.broadcasted_iota(jnp.int32, (tt, tt), 0)
    c = lax.broadcasted_iota(jnp.int32, (tt, tt), 1)
    tri = jnp.where((r // CHUNK == c // CHUNK) & (c <= r), 1.0, 0.0).astype(F32)
    cums = _ein("ij,jk->ik", LHS_EXACT, tri, raw)
    bg = jnp.where(lane < DN_HEADS, raw, cums)
    return qkvn, bg


def _mix_math(o, z, gb, gc, gc_halo, hv, hv_halo, dng, scg, scw):
    outs = []
    for h in range(DN_HEADS):
        sl = slice(h * HEAD_DIM, (h + 1) * HEAD_DIM)
        oh = o[:, sl]
        outs.append(oh * lax.rsqrt(jnp.mean(oh * oh, axis=-1, keepdims=True) + EPS) * dng * jax.nn.silu(z[:, sl]))
    y = gb * _causal_conv(gc * hv, gc_halo * hv_halo, scw, SC_CONV)
    gw = SC_WIDTH // SC_GROUPS
    for g in range(SC_GROUPS):
        sl = slice(g * gw, (g + 1) * gw)
        yg = y[:, sl]
        outs.append(yg * lax.rsqrt(jnp.mean(yg * yg, axis=-1, keepdims=True) + EPS) * scg[:, sl])
    return jnp.concatenate(outs, axis=1)


def _tri_inverse(a):
    c = a.shape[-1]
    r = lax.broadcasted_iota(jnp.int32, (c, c), 0)
    q = lax.broadcasted_iota(jnp.int32, (c, c), 1)
    eye = jnp.where(r == q, 1.0, 0.0).astype(F32)[None]
    blk = (r // 16 == q // 16)[None]
    d = jnp.where(blk, a, 0.0)
    o = a - d
    mm = functools.partial(_ein, "bij,bjk->bik", PRECISE)
    p = eye - d
    n = mm(d, d)
    for _ in range(2):
        both = mm(jnp.concatenate([n, p], axis=1), n)
        n = both[:, :c]
        p = p + both[:, c:]
    p = p + mm(p, n)
    e = mm(p, o)
    e2 = mm(e, e)
    left = eye - e + e2 - mm(e, e2)
    return mm(left, p)


@jax.custom_vjp
def _inverse_known(a, tinv):
    return tinv


def _inverse_known_fwd(a, tinv):
    return tinv, tinv


def _inverse_known_bwd(tinv, ct):
    left = _ein("bji,bjk->bik", PRECISE, tinv, ct)
    return -_ein("bik,bjk->bij", PRECISE, left, tinv), jnp.zeros_like(tinv)


_inverse_known.defvjp(_inverse_known_fwd, _inverse_known_bwd)


def _delta_intra_math(q, k, v, bg, head, tinv_known=None):
    n = q.shape[0]
    nb = n // CHUNK
    lane = lax.broadcasted_iota(jnp.int32, bg.shape, 1)
    beta = jnp.sum(jnp.where(lane == head, bg, 0.0), axis=1, keepdims=True).reshape(nb, CHUNK, 1)
    gc = jnp.sum(jnp.where(lane == head + DN_HEADS, bg, 0.0), axis=1, keepdims=True).reshape(nb, CHUNK, 1)
    q3, k3, v3 = (a.reshape(nb, CHUNK, HEAD_DIM) for a in (q, k, v))
    r = lax.broadcasted_iota(jnp.int32, (CHUNK, CHUNK), 0)
    c = lax.broadcasted_iota(jnp.int32, (CHUNK, CHUNK), 1)
    eye = jnp.where(r == c, 1.0, 0.0).astype(F32)[None]
    gcr = _ein("bik,bkj->bij", LHS_EXACT, jnp.ones((nb, CHUNK, CHUNK), F32), gc * eye)
    decay = jnp.exp(jnp.where((r >= c)[None], gc - gcr, -1e30))
    kb = k3 * beta
    vb = v3 * beta
    egc = jnp.exp(gc)
    on_k = _ein("bcd,bmd->bcm", FAST, jnp.concatenate([kb, q3], axis=1), k3)
    a = jnp.where((r > c)[None], on_k[:, :CHUNK] * decay, 0.0)
    tinv = _tri_inverse(a) if tinv_known is None else _inverse_known(a, tinv_known)
    uw = _ein("bcm,bmd->bcd", PRECISE, tinv, jnp.concatenate([vb, kb * egc], axis=2))
    u, w = uw[:, :, :HEAD_DIM], uw[:, :, HEAD_DIM:]
    qk = on_k[:, CHUNK:] * decay
    row = lax.broadcasted_iota(jnp.int32, (nb, CHUNK, 1), 1)
    glast = jnp.sum(jnp.where(row == CHUNK - 1, gc, 0.0), axis=1, keepdims=True)
    qd = q3 * egc
    kd = k3 * jnp.exp(glast - gc)
    glb = jnp.broadcast_to(jnp.exp(glast), (nb, HALO, LANES))
    flat = lambda x: x.reshape(n, HEAD_DIM)
    return flat(u), flat(w), flat(qd), flat(kd), qk, glb, tinv


def _delta_step_math(u, w, qd, kd, qk, gl, s):
    c = u.shape[0]
    on_s = _ein("ck,kv->cv", FAST, jnp.concatenate([w, qd], axis=0), s)
    vnew = u - on_s[:c]
    on_v = _ein("cm,mv->cv", FAST, jnp.concatenate([qk, kd.T], axis=0), vnew)
    o = on_s[c:] + on_v[:c]
    s2 = s * gl + on_v[c:]
    return o, s2


def _delta_intra(qkvn, bg, nb):
    t = qkvn.shape[0]
    n = t // CHUNK
    rows = nb * CHUNK

    def body(q_ref, k_ref, v_ref, bg_ref, u_ref, w_ref, qd_ref, kd_ref, qk_ref, gl_ref, ti_ref):
        outs = _delta_intra_math(q_ref[...], k_ref[...], v_ref[...], bg_ref[...], pl.program_id(1))
        for r, v in zip((u_ref, w_ref, qd_ref, kd_ref, qk_ref, gl_ref, ti_ref), outs):
            r[...] = v

    col = lambda off: pl.BlockSpec((rows, HEAD_DIM), lambda b, h, off=off: (b, off + h))
    tok = jax.ShapeDtypeStruct((t, DN_WIDTH), F32)
    return pl.pallas_call(
        body, name="delta_intra", grid=(n // nb, DN_HEADS),
        in_specs=[col(0), col(DN_HEADS), col(2 * DN_HEADS), pl.BlockSpec((rows, LANES), lambda b, h: (b, 0))],
        out_specs=[col(0)] * 4 + [pl.BlockSpec((nb, None, CHUNK, CHUNK), lambda b, h: (b, h, 0, 0)),
                                  pl.BlockSpec((nb, None, HALO, LANES), lambda b, h: (b, h, 0, 0)),
                                  pl.BlockSpec((nb, None, CHUNK, CHUNK), lambda b, h: (b, h, 0, 0))],
        out_shape=[tok] * 4 + [jax.ShapeDtypeStruct((n, DN_HEADS, CHUNK, CHUNK), F32),
                               jax.ShapeDtypeStruct((n, DN_HEADS, HALO, LANES), F32),
                               jax.ShapeDtypeStruct((n, DN_HEADS, CHUNK, CHUNK), F32)],
        compiler_params=_params("parallel", "arbitrary"),
    )(qkvn, qkvn, qkvn, bg)


def _delta_intra_bwd(qkvn, bg, tinv, cts, nb):
    t = qkvn.shape[0]
    n = t // CHUNK
    rows = nb * CHUNK

    def body(q_ref, k_ref, v_ref, bg_ref, ti_ref, du, dw, dqd, dkd, dqk, dgl, dq_ref, dk_ref, dv_ref, dbg_ref):
        h = pl.program_id(1)
        ti = ti_ref[...]
        _, vjp = jax.vjp(lambda q, k, v, b: _delta_intra_math(q, k, v, b, h, ti)[:6],
                         q_ref[...], k_ref[...], v_ref[...], bg_ref[...])
        dq, dk, dv, dbg = vjp((du[...], dw[...], dqd[...], dkd[...], dqk[...], dgl[...]))
        dq_ref[...] = dq
        dk_ref[...] = dk
        dv_ref[...] = dv

        @pl.when(h == 0)
        def _():
            dbg_ref[...] = jnp.zeros_like(dbg_ref)

        dbg_ref[...] += dbg

    col = lambda off: pl.BlockSpec((rows, HEAD_DIM), lambda b, h, off=off: (b, off + h))
    bgs = pl.BlockSpec((rows, LANES), lambda b, h: (b, 0))
    qks = pl.BlockSpec((nb, None, CHUNK, CHUNK), lambda b, h: (b, h, 0, 0))
    gls = pl.BlockSpec((nb, None, HALO, LANES), lambda b, h: (b, h, 0, 0))
    tok = jax.ShapeDtypeStruct((t, DN_WIDTH), F32)
    return pl.pallas_call(
        body, name="delta_intra_bwd", grid=(n // nb, DN_HEADS),
        in_specs=[col(0), col(DN_HEADS), col(2 * DN_HEADS), bgs, qks, col(0), col(0), col(0), col(0), qks, gls],
        out_specs=[col(0), col(0), col(0), bgs],
        out_shape=[tok, tok, tok, jax.ShapeDtypeStruct((t, LANES), F32)],
        compiler_params=_params("parallel", "arbitrary"),
    )(qkvn, qkvn, qkvn, bg, tinv, *cts)


def _delta_scan(u, w, qd, kd, qk, glb, cb):
    t = u.shape[0]
    n = t // CHUNK
    rows = cb * CHUNK

    def body(u_ref, w_ref, qd_ref, kd_ref, qk_ref, gl_ref, o_ref, s_ref, s_scr):
        @pl.when(pl.program_id(0) == 0)
        def _():
            s_scr[...] = jnp.zeros_like(s_scr)

        def chunk(c, carry):
            r0 = pl.multiple_of(c * CHUNK, CHUNK)
            for h in range(DN_HEADS):
                sl = (pl.ds(r0, CHUNK), slice(h * HEAD_DIM, (h + 1) * HEAD_DIM))
                s = s_scr[h]
                s_ref[c, h] = s
                o, s2 = _delta_step_math(u_ref[sl], w_ref[sl], qd_ref[sl], kd_ref[sl], qk_ref[c, h],
                                         gl_ref[c, h][0:1, :], s)
                o_ref[sl] = o
                s_scr[h] = s2
            return carry

        lax.fori_loop(0, cb, chunk, 0)

    tok = pl.BlockSpec((rows, DN_WIDTH), lambda i: (i, 0))
    return pl.pallas_call(
        body, name="delta_scan", grid=(n // cb,),
        in_specs=[tok] * 4 + [pl.BlockSpec((cb, DN_HEADS, CHUNK, CHUNK), lambda i: (i, 0, 0, 0)),
                              pl.BlockSpec((cb, DN_HEADS, HALO, LANES), lambda i: (i, 0, 0, 0))],
        out_specs=[tok, pl.BlockSpec((cb, DN_HEADS, HEAD_DIM, HEAD_DIM), lambda i: (i, 0, 0, 0))],
        out_shape=[jax.ShapeDtypeStruct((t, DN_WIDTH), F32),
                   jax.ShapeDtypeStruct((n, DN_HEADS, HEAD_DIM, HEAD_DIM), F32)],
        scratch_shapes=[pltpu.VMEM((DN_HEADS, HEAD_DIM, HEAD_DIM), F32)],
        compiler_params=_params("arbitrary"),
    )(u, w, qd, kd, qk, glb)


def _delta_scan_bwd(u, w, qd, kd, qk, glb, s_all, do, cb):
    t = u.shape[0]
    n = t // CHUNK
    nblk = n // cb
    rows = cb * CHUNK

    def body(u_ref, w_ref, qd_ref, kd_ref, qk_ref, gl_ref, s_ref, do_ref,
             du_ref, dw_ref, dqd_ref, dkd_ref, dqk_ref, dgl_ref, ds_scr):
        @pl.when(pl.program_id(0) == 0)
        def _():
            ds_scr[...] = jnp.zeros_like(ds_scr)

        def chunk(step, carry):
            c = cb - 1 - step
            r0 = pl.multiple_of(c * CHUNK, CHUNK)
            for h in range(DN_HEADS):
                sl = (pl.ds(r0, CHUNK), slice(h * HEAD_DIM, (h + 1) * HEAD_DIM))
                gl_tile = gl_ref[c, h]
                prim = (u_ref[sl], w_ref[sl], qd_ref[sl], kd_ref[sl], qk_ref[c, h], gl_tile, s_ref[c, h])
                _, vjp = jax.vjp(lambda a, b, cc, d, e, g, s: _delta_step_math(a, b, cc, d, e, g[0:1, :], s), *prim)
                du, dw, dqd, dkd, dqk, dgl, ds = vjp((do_ref[sl], ds_scr[h]))
                du_ref[sl] = du
                dw_ref[sl] = dw
                dqd_ref[sl] = dqd
                dkd_ref[sl] = dkd
                dqk_ref[c, h] = dqk
                dgl_ref[c, h] = dgl
                ds_scr[h] = ds
            return carry

        lax.fori_loop(0, cb, chunk, 0)

    rev = lambda i: nblk - 1 - i
    tok = pl.BlockSpec((rows, DN_WIDTH), lambda i: (rev(i), 0))
    qks = pl.BlockSpec((cb, DN_HEADS, CHUNK, CHUNK), lambda i: (rev(i), 0, 0, 0))
    gls = pl.BlockSpec((cb, DN_HEADS, HALO, LANES), lambda i: (rev(i), 0, 0, 0))
    ss = pl.BlockSpec((cb, DN_HEADS, HEAD_DIM, HEAD_DIM), lambda i: (rev(i), 0, 0, 0))
    tshape = jax.ShapeDtypeStruct((t, DN_WIDTH), F32)
    return pl.pallas_call(
        body, name="delta_scan_bwd", grid=(nblk,),
        in_specs=[tok] * 4 + [qks, gls, ss, tok],
        out_specs=[tok] * 4 + [qks, gls],
        out_shape=[tshape] * 4 + [jax.ShapeDtypeStruct(qk.shape, F32), jax.ShapeDtypeStruct(glb.shape, F32)],
        scratch_shapes=[pltpu.VMEM((DN_HEADS, HEAD_DIM, HEAD_DIM), F32)],
        compiler_params=_params("arbitrary"),
    )(u, w, qd, kd, qk, glb, s_all, do)


def _peer(mask):
    x, y, c = lax.axis_index("x"), lax.axis_index("y"), lax.axis_index("c")
    return (x ^ ((mask >> 2) & 1), y ^ ((mask >> 1) & 1), c ^ (mask & 1))


def _my_index():
    return 4 * lax.axis_index("x") + 2 * lax.axis_index("y") + lax.axis_index("c")


class _Exchange:
    CHIP_MASKS = (4, 2, 6)

    def __init__(self, kind, arrays):
        self.kind = kind
        self.arrays = list(arrays)
        self.na = na = len(self.arrays)
        if kind == "gather":
            self.out_shape = [jax.ShapeDtypeStruct((N_DEV,) + a.shape, a.dtype) for a in self.arrays]
        else:
            self.out_shape = [jax.ShapeDtypeStruct(a.shape, a.dtype) for a in self.arrays]
        self.scratch = [pltpu.SemaphoreType.DMA((na, 7)), pltpu.SemaphoreType.DMA((na, 7)),
                        pltpu.SemaphoreType.DMA((na,))]

    def _copies(self, ins, outs, sems):
        send_sems, recv_sems, local_sems = sems
        me = _my_index()
        local, first, passed, arrivals = [], [], [], []
        if self.kind == "gather":
            def rc(a, k, block, to, own=False):
                def make():
                    dst = outs[a].at[block]
                    return pltpu.make_async_remote_copy(src_ref=ins[a] if own else dst, dst_ref=dst,
                                                        send_sem=send_sems.at[a, k], recv_sem=recv_sems.at[a, k],
                                                        device_id=to, device_id_type=MESH)
                return make

            sib = _peer(1)
            for a in range(self.na):
                local.append(lambda a=a: pltpu.make_async_copy(ins[a], outs[a].at[me], local_sems.at[a]))
                first.append(rc(a, 0, me, sib, own=True))
                arrivals.append(rc(a, 0, me ^ 1, _peer(0)))
                for j, m in enumerate(self.CHIP_MASKS):
                    first.append(rc(a, 1 + j, me, _peer(m), own=True))
                    passed.append((rc(a, 1 + j, me ^ m, _peer(0)), rc(a, 4 + j, me ^ m, sib)))
                    arrivals.append(rc(a, 4 + j, me ^ m ^ 1, _peer(0)))
        else:
            for a in range(self.na):
                local.append(lambda a=a: pltpu.make_async_copy(ins[a].at[me], outs[a].at[me], local_sems.at[a]))
                for m in range(1, N_DEV):
                    def make(a=a, m=m):
                        return pltpu.make_async_remote_copy(
                            src_ref=ins[a].at[me ^ m], dst_ref=outs[a].at[me], send_sem=send_sems.at[a, m - 1],
                            recv_sem=recv_sems.at[a, m - 1], device_id=_peer(m), device_id_type=MESH)
                    first.append(make)
                    arrivals.append(make)
        return local, first, passed, arrivals

    def start(self, ins, outs, sems):
        local, first, _, _ = self._copies(ins, outs, sems)
        for make in local + first:
            make().start()

    def finish(self, ins, outs, sems):
        local, first, passed, arrivals = self._copies(ins, outs, sems)
        for landed, onward in passed:
            landed().wait_recv()
            onward().start()
        for make in arrivals:
            make().wait_recv()
        for make in first + [p for _, p in passed]:
            make().wait_send()
        for make in local:
            make().wait()

    def run(self, name):
        na = self.na

        def body(*refs):
            ins, outs, sems = refs[:na], refs[na:2 * na], refs[2 * na:]
            self.start(ins, outs, sems)
            self.finish(ins, outs, sems)

        hbm = pl.BlockSpec(memory_space=pltpu.HBM)
        return pl.pallas_call(
            body, name=name, in_specs=[hbm] * na, out_specs=[hbm] * na, out_shape=self.out_shape,
            scratch_shapes=self.scratch, compiler_params=pltpu.CompilerParams(has_side_effects=True),
        )(*self.arrays)


def _all_gather(shards, name):
    return _Exchange("gather", shards).run(name)


def _adamw_math(w, g, m, v):
    m2 = ADAM_B1 * m + (1.0 - ADAM_B1) * g
    v2 = ADAM_B2 * v + (1.0 - ADAM_B2) * jnp.square(g)
    m_hat = m2 / (1.0 - ADAM_B1 ** ADAM_STEP)
    v_hat = v2 / (1.0 - ADAM_B2 ** ADAM_STEP)
    delta = -ADAM_LR * (m_hat / (jnp.sqrt(v_hat) + ADAM_EPS) + ADAM_WD * w)
    return delta, m2, v2


def _sum_adamw(parts, w, m, v, name):
    r, c = w.shape
    tr = _pick(r, (512, 256, 352, 128, 64, 32, 16, 8))
    np_ = parts.shape[0]

    def body(p_ref, w_ref, m_ref, v_ref, g_ref, d_ref, m2_ref, v2_ref):
        g = p_ref[0].astype(F32)
        for d in range(1, np_):
            g = g + p_ref[d].astype(F32)
        delta, m2, v2 = _adamw_math(w_ref[...], g, m_ref[...], v_ref[...])
        g_ref[...] = g
        d_ref[...] = delta
        m2_ref[...] = m2
        v2_ref[...] = v2

    blk = pl.BlockSpec((tr, c), lambda i: (i, 0))
    shp = jax.ShapeDtypeStruct((r, c), F32)
    return pl.pallas_call(
        body, name=name, grid=(r // tr,),
        in_specs=[pl.BlockSpec((np_, tr, c), lambda i: (0, i, 0)), blk, blk, blk],
        out_specs=[blk] * 4, out_shape=[shp] * 4, compiler_params=_params("parallel"),
    )(parts, w, m, v)


def _sum_rows(parts, name):
    np_, r, c = parts.shape

    def body(p_ref, o_ref):
        g = p_ref[0]
        for d in range(1, np_):
            g = g + p_ref[d]
        o_ref[...] = g

    return pl.pallas_call(body, name=name, out_shape=jax.ShapeDtypeStruct((r, c), F32))(parts)


def _pad_w_in(w):
    d = w.shape[1]
    n_ba = 2 * DN_HEADS
    a = w[:SC_OFF]
    ba = w[SC_OFF:SC_OFF + n_ba]
    sc = w[SC_OFF + n_ba:]
    return jnp.concatenate([a, sc, ba, jnp.zeros((BA_W - n_ba, d), w.dtype)], axis=0)


def _unpad_w_in(wp):
    n_ba = 2 * DN_HEADS
    return jnp.concatenate([wp[:SC_OFF], wp[BA_OFF:BA_OFF + n_ba], wp[SC_OFF:BA_OFF]], axis=0)


def _lane_row(v, off):
    return jnp.pad(v.astype(F32), (off, LANES - off - v.shape[0]))[None]


TT = 256
NB_INTRA = 32
CB_SCAN = 8


def _layer_fwd(x, p, ride_proj, ride_ffn, late):
    t, d = x.shape
    got_proj = got_ffn = None
    if ride_proj:
        h, proj, got_proj = _norm_matmul(x, p["norm1_g"], p["w_in"], "proj_fwd_gather", ride=ride_proj)
    else:
        h, proj = _norm_matmul(x, p["norm1_g"], p["w_in"], "proj_fwd")
    p = {**p, **late(got_proj)}
    qkvn, bg = _tok_call(
        "dn_pre", lambda tv, cv: (_dn_pre_math(*tv, *cv), ()), t, TT,
        [(proj, QKV_W, 0, "cur"), (proj, QKV_W, 0, "prev"), (proj, LANES, BA_OFF // LANES, "cur")],
        [p["dn_conv_w"], p["alog_row"], p["dt_row"]],
        [(TT, QKV_W, F32), (TT, LANES, F32)], [])
    u, w, qd, kd, qk, glb, tinv = _delta_intra(qkvn, bg, min(NB_INTRA, t // CHUNK))
    o, s_all = _delta_scan(u, w, qd, kd, qk, glb, CB_SCAN)
    cb0 = SC_OFF // SC_WIDTH
    mix_in = [(o, DN_WIDTH, 0, "cur"), (proj, DN_WIDTH, Z_OFF // DN_WIDTH, "cur"),
              (proj, SC_WIDTH, cb0, "cur"), (proj, SC_WIDTH, cb0 + 1, "cur"), (proj, SC_WIDTH, cb0 + 1, "prev"),
              (proj, SC_WIDTH, cb0 + 2, "cur"), (proj, SC_WIDTH, cb0 + 2, "prev")]
    mix_const = [p["dn_norm_g"], p["sc_norm_g"], p["sc_conv_w"]]
    (cat,) = _tok_call("mix_post", lambda tv, cv: ((_mix_math(*tv, *cv),), ()), t, TT,
                       mix_in, mix_const, [(TT, 2 * DN_WIDTH, MXU_DTYPE)], [])
    x_mid = _matmul(cat, p["w_out"], "nn", F32, "out_proj", residual=x)
    if ride_ffn:
        h2, gu, act, got_ffn = _ffn_up_swiglu(x_mid, p["norm2_g"], p["w_gu"], "ffn_up_gather", ride=ride_ffn)
    else:
        h2, gu, act = _ffn_up_swiglu(x_mid, p["norm2_g"], p["w_gu"], "ffn_up")
    x_out = _matmul(act, p["w_down"], "nn", F32, "ffn_down", residual=x_mid)
    saved = dict(x=x, h=h, proj=proj, qkvn=qkvn, bg=bg, u=u, w=w, qd=qd, kd=kd, qk=qk, glb=glb, tinv=tinv, s_all=s_all, o=o,
                 cat=cat, x_mid=x_mid, h2=h2, gu=gu, act=act, mix_in=mix_in, mix_const=mix_const)
    return x_out, saved, p, got_proj, got_ffn


def _layer_bwd(dx_out, p, s, ride_prev, scatter_of):
    t, d = dx_out.shape
    got = {}
    dgu = _ffn_down_dx_swiglu(dx_out, p["w_down"], s["gu"], "ffn_down_dx")
    d_w_down = _matmul(s["act"], dx_out, "tn", MXU_DTYPE, "ffn_down_dw")
    if ride_prev:
        dx_mid, d_norm2, got["prev"] = _matmul_norm_bwd(dgu, p["w_gu"], s["x_mid"], p["norm2_g"], dx_out,
                                                        "ffn_up_dx_scatter", ride=ride_prev, w_rows_k=True)
    else:
        dx_mid, d_norm2 = _matmul_norm_bwd(dgu, p["w_gu"], s["x_mid"], p["norm2_g"], dx_out, "ffn_up_dx",
                                           w_rows_k=True)
    d_w_gu, got["down"] = _matmul(dgu, s["h2"], "tn", MXU_DTYPE, "ffn_up_dw_scatter", ride=scatter_of(d_w_down))
    d_gate, d_up = _split_gu(d_w_gu)
    dcat = _matmul(dx_mid, p["w_out"], "nt", F32, "out_proj_dx")
    d_w_out = _matmul(s["cat"], dx_mid, "tn", MXU_DTYPE, "out_proj_dw")

    def mix_bwd(tv, cv):
        prim = tuple(tv[:7]) + tuple(cv)
        _, vjp = jax.vjp(_mix_math, *prim)
        do, dz, dgb, dgc, dgch, dhv, dhvh, ddng, dscg, dscw = vjp(tv[7])
        return (do, dz, dgb, dgc, dgch, dhv, dhvh), (ddng, dscg, dscw)

    wide = (TT, DN_WIDTH, F32)
    final = (TT, DN_WIDTH, MXU_DTYPE)
    halo = (HALO, SC_WIDTH, F32)
    do, dz, dgb, dgc, dgc_h, dhv, dhv_h, d_dn_norm, d_sc_norm, d_sc_conv = _tok_call(
        "mix_post_bwd", mix_bwd, t, TT, s["mix_in"] + [(dcat, 2 * DN_WIDTH, 0, "cur")], s["mix_const"],
        [wide, final, final, wide, halo, wide, halo],
        [((1, HEAD_DIM), F32), ((1, SC_WIDTH), F32), ((SC_CONV, SC_WIDTH), F32)])
    cts = _delta_scan_bwd(s["u"], s["w"], s["qd"], s["kd"], s["qk"], s["glb"], s["s_all"], do, CB_SCAN)
    dq, dk, dv, dbg = _delta_intra_bwd(s["qkvn"], s["bg"], s["tinv"], cts, min(NB_INTRA, t // CHUNK))
    proj = s["proj"]

    def with_halo(cur, nxt):
        return cur + jnp.concatenate([jnp.zeros((TT - HALO, cur.shape[1]), F32), nxt], axis=0)

    def dn_pre_bwd(tv, cv, carried):
        cur, hal, ba, dq_, dk_, dv_, dbg_, dz_, dgb_, dgc_, dgc_n, dhv_, dhv_n = tv
        _, vjp = jax.vjp(_dn_pre_math, cur, hal, ba, *cv)
        dcur, dhal, dba, dcw, dal, ddt = vjp((jnp.concatenate([dq_, dk_, dv_], axis=1), dbg_))
        dproj_rows = jnp.concatenate([with_halo(dcur, carried[0]), dz_.astype(F32), dgb_.astype(F32), with_halo(dgc_, dgc_n),
                                      with_halo(dhv_, dhv_n), dba, jnp.zeros((TT, BA_W - LANES), F32)], axis=1)
        return (dproj_rows,), (dcw, dal, ddt), (dhal,)

    dproj, d_dn_conv, d_alog, d_dt = _tok_call(
        "dn_pre_bwd", dn_pre_bwd, t, TT,
        [(proj, QKV_W, 0, "cur"), (proj, QKV_W, 0, "prev"), (proj, LANES, BA_OFF // LANES, "cur"),
         (dq, DN_WIDTH, 0, "cur"), (dk, DN_WIDTH, 0, "cur"), (dv, DN_WIDTH, 0, "cur"), (dbg, LANES, 0, "cur"),
         (dz, DN_WIDTH, 0, "cur"), (dgb, SC_WIDTH, 0, "cur"), (dgc, SC_WIDTH, 0, "cur"), (dgc_h, SC_WIDTH, 0, "next8"),
         (dhv, SC_WIDTH, 0, "cur"), (dhv_h, SC_WIDTH, 0, "next8")],
        [p["dn_conv_w"], p["alog_row"], p["dt_row"]],
        [(TT, PROJ_W, MXU_DTYPE)],
        [((DN_CONV, QKV_W), F32), ((1, LANES), F32), ((1, LANES), F32)],
        carry=[((HALO, QKV_W), F32)])
    dx_in, d_norm1, got["gate"] = _matmul_norm_bwd(dproj, p["w_in"], s["x"], p["norm1_g"], dx_mid,
                                                   "proj_dx_scatter", ride=scatter_of(d_gate), w_rows_k=True)
    d_w_in, got["up"] = _matmul(dproj, s["h"], "tn", MXU_DTYPE, "proj_dw_scatter", ride=scatter_of(d_up))
    grads = dict(w_in=d_w_in, w_out=d_w_out, w_gu=d_w_gu, w_down=d_w_down, norm1_g=d_norm1, norm2_g=d_norm2,
                 dn_norm_g=d_dn_norm, sc_norm_g=d_sc_norm, sc_conv_w=d_sc_conv, dn_conv_w=d_dn_conv,
                 alog=d_alog, dt=d_dt)
    return dx_in, grads, got


def _final_loss(x, g, target):
    t, d = x.shape

    def fn(tv, cv):
        xv, tg = tv

        def loss_fn(xx, gg):
            err = jnp.square(_rms_norm(xx, gg) - tg)
            return 0.5 * jnp.sum(jnp.mean(err, axis=-1))

        loss, vjp = jax.vjp(loss_fn, xv, cv[0])
        dx, dg = vjp(jnp.ones((), F32))
        return (dx,), (jnp.full((1, LANES), loss, F32), dg)

    return _tok_call("final_loss", fn, t, TT, [(x, d, 0, "cur"), (target, d, 0, "cur")], [g],
                     [(TT, d, F32)], [((1, LANES), F32), ((1, d), F32)])


def _pack_rows(arrs):
    rows, offs, r0 = [], [], 0
    for a in arrs:
        n = a.size
        nr = -(-n // LANES)
        flat = jnp.pad(a.reshape(-1).astype(F32), (0, nr * LANES - n))
        rows.append(flat.reshape(nr, LANES))
        offs.append((r0, nr, a.shape))
        r0 += nr
    pad = (-r0) % 8
    if pad:
        rows.append(jnp.zeros((pad, LANES), F32))
    return jnp.concatenate(rows, axis=0), offs


def _unpack_rows(packed, offs):
    out = []
    for r0, nr, shp in offs:
        n = 1
        for s_ in shp:
            n *= s_
        out.append(packed[r0:r0 + nr].reshape(-1)[:n].reshape(shp))
    return out


def kernel(x, norm1_g, w_in, dn_conv_w, dn_a_log, dn_dt_bias, dn_norm_g, sc_conv_w, sc_norm_g, w_out, norm2_g, ffn_w_gate, ffn_w_up, ffn_w_down, final_norm_g, loss_target, m_norm1_g, m_w_in, m_dn_conv_w, m_dn_a_log, m_dn_dt_bias, m_dn_norm_g, m_sc_conv_w, m_sc_norm_g, m_w_out, m_norm2_g, m_ffn_w_gate, m_ffn_w_up, m_ffn_w_down, m_final_norm_g, v_norm1_g, v_w_in, v_dn_conv_w, v_dn_a_log, v_dn_dt_bias, v_dn_norm_g, v_sc_conv_w, v_sc_norm_g, v_w_out, v_norm2_g, v_ffn_w_gate, v_ffn_w_up, v_ffn_w_down, v_final_norm_g):
    depth, d, cin = w_in.shape
    t = x.shape[1]
    dff_s = ffn_w_gate.shape[2]
    tr = lambda a: a.transpose(0, 2, 1)
    gate_t, up_t = tr(ffn_w_gate), tr(ffn_w_up)
    by_layer = lambda a: [a[l].T for l in range(depth)]
    win_t = by_layer(w_in)
    me = _my_index()
    x2 = x.reshape(t, d)
    tgt = loss_target.reshape(t, d)

    def shards(l):
        return [a[l].astype(MXU_DTYPE) for a in (win_t, w_out, gate_t, up_t, ffn_w_down)]

    conv_pack, conv_offs = _pack_rows([dn_conv_w, sc_conv_w])
    conv_all, *first = _all_gather([conv_pack] + shards(0)[:2], "gather_first")
    dn_parts, sc_parts = zip(*[_unpack_rows(conv_all[j], conv_offs) for j in range(N_DEV)])
    dn_conv_full = jnp.concatenate(dn_parts, axis=2)
    sc_conv_full = jnp.concatenate(sc_parts, axis=2)

    def mixer_params(l, g_in, g_out):
        return dict(
            w_in=_pad_w_in(g_in.reshape(N_DEV * cin, d)), w_out=g_out.reshape(d, d),
            norm1_g=norm1_g[l][None], norm2_g=norm2_g[l][None], dn_norm_g=dn_norm_g[l][None],
            sc_norm_g=sc_norm_g[l][None], dn_conv_w=dn_conv_full[l], sc_conv_w=sc_conv_full[l],
            alog_row=_lane_row(dn_a_log[l], DN_HEADS), dt_row=_lane_row(dn_dt_bias[l], DN_HEADS))

    def ffn_params(g_gate, g_up, g_down):
        dff = N_DEV * dff_s
        return dict(w_gu=_interleave_gu(g_gate.reshape(dff, d), g_up.reshape(dff, d)), w_down=g_down.reshape(dff, d))

    nxt = mixer_params(0, *first)
    nxt_ffn = None
    params, saved = [], []
    xc = x2
    for l in range(depth):
        last = l + 1 == depth
        own_ffn = shards(l)[2:] if nxt_ffn is None else []
        ahead = [] if last else shards(l + 1)
        ride_proj = _Exchange("gather", own_ffn + ahead[:2]) if own_ffn or ahead else None
        ride_ffn = None if last else _Exchange("gather", ahead[2:])

        def late(got, own_ffn=own_ffn, nxt_ffn=nxt_ffn):
            return ffn_params(*got[:3]) if own_ffn else nxt_ffn

        xc, s, p_l, got_proj, got_ffn = _layer_fwd(xc, nxt, ride_proj, ride_ffn, late)
        params.append(p_l)
        saved.append(s)
        if not last:
            nxt = mixer_params(l + 1, *got_proj[len(own_ffn):])
            nxt_ffn = ffn_params(*got_ffn)
    dx, loss_part, d_final = _final_loss(xc, final_norm_g[None], tgt)

    names = ("w_in", "w_out", "ffn_w_gate", "ffn_w_up", "ffn_w_down")
    big_out = {n: {k: [None] * depth for k in ("g", "d", "m", "v")} for n in names}
    w_loc = dict(w_in=win_t, w_out=w_out, ffn_w_gate=gate_t, ffn_w_up=up_t, ffn_w_down=ffn_w_down)
    m_loc = dict(w_in=by_layer(m_w_in), w_out=m_w_out, ffn_w_gate=tr(m_ffn_w_gate), ffn_w_up=tr(m_ffn_w_up), ffn_w_down=m_ffn_w_down)
    v_loc = dict(w_in=by_layer(v_w_in), w_out=v_w_out, ffn_w_gate=tr(v_ffn_w_gate), ffn_w_up=tr(v_ffn_w_up), ffn_w_down=v_ffn_w_down)


    def scatter_of(grad):
        return _Exchange("scatter", [grad.reshape(N_DEV, dff_s, d).astype(MXU_DTYPE)])

    def apply(l, which, recv):
        for n, r in zip(which, recv):
            res = _sum_adamw(r, w_loc[n][l], m_loc[n][l], v_loc[n][l], "adamw_" + n)
            for k, a in zip(("g", "d", "m", "v"), res):
                big_out[n][k][l] = a

    grads = [None] * depth
    pending = None
    for l in reversed(range(depth)):
        ride_prev = _Exchange("scatter", pending) if pending else None
        dx, grads[l], got = _layer_bwd(dx, params[l], saved[l], ride_prev, scatter_of)
        if ride_prev:
            apply(l + 1, names[:2], got["prev"])
        apply(l, names[2:], [*got["gate"], *got["up"], *got["down"]])
        pending = [_unpad_w_in(grads[l]["w_in"]).reshape(N_DEV, cin, d).astype(MXU_DTYPE),
                   grads[l]["w_out"].reshape(N_DEV, d // N_DEV, d).astype(MXU_DTYPE)]
    apply(0, names[:2], _Exchange("scatter", pending).run("scatter_last"))
    grad_x = dx.reshape(x.shape)
    big_out = {n: {k: jnp.stack(v_, axis=1 if n == "w_in" else 0) for k, v_ in o.items()} for n, o in big_out.items()}
    for n in ("ffn_w_gate", "ffn_w_up"):
        big_out[n] = {k: tr(a) for k, a in big_out[n].items()}
    big_out["w_in"] = {k: a.transpose(1, 2, 0) for k, a in big_out["w_in"].items()}

    stack = lambda key: jnp.stack([grads[l][key] for l in range(depth)])
    small_parts = [stack("norm1_g").reshape(depth, d), stack("norm2_g").reshape(depth, d), d_final.reshape(d),
                   stack("dn_norm_g").reshape(depth, HEAD_DIM), stack("sc_norm_g").reshape(depth, SC_WIDTH),
                   stack("alog").reshape(depth, LANES), stack("dt").reshape(depth, LANES),
                   stack("dn_conv_w"), stack("sc_conv_w"), loss_part]
    small_pack, small_offs = _pack_rows(small_parts)
    (small_all,) = _all_gather([small_pack], "gather_small")
    total = _sum_rows(small_all, "sum_small")
    (g_n1, g_n2, g_fin, g_dnn, g_scn, g_alog, g_dt, g_dnc, g_scc, loss_row) = _unpack_rows(total, small_offs)
    loss = loss_row[0, 0]
    g_alog = g_alog[:, DN_HEADS:2 * DN_HEADS]
    g_dt = g_dt[:, DN_HEADS:2 * DN_HEADS]
    dnc_w = dn_conv_w.shape[2]
    scc_w = sc_conv_w.shape[2]
    g_dnc = lax.dynamic_slice_in_dim(g_dnc, me * dnc_w, dnc_w, axis=2)
    g_scc = lax.dynamic_slice_in_dim(g_scc, me * scc_w, scc_w, axis=2)
    sm_g = [g_n1, g_dnc, g_alog, g_dt, g_dnn, g_scc, g_scn, g_n2, g_fin]
    sm_w = [norm1_g, dn_conv_w, dn_a_log, dn_dt_bias, dn_norm_g, sc_conv_w, sc_norm_g, norm2_g, final_norm_g]
    sm_m = [m_norm1_g, m_dn_conv_w, m_dn_a_log, m_dn_dt_bias, m_dn_norm_g, m_sc_conv_w, m_sc_norm_g, m_norm2_g, m_final_norm_g]
    sm_v = [v_norm1_g, v_dn_conv_w, v_dn_a_log, v_dn_dt_bias, v_dn_norm_g, v_sc_conv_w, v_sc_norm_g, v_norm2_g, v_final_norm_g]
    pg, offs = _pack_rows(sm_g)
    pw, _ = _pack_rows(sm_w)
    pm, _ = _pack_rows(sm_m)
    pv, _ = _pack_rows(sm_v)
    sg, sd, sm_, sv = _sum_adamw(pg[None], pw, pm, pv, "adamw_small")
    small_out = {k: _unpack_rows(a, offs) for k, a in zip(("g", "d", "m", "v"), (sg, sd, sm_, sv))}

    def outputs(k):
        s_ = small_out[k]
        b = big_out
        return [s_[0], b["w_in"][k], s_[1], s_[2], s_[3], s_[4], s_[5], s_[6], b["w_out"][k], s_[7],
                b["ffn_w_gate"][k], b["ffn_w_up"][k], b["ffn_w_down"][k], s_[8]]

    return (loss, grad_x, *outputs("g"), *outputs("d"), *outputs("m"), *outputs("v"))
```

```python
import functools

import jax
import jax.numpy as jnp
from jax import lax
from jax.experimental import pallas as pl
from jax.experimental.pallas import tpu as pltpu

F32 = jnp.float32
MXU_DTYPE = jnp.bfloat16
MESH = pl.DeviceIdType.MESH

N_DEV = 8
EPS = 1e-6
DN_HEADS = 4
HEAD_DIM = 128
DN_WIDTH = DN_HEADS * HEAD_DIM
SC_WIDTH = 512
SC_GROUPS = 4
DN_CONV = 4
SC_CONV = 3
CHUNK = 64
HALO = 8
LANES = 128

QKV_W = 3 * DN_WIDTH
Z_OFF = QKV_W
SC_OFF = Z_OFF + DN_WIDTH
BA_OFF = SC_OFF + 3 * SC_WIDTH
BA_W = 256
PROJ_W = BA_OFF + BA_W

ADAM_LR = 0.001
ADAM_B1 = 0.9
ADAM_B2 = 0.999
ADAM_EPS = 1e-08
ADAM_WD = 0.01
ADAM_STEP = 10


def _pick(n, cands):
    for c in cands:
        if n % c == 0:
            return c
    return n


def _params(*sem):
    return pltpu.CompilerParams(dimension_semantics=sem)


def _rms_norm(x, g):
    return x * lax.rsqrt(jnp.mean(x * x, axis=-1, keepdims=True) + EPS) * g


def _dot(a, b, dims=(((1,), (0,)), ((), ()))):
    return lax.dot_general(a.astype(MXU_DTYPE), b.astype(MXU_DTYPE), dims, preferred_element_type=F32)


def _split_terms(x, terms):
    out = []
    for _ in range(terms):
        hi = x.astype(MXU_DTYPE)
        out.append(hi)
        x = x - hi.astype(F32)
    return out


def _ein_impl(spec, terms, a, b):
    ta, tb = terms
    if ta == 1 and tb == 1:
        return jnp.einsum(spec, a.astype(MXU_DTYPE), b.astype(MXU_DTYPE), preferred_element_type=F32)
    pa, pb = _split_terms(a, ta), _split_terms(b, tb)
    order = max(ta, tb) - 1
    acc = None
    for deg in range(order, -1, -1):
        for i in range(ta):
            j = deg - i
            if 0 <= j < tb:
                t = jnp.einsum(spec, pa[i], pb[j], preferred_element_type=F32)
                acc = t if acc is None else acc + t
    return acc


@functools.partial(jax.custom_vjp, nondiff_argnums=(0, 1))
def _ein(spec, terms, a, b):
    return _ein_impl(spec, terms, a, b)


def _ein_fwd(spec, terms, a, b):
    return _ein_impl(spec, terms, a, b), (a, b)


def _ein_bwd(spec, terms, res, ct):
    a, b = res
    xy, z = spec.split("->")
    x, y = xy.split(",")
    tc = min(max(terms), 2)
    da = _ein_impl(f"{z},{y}->{x}", (tc, terms[1]), ct, b)
    db = _ein_impl(f"{x},{z}->{y}", (terms[0], tc), a, ct)
    return da, db


_ein.defvjp(_ein_fwd, _ein_bwd)

FAST = (1, 1)
PRECISE = (2, 2)
LHS_EXACT = (1, 3)


def _causal_conv(cur, halo, w, k):
    tt = cur.shape[0]
    xp = jnp.concatenate([halo, cur], axis=0)
    y = None
    for j in range(k):
        start = HALO - (k - 1) + j
        term = xp[start:start + tt] * w[j:j + 1]
        y = term if y is None else y + term
    return y


def _matmul(a, b, mode, out_dtype, name, residual=None, ride=None):
    if mode == "nn":
        (m, k), (k2, n) = a.shape, b.shape
    elif mode == "nt":
        (m, k), (n, k2) = a.shape, b.shape
    else:
        (k, m), (k2, n) = a.shape, b.shape
    assert k == k2
    tm = _pick(m, (1024, 1408, 1280, 512, 256, 128))
    tn = _pick(n, (1280, 1408, 1024, 512, 256, 128))
    tk = k if k <= 2816 else _pick(k, (2048, 1024, 768, 512))
    gi, gj, nk = m // tm, n // tn, k // tk
    dims = {"nn": (((1,), (0,)), ((), ())), "nt": (((1,), (1,)), ((), ())), "tn": (((0,), (0,)), ((), ()))}[mode]
    a_spec = {"nn": pl.BlockSpec((tm, tk), lambda i, j, q: (i, q)),
              "nt": pl.BlockSpec((tm, tk), lambda i, j, q: (i, q)),
              "tn": pl.BlockSpec((tk, tm), lambda i, j, q: (q, i))}[mode]
    b_spec = {"nn": pl.BlockSpec((tk, tn), lambda i, j, q: (q, j)),
              "nt": pl.BlockSpec((tn, tk), lambda i, j, q: (j, q)),
              "tn": pl.BlockSpec((tk, tn), lambda i, j, q: (q, j))}[mode]
    o_spec = pl.BlockSpec((tm, tn), lambda i, j, q: (i, j))
    has_res = residual is not None
    n_in = 3 if has_res else 2
    nr = ride.na if ride else 0

    def body(*refs):
        a_ref, b_ref = refs[0], refs[1]
        r_ref = refs[2] if has_res else None
        ride_in = refs[n_in:n_in + nr]
        o_ref = refs[n_in + nr]
        ride_out = refs[n_in + nr + 1:n_in + 2 * nr + 1]
        acc = refs[n_in + 2 * nr + 1]
        ride_sems = refs[n_in + 2 * nr + 2:]
        i, j, q = pl.program_id(0), pl.program_id(1), pl.program_id(2)
        if ride:
            @pl.when((i == 0) & (j == 0) & (q == 0))
            def _():
                ride.start(ride_in, ride_out, ride_sems)

        @pl.when(q == 0)
        def _():
            acc[...] = jnp.zeros_like(acc)

        acc[...] += _dot(a_ref[...], b_ref[...], dims)

        @pl.when(q == nk - 1)
        def _():
            r = acc[...]
            if has_res:
                r = r + r_ref[...]
            o_ref[...] = r.astype(o_ref.dtype)

        if ride:
            @pl.when((i == gi - 1) & (j == gj - 1) & (q == nk - 1))
            def _():
                ride.finish(ride_in, ride_out, ride_sems)

    hbm = pl.BlockSpec(memory_space=pltpu.HBM)
    in_specs = [a_spec, b_spec] + ([o_spec] if has_res else []) + [hbm] * nr
    args = (a, b) + ((residual,) if has_res else ()) + (tuple(ride.arrays) if ride else ())
    res = pl.pallas_call(
        body, name=name, grid=(gi, gj, nk), in_specs=in_specs, out_specs=[o_spec] + [hbm] * nr,
        out_shape=[jax.ShapeDtypeStruct((m, n), out_dtype)] + (ride.out_shape if ride else []),
        scratch_shapes=[pltpu.VMEM((tm, tn), F32)] + (ride.scratch if ride else []),
        compiler_params=pltpu.CompilerParams(
            dimension_semantics=("arbitrary",) * 3 if ride else ("parallel", "parallel", "arbitrary"),
            has_side_effects=bool(ride)),
    )(*args)
    return (res[0], res[1:]) if ride else res[0]


def _norm_matmul(x, g, w, name, ride=None):
    t, d = x.shape
    n = w.shape[0]
    tm = _pick(t, (1024, 512, 256, 128))
    tn = _pick(n, (1280, 1408, 1024, 512, 256, 128))
    gi, gj = t // tm, n // tn
    nr = ride.na if ride else 0

    def body(*refs):
        x_ref, g_ref, w_ref = refs[:3]
        ride_in = refs[3:3 + nr]
        h_ref, y_ref = refs[3 + nr], refs[4 + nr]
        ride_out = refs[5 + nr:5 + 2 * nr]
        h_scr = refs[5 + 2 * nr]
        ride_sems = refs[6 + 2 * nr:]
        i, j = pl.program_id(0), pl.program_id(1)
        if ride:
            @pl.when((i == 0) & (j == 0))
            def _():
                ride.start(ride_in, ride_out, ride_sems)

        @pl.when(j == 0)
        def _():
            h = _rms_norm(x_ref[...], g_ref[...]).astype(MXU_DTYPE)
            h_scr[...] = h
            h_ref[...] = h

        y_ref[...] = _dot(h_scr[...], w_ref[...], (((1,), (1,)), ((), ())))

        if ride:
            @pl.when((i == gi - 1) & (j == gj - 1))
            def _():
                ride.finish(ride_in, ride_out, ride_sems)

    hbm = pl.BlockSpec(memory_space=pltpu.HBM)
    res = pl.pallas_call(
        body, name=name, grid=(gi, gj),
        in_specs=[pl.BlockSpec((tm, d), lambda i, j: (i, 0)), pl.BlockSpec((1, d), lambda i, j: (0, 0)),
                  pl.BlockSpec((tn, d), lambda i, j: (j, 0))] + [hbm] * nr,
        out_specs=[pl.BlockSpec((tm, d), lambda i, j: (i, 0)), pl.BlockSpec((tm, tn), lambda i, j: (i, j))] + [hbm] * nr,
        out_shape=[jax.ShapeDtypeStruct((t, d), MXU_DTYPE), jax.ShapeDtypeStruct((t, n), F32)]
        + (ride.out_shape if ride else []),
        scratch_shapes=[pltpu.VMEM((tm, d), MXU_DTYPE)] + (ride.scratch if ride else []),
        compiler_params=pltpu.CompilerParams(
            dimension_semantics=("arbitrary",) * 2 if ride else ("parallel", "arbitrary"),
            has_side_effects=bool(ride)),
    )(x, g, w, *(ride.arrays if ride else ()))
    return (res[0], res[1], res[2:]) if ride else (res[0], res[1])


def _swiglu_math(g, u):
    return jax.nn.silu(g) * u


def _gu_tile(dff):
    return _pick(dff, (1408, 1024, 512, 256, 128))


def _interleave_gu(gate_t, up_t):
    tn = _gu_tile(gate_t.shape[0])
    pieces = []
    for j in range(gate_t.shape[0] // tn):
        pieces += [gate_t[j * tn:(j + 1) * tn], up_t[j * tn:(j + 1) * tn]]
    return jnp.concatenate(pieces, axis=0)


def _split_gu(gu_t):
    dff = gu_t.shape[0] // 2
    tn = _gu_tile(dff)
    tiles = [gu_t[j * tn:(j + 1) * tn] for j in range(2 * dff // tn)]
    return jnp.concatenate(tiles[0::2], axis=0), jnp.concatenate(tiles[1::2], axis=0)


def _ffn_up_swiglu(x, g, w_gu, name, ride=None):
    t, d = x.shape
    dff = w_gu.shape[0] // 2
    tn = _gu_tile(dff)
    tm = _pick(t, (512, 256, 128))
    gj, gi = dff // tn, t // tm
    nr = ride.na if ride else 0

    def body(*refs):
        x_ref, g_ref, w_ref = refs[:3]
        ride_in = refs[3:3 + nr]
        h_ref, gu_ref, act_ref = refs[3 + nr:6 + nr]
        ride_out = refs[6 + nr:6 + 2 * nr]
        ride_sems = refs[6 + 2 * nr:]
        j, i = pl.program_id(0), pl.program_id(1)
        if ride:
            @pl.when((i == 0) & (j == 0))
            def _():
                ride.start(ride_in, ride_out, ride_sems)

        h = _rms_norm(x_ref[...], g_ref[...]).astype(MXU_DTYPE)

        @pl.when(j == 0)
        def _():
            h_ref[...] = h

        y = _dot(h, w_ref[...], (((1,), (1,)), ((), ())))
        gu_ref[...] = y.astype(gu_ref.dtype)
        act_ref[...] = _swiglu_math(y[:, :tn], y[:, tn:]).astype(act_ref.dtype)

        if ride:
            @pl.when((i == gi - 1) & (j == gj - 1))
            def _():
                ride.finish(ride_in, ride_out, ride_sems)

    hbm = pl.BlockSpec(memory_space=pltpu.HBM)
    res = pl.pallas_call(
        body, name=name, grid=(gj, gi),
        in_specs=[pl.BlockSpec((tm, d), lambda j, i: (i, 0)), pl.BlockSpec((1, d), lambda j, i: (0, 0)),
                  pl.BlockSpec((2 * tn, d), lambda j, i: (j, 0))] + [hbm] * nr,
        out_specs=[pl.BlockSpec((tm, d), lambda j, i: (jnp.where(j == 0, i, gi - 1), 0)),
                   pl.BlockSpec((tm, 2 * tn), lambda j, i: (i, j)),
                   pl.BlockSpec((tm, tn), lambda j, i: (i, j))] + [hbm] * nr,
        out_shape=[jax.ShapeDtypeStruct((t, d), MXU_DTYPE), jax.ShapeDtypeStruct((t, 2 * dff), MXU_DTYPE),
                   jax.ShapeDtypeStruct((t, dff), MXU_DTYPE)] + (ride.out_shape if ride else []),
        scratch_shapes=(ride.scratch if ride else []),
        compiler_params=pltpu.CompilerParams(dimension_semantics=("arbitrary", "arbitrary"),
                                             has_side_effects=bool(ride)),
    )(x, g, w_gu, *(ride.arrays if ride else ()))
    return (res[0], res[1], res[2], res[3:]) if ride else tuple(res)


def _ffn_down_dx_swiglu(dx_out, w_down, gu, name):
    t, d = dx_out.shape
    dff = w_down.shape[0]
    tn = _gu_tile(dff)
    tm = _pick(t, (512, 256, 128))

    def body(dx_ref, w_ref, gu_ref, o_ref):
        dact = _dot(dx_ref[...], w_ref[...], (((1,), (1,)), ((), ())))
        gu_v = gu_ref[...].astype(F32)
        _, vjp = jax.vjp(_swiglu_math, gu_v[:, :tn], gu_v[:, tn:])
        dg, du = vjp(dact)
        o_ref[...] = jnp.concatenate([dg, du], axis=1).astype(o_ref.dtype)

    return pl.pallas_call(
        body, name=name, grid=(dff // tn, t // tm),
        in_specs=[pl.BlockSpec((tm, d), lambda j, i: (i, 0)), pl.BlockSpec((tn, d), lambda j, i: (j, 0)),
                  pl.BlockSpec((tm, 2 * tn), lambda j, i: (i, j))],
        out_specs=pl.BlockSpec((tm, 2 * tn), lambda j, i: (i, j)),
        out_shape=jax.ShapeDtypeStruct((t, 2 * dff), MXU_DTYPE),
        compiler_params=_params("parallel", "parallel"),
    )(dx_out, w_down, gu)


def _matmul_norm_bwd(dy, w, x, g, dres, name, ride=None, w_rows_k=False):
    t, k = dy.shape
    d = w.shape[1] if w_rows_k else w.shape[0]
    tm = _pick(t, (1024, 512, 256, 128))
    tk = k if k <= 2816 else _pick(k, (1408, 1280, 1024, 768, 512))
    tr = _pick(tm, (256, 128))
    gi, nk = t // tm, k // tk
    nr = ride.na if ride else 0

    def body(*refs):
        dy_ref, w_ref, x_ref, g_ref, dres_ref = refs[:5]
        ride_in = refs[5:5 + nr]
        dx_ref, dg_ref = refs[5 + nr], refs[6 + nr]
        ride_out = refs[7 + nr:7 + 2 * nr]
        acc = refs[7 + 2 * nr]
        ride_sems = refs[8 + 2 * nr:]
        i, q = pl.program_id(0), pl.program_id(1)

        @pl.when((i == 0) & (q == 0))
        def _():
            dg_ref[...] = jnp.zeros_like(dg_ref)
            if ride:
                ride.start(ride_in, ride_out, ride_sems)

        @pl.when(q == 0)
        def _():
            acc[...] = jnp.zeros_like(acc)

        acc[...] += _dot(dy_ref[...], w_ref[...], (((1,), (0 if w_rows_k else 1,)), ((), ())))

        @pl.when(q == nk - 1)
        def _():
            for r in range(tm // tr):
                rows = slice(r * tr, (r + 1) * tr)
                _, vjp = jax.vjp(_rms_norm, x_ref[rows], g_ref[...])
                dxn, dg = vjp(acc[rows])
                dx_ref[rows] = dres_ref[rows] + dxn
                dg_ref[...] += dg

        if ride:
            @pl.when((i == gi - 1) & (q == nk - 1))
            def _():
                ride.finish(ride_in, ride_out, ride_sems)

    hbm = pl.BlockSpec(memory_space=pltpu.HBM)
    row = pl.BlockSpec((tm, d), lambda i, q: (i, 0))
    res = pl.pallas_call(
        body, name=name, grid=(gi, nk),
        in_specs=[pl.BlockSpec((tm, tk), lambda i, q: (i, q)),
                  pl.BlockSpec((tk, d), lambda i, q: (q, 0)) if w_rows_k else pl.BlockSpec((d, tk), lambda i, q: (0, q)),
                  row,
                  pl.BlockSpec((1, d), lambda i, q: (0, 0)), row] + [hbm] * nr,
        out_specs=[row, pl.BlockSpec((1, d), lambda i, q: (0, 0))] + [hbm] * nr,
        out_shape=[jax.ShapeDtypeStruct((t, d), F32), jax.ShapeDtypeStruct((1, d), F32)]
        + (ride.out_shape if ride else []),
        scratch_shapes=[pltpu.VMEM((tm, d), F32)] + (ride.scratch if ride else []),
        compiler_params=pltpu.CompilerParams(dimension_semantics=("arbitrary", "arbitrary"),
                                             has_side_effects=bool(ride)),
    )(dy, w, x, g, dres, *(ride.arrays if ride else ()))
    return (res[0], res[1], res[2:]) if ride else (res[0], res[1])


def _tok_call(name, fn, t, tt, tok_in, const_in, tok_out, acc_out, carry=()):
    nblk = t // tt
    hb = tt // HALO
    blk = (lambda i: nblk - 1 - i) if carry else (lambda i: i)
    in_specs, args = [], []
    for arr, w, cb, mode in tok_in:
        if mode == "cur":
            spec = pl.BlockSpec((tt, w), lambda i, cb=cb: (blk(i), cb))
        elif mode == "prev":
            spec = pl.BlockSpec((HALO, w), lambda i, cb=cb: (jnp.maximum(blk(i) * hb - 1, 0), cb))
        else:
            spec = pl.BlockSpec((HALO, w), lambda i, cb=cb: (jnp.minimum(blk(i) + 1, nblk - 1), cb))
        in_specs.append(spec)
        args.append(arr)
    for arr in const_in:
        in_specs.append(pl.BlockSpec(arr.shape, lambda i: (0, 0)))
        args.append(arr)
    out_specs, out_shape = [], []
    for rows, w, dt in tok_out:
        out_specs.append(pl.BlockSpec((rows, w), lambda i: (blk(i), 0)))
        out_shape.append(jax.ShapeDtypeStruct((nblk * rows, w), dt))
    for shp, dt in acc_out:
        out_specs.append(pl.BlockSpec(shp, lambda i: (0, 0)))
        out_shape.append(jax.ShapeDtypeStruct(shp, dt))
    n_tok, n_const, n_out, n_acc = len(tok_in), len(const_in), len(tok_out), len(acc_out)

    def body(*refs):
        i = pl.program_id(0)
        b = blk(i)
        tok_vals = []
        for (_, _, _, mode), r in zip(tok_in, refs[:n_tok]):
            v = r[...]
            if mode == "prev":
                v = jnp.where(b > 0, v, jnp.zeros_like(v))
            elif mode == "next8":
                v = jnp.where(b < nblk - 1, v, jnp.zeros_like(v))
            tok_vals.append(v)
        const_vals = [r[...] for r in refs[n_tok:n_tok + n_const]]
        o_refs = refs[n_tok + n_const:n_tok + n_const + n_out]
        a_refs = refs[n_tok + n_const + n_out:n_tok + n_const + n_out + n_acc]
        c_refs = refs[n_tok + n_const + n_out + n_acc:]
        if a_refs or c_refs:
            @pl.when(i == 0)
            def _():
                for r in (*a_refs, *c_refs):
                    r[...] = jnp.zeros_like(r)

        if c_refs:
            outs, accs, carried = fn(tok_vals, const_vals, [r[...] for r in c_refs])
            for r, v in zip(c_refs, carried):
                r[...] = v
        else:
            outs, accs = fn(tok_vals, const_vals)
        for r, v in zip(o_refs, outs):
            r[...] = v.astype(r.dtype)
        for r, v in zip(a_refs, accs):
            r[...] += v.astype(r.dtype)

    res = pl.pallas_call(
        body, name=name, grid=(nblk,), in_specs=in_specs, out_specs=out_specs, out_shape=out_shape,
        scratch_shapes=[pltpu.VMEM(shp, dt) for shp, dt in carry],
        compiler_params=_params("arbitrary" if acc_out or carry else "parallel"),
    )(*args)
    return res


def _dn_pre_math(cur, halo, ba, cw, alog, dtb):
    tt = cur.shape[0]
    a = jax.nn.silu(_causal_conv(cur, halo, cw, DN_CONV))
    pieces = []
    for p in range(2 * DN_HEADS):
        xh = a[:, p * HEAD_DIM:(p + 1) * HEAD_DIM]
        xh = xh * lax.rsqrt(jnp.sum(xh * xh, axis=-1, keepdims=True) + EPS)
        if p < DN_HEADS:
            xh = xh * (HEAD_DIM ** -0.5)
        pieces.append(xh)
    pieces.append(a[:, 2 * DN_WIDTH:])
    qkvn = jnp.concatenate(pieces, axis=1)
    lane = lax.broadcasted_iota(jnp.int32, ba.shape, 1)
    raw = jnp.where(lane < DN_HEADS, jax.nn.sigmoid(ba), -jnp.exp(alog) * jax.nn.softplus(ba + dtb))
    r = lax.broadcasted_iota(jnp.int32, (tt, tt), 0)
    c = lax.broadcasted_iota(jnp.int32, (tt, tt), 1)
    tri = jnp.where((r // CHUNK == c // CHUNK) & (c <= r), 1.0, 0.0).astype(F32)
    cums = _ein("ij,jk->ik", LHS_EXACT, tri, raw)
    bg = jnp.where(lane < DN_HEADS, raw, cums)
    return qkvn, bg


def _mix_math(o, z, gb, gc, gc_halo, hv, hv_halo, dng, scg, scw):
    outs = []
    for h in range(DN_HEADS):
        sl = slice(h * HEAD_DIM, (h + 1) * HEAD_DIM)
        oh = o[:, sl]
        outs.append(oh * lax.rsqrt(jnp.mean(oh * oh, axis=-1, keepdims=True) + EPS) * dng * jax.nn.silu(z[:, sl]))
    y = gb * _causal_conv(gc * hv, gc_halo * hv_halo, scw, SC_CONV)
    gw = SC_WIDTH // SC_GROUPS
    for g in range(SC_GROUPS):
        sl = slice(g * gw, (g + 1) * gw)
        yg = y[:, sl]
        outs.append(yg * lax.rsqrt(jnp.mean(yg * yg, axis=-1, keepdims=True) + EPS) * scg[:, sl])
    return jnp.concatenate(outs, axis=1)


def _tri_inverse(a):
    c = a.shape[-1]
    r = lax.broadcasted_iota(jnp.int32, (c, c), 0)
    q = lax.broadcasted_iota(jnp.int32, (c, c), 1)
    eye = jnp.where(r == q, 1.0, 0.0).astype(F32)[None]
    blk = (r // 16 == q // 16)[None]
    d = jnp.where(blk, a, 0.0)
    o = a - d
    mm = functools.partial(_ein, "bij,bjk->bik", PRECISE)
    p = eye - d
    n = mm(d, d)
    for _ in range(2):
        both = mm(jnp.concatenate([n, p], axis=1), n)
        n = both[:, :c]
        p = p + both[:, c:]
    p = p + mm(p, n)
    e = mm(p, o)
    e2 = mm(e, e)
    left = eye - e + e2 - mm(e, e2)
    return mm(left, p)


@jax.custom_vjp
def _inverse_known(a, tinv):
    return tinv


def _inverse_known_fwd(a, tinv):
    return tinv, tinv


def _inverse_known_bwd(tinv, ct):
    left = _ein("bji,bjk->bik", PRECISE, tinv, ct)
    return -_ein("bik,bjk->bij", PRECISE, left, tinv), jnp.zeros_like(tinv)


_inverse_known.defvjp(_inverse_known_fwd, _inverse_known_bwd)


def _delta_intra_math(q, k, v, bg, head, tinv_known=None):
    n = q.shape[0]
    nb = n // CHUNK
    lane = lax.broadcasted_iota(jnp.int32, bg.shape, 1)
    beta = jnp.sum(jnp.where(lane == head, bg, 0.0), axis=1, keepdims=True).reshape(nb, CHUNK, 1)
    gc = jnp.sum(jnp.where(lane == head + DN_HEADS, bg, 0.0), axis=1, keepdims=True).reshape(nb, CHUNK, 1)
    q3, k3, v3 = (a.reshape(nb, CHUNK, HEAD_DIM) for a in (q, k, v))
    r = lax.broadcasted_iota(jnp.int32, (CHUNK, CHUNK), 0)
    c = lax.broadcasted_iota(jnp.int32, (CHUNK, CHUNK), 1)
    eye = jnp.where(r == c, 1.0, 0.0).astype(F32)[None]
    gcr = _ein("bik,bkj->bij", LHS_EXACT, jnp.ones((nb, CHUNK, CHUNK), F32), gc * eye)
    decay = jnp.exp(jnp.where((r >= c)[None], gc - gcr, -1e30))
    kb = k3 * beta
    vb = v3 * beta
    egc = jnp.exp(gc)
    on_k = _ein("bcd,bmd->bcm", FAST, jnp.concatenate([kb, q3], axis=1), k3)
    a = jnp.where((r > c)[None], on_k[:, :CHUNK] * decay, 0.0)
    tinv = _tri_inverse(a) if tinv_known is None else _inverse_known(a, tinv_known)
    uw = _ein("bcm,bmd->bcd", PRECISE, tinv, jnp.concatenate([vb, kb * egc], axis=2))
    u, w = uw[:, :, :HEAD_DIM], uw[:, :, HEAD_DIM:]
    qk = on_k[:, CHUNK:] * decay
    row = lax.broadcasted_iota(jnp.int32, (nb, CHUNK, 1), 1)
    glast = jnp.sum(jnp.where(row == CHUNK - 1, gc, 0.0), axis=1, keepdims=True)
    qd = q3 * egc
    kd = k3 * jnp.exp(glast - gc)
    glb = jnp.broadcast_to(jnp.exp(glast), (nb, HALO, LANES))
    flat = lambda x: x.reshape(n, HEAD_DIM)
    return flat(u), flat(w), flat(qd), flat(kd), qk, glb, tinv


def _delta_step_math(u, w, qd, kd, qk, gl, s):
    c = u.shape[0]
    on_s = _ein("ck,kv->cv", FAST, jnp.concatenate([w, qd], axis=0), s)
    vnew = u - on_s[:c]
    on_v = _ein("cm,mv->cv", FAST, jnp.concatenate([qk, kd.T], axis=0), vnew)
    o = on_s[c:] + on_v[:c]
    s2 = s * gl + on_v[c:]
    return o, s2


def _delta_intra(qkvn, bg, nb):
    t = qkvn.shape[0]
    n = t // CHUNK
    rows = nb * CHUNK

    def body(q_ref, k_ref, v_ref, bg_ref, u_ref, w_ref, qd_ref, kd_ref, qk_ref, gl_ref, ti_ref):
        outs = _delta_intra_math(q_ref[...], k_ref[...], v_ref[...], bg_ref[...], pl.program_id(1))
        for r, v in zip((u_ref, w_ref, qd_ref, kd_ref, qk_ref, gl_ref, ti_ref), outs):
            r[...] = v

    col = lambda off: pl.BlockSpec((rows, HEAD_DIM), lambda b, h, off=off: (b, off + h))
    tok = jax.ShapeDtypeStruct((t, DN_WIDTH), F32)
    return pl.pallas_call(
        body, name="delta_intra", grid=(n // nb, DN_HEADS),
        in_specs=[col(0), col(DN_HEADS), col(2 * DN_HEADS), pl.BlockSpec((rows, LANES), lambda b, h: (b, 0))],
        out_specs=[col(0)] * 4 + [pl.BlockSpec((nb, None, CHUNK, CHUNK), lambda b, h: (b, h, 0, 0)),
                                  pl.BlockSpec((nb, None, HALO, LANES), lambda b, h: (b, h, 0, 0)),
                                  pl.BlockSpec((nb, None, CHUNK, CHUNK), lambda b, h: (b, h, 0, 0))],
        out_shape=[tok] * 4 + [jax.ShapeDtypeStruct((n, DN_HEADS, CHUNK, CHUNK), F32),
                               jax.ShapeDtypeStruct((n, DN_HEADS, HALO, LANES), F32),
                               jax.ShapeDtypeStruct((n, DN_HEADS, CHUNK, CHUNK), F32)],
        compiler_params=_params("parallel", "arbitrary"),
    )(qkvn, qkvn, qkvn, bg)


def _delta_intra_bwd(qkvn, bg, tinv, cts, nb):
    t = qkvn.shape[0]
    n = t // CHUNK
    rows = nb * CHUNK

    def body(q_ref, k_ref, v_ref, bg_ref, ti_ref, du, dw, dqd, dkd, dqk, dgl, dq_ref, dk_ref, dv_ref, dbg_ref):
        h = pl.program_id(1)
        ti = ti_ref[...]
        _, vjp = jax.vjp(lambda q, k, v, b: _delta_intra_math(q, k, v, b, h, ti)[:6],
                         q_ref[...], k_ref[...], v_ref[...], bg_ref[...])
        dq, dk, dv, dbg = vjp((du[...], dw[...], dqd[...], dkd[...], dqk[...], dgl[...]))
        dq_ref[...] = dq
        dk_ref[...] = dk
        dv_ref[...] = dv

        @pl.when(h == 0)
        def _():
            dbg_ref[...] = jnp.zeros_like(dbg_ref)

        dbg_ref[...] += dbg

    col = lambda off: pl.BlockSpec((rows, HEAD_DIM), lambda b, h, off=off: (b, off + h))
    bgs = pl.BlockSpec((rows, LANES), lambda b, h: (b, 0))
    qks = pl.BlockSpec((nb, None, CHUNK, CHUNK), lambda b, h: (b, h, 0, 0))
    gls = pl.BlockSpec((nb, None, HALO, LANES), lambda b, h: (b, h, 0, 0))
    tok = jax.ShapeDtypeStruct((t, DN_WIDTH), F32)
    return pl.pallas_call(
        body, name="delta_intra_bwd", grid=(n // nb, DN_HEADS),
        in_specs=[col(0), col(DN_HEADS), col(2 * DN_HEADS), bgs, qks, col(0), col(0), col(0), col(0), qks, gls],
        out_specs=[col(0), col(0), col(0), bgs],
        out_shape=[tok, tok, tok, jax.ShapeDtypeStruct((t, LANES), F32)],
        compiler_params=_params("parallel", "arbitrary"),
    )(qkvn, qkvn, qkvn, bg, tinv, *cts)


def _delta_scan(u, w, qd, kd, qk, glb, cb):
    t = u.shape[0]
    n = t // CHUNK
    rows = cb * CHUNK

    def body(u_ref, w_ref, qd_ref, kd_ref, qk_ref, gl_ref, o_ref, s_ref, s_scr):
        @pl.when(pl.program_id(0) == 0)
        def _():
            s_scr[...] = jnp.zeros_like(s_scr)

        def chunk(c, carry):
            r0 = pl.multiple_of(c * CHUNK, CHUNK)
            for h in range(DN_HEADS):
                sl = (pl.ds(r0, CHUNK), slice(h * HEAD_DIM, (h + 1) * HEAD_DIM))
                s = s_scr[h]
                s_ref[c, h] = s
                o, s2 = _delta_step_math(u_ref[sl], w_ref[sl], qd_ref[sl], kd_ref[sl], qk_ref[c, h],
                                         gl_ref[c, h][0:1, :], s)
                o_ref[sl] = o
                s_scr[h] = s2
            return carry

        lax.fori_loop(0, cb, chunk, 0)

    tok = pl.BlockSpec((rows, DN_WIDTH), lambda i: (i, 0))
    return pl.pallas_call(
        body, name="delta_scan", grid=(n // cb,),
        in_specs=[tok] * 4 + [pl.BlockSpec((cb, DN_HEADS, CHUNK, CHUNK), lambda i: (i, 0, 0, 0)),
                              pl.BlockSpec((cb, DN_HEADS, HALO, LANES), lambda i: (i, 0, 0, 0))],
        out_specs=[tok, pl.BlockSpec((cb, DN_HEADS, HEAD_DIM, HEAD_DIM), lambda i: (i, 0, 0, 0))],
        out_shape=[jax.ShapeDtypeStruct((t, DN_WIDTH), F32),
                   jax.ShapeDtypeStruct((n, DN_HEADS, HEAD_DIM, HEAD_DIM), F32)],
        scratch_shapes=[pltpu.VMEM((DN_HEADS, HEAD_DIM, HEAD_DIM), F32)],
        compiler_params=_params("arbitrary"),
    )(u, w, qd, kd, qk, glb)


def _delta_scan_bwd(u, w, qd, kd, qk, glb, s_all, do, cb):
    t = u.shape[0]
    n = t // CHUNK
    nblk = n // cb
    rows = cb * CHUNK

    def body(u_ref, w_ref, qd_ref, kd_ref, qk_ref, gl_ref, s_ref, do_ref,
             du_ref, dw_ref, dqd_ref, dkd_ref, dqk_ref, dgl_ref, ds_scr):
        @pl.when(pl.program_id(0) == 0)
        def _():
            ds_scr[...] = jnp.zeros_like(ds_scr)

        def chunk(step, carry):
            c = cb - 1 - step
            r0 = pl.multiple_of(c * CHUNK, CHUNK)
            for h in range(DN_HEADS):
                sl = (pl.ds(r0, CHUNK), slice(h * HEAD_DIM, (h + 1) * HEAD_DIM))
                gl_tile = gl_ref[c, h]
                prim = (u_ref[sl], w_ref[sl], qd_ref[sl], kd_ref[sl], qk_ref[c, h], gl_tile, s_ref[c, h])
                _, vjp = jax.vjp(lambda a, b, cc, d, e, g, s: _delta_step_math(a, b, cc, d, e, g[0:1, :], s), *prim)
                du, dw, dqd, dkd, dqk, dgl, ds = vjp((do_ref[sl], ds_scr[h]))
                du_ref[sl] = du
                dw_ref[sl] = dw
                dqd_ref[sl] = dqd
                dkd_ref[sl] = dkd
                dqk_ref[c, h] = dqk
                dgl_ref[c, h] = dgl
                ds_scr[h] = ds
            return carry

        lax.fori_loop(0, cb, chunk, 0)

    rev = lambda i: nblk - 1 - i
    tok = pl.BlockSpec((rows, DN_WIDTH), lambda i: (rev(i), 0))
    qks = pl.BlockSpec((cb, DN_HEADS, CHUNK, CHUNK), lambda i: (rev(i), 0, 0, 0))
    gls = pl.BlockSpec((cb, DN_HEADS, HALO, LANES), lambda i: (rev(i), 0, 0, 0))
    ss = pl.BlockSpec((cb, DN_HEADS, HEAD_DIM, HEAD_DIM), lambda i: (rev(i), 0, 0, 0))
    tshape = jax.ShapeDtypeStruct((t, DN_WIDTH), F32)
    return pl.pallas_call(
        body, name="delta_scan_bwd", grid=(nblk,),
        in_specs=[tok] * 4 + [qks, gls, ss, tok],
        out_specs=[tok] * 4 + [qks, gls],
        out_shape=[tshape] * 4 + [jax.ShapeDtypeStruct(qk.shape, F32), jax.ShapeDtypeStruct(glb.shape, F32)],
        scratch_shapes=[pltpu.VMEM((DN_HEADS, HEAD_DIM, HEAD_DIM), F32)],
        compiler_params=_params("arbitrary"),
    )(u, w, qd, kd, qk, glb, s_all, do)


def _peer(mask):
    x, y, c = lax.axis_index("x"), lax.axis_index("y"), lax.axis_index("c")
    return (x ^ ((mask >> 2) & 1), y ^ ((mask >> 1) & 1), c ^ (mask & 1))


def _my_index():
    return 4 * lax.axis_index("x") + 2 * lax.axis_index("y") + lax.axis_index("c")


class _Exchange:
    CHIP_MASKS = (4, 2, 6)

    def __init__(self, kind, arrays):
        self.kind = kind
        self.arrays = list(arrays)
        self.na = na = len(self.arrays)
        if kind == "gather":
            self.out_shape = [jax.ShapeDtypeStruct((N_DEV,) + a.shape, a.dtype) for a in self.arrays]
        else:
            self.out_shape = [jax.ShapeDtypeStruct(a.shape, a.dtype) for a in self.arrays]
        self.scratch = [pltpu.SemaphoreType.DMA((na, 7)), pltpu.SemaphoreType.DMA((na, 7)),
                        pltpu.SemaphoreType.DMA((na,))]

    def _copies(self, ins, outs, sems):
        send_sems, recv_sems, local_sems = sems
        me = _my_index()
        local, first, passed, arrivals = [], [], [], []
        if self.kind == "gather":
            def rc(a, k, block, to, own=False):
                def make():
                    dst = outs[a].at[block]
                    return pltpu.make_async_remote_copy(src_ref=ins[a] if own else dst, dst_ref=dst,
                                                        send_sem=send_sems.at[a, k], recv_sem=recv_sems.at[a, k],
                                                        device_id=to, device_id_type=MESH)
                return make

            sib = _peer(1)
            for a in range(self.na):
                local.append(lambda a=a: pltpu.make_async_copy(ins[a], outs[a].at[me], local_sems.at[a]))
                first.append(rc(a, 0, me, sib, own=True))
                arrivals.append(rc(a, 0, me ^ 1, _peer(0)))
                for j, m in enumerate(self.CHIP_MASKS):
                    first.append(rc(a, 1 + j, me, _peer(m), own=True))
                    passed.append((rc(a, 1 + j, me ^ m, _peer(0)), rc(a, 4 + j, me ^ m, sib)))
                    arrivals.append(rc(a, 4 + j, me ^ m ^ 1, _peer(0)))
        else:
            for a in range(self.na):
                local.append(lambda a=a: pltpu.make_async_copy(ins[a].at[me], outs[a].at[me], local_sems.at[a]))
                for m in range(1, N_DEV):
                    def make(a=a, m=m):
                        return pltpu.make_async_remote_copy(
                            src_ref=ins[a].at[me ^ m], dst_ref=outs[a].at[me], send_sem=send_sems.at[a, m - 1],
                            recv_sem=recv_sems.at[a, m - 1], device_id=_peer(m), device_id_type=MESH)
                    first.append(make)
                    arrivals.append(make)
        return local, first, passed, arrivals

    def start(self, ins, outs, sems):
        local, first, _, _ = self._copies(ins, outs, sems)
        for make in local + first:
            make().start()

    def finish(self, ins, outs, sems):
        local, first, passed, arrivals = self._copies(ins, outs, sems)
        for landed, onward in passed:
            landed().wait_recv()
            onward().start()
        for make in arrivals:
            make().wait_recv()
        for make in first + [p for _, p in passed]:
            make().wait_send()
        for make in local:
            make().wait()

    def run(self, name):
        na = self.na

        def body(*refs):
            ins, outs, sems = refs[:na], refs[na:2 * na], refs[2 * na:]
            self.start(ins, outs, sems)
            self.finish(ins, outs, sems)

        hbm = pl.BlockSpec(memory_space=pltpu.HBM)
        return pl.pallas_call(
            body, name=name, in_specs=[hbm] * na, out_specs=[hbm] * na, out_shape=self.out_shape,
            scratch_shapes=self.scratch, compiler_params=pltpu.CompilerParams(has_side_effects=True),
        )(*self.arrays)


def _all_gather(shards, name):
    return _Exchange("gather", shards).run(name)


def _adamw_math(w, g, m, v):
    m2 = ADAM_B1 * m + (1.0 - ADAM_B1) * g
    v2 = ADAM_B2 * v + (1.0 - ADAM_B2) * jnp.square(g)
    m_hat = m2 / (1.0 - ADAM_B1 ** ADAM_STEP)
    v_hat = v2 / (1.0 - ADAM_B2 ** ADAM_STEP)
    delta = -ADAM_LR * (m_hat / (jnp.sqrt(v_hat) + ADAM_EPS) + ADAM_WD * w)
    return delta, m2, v2


def _sum_adamw(parts, w, m, v, name):
    r, c = w.shape
    tr = _pick(r, (512, 256, 352, 128, 64, 32, 16, 8))
    np_ = parts.shape[0]

    def body(p_ref, w_ref, m_ref, v_ref, g_ref, d_ref, m2_ref, v2_ref):
        g = p_ref[0].astype(F32)
        for d in range(1, np_):
            g = g + p_ref[d].astype(F32)
        delta, m2, v2 = _adamw_math(w_ref[...], g, m_ref[...], v_ref[...])
        g_ref[...] = g
        d_ref[...] = delta
        m2_ref[...] = m2
        v2_ref[...] = v2

    blk = pl.BlockSpec((tr, c), lambda i: (i, 0))
    shp = jax.ShapeDtypeStruct((r, c), F32)
    return pl.pallas_call(
        body, name=name, grid=(r // tr,),
        in_specs=[pl.BlockSpec((np_, tr, c), lambda i: (0, i, 0)), blk, blk, blk],
        out_specs=[blk] * 4, out_shape=[shp] * 4, compiler_params=_params("parallel"),
    )(parts, w, m, v)


def _sum_rows(parts, name):
    np_, r, c = parts.shape

    def body(p_ref, o_ref):
        g = p_ref[0]
        for d in range(1, np_):
            g = g + p_ref[d]
        o_ref[...] = g

    return pl.pallas_call(body, name=name, out_shape=jax.ShapeDtypeStruct((r, c), F32))(parts)


def _pad_w_in(w):
    d = w.shape[1]
    n_ba = 2 * DN_HEADS
    a = w[:SC_OFF]
    ba = w[SC_OFF:SC_OFF + n_ba]
    sc = w[SC_OFF + n_ba:]
    return jnp.concatenate([a, sc, ba, jnp.zeros((BA_W - n_ba, d), w.dtype)], axis=0)


def _unpad_w_in(wp):
    n_ba = 2 * DN_HEADS
    return jnp.concatenate([wp[:SC_OFF], wp[BA_OFF:BA_OFF + n_ba], wp[SC_OFF:BA_OFF]], axis=0)


def _lane_row(v, off):
    return jnp.pad(v.astype(F32), (off, LANES - off - v.shape[0]))[None]


TT = 256
TT_FWD = 512
NB_INTRA = 32
CB_SCAN = 8


def _layer_fwd(x, p, ride_proj, ride_ffn, late):
    t, d = x.shape
    got_proj = got_ffn = None
    if ride_proj:
        h, proj, got_proj = _norm_matmul(x, p["norm1_g"], p["w_in"], "proj_fwd_gather", ride=ride_proj)
    else:
        h, proj = _norm_matmul(x, p["norm1_g"], p["w_in"], "proj_fwd")
    p = {**p, **late(got_proj)}
    qkvn, bg = _tok_call(
        "dn_pre", lambda tv, cv: (_dn_pre_math(*tv, *cv), ()), t, min(TT_FWD, t),
        [(proj, QKV_W, 0, "cur"), (proj, QKV_W, 0, "prev"), (proj, LANES, BA_OFF // LANES, "cur")],
        [p["dn_conv_w"], p["alog_row"], p["dt_row"]],
        [(min(TT_FWD, t), QKV_W, F32), (min(TT_FWD, t), LANES, F32)], [])
    u, w, qd, kd, qk, glb, tinv = _delta_intra(qkvn, bg, min(NB_INTRA, t // CHUNK))
    o, s_all = _delta_scan(u, w, qd, kd, qk, glb, CB_SCAN)
    cb0 = SC_OFF // SC_WIDTH
    mix_in = [(o, DN_WIDTH, 0, "cur"), (proj, DN_WIDTH, Z_OFF // DN_WIDTH, "cur"),
              (proj, SC_WIDTH, cb0, "cur"), (proj, SC_WIDTH, cb0 + 1, "cur"), (proj, SC_WIDTH, cb0 + 1, "prev"),
              (proj, SC_WIDTH, cb0 + 2, "cur"), (proj, SC_WIDTH, cb0 + 2, "prev")]
    mix_const = [p["dn_norm_g"], p["sc_norm_g"], p["sc_conv_w"]]
    (cat,) = _tok_call("mix_post", lambda tv, cv: ((_mix_math(*tv, *cv),), ()), t, min(TT_FWD, t),
                       mix_in, mix_const, [(min(TT_FWD, t), 2 * DN_WIDTH, MXU_DTYPE)], [])
    x_mid = _matmul(cat, p["w_out"], "nn", F32, "out_proj", residual=x)
    if ride_ffn:
        h2, gu, act, got_ffn = _ffn_up_swiglu(x_mid, p["norm2_g"], p["w_gu"], "ffn_up_gather", ride=ride_ffn)
    else:
        h2, gu, act = _ffn_up_swiglu(x_mid, p["norm2_g"], p["w_gu"], "ffn_up")
    x_out = _matmul(act, p["w_down"], "nn", F32, "ffn_down", residual=x_mid)
    saved = dict(x=x, h=h, proj=proj, qkvn=qkvn, bg=bg, u=u, w=w, qd=qd, kd=kd, qk=qk, glb=glb, tinv=tinv, s_all=s_all, o=o,
                 cat=cat, x_mid=x_mid, h2=h2, gu=gu, act=act, mix_in=mix_in, mix_const=mix_const)
    return x_out, saved, p, got_proj, got_ffn


def _layer_bwd(dx_out, p, s, ride_prev, scatter_of):
    t, d = dx_out.shape
    got = {}
    dgu = _ffn_down_dx_swiglu(dx_out, p["w_down"], s["gu"], "ffn_down_dx")
    d_w_down = _matmul(s["act"], dx_out, "tn", MXU_DTYPE, "ffn_down_dw")
    if ride_prev:
        dx_mid, d_norm2, got["prev"] = _matmul_norm_bwd(dgu, p["w_gu"], s["x_mid"], p["norm2_g"], dx_out,
                                                        "ffn_up_dx_scatter", ride=ride_prev, w_rows_k=True)
    else:
        dx_mid, d_norm2 = _matmul_norm_bwd(dgu, p["w_gu"], s["x_mid"], p["norm2_g"], dx_out, "ffn_up_dx",
                                           w_rows_k=True)
    d_w_gu, got["down"] = _matmul(dgu, s["h2"], "tn", MXU_DTYPE, "ffn_up_dw_scatter", ride=scatter_of(d_w_down))
    d_gate, d_up = _split_gu(d_w_gu)
    dcat = _matmul(dx_mid, p["w_out"], "nt", F32, "out_proj_dx")
    d_w_out = _matmul(s["cat"], dx_mid, "tn", MXU_DTYPE, "out_proj_dw")

    def mix_bwd(tv, cv):
        prim = tuple(tv[:7]) + tuple(cv)
        _, vjp = jax.vjp(_mix_math, *prim)
        do, dz, dgb, dgc, dgch, dhv, dhvh, ddng, dscg, dscw = vjp(tv[7])
        return (do, dz, dgb, dgc, dgch, dhv, dhvh), (ddng, dscg, dscw)

    wide = (TT, DN_WIDTH, F32)
    final = (TT, DN_WIDTH, MXU_DTYPE)
    halo = (HALO, SC_WIDTH, F32)
    do, dz, dgb, dgc, dgc_h, dhv, dhv_h, d_dn_norm, d_sc_norm, d_sc_conv = _tok_call(
        "mix_post_bwd", mix_bwd, t, TT, s["mix_in"] + [(dcat, 2 * DN_WIDTH, 0, "cur")], s["mix_const"],
        [wide, final, final, wide, halo, wide, halo],
        [((1, HEAD_DIM), F32), ((1, SC_WIDTH), F32), ((SC_CONV, SC_WIDTH), F32)])
    cts = _delta_scan_bwd(s["u"], s["w"], s["qd"], s["kd"], s["qk"], s["glb"], s["s_all"], do, CB_SCAN)
    dq, dk, dv, dbg = _delta_intra_bwd(s["qkvn"], s["bg"], s["tinv"], cts, min(NB_INTRA, t // CHUNK))
    proj = s["proj"]

    def with_halo(cur, nxt):
        return cur + jnp.concatenate([jnp.zeros((TT - HALO, cur.shape[1]), F32), nxt], axis=0)

    def dn_pre_bwd(tv, cv, carried):
        cur, hal, ba, dq_, dk_, dv_, dbg_, dz_, dgb_, dgc_, dgc_n, dhv_, dhv_n = tv
        _, vjp = jax.vjp(_dn_pre_math, cur, hal, ba, *cv)
        dcur, dhal, dba, dcw, dal, ddt = vjp((jnp.concatenate([dq_, dk_, dv_], axis=1), dbg_))
        dproj_rows = jnp.concatenate([with_halo(dcur, carried[0]), dz_.astype(F32), dgb_.astype(F32), with_halo(dgc_, dgc_n),
                                      with_halo(dhv_, dhv_n), dba, jnp.zeros((TT, BA_W - LANES), F32)], axis=1)
        return (dproj_rows,), (dcw, dal, ddt), (dhal,)

    dproj, d_dn_conv, d_alog, d_dt = _tok_call(
        "dn_pre_bwd", dn_pre_bwd, t, TT,
        [(proj, QKV_W, 0, "cur"), (proj, QKV_W, 0, "prev"), (proj, LANES, BA_OFF // LANES, "cur"),
         (dq, DN_WIDTH, 0, "cur"), (dk, DN_WIDTH, 0, "cur"), (dv, DN_WIDTH, 0, "cur"), (dbg, LANES, 0, "cur"),
         (dz, DN_WIDTH, 0, "cur"), (dgb, SC_WIDTH, 0, "cur"), (dgc, SC_WIDTH, 0, "cur"), (dgc_h, SC_WIDTH, 0, "next8"),
         (dhv, SC_WIDTH, 0, "cur"), (dhv_h, SC_WIDTH, 0, "next8")],
        [p["dn_conv_w"], p["alog_row"], p["dt_row"]],
        [(TT, PROJ_W, MXU_DTYPE)],
        [((DN_CONV, QKV_W), F32), ((1, LANES), F32), ((1, LANES), F32)],
        carry=[((HALO, QKV_W), F32)])
    dx_in, d_norm1, got["gate"] = _matmul_norm_bwd(dproj, p["w_in"], s["x"], p["norm1_g"], dx_mid,
                                                   "proj_dx_scatter", ride=scatter_of(d_gate), w_rows_k=True)
    d_w_in, got["up"] = _matmul(dproj, s["h"], "tn", MXU_DTYPE, "proj_dw_scatter", ride=scatter_of(d_up))
    grads = dict(w_in=d_w_in, w_out=d_w_out, w_gu=d_w_gu, w_down=d_w_down, norm1_g=d_norm1, norm2_g=d_norm2,
                 dn_norm_g=d_dn_norm, sc_norm_g=d_sc_norm, sc_conv_w=d_sc_conv, dn_conv_w=d_dn_conv,
                 alog=d_alog, dt=d_dt)
    return dx_in, grads, got


def _final_loss(x, g, target):
    t, d = x.shape

    def fn(tv, cv):
        xv, tg = tv

        def loss_fn(xx, gg):
            err = jnp.square(_rms_norm(xx, gg) - tg)
            return 0.5 * jnp.sum(jnp.mean(err, axis=-1))

        loss, vjp = jax.vjp(loss_fn, xv, cv[0])
        dx, dg = vjp(jnp.ones((), F32))
        return (dx,), (jnp.full((1, LANES), loss, F32), dg)

    return _tok_call("final_loss", fn, t, TT, [(x, d, 0, "cur"), (target, d, 0, "cur")], [g],
                     [(TT, d, F32)], [((1, LANES), F32), ((1, d), F32)])


def _pack_rows(arrs):
    rows, offs, r0 = [], [], 0
    for a in arrs:
        n = a.size
        nr = -(-n // LANES)
        flat = jnp.pad(a.reshape(-1).astype(F32), (0, nr * LANES - n))
        rows.append(flat.reshape(nr, LANES))
        offs.append((r0, nr, a.shape))
        r0 += nr
    pad = (-r0) % 8
    if pad:
        rows.append(jnp.zeros((pad, LANES), F32))
    return jnp.concatenate(rows, axis=0), offs


def _unpack_rows(packed, offs):
    out = []
    for r0, nr, shp in offs:
        n = 1
        for s_ in shp:
            n *= s_
        out.append(packed[r0:r0 + nr].reshape(-1)[:n].reshape(shp))
    return out


def kernel(x, norm1_g, w_in, dn_conv_w, dn_a_log, dn_dt_bias, dn_norm_g, sc_conv_w, sc_norm_g, w_out, norm2_g, ffn_w_gate, ffn_w_up, ffn_w_down, final_norm_g, loss_target, m_norm1_g, m_w_in, m_dn_conv_w, m_dn_a_log, m_dn_dt_bias, m_dn_norm_g, m_sc_conv_w, m_sc_norm_g, m_w_out, m_norm2_g, m_ffn_w_gate, m_ffn_w_up, m_ffn_w_down, m_final_norm_g, v_norm1_g, v_w_in, v_dn_conv_w, v_dn_a_log, v_dn_dt_bias, v_dn_norm_g, v_sc_conv_w, v_sc_norm_g, v_w_out, v_norm2_g, v_ffn_w_gate, v_ffn_w_up, v_ffn_w_down, v_final_norm_g):
    depth, d, cin = w_in.shape
    t = x.shape[1]
    dff_s = ffn_w_gate.shape[2]
    tr = lambda a: a.transpose(0, 2, 1)
    gate_t, up_t = tr(ffn_w_gate), tr(ffn_w_up)
    by_layer = lambda a: [a[l].T for l in range(depth)]
    win_t = by_layer(w_in)
    me = _my_index()
    x2 = x.reshape(t, d)
    tgt = loss_target.reshape(t, d)

    def shards(l):
        return [a[l].astype(MXU_DTYPE) for a in (win_t, w_out, gate_t, up_t, ffn_w_down)]

    conv_pack, conv_offs = _pack_rows([dn_conv_w, sc_conv_w])
    conv_all, *first = _all_gather([conv_pack] + shards(0)[:2], "gather_first")
    dn_parts, sc_parts = zip(*[_unpack_rows(conv_all[j], conv_offs) for j in range(N_DEV)])
    dn_conv_full = jnp.concatenate(dn_parts, axis=2)
    sc_conv_full = jnp.concatenate(sc_parts, axis=2)

    def mixer_params(l, g_in, g_out):
        return dict(
            w_in=_pad_w_in(g_in.reshape(N_DEV * cin, d)), w_out=g_out.reshape(d, d),
            norm1_g=norm1_g[l][None], norm2_g=norm2_g[l][None], dn_norm_g=dn_norm_g[l][None],
            sc_norm_g=sc_norm_g[l][None], dn_conv_w=dn_conv_full[l], sc_conv_w=sc_conv_full[l],
            alog_row=_lane_row(dn_a_log[l], DN_HEADS), dt_row=_lane_row(dn_dt_bias[l], DN_HEADS))

    def ffn_params(g_gate, g_up, g_down):
        dff = N_DEV * dff_s
        return dict(w_gu=_interleave_gu(g_gate.reshape(dff, d), g_up.reshape(dff, d)), w_down=g_down.reshape(dff, d))

    nxt = mixer_params(0, *first)
    nxt_ffn = None
    params, saved = [], []
    xc = x2
    for l in range(depth):
        last = l + 1 == depth
        own_ffn = shards(l)[2:] if nxt_ffn is None else []
        ahead = [] if last else shards(l + 1)
        ride_proj = _Exchange("gather", own_ffn + ahead[:2]) if own_ffn or ahead else None
        ride_ffn = None if last else _Exchange("gather", ahead[2:])

        def late(got, own_ffn=own_ffn, nxt_ffn=nxt_ffn):
            return ffn_params(*got[:3]) if own_ffn else nxt_ffn

        xc, s, p_l, got_proj, got_ffn = _layer_fwd(xc, nxt, ride_proj, ride_ffn, late)
        params.append(p_l)
        saved.append(s)
        if not last:
            nxt = mixer_params(l + 1, *got_proj[len(own_ffn):])
            nxt_ffn = ffn_params(*got_ffn)
    dx, loss_part, d_final = _final_loss(xc, final_norm_g[None], tgt)

    names = ("w_in", "w_out", "ffn_w_gate", "ffn_w_up", "ffn_w_down")
    big_out = {n: {k: [None] * depth for k in ("g", "d", "m", "v")} for n in names}
    w_loc = dict(w_in=win_t, w_out=w_out, ffn_w_gate=gate_t, ffn_w_up=up_t, ffn_w_down=ffn_w_down)
    m_loc = dict(w_in=by_layer(m_w_in), w_out=m_w_out, ffn_w_gate=tr(m_ffn_w_gate), ffn_w_up=tr(m_ffn_w_up), ffn_w_down=m_ffn_w_down)
    v_loc = dict(w_in=by_layer(v_w_in), w_out=v_w_out, ffn_w_gate=tr(v_ffn_w_gate), ffn_w_up=tr(v_ffn_w_up), ffn_w_down=v_ffn_w_down)


    def scatter_of(grad):
        return _Exchange("scatter", [grad.reshape(N_DEV, dff_s, d).astype(MXU_DTYPE)])

    def apply(l, which, recv):
        for n, r in zip(which, recv):
            res = _sum_adamw(r, w_loc[n][l], m_loc[n][l], v_loc[n][l], "adamw_" + n)
            for k, a in zip(("g", "d", "m", "v"), res):
                big_out[n][k][l] = a

    grads = [None] * depth
    pending = None
    for l in reversed(range(depth)):
        ride_prev = _Exchange("scatter", pending) if pending else None
        dx, grads[l], got = _layer_bwd(dx, params[l], saved[l], ride_prev, scatter_of)
        if ride_prev:
            apply(l + 1, names[:2], got["prev"])
        apply(l, names[2:], [*got["gate"], *got["up"], *got["down"]])
        pending = [_unpad_w_in(grads[l]["w_in"]).reshape(N_DEV, cin, d).astype(MXU_DTYPE),
                   grads[l]["w_out"].reshape(N_DEV, d // N_DEV, d).astype(MXU_DTYPE)]
    apply(0, names[:2], _Exchange("scatter", pending).run("scatter_last"))
    grad_x = dx.reshape(x.shape)
    big_out = {n: {k: jnp.stack(v_, axis=1 if n == "w_in" else 0) for k, v_ in o.items()} for n, o in big_out.items()}
    for n in ("ffn_w_gate", "ffn_w_up"):
        big_out[n] = {k: tr(a) for k, a in big_out[n].items()}
    big_out["w_in"] = {k: a.transpose(1, 2, 0) for k, a in big_out["w_in"].items()}

    stack = lambda key: jnp.stack([grads[l][key] for l in range(depth)])
    small_parts = [stack("norm1_g").reshape(depth, d), stack("norm2_g").reshape(depth, d), d_final.reshape(d),
                   stack("dn_norm_g").reshape(depth, HEAD_DIM), stack("sc_norm_g").reshape(depth, SC_WIDTH),
                   stack("alog").reshape(depth, LANES), stack("dt").reshape(depth, LANES),
                   stack("dn_conv_w"), stack("sc_conv_w"), loss_part]
    small_pack, small_offs = _pack_rows(small_parts)
    (small_all,) = _all_gather([small_pack], "gather_small")
    total = _sum_rows(small_all, "sum_small")
    (g_n1, g_n2, g_fin, g_dnn, g_scn, g_alog, g_dt, g_dnc, g_scc, loss_row) = _unpack_rows(total, small_offs)
    loss = loss_row[0, 0]
    g_alog = g_alog[:, DN_HEADS:2 * DN_HEADS]
    g_dt = g_dt[:, DN_HEADS:2 * DN_HEADS]
    dnc_w = dn_conv_w.shape[2]
    scc_w = sc_conv_w.shape[2]
    g_dnc = lax.dynamic_slice_in_dim(g_dnc, me * dnc_w, dnc_w, axis=2)
    g_scc = lax.dynamic_slice_in_dim(g_scc, me * scc_w, scc_w, axis=2)
    sm_g = [g_n1, g_dnc, g_alog, g_dt, g_dnn, g_scc, g_scn, g_n2, g_fin]
    sm_w = [norm1_g, dn_conv_w, dn_a_log, dn_dt_bias, dn_norm_g, sc_conv_w, sc_norm_g, norm2_g, final_norm_g]
    sm_m = [m_norm1_g, m_dn_conv_w, m_dn_a_log, m_dn_dt_bias, m_dn_norm_g, m_sc_conv_w, m_sc_norm_g, m_norm2_g, m_final_norm_g]
    sm_v = [v_norm1_g, v_dn_conv_w, v_dn_a_log, v_dn_dt_bias, v_dn_norm_g, v_sc_conv_w, v_sc_norm_g, v_norm2_g, v_final_norm_g]
    pg, offs = _pack_rows(sm_g)
    pw, _ = _pack_rows(sm_w)
    pm, _ = _pack_rows(sm_m)
    pv, _ = _pack_rows(sm_v)
    sg, sd, sm_, sv = _sum_adamw(pg[None], pw, pm, pv, "adamw_small")
    small_out = {k: _unpack_rows(a, offs) for k, a in zip(("g", "d", "m", "v"), (sg, sd, sm_, sv))}

    def outputs(k):
        s_ = small_out[k]
        b = big_out
        return [s_[0], b["w_in"][k], s_[1], s_[2], s_[3], s_[4], s_[5], s_[6], b["w_out"][k], s_[7],
                b["ffn_w_gate"][k], b["ffn_w_up"][k], b["ffn_w_down"][k], s_[8]]

    return (loss, grad_x, *outputs("g"), *outputs("d"), *outputs("m"), *outputs("v"))
```

```python
import functools

import jax
import jax.numpy as jnp
from jax import lax
from jax.experimental import pallas as pl
from jax.experimental.pallas import tpu as pltpu

F32 = jnp.float32
MXU_DTYPE = jnp.bfloat16
MESH = pl.DeviceIdType.MESH

N_DEV = 8
EPS = 1e-6
DN_HEADS = 4
HEAD_DIM = 128
DN_WIDTH = DN_HEADS * HEAD_DIM
SC_WIDTH = 512
SC_GROUPS = 4
DN_CONV = 4
SC_CONV = 3
CHUNK = 64
HALO = 8
LANES = 128

QKV_W = 3 * DN_WIDTH
Z_OFF = QKV_W
SC_OFF = Z_OFF + DN_WIDTH
BA_OFF = SC_OFF + 3 * SC_WIDTH
BA_W = 256
PROJ_W = BA_OFF + BA_W

ADAM_LR = 0.001
ADAM_B1 = 0.9
ADAM_B2 = 0.999
ADAM_EPS = 1e-08
ADAM_WD = 0.01
ADAM_STEP = 10


def _pick(n, cands):
    for c in cands:
        if n % c == 0:
            return c
    return n


def _params(*sem):
    return pltpu.CompilerParams(dimension_semantics=sem)


def _rms_norm(x, g):
    return x * lax.rsqrt(jnp.mean(x * x, axis=-1, keepdims=True) + EPS) * g


def _dot(a, b, dims=(((1,), (0,)), ((), ()))):
    return lax.dot_general(a.astype(MXU_DTYPE), b.astype(MXU_DTYPE), dims, preferred_element_type=F32)


def _split_terms(x, terms):
    out = []
    for _ in range(terms):
        hi = x.astype(MXU_DTYPE)
        out.append(hi)
        x = x - hi.astype(F32)
    return out


def _ein_impl(spec, terms, a, b):
    ta, tb = terms
    if ta == 1 and tb == 1:
        return jnp.einsum(spec, a.astype(MXU_DTYPE), b.astype(MXU_DTYPE), preferred_element_type=F32)
    pa, pb = _split_terms(a, ta), _split_terms(b, tb)
    order = max(ta, tb) - 1
    acc = None
    for deg in range(order, -1, -1):
        for i in range(ta):
            j = deg - i
            if 0 <= j < tb:
                t = jnp.einsum(spec, pa[i], pb[j], preferred_element_type=F32)
                acc = t if acc is None else acc + t
    return acc


@functools.partial(jax.custom_vjp, nondiff_argnums=(0, 1))
def _ein(spec, terms, a, b):
    return _ein_impl(spec, terms, a, b)


def _ein_fwd(spec, terms, a, b):
    return _ein_impl(spec, terms, a, b), (a, b)


def _ein_bwd(spec, terms, res, ct):
    a, b = res
    xy, z = spec.split("->")
    x, y = xy.split(",")
    tc = min(max(terms), 2)
    da = _ein_impl(f"{z},{y}->{x}", (tc, terms[1]), ct, b)
    db = _ein_impl(f"{x},{z}->{y}", (terms[0], tc), a, ct)
    return da, db


_ein.defvjp(_ein_fwd, _ein_bwd)

FAST = (1, 1)
PRECISE = (2, 2)
LHS_EXACT = (1, 3)


def _causal_conv(cur, halo, w, k):
    tt = cur.shape[0]
    xp = jnp.concatenate([halo, cur], axis=0)
    y = None
    for j in range(k):
        start = HALO - (k - 1) + j
        term = xp[start:start + tt] * w[j:j + 1]
        y = term if y is None else y + term
    return y


def _matmul(a, b, mode, out_dtype, name, residual=None, ride=None):
    if mode == "nn":
        (m, k), (k2, n) = a.shape, b.shape
    elif mode == "nt":
        (m, k), (n, k2) = a.shape, b.shape
    else:
        (k, m), (k2, n) = a.shape, b.shape
    assert k == k2
    tm = _pick(m, (1024, 1408, 1280, 512, 256, 128))
    tn = _pick(n, (1280, 1408, 1024, 512, 256, 128))
    tk = k if k <= 2816 else _pick(k, (2048, 1024, 768, 512))
    gi, gj, nk = m // tm, n // tn, k // tk
    dims = {"nn": (((1,), (0,)), ((), ())), "nt": (((1,), (1,)), ((), ())), "tn": (((0,), (0,)), ((), ()))}[mode]
    a_spec = {"nn": pl.BlockSpec((tm, tk), lambda i, j, q: (i, q)),
              "nt": pl.BlockSpec((tm, tk), lambda i, j, q: (i, q)),
              "tn": pl.BlockSpec((tk, tm), lambda i, j, q: (q, i))}[mode]
    b_spec = {"nn": pl.BlockSpec((tk, tn), lambda i, j, q: (q, j)),
              "nt": pl.BlockSpec((tn, tk), lambda i, j, q: (j, q)),
              "tn": pl.BlockSpec((tk, tn), lambda i, j, q: (q, j))}[mode]
    o_spec = pl.BlockSpec((tm, tn), lambda i, j, q: (i, j))
    has_res = residual is not None
    n_in = 3 if has_res else 2
    nr = ride.na if ride else 0

    def body(*refs):
        a_ref, b_ref = refs[0], refs[1]
        r_ref = refs[2] if has_res else None
        ride_in = refs[n_in:n_in + nr]
        o_ref = refs[n_in + nr]
        ride_out = refs[n_in + nr + 1:n_in + 2 * nr + 1]
        acc = refs[n_in + 2 * nr + 1]
        ride_sems = refs[n_in + 2 * nr + 2:]
        i, j, q = pl.program_id(0), pl.program_id(1), pl.program_id(2)
        if ride:
            @pl.when((i == 0) & (j == 0) & (q == 0))
            def _():
                ride.start(ride_in, ride_out, ride_sems)

        @pl.when(q == 0)
        def _():
            acc[...] = jnp.zeros_like(acc)

        acc[...] += _dot(a_ref[...], b_ref[...], dims)

        @pl.when(q == nk - 1)
        def _():
            r = acc[...]
            if has_res:
                r = r + r_ref[...]
            o_ref[...] = r.astype(o_ref.dtype)

        if ride:
            @pl.when((i == gi - 1) & (j == gj - 1) & (q == nk - 1))
            def _():
                ride.finish(ride_in, ride_out, ride_sems)

    hbm = pl.BlockSpec(memory_space=pltpu.HBM)
    in_specs = [a_spec, b_spec] + ([o_spec] if has_res else []) + [hbm] * nr
    args = (a, b) + ((residual,) if has_res else ()) + (tuple(ride.arrays) if ride else ())
    res = pl.pallas_call(
        body, name=name, grid=(gi, gj, nk), in_specs=in_specs, out_specs=[o_spec] + [hbm] * nr,
        out_shape=[jax.ShapeDtypeStruct((m, n), out_dtype)] + (ride.out_shape if ride else []),
        scratch_shapes=[pltpu.VMEM((tm, tn), F32)] + (ride.scratch if ride else []),
        compiler_params=pltpu.CompilerParams(
            dimension_semantics=("arbitrary",) * 3 if ride else ("parallel", "parallel", "arbitrary"),
            has_side_effects=bool(ride)),
    )(*args)
    return (res[0], res[1:]) if ride else res[0]


def _norm_matmul(x, g, w, name, ride=None):
    t, d = x.shape
    n = w.shape[0]
    tm = _pick(t, (1024, 512, 256, 128))
    tn = _pick(n, (1280, 1408, 1024, 512, 256, 128))
    gi, gj = t // tm, n // tn
    nr = ride.na if ride else 0

    def body(*refs):
        x_ref, g_ref, w_ref = refs[:3]
        ride_in = refs[3:3 + nr]
        h_ref, y_ref = refs[3 + nr], refs[4 + nr]
        ride_out = refs[5 + nr:5 + 2 * nr]
        h_scr = refs[5 + 2 * nr]
        ride_sems = refs[6 + 2 * nr:]
        i, j = pl.program_id(0), pl.program_id(1)
        if ride:
            @pl.when((i == 0) & (j == 0))
            def _():
                ride.start(ride_in, ride_out, ride_sems)

        @pl.when(j == 0)
        def _():
            h = _rms_norm(x_ref[...], g_ref[...]).astype(MXU_DTYPE)
            h_scr[...] = h
            h_ref[...] = h

        y_ref[...] = _dot(h_scr[...], w_ref[...], (((1,), (1,)), ((), ())))

        if ride:
            @pl.when((i == gi - 1) & (j == gj - 1))
            def _():
                ride.finish(ride_in, ride_out, ride_sems)

    hbm = pl.BlockSpec(memory_space=pltpu.HBM)
    res = pl.pallas_call(
        body, name=name, grid=(gi, gj),
        in_specs=[pl.BlockSpec((tm, d), lambda i, j: (i, 0)), pl.BlockSpec((1, d), lambda i, j: (0, 0)),
                  pl.BlockSpec((tn, d), lambda i, j: (j, 0))] + [hbm] * nr,
        out_specs=[pl.BlockSpec((tm, d), lambda i, j: (i, 0)), pl.BlockSpec((tm, tn), lambda i, j: (i, j))] + [hbm] * nr,
        out_shape=[jax.ShapeDtypeStruct((t, d), MXU_DTYPE), jax.ShapeDtypeStruct((t, n), F32)]
        + (ride.out_shape if ride else []),
        scratch_shapes=[pltpu.VMEM((tm, d), MXU_DTYPE)] + (ride.scratch if ride else []),
        compiler_params=pltpu.CompilerParams(
            dimension_semantics=("arbitrary",) * 2 if ride else ("parallel", "arbitrary"),
            has_side_effects=bool(ride)),
    )(x, g, w, *(ride.arrays if ride else ()))
    return (res[0], res[1], res[2:]) if ride else (res[0], res[1])


def _swiglu_math(g, u):
    return jax.nn.silu(g) * u


def _gu_tile(dff):
    return _pick(dff, (1408, 1024, 512, 256, 128))


def _interleave_gu(gate_t, up_t):
    tn = _gu_tile(gate_t.shape[0])
    pieces = []
    for j in range(gate_t.shape[0] // tn):
        pieces += [gate_t[j * tn:(j + 1) * tn], up_t[j * tn:(j + 1) * tn]]
    return jnp.concatenate(pieces, axis=0)


def _split_gu(gu_t):
    dff = gu_t.shape[0] // 2
    tn = _gu_tile(dff)
    tiles = [gu_t[j * tn:(j + 1) * tn] for j in range(2 * dff // tn)]
    return jnp.concatenate(tiles[0::2], axis=0), jnp.concatenate(tiles[1::2], axis=0)


def _ffn_up_swiglu(x, g, w_gu, name, ride=None):
    t, d = x.shape
    dff = w_gu.shape[0] // 2
    tn = _gu_tile(dff)
    tm = _pick(t, (512, 256, 128))
    gj, gi = dff // tn, t // tm
    nr = ride.na if ride else 0

    def body(*refs):
        x_ref, g_ref, w_ref = refs[:3]
        ride_in = refs[3:3 + nr]
        h_ref, gu_ref, act_ref = refs[3 + nr:6 + nr]
        ride_out = refs[6 + nr:6 + 2 * nr]
        ride_sems = refs[6 + 2 * nr:]
        j, i = pl.program_id(0), pl.program_id(1)
        if ride:
            @pl.when((i == 0) & (j == 0))
            def _():
                ride.start(ride_in, ride_out, ride_sems)

        h = _rms_norm(x_ref[...], g_ref[...]).astype(MXU_DTYPE)

        @pl.when(j == 0)
        def _():
            h_ref[...] = h

        y = _dot(h, w_ref[...], (((1,), (1,)), ((), ())))
        gu_ref[...] = y.astype(gu_ref.dtype)
        act_ref[...] = _swiglu_math(y[:, :tn], y[:, tn:]).astype(act_ref.dtype)

        if ride:
            @pl.when((i == gi - 1) & (j == gj - 1))
            def _():
                ride.finish(ride_in, ride_out, ride_sems)

    hbm = pl.BlockSpec(memory_space=pltpu.HBM)
    res = pl.pallas_call(
        body, name=name, grid=(gj, gi),
        in_specs=[pl.BlockSpec((tm, d), lambda j, i: (i, 0)), pl.BlockSpec((1, d), lambda j, i: (0, 0)),
                  pl.BlockSpec((2 * tn, d), lambda j, i: (j, 0))] + [hbm] * nr,
        out_specs=[pl.BlockSpec((tm, d), lambda j, i: (jnp.where(j == 0, i, gi - 1), 0)),
                   pl.BlockSpec((tm, 2 * tn), lambda j, i: (i, j)),
                   pl.BlockSpec((tm, tn), lambda j, i: (i, j))] + [hbm] * nr,
        out_shape=[jax.ShapeDtypeStruct((t, d), MXU_DTYPE), jax.ShapeDtypeStruct((t, 2 * dff), MXU_DTYPE),
                   jax.ShapeDtypeStruct((t, dff), MXU_DTYPE)] + (ride.out_shape if ride else []),
        scratch_shapes=(ride.scratch if ride else []),
        compiler_params=pltpu.CompilerParams(dimension_semantics=("arbitrary", "arbitrary"),
                                             has_side_effects=bool(ride)),
    )(x, g, w_gu, *(ride.arrays if ride else ()))
    return (res[0], res[1], res[2], res[3:]) if ride else tuple(res)


def _ffn_down_dx_swiglu(dx_out, w_down, gu, name):
    t, d = dx_out.shape
    dff = w_down.shape[0]
    tn = _gu_tile(dff)
    tm = _pick(t, (512, 256, 128))

    def body(dx_ref, w_ref, gu_ref, o_ref):
        dact = _dot(dx_ref[...], w_ref[...], (((1,), (1,)), ((), ())))
        gu_v = gu_ref[...].astype(F32)
        _, vjp = jax.vjp(_swiglu_math, gu_v[:, :tn], gu_v[:, tn:])
        dg, du = vjp(dact)
        o_ref[...] = jnp.concatenate([dg, du], axis=1).astype(o_ref.dtype)

    return pl.pallas_call(
        body, name=name, grid=(dff // tn, t // tm),
        in_specs=[pl.BlockSpec((tm, d), lambda j, i: (i, 0)), pl.BlockSpec((tn, d), lambda j, i: (j, 0)),
                  pl.BlockSpec((tm, 2 * tn), lambda j, i: (i, j))],
        out_specs=pl.BlockSpec((tm, 2 * tn), lambda j, i: (i, j)),
        out_shape=jax.ShapeDtypeStruct((t, 2 * dff), MXU_DTYPE),
        compiler_params=_params("parallel", "parallel"),
    )(dx_out, w_down, gu)


def _matmul_norm_bwd(dy, w, x, g, dres, name, ride=None, w_rows_k=False):
    t, k = dy.shape
    d = w.shape[1] if w_rows_k else w.shape[0]
    tm = _pick(t, (1024, 512, 256, 128))
    tk = k if k <= 2816 else _pick(k, (1408, 1280, 1024, 768, 512))
    tr = _pick(tm, (256, 128))
    gi, nk = t // tm, k // tk
    nr = ride.na if ride else 0

    def body(*refs):
        dy_ref, w_ref, x_ref, g_ref, dres_ref = refs[:5]
        ride_in = refs[5:5 + nr]
        dx_ref, dg_ref = refs[5 + nr], refs[6 + nr]
        ride_out = refs[7 + nr:7 + 2 * nr]
        acc = refs[7 + 2 * nr]
        ride_sems = refs[8 + 2 * nr:]
        i, q = pl.program_id(0), pl.program_id(1)

        @pl.when((i == 0) & (q == 0))
        def _():
            dg_ref[...] = jnp.zeros_like(dg_ref)
            if ride:
                ride.start(ride_in, ride_out, ride_sems)

        @pl.when(q == 0)
        def _():
            acc[...] = jnp.zeros_like(acc)

        acc[...] += _dot(dy_ref[...], w_ref[...], (((1,), (0 if w_rows_k else 1,)), ((), ())))

        @pl.when(q == nk - 1)
        def _():
            for r in range(tm // tr):
                rows = slice(r * tr, (r + 1) * tr)
                _, vjp = jax.vjp(_rms_norm, x_ref[rows], g_ref[...])
                dxn, dg = vjp(acc[rows])
                dx_ref[rows] = dres_ref[rows] + dxn
                dg_ref[...] += dg

        if ride:
            @pl.when((i == gi - 1) & (q == nk - 1))
            def _():
                ride.finish(ride_in, ride_out, ride_sems)

    hbm = pl.BlockSpec(memory_space=pltpu.HBM)
    row = pl.BlockSpec((tm, d), lambda i, q: (i, 0))
    res = pl.pallas_call(
        body, name=name, grid=(gi, nk),
        in_specs=[pl.BlockSpec((tm, tk), lambda i, q: (i, q)),
                  pl.BlockSpec((tk, d), lambda i, q: (q, 0)) if w_rows_k else pl.BlockSpec((d, tk), lambda i, q: (0, q)),
                  row,
                  pl.BlockSpec((1, d), lambda i, q: (0, 0)), row] + [hbm] * nr,
        out_specs=[row, pl.BlockSpec((1, d), lambda i, q: (0, 0))] + [hbm] * nr,
        out_shape=[jax.ShapeDtypeStruct((t, d), F32), jax.ShapeDtypeStruct((1, d), F32)]
        + (ride.out_shape if ride else []),
        scratch_shapes=[pltpu.VMEM((tm, d), F32)] + (ride.scratch if ride else []),
        compiler_params=pltpu.CompilerParams(dimension_semantics=("arbitrary", "arbitrary"),
                                             has_side_effects=bool(ride)),
    )(dy, w, x, g, dres, *(ride.arrays if ride else ()))
    return (res[0], res[1], res[2:]) if ride else (res[0], res[1])


def _tok_call(name, fn, t, tt, tok_in, const_in, tok_out, acc_out, carry=()):
    nblk = t // tt
    hb = tt // HALO
    blk = (lambda i: nblk - 1 - i) if carry else (lambda i: i)
    in_specs, args = [], []
    for arr, w, cb, mode in tok_in:
        if mode == "cur":
            spec = pl.BlockSpec((tt, w), lambda i, cb=cb: (blk(i), cb))
        elif mode == "prev":
            spec = pl.BlockSpec((HALO, w), lambda i, cb=cb: (jnp.maximum(blk(i) * hb - 1, 0), cb))
        else:
            spec = pl.BlockSpec((HALO, w), lambda i, cb=cb: (jnp.minimum(blk(i) + 1, nblk - 1), cb))
        in_specs.append(spec)
        args.append(arr)
    for arr in const_in:
        in_specs.append(pl.BlockSpec(arr.shape, lambda i: (0, 0)))
        args.append(arr)
    out_specs, out_shape = [], []
    for rows, w, dt in tok_out:
        out_specs.append(pl.BlockSpec((rows, w), lambda i: (blk(i), 0)))
        out_shape.append(jax.ShapeDtypeStruct((nblk * rows, w), dt))
    for shp, dt in acc_out:
        out_specs.append(pl.BlockSpec(shp, lambda i: (0, 0)))
        out_shape.append(jax.ShapeDtypeStruct(shp, dt))
    n_tok, n_const, n_out, n_acc = len(tok_in), len(const_in), len(tok_out), len(acc_out)

    def body(*refs):
        i = pl.program_id(0)
        b = blk(i)
        tok_vals = []
        for (_, _, _, mode), r in zip(tok_in, refs[:n_tok]):
            v = r[...]
            if mode == "prev":
                v = jnp.where(b > 0, v, jnp.zeros_like(v))
            elif mode == "next8":
                v = jnp.where(b < nblk - 1, v, jnp.zeros_like(v))
            tok_vals.append(v)
        const_vals = [r[...] for r in refs[n_tok:n_tok + n_const]]
        o_refs = refs[n_tok + n_const:n_tok + n_const + n_out]
        a_refs = refs[n_tok + n_const + n_out:n_tok + n_const + n_out + n_acc]
        c_refs = refs[n_tok + n_const + n_out + n_acc:]
        if a_refs or c_refs:
            @pl.when(i == 0)
            def _():
                for r in (*a_refs, *c_refs):
                    r[...] = jnp.zeros_like(r)

        if c_refs:
            outs, accs, carried = fn(tok_vals, const_vals, [r[...] for r in c_refs])
            for r, v in zip(c_refs, carried):
                r[...] = v
        else:
            outs, accs = fn(tok_vals, const_vals)
        for r, v in zip(o_refs, outs):
            r[...] = v.astype(r.dtype)
        for r, v in zip(a_refs, accs):
            r[...] += v.astype(r.dtype)

    res = pl.pallas_call(
        body, name=name, grid=(nblk,), in_specs=in_specs, out_specs=out_specs, out_shape=out_shape,
        scratch_shapes=[pltpu.VMEM(shp, dt) for shp, dt in carry],
        compiler_params=_params("arbitrary" if acc_out or carry else "parallel"),
    )(*args)
    return res


def _dn_pre_math(cur, halo, ba, cw, alog, dtb):
    tt = cur.shape[0]
    a = jax.nn.silu(_causal_conv(cur, halo, cw, DN_CONV))
    pieces = []
    for p in range(2 * DN_HEADS):
        xh = a[:, p * HEAD_DIM:(p + 1) * HEAD_DIM]
        xh = xh * lax.rsqrt(jnp.sum(xh * xh, axis=-1, keepdims=True) + EPS)
        if p < DN_HEADS:
            xh = xh * (HEAD_DIM ** -0.5)
        pieces.append(xh)
    pieces.append(a[:, 2 * DN_WIDTH:])
    qkvn = jnp.concatenate(pieces, axis=1)
    lane = lax.broadcasted_iota(jnp.int32, ba.shape, 1)
    raw = jnp.where(lane < DN_HEADS, jax.nn.sigmoid(ba), -jnp.exp(alog) * jax.nn.softplus(ba + dtb))
    r = lax.broadcasted_iota(jnp.int32, (tt, tt), 0)
    c = lax.broadcasted_iota(jnp.int32, (tt, tt), 1)
    tri = jnp.where((r // CHUNK == c // CHUNK) & (c <= r), 1.0, 0.0).astype(F32)
    cums = _ein("ij,jk->ik", LHS_EXACT, tri, raw)
    bg = jnp.where(lane < DN_HEADS, raw, cums)
    return qkvn, bg


def _mix_math(o, z, gb, gc, gc_halo, hv, hv_halo, dng, scg, scw):
    outs = []
    for h in range(DN_HEADS):
        sl = slice(h * HEAD_DIM, (h + 1) * HEAD_DIM)
        oh = o[:, sl]
        outs.append(oh * lax.rsqrt(jnp.mean(oh * oh, axis=-1, keepdims=True) + EPS) * dng * jax.nn.silu(z[:, sl]))
    y = gb * _causal_conv(gc * hv, gc_halo * hv_halo, scw, SC_CONV)
    gw = SC_WIDTH // SC_GROUPS
    for g in range(SC_GROUPS):
        sl = slice(g * gw, (g + 1) * gw)
        yg = y[:, sl]
        outs.append(yg * lax.rsqrt(jnp.mean(yg * yg, axis=-1, keepdims=True) + EPS) * scg[:, sl])
    return jnp.concatenate(outs, axis=1)


def _tri_inverse(a):
    c = a.shape[-1]
    r = lax.broadcasted_iota(jnp.int32, (c, c), 0)
    q = lax.broadcasted_iota(jnp.int32, (c, c), 1)
    eye = jnp.where(r == q, 1.0, 0.0).astype(F32)[None]
    blk = (r // 16 == q // 16)[None]
    d = jnp.where(blk, a, 0.0)
    o = a - d
    mm = functools.partial(_ein, "bij,bjk->bik", PRECISE)
    p = eye - d
    n = mm(d, d)
    for _ in range(2):
        both = mm(jnp.concatenate([n, p], axis=1), n)
        n = both[:, :c]
        p = p + both[:, c:]
    p = p + mm(p, n)
    e = mm(p, o)
    e2 = mm(e, e)
    left = eye - e + e2 - mm(e, e2)
    return mm(left, p)


@jax.custom_vjp
def _inverse_known(a, tinv):
    return tinv


def _inverse_known_fwd(a, tinv):
    return tinv, tinv


def _inverse_known_bwd(tinv, ct):
    left = _ein("bji,bjk->bik", PRECISE, tinv, ct)
    return -_ein("bik,bjk->bij", PRECISE, left, tinv), jnp.zeros_like(tinv)


_inverse_known.defvjp(_inverse_known_fwd, _inverse_known_bwd)


def _delta_intra_math(q, k, v, bg, head, tinv_known=None):
    n = q.shape[0]
    nb = n // CHUNK
    lane = lax.broadcasted_iota(jnp.int32, bg.shape, 1)
    beta = jnp.sum(jnp.where(lane == head, bg, 0.0), axis=1, keepdims=True).reshape(nb, CHUNK, 1)
    gc = jnp.sum(jnp.where(lane == head + DN_HEADS, bg, 0.0), axis=1, keepdims=True).reshape(nb, CHUNK, 1)
    q3, k3, v3 = (a.reshape(nb, CHUNK, HEAD_DIM) for a in (q, k, v))
    r = lax.broadcasted_iota(jnp.int32, (CHUNK, CHUNK), 0)
    c = lax.broadcasted_iota(jnp.int32, (CHUNK, CHUNK), 1)
    eye = jnp.where(r == c, 1.0, 0.0).astype(F32)[None]
    gcr = _ein("bik,bkj->bij", LHS_EXACT, jnp.ones((nb, CHUNK, CHUNK), F32), gc * eye)
    decay = jnp.exp(jnp.where((r >= c)[None], gc - gcr, -1e30))
    kb = k3 * beta
    vb = v3 * beta
    egc = jnp.exp(gc)
    on_k = _ein("bcd,bmd->bcm", FAST, jnp.concatenate([kb, q3], axis=1), k3)
    a = jnp.where((r > c)[None], on_k[:, :CHUNK] * decay, 0.0)
    tinv = _tri_inverse(a) if tinv_known is None else _inverse_known(a, tinv_known)
    uw = _ein("bcm,bmd->bcd", PRECISE, tinv, jnp.concatenate([vb, kb * egc], axis=2))
    u, w = uw[:, :, :HEAD_DIM], uw[:, :, HEAD_DIM:]
    qk = on_k[:, CHUNK:] * decay
    row = lax.broadcasted_iota(jnp.int32, (nb, CHUNK, 1), 1)
    glast = jnp.sum(jnp.where(row == CHUNK - 1, gc, 0.0), axis=1, keepdims=True)
    qd = q3 * egc
    kd = k3 * jnp.exp(glast - gc)
    glb = jnp.broadcast_to(jnp.exp(glast), (nb, HALO, LANES))
    flat = lambda x: x.reshape(n, HEAD_DIM)
    return flat(u), flat(w), flat(qd), flat(kd), qk, glb, tinv


def _delta_step_math(u, w, qd, kd, qk, gl, s):
    c = u.shape[0]
    on_s = _ein("ck,kv->cv", FAST, jnp.concatenate([w, qd], axis=0), s)
    vnew = u - on_s[:c]
    on_v = _ein("cm,mv->cv", FAST, jnp.concatenate([qk, kd.T], axis=0), vnew)
    o = on_s[c:] + on_v[:c]
    s2 = s * gl + on_v[c:]
    return o, s2


def _delta_intra(qkvn, bg, nb):
    t = qkvn.shape[0]
    n = t // CHUNK
    rows = nb * CHUNK

    def body(q_ref, k_ref, v_ref, bg_ref, u_ref, w_ref, qd_ref, kd_ref, qk_ref, gl_ref, ti_ref):
        outs = _delta_intra_math(q_ref[...], k_ref[...], v_ref[...], bg_ref[...], pl.program_id(1))
        for r, v in zip((u_ref, w_ref, qd_ref, kd_ref, qk_ref, gl_ref, ti_ref), outs):
            r[...] = v

    col = lambda off: pl.BlockSpec((rows, HEAD_DIM), lambda b, h, off=off: (b, off + h))
    tok = jax.ShapeDtypeStruct((t, DN_WIDTH), F32)
    return pl.pallas_call(
        body, name="delta_intra", grid=(n // nb, DN_HEADS),
        in_specs=[col(0), col(DN_HEADS), col(2 * DN_HEADS), pl.BlockSpec((rows, LANES), lambda b, h: (b, 0))],
        out_specs=[col(0)] * 4 + [pl.BlockSpec((nb, None, CHUNK, CHUNK), lambda b, h: (b, h, 0, 0)),
                                  pl.BlockSpec((nb, None, HALO, LANES), lambda b, h: (b, h, 0, 0)),
                                  pl.BlockSpec((nb, None, CHUNK, CHUNK), lambda b, h: (b, h, 0, 0))],
        out_shape=[tok] * 4 + [jax.ShapeDtypeStruct((n, DN_HEADS, CHUNK, CHUNK), F32),
                               jax.ShapeDtypeStruct((n, DN_HEADS, HALO, LANES), F32),
                               jax.ShapeDtypeStruct((n, DN_HEADS, CHUNK, CHUNK), F32)],
        compiler_params=_params("parallel", "arbitrary"),
    )(qkvn, qkvn, qkvn, bg)


def _delta_intra_bwd(qkvn, bg, tinv, cts, nb):
    t = qkvn.shape[0]
    n = t // CHUNK
    rows = nb * CHUNK

    def body(q_ref, k_ref, v_ref, bg_ref, ti_ref, du, dw, dqd, dkd, dqk, dgl, dq_ref, dk_ref, dv_ref, dbg_ref):
        h = pl.program_id(1)
        ti = ti_ref[...]
        _, vjp = jax.vjp(lambda q, k, v, b: _delta_intra_math(q, k, v, b, h, ti)[:6],
                         q_ref[...], k_ref[...], v_ref[...], bg_ref[...])
        dq, dk, dv, dbg = vjp((du[...], dw[...], dqd[...], dkd[...], dqk[...], dgl[...]))
        dq_ref[...] = dq
        dk_ref[...] = dk
        dv_ref[...] = dv

        @pl.when(h == 0)
        def _():
            dbg_ref[...] = jnp.zeros_like(dbg_ref)

        dbg_ref[...] += dbg

    col = lambda off: pl.BlockSpec((rows, HEAD_DIM), lambda b, h, off=off: (b, off + h))
    bgs = pl.BlockSpec((rows, LANES), lambda b, h: (b, 0))
    qks = pl.BlockSpec((nb, None, CHUNK, CHUNK), lambda b, h: (b, h, 0, 0))
    gls = pl.BlockSpec((nb, None, HALO, LANES), lambda b, h: (b, h, 0, 0))
    tok = jax.ShapeDtypeStruct((t, DN_WIDTH), F32)
    return pl.pallas_call(
        body, name="delta_intra_bwd", grid=(n // nb, DN_HEADS),
        in_specs=[col(0), col(DN_HEADS), col(2 * DN_HEADS), bgs, qks, col(0), col(0), col(0), col(0), qks, gls],
        out_specs=[col(0), col(0), col(0), bgs],
        out_shape=[tok, tok, tok, jax.ShapeDtypeStruct((t, LANES), F32)],
        compiler_params=_params("parallel", "arbitrary"),
    )(qkvn, qkvn, qkvn, bg, tinv, *cts)


def _delta_scan(u, w, qd, kd, qk, glb, cb):
    t = u.shape[0]
    n = t // CHUNK
    rows = cb * CHUNK

    def body(u_ref, w_ref, qd_ref, kd_ref, qk_ref, gl_ref, o_ref, s_ref, s_scr):
        @pl.when(pl.program_id(0) == 0)
        def _():
            s_scr[...] = jnp.zeros_like(s_scr)

        def chunk(c, carry):
            r0 = pl.multiple_of(c * CHUNK, CHUNK)
            for h in range(DN_HEADS):
                sl = (pl.ds(r0, CHUNK), slice(h * HEAD_DIM, (h + 1) * HEAD_DIM))
                s = s_scr[h]
                s_ref[c, h] = s
                o, s2 = _delta_step_math(u_ref[sl], w_ref[sl], qd_ref[sl], kd_ref[sl], qk_ref[c, h],
                                         gl_ref[c, h][0:1, :], s)
                o_ref[sl] = o
                s_scr[h] = s2
            return carry

        lax.fori_loop(0, cb, chunk, 0)

    tok = pl.BlockSpec((rows, DN_WIDTH), lambda i: (i, 0))
    return pl.pallas_call(
        body, name="delta_scan", grid=(n // cb,),
        in_specs=[tok] * 4 + [pl.BlockSpec((cb, DN_HEADS, CHUNK, CHUNK), lambda i: (i, 0, 0, 0)),
                              pl.BlockSpec((cb, DN_HEADS, HALO, LANES), lambda i: (i, 0, 0, 0))],
        out_specs=[tok, pl.BlockSpec((cb, DN_HEADS, HEAD_DIM, HEAD_DIM), lambda i: (i, 0, 0, 0))],
        out_shape=[jax.ShapeDtypeStruct((t, DN_WIDTH), F32),
                   jax.ShapeDtypeStruct((n, DN_HEADS, HEAD_DIM, HEAD_DIM), F32)],
        scratch_shapes=[pltpu.VMEM((DN_HEADS, HEAD_DIM, HEAD_DIM), F32)],
        compiler_params=_params("arbitrary"),
    )(u, w, qd, kd, qk, glb)


def _delta_scan_bwd(u, w, qd, kd, qk, glb, s_all, do, cb):
    t = u.shape[0]
    n = t // CHUNK
    nblk = n // cb
    rows = cb * CHUNK

    def body(u_ref, w_ref, qd_ref, kd_ref, qk_ref, gl_ref, s_ref, do_ref,
             du_ref, dw_ref, dqd_ref, dkd_ref, dqk_ref, dgl_ref, ds_scr):
        @pl.when(pl.program_id(0) == 0)
        def _():
            ds_scr[...] = jnp.zeros_like(ds_scr)

        def chunk(step, carry):
            c = cb - 1 - step
            r0 = pl.multiple_of(c * CHUNK, CHUNK)
            for h in range(DN_HEADS):
                sl = (pl.ds(r0, CHUNK), slice(h * HEAD_DIM, (h + 1) * HEAD_DIM))
                gl_tile = gl_ref[c, h]
                prim = (u_ref[sl], w_ref[sl], qd_ref[sl], kd_ref[sl], qk_ref[c, h], gl_tile, s_ref[c, h])
                _, vjp = jax.vjp(lambda a, b, cc, d, e, g, s: _delta_step_math(a, b, cc, d, e, g[0:1, :], s), *prim)
                du, dw, dqd, dkd, dqk, dgl, ds = vjp((do_ref[sl], ds_scr[h]))
                du_ref[sl] = du
                dw_ref[sl] = dw
                dqd_ref[sl] = dqd
                dkd_ref[sl] = dkd
                dqk_ref[c, h] = dqk
                dgl_ref[c, h] = dgl
                ds_scr[h] = ds
            return carry

        lax.fori_loop(0, cb, chunk, 0)

    rev = lambda i: nblk - 1 - i
    tok = pl.BlockSpec((rows, DN_WIDTH), lambda i: (rev(i), 0))
    qks = pl.BlockSpec((cb, DN_HEADS, CHUNK, CHUNK), lambda i: (rev(i), 0, 0, 0))
    gls = pl.BlockSpec((cb, DN_HEADS, HALO, LANES), lambda i: (rev(i), 0, 0, 0))
    ss = pl.BlockSpec((cb, DN_HEADS, HEAD_DIM, HEAD_DIM), lambda i: (rev(i), 0, 0, 0))
    tshape = jax.ShapeDtypeStruct((t, DN_WIDTH), F32)
    return pl.pallas_call(
        body, name="delta_scan_bwd", grid=(nblk,),
        in_specs=[tok] * 4 + [qks, gls, ss, tok],
        out_specs=[tok] * 4 + [qks, gls],
        out_shape=[tshape] * 4 + [jax.ShapeDtypeStruct(qk.shape, F32), jax.ShapeDtypeStruct(glb.shape, F32)],
        scratch_shapes=[pltpu.VMEM((DN_HEADS, HEAD_DIM, HEAD_DIM), F32)],
        compiler_params=_params("arbitrary"),
    )(u, w, qd, kd, qk, glb, s_all, do)


def _peer(mask):
    x, y, c = lax.axis_index("x"), lax.axis_index("y"), lax.axis_index("c")
    return (x ^ ((mask >> 2) & 1), y ^ ((mask >> 1) & 1), c ^ (mask & 1))


def _my_index():
    return 4 * lax.axis_index("x") + 2 * lax.axis_index("y") + lax.axis_index("c")


class _Exchange:
    CHIP_MASKS = (4, 2, 6)

    def __init__(self, kind, arrays):
        self.kind = kind
        self.arrays = list(arrays)
        self.na = na = len(self.arrays)
        if kind == "gather":
            self.out_shape = [jax.ShapeDtypeStruct((N_DEV,) + a.shape, a.dtype) for a in self.arrays]
        else:
            self.out_shape = [jax.ShapeDtypeStruct(a.shape, a.dtype) for a in self.arrays]
        self.scratch = [pltpu.SemaphoreType.DMA((na, 7)), pltpu.SemaphoreType.DMA((na, 7)),
                        pltpu.SemaphoreType.DMA((na,))]

    def _copies(self, ins, outs, sems):
        send_sems, recv_sems, local_sems = sems
        me = _my_index()
        local, first, passed, arrivals = [], [], [], []
        if self.kind == "gather":
            def rc(a, k, block, to, own=False):
                def make():
                    dst = outs[a].at[block]
                    return pltpu.make_async_remote_copy(src_ref=ins[a] if own else dst, dst_ref=dst,
                                                        send_sem=send_sems.at[a, k], recv_sem=recv_sems.at[a, k],
                                                        device_id=to, device_id_type=MESH)
                return make

            sib = _peer(1)
            for a in range(self.na):
                local.append(lambda a=a: pltpu.make_async_copy(ins[a], outs[a].at[me], local_sems.at[a]))
                first.append(rc(a, 0, me, sib, own=True))
                arrivals.append(rc(a, 0, me ^ 1, _peer(0)))
                for j, m in enumerate(self.CHIP_MASKS):
                    first.append(rc(a, 1 + j, me, _peer(m), own=True))
                    passed.append((rc(a, 1 + j, me ^ m, _peer(0)), rc(a, 4 + j, me ^ m, sib)))
                    arrivals.append(rc(a, 4 + j, me ^ m ^ 1, _peer(0)))
        else:
            for a in range(self.na):
                local.append(lambda a=a: pltpu.make_async_copy(ins[a].at[me], outs[a].at[me], local_sems.at[a]))
                for m in range(1, N_DEV):
                    def make(a=a, m=m):
                        return pltpu.make_async_remote_copy(
                            src_ref=ins[a].at[me ^ m], dst_ref=outs[a].at[me], send_sem=send_sems.at[a, m - 1],
                            recv_sem=recv_sems.at[a, m - 1], device_id=_peer(m), device_id_type=MESH)
                    first.append(make)
                    arrivals.append(make)
        return local, first, passed, arrivals

    def start(self, ins, outs, sems):
        local, first, _, _ = self._copies(ins, outs, sems)
        for make in local + first:
            make().start()

    def finish(self, ins, outs, sems):
        local, first, passed, arrivals = self._copies(ins, outs, sems)
        for landed, onward in passed:
            landed().wait_recv()
            onward().start()
        for make in arrivals:
            make().wait_recv()
        for make in first + [p for _, p in passed]:
            make().wait_send()
        for make in local:
            make().wait()

    def run(self, name):
        na = self.na

        def body(*refs):
            ins, outs, sems = refs[:na], refs[na:2 * na], refs[2 * na:]
            self.start(ins, outs, sems)
            self.finish(ins, outs, sems)

        hbm = pl.BlockSpec(memory_space=pltpu.HBM)
        return pl.pallas_call(
            body, name=name, in_specs=[hbm] * na, out_specs=[hbm] * na, out_shape=self.out_shape,
            scratch_shapes=self.scratch, compiler_params=pltpu.CompilerParams(has_side_effects=True),
        )(*self.arrays)


def _all_gather(shards, name):
    return _Exchange("gather", shards).run(name)


def _adamw_math(w, g, m, v):
    m2 = ADAM_B1 * m + (1.0 - ADAM_B1) * g
    v2 = ADAM_B2 * v + (1.0 - ADAM_B2) * jnp.square(g)
    m_hat = m2 / (1.0 - ADAM_B1 ** ADAM_STEP)
    v_hat = v2 / (1.0 - ADAM_B2 ** ADAM_STEP)
    delta = -ADAM_LR * (m_hat / (jnp.sqrt(v_hat) + ADAM_EPS) + ADAM_WD * w)
    return delta, m2, v2


def _sum_adamw(parts, w, m, v, name):
    r, c = w.shape
    tr = _pick(r, (512, 256, 176, 128, 64, 32, 16, 8))
    np_ = parts.shape[0]

    def body(p_ref, w_ref, m_ref, v_ref, g_ref, d_ref, m2_ref, v2_ref):
        g = p_ref[0].astype(F32)
        for d in range(1, np_):
            g = g + p_ref[d].astype(F32)
        delta, m2, v2 = _adamw_math(w_ref[...], g, m_ref[...], v_ref[...])
        g_ref[...] = g
        d_ref[...] = delta
        m2_ref[...] = m2
        v2_ref[...] = v2

    blk = pl.BlockSpec((tr, c), lambda i: (i, 0))
    shp = jax.ShapeDtypeStruct((r, c), F32)
    return pl.pallas_call(
        body, name=name, grid=(r // tr,),
        in_specs=[pl.BlockSpec((np_, tr, c), lambda i: (0, i, 0)), blk, blk, blk],
        out_specs=[blk] * 4, out_shape=[shp] * 4, compiler_params=_params("parallel"),
    )(parts, w, m, v)


def _sum_rows(parts, name):
    np_, r, c = parts.shape

    def body(p_ref, o_ref):
        g = p_ref[0]
        for d in range(1, np_):
            g = g + p_ref[d]
        o_ref[...] = g

    return pl.pallas_call(body, name=name, out_shape=jax.ShapeDtypeStruct((r, c), F32))(parts)


def _pad_w_in(w):
    d = w.shape[1]
    n_ba = 2 * DN_HEADS
    a = w[:SC_OFF]
    ba = w[SC_OFF:SC_OFF + n_ba]
    sc = w[SC_OFF + n_ba:]
    return jnp.concatenate([a, sc, ba, jnp.zeros((BA_W - n_ba, d), w.dtype)], axis=0)


def _unpad_w_in(wp):
    n_ba = 2 * DN_HEADS
    return jnp.concatenate([wp[:SC_OFF], wp[BA_OFF:BA_OFF + n_ba], wp[SC_OFF:BA_OFF]], axis=0)


def _lane_row(v, off):
    return jnp.pad(v.astype(F32), (off, LANES - off - v.shape[0]))[None]


TT = 256
TT_FWD = 512
NB_INTRA = 32
CB_SCAN = 8


def _layer_fwd(x, p, ride_proj, ride_ffn, late):
    t, d = x.shape
    got_proj = got_ffn = None
    if ride_proj:
        h, proj, got_proj = _norm_matmul(x, p["norm1_g"], p["w_in"], "proj_fwd_gather", ride=ride_proj)
    else:
        h, proj = _norm_matmul(x, p["norm1_g"], p["w_in"], "proj_fwd")
    p = {**p, **late(got_proj)}
    qkvn, bg = _tok_call(
        "dn_pre", lambda tv, cv: (_dn_pre_math(*tv, *cv), ()), t, min(TT_FWD, t),
        [(proj, QKV_W, 0, "cur"), (proj, QKV_W, 0, "prev"), (proj, LANES, BA_OFF // LANES, "cur")],
        [p["dn_conv_w"], p["alog_row"], p["dt_row"]],
        [(min(TT_FWD, t), QKV_W, F32), (min(TT_FWD, t), LANES, F32)], [])
    u, w, qd, kd, qk, glb, tinv = _delta_intra(qkvn, bg, min(NB_INTRA, t // CHUNK))
    o, s_all = _delta_scan(u, w, qd, kd, qk, glb, CB_SCAN)
    cb0 = SC_OFF // SC_WIDTH
    mix_in = [(o, DN_WIDTH, 0, "cur"), (proj, DN_WIDTH, Z_OFF // DN_WIDTH, "cur"),
              (proj, SC_WIDTH, cb0, "cur"), (proj, SC_WIDTH, cb0 + 1, "cur"), (proj, SC_WIDTH, cb0 + 1, "prev"),
              (proj, SC_WIDTH, cb0 + 2, "cur"), (proj, SC_WIDTH, cb0 + 2, "prev")]
    mix_const = [p["dn_norm_g"], p["sc_norm_g"], p["sc_conv_w"]]
    (cat,) = _tok_call("mix_post", lambda tv, cv: ((_mix_math(*tv, *cv),), ()), t, min(TT_FWD, t),
                       mix_in, mix_const, [(min(TT_FWD, t), 2 * DN_WIDTH, MXU_DTYPE)], [])
    x_mid = _matmul(cat, p["w_out"], "nn", F32, "out_proj", residual=x)
    if ride_ffn:
        h2, gu, act, got_ffn = _ffn_up_swiglu(x_mid, p["norm2_g"], p["w_gu"], "ffn_up_gather", ride=ride_ffn)
    else:
        h2, gu, act = _ffn_up_swiglu(x_mid, p["norm2_g"], p["w_gu"], "ffn_up")
    x_out = _matmul(act, p["w_down"], "nn", F32, "ffn_down", residual=x_mid)
    saved = dict(x=x, h=h, proj=proj, qkvn=qkvn, bg=bg, u=u, w=w, qd=qd, kd=kd, qk=qk, glb=glb, tinv=tinv, s_all=s_all, o=o,
                 cat=cat, x_mid=x_mid, h2=h2, gu=gu, act=act, mix_in=mix_in, mix_const=mix_const)
    return x_out, saved, p, got_proj, got_ffn


def _layer_bwd(dx_out, p, s, ride_prev, scatter_of):
    t, d = dx_out.shape
    got = {}
    dgu = _ffn_down_dx_swiglu(dx_out, p["w_down"], s["gu"], "ffn_down_dx")
    d_w_down = _matmul(s["act"], dx_out, "tn", MXU_DTYPE, "ffn_down_dw")
    if ride_prev:
        dx_mid, d_norm2, got["prev"] = _matmul_norm_bwd(dgu, p["w_gu"], s["x_mid"], p["norm2_g"], dx_out,
                                                        "ffn_up_dx_scatter", ride=ride_prev, w_rows_k=True)
    else:
        dx_mid, d_norm2 = _matmul_norm_bwd(dgu, p["w_gu"], s["x_mid"], p["norm2_g"], dx_out, "ffn_up_dx",
                                           w_rows_k=True)
    d_w_gu, got["down"] = _matmul(dgu, s["h2"], "tn", MXU_DTYPE, "ffn_up_dw_scatter", ride=scatter_of(d_w_down))
    d_gate, d_up = _split_gu(d_w_gu)
    dcat = _matmul(dx_mid, p["w_out"], "nt", F32, "out_proj_dx")
    d_w_out = _matmul(s["cat"], dx_mid, "tn", MXU_DTYPE, "out_proj_dw")

    def mix_bwd(tv, cv):
        prim = tuple(tv[:7]) + tuple(cv)
        _, vjp = jax.vjp(_mix_math, *prim)
        do, dz, dgb, dgc, dgch, dhv, dhvh, ddng, dscg, dscw = vjp(tv[7])
        return (do, dz, dgb, dgc, dgch, dhv, dhvh), (ddng, dscg, dscw)

    wide = (TT, DN_WIDTH, F32)
    final = (TT, DN_WIDTH, MXU_DTYPE)
    halo = (HALO, SC_WIDTH, F32)
    do, dz, dgb, dgc, dgc_h, dhv, dhv_h, d_dn_norm, d_sc_norm, d_sc_conv = _tok_call(
        "mix_post_bwd", mix_bwd, t, TT, s["mix_in"] + [(dcat, 2 * DN_WIDTH, 0, "cur")], s["mix_const"],
        [wide, final, final, wide, halo, wide, halo],
        [((1, HEAD_DIM), F32), ((1, SC_WIDTH), F32), ((SC_CONV, SC_WIDTH), F32)])
    cts = _delta_scan_bwd(s["u"], s["w"], s["qd"], s["kd"], s["qk"], s["glb"], s["s_all"], do, CB_SCAN)
    dq, dk, dv, dbg = _delta_intra_bwd(s["qkvn"], s["bg"], s["tinv"], cts, min(NB_INTRA, t // CHUNK))
    proj = s["proj"]

    def with_halo(cur, nxt):
        return cur + jnp.concatenate([jnp.zeros((TT - HALO, cur.shape[1]), F32), nxt], axis=0)

    def dn_pre_bwd(tv, cv, carried):
        cur, hal, ba, dq_, dk_, dv_, dbg_, dz_, dgb_, dgc_, dgc_n, dhv_, dhv_n = tv
        _, vjp = jax.vjp(_dn_pre_math, cur, hal, ba, *cv)
        dcur, dhal, dba, dcw, dal, ddt = vjp((jnp.concatenate([dq_, dk_, dv_], axis=1), dbg_))
        dproj_rows = jnp.concatenate([with_halo(dcur, carried[0]), dz_.astype(F32), dgb_.astype(F32), with_halo(dgc_, dgc_n),
                                      with_halo(dhv_, dhv_n), dba, jnp.zeros((TT, BA_W - LANES), F32)], axis=1)
        return (dproj_rows,), (dcw, dal, ddt), (dhal,)

    dproj, d_dn_conv, d_alog, d_dt = _tok_call(
        "dn_pre_bwd", dn_pre_bwd, t, TT,
        [(proj, QKV_W, 0, "cur"), (proj, QKV_W, 0, "prev"), (proj, LANES, BA_OFF // LANES, "cur"),
         (dq, DN_WIDTH, 0, "cur"), (dk, DN_WIDTH, 0, "cur"), (dv, DN_WIDTH, 0, "cur"), (dbg, LANES, 0, "cur"),
         (dz, DN_WIDTH, 0, "cur"), (dgb, SC_WIDTH, 0, "cur"), (dgc, SC_WIDTH, 0, "cur"), (dgc_h, SC_WIDTH, 0, "next8"),
         (dhv, SC_WIDTH, 0, "cur"), (dhv_h, SC_WIDTH, 0, "next8")],
        [p["dn_conv_w"], p["alog_row"], p["dt_row"]],
        [(TT, PROJ_W, MXU_DTYPE)],
        [((DN_CONV, QKV_W), F32), ((1, LANES), F32), ((1, LANES), F32)],
        carry=[((HALO, QKV_W), F32)])
    dx_in, d_norm1, got["gate"] = _matmul_norm_bwd(dproj, p["w_in"], s["x"], p["norm1_g"], dx_mid,
                                                   "proj_dx_scatter", ride=scatter_of(d_gate), w_rows_k=True)
    d_w_in, got["up"] = _matmul(dproj, s["h"], "tn", MXU_DTYPE, "proj_dw_scatter", ride=scatter_of(d_up))
    grads = dict(w_in=d_w_in, w_out=d_w_out, w_gu=d_w_gu, w_down=d_w_down, norm1_g=d_norm1, norm2_g=d_norm2,
                 dn_norm_g=d_dn_norm, sc_norm_g=d_sc_norm, sc_conv_w=d_sc_conv, dn_conv_w=d_dn_conv,
                 alog=d_alog, dt=d_dt)
    return dx_in, grads, got


def _final_loss(x, g, target):
    t, d = x.shape

    def fn(tv, cv):
        xv, tg = tv

        def loss_fn(xx, gg):
            err = jnp.square(_rms_norm(xx, gg) - tg)
            return 0.5 * jnp.sum(jnp.mean(err, axis=-1))

        loss, vjp = jax.vjp(loss_fn, xv, cv[0])
        dx, dg = vjp(jnp.ones((), F32))
        return (dx,), (jnp.full((1, LANES), loss, F32), dg)

    return _tok_call("final_loss", fn, t, TT, [(x, d, 0, "cur"), (target, d, 0, "cur")], [g],
                     [(TT, d, F32)], [((1, LANES), F32), ((1, d), F32)])


def _pack_rows(arrs):
    rows, offs, r0 = [], [], 0
    for a in arrs:
        n = a.size
        nr = -(-n // LANES)
        flat = jnp.pad(a.reshape(-1).astype(F32), (0, nr * LANES - n))
        rows.append(flat.reshape(nr, LANES))
        offs.append((r0, nr, a.shape))
        r0 += nr
    pad = (-r0) % 8
    if pad:
        rows.append(jnp.zeros((pad, LANES), F32))
    return jnp.concatenate(rows, axis=0), offs


def _unpack_rows(packed, offs):
    out = []
    for r0, nr, shp in offs:
        n = 1
        for s_ in shp:
            n *= s_
        out.append(packed[r0:r0 + nr].reshape(-1)[:n].reshape(shp))
    return out


def kernel(x, norm1_g, w_in, dn_conv_w, dn_a_log, dn_dt_bias, dn_norm_g, sc_conv_w, sc_norm_g, w_out, norm2_g, ffn_w_gate, ffn_w_up, ffn_w_down, final_norm_g, loss_target, m_norm1_g, m_w_in, m_dn_conv_w, m_dn_a_log, m_dn_dt_bias, m_dn_norm_g, m_sc_conv_w, m_sc_norm_g, m_w_out, m_norm2_g, m_ffn_w_gate, m_ffn_w_up, m_ffn_w_down, m_final_norm_g, v_norm1_g, v_w_in, v_dn_conv_w, v_dn_a_log, v_dn_dt_bias, v_dn_norm_g, v_sc_conv_w, v_sc_norm_g, v_w_out, v_norm2_g, v_ffn_w_gate, v_ffn_w_up, v_ffn_w_down, v_final_norm_g):
    depth, d, cin = w_in.shape
    t = x.shape[1]
    dff_s = ffn_w_gate.shape[2]
    tr = lambda a: a.transpose(0, 2, 1)
    gate_t, up_t = tr(ffn_w_gate), tr(ffn_w_up)
    by_layer = lambda a: [a[l].T for l in range(depth)]
    win_t = by_layer(w_in)
    me = _my_index()
    x2 = x.reshape(t, d)
    tgt = loss_target.reshape(t, d)

    def shards(l):
        return [a[l].astype(MXU_DTYPE) for a in (win_t, w_out, gate_t, up_t, ffn_w_down)]

    conv_pack, conv_offs = _pack_rows([dn_conv_w, sc_conv_w])
    conv_all, *first = _all_gather([conv_pack] + shards(0)[:2], "gather_first")
    dn_parts, sc_parts = zip(*[_unpack_rows(conv_all[j], conv_offs) for j in range(N_DEV)])
    dn_conv_full = jnp.concatenate(dn_parts, axis=2)
    sc_conv_full = jnp.concatenate(sc_parts, axis=2)

    def mixer_params(l, g_in, g_out):
        return dict(
            w_in=_pad_w_in(g_in.reshape(N_DEV * cin, d)), w_out=g_out.reshape(d, d),
            norm1_g=norm1_g[l][None], norm2_g=norm2_g[l][None], dn_norm_g=dn_norm_g[l][None],
            sc_norm_g=sc_norm_g[l][None], dn_conv_w=dn_conv_full[l], sc_conv_w=sc_conv_full[l],
            alog_row=_lane_row(dn_a_log[l], DN_HEADS), dt_row=_lane_row(dn_dt_bias[l], DN_HEADS))

    def ffn_params(g_gate, g_up, g_down):
        dff = N_DEV * dff_s
        return dict(w_gu=_interleave_gu(g_gate.reshape(dff, d), g_up.reshape(dff, d)), w_down=g_down.reshape(dff, d))

    nxt = mixer_params(0, *first)
    nxt_ffn = None
    params, saved = [], []
    xc = x2
    for l in range(depth):
        last = l + 1 == depth
        own_ffn = shards(l)[2:] if nxt_ffn is None else []
        ahead = [] if last else shards(l + 1)
        ride_proj = _Exchange("gather", own_ffn + ahead[:2]) if own_ffn or ahead else None
        ride_ffn = None if last else _Exchange("gather", ahead[2:])

        def late(got, own_ffn=own_ffn, nxt_ffn=nxt_ffn):
            return ffn_params(*got[:3]) if own_ffn else nxt_ffn

        xc, s, p_l, got_proj, got_ffn = _layer_fwd(xc, nxt, ride_proj, ride_ffn, late)
        params.append(p_l)
        saved.append(s)
        if not last:
            nxt = mixer_params(l + 1, *got_proj[len(own_ffn):])
            nxt_ffn = ffn_params(*got_ffn)
    dx, loss_part, d_final = _final_loss(xc, final_norm_g[None], tgt)

    names = ("w_in", "w_out", "ffn_w_gate", "ffn_w_up", "ffn_w_down")
    big_out = {n: {k: [None] * depth for k in ("g", "d", "m", "v")} for n in names}
    w_loc = dict(w_in=win_t, w_out=w_out, ffn_w_gate=gate_t, ffn_w_up=up_t, ffn_w_down=ffn_w_down)
    m_loc = dict(w_in=by_layer(m_w_in), w_out=m_w_out, ffn_w_gate=tr(m_ffn_w_gate), ffn_w_up=tr(m_ffn_w_up), ffn_w_down=m_ffn_w_down)
    v_loc = dict(w_in=by_layer(v_w_in), w_out=v_w_out, ffn_w_gate=tr(v_ffn_w_gate), ffn_w_up=tr(v_ffn_w_up), ffn_w_down=v_ffn_w_down)


    def scatter_of(grad):
        return _Exchange("scatter", [grad.reshape(N_DEV, dff_s, d).astype(MXU_DTYPE)])

    def apply(l, which, recv):
        for n, r in zip(which, recv):
            res = _sum_adamw(r, w_loc[n][l], m_loc[n][l], v_loc[n][l], "adamw_" + n)
            for k, a in zip(("g", "d", "m", "v"), res):
                big_out[n][k][l] = a

    grads = [None] * depth
    pending = None
    for l in reversed(range(depth)):
        ride_prev = _Exchange("scatter", pending) if pending else None
        dx, grads[l], got = _layer_bwd(dx, params[l], saved[l], ride_prev, scatter_of)
        if ride_prev:
            apply(l + 1, names[:2], got["prev"])
        apply(l, names[2:], [*got["gate"], *got["up"], *got["down"]])
        pending = [_unpad_w_in(grads[l]["w_in"]).reshape(N_DEV, cin, d).astype(MXU_DTYPE),
                   grads[l]["w_out"].reshape(N_DEV, d // N_DEV, d).astype(MXU_DTYPE)]
    apply(0, names[:2], _Exchange("scatter", pending).run("scatter_last"))
    grad_x = dx.reshape(x.shape)
    big_out = {n: {k: jnp.stack(v_, axis=1 if n == "w_in" else 0) for k, v_ in o.items()} for n, o in big_out.items()}
    for n in ("ffn_w_gate", "ffn_w_up"):
        big_out[n] = {k: tr(a) for k, a in big_out[n].items()}
    big_out["w_in"] = {k: a.transpose(1, 2, 0) for k, a in big_out["w_in"].items()}

    stack = lambda key: jnp.stack([grads[l][key] for l in range(depth)])
    small_parts = [stack("norm1_g").reshape(depth, d), stack("norm2_g").reshape(depth, d), d_final.reshape(d),
                   stack("dn_norm_g").reshape(depth, HEAD_DIM), stack("sc_norm_g").reshape(depth, SC_WIDTH),
                   stack("alog").reshape(depth, LANES), stack("dt").reshape(depth, LANES),
                   stack("dn_conv_w"), stack("sc_conv_w"), loss_part]
    small_pack, small_offs = _pack_rows(small_parts)
    (small_all,) = _all_gather([small_pack], "gather_small")
    total = _sum_rows(small_all, "sum_small")
    (g_n1, g_n2, g_fin, g_dnn, g_scn, g_alog, g_dt, g_dnc, g_scc, loss_row) = _unpack_rows(total, small_offs)
    loss = loss_row[0, 0]
    g_alog = g_alog[:, DN_HEADS:2 * DN_HEADS]
    g_dt = g_dt[:, DN_HEADS:2 * DN_HEADS]
    dnc_w = dn_conv_w.shape[2]
    scc_w = sc_conv_w.shape[2]
    g_dnc = lax.dynamic_slice_in_dim(g_dnc, me * dnc_w, dnc_w, axis=2)
    g_scc = lax.dynamic_slice_in_dim(g_scc, me * scc_w, scc_w, axis=2)
    sm_g = [g_n1, g_dnc, g_alog, g_dt, g_dnn, g_scc, g_scn, g_n2, g_fin]
    sm_w = [norm1_g, dn_conv_w, dn_a_log, dn_dt_bias, dn_norm_g, sc_conv_w, sc_norm_g, norm2_g, final_norm_g]
    sm_m = [m_norm1_g, m_dn_conv_w, m_dn_a_log, m_dn_dt_bias, m_dn_norm_g, m_sc_conv_w, m_sc_norm_g, m_norm2_g, m_final_norm_g]
    sm_v = [v_norm1_g, v_dn_conv_w, v_dn_a_log, v_dn_dt_bias, v_dn_norm_g, v_sc_conv_w, v_sc_norm_g, v_norm2_g, v_final_norm_g]
    pg, offs = _pack_rows(sm_g)
    pw, _ = _pack_rows(sm_w)
    pm, _ = _pack_rows(sm_m)
    pv, _ = _pack_rows(sm_v)
    sg, sd, sm_, sv = _sum_adamw(pg[None], pw, pm, pv, "adamw_small")
    small_out = {k: _unpack_rows(a, offs) for k, a in zip(("g", "d", "m", "v"), (sg, sd, sm_, sv))}

    def outputs(k):
        s_ = small_out[k]
        b = big_out
        return [s_[0], b["w_in"][k], s_[1], s_[2], s_[3], s_[4], s_[5], s_[6], b["w_out"][k], s_[7],
                b["ffn_w_gate"][k], b["ffn_w_up"][k], b["ffn_w_down"][k], s_[8]]

    return (loss, grad_x, *outputs("g"), *outputs("d"), *outputs("m"), *outputs("v"))
```
